```python
import jax, jax.numpy as jnp
from jax import lax
import numpy as np

D_MODEL = 1024
BATCH = 32
SEQ = 256
DEPTH = 2
DEC_BATCH = 8
DEC_SEQ = 1024
PAST_LEN = 512

GRID_W = 64
N_MIXERS = 2
N_HEADS = 16
N_KV_HEADS = 4
HEAD_DIM = D_MODEL // N_HEADS
Q_PER_KV = N_HEADS // N_KV_HEADS
ROPE_THETA = 10000.0
Q_BLOCK = 128
POOL_WINDOWS = (2, 4, 8, 16)
N_POOL_GROUPS = len(POOL_WINDOWS)
POOL_GROUP_DIM = D_MODEL // N_POOL_GROUPS
N_EXPERT_GROUPS = 4
EXPERTS_PER_GROUP = 8
N_EXPERTS = N_EXPERT_GROUPS * EXPERTS_PER_GROUP
TOP_K_IN_GROUP = 2
D_EXPERT = D_MODEL // 4
N_ATTN_LAYERS = (DEPTH + N_MIXERS - 1) // N_MIXERS
N_POOL_LAYERS = DEPTH // N_MIXERS
N_MOD = 6
EPS = 1e-6

kernel_name = "hybrid_diffusion_gqa_pool_hmoe_step"

F32 = jnp.float32


def rms_norm(x, g):
    xf = x.astype(F32)
    y = xf * lax.rsqrt(jnp.mean(xf * xf, axis=-1, keepdims=True) + EPS)
    return (y * g.astype(F32)).astype(x.dtype)


def ada_modulation(cond, w, b):
    m = jax.nn.silu(cond) @ w + b
    m = m.reshape(cond.shape[0], 1, N_MOD, D_MODEL)
    return tuple(m[:, :, i] for i in range(N_MOD))


def modulate(x, g, shift, scale):
    return rms_norm(x, g) * (1.0 + scale) + shift


def grid_rope_tables(length):
    rows = length // GRID_W
    row = jnp.broadcast_to(jnp.arange(rows, dtype=F32)[:, None], (rows, GRID_W)).reshape(-1)
    col = jnp.broadcast_to(jnp.arange(GRID_W, dtype=F32)[None, :], (rows, GRID_W)).reshape(-1)
    axis_dim = HEAD_DIM // 2
    inv_freq = jnp.power(ROPE_THETA, -jnp.arange(0, axis_dim, 2, dtype=F32) / axis_dim)
    ang = jnp.concatenate([row[:, None] * inv_freq, col[:, None] * inv_freq], axis=-1)
    return jnp.cos(ang), jnp.sin(ang)


def apply_grid_rope(x, cos, sin):
    B, L, n, _ = x.shape
    quarter = HEAD_DIM // 4
    xs = x.astype(F32).reshape(B, L, n, 2, 2, quarter)
    x1 = xs[..., 0, :]
    x2 = xs[..., 1, :]
    c = cos.reshape(1, L, 1, 2, quarter)
    s = sin.reshape(1, L, 1, 2, quarter)
    out = jnp.stack([x1 * c - x2 * s, x2 * c + x1 * s], axis=-2)
    return out.reshape(x.shape).astype(x.dtype)


def qkv_projection(h, w_qkv, q_g, k_g):
    B, L, _ = h.shape
    qkv = h @ w_qkv
    nq = N_HEADS * HEAD_DIM
    nk = N_KV_HEADS * HEAD_DIM
    q = qkv[..., :nq].reshape(B, L, N_HEADS, HEAD_DIM)
    k = qkv[..., nq:nq + nk].reshape(B, L, N_KV_HEADS, HEAD_DIM)
    v = qkv[..., nq + nk:].reshape(B, L, N_KV_HEADS, HEAD_DIM)
    return rms_norm(q, q_g), rms_norm(k, k_g), v


def block_attention(q, k, v):
    B, L, _, _ = q.shape
    nb = L // Q_BLOCK
    qb = q.reshape(B, nb, Q_BLOCK, N_KV_HEADS, Q_PER_KV, HEAD_DIM).transpose(1, 0, 2, 3, 4, 5)
    kf = k.astype(F32)
    vf = v.astype(F32)
    scale = HEAD_DIM ** -0.5

    def one_block(qblk):
        s = jnp.einsum('bqkgd,bskd->bkgqs', qblk.astype(F32), kf) * scale
        p = jax.nn.softmax(s, axis=-1)
        return jnp.einsum('bkgqs,bskd->bqkgd', p, vf).astype(v.dtype)

    o = lax.map(one_block, qb)
    return o.transpose(1, 0, 2, 3, 4, 5).reshape(B, L, N_HEADS * HEAD_DIM)


def attn_mixer_context(h, w_qkv, q_g, k_g, w_o):
    q, k, v = qkv_projection(h, w_qkv, q_g, k_g)
    o = block_attention(q, k, v)
    return o @ w_o, k, v


def attn_mixer_latent(h, k_ctx, v_ctx, w_qkv, q_g, k_g, w_o):
    q, k, v = qkv_projection(h, w_qkv, q_g, k_g)
    cos, sin = grid_rope_tables(h.shape[1])
    q = apply_grid_rope(q, cos, sin)
    k = apply_grid_rope(k, cos, sin)
    k_all = jnp.concatenate([k_ctx.astype(k.dtype), k], axis=1)
    v_all = jnp.concatenate([v_ctx.astype(v.dtype), v], axis=1)
    o = block_attention(q, k_all, v_all)
    return o @ w_o


def multi_scale_pool(h, w_pool, scale):
    B, L, D = h.shape
    hf = h.astype(F32)
    csum = jnp.concatenate([jnp.zeros((B, 1, D), F32), jnp.cumsum(hf, axis=1)], axis=1)
    csum = csum.reshape(B, L + 1, N_POOL_GROUPS, POOL_GROUP_DIM)
    win = jnp.array(POOL_WINDOWS, dtype=jnp.int32)
    left = win // 2
    right = win - left - 1
    t = jnp.arange(L, dtype=jnp.int32)[:, None]
    lo = jnp.maximum(t - left, 0)
    hi = jnp.minimum(t + right + 1, L)
    gi = jnp.arange(N_POOL_GROUPS)[None, :]
    window_sum = csum[:, hi, gi] - csum[:, lo, gi]
    mean = window_sum / (hi - lo).astype(F32)[None, :, :, None]
    diff = mean - hf.reshape(B, L, N_POOL_GROUPS, POOL_GROUP_DIM)
    out = jnp.einsum('blgc,gcd->blgd', diff, w_pool.astype(F32)).reshape(B, L, D)
    return (out * scale.astype(F32)).astype(h.dtype)


def hierarchical_moe(h, w_group, w_expert, w_gate, w_up, w_down):
    B, L, D = h.shape
    x = h.reshape(B * L, D)
    xf = x.astype(F32)
    group_logits = xf @ w_group.astype(F32)
    group_prob = jax.nn.softmax(group_logits, axis=-1)
    g_idx = jnp.argmax(group_logits, axis=-1)
    g_w = jnp.take_along_axis(group_prob, g_idx[:, None], axis=1)[:, 0]
    exp_logits = jnp.einsum('td,gde->tge', xf, w_expert.astype(F32))
    sel = jnp.take_along_axis(exp_logits, g_idx[:, None, None], axis=1)[:, 0]
    top_logit, top_j = lax.top_k(sel, TOP_K_IN_GROUP)
    weights = g_w[:, None] * jax.nn.softmax(top_logit, axis=-1)
    e_idx = g_idx[:, None] * EXPERTS_PER_GROUP + top_j
    combine = jnp.sum(jax.nn.one_hot(e_idx, N_EXPERTS, dtype=F32) * weights[..., None], axis=1)
    a = jnp.einsum('td,edf->tef', x, w_gate)
    b = jnp.einsum('td,edf->tef', x, w_up)
    hid = jax.nn.silu(a) * b * combine[:, :, None].astype(x.dtype)
    y = jnp.einsum('tef,efd->td', hid, w_down)
    return y.reshape(B, L, D)


def setup_inputs(seed: int = 0) -> dict:
    key = jax.random.key(seed)
    ks = jax.random.split(key, 24)

    def nrm(k, shape, s):
        return jax.random.normal(k, shape, F32) * s

    dq = (N_HEADS + 2 * N_KV_HEADS) * HEAD_DIM
    return {
        "x_prompt": nrm(ks[0], (BATCH, SEQ, D_MODEL), 1.0),
        "x_sample": nrm(ks[1], (DEC_BATCH, DEC_SEQ, D_MODEL), 1.0),
        "c": nrm(ks[2], (DEC_BATCH, D_MODEL), 1.0),
        "cache_k": nrm(ks[3], (DEC_BATCH, N_ATTN_LAYERS, PAST_LEN, N_KV_HEADS, HEAD_DIM), 1.0),
        "cache_v": nrm(ks[4], (DEC_BATCH, N_ATTN_LAYERS, PAST_LEN, N_KV_HEADS, HEAD_DIM), 1.0),
        "c_ctx": nrm(ks[5], (D_MODEL,), 1.0),
        "norm_mix_g": 1.0 + nrm(ks[6], (DEPTH, D_MODEL), 0.02),
        "norm_ffn_g": 1.0 + nrm(ks[7], (DEPTH, D_MODEL), 0.02),
        "ada_w": nrm(ks[8], (DEPTH, D_MODEL, N_MOD * D_MODEL), D_MODEL ** -0.5),
        "ada_b": nrm(ks[9], (DEPTH, N_MOD * D_MODEL), 0.02),
        "attn_w_qkv": nrm(ks[10], (N_ATTN_LAYERS, D_MODEL, dq), D_MODEL ** -0.5),
        "attn_q_norm": 1.0 + nrm(ks[11], (N_ATTN_LAYERS, HEAD_DIM), 0.02),
        "attn_k_norm": 1.0 + nrm(ks[12], (N_ATTN_LAYERS, HEAD_DIM), 0.02),
        "attn_w_o": nrm(ks[13], (N_ATTN_LAYERS, N_HEADS * HEAD_DIM, D_MODEL), (N_HEADS * HEAD_DIM) ** -0.5),
        "pool_w": nrm(ks[14], (N_POOL_LAYERS, N_POOL_GROUPS, POOL_GROUP_DIM, POOL_GROUP_DIM), POOL_GROUP_DIM ** -0.5),
        "pool_scale": 1.0 + nrm(ks[15], (N_POOL_LAYERS, D_MODEL), 0.1),
        "moe_w_group": nrm(ks[16], (DEPTH, D_MODEL, N_EXPERT_GROUPS), D_MODEL ** -0.5),
        "moe_w_expert": nrm(ks[17], (DEPTH, N_EXPERT_GROUPS, D_MODEL, EXPERTS_PER_GROUP), D_MODEL ** -0.5),
        "moe_w_gate": nrm(ks[18], (DEPTH, N_EXPERTS, D_MODEL, D_EXPERT), D_MODEL ** -0.5),
        "moe_w_up": nrm(ks[19], (DEPTH, N_EXPERTS, D_MODEL, D_EXPERT), D_MODEL ** -0.5),
        "moe_w_down": nrm(ks[20], (DEPTH, N_EXPERTS, D_EXPERT, D_MODEL), D_EXPERT ** -0.5),
    }


def reference(x_prompt, x_sample, c, cache_k, cache_v, c_ctx, norm_mix_g, norm_ffn_g, ada_w, ada_b,
              attn_w_qkv, attn_q_norm, attn_k_norm, attn_w_o, pool_w, pool_scale,
              moe_w_group, moe_w_expert, moe_w_gate, moe_w_up, moe_w_down):
    x = x_prompt
    cond_ctx = c_ctx[None, :]
    new_k_list = []
    new_v_list = []
    for l in range(DEPTH):
        sh1, sc1, g1, sh2, sc2, g2 = ada_modulation(cond_ctx, ada_w[l], ada_b[l])
        h = modulate(x, norm_mix_g[l], sh1, sc1)
        if l % N_MIXERS == 0:
            a = l // N_MIXERS
            m, k, v = attn_mixer_context(h, attn_w_qkv[a], attn_q_norm[a], attn_k_norm[a], attn_w_o[a])
            new_k_list.append(k)
            new_v_list.append(v)
        else:
            p = l // N_MIXERS
            m = multi_scale_pool(h, pool_w[p], pool_scale[p])
        x = x + g1 * m
        h = modulate(x, norm_ffn_g[l], sh2, sc2)
        x = x + g2 * hierarchical_moe(h, moe_w_group[l], moe_w_expert[l], moe_w_gate[l],
                                      moe_w_up[l], moe_w_down[l])
    y_prompt = x
    new_k = jnp.stack(new_k_list, axis=1)
    new_v = jnp.stack(new_v_list, axis=1)

    x = x_sample
    for l in range(DEPTH):
        sh1, sc1, g1, sh2, sc2, g2 = ada_modulation(c, ada_w[l], ada_b[l])
        h = modulate(x, norm_mix_g[l], sh1, sc1)
        if l % N_MIXERS == 0:
            a = l // N_MIXERS
            m = attn_mixer_latent(h, cache_k[:, a], cache_v[:, a], attn_w_qkv[a], attn_q_norm[a],
                                  attn_k_norm[a], attn_w_o[a])
        else:
            p = l // N_MIXERS
            m = multi_scale_pool(h, pool_w[p], pool_scale[p])
        x = x + g1 * m
        h = modulate(x, norm_ffn_g[l], sh2, sc2)
        x = x + g2 * hierarchical_moe(h, moe_w_group[l], moe_w_expert[l], moe_w_gate[l],
                                      moe_w_up[l], moe_w_down[l])
    y_sample = x
    return (y_prompt, y_sample, new_k, new_v)
```

```python
import functools

import jax
import jax.numpy as jnp
from jax import lax
from jax.experimental import pallas as pl
from jax.experimental.pallas import tpu as pltpu

F32 = jnp.float32
BF16 = jnp.bfloat16
I32 = jnp.int32

D = 1024
BATCH, SEQ = 32, 256
DEC_BATCH, DEC_SEQ, PAST = 8, 1024, 512
T_CTX = BATCH * SEQ
T_LAT = DEC_BATCH * DEC_SEQ
T = T_CTX + T_LAT
GRID_W = 64
N_HEADS, N_KV, HD = 16, 4, 64
DQ = N_HEADS * HD
DKV = N_KV * HD
ROPE_THETA = 10000.0
POOL_WINDOWS = (2, 4, 8, 16)
PGD = D // len(POOL_WINDOWS)
N_GROUPS, E_PER_G, TOP_K = 4, 8, 2
NE = N_GROUPS * E_PER_G
DE = D // 4
N_MOD = 6
EPS = 1e-6
N_COND = 16

TILE = 512
NT = T // TILE
NT_CTX = T_CTX // TILE
PTILE = 1024
TM = 256
N_ROW_TILES = (TOP_K * T) // TM + NE
NPAD = N_ROW_TILES * TM
DT = 512
CT = 256
LANES = 128
NEG_INF = float("-inf")


def _vmem_limit(mib):
    return pltpu.CompilerParams(vmem_limit_bytes=mib * 1024 * 1024)


def _modulate(x, gain_scale, shift):
    ms = jnp.mean(x * x, axis=-1, keepdims=True)
    return (x * lax.rsqrt(ms + EPS)) * gain_scale + shift


def _head_norm(z, gain, ind, ind_t):
    ss = jnp.dot((z * z).astype(BF16), ind, preferred_element_type=F32)
    inv = lax.rsqrt(ss * (1.0 / HD) + EPS)
    inv_hi = inv.astype(BF16)
    inv_lo = (inv - inv_hi.astype(F32)).astype(BF16)
    scale = (jnp.dot(inv_hi, ind_t, preferred_element_type=F32)
             + jnp.dot(inv_lo, ind_t, preferred_element_type=F32))
    return z * scale * gain


def _rope(z, cos_t, sin_t):
    lane = lax.broadcasted_iota(I32, (z.shape[0], LANES), 1)
    low = (lane % 32) < 16
    outs = []
    for c in range(z.shape[1] // LANES):
        zc = z[:, c * LANES:(c + 1) * LANES]
        up = pltpu.roll(zc, 16, axis=1)
        dn = pltpu.roll(zc, LANES - 16, axis=1)
        outs.append(zc * cos_t + jnp.where(low, dn, up) * sin_t)
    return jnp.concatenate(outs, axis=1)


def _ada_kernel(c_ref, w_ref, b_ref, o_ref):
    c = c_ref[...]
    a = c * (1.0 / (1.0 + jnp.exp(-c)))
    o_ref[0] = jnp.dot(a.astype(BF16), w_ref[0].astype(BF16), preferred_element_type=F32) + b_ref[0]


def _ada(cond, ada_w, ada_b):
    nb = 1536
    depth = ada_w.shape[0]
    out = pl.pallas_call(
        _ada_kernel,
        grid=(depth, (N_MOD * D) // nb),
        in_specs=[
            pl.BlockSpec((N_COND, D), lambda l, j: (0, 0)),
            pl.BlockSpec((1, D, nb), lambda l, j: (l, 0, j)),
            pl.BlockSpec((1, 1, nb), lambda l, j: (l, 0, j)),
        ],
        out_specs=pl.BlockSpec((1, N_COND, nb), lambda l, j: (l, 0, j)),
        out_shape=jax.ShapeDtypeStruct((depth, N_COND, N_MOD * D), F32),
        compiler_params=_vmem_limit(40),
        name="ada",
    )(cond, ada_w, ada_b.reshape(depth, 1, N_MOD * D))
    return out.reshape(depth, N_COND, N_MOD, D)


def _cond_index(i, tile):
    n_ctx = T_CTX // tile
    per_batch = DEC_SEQ // tile
    return jnp.where(i < n_ctx, 0, 1 + (i - n_ctx) // per_batch)


def _qkv_kernel(xp_ref, xs_ref, mod_ref, g_ref, w_ref, qg_ref, kg_ref, indq_ref, indqt_ref,
                indk_ref, indkt_ref, dup_ref, dupt_ref, cos_ref, sin_ref,
                q_ref, kt_ref, v2_ref, nk_ref, nv_ref):
    i = pl.program_id(0)
    is_lat = i >= NT_CTX
    x = jnp.where(is_lat, xs_ref[...], xp_ref[...])
    h = _modulate(x, g_ref[...] * (1.0 + mod_ref[0, 1:2, :]), mod_ref[0, 0:1, :])
    qkv = jnp.dot(h.astype(BF16), w_ref[...], preferred_element_type=F32)
    q = _head_norm(qkv[:, :DQ], qg_ref[...], indq_ref[...], indqt_ref[...])
    k = _head_norm(qkv[:, DQ:DQ + DKV], kg_ref[...], indk_ref[...], indkt_ref[...])
    v = qkv[:, DQ + DKV:]

    def store(qn, kn):
        q_ref[...] = (qn * (HD ** -0.5)).astype(BF16)
        kt_ref[...] = lax.dot_general(dupt_ref[...], kn.astype(BF16), (((1,), (1,)), ((), ())),
                                      preferred_element_type=F32).astype(BF16)
        v2_ref[...] = jnp.dot(v.astype(BF16), dup_ref[...], preferred_element_type=F32).astype(BF16)

    @pl.when(is_lat)
    def _():
        store(_rope(q, cos_ref[...], sin_ref[...]), _rope(k, cos_ref[...], sin_ref[...]))

    @pl.when(jnp.logical_not(is_lat))
    def _():
        store(q, k)
        nk_ref[...] = k
        nv_ref[...] = v


def _rope_tables():
    rows = DEC_SEQ // GRID_W
    row = jnp.broadcast_to(jnp.arange(rows, dtype=F32)[:, None], (rows, GRID_W)).reshape(-1)
    col = jnp.broadcast_to(jnp.arange(GRID_W, dtype=F32)[None, :], (rows, GRID_W)).reshape(-1)
    axis_dim = HD // 2
    inv_freq = jnp.power(ROPE_THETA, -jnp.arange(0, axis_dim, 2, dtype=F32) / axis_dim)
    ang = jnp.concatenate([row[:, None] * inv_freq, col[:, None] * inv_freq], axis=-1)
    cos, sin = jnp.cos(ang), jnp.sin(ang)
    quarter = HD // 4
    cos_h = jnp.concatenate([cos[:, :quarter], cos[:, :quarter], cos[:, quarter:], cos[:, quarter:]], axis=1)
    sin_h = jnp.concatenate([-sin[:, :quarter], sin[:, :quarter], -sin[:, quarter:], sin[:, quarter:]], axis=1)
    return jnp.tile(cos_h, (1, LANES // HD)), jnp.tile(sin_h, (1, LANES // HD))


def _head_indicators(width):
    col = jnp.arange(width)[:, None]
    head = jnp.arange(LANES)[None, :]
    ind = (col // HD == head).astype(BF16)
    return ind, ind.T


def _dup_matrix():
    src = jnp.arange(DKV)[:, None]
    dst = jnp.arange(2 * DKV)[None, :]
    return (src == HD * (dst // LANES) + dst % HD).astype(BF16)


def _qkv(xp, xs, mod, g, w_qkv, q_gain, k_gain):
    indq, indqt = _head_indicators(DQ)
    indk, indkt = _head_indicators(DKV)
    dup = _dup_matrix()
    cos_t, sin_t = _rope_tables()
    per_batch = DEC_SEQ // TILE
    const = lambda shape: pl.BlockSpec(shape, lambda i: (0,) * len(shape))
    return pl.pallas_call(
        _qkv_kernel,
        grid=(NT,),
        in_specs=[
            pl.BlockSpec((TILE, D), lambda i: (jnp.minimum(i, NT_CTX - 1), 0)),
            pl.BlockSpec((TILE, D), lambda i: (jnp.maximum(i - NT_CTX, 0), 0)),
            pl.BlockSpec((1, N_MOD, D), lambda i: (_cond_index(i, TILE), 0, 0)),
            const((1, D)),
            const((D, DQ + 2 * DKV)),
            const((1, DQ)), const((1, DKV)),
            const((DQ, LANES)), const((LANES, DQ)),
            const((DKV, LANES)), const((LANES, DKV)),
            const((DKV, 2 * DKV)), const((2 * DKV, DKV)),
            pl.BlockSpec((TILE, LANES), lambda i: (jnp.maximum(i - NT_CTX, 0) % per_batch, 0)),
            pl.BlockSpec((TILE, LANES), lambda i: (jnp.maximum(i - NT_CTX, 0) % per_batch, 0)),
        ],
        out_specs=[
            pl.BlockSpec((TILE, DQ), lambda i: (i, 0)),
            pl.BlockSpec((2 * DKV, TILE), lambda i: (0, i)),
            pl.BlockSpec((TILE, 2 * DKV), lambda i: (i, 0)),
            pl.BlockSpec((TILE, DKV), lambda i: (jnp.minimum(i, NT_CTX - 1), 0)),
            pl.BlockSpec((TILE, DKV), lambda i: (jnp.minimum(i, NT_CTX - 1), 0)),
        ],
        out_shape=[
            jax.ShapeDtypeStruct((T, DQ), BF16),
            jax.ShapeDtypeStruct((2 * DKV, T), BF16),
            jax.ShapeDtypeStruct((T, 2 * DKV), BF16),
            jax.ShapeDtypeStruct((T_CTX, DKV), F32),
            jax.ShapeDtypeStruct((T_CTX, DKV), F32),
        ],
        compiler_params=_vmem_limit(56),
        name="qkv",
    )(xp, xs, mod, g, w_qkv, jnp.tile(q_gain, (1, N_HEADS)), jnp.tile(k_gain, (1, N_KV)),
      indq, indqt, indk, indkt, dup, dup.T, cos_t, sin_t)


def _attn_pair(qp, kts, vlos, vhis):
    lq = qp.shape[0]
    lane = lax.broadcasted_iota(I32, qp.shape, 1)
    zero = jnp.zeros_like(qp)
    q2 = jnp.concatenate([jnp.where(lane < HD, qp, zero), jnp.where(lane >= HD, qp, zero)], axis=0)
    ss = [jnp.dot(q2, kt, preferred_element_type=F32) for kt in kts]
    m = ss[0].max(axis=-1, keepdims=True)
    for s in ss[1:]:
        m = jnp.maximum(m, s.max(axis=-1, keepdims=True))
    acc = None
    l = None
    for s, vlo, vhi in zip(ss, vlos, vhis):
        e = jnp.exp(s - m)
        ls = e.sum(axis=-1, keepdims=True)
        l = ls if l is None else l + ls
        eb = e.astype(BF16)
        t = (jnp.dot(eb[:lq], vlo, preferred_element_type=F32)
             + jnp.dot(eb[lq:], vhi, preferred_element_type=F32))
        acc = t if acc is None else acc + t
    linv = 1.0 / l
    return acc * jnp.where(lane < HD, linv[:lq], linv[lq:])


def _split_values(vd):
    lane = lax.broadcasted_iota(I32, vd.shape, 1)
    zero = jnp.zeros_like(vd)
    return jnp.where(lane < HD, vd, zero), jnp.where(lane >= HD, vd, zero)


def _attn_ctx_kernel(q_ref, kt_ref, v_ref, o_ref):
    for g in range(N_KV):
        kt = kt_ref[g * LANES:(g + 1) * LANES, :]
        vlo, vhi = _split_values(v_ref[:, g * LANES:(g + 1) * LANES])
        for p in range(2):
            c = (2 * g + p) * LANES
            o_ref[:, c:c + LANES] = _attn_pair(q_ref[:, c:c + LANES], [kt], [vlo], [vhi]).astype(BF16)


def _attn_lat_kernel(q_ref, ktc_ref, vc_ref, ktn_ref, vn_ref, o_ref):
    for g in range(N_KV):
        rows = slice(g * LANES, (g + 1) * LANES)
        kts = [ktc_ref[0, rows, :], ktn_ref[rows, :]]
        vc_lo, vc_hi = _split_values(vc_ref[0, :, rows])
        vn_lo, vn_hi = _split_values(vn_ref[:, rows])
        for p in range(2):
            c = (2 * g + p) * LANES
            o_ref[:, c:c + LANES] = _attn_pair(q_ref[:, c:c + LANES], kts, [vc_lo, vn_lo],
                                               [vc_hi, vn_hi]).astype(BF16)


def _cache_kernel(ck_ref, cv_ref, dup_ref, dupt_ref, kt_ref, v2_ref):
    kt_ref[0] = lax.dot_general(dupt_ref[...], ck_ref[0].astype(BF16), (((1,), (1,)), ((), ())),
                                preferred_element_type=F32).astype(BF16)
    v2_ref[0] = jnp.dot(cv_ref[0].astype(BF16), dup_ref[...], preferred_element_type=F32).astype(BF16)


def _attention(q, kt2, v2, cache_k, cache_v):
    dup = _dup_matrix()
    ktc, vc2 = pl.pallas_call(
        _cache_kernel,
        grid=(DEC_BATCH,),
        in_specs=[
            pl.BlockSpec((1, PAST, DKV), lambda b: (b, 0, 0)),
            pl.BlockSpec((1, PAST, DKV), lambda b: (b, 0, 0)),
            pl.BlockSpec((DKV, 2 * DKV), lambda b: (0, 0)),
            pl.BlockSpec((2 * DKV, DKV), lambda b: (0, 0)),
        ],
        out_specs=[
            pl.BlockSpec((1, 2 * DKV, PAST), lambda b: (b, 0, 0)),
            pl.BlockSpec((1, PAST, 2 * DKV), lambda b: (b, 0, 0)),
        ],
        out_shape=[
            jax.ShapeDtypeStruct((DEC_BATCH, 2 * DKV, PAST), BF16),
            jax.ShapeDtypeStruct((DEC_BATCH, PAST, 2 * DKV), BF16),
        ],
        name="cache_prep",
    )(cache_k, cache_v, dup, dup.T)

    o_ctx = pl.pallas_call(
        _attn_ctx_kernel,
        grid=(BATCH,),
        in_specs=[
            pl.BlockSpec((SEQ, DQ), lambda b: (b, 0)),
            pl.BlockSpec((2 * DKV, SEQ), lambda b: (0, b)),
            pl.BlockSpec((SEQ, 2 * DKV), lambda b: (b, 0)),
        ],
        out_specs=pl.BlockSpec((SEQ, DQ), lambda b: (b, 0)),
        out_shape=jax.ShapeDtypeStruct((T_CTX, DQ), BF16),
        name="attn_ctx",
    )(q, kt2, v2)

    qb = 256
    nqb = DEC_SEQ // qb
    lat0 = T_CTX // qb
    o_lat = pl.pallas_call(
        _attn_lat_kernel,
        grid=(DEC_BATCH, nqb),
        in_specs=[
            pl.BlockSpec((qb, DQ), lambda b, j: (lat0 + b * nqb + j, 0)),
            pl.BlockSpec((1, 2 * DKV, PAST), lambda b, j: (b, 0, 0)),
            pl.BlockSpec((1, PAST, 2 * DKV), lambda b, j: (b, 0, 0)),
            pl.BlockSpec((2 * DKV, DEC_SEQ), lambda b, j: (0, T_CTX // DEC_SEQ + b)),
            pl.BlockSpec((DEC_SEQ, 2 * DKV), lambda b, j: (T_CTX // DEC_SEQ + b, 0)),
        ],
        out_specs=pl.BlockSpec((qb, DQ), lambda b, j: (b * nqb + j, 0)),
        out_shape=jax.ShapeDtypeStruct((T_LAT, DQ), BF16),
        compiler_params=_vmem_limit(48),
        name="attn_lat",
    )(q, ktc, vc2, kt2, v2)
    return o_ctx, o_lat


def _route(h, wr_hl_ref, wr_hi_ref, tri_ref, cnt_ref, e_ref, rank_ref, wcol_ref, cnt_out_ref):
    tl = h.shape[0]
    h_hi = h.astype(BF16)
    h_lo = (h - h_hi.astype(F32)).astype(BF16)
    lg2 = jnp.dot(h_hi, wr_hl_ref[...], preferred_element_type=F32)
    lg = lg2[:, :LANES] + lg2[:, LANES:] + jnp.dot(h_lo, wr_hi_ref[...], preferred_element_type=F32)
    lt = lg.T

    sub8 = lax.broadcasted_iota(I32, (8, tl), 0).astype(F32)
    gl = jnp.where(sub8 < N_GROUPS, lt[0:8], NEG_INF)
    gmax = gl.max(axis=0, keepdims=True)
    gidx = jnp.min(jnp.where(gl == gmax, sub8, 8.0), axis=0, keepdims=True)
    gsum = jnp.sum(jnp.exp(gl - gmax), axis=0, keepdims=True)
    sel = lt[8:16]
    for g in range(1, N_GROUPS):
        sel = jnp.where(gidx == g, lt[8 + 8 * g:16 + 8 * g], sel)
    t1 = sel.max(axis=0, keepdims=True)
    j1 = jnp.min(jnp.where(sel == t1, sub8, 8.0), axis=0, keepdims=True)
    sel2 = jnp.where(sub8 == j1, NEG_INF, sel)
    t2 = sel2.max(axis=0, keepdims=True)
    j2 = jnp.min(jnp.where(sel2 == t2, sub8, 8.0), axis=0, keepdims=True)
    ex = jnp.exp(t2 - t1)
    den = 1.0 + ex
    gw = 1.0 / gsum
    w0 = gw * (1.0 / den)
    w1 = gw * (ex / den)
    e0 = gidx * E_PER_G + j1
    e1 = gidx * E_PER_G + j2

    sub_e = lax.broadcasted_iota(I32, (NE, tl), 0).astype(F32)
    oh0 = (sub_e == e0).astype(F32)
    oh1 = (sub_e == e1).astype(F32)
    c = oh0 + oh1
    carry = cnt_ref[:, 0:1]
    pieces = []
    for ch in range(tl // 256):
        cc = c[:, ch * 256:(ch + 1) * 256]
        pieces.append(jnp.dot(cc.astype(BF16), tri_ref[...], preferred_element_type=F32) + carry)
        carry = carry + jnp.sum(cc, axis=1, keepdims=True)
    csum = jnp.concatenate(pieces, axis=1)
    cnt_ref[...] = jnp.broadcast_to(carry, (NE, LANES))
    cnt_out_ref[...] = jnp.broadcast_to(carry, (NE, LANES))
    r0 = jnp.sum(oh0 * csum, axis=0, keepdims=True) - 1.0
    r1 = jnp.sum(oh1 * csum, axis=0, keepdims=True) - 1.0
    e_ref[0:1, :] = e0.astype(I32)
    e_ref[1:2, :] = e1.astype(I32)
    rank_ref[0:1, :] = r0.astype(I32)
    rank_ref[1:2, :] = r1.astype(I32)
    sub_w = lax.broadcasted_iota(I32, (LANES, tl), 0)
    w8 = jnp.where(sub_w == 0, w0, jnp.where(sub_w == 1, w1, 0.0))
    wcol_ref[...] = w8.T


def _post_kernel(o_ref, xp_ref, xs_ref, mod_ref, wo_ref, g_ref, wr_hl_ref, wr_hi_ref, tri_ref,
                 x1_ref, h_ref, e_ref, rank_ref, wcol_ref, cnt_out_ref, cnt_ref):
    i = pl.program_id(0)

    @pl.when(i == 0)
    def _():
        cnt_ref[...] = jnp.zeros_like(cnt_ref)

    x = jnp.where(i >= NT_CTX, xs_ref[...], xp_ref[...])
    m = jnp.dot(o_ref[...], wo_ref[...], preferred_element_type=F32)
    x1 = x + mod_ref[0, 2:3, :] * m
    x1_ref[...] = x1
    h = _modulate(x1, g_ref[...] * (1.0 + mod_ref[0, 4:5, :]), mod_ref[0, 3:4, :])
    h_ref[...] = h
    _route(h, wr_hl_ref, wr_hi_ref, tri_ref, cnt_ref, e_ref, rank_ref, wcol_ref, cnt_out_ref)


def _router_weights(w_group, w_expert):
    w = jnp.zeros((D, LANES), F32)
    w = w.at[:, :N_GROUPS].set(w_group)
    w = w.at[:, 8:8 + NE].set(jnp.transpose(w_expert, (1, 0, 2)).reshape(D, NE))
    w_hi = w.astype(BF16)
    w_lo = (w - w_hi.astype(F32)).astype(BF16)
    return jnp.concatenate([w_hi, w_lo], axis=1), w_hi


def _route_outs(tile):
    specs = [
        pl.BlockSpec((tile, D), lambda i: (i, 0)),
        pl.BlockSpec((tile, D), lambda i: (i, 0)),
        pl.BlockSpec((TOP_K, tile), lambda i: (0, i)),
        pl.BlockSpec((TOP_K, tile), lambda i: (0, i)),
        pl.BlockSpec((tile, LANES), lambda i: (i, 0)),
        pl.BlockSpec((NE, LANES), lambda i: (0, 0)),
    ]
    shapes = [
        jax.ShapeDtypeStruct((T, D), F32),
        jax.ShapeDtypeStruct((T, D), F32),
        jax.ShapeDtypeStruct((TOP_K, T), I32),
        jax.ShapeDtypeStruct((TOP_K, T), I32),
        jax.ShapeDtypeStruct((T, LANES), F32),
        jax.ShapeDtypeStruct((NE, LANES), F32),
    ]
    return specs, shapes


def _tri():
    a = jnp.arange(256)
    return (a[:, None] <= a[None, :]).astype(BF16)


def _post(o, xp, xs, mod, w_o, g_ffn, wr_hl, wr_hi):
    const = lambda shape: pl.BlockSpec(shape, lambda i: (0,) * len(shape))
    out_specs, out_shape = _route_outs(TILE)
    return pl.pallas_call(
        _post_kernel,
        grid=(NT,),
        in_specs=[
            pl.BlockSpec((TILE, DQ), lambda i: (i, 0)),
            pl.BlockSpec((TILE, D), lambda i: (jnp.minimum(i, NT_CTX - 1), 0)),
            pl.BlockSpec((TILE, D), lambda i: (jnp.maximum(i - NT_CTX, 0), 0)),
            pl.BlockSpec((1, N_MOD, D), lambda i: (_cond_index(i, TILE), 0, 0)),
            const((DQ, D)), const((1, D)), const((D, 2 * LANES)), const((D, LANES)), const((256, 256)),
        ],
        out_specs=out_specs,
        out_shape=out_shape,
        scratch_shapes=[pltpu.VMEM((NE, LANES), F32)],
        compiler_params=_vmem_limit(56),
        name="post_attn",
    )(o, xp, xs, mod, w_o, g_ffn, wr_hl, wr_hi, _tri())


def _pool_kernel(x_ref, mod_ref, g_ref, pw_ref, ps_ref, gf_ref, wr_hl_ref, wr_hi_ref, tri_ref,
                 x1_ref, h_ref, e_ref, rank_ref, wcol_ref, cnt_out_ref, cnt_ref):
    i = pl.program_id(0)

    @pl.when(i == 0)
    def _():
        cnt_ref[...] = jnp.zeros_like(cnt_ref)

    x = x_ref[...]
    h = _modulate(x, g_ref[...] * (1.0 + mod_ref[0, 1:2, :]), mod_ref[0, 0:1, :])
    seq = jnp.where(i >= T_CTX // PTILE, DEC_SEQ, SEQ)
    row = lax.broadcasted_iota(I32, (PTILE, PGD), 0)
    t = row & (seq - 1)

    def back(z, s):
        return jnp.where(t >= s, pltpu.roll(z, s, axis=0), 0.0)

    def fwd(z, s):
        return jnp.where(t + s < seq, pltpu.roll(z, PTILE - s, axis=0), 0.0)

    outs = []
    for gi, win in enumerate(POOL_WINDOWS):
        hg = h[:, gi * PGD:(gi + 1) * PGD]
        left = win // 2
        right = win - left - 1
        p = hg + back(hg, 1)
        width = 2
        while 2 * width <= left:
            p = p + back(p, width)
            width *= 2
        if left + 1 > width:
            p = p + back(hg, width)
        tot = p
        if right > 0:
            f = fwd(hg, 1)
            fw = 1
            parts = {1: f}
            while 2 * fw <= right:
                f = f + fwd(f, fw)
                fw *= 2
                parts[fw] = f
            lead = f
            done = fw
            rem = right - done
            while rem > 0:
                step = max(w_ for w_ in parts if w_ <= rem)
                lead = lead + fwd(parts[step], done)
                done += step
                rem -= step
            tot = tot + lead
        lo = jnp.maximum(t - left, 0)
        hi = jnp.minimum(t + right + 1, seq)
        diff = tot / (hi - lo).astype(F32) - hg
        outs.append(jnp.dot(diff.astype(BF16), pw_ref[gi], preferred_element_type=F32))
    m = jnp.concatenate(outs, axis=1) * ps_ref[...]
    x1 = x + mod_ref[0, 2:3, :] * m
    x1_ref[...] = x1
    hf = _modulate(x1, gf_ref[...] * (1.0 + mod_ref[0, 4:5, :]), mod_ref[0, 3:4, :])
    h_ref[...] = hf
    _route(hf, wr_hl_ref, wr_hi_ref, tri_ref, cnt_ref, e_ref, rank_ref, wcol_ref, cnt_out_ref)


def _pool(x, mod, g_mix, pool_w, pool_scale, g_ffn, wr_hl, wr_hi):
    const = lambda shape: pl.BlockSpec(shape, lambda i: (0,) * len(shape))
    out_specs, out_shape = _route_outs(PTILE)
    return pl.pallas_call(
        _pool_kernel,
        grid=(T // PTILE,),
        in_specs=[
            pl.BlockSpec((PTILE, D), lambda i: (i, 0)),
            pl.BlockSpec((1, N_MOD, D), lambda i: (_cond_index(i, PTILE), 0, 0)),
            const((1, D)), const((len(POOL_WINDOWS), PGD, PGD)), const((1, D)), const((1, D)),
            const((D, 2 * LANES)), const((D, LANES)), const((256, 256)),
        ],
        out_specs=out_specs,
        out_shape=out_shape,
        scratch_shapes=[pltpu.VMEM((NE, LANES), F32)],
        compiler_params=_vmem_limit(56),
        name="pool_mixer",
    )(x, mod, g_mix, pool_w, pool_scale, g_ffn, wr_hl, wr_hi, _tri())


def _plan(e, rank, cnt):
    counts = cnt[:, 0].astype(I32)
    n_tiles = (counts + TM - 1) // TM
    tile_end = jnp.cumsum(n_tiles)
    base = (tile_end - n_tiles) * TM
    pos = jnp.sum(jnp.where(e[None] == jnp.arange(NE, dtype=I32)[:, None, None], base[:, None, None], 0),
                  axis=0) + rank
    used = tile_end[-1]
    tile_id = jnp.minimum(jnp.arange(N_ROW_TILES, dtype=I32), used - 1)
    tile_expert = jnp.sum((tile_id[:, None] >= tile_end[None, :]).astype(I32), axis=1)
    return pos, tile_expert, used.reshape(1)


def _row_copy(src_ref, src_row, dst_ref, dst_row, sem):
    return pltpu.make_async_copy(src_ref.at[pl.ds(src_row, 1)], dst_ref.at[pl.ds(dst_row, 1)], sem)


def _dispatch_kernel(pos_ref, h_ref, xs_ref, sem):
    def start(r, carry):
        for k in range(TOP_K):
            _row_copy(h_ref, r, xs_ref, pos_ref[k, r], sem).start()
        return carry

    def wait(r, carry):
        for k in range(TOP_K):
            _row_copy(h_ref, 0, xs_ref, 0, sem).wait()
        return carry

    lax.fori_loop(0, DT, start, 0, unroll=8)
    lax.fori_loop(0, DT, wait, 0, unroll=8)


def _dispatch(pos, h):
    return pl.pallas_call(
        _dispatch_kernel,
        grid=(T // DT,),
        in_specs=[
            pl.BlockSpec((TOP_K, DT), lambda i: (0, i), memory_space=pltpu.SMEM),
            pl.BlockSpec((DT, D), lambda i: (i, 0)),
        ],
        out_specs=pl.BlockSpec(memory_space=pl.ANY),
        out_shape=jax.ShapeDtypeStruct((NPAD, D), F32),
        scratch_shapes=[pltpu.SemaphoreType.DMA],
        name="moe_dispatch",
    )(pos, h)


def _moe_kernel(te_ref, used_ref, x_ref, wg_ref, wu_ref, wd_ref, y_ref, wg_s, wu_s, wd_s):
    i = pl.program_id(0)

    @pl.when(i < used_ref[0])
    def _():
        prev = te_ref[jnp.maximum(i - 1, 0)]

        @pl.when(jnp.logical_or(i == 0, te_ref[i] != prev))
        def _():
            wg_s[...] = wg_ref[0].astype(BF16)
            wu_s[...] = wu_ref[0].astype(BF16)
            wd_s[...] = wd_ref[0].astype(BF16)

        x = x_ref[...].astype(BF16)
        a = jnp.dot(x, wg_s[...], preferred_element_type=F32)
        b = jnp.dot(x, wu_s[...], preferred_element_type=F32)
        hid = a * (1.0 / (1.0 + jnp.exp(-a))) * b
        y_ref[...] = jnp.dot(hid.astype(BF16), wd_s[...], preferred_element_type=F32)


def _experts(xs, tile_expert, used, w_gate, w_up, w_down):
    row = lambda i, te, used: (jnp.minimum(i, used[0] - 1), 0)
    grid_spec = pltpu.PrefetchScalarGridSpec(
        num_scalar_prefetch=2,
        grid=(N_ROW_TILES,),
        in_specs=[
            pl.BlockSpec((TM, D), row),
            pl.BlockSpec((1, D, DE), lambda i, te, used: (te[i], 0, 0)),
            pl.BlockSpec((1, D, DE), lambda i, te, used: (te[i], 0, 0)),
            pl.BlockSpec((1, DE, D), lambda i, te, used: (te[i], 0, 0)),
        ],
        out_specs=pl.BlockSpec((TM, D), row),
        scratch_shapes=[pltpu.VMEM((D, DE), BF16), pltpu.VMEM((D, DE), BF16), pltpu.VMEM((DE, D), BF16)],
    )
    return pl.pallas_call(
        _moe_kernel,
        grid_spec=grid_spec,
        out_shape=jax.ShapeDtypeStruct((NPAD, D), F32),
        compiler_params=_vmem_limit(40),
        name="moe_experts",
    )(tile_expert, used, xs, w_gate, w_up, w_down)


def _gather_rows(pos_ref, ys_ref, ybuf, sem):
    def start(r, carry):
        for k in range(TOP_K):
            _row_copy(ys_ref, pos_ref[k, r], ybuf.at[k], r, sem).start()
        return carry

    def wait(r, carry):
        for k in range(TOP_K):
            _row_copy(ys_ref, 0, ybuf.at[k], 0, sem).wait()
        return carry

    lax.fori_loop(0, CT, start, 0, unroll=8)
    lax.fori_loop(0, CT, wait, 0, unroll=8)


def _combined(x_ref, wcol_ref, mod_ref, ybuf):
    y = wcol_ref[:, 0:1] * ybuf[0] + wcol_ref[:, 1:2] * ybuf[1]
    return x_ref[...] + mod_ref[0, 5:6, :] * y


def _combine_kernel(pos_ref, ys_ref, x_ref, wcol_ref, mod_ref, o_ref, ybuf, sem):
    _gather_rows(pos_ref, ys_ref, ybuf, sem)
    o_ref[...] = _combined(x_ref, wcol_ref, mod_ref, ybuf)


def _combine_final_kernel(pos_ref, ys_ref, x_ref, wcol_ref, mod_ref, op_ref, os_ref, ybuf, sem):
    i = pl.program_id(0)
    _gather_rows(pos_ref, ys_ref, ybuf, sem)
    out = _combined(x_ref, wcol_ref, mod_ref, ybuf)

    @pl.when(i < T_CTX // CT)
    def _():
        op_ref[...] = out

    @pl.when(i >= T_CTX // CT)
    def _():
        os_ref[...] = out


def _combine(pos, ys, x, wcol, mod, final):
    n_ctx = T_CTX // CT
    in_specs = [
        pl.BlockSpec((TOP_K, CT), lambda i: (0, i), memory_space=pltpu.SMEM),
        pl.BlockSpec(memory_space=pl.ANY),
        pl.BlockSpec((CT, D), lambda i: (i, 0)),
        pl.BlockSpec((CT, LANES), lambda i: (i, 0)),
        pl.BlockSpec((1, N_MOD, D), lambda i: (_cond_index(i, CT), 0, 0)),
    ]
    scratch = [pltpu.VMEM((TOP_K, CT, D), F32), pltpu.SemaphoreType.DMA]
    if not final:
        return pl.pallas_call(
            _combine_kernel,
            grid=(T // CT,),
            in_specs=in_specs,
            out_specs=pl.BlockSpec((CT, D), lambda i: (i, 0)),
            out_shape=jax.ShapeDtypeStruct((T, D), F32),
            scratch_shapes=scratch,
            name="moe_combine",
        )(pos, ys, x, wcol, mod)
    return pl.pallas_call(
        _combine_final_kernel,
        grid=(T // CT,),
        in_specs=in_specs,
        out_specs=[
            pl.BlockSpec((CT, D), lambda i: (jnp.minimum(i, n_ctx - 1), 0)),
            pl.BlockSpec((CT, D), lambda i: (jnp.maximum(i - n_ctx, 0), 0)),
        ],
        out_shape=[jax.ShapeDtypeStruct((T_CTX, D), F32), jax.ShapeDtypeStruct((T_LAT, D), F32)],
        scratch_shapes=scratch,
        name="moe_combine_final",
    )(pos, ys, x, wcol, mod)


def _moe(h, e, rank, cnt, wcol, x, mod, w_gate, w_up, w_down, final):
    pos, tile_expert, used = _plan(e, rank, cnt)
    xs = _dispatch(pos, h)
    ys = _experts(xs, tile_expert, used, w_gate, w_up, w_down)
    return _combine(pos, ys, x, wcol, mod, final)


def kernel(x_prompt, x_sample, c, cache_k, cache_v, c_ctx, norm_mix_g, norm_ffn_g, ada_w, ada_b, attn_w_qkv, attn_q_norm, attn_k_norm, attn_w_o, pool_w, pool_scale, moe_w_group, moe_w_expert, moe_w_gate, moe_w_up, moe_w_down):
    xp = x_prompt.reshape(T_CTX, D)
    xs = x_sample.reshape(T_LAT, D)
    cond = jnp.concatenate([c_ctx[None, :], c, jnp.zeros((N_COND - 1 - DEC_BATCH, D), F32)], axis=0)
    mod = _ada(cond, ada_w, ada_b)

    q, kt2, v2, new_k, new_v = _qkv(xp, xs, mod[0], norm_mix_g[0:1], attn_w_qkv[0].astype(BF16),
                                    attn_q_norm[0:1], attn_k_norm[0:1])
    o_ctx, o_lat = _attention(q, kt2, v2, cache_k[:, 0].reshape(DEC_BATCH, PAST, DKV),
                              cache_v[:, 0].reshape(DEC_BATCH, PAST, DKV))
    o = jnp.concatenate([o_ctx, o_lat], axis=0)
    wr_hl, wr_hi = _router_weights(moe_w_group[0], moe_w_expert[0])
    x1, h, e, rank, wcol, cnt = _post(o, xp, xs, mod[0], attn_w_o[0].astype(BF16), norm_ffn_g[0:1],
                                      wr_hl, wr_hi)
    x2 = _moe(h, e, rank, cnt, wcol, x1, mod[0], moe_w_gate[0], moe_w_up[0], moe_w_down[0], final=False)

    wr_hl, wr_hi = _router_weights(moe_w_group[1], moe_w_expert[1])
    x3, h, e, rank, wcol, cnt = _pool(x2, mod[1], norm_mix_g[1:2], pool_w[0].astype(BF16),
                                      pool_scale[0:1], norm_ffn_g[1:2], wr_hl, wr_hi)
    y_prompt, y_sample = _moe(h, e, rank, cnt, wcol, x3, mod[1], moe_w_gate[1], moe_w_up[1],
                              moe_w_down[1], final=True)

    return (y_prompt.reshape(BATCH, SEQ, D), y_sample.reshape(DEC_BATCH, DEC_SEQ, D),
            new_k.reshape(BATCH, 1, SEQ, N_KV, HD), new_v.reshape(BATCH, 1, SEQ, N_KV, HD))
```

```python
import functools

import jax
import jax.numpy as jnp
from jax import lax
from jax.experimental import pallas as pl
from jax.experimental.pallas import tpu as pltpu
from jax.experimental.pallas import tpu_sc as plsc

F32 = jnp.float32
BF16 = jnp.bfloat16
I32 = jnp.int32
U32 = jnp.uint32

D = 1024
BATCH, SEQ = 32, 256
DEC_BATCH, DEC_SEQ, PAST = 8, 1024, 512
T_CTX = BATCH * SEQ
T_LAT = DEC_BATCH * DEC_SEQ
T = T_CTX + T_LAT
GRID_W = 64
N_HEADS, N_KV, HD = 16, 4, 64
DQ = N_HEADS * HD
DKV = N_KV * HD
ROPE_THETA = 10000.0
POOL_WINDOWS = (2, 4, 8, 16)
PGD = D // len(POOL_WINDOWS)
N_GROUPS, E_PER_G, TOP_K = 4, 8, 2
NE = N_GROUPS * E_PER_G
DE = D // 4
N_MOD = 6
EPS = 1e-6
N_COND = 16

TILE = 512
NT = T // TILE
NT_CTX = T_CTX // TILE
PTILE = 1024
TM = 256
N_ROW_TILES = (TOP_K * T) // TM + NE
NPAD = N_ROW_TILES * TM
CT = 512
LANES = 128
NEG_INF = float("-inf")
HALF = D // 2
SC_CORES, SC_SUBCORES = 2, 16
SC_WORKERS = SC_CORES * SC_SUBCORES
SC_WIN = 64


def _vmem_limit(mib):
    return pltpu.CompilerParams(vmem_limit_bytes=mib * 1024 * 1024)


def _modulate(x, gain_scale, shift):
    ms = jnp.mean(x * x, axis=-1, keepdims=True)
    return (x * lax.rsqrt(ms + EPS)) * gain_scale + shift


def _head_norm(z, gain, ind, ind_t):
    ss = jnp.dot((z * z).astype(BF16), ind, preferred_element_type=F32)
    inv = lax.rsqrt(ss * (1.0 / HD) + EPS)
    inv_hi = inv.astype(BF16)
    inv_lo = (inv - inv_hi.astype(F32)).astype(BF16)
    scale = (jnp.dot(inv_hi, ind_t, preferred_element_type=F32)
             + jnp.dot(inv_lo, ind_t, preferred_element_type=F32))
    return z * scale * gain


def _rope(z, cos_t, sin_t):
    lane = lax.broadcasted_iota(I32, (z.shape[0], LANES), 1)
    low = (lane % 32) < 16
    outs = []
    for c in range(z.shape[1] // LANES):
        zc = z[:, c * LANES:(c + 1) * LANES]
        up = pltpu.roll(zc, 16, axis=1)
        dn = pltpu.roll(zc, LANES - 16, axis=1)
        outs.append(zc * cos_t + jnp.where(low, dn, up) * sin_t)
    return jnp.concatenate(outs, axis=1)


def _pack_rows(z):
    bits = lax.bitcast_convert_type(z.astype(BF16).astype(F32), U32)
    return bits[:, :HALF] | (bits[:, HALF:] >> 16)


def _unpack_rows(p):
    hi = lax.bitcast_convert_type(p & jnp.uint32(0xFFFF0000), F32)
    lo = lax.bitcast_convert_type(p << 16, F32)
    return hi, lo


def _ada_kernel(c_ref, w_ref, b_ref, o_ref):
    c = c_ref[...]
    a = c * (1.0 / (1.0 + jnp.exp(-c)))
    o_ref[0] = jnp.dot(a.astype(BF16), w_ref[0].astype(BF16), preferred_element_type=F32) + b_ref[0]


def _ada(cond, ada_w, ada_b):
    nb = 1536
    depth = ada_w.shape[0]
    out = pl.pallas_call(
        _ada_kernel,
        grid=(depth, (N_MOD * D) // nb),
        in_specs=[
            pl.BlockSpec((N_COND, D), lambda l, j: (0, 0)),
            pl.BlockSpec((1, D, nb), lambda l, j: (l, 0, j)),
            pl.BlockSpec((1, 1, nb), lambda l, j: (l, 0, j)),
        ],
        out_specs=pl.BlockSpec((1, N_COND, nb), lambda l, j: (l, 0, j)),
        out_shape=jax.ShapeDtypeStruct((depth, N_COND, N_MOD * D), F32),
        compiler_params=_vmem_limit(40),
        name="ada",
    )(cond, ada_w, ada_b.reshape(depth, 1, N_MOD * D))
    return out.reshape(depth, N_COND, N_MOD, D)


def _cond_index(i, tile):
    n_ctx = T_CTX // tile
    per_batch = DEC_SEQ // tile
    return jnp.where(i < n_ctx, 0, 1 + (i - n_ctx) // per_batch)


def _qkv_kernel(xp_ref, xs_ref, mod_ref, g_ref, w_ref, qg_ref, kg_ref, indq_ref, indqt_ref,
                indk_ref, indkt_ref, dup_ref, dupt_ref, cos_ref, sin_ref,
                q_ref, kt_ref, v2_ref, nk_ref, nv_ref):
    i = pl.program_id(0)
    is_lat = i >= NT_CTX
    x = jnp.where(is_lat, xs_ref[...], xp_ref[...])
    h = _modulate(x, g_ref[...] * (1.0 + mod_ref[0, 1:2, :]), mod_ref[0, 0:1, :])
    qkv = jnp.dot(h.astype(BF16), w_ref[...], preferred_element_type=F32)
    q = _head_norm(qkv[:, :DQ], qg_ref[...], indq_ref[...], indqt_ref[...])
    k = _head_norm(qkv[:, DQ:DQ + DKV], kg_ref[...], indk_ref[...], indkt_ref[...])
    v = qkv[:, DQ + DKV:]

    def store(qn, kn):
        q_ref[...] = (qn * (HD ** -0.5)).astype(BF16)
        kt_ref[...] = lax.dot_general(dupt_ref[...], kn.astype(BF16), (((1,), (1,)), ((), ())),
                                      preferred_element_type=F32).astype(BF16)
        v2_ref[...] = jnp.dot(v.astype(BF16), dup_ref[...], preferred_element_type=F32).astype(BF16)

    @pl.when(is_lat)
    def _():
        store(_rope(q, cos_ref[...], sin_ref[...]), _rope(k, cos_ref[...], sin_ref[...]))

    @pl.when(jnp.logical_not(is_lat))
    def _():
        store(q, k)
        nk_ref[...] = k
        nv_ref[...] = v


def _rope_tables():
    rows = DEC_SEQ // GRID_W
    row = jnp.broadcast_to(jnp.arange(rows, dtype=F32)[:, None], (rows, GRID_W)).reshape(-1)
    col = jnp.broadcast_to(jnp.arange(GRID_W, dtype=F32)[None, :], (rows, GRID_W)).reshape(-1)
    axis_dim = HD // 2
    inv_freq = jnp.power(ROPE_THETA, -jnp.arange(0, axis_dim, 2, dtype=F32) / axis_dim)
    ang = jnp.concatenate([row[:, None] * inv_freq, col[:, None] * inv_freq], axis=-1)
    cos, sin = jnp.cos(ang), jnp.sin(ang)
    quarter = HD // 4
    cos_h = jnp.concatenate([cos[:, :quarter], cos[:, :quarter], cos[:, quarter:], cos[:, quarter:]], axis=1)
    sin_h = jnp.concatenate([-sin[:, :quarter], sin[:, :quarter], -sin[:, quarter:], sin[:, quarter:]], axis=1)
    return jnp.tile(cos_h, (1, LANES // HD)), jnp.tile(sin_h, (1, LANES // HD))


def _head_indicators(width):
    col = jnp.arange(width)[:, None]
    head = jnp.arange(LANES)[None, :]
    ind = (col // HD == head).astype(BF16)
    return ind, ind.T


def _dup_matrix():
    src = jnp.arange(DKV)[:, None]
    dst = jnp.arange(2 * DKV)[None, :]
    return (src == HD * (dst // LANES) + dst % HD).astype(BF16)


def _qkv(xp, xs, mod, g, w_qkv, q_gain, k_gain):
    indq, indqt = _head_indicators(DQ)
    indk, indkt = _head_indicators(DKV)
    dup = _dup_matrix()
    cos_t, sin_t = _rope_tables()
    per_batch = DEC_SEQ // TILE
    const = lambda shape: pl.BlockSpec(shape, lambda i: (0,) * len(shape))
    return pl.pallas_call(
        _qkv_kernel,
        grid=(NT,),
        in_specs=[
            pl.BlockSpec((TILE, D), lambda i: (jnp.minimum(i, NT_CTX - 1), 0)),
            pl.BlockSpec((TILE, D), lambda i: (jnp.maximum(i - NT_CTX, 0), 0)),
            pl.BlockSpec((1, N_MOD, D), lambda i: (_cond_index(i, TILE), 0, 0)),
            const((1, D)),
            const((D, DQ + 2 * DKV)),
            const((1, DQ)), const((1, DKV)),
            const((DQ, LANES)), const((LANES, DQ)),
            const((DKV, LANES)), const((LANES, DKV)),
            const((DKV, 2 * DKV)), const((2 * DKV, DKV)),
            pl.BlockSpec((TILE, LANES), lambda i: (jnp.maximum(i - NT_CTX, 0) % per_batch, 0)),
            pl.BlockSpec((TILE, LANES), lambda i: (jnp.maximum(i - NT_CTX, 0) % per_batch, 0)),
        ],
        out_specs=[
            pl.BlockSpec((TILE, DQ), lambda i: (i, 0)),
            pl.BlockSpec((2 * DKV, TILE), lambda i: (0, i)),
            pl.BlockSpec((TILE, 2 * DKV), lambda i: (i, 0)),
            pl.BlockSpec((TILE, DKV), lambda i: (jnp.minimum(i, NT_CTX - 1), 0)),
            pl.BlockSpec((TILE, DKV), lambda i: (jnp.minimum(i, NT_CTX - 1), 0)),
        ],
        out_shape=[
            jax.ShapeDtypeStruct((T, DQ), BF16),
            jax.ShapeDtypeStruct((2 * DKV, T), BF16),
            jax.ShapeDtypeStruct((T, 2 * DKV), BF16),
            jax.ShapeDtypeStruct((T_CTX, DKV), F32),
            jax.ShapeDtypeStruct((T_CTX, DKV), F32),
        ],
        compiler_params=_vmem_limit(56),
        name="qkv",
    )(xp, xs, mod, g, w_qkv, jnp.tile(q_gain, (1, N_HEADS)), jnp.tile(k_gain, (1, N_KV)),
      indq, indqt, indk, indkt, dup, dup.T, cos_t, sin_t)


def _attn_pair(qp, kts, vlos, vhis):
    lq = qp.shape[0]
    lane = lax.broadcasted_iota(I32, qp.shape, 1)
    zero = jnp.zeros_like(qp)
    q2 = jnp.concatenate([jnp.where(lane < HD, qp, zero), jnp.where(lane >= HD, qp, zero)], axis=0)
    ss = [jnp.dot(q2, kt, preferred_element_type=F32) for kt in kts]
    m = ss[0].max(axis=-1, keepdims=True)
    for s in ss[1:]:
        m = jnp.maximum(m, s.max(axis=-1, keepdims=True))
    acc = None
    l = None
    for s, vlo, vhi in zip(ss, vlos, vhis):
        e = jnp.exp(s - m)
        ls = e.sum(axis=-1, keepdims=True)
        l = ls if l is None else l + ls
        eb = e.astype(BF16)
        t = (jnp.dot(eb[:lq], vlo, preferred_element_type=F32)
             + jnp.dot(eb[lq:], vhi, preferred_element_type=F32))
        acc = t if acc is None else acc + t
    linv = 1.0 / l
    return acc * jnp.where(lane < HD, linv[:lq], linv[lq:])


def _split_values(vd):
    lane = lax.broadcasted_iota(I32, vd.shape, 1)
    zero = jnp.zeros_like(vd)
    return jnp.where(lane < HD, vd, zero), jnp.where(lane >= HD, vd, zero)


def _attn_ctx_kernel(q_ref, kt_ref, v_ref, o_ref):
    for g in range(N_KV):
        kt = kt_ref[g * LANES:(g + 1) * LANES, :]
        vlo, vhi = _split_values(v_ref[:, g * LANES:(g + 1) * LANES])
        for p in range(2):
            c = (2 * g + p) * LANES
            o_ref[:, c:c + LANES] = _attn_pair(q_ref[:, c:c + LANES], [kt], [vlo], [vhi]).astype(BF16)


def _attn_lat_kernel(q_ref, ktc_ref, vc_ref, ktn_ref, vn_ref, o_in_ref, o_ref):
    del o_in_ref
    for g in range(N_KV):
        rows = slice(g * LANES, (g + 1) * LANES)
        kts = [ktc_ref[0, rows, :], ktn_ref[rows, :]]
        vc_lo, vc_hi = _split_values(vc_ref[0, :, rows])
        vn_lo, vn_hi = _split_values(vn_ref[:, rows])
        for p in range(2):
            c = (2 * g + p) * LANES
            o_ref[:, c:c + LANES] = _attn_pair(q_ref[:, c:c + LANES], kts, [vc_lo, vn_lo],
                                               [vc_hi, vn_hi]).astype(BF16)


def _cache_kernel(ck_ref, cv_ref, dup_ref, dupt_ref, kt_ref, v2_ref):
    kt_ref[0] = lax.dot_general(dupt_ref[...], ck_ref[0].astype(BF16), (((1,), (1,)), ((), ())),
                                preferred_element_type=F32).astype(BF16)
    v2_ref[0] = jnp.dot(cv_ref[0].astype(BF16), dup_ref[...], preferred_element_type=F32).astype(BF16)


def _attention(q, kt2, v2, cache_k, cache_v):
    dup = _dup_matrix()
    ktc, vc2 = pl.pallas_call(
        _cache_kernel,
        grid=(DEC_BATCH,),
        in_specs=[
            pl.BlockSpec((1, PAST, DKV), lambda b: (b, 0, 0)),
            pl.BlockSpec((1, PAST, DKV), lambda b: (b, 0, 0)),
            pl.BlockSpec((DKV, 2 * DKV), lambda b: (0, 0)),
            pl.BlockSpec((2 * DKV, DKV), lambda b: (0, 0)),
        ],
        out_specs=[
            pl.BlockSpec((1, 2 * DKV, PAST), lambda b: (b, 0, 0)),
            pl.BlockSpec((1, PAST, 2 * DKV), lambda b: (b, 0, 0)),
        ],
        out_shape=[
            jax.ShapeDtypeStruct((DEC_BATCH, 2 * DKV, PAST), BF16),
            jax.ShapeDtypeStruct((DEC_BATCH, PAST, 2 * DKV), BF16),
        ],
        name="cache_prep",
    )(cache_k, cache_v, dup, dup.T)

    o_ctx = pl.pallas_call(
        _attn_ctx_kernel,
        grid=(BATCH,),
        in_specs=[
            pl.BlockSpec((SEQ, DQ), lambda b: (b, 0)),
            pl.BlockSpec((2 * DKV, SEQ), lambda b: (0, b)),
            pl.BlockSpec((SEQ, 2 * DKV), lambda b: (b, 0)),
        ],
        out_specs=pl.BlockSpec((SEQ, DQ), lambda b: (b, 0)),
        out_shape=jax.ShapeDtypeStruct((T, DQ), BF16),
        name="attn_ctx",
    )(q, kt2, v2)

    qb = 256
    nqb = DEC_SEQ // qb
    lat0 = T_CTX // qb
    return pl.pallas_call(
        _attn_lat_kernel,
        grid=(DEC_BATCH, nqb),
        in_specs=[
            pl.BlockSpec((qb, DQ), lambda b, j: (lat0 + b * nqb + j, 0)),
            pl.BlockSpec((1, 2 * DKV, PAST), lambda b, j: (b, 0, 0)),
            pl.BlockSpec((1, PAST, 2 * DKV), lambda b, j: (b, 0, 0)),
            pl.BlockSpec((2 * DKV, DEC_SEQ), lambda b, j: (0, T_CTX // DEC_SEQ + b)),
            pl.BlockSpec((DEC_SEQ, 2 * DKV), lambda b, j: (T_CTX // DEC_SEQ + b, 0)),
            pl.BlockSpec(memory_space=pl.ANY),
        ],
        out_specs=pl.BlockSpec((qb, DQ), lambda b, j: (lat0 + b * nqb + j, 0)),
        out_shape=jax.ShapeDtypeStruct((T, DQ), BF16),
        input_output_aliases={5: 0},
        compiler_params=_vmem_limit(48),
        name="attn_lat",
    )(q, ktc, vc2, kt2, v2, o_ctx)


def _route(h, wr_hl_ref, wr_hi_ref, tri_ref, cnt_ref, e_ref, rank_ref, wcol_ref, cnt_out_ref):
    tl = h.shape[0]
    h_hi = h.astype(BF16)
    h_lo = (h - h_hi.astype(F32)).astype(BF16)
    lg2 = jnp.dot(h_hi, wr_hl_ref[...], preferred_element_type=F32)
    lg = lg2[:, :LANES] + lg2[:, LANES:] + jnp.dot(h_lo, wr_hi_ref[...], preferred_element_type=F32)
    lt = lg.T

    sub8 = lax.broadcasted_iota(I32, (8, tl), 0).astype(F32)
    gl = jnp.where(sub8 < N_GROUPS, lt[0:8], NEG_INF)
    gmax = gl.max(axis=0, keepdims=True)
    gidx = jnp.min(jnp.where(gl == gmax, sub8, 8.0), axis=0, keepdims=True)
    gsum = jnp.sum(jnp.exp(gl - gmax), axis=0, keepdims=True)
    sel = lt[8:16]
    for g in range(1, N_GROUPS):
        sel = jnp.where(gidx == g, lt[8 + 8 * g:16 + 8 * g], sel)
    t1 = sel.max(axis=0, keepdims=True)
    j1 = jnp.min(jnp.where(sel == t1, sub8, 8.0), axis=0, keepdims=True)
    sel2 = jnp.where(sub8 == j1, NEG_INF, sel)
    t2 = sel2.max(axis=0, keepdims=True)
    j2 = jnp.min(jnp.where(sel2 == t2, sub8, 8.0), axis=0, keepdims=True)
    ex = jnp.exp(t2 - t1)
    den = 1.0 + ex
    gw = 1.0 / gsum
    w0 = gw * (1.0 / den)
    w1 = gw * (ex / den)
    e0 = gidx * E_PER_G + j1
    e1 = gidx * E_PER_G + j2

    sub_e = lax.broadcasted_iota(I32, (NE, tl), 0).astype(F32)
    oh0 = (sub_e == e0).astype(F32)
    oh1 = (sub_e == e1).astype(F32)
    c = oh0 + oh1
    carry = cnt_ref[:, 0:1]
    pieces = []
    for ch in range(tl // 256):
        cc = c[:, ch * 256:(ch + 1) * 256]
        pieces.append(jnp.dot(cc.astype(BF16), tri_ref[...], preferred_element_type=F32) + carry)
        carry = carry + jnp.sum(cc, axis=1, keepdims=True)
    csum = jnp.concatenate(pieces, axis=1)
    cnt_ref[...] = jnp.broadcast_to(carry, (NE, LANES))
    cnt_out_ref[...] = jnp.broadcast_to(carry, (NE, LANES))
    r0 = jnp.sum(oh0 * csum, axis=0, keepdims=True) - 1.0
    r1 = jnp.sum(oh1 * csum, axis=0, keepdims=True) - 1.0
    e_ref[0:1, :] = e0.astype(I32)
    e_ref[1:2, :] = e1.astype(I32)
    rank_ref[0:1, :] = r0.astype(I32)
    rank_ref[1:2, :] = r1.astype(I32)
    sub_w = lax.broadcasted_iota(I32, (LANES, tl), 0)
    w8 = jnp.where(sub_w == 0, w0, jnp.where(sub_w == 1, w1, 0.0))
    wcol_ref[...] = w8.T


def _post_kernel(o_ref, xp_ref, xs_ref, mod_ref, wo_ref, g_ref, wr_hl_ref, wr_hi_ref, tri_ref,
                 x1_ref, h_ref, e_ref, rank_ref, wcol_ref, cnt_out_ref, cnt_ref):
    i = pl.program_id(0)

    @pl.when(i == 0)
    def _():
        cnt_ref[...] = jnp.zeros_like(cnt_ref)

    x = jnp.where(i >= NT_CTX, xs_ref[...], xp_ref[...])
    m = jnp.dot(o_ref[...], wo_ref[...], preferred_element_type=F32)
    x1 = x + mod_ref[0, 2:3, :] * m
    x1_ref[...] = x1
    h = _modulate(x1, g_ref[...] * (1.0 + mod_ref[0, 4:5, :]), mod_ref[0, 3:4, :])
    h_ref[...] = _pack_rows(h)
    _route(h, wr_hl_ref, wr_hi_ref, tri_ref, cnt_ref, e_ref, rank_ref, wcol_ref, cnt_out_ref)


def _router_weights(w_group, w_expert):
    w = jnp.zeros((D, LANES), F32)
    w = w.at[:, :N_GROUPS].set(w_group)
    w = w.at[:, 8:8 + NE].set(jnp.transpose(w_expert, (1, 0, 2)).reshape(D, NE))
    w_hi = w.astype(BF16)
    w_lo = (w - w_hi.astype(F32)).astype(BF16)
    return jnp.concatenate([w_hi, w_lo], axis=1), w_hi


def _route_outs(tile):
    specs = [
        pl.BlockSpec((tile, D), lambda i: (i, 0)),
        pl.BlockSpec((tile, HALF), lambda i: (i, 0)),
        pl.BlockSpec((TOP_K, tile), lambda i: (0, i)),
        pl.BlockSpec((TOP_K, tile), lambda i: (0, i)),
        pl.BlockSpec((tile, LANES), lambda i: (i, 0)),
        pl.BlockSpec((NE, LANES), lambda i: (0, 0)),
    ]
    shapes = [
        jax.ShapeDtypeStruct((T, D), F32),
        jax.ShapeDtypeStruct((T, HALF), U32),
        jax.ShapeDtypeStruct((TOP_K, T), I32),
        jax.ShapeDtypeStruct((TOP_K, T), I32),
        jax.ShapeDtypeStruct((T, LANES), F32),
        jax.ShapeDtypeStruct((NE, LANES), F32),
    ]
    return specs, shapes


def _tri():
    a = jnp.arange(256)
    return (a[:, None] <= a[None, :]).astype(BF16)


def _post(o, xp, xs, mod, w_o, g_ffn, wr_hl, wr_hi):
    const = lambda shape: pl.BlockSpec(shape, lambda i: (0,) * len(shape))
    out_specs, out_shape = _route_outs(TILE)
    return pl.pallas_call(
        _post_kernel,
        grid=(NT,),
        in_specs=[
            pl.BlockSpec((TILE, DQ), lambda i: (i, 0)),
            pl.BlockSpec((TILE, D), lambda i: (jnp.minimum(i, NT_CTX - 1), 0)),
            pl.BlockSpec((TILE, D), lambda i: (jnp.maximum(i - NT_CTX, 0), 0)),
            pl.BlockSpec((1, N_MOD, D), lambda i: (_cond_index(i, TILE), 0, 0)),
            const((DQ, D)), const((1, D)), const((D, 2 * LANES)), const((D, LANES)), const((256, 256)),
        ],
        out_specs=out_specs,
        out_shape=out_shape,
        scratch_shapes=[pltpu.VMEM((NE, LANES), F32)],
        compiler_params=_vmem_limit(56),
        name="post_attn",
    )(o, xp, xs, mod, w_o, g_ffn, wr_hl, wr_hi, _tri())


def _pool_kernel(x_ref, mod_ref, g_ref, pw_ref, ps_ref, gf_ref, wr_hl_ref, wr_hi_ref, tri_ref,
                 x1_ref, h_ref, e_ref, rank_ref, wcol_ref, cnt_out_ref, cnt_ref):
    i = pl.program_id(0)

    @pl.when(i == 0)
    def _():
        cnt_ref[...] = jnp.zeros_like(cnt_ref)

    x = x_ref[...]
    h = _modulate(x, g_ref[...] * (1.0 + mod_ref[0, 1:2, :]), mod_ref[0, 0:1, :])
    seq = jnp.where(i >= T_CTX // PTILE, DEC_SEQ, SEQ)
    row = lax.broadcasted_iota(I32, (PTILE, PGD), 0)
    t = row & (seq - 1)

    def back(z, s):
        return jnp.where(t >= s, pltpu.roll(z, s, axis=0), 0.0)

    def fwd(z, s):
        return jnp.where(t + s < seq, pltpu.roll(z, PTILE - s, axis=0), 0.0)

    outs = []
    for gi, win in enumerate(POOL_WINDOWS):
        hg = h[:, gi * PGD:(gi + 1) * PGD]
        left = win // 2
        right = win - left - 1
        p = hg + back(hg, 1)
        width = 2
        while 2 * width <= left:
            p = p + back(p, width)
            width *= 2
        if left + 1 > width:
            p = p + back(hg, width)
        tot = p
        if right > 0:
            f = fwd(hg, 1)
            fw = 1
            parts = {1: f}
            while 2 * fw <= right:
                f = f + fwd(f, fw)
                fw *= 2
                parts[fw] = f
            lead = f
            done = fw
            rem = right - done
            while rem > 0:
                step = max(w_ for w_ in parts if w_ <= rem)
                lead = lead + fwd(parts[step], done)
                done += step
                rem -= step
            tot = tot + lead
        lo = jnp.maximum(t - left, 0)
        hi = jnp.minimum(t + right + 1, seq)
        diff = tot / (hi - lo).astype(F32) - hg
        outs.append(jnp.dot(diff.astype(BF16), pw_ref[gi], preferred_element_type=F32))
    m = jnp.concatenate(outs, axis=1) * ps_ref[...]
    x1 = x + mod_ref[0, 2:3, :] * m
    x1_ref[...] = x1
    hf = _modulate(x1, gf_ref[...] * (1.0 + mod_ref[0, 4:5, :]), mod_ref[0, 3:4, :])
    h_ref[...] = _pack_rows(hf)
    _route(hf, wr_hl_ref, wr_hi_ref, tri_ref, cnt_ref, e_ref, rank_ref, wcol_ref, cnt_out_ref)


def _pool(x, mod, g_mix, pool_w, pool_scale, g_ffn, wr_hl, wr_hi):
    const = lambda shape: pl.BlockSpec(shape, lambda i: (0,) * len(shape))
    out_specs, out_shape = _route_outs(PTILE)
    return pl.pallas_call(
        _pool_kernel,
        grid=(T // PTILE,),
        in_specs=[
            pl.BlockSpec((PTILE, D), lambda i: (i, 0)),
            pl.BlockSpec((1, N_MOD, D), lambda i: (_cond_index(i, PTILE), 0, 0)),
            const((1, D)), const((len(POOL_WINDOWS), PGD, PGD)), const((1, D)), const((1, D)),
            const((D, 2 * LANES)), const((D, LANES)), const((256, 256)),
        ],
        out_specs=out_specs,
        out_shape=out_shape,
        scratch_shapes=[pltpu.VMEM((NE, LANES), F32)],
        compiler_params=_vmem_limit(56),
        name="pool_mixer",
    )(x, mod, g_mix, pool_w, pool_scale, g_ffn, wr_hl, wr_hi, _tri())


def _plan(e, rank, cnt):
    counts = cnt[:, 0].astype(I32)
    n_tiles = (counts + TM - 1) // TM
    tile_end = jnp.cumsum(n_tiles)
    base = (tile_end - n_tiles) * TM
    pos = jnp.sum(jnp.where(e[None] == jnp.arange(NE, dtype=I32)[:, None, None], base[:, None, None], 0),
                  axis=0) + rank
    used = tile_end[-1]
    tile_id = jnp.minimum(jnp.arange(N_ROW_TILES, dtype=I32), used - 1)
    tile_expert = jnp.sum((tile_id[:, None] >= tile_end[None, :]).astype(I32), axis=1)
    return pos, tile_expert, used.reshape(1)


def _sc_mesh():
    return plsc.VectorSubcoreMesh(core_axis_name="core", subcore_axis_name="subcore")


def _sc_worker():
    return lax.axis_index("core") * SC_SUBCORES + lax.axis_index("subcore")


def _sc_dispatch(h, pos):
    per = T // SC_WORKERS
    nwin = per // SC_WIN

    @functools.partial(
        pl.kernel, out_type=jax.ShapeDtypeStruct((NPAD, HALF), U32), mesh=_sc_mesh(),
        scratch_types=[pltpu.VMEM((2, SC_WIN, HALF), U32), pltpu.VMEM((2, SC_WIN), I32),
                       pltpu.VMEM((2, SC_WIN), I32), pltpu.SemaphoreType.DMA((2,)),
                       pltpu.SemaphoreType.DMA((2,))])
    def run(h_hbm, pos_hbm, xs_hbm, buf, idx0, idx1, sem_in, sem_out):
        first = _sc_worker() * per

        def loads(j, s):
            base = first + j * SC_WIN
            return (pltpu.make_async_copy(h_hbm.at[pl.ds(base, SC_WIN)], buf.at[s], sem_in.at[s]),
                    pltpu.make_async_copy(pos_hbm.at[pl.ds(base, SC_WIN)], idx0.at[s], sem_in.at[s]),
                    pltpu.make_async_copy(pos_hbm.at[pl.ds(T + base, SC_WIN)], idx1.at[s], sem_in.at[s]))

        def stores(s):
            return (pltpu.make_async_copy(buf.at[s], xs_hbm.at[idx0.at[s]], sem_out.at[s]),
                    pltpu.make_async_copy(buf.at[s], xs_hbm.at[idx1.at[s]], sem_out.at[s]))

        for c in loads(0, 0):
            c.start()

        @pl.loop(0, nwin // 2)
        def _(jj):
            for s in range(2):
                j = jj * 2 + s
                for c in loads(j, s):
                    c.wait()
                for c in stores(s):
                    c.start()

                @pl.when(j >= 1)
                def _():
                    for c in stores(1 - s):
                        c.wait()

                @pl.when(j + 1 < nwin)
                def _():
                    for c in loads(j + 1, 1 - s):
                        c.start()

        for c in stores((nwin - 1) % 2):
            c.wait()

    return run(h, pos.reshape(TOP_K * T))


def _sc_gather(ys, pos):
    n = TOP_K * T
    per = n // SC_WORKERS
    nwin = per // SC_WIN

    @functools.partial(
        pl.kernel, out_type=jax.ShapeDtypeStruct((n, HALF), U32), mesh=_sc_mesh(),
        scratch_types=[pltpu.VMEM((2, SC_WIN, HALF), U32), pltpu.VMEM((per,), I32),
                       pltpu.SemaphoreType.DMA((2,)), pltpu.SemaphoreType.DMA((2,))])
    def run(ys_hbm, pos_hbm, out_hbm, buf, idx, sem_in, sem_out):
        first = _sc_worker() * per
        pltpu.sync_copy(pos_hbm.at[pl.ds(first, per)], idx)

        def load(j, s):
            return pltpu.make_async_copy(ys_hbm.at[idx.at[pl.ds(j * SC_WIN, SC_WIN)]], buf.at[s], sem_in.at[s])

        def store(j, s):
            return pltpu.make_async_copy(buf.at[s], out_hbm.at[pl.ds(first + j * SC_WIN, SC_WIN)], sem_out.at[s])

        load(0, 0).start()

        @pl.loop(0, nwin // 2)
        def _(jj):
            for s in range(2):
                j = jj * 2 + s
                load(j, s).wait()
                store(j, s).start()

                @pl.when(j >= 1)
                def _():
                    store(j - 1, 1 - s).wait()

                @pl.when(j + 1 < nwin)
                def _():
                    load(j + 1, 1 - s).start()

        store(nwin - 1, (nwin - 1) % 2).wait()

    return run(ys, pos.reshape(n))


def _moe_kernel(te_ref, used_ref, x_ref, wg_ref, wu_ref, wd_ref, y_ref, wg_s, wu_s, wd_s):
    i = pl.program_id(0)

    @pl.when(i < used_ref[0])
    def _():
        prev = te_ref[jnp.maximum(i - 1, 0)]

        @pl.when(jnp.logical_or(i == 0, te_ref[i] != prev))
        def _():
            wg_s[...] = wg_ref[0, 0].astype(BF16)
            wu_s[...] = wu_ref[0, 0].astype(BF16)
            wd_s[...] = wd_ref[0, 0].astype(BF16)

        hi, lo = _unpack_rows(x_ref[...])
        x = jnp.concatenate([hi, lo], axis=1).astype(BF16)
        a = jnp.dot(x, wg_s[...], preferred_element_type=F32)
        b = jnp.dot(x, wu_s[...], preferred_element_type=F32)
        hid = a * (1.0 / (1.0 + jnp.exp(-a))) * b
        y_ref[...] = _pack_rows(jnp.dot(hid.astype(BF16), wd_s[...], preferred_element_type=F32))


def _experts(xs, tile_expert, used, layer, w_gate, w_up, w_down):
    row = lambda i, te, used: (jnp.minimum(i, used[0] - 1), 0)
    grid_spec = pltpu.PrefetchScalarGridSpec(
        num_scalar_prefetch=2,
        grid=(N_ROW_TILES,),
        in_specs=[
            pl.BlockSpec((TM, HALF), row),
            pl.BlockSpec((1, 1, D, DE), lambda i, te, used: (layer, te[i], 0, 0)),
            pl.BlockSpec((1, 1, D, DE), lambda i, te, used: (layer, te[i], 0, 0)),
            pl.BlockSpec((1, 1, DE, D), lambda i, te, used: (layer, te[i], 0, 0)),
        ],
        out_specs=pl.BlockSpec((TM, HALF), row),
        scratch_shapes=[pltpu.VMEM((D, DE), BF16), pltpu.VMEM((D, DE), BF16), pltpu.VMEM((DE, D), BF16)],
    )
    return pl.pallas_call(
        _moe_kernel,
        grid_spec=grid_spec,
        out_shape=jax.ShapeDtypeStruct((NPAD, HALF), U32),
        compiler_params=_vmem_limit(40),
        name="moe_experts",
    )(tile_expert, used, xs, w_gate, w_up, w_down)


def _combined(y0_ref, y1_ref, x_ref, wcol_ref, mod_ref):
    w0 = wcol_ref[:, 0:1]
    w1 = wcol_ref[:, 1:2]
    hi0, lo0 = _unpack_rows(y0_ref[...])
    hi1, lo1 = _unpack_rows(y1_ref[...])
    y = jnp.concatenate([w0 * hi0 + w1 * hi1, w0 * lo0 + w1 * lo1], axis=1)
    return x_ref[...] + mod_ref[0, 5:6, :] * y


def _combine_kernel(y0_ref, y1_ref, x_ref, wcol_ref, mod_ref, o_ref):
    o_ref[...] = _combined(y0_ref, y1_ref, x_ref, wcol_ref, mod_ref)


def _combine_final_kernel(y0_ref, y1_ref, x_ref, wcol_ref, mod_ref, op_ref, os_ref):
    i = pl.program_id(0)
    out = _combined(y0_ref, y1_ref, x_ref, wcol_ref, mod_ref)

    @pl.when(i < T_CTX // CT)
    def _():
        op_ref[...] = out

    @pl.when(i >= T_CTX // CT)
    def _():
        os_ref[...] = out


def _combine(y2, x, wcol, mod, final):
    n_ctx = T_CTX // CT
    in_specs = [
        pl.BlockSpec((CT, HALF), lambda i: (i, 0)),
        pl.BlockSpec((CT, HALF), lambda i: (T // CT + i, 0)),
        pl.BlockSpec((CT, D), lambda i: (i, 0)),
        pl.BlockSpec((CT, LANES), lambda i: (i, 0)),
        pl.BlockSpec((1, N_MOD, D), lambda i: (_cond_index(i, CT), 0, 0)),
    ]
    if not final:
        return pl.pallas_call(
            _combine_kernel,
            grid=(T // CT,),
            in_specs=in_specs,
            out_specs=pl.BlockSpec((CT, D), lambda i: (i, 0)),
            out_shape=jax.ShapeDtypeStruct((T, D), F32),
            name="moe_combine",
        )(y2, y2, x, wcol, mod)
    return pl.pallas_call(
        _combine_final_kernel,
        grid=(T // CT,),
        in_specs=in_specs,
        out_specs=[
            pl.BlockSpec((CT, D), lambda i: (jnp.minimum(i, n_ctx - 1), 0)),
            pl.BlockSpec((CT, D), lambda i: (jnp.maximum(i - n_ctx, 0), 0)),
        ],
        out_shape=[jax.ShapeDtypeStruct((T_CTX, D), F32), jax.ShapeDtypeStruct((T_LAT, D), F32)],
        name="moe_combine_final",
    )(y2, y2, x, wcol, mod)


def _moe(h, e, rank, cnt, wcol, x, mod, layer, w_gate, w_up, w_down, final):
    pos, tile_expert, used = _plan(e, rank, cnt)
    xs = _sc_dispatch(h, pos)
    ys = _experts(xs, tile_expert, used, layer, w_gate, w_up, w_down)
    y2 = _sc_gather(ys, pos)
    return _combine(y2, x, wcol, mod, final)


def kernel(x_prompt, x_sample, c, cache_k, cache_v, c_ctx, norm_mix_g, norm_ffn_g, ada_w, ada_b, attn_w_qkv, attn_q_norm, attn_k_norm, attn_w_o, pool_w, pool_scale, moe_w_group, moe_w_expert, moe_w_gate, moe_w_up, moe_w_down):
    xp = x_prompt.reshape(T_CTX, D)
    xs = x_sample.reshape(T_LAT, D)
    cond = jnp.concatenate([c_ctx[None, :], c, jnp.zeros((N_COND - 1 - DEC_BATCH, D), F32)], axis=0)
    mod = _ada(cond, ada_w, ada_b)

    q, kt2, v2, new_k, new_v = _qkv(xp, xs, mod[0], norm_mix_g[0:1], attn_w_qkv[0].astype(BF16),
                                    attn_q_norm[0:1], attn_k_norm[0:1])
    o = _attention(q, kt2, v2, cache_k[:, 0].reshape(DEC_BATCH, PAST, DKV),
                   cache_v[:, 0].reshape(DEC_BATCH, PAST, DKV))
    wr_hl, wr_hi = _router_weights(moe_w_group[0], moe_w_expert[0])
    x1, h, e, rank, wcol, cnt = _post(o, xp, xs, mod[0], attn_w_o[0].astype(BF16), norm_ffn_g[0:1],
                                      wr_hl, wr_hi)
    x2 = _moe(h, e, rank, cnt, wcol, x1, mod[0], 0, moe_w_gate, moe_w_up, moe_w_down, final=False)

    wr_hl, wr_hi = _router_weights(moe_w_group[1], moe_w_expert[1])
    x3, h, e, rank, wcol, cnt = _pool(x2, mod[1], norm_mix_g[1:2], pool_w[0].astype(BF16),
                                      pool_scale[0:1], norm_ffn_g[1:2], wr_hl, wr_hi)
    y_prompt, y_sample = _moe(h, e, rank, cnt, wcol, x3, mod[1], 1, moe_w_gate, moe_w_up, moe_w_down,
                              final=True)

    return (y_prompt.reshape(BATCH, SEQ, D), y_sample.reshape(DEC_BATCH, DEC_SEQ, D),
            new_k.reshape(BATCH, 1, SEQ, N_KV, HD), new_v.reshape(BATCH, 1, SEQ, N_KV, HD))
```

```python
import functools

import jax
import jax.numpy as jnp
from jax import lax
from jax.experimental import pallas as pl
from jax.experimental.pallas import tpu as pltpu
from jax.experimental.pallas import tpu_sc as plsc

F32 = jnp.float32
BF16 = jnp.bfloat16
I32 = jnp.int32
U32 = jnp.uint32

D = 1024
BATCH, SEQ = 32, 256
DEC_BATCH, DEC_SEQ, PAST = 8, 1024, 512
T_CTX = BATCH * SEQ
T_LAT = DEC_BATCH * DEC_SEQ
T = T_CTX + T_LAT
GRID_W = 64
N_HEADS, N_KV, HD = 16, 4, 64
DQ = N_HEADS * HD
DKV = N_KV * HD
ROPE_THETA = 10000.0
POOL_WINDOWS = (2, 4, 8, 16)
PGD = D // len(POOL_WINDOWS)
N_GROUPS, E_PER_G, TOP_K = 4, 8, 2
NE = N_GROUPS * E_PER_G
DE = D // 4
N_MOD = 6
EPS = 1e-6
N_COND = 16

TILE = 512
NT = T // TILE
NT_CTX = T_CTX // TILE
PTILE = 1024
TM = 512
N_ROW_TILES = (TOP_K * T) // TM + NE
NPAD = N_ROW_TILES * TM
CT = 512
LANES = 128
NEG_INF = float("-inf")
HALF = D // 2
SC_CORES, SC_SUBCORES = 2, 16
SC_WORKERS = SC_CORES * SC_SUBCORES
SC_WIN = 64


def _vmem_limit(mib):
    return pltpu.CompilerParams(vmem_limit_bytes=mib * 1024 * 1024)


def _modulate(x, gain_scale, shift):
    ms = jnp.mean(x * x, axis=-1, keepdims=True)
    return (x * lax.rsqrt(ms + EPS)) * gain_scale + shift


def _head_norm(z, gain, ind, ind_t2):
    ss = jnp.dot((z * z).astype(BF16), ind, preferred_element_type=F32)
    inv = lax.rsqrt(ss * (1.0 / HD) + EPS)
    inv_hi = inv.astype(BF16)
    inv_lo = (inv - inv_hi.astype(F32)).astype(BF16)
    scale = jnp.dot(jnp.concatenate([inv_hi, inv_lo], axis=1), ind_t2, preferred_element_type=F32)
    return z * scale * gain


def _rope(z, cos_t, sin_t):
    lane = lax.broadcasted_iota(I32, (z.shape[0], LANES), 1)
    low = (lane % 32) < 16
    outs = []
    for c in range(z.shape[1] // LANES):
        zc = z[:, c * LANES:(c + 1) * LANES]
        up = pltpu.roll(zc, 16, axis=1)
        dn = pltpu.roll(zc, LANES - 16, axis=1)
        outs.append(zc * cos_t + jnp.where(low, dn, up) * sin_t)
    return jnp.concatenate(outs, axis=1)


def _pack_rows(z):
    bits = lax.bitcast_convert_type(z.astype(BF16).astype(F32), U32)
    return bits[:, :HALF] | (bits[:, HALF:] >> 16)


def _unpack_rows(p):
    hi = lax.bitcast_convert_type(p & jnp.uint32(0xFFFF0000), F32)
    lo = lax.bitcast_convert_type(p << 16, F32)
    return hi, lo


def _ada_kernel(c_ref, w_ref, b_ref, o_ref):
    c = c_ref[...]
    a = c * (1.0 / (1.0 + jnp.exp(-c)))
    o_ref[0] = jnp.dot(a.astype(BF16), w_ref[0].astype(BF16), preferred_element_type=F32) + b_ref[0]


def _ada(cond, ada_w, ada_b):
    nb = 1536
    depth = ada_w.shape[0]
    out = pl.pallas_call(
        _ada_kernel,
        grid=(depth, (N_MOD * D) // nb),
        in_specs=[
            pl.BlockSpec((N_COND, D), lambda l, j: (0, 0)),
            pl.BlockSpec((1, D, nb), lambda l, j: (l, 0, j)),
            pl.BlockSpec((1, 1, nb), lambda l, j: (l, 0, j)),
        ],
        out_specs=pl.BlockSpec((1, N_COND, nb), lambda l, j: (l, 0, j)),
        out_shape=jax.ShapeDtypeStruct((depth, N_COND, N_MOD * D), F32),
        compiler_params=_vmem_limit(40),
        name="ada",
    )(cond, ada_w, ada_b.reshape(depth, 1, N_MOD * D))
    return out.reshape(depth, N_COND, N_MOD, D)


def _cond_index(i, tile):
    n_ctx = T_CTX // tile
    per_batch = DEC_SEQ // tile
    return jnp.where(i < n_ctx, 0, 1 + (i - n_ctx) // per_batch)


def _qkv_kernel(xp_ref, xs_ref, mod_ref, g_ref, w_ref, qg_ref, kg_ref, indq_ref, indqt_ref,
                indk_ref, indkt_ref, cos_ref, sin_ref,
                q_ref, kt_ref, v2_ref, nk_ref, nv_ref):
    i = pl.program_id(0)
    is_lat = i >= NT_CTX
    x = jnp.where(is_lat, xs_ref[...], xp_ref[...])
    h = _modulate(x, g_ref[...] * (1.0 + mod_ref[0, 1:2, :]), mod_ref[0, 0:1, :])
    qkv = jnp.dot(h.astype(BF16), w_ref[...], preferred_element_type=F32)
    q = _head_norm(qkv[:, :DQ], qg_ref[...], indq_ref[...], indqt_ref[...])
    k = _head_norm(qkv[:, DQ:DQ + DKV], kg_ref[...], indk_ref[...], indkt_ref[...])
    v = qkv[:, DQ + DKV:]

    lane = lax.broadcasted_iota(I32, (TILE, LANES), 1)
    for p in range(N_KV // 2):
        blk = v[:, p * LANES:(p + 1) * LANES]
        swp = pltpu.roll(blk, HD, axis=1)
        v2_ref[:, (2 * p) * LANES:(2 * p + 1) * LANES] = jnp.where(lane < HD, blk, swp).astype(BF16)
        v2_ref[:, (2 * p + 1) * LANES:(2 * p + 2) * LANES] = jnp.where(lane < HD, swp, blk).astype(BF16)

    def store(qn, kn):
        q_ref[...] = (qn * (HD ** -0.5)).astype(BF16)
        knt = kn.T.astype(BF16)
        for g in range(N_KV):
            kt_ref[g * LANES:g * LANES + HD, :] = knt[g * HD:(g + 1) * HD]
            kt_ref[g * LANES + HD:(g + 1) * LANES, :] = knt[g * HD:(g + 1) * HD]

    @pl.when(is_lat)
    def _():
        store(_rope(q, cos_ref[...], sin_ref[...]), _rope(k, cos_ref[...], sin_ref[...]))

    @pl.when(jnp.logical_not(is_lat))
    def _():
        store(q, k)
        nk_ref[...] = k
        nv_ref[...] = v


def _rope_tables():
    rows = DEC_SEQ // GRID_W
    row = jnp.broadcast_to(jnp.arange(rows, dtype=F32)[:, None], (rows, GRID_W)).reshape(-1)
    col = jnp.broadcast_to(jnp.arange(GRID_W, dtype=F32)[None, :], (rows, GRID_W)).reshape(-1)
    axis_dim = HD // 2
    inv_freq = jnp.power(ROPE_THETA, -jnp.arange(0, axis_dim, 2, dtype=F32) / axis_dim)
    ang = jnp.concatenate([row[:, None] * inv_freq, col[:, None] * inv_freq], axis=-1)
    cos, sin = jnp.cos(ang), jnp.sin(ang)
    quarter = HD // 4
    cos_h = jnp.concatenate([cos[:, :quarter], cos[:, :quarter], cos[:, quarter:], cos[:, quarter:]], axis=1)
    sin_h = jnp.concatenate([-sin[:, :quarter], sin[:, :quarter], -sin[:, quarter:], sin[:, quarter:]], axis=1)
    return jnp.tile(cos_h, (1, LANES // HD)), jnp.tile(sin_h, (1, LANES // HD))


def _head_indicators(width):
    col = jnp.arange(width)[:, None]
    head = jnp.arange(LANES)[None, :]
    ind = (col // HD == head).astype(BF16)
    return ind, jnp.concatenate([ind.T, ind.T], axis=0)


def _dup_matrix():
    src = jnp.arange(DKV)[:, None]
    dst = jnp.arange(2 * DKV)[None, :]
    return (src == HD * (dst // LANES) + dst % HD).astype(BF16)


def _qkv(xp, xs, mod, g, w_qkv, q_gain, k_gain):
    indq, indqt = _head_indicators(DQ)
    indk, indkt = _head_indicators(DKV)
    cos_t, sin_t = _rope_tables()
    per_batch = DEC_SEQ // TILE
    const = lambda shape: pl.BlockSpec(shape, lambda i: (0,) * len(shape))
    return pl.pallas_call(
        _qkv_kernel,
        grid=(NT,),
        in_specs=[
            pl.BlockSpec((TILE, D), lambda i: (jnp.minimum(i, NT_CTX - 1), 0)),
            pl.BlockSpec((TILE, D), lambda i: (jnp.maximum(i - NT_CTX, 0), 0)),
            pl.BlockSpec((1, N_MOD, D), lambda i: (_cond_index(i, TILE), 0, 0)),
            const((1, D)),
            const((D, DQ + 2 * DKV)),
            const((1, DQ)), const((1, DKV)),
            const((DQ, LANES)), const((2 * LANES, DQ)),
            const((DKV, LANES)), const((2 * LANES, DKV)),
            pl.BlockSpec((TILE, LANES), lambda i: (jnp.maximum(i - NT_CTX, 0) % per_batch, 0)),
            pl.BlockSpec((TILE, LANES), lambda i: (jnp.maximum(i - NT_CTX, 0) % per_batch, 0)),
        ],
        out_specs=[
            pl.BlockSpec((TILE, DQ), lambda i: (i, 0)),
            pl.BlockSpec((2 * DKV, TILE), lambda i: (0, i)),
            pl.BlockSpec((TILE, 2 * DKV), lambda i: (i, 0)),
            pl.BlockSpec((TILE, DKV), lambda i: (jnp.minimum(i, NT_CTX - 1), 0)),
            pl.BlockSpec((TILE, DKV), lambda i: (jnp.minimum(i, NT_CTX - 1), 0)),
        ],
        out_shape=[
            jax.ShapeDtypeStruct((T, DQ), BF16),
            jax.ShapeDtypeStruct((2 * DKV, T), BF16),
            jax.ShapeDtypeStruct((T, 2 * DKV), BF16),
            jax.ShapeDtypeStruct((T_CTX, DKV), F32),
            jax.ShapeDtypeStruct((T_CTX, DKV), F32),
        ],
        compiler_params=_vmem_limit(56),
        name="qkv",
    )(xp, xs, mod, g, w_qkv, jnp.tile(q_gain, (1, N_HEADS)), jnp.tile(k_gain, (1, N_KV)),
      indq, indqt, indk, indkt, cos_t, sin_t)


def _attn_pair(qp, kts, vlos, vhis):
    lq = qp.shape[0]
    lane = lax.broadcasted_iota(I32, qp.shape, 1)
    zero = jnp.zeros_like(qp)
    q2 = jnp.concatenate([jnp.where(lane < HD, qp, zero), jnp.where(lane >= HD, qp, zero)], axis=0)
    ss = [jnp.dot(q2, kt, preferred_element_type=F32) for kt in kts]
    m = ss[0].max(axis=-1, keepdims=True)
    for s in ss[1:]:
        m = jnp.maximum(m, s.max(axis=-1, keepdims=True))
    acc = None
    l = None
    for s, vlo, vhi in zip(ss, vlos, vhis):
        e = jnp.exp(s - m)
        ls = e.sum(axis=-1, keepdims=True)
        l = ls if l is None else l + ls
        eb = e.astype(BF16)
        t = (jnp.dot(eb[:lq], vlo, preferred_element_type=F32)
             + jnp.dot(eb[lq:], vhi, preferred_element_type=F32))
        acc = t if acc is None else acc + t
    linv = 1.0 / l
    return acc * jnp.where(lane < HD, linv[:lq], linv[lq:])


def _split_values(vd):
    lane = lax.broadcasted_iota(I32, vd.shape, 1)
    zero = jnp.zeros_like(vd)
    return jnp.where(lane < HD, vd, zero), jnp.where(lane >= HD, vd, zero)


def _attn_ctx_kernel(q_ref, kt_ref, v_ref, o_ref):
    for g in range(N_KV):
        kt = kt_ref[g * LANES:(g + 1) * LANES, :]
        vlo, vhi = _split_values(v_ref[:, g * LANES:(g + 1) * LANES])
        for p in range(2):
            c = (2 * g + p) * LANES
            o_ref[:, c:c + LANES] = _attn_pair(q_ref[:, c:c + LANES], [kt], [vlo], [vhi]).astype(BF16)


def _attn_lat_kernel(q_ref, ktc_ref, vc_ref, ktn_ref, vn_ref, o_in_ref, o_ref):
    del o_in_ref
    for g in range(N_KV):
        rows = slice(g * LANES, (g + 1) * LANES)
        kts = [ktc_ref[0, rows, :], ktn_ref[rows, :]]
        vc_lo, vc_hi = _split_values(vc_ref[0, :, rows])
        vn_lo, vn_hi = _split_values(vn_ref[:, rows])
        for p in range(2):
            c = (2 * g + p) * LANES
            o_ref[:, c:c + LANES] = _attn_pair(q_ref[:, c:c + LANES], kts, [vc_lo, vn_lo],
                                               [vc_hi, vn_hi]).astype(BF16)


def _cache_kernel(ck_ref, cv_ref, dup_ref, dupt_ref, kt_ref, v2_ref):
    kt_ref[0] = lax.dot_general(dupt_ref[...], ck_ref[0].astype(BF16), (((1,), (1,)), ((), ())),
                                preferred_element_type=F32).astype(BF16)
    v2_ref[0] = jnp.dot(cv_ref[0].astype(BF16), dup_ref[...], preferred_element_type=F32).astype(BF16)


def _attention(q, kt2, v2, cache_k, cache_v):
    dup = _dup_matrix()
    ktc, vc2 = pl.pallas_call(
        _cache_kernel,
        grid=(DEC_BATCH,),
        in_specs=[
            pl.BlockSpec((1, PAST, DKV), lambda b: (b, 0, 0)),
            pl.BlockSpec((1, PAST, DKV), lambda b: (b, 0, 0)),
            pl.BlockSpec((DKV, 2 * DKV), lambda b: (0, 0)),
            pl.BlockSpec((2 * DKV, DKV), lambda b: (0, 0)),
        ],
        out_specs=[
            pl.BlockSpec((1, 2 * DKV, PAST), lambda b: (b, 0, 0)),
            pl.BlockSpec((1, PAST, 2 * DKV), lambda b: (b, 0, 0)),
        ],
        out_shape=[
            jax.ShapeDtypeStruct((DEC_BATCH, 2 * DKV, PAST), BF16),
            jax.ShapeDtypeStruct((DEC_BATCH, PAST, 2 * DKV), BF16),
        ],
        name="cache_prep",
    )(cache_k, cache_v, dup, dup.T)

    o_ctx = pl.pallas_call(
        _attn_ctx_kernel,
        grid=(BATCH,),
        in_specs=[
            pl.BlockSpec((SEQ, DQ), lambda b: (b, 0)),
            pl.BlockSpec((2 * DKV, SEQ), lambda b: (0, b)),
            pl.BlockSpec((SEQ, 2 * DKV), lambda b: (b, 0)),
        ],
        out_specs=pl.BlockSpec((SEQ, DQ), lambda b: (b, 0)),
        out_shape=jax.ShapeDtypeStruct((T, DQ), BF16),
        name="attn_ctx",
    )(q, kt2, v2)

    qb = 256
    nqb = DEC_SEQ // qb
    lat0 = T_CTX // qb
    return pl.pallas_call(
        _attn_lat_kernel,
        grid=(DEC_BATCH, nqb),
        in_specs=[
            pl.BlockSpec((qb, DQ), lambda b, j: (lat0 + b * nqb + j, 0)),
            pl.BlockSpec((1, 2 * DKV, PAST), lambda b, j: (b, 0, 0)),
            pl.BlockSpec((1, PAST, 2 * DKV), lambda b, j: (b, 0, 0)),
            pl.BlockSpec((2 * DKV, DEC_SEQ), lambda b, j: (0, T_CTX // DEC_SEQ + b)),
            pl.BlockSpec((DEC_SEQ, 2 * DKV), lambda b, j: (T_CTX // DEC_SEQ + b, 0)),
            pl.BlockSpec(memory_space=pl.ANY),
        ],
        out_specs=pl.BlockSpec((qb, DQ), lambda b, j: (lat0 + b * nqb + j, 0)),
        out_shape=jax.ShapeDtypeStruct((T, DQ), BF16),
        input_output_aliases={5: 0},
        compiler_params=_vmem_limit(48),
        name="attn_lat",
    )(q, ktc, vc2, kt2, v2, o_ctx)


def _route(h, wr_hl_ref, wr_hi_ref, tri_ref, cnt_ref, e_ref, rank_ref, wcol_ref, cnt_out_ref):
    tl = h.shape[0]
    h_hi = h.astype(BF16)
    h_lo = (h - h_hi.astype(F32)).astype(BF16)
    lg2 = jnp.dot(h_hi, wr_hl_ref[...], preferred_element_type=F32)
    lg = lg2[:, :LANES] + lg2[:, LANES:] + jnp.dot(h_lo, wr_hi_ref[...], preferred_element_type=F32)
    lt = lg.T

    sub8 = lax.broadcasted_iota(I32, (8, tl), 0).astype(F32)
    gl = jnp.where(sub8 < N_GROUPS, lt[0:8], NEG_INF)
    gmax = gl.max(axis=0, keepdims=True)
    gidx = jnp.min(jnp.where(gl == gmax, sub8, 8.0), axis=0, keepdims=True)
    gsum = jnp.sum(jnp.exp(gl - gmax), axis=0, keepdims=True)
    sel = lt[8:16]
    for g in range(1, N_GROUPS):
        sel = jnp.where(gidx == g, lt[8 + 8 * g:16 + 8 * g], sel)
    t1 = sel.max(axis=0, keepdims=True)
    j1 = jnp.min(jnp.where(sel == t1, sub8, 8.0), axis=0, keepdims=True)
    sel2 = jnp.where(sub8 == j1, NEG_INF, sel)
    t2 = sel2.max(axis=0, keepdims=True)
    j2 = jnp.min(jnp.where(sel2 == t2, sub8, 8.0), axis=0, keepdims=True)
    ex = jnp.exp(t2 - t1)
    den = 1.0 + ex
    gw = 1.0 / gsum
    w0 = gw * (1.0 / den)
    w1 = gw * (ex / den)
    e0 = gidx * E_PER_G + j1
    e1 = gidx * E_PER_G + j2

    sub_e = lax.broadcasted_iota(I32, (NE, tl), 0).astype(F32)
    oh0 = (sub_e == e0).astype(F32)
    oh1 = (sub_e == e1).astype(F32)
    c = oh0 + oh1
    carry = cnt_ref[:, 0:1]
    pieces = []
    for ch in range(tl // 256):
        cc = c[:, ch * 256:(ch + 1) * 256]
        pieces.append(jnp.dot(cc.astype(BF16), tri_ref[...], preferred_element_type=F32) + carry)
        carry = carry + jnp.sum(cc, axis=1, keepdims=True)
    csum = jnp.concatenate(pieces, axis=1)
    cnt_ref[...] = jnp.broadcast_to(carry, (NE, LANES))
    cnt_out_ref[...] = jnp.broadcast_to(carry, (NE, LANES))
    r0 = jnp.sum(oh0 * csum, axis=0, keepdims=True) - 1.0
    r1 = jnp.sum(oh1 * csum, axis=0, keepdims=True) - 1.0
    e_ref[0:1, :] = e0.astype(I32)
    e_ref[1:2, :] = e1.astype(I32)
    rank_ref[0:1, :] = r0.astype(I32)
    rank_ref[1:2, :] = r1.astype(I32)
    sub_w = lax.broadcasted_iota(I32, (LANES, tl), 0)
    w8 = jnp.where(sub_w == 0, w0, jnp.where(sub_w == 1, w1, 0.0))
    wcol_ref[...] = w8.T


def _post_kernel(o_ref, xp_ref, xs_ref, mod_ref, wo_ref, g_ref, wr_hl_ref, wr_hi_ref, tri_ref,
                 x1_ref, h_ref, e_ref, rank_ref, wcol_ref, cnt_out_ref, cnt_ref):
    i = pl.program_id(0)

    @pl.when(i == 0)
    def _():
        cnt_ref[...] = jnp.zeros_like(cnt_ref)

    x = jnp.where(i >= NT_CTX, xs_ref[...], xp_ref[...])
    m = jnp.dot(o_ref[...], wo_ref[...], preferred_element_type=F32)
    x1 = x + mod_ref[0, 2:3, :] * m
    x1_ref[...] = x1
    h = _modulate(x1, g_ref[...] * (1.0 + mod_ref[0, 4:5, :]), mod_ref[0, 3:4, :])
    h_ref[...] = _pack_rows(h)
    _route(h, wr_hl_ref, wr_hi_ref, tri_ref, cnt_ref, e_ref, rank_ref, wcol_ref, cnt_out_ref)


def _router_weights(w_group, w_expert):
    w = jnp.zeros((D, LANES), F32)
    w = w.at[:, :N_GROUPS].set(w_group)
    w = w.at[:, 8:8 + NE].set(jnp.transpose(w_expert, (1, 0, 2)).reshape(D, NE))
    w_hi = w.astype(BF16)
    w_lo = (w - w_hi.astype(F32)).astype(BF16)
    return jnp.concatenate([w_hi, w_lo], axis=1), w_hi


def _route_outs(tile):
    specs = [
        pl.BlockSpec((tile, D), lambda i: (i, 0)),
        pl.BlockSpec((tile, HALF), lambda i: (i, 0)),
        pl.BlockSpec((TOP_K, tile), lambda i: (0, i)),
        pl.BlockSpec((TOP_K, tile), lambda i: (0, i)),
        pl.BlockSpec((tile, LANES), lambda i: (i, 0)),
        pl.BlockSpec((NE, LANES), lambda i: (0, 0)),
    ]
    shapes = [
        jax.ShapeDtypeStruct((T, D), F32),
        jax.ShapeDtypeStruct((T, HALF), U32),
        jax.ShapeDtypeStruct((TOP_K, T), I32),
        jax.ShapeDtypeStruct((TOP_K, T), I32),
        jax.ShapeDtypeStruct((T, LANES), F32),
        jax.ShapeDtypeStruct((NE, LANES), F32),
    ]
    return specs, shapes


def _tri():
    a = jnp.arange(256)
    return (a[:, None] <= a[None, :]).astype(BF16)


def _post(o, xp, xs, mod, w_o, g_ffn, wr_hl, wr_hi):
    const = lambda shape: pl.BlockSpec(shape, lambda i: (0,) * len(shape))
    out_specs, out_shape = _route_outs(TILE)
    return pl.pallas_call(
        _post_kernel,
        grid=(NT,),
        in_specs=[
            pl.BlockSpec((TILE, DQ), lambda i: (i, 0)),
            pl.BlockSpec((TILE, D), lambda i: (jnp.minimum(i, NT_CTX - 1), 0)),
            pl.BlockSpec((TILE, D), lambda i: (jnp.maximum(i - NT_CTX, 0), 0)),
            pl.BlockSpec((1, N_MOD, D), lambda i: (_cond_index(i, TILE), 0, 0)),
            const((DQ, D)), const((1, D)), const((D, 2 * LANES)), const((D, LANES)), const((256, 256)),
        ],
        out_specs=out_specs,
        out_shape=out_shape,
        scratch_shapes=[pltpu.VMEM((NE, LANES), F32)],
        compiler_params=_vmem_limit(56),
        name="post_attn",
    )(o, xp, xs, mod, w_o, g_ffn, wr_hl, wr_hi, _tri())


def _pool_kernel(y0_ref, y1_ref, xin_ref, wcin_ref, modp_ref, mod_ref, g_ref, pw_ref, ps_ref, gf_ref,
                 wr_hl_ref, wr_hi_ref, tri_ref,
                 x1_ref, h_ref, e_ref, rank_ref, wcol_ref, cnt_out_ref, cnt_ref):
    i = pl.program_id(0)

    @pl.when(i == 0)
    def _():
        cnt_ref[...] = jnp.zeros_like(cnt_ref)

    x = _combined(y0_ref, y1_ref, xin_ref, wcin_ref, modp_ref)
    h = _modulate(x, g_ref[...] * (1.0 + mod_ref[0, 1:2, :]), mod_ref[0, 0:1, :])
    seq = jnp.where(i >= T_CTX // PTILE, DEC_SEQ, SEQ)
    row = lax.broadcasted_iota(I32, (PTILE, PGD), 0)
    t = row & (seq - 1)

    def back(z, s):
        return jnp.where(t >= s, pltpu.roll(z, s, axis=0), 0.0)

    def fwd(z, s):
        return jnp.where(t + s < seq, pltpu.roll(z, PTILE - s, axis=0), 0.0)

    outs = []
    for gi, win in enumerate(POOL_WINDOWS):
        hg = h[:, gi * PGD:(gi + 1) * PGD]
        left = win // 2
        right = win - left - 1
        p = hg + back(hg, 1)
        width = 2
        while 2 * width <= left:
            p = p + back(p, width)
            width *= 2
        if left + 1 > width:
            p = p + back(hg, width)
        tot = p
        if right > 0:
            f = fwd(hg, 1)
            fw = 1
            parts = {1: f}
            while 2 * fw <= right:
                f = f + fwd(f, fw)
                fw *= 2
                parts[fw] = f
            lead = f
            done = fw
            rem = right - done
            while rem > 0:
                step = max(w_ for w_ in parts if w_ <= rem)
                lead = lead + fwd(parts[step], done)
                done += step
                rem -= step
            tot = tot + lead
        lo = jnp.maximum(t - left, 0)
        hi = jnp.minimum(t + right + 1, seq)
        diff = tot / (hi - lo).astype(F32) - hg
        outs.append(jnp.dot(diff.astype(BF16), pw_ref[gi], preferred_element_type=F32))
    m = jnp.concatenate(outs, axis=1) * ps_ref[...]
    x1 = x + mod_ref[0, 2:3, :] * m
    x1_ref[...] = x1
    hf = _modulate(x1, gf_ref[...] * (1.0 + mod_ref[0, 4:5, :]), mod_ref[0, 3:4, :])
    h_ref[...] = _pack_rows(hf)
    _route(hf, wr_hl_ref, wr_hi_ref, tri_ref, cnt_ref, e_ref, rank_ref, wcol_ref, cnt_out_ref)


def _pool(y2, x, wcol, mod_prev, mod, g_mix, pool_w, pool_scale, g_ffn, wr_hl, wr_hi):
    const = lambda shape: pl.BlockSpec(shape, lambda i: (0,) * len(shape))
    out_specs, out_shape = _route_outs(PTILE)
    return pl.pallas_call(
        _pool_kernel,
        grid=(T // PTILE,),
        in_specs=[
            pl.BlockSpec((PTILE, HALF), lambda i: (i, 0)),
            pl.BlockSpec((PTILE, HALF), lambda i: (T // PTILE + i, 0)),
            pl.BlockSpec((PTILE, D), lambda i: (i, 0)),
            pl.BlockSpec((PTILE, LANES), lambda i: (i, 0)),
            pl.BlockSpec((1, N_MOD, D), lambda i: (_cond_index(i, PTILE), 0, 0)),
            pl.BlockSpec((1, N_MOD, D), lambda i: (_cond_index(i, PTILE), 0, 0)),
            const((1, D)), const((len(POOL_WINDOWS), PGD, PGD)), const((1, D)), const((1, D)),
            const((D, 2 * LANES)), const((D, LANES)), const((256, 256)),
        ],
        out_specs=out_specs,
        out_shape=out_shape,
        scratch_shapes=[pltpu.VMEM((NE, LANES), F32)],
        compiler_params=_vmem_limit(56),
        name="pool_mixer",
    )(y2, y2, x, wcol, mod_prev, mod, g_mix, pool_w, pool_scale, g_ffn, wr_hl, wr_hi, _tri())


def _plan(e, rank, cnt):
    counts = cnt[:, 0].astype(I32)
    n_tiles = (counts + TM - 1) // TM
    tile_end = jnp.cumsum(n_tiles)
    base = (tile_end - n_tiles) * TM
    pos = jnp.sum(jnp.where(e[None] == jnp.arange(NE, dtype=I32)[:, None, None], base[:, None, None], 0),
                  axis=0) + rank
    used = tile_end[-1]
    tile_id = jnp.minimum(jnp.arange(N_ROW_TILES, dtype=I32), used - 1)
    tile_expert = jnp.sum((tile_id[:, None] >= tile_end[None, :]).astype(I32), axis=1)
    return pos, tile_expert, used.reshape(1)


def _sc_mesh():
    return plsc.VectorSubcoreMesh(core_axis_name="core", subcore_axis_name="subcore")


def _sc_worker():
    return lax.axis_index("core") * SC_SUBCORES + lax.axis_index("subcore")


def _sc_dispatch(h, pos):
    per = T // SC_WORKERS
    nwin = per // SC_WIN

    @functools.partial(
        pl.kernel, out_type=jax.ShapeDtypeStruct((NPAD, HALF), U32), mesh=_sc_mesh(),
        scratch_types=[pltpu.VMEM((2, SC_WIN, HALF), U32), pltpu.VMEM((2, SC_WIN), I32),
                       pltpu.VMEM((2, SC_WIN), I32), pltpu.SemaphoreType.DMA((2,)),
                       pltpu.SemaphoreType.DMA((2,))])
    def run(h_hbm, pos_hbm, xs_hbm, buf, idx0, idx1, sem_in, sem_out):
        first = _sc_worker() * per

        def loads(j, s):
            base = first + j * SC_WIN
            return (pltpu.make_async_copy(h_hbm.at[pl.ds(base, SC_WIN)], buf.at[s], sem_in.at[s]),
                    pltpu.make_async_copy(pos_hbm.at[pl.ds(base, SC_WIN)], idx0.at[s], sem_in.at[s]),
                    pltpu.make_async_copy(pos_hbm.at[pl.ds(T + base, SC_WIN)], idx1.at[s], sem_in.at[s]))

        def stores(s):
            return (pltpu.make_async_copy(buf.at[s], xs_hbm.at[idx0.at[s]], sem_out.at[s]),
                    pltpu.make_async_copy(buf.at[s], xs_hbm.at[idx1.at[s]], sem_out.at[s]))

        for c in loads(0, 0):
            c.start()

        @pl.loop(0, nwin // 2)
        def _(jj):
            for s in range(2):
                j = jj * 2 + s
                for c in loads(j, s):
                    c.wait()
                for c in stores(s):
                    c.start()

                @pl.when(j >= 1)
                def _():
                    for c in stores(1 - s):
                        c.wait()

                @pl.when(j + 1 < nwin)
                def _():
                    for c in loads(j + 1, 1 - s):
                        c.start()

        for c in stores((nwin - 1) % 2):
            c.wait()

    return run(h, pos.reshape(TOP_K * T))


def _sc_gather(ys, pos):
    n = TOP_K * T
    per = n // SC_WORKERS
    nwin = per // SC_WIN

    @functools.partial(
        pl.kernel, out_type=jax.ShapeDtypeStruct((n, HALF), U32), mesh=_sc_mesh(),
        scratch_types=[pltpu.VMEM((2, SC_WIN, HALF), U32), pltpu.VMEM((per,), I32),
                       pltpu.SemaphoreType.DMA((2,)), pltpu.SemaphoreType.DMA((2,))])
    def run(ys_hbm, pos_hbm, out_hbm, buf, idx, sem_in, sem_out):
        first = _sc_worker() * per
        pltpu.sync_copy(pos_hbm.at[pl.ds(first, per)], idx)

        def load(j, s):
            return pltpu.make_async_copy(ys_hbm.at[idx.at[pl.ds(j * SC_WIN, SC_WIN)]], buf.at[s], sem_in.at[s])

        def store(j, s):
            return pltpu.make_async_copy(buf.at[s], out_hbm.at[pl.ds(first + j * SC_WIN, SC_WIN)], sem_out.at[s])

        load(0, 0).start()

        @pl.loop(0, nwin // 2)
        def _(jj):
            for s in range(2):
                j = jj * 2 + s
                load(j, s).wait()
                store(j, s).start()

                @pl.when(j >= 1)
                def _():
                    store(j - 1, 1 - s).wait()

                @pl.when(j + 1 < nwin)
                def _():
                    load(j + 1, 1 - s).start()

        store(nwin - 1, (nwin - 1) % 2).wait()

    return run(ys, pos.reshape(n))


def _moe_kernel(te_ref, used_ref, x_ref, wg_ref, wu_ref, wd_ref, y_ref, wg_s, wu_s, wd_s):
    i = pl.program_id(0)

    @pl.when(i < used_ref[0])
    def _():
        prev = te_ref[jnp.maximum(i - 1, 0)]

        @pl.when(jnp.logical_or(i == 0, te_ref[i] != prev))
        def _():
            wg_s[...] = wg_ref[0, 0].astype(BF16)
            wu_s[...] = wu_ref[0, 0].astype(BF16)
            wd_s[...] = wd_ref[0, 0].astype(BF16)

        hi, lo = _unpack_rows(x_ref[...])
        x = jnp.concatenate([hi, lo], axis=1).astype(BF16)
        a = jnp.dot(x, wg_s[...], preferred_element_type=F32)
        b = jnp.dot(x, wu_s[...], preferred_element_type=F32)
        hid = a * (1.0 / (1.0 + jnp.exp(-a))) * b
        y_ref[...] = _pack_rows(jnp.dot(hid.astype(BF16), wd_s[...], preferred_element_type=F32))


def _experts(xs, tile_expert, used, layer, w_gate, w_up, w_down):
    row = lambda i, te, used: (jnp.minimum(i, used[0] - 1), 0)
    grid_spec = pltpu.PrefetchScalarGridSpec(
        num_scalar_prefetch=2,
        grid=(N_ROW_TILES,),
        in_specs=[
            pl.BlockSpec((TM, HALF), row),
            pl.BlockSpec((1, 1, D, DE), lambda i, te, used: (layer, te[i], 0, 0)),
            pl.BlockSpec((1, 1, D, DE), lambda i, te, used: (layer, te[i], 0, 0)),
            pl.BlockSpec((1, 1, DE, D), lambda i, te, used: (layer, te[i], 0, 0)),
        ],
        out_specs=pl.BlockSpec((TM, HALF), row),
        scratch_shapes=[pltpu.VMEM((D, DE), BF16), pltpu.VMEM((D, DE), BF16), pltpu.VMEM((DE, D), BF16)],
    )
    return pl.pallas_call(
        _moe_kernel,
        grid_spec=grid_spec,
        out_shape=jax.ShapeDtypeStruct((NPAD, HALF), U32),
        compiler_params=_vmem_limit(40),
        name="moe_experts",
    )(tile_expert, used, xs, w_gate, w_up, w_down)


def _combined(y0_ref, y1_ref, x_ref, wcol_ref, mod_ref):
    w0 = wcol_ref[:, 0:1]
    w1 = wcol_ref[:, 1:2]
    hi0, lo0 = _unpack_rows(y0_ref[...])
    hi1, lo1 = _unpack_rows(y1_ref[...])
    y = jnp.concatenate([w0 * hi0 + w1 * hi1, w0 * lo0 + w1 * lo1], axis=1)
    return x_ref[...] + mod_ref[0, 5:6, :] * y


def _combine_final_kernel(y0_ref, y1_ref, x_ref, wcol_ref, mod_ref, op_ref, os_ref):
    i = pl.program_id(0)
    out = _combined(y0_ref, y1_ref, x_ref, wcol_ref, mod_ref)

    @pl.when(i < T_CTX // CT)
    def _():
        op_ref[...] = out

    @pl.when(i >= T_CTX // CT)
    def _():
        os_ref[...] = out


def _combine(y2, x, wcol, mod):
    n_ctx = T_CTX // CT
    return pl.pallas_call(
        _combine_final_kernel,
        grid=(T // CT,),
        in_specs=[
            pl.BlockSpec((CT, HALF), lambda i: (i, 0)),
            pl.BlockSpec((CT, HALF), lambda i: (T // CT + i, 0)),
            pl.BlockSpec((CT, D), lambda i: (i, 0)),
            pl.BlockSpec((CT, LANES), lambda i: (i, 0)),
            pl.BlockSpec((1, N_MOD, D), lambda i: (_cond_index(i, CT), 0, 0)),
        ],
        out_specs=[
            pl.BlockSpec((CT, D), lambda i: (jnp.minimum(i, n_ctx - 1), 0)),
            pl.BlockSpec((CT, D), lambda i: (jnp.maximum(i - n_ctx, 0), 0)),
        ],
        out_shape=[jax.ShapeDtypeStruct((T_CTX, D), F32), jax.ShapeDtypeStruct((T_LAT, D), F32)],
        name="moe_combine_final",
    )(y2, y2, x, wcol, mod)


def _moe(h, e, rank, cnt, layer, w_gate, w_up, w_down):
    pos, tile_expert, used = _plan(e, rank, cnt)
    xs = _sc_dispatch(h, pos)
    ys = _experts(xs, tile_expert, used, layer, w_gate, w_up, w_down)
    return _sc_gather(ys, pos)


def kernel(x_prompt, x_sample, c, cache_k, cache_v, c_ctx, norm_mix_g, norm_ffn_g, ada_w, ada_b, attn_w_qkv, attn_q_norm, attn_k_norm, attn_w_o, pool_w, pool_scale, moe_w_group, moe_w_expert, moe_w_gate, moe_w_up, moe_w_down):
    xp = x_prompt.reshape(T_CTX, D)
    xs = x_sample.reshape(T_LAT, D)
    cond = jnp.concatenate([c_ctx[None, :], c, jnp.zeros((N_COND - 1 - DEC_BATCH, D), F32)], axis=0)
    mod = _ada(cond, ada_w, ada_b)

    q, kt2, v2, new_k, new_v = _qkv(xp, xs, mod[0], norm_mix_g[0:1], attn_w_qkv[0].astype(BF16),
                                    attn_q_norm[0:1], attn_k_norm[0:1])
    o = _attention(q, kt2, v2, cache_k[:, 0].reshape(DEC_BATCH, PAST, DKV),
                   cache_v[:, 0].reshape(DEC_BATCH, PAST, DKV))
    wr_hl, wr_hi = _router_weights(moe_w_group[0], moe_w_expert[0])
    x1, h, e, rank, wcol, cnt = _post(o, xp, xs, mod[0], attn_w_o[0].astype(BF16), norm_ffn_g[0:1],
                                      wr_hl, wr_hi)
    y2 = _moe(h, e, rank, cnt, 0, moe_w_gate, moe_w_up, moe_w_down)

    wr_hl, wr_hi = _router_weights(moe_w_group[1], moe_w_expert[1])
    x3, h, e, rank, wcol, cnt = _pool(y2, x1, wcol, mod[0], mod[1], norm_mix_g[1:2], pool_w[0].astype(BF16),
                                      pool_scale[0:1], norm_ffn_g[1:2], wr_hl, wr_hi)
    y2 = _moe(h, e, rank, cnt, 1, moe_w_gate, moe_w_up, moe_w_down)
    y_prompt, y_sample = _combine(y2, x3, wcol, mod[1])

    return (y_prompt.reshape(BATCH, SEQ, D), y_sample.reshape(DEC_BATCH, DEC_SEQ, D),
            new_k.reshape(BATCH, 1, SEQ, N_KV, HD), new_v.reshape(BATCH, 1, SEQ, N_KV, HD))
```

```python
import functools

import jax
import jax.numpy as jnp
from jax import lax
from jax.experimental import pallas as pl
from jax.experimental.pallas import tpu as pltpu
from jax.experimental.pallas import tpu_sc as plsc

F32 = jnp.float32
BF16 = jnp.bfloat16
I32 = jnp.int32
U32 = jnp.uint32

D = 1024
BATCH, SEQ = 32, 256
DEC_BATCH, DEC_SEQ, PAST = 8, 1024, 512
T_CTX = BATCH * SEQ
T_LAT = DEC_BATCH * DEC_SEQ
T = T_CTX + T_LAT
GRID_W = 64
N_HEADS, N_KV, HD = 16, 4, 64
DQ = N_HEADS * HD
DKV = N_KV * HD
ROPE_THETA = 10000.0
POOL_WINDOWS = (2, 4, 8, 16)
PGD = D // len(POOL_WINDOWS)
N_GROUPS, E_PER_G, TOP_K = 4, 8, 2
NE = N_GROUPS * E_PER_G
DE = D // 4
N_MOD = 6
EPS = 1e-6
N_COND = 16

TILE = 512
NT = T // TILE
NT_CTX = T_CTX // TILE
PTILE = 1024
PB = 128
PBW = 256
PB_HALO = (PBW - PB) // 2
TM = 512
N_ROW_TILES = (TOP_K * T) // TM + NE
NPAD = N_ROW_TILES * TM
CT = 512
LANES = 128
NEG_INF = float("-inf")
HALF = D // 2
SC_CORES, SC_SUBCORES = 2, 16
SC_WORKERS = SC_CORES * SC_SUBCORES
SC_WIN = 64


def _vmem_limit(mib):
    return pltpu.CompilerParams(vmem_limit_bytes=mib * 1024 * 1024)


def _modulate(x, gain_scale, shift):
    ms = jnp.mean(x * x, axis=-1, keepdims=True)
    return (x * lax.rsqrt(ms + EPS)) * gain_scale + shift


def _head_norm(z, gain, ind, ind_t2):
    ss = jnp.dot((z * z).astype(BF16), ind, preferred_element_type=F32)
    inv = lax.rsqrt(ss * (1.0 / HD) + EPS)
    inv_hi = inv.astype(BF16)
    inv_lo = (inv - inv_hi.astype(F32)).astype(BF16)
    scale = jnp.dot(jnp.concatenate([inv_hi, inv_lo], axis=1), ind_t2, preferred_element_type=F32)
    return z * scale * gain


def _rope(z, cos_t, sin_t):
    lane = lax.broadcasted_iota(I32, (z.shape[0], LANES), 1)
    low = (lane % 32) < 16
    outs = []
    for c in range(z.shape[1] // LANES):
        zc = z[:, c * LANES:(c + 1) * LANES]
        up = pltpu.roll(zc, 16, axis=1)
        dn = pltpu.roll(zc, LANES - 16, axis=1)
        outs.append(zc * cos_t + jnp.where(low, dn, up) * sin_t)
    return jnp.concatenate(outs, axis=1)


def _pack_rows(z):
    bits = lax.bitcast_convert_type(z.astype(BF16).astype(F32), U32)
    return bits[:, :HALF] | (bits[:, HALF:] >> 16)


def _unpack_rows(p):
    hi = lax.bitcast_convert_type(p & jnp.uint32(0xFFFF0000), F32)
    lo = lax.bitcast_convert_type(p << 16, F32)
    return hi, lo


def _ada_kernel(c_ref, w_ref, b_ref, o_ref):
    c = c_ref[...]
    a = c * (1.0 / (1.0 + jnp.exp(-c)))
    o_ref[0] = jnp.dot(a.astype(BF16), w_ref[0].astype(BF16), preferred_element_type=F32) + b_ref[0]


def _ada(cond, ada_w, ada_b):
    nb = 1536
    depth = ada_w.shape[0]
    out = pl.pallas_call(
        _ada_kernel,
        grid=(depth, (N_MOD * D) // nb),
        in_specs=[
            pl.BlockSpec((N_COND, D), lambda l, j: (0, 0)),
            pl.BlockSpec((1, D, nb), lambda l, j: (l, 0, j)),
            pl.BlockSpec((1, 1, nb), lambda l, j: (l, 0, j)),
        ],
        out_specs=pl.BlockSpec((1, N_COND, nb), lambda l, j: (l, 0, j)),
        out_shape=jax.ShapeDtypeStruct((depth, N_COND, N_MOD * D), F32),
        compiler_params=_vmem_limit(40),
        name="ada",
    )(cond, ada_w, ada_b.reshape(depth, 1, N_MOD * D))
    return out.reshape(depth, N_COND, N_MOD, D)


def _cond_index(i, tile):
    n_ctx = T_CTX // tile
    per_batch = DEC_SEQ // tile
    return jnp.where(i < n_ctx, 0, 1 + (i - n_ctx) // per_batch)


def _qkv_kernel(xp_ref, xs_ref, mod_ref, g_ref, w_ref, qg_ref, kg_ref, indq_ref, indqt_ref,
                indk_ref, indkt_ref, cos_ref, sin_ref,
                q_ref, kt_ref, v2_ref, nk_ref, nv_ref):
    i = pl.program_id(0)
    is_lat = i >= NT_CTX
    x = jnp.where(is_lat, xs_ref[...], xp_ref[...])
    h = _modulate(x, g_ref[...] * (1.0 + mod_ref[0, 1:2, :]), mod_ref[0, 0:1, :])
    qkv = jnp.dot(h.astype(BF16), w_ref[...], preferred_element_type=F32)
    q = _head_norm(qkv[:, :DQ], qg_ref[...], indq_ref[...], indqt_ref[...])
    k = _head_norm(qkv[:, DQ:DQ + DKV], kg_ref[...], indk_ref[...], indkt_ref[...])
    v = qkv[:, DQ + DKV:]

    lane = lax.broadcasted_iota(I32, (TILE, LANES), 1)
    for p in range(N_KV // 2):
        blk = v[:, p * LANES:(p + 1) * LANES]
        swp = pltpu.roll(blk, HD, axis=1)
        v2_ref[:, (2 * p) * LANES:(2 * p + 1) * LANES] = jnp.where(lane < HD, blk, swp).astype(BF16)
        v2_ref[:, (2 * p + 1) * LANES:(2 * p + 2) * LANES] = jnp.where(lane < HD, swp, blk).astype(BF16)

    def store(qn, kn):
        q_ref[...] = (qn * (HD ** -0.5)).astype(BF16)
        knt = kn.T.astype(BF16)
        for g in range(N_KV):
            kt_ref[g * LANES:g * LANES + HD, :] = knt[g * HD:(g + 1) * HD]
            kt_ref[g * LANES + HD:(g + 1) * LANES, :] = knt[g * HD:(g + 1) * HD]

    @pl.when(is_lat)
    def _():
        store(_rope(q, cos_ref[...], sin_ref[...]), _rope(k, cos_ref[...], sin_ref[...]))

    @pl.when(jnp.logical_not(is_lat))
    def _():
        store(q, k)
        nk_ref[...] = k
        nv_ref[...] = v


def _rope_tables():
    rows = DEC_SEQ // GRID_W
    row = jnp.broadcast_to(jnp.arange(rows, dtype=F32)[:, None], (rows, GRID_W)).reshape(-1)
    col = jnp.broadcast_to(jnp.arange(GRID_W, dtype=F32)[None, :], (rows, GRID_W)).reshape(-1)
    axis_dim = HD // 2
    inv_freq = jnp.power(ROPE_THETA, -jnp.arange(0, axis_dim, 2, dtype=F32) / axis_dim)
    ang = jnp.concatenate([row[:, None] * inv_freq, col[:, None] * inv_freq], axis=-1)
    cos, sin = jnp.cos(ang), jnp.sin(ang)
    quarter = HD // 4
    cos_h = jnp.concatenate([cos[:, :quarter], cos[:, :quarter], cos[:, quarter:], cos[:, quarter:]], axis=1)
    sin_h = jnp.concatenate([-sin[:, :quarter], sin[:, :quarter], -sin[:, quarter:], sin[:, quarter:]], axis=1)
    return jnp.tile(cos_h, (1, LANES // HD)), jnp.tile(sin_h, (1, LANES // HD))


def _head_indicators(width):
    col = jnp.arange(width)[:, None]
    head = jnp.arange(LANES)[None, :]
    ind = (col // HD == head).astype(BF16)
    return ind, jnp.concatenate([ind.T, ind.T], axis=0)


def _dup_matrix():
    src = jnp.arange(DKV)[:, None]
    dst = jnp.arange(2 * DKV)[None, :]
    return (src == HD * (dst // LANES) + dst % HD).astype(BF16)


def _qkv(xp, xs, mod, g, w_qkv, q_gain, k_gain):
    indq, indqt = _head_indicators(DQ)
    indk, indkt = _head_indicators(DKV)
    cos_t, sin_t = _rope_tables()
    per_batch = DEC_SEQ // TILE
    const = lambda shape: pl.BlockSpec(shape, lambda i: (0,) * len(shape))
    return pl.pallas_call(
        _qkv_kernel,
        grid=(NT,),
        in_specs=[
            pl.BlockSpec((TILE, D), lambda i: (jnp.minimum(i, NT_CTX - 1), 0)),
            pl.BlockSpec((TILE, D), lambda i: (jnp.maximum(i - NT_CTX, 0), 0)),
            pl.BlockSpec((1, N_MOD, D), lambda i: (_cond_index(i, TILE), 0, 0)),
            const((1, D)),
            const((D, DQ + 2 * DKV)),
            const((1, DQ)), const((1, DKV)),
            const((DQ, LANES)), const((2 * LANES, DQ)),
            const((DKV, LANES)), const((2 * LANES, DKV)),
            pl.BlockSpec((TILE, LANES), lambda i: (jnp.maximum(i - NT_CTX, 0) % per_batch, 0)),
            pl.BlockSpec((TILE, LANES), lambda i: (jnp.maximum(i - NT_CTX, 0) % per_batch, 0)),
        ],
        out_specs=[
            pl.BlockSpec((TILE, DQ), lambda i: (i, 0)),
            pl.BlockSpec((2 * DKV, TILE), lambda i: (0, i)),
            pl.BlockSpec((TILE, 2 * DKV), lambda i: (i, 0)),
            pl.BlockSpec((TILE, DKV), lambda i: (jnp.minimum(i, NT_CTX - 1), 0)),
            pl.BlockSpec((TILE, DKV), lambda i: (jnp.minimum(i, NT_CTX - 1), 0)),
        ],
        out_shape=[
            jax.ShapeDtypeStruct((T, DQ), BF16),
            jax.ShapeDtypeStruct((2 * DKV, T), BF16),
            jax.ShapeDtypeStruct((T, 2 * DKV), BF16),
            jax.ShapeDtypeStruct((T_CTX, DKV), F32),
            jax.ShapeDtypeStruct((T_CTX, DKV), F32),
        ],
        compiler_params=_vmem_limit(56),
        name="qkv",
    )(xp, xs, mod, g, w_qkv, jnp.tile(q_gain, (1, N_HEADS)), jnp.tile(k_gain, (1, N_KV)),
      indq, indqt, indk, indkt, cos_t, sin_t)


def _attn_pair(qp, kts, vlos, vhis):
    lq = qp.shape[0]
    lane = lax.broadcasted_iota(I32, qp.shape, 1)
    zero = jnp.zeros_like(qp)
    q2 = jnp.concatenate([jnp.where(lane < HD, qp, zero), jnp.where(lane >= HD, qp, zero)], axis=0)
    ss = [jnp.dot(q2, kt, preferred_element_type=F32) for kt in kts]
    m = ss[0].max(axis=-1, keepdims=True)
    for s in ss[1:]:
        m = jnp.maximum(m, s.max(axis=-1, keepdims=True))
    acc = None
    l = None
    for s, vlo, vhi in zip(ss, vlos, vhis):
        e = jnp.exp(s - m)
        ls = e.sum(axis=-1, keepdims=True)
        l = ls if l is None else l + ls
        eb = e.astype(BF16)
        t = (jnp.dot(eb[:lq], vlo, preferred_element_type=F32)
             + jnp.dot(eb[lq:], vhi, preferred_element_type=F32))
        acc = t if acc is None else acc + t
    linv = 1.0 / l
    return acc * jnp.where(lane < HD, linv[:lq], linv[lq:])


def _split_values(vd):
    lane = lax.broadcasted_iota(I32, vd.shape, 1)
    zero = jnp.zeros_like(vd)
    return jnp.where(lane < HD, vd, zero), jnp.where(lane >= HD, vd, zero)


def _attn_ctx_kernel(q_ref, kt_ref, v_ref, o_ref):
    for g in range(N_KV):
        kt = kt_ref[g * LANES:(g + 1) * LANES, :]
        vlo, vhi = _split_values(v_ref[:, g * LANES:(g + 1) * LANES])
        for p in range(2):
            c = (2 * g + p) * LANES
            o_ref[:, c:c + LANES] = _attn_pair(q_ref[:, c:c + LANES], [kt], [vlo], [vhi]).astype(BF16)


def _attn_lat_kernel(q_ref, ktc_ref, vc_ref, ktn_ref, vn_ref, o_in_ref, o_ref):
    del o_in_ref
    for g in range(N_KV):
        rows = slice(g * LANES, (g + 1) * LANES)
        kts = [ktc_ref[0, rows, :], ktn_ref[rows, :]]
        vc_lo, vc_hi = _split_values(vc_ref[0, :, rows])
        vn_lo, vn_hi = _split_values(vn_ref[:, rows])
        for p in range(2):
            c = (2 * g + p) * LANES
            o_ref[:, c:c + LANES] = _attn_pair(q_ref[:, c:c + LANES], kts, [vc_lo, vn_lo],
                                               [vc_hi, vn_hi]).astype(BF16)


def _cache_kernel(ck_ref, cv_ref, dup_ref, dupt_ref, kt_ref, v2_ref):
    kt_ref[0] = lax.dot_general(dupt_ref[...], ck_ref[0].astype(BF16), (((1,), (1,)), ((), ())),
                                preferred_element_type=F32).astype(BF16)
    v2_ref[0] = jnp.dot(cv_ref[0].astype(BF16), dup_ref[...], preferred_element_type=F32).astype(BF16)


def _attention(q, kt2, v2, cache_k, cache_v):
    dup = _dup_matrix()
    ktc, vc2 = pl.pallas_call(
        _cache_kernel,
        grid=(DEC_BATCH,),
        in_specs=[
            pl.BlockSpec((1, PAST, DKV), lambda b: (b, 0, 0)),
            pl.BlockSpec((1, PAST, DKV), lambda b: (b, 0, 0)),
            pl.BlockSpec((DKV, 2 * DKV), lambda b: (0, 0)),
            pl.BlockSpec((2 * DKV, DKV), lambda b: (0, 0)),
        ],
        out_specs=[
            pl.BlockSpec((1, 2 * DKV, PAST), lambda b: (b, 0, 0)),
            pl.BlockSpec((1, PAST, 2 * DKV), lambda b: (b, 0, 0)),
        ],
        out_shape=[
            jax.ShapeDtypeStruct((DEC_BATCH, 2 * DKV, PAST), BF16),
            jax.ShapeDtypeStruct((DEC_BATCH, PAST, 2 * DKV), BF16),
        ],
        name="cache_prep",
    )(cache_k, cache_v, dup, dup.T)

    o_ctx = pl.pallas_call(
        _attn_ctx_kernel,
        grid=(BATCH,),
        in_specs=[
            pl.BlockSpec((SEQ, DQ), lambda b: (b, 0)),
            pl.BlockSpec((2 * DKV, SEQ), lambda b: (0, b)),
            pl.BlockSpec((SEQ, 2 * DKV), lambda b: (b, 0)),
        ],
        out_specs=pl.BlockSpec((SEQ, DQ), lambda b: (b, 0)),
        out_shape=jax.ShapeDtypeStruct((T, DQ), BF16),
        name="attn_ctx",
    )(q, kt2, v2)

    qb = 256
    nqb = DEC_SEQ // qb
    lat0 = T_CTX // qb
    return pl.pallas_call(
        _attn_lat_kernel,
        grid=(DEC_BATCH, nqb),
        in_specs=[
            pl.BlockSpec((qb, DQ), lambda b, j: (lat0 + b * nqb + j, 0)),
            pl.BlockSpec((1, 2 * DKV, PAST), lambda b, j: (b, 0, 0)),
            pl.BlockSpec((1, PAST, 2 * DKV), lambda b, j: (b, 0, 0)),
            pl.BlockSpec((2 * DKV, DEC_SEQ), lambda b, j: (0, T_CTX // DEC_SEQ + b)),
            pl.BlockSpec((DEC_SEQ, 2 * DKV), lambda b, j: (T_CTX // DEC_SEQ + b, 0)),
            pl.BlockSpec(memory_space=pl.ANY),
        ],
        out_specs=pl.BlockSpec((qb, DQ), lambda b, j: (lat0 + b * nqb + j, 0)),
        out_shape=jax.ShapeDtypeStruct((T, DQ), BF16),
        input_output_aliases={5: 0},
        compiler_params=_vmem_limit(48),
        name="attn_lat",
    )(q, ktc, vc2, kt2, v2, o_ctx)


def _route(h, wr_hl_ref, wr_hi_ref, tri_ref, cnt_ref, e_ref, rank_ref, wcol_ref, cnt_out_ref):
    tl = h.shape[0]
    h_hi = h.astype(BF16)
    h_lo = (h - h_hi.astype(F32)).astype(BF16)
    lg2 = jnp.dot(h_hi, wr_hl_ref[...], preferred_element_type=F32)
    lg = lg2[:, :LANES] + lg2[:, LANES:] + jnp.dot(h_lo, wr_hi_ref[...], preferred_element_type=F32)
    lt = lg.T

    sub8 = lax.broadcasted_iota(I32, (8, tl), 0).astype(F32)
    gl = jnp.where(sub8 < N_GROUPS, lt[0:8], NEG_INF)
    gmax = gl.max(axis=0, keepdims=True)
    gidx = jnp.min(jnp.where(gl == gmax, sub8, 8.0), axis=0, keepdims=True)
    gsum = jnp.sum(jnp.exp(gl - gmax), axis=0, keepdims=True)
    sel = lt[8:16]
    for g in range(1, N_GROUPS):
        sel = jnp.where(gidx == g, lt[8 + 8 * g:16 + 8 * g], sel)
    t1 = sel.max(axis=0, keepdims=True)
    j1 = jnp.min(jnp.where(sel == t1, sub8, 8.0), axis=0, keepdims=True)
    sel2 = jnp.where(sub8 == j1, NEG_INF, sel)
    t2 = sel2.max(axis=0, keepdims=True)
    j2 = jnp.min(jnp.where(sel2 == t2, sub8, 8.0), axis=0, keepdims=True)
    ex = jnp.exp(t2 - t1)
    den = 1.0 + ex
    gw = 1.0 / gsum
    w0 = gw * (1.0 / den)
    w1 = gw * (ex / den)
    e0 = gidx * E_PER_G + j1
    e1 = gidx * E_PER_G + j2

    sub_e = lax.broadcasted_iota(I32, (NE, tl), 0).astype(F32)
    oh0 = (sub_e == e0).astype(F32)
    oh1 = (sub_e == e1).astype(F32)
    c = oh0 + oh1
    carry = cnt_ref[:, 0:1]
    pieces = []
    for ch in range(tl // 256):
        cc = c[:, ch * 256:(ch + 1) * 256]
        pieces.append(jnp.dot(cc.astype(BF16), tri_ref[...], preferred_element_type=F32) + carry)
        carry = carry + jnp.sum(cc, axis=1, keepdims=True)
    csum = jnp.concatenate(pieces, axis=1)
    cnt_ref[...] = jnp.broadcast_to(carry, (NE, LANES))
    cnt_out_ref[...] = jnp.broadcast_to(carry, (NE, LANES))
    r0 = jnp.sum(oh0 * csum, axis=0, keepdims=True) - 1.0
    r1 = jnp.sum(oh1 * csum, axis=0, keepdims=True) - 1.0
    e_ref[0:1, :] = e0.astype(I32)
    e_ref[1:2, :] = e1.astype(I32)
    rank_ref[0:1, :] = r0.astype(I32)
    rank_ref[1:2, :] = r1.astype(I32)
    sub_w = lax.broadcasted_iota(I32, (LANES, tl), 0)
    w8 = jnp.where(sub_w == 0, w0, jnp.where(sub_w == 1, w1, 0.0))
    wcol_ref[...] = w8.T


def _post_kernel(o_ref, xp_ref, xs_ref, mod_ref, wo_ref, g_ref, wr_hl_ref, wr_hi_ref, tri_ref,
                 x1_ref, h_ref, e_ref, rank_ref, wcol_ref, cnt_out_ref, cnt_ref):
    i = pl.program_id(0)

    @pl.when(i == 0)
    def _():
        cnt_ref[...] = jnp.zeros_like(cnt_ref)

    x = jnp.where(i >= NT_CTX, xs_ref[...], xp_ref[...])
    m = jnp.dot(o_ref[...], wo_ref[...], preferred_element_type=F32)
    x1 = x + mod_ref[0, 2:3, :] * m
    x1_ref[...] = x1
    h = _modulate(x1, g_ref[...] * (1.0 + mod_ref[0, 4:5, :]), mod_ref[0, 3:4, :])
    h_ref[...] = _pack_rows(h)
    _route(h, wr_hl_ref, wr_hi_ref, tri_ref, cnt_ref, e_ref, rank_ref, wcol_ref, cnt_out_ref)


def _router_weights(w_group, w_expert):
    w = jnp.zeros((D, LANES), F32)
    w = w.at[:, :N_GROUPS].set(w_group)
    w = w.at[:, 8:8 + NE].set(jnp.transpose(w_expert, (1, 0, 2)).reshape(D, NE))
    w_hi = w.astype(BF16)
    w_lo = (w - w_hi.astype(F32)).astype(BF16)
    return jnp.concatenate([w_hi, w_lo], axis=1), w_hi


def _route_outs(tile):
    specs = [
        pl.BlockSpec((tile, D), lambda i: (i, 0)),
        pl.BlockSpec((tile, HALF), lambda i: (i, 0)),
        pl.BlockSpec((TOP_K, tile), lambda i: (0, i)),
        pl.BlockSpec((TOP_K, tile), lambda i: (0, i)),
        pl.BlockSpec((tile, LANES), lambda i: (i, 0)),
        pl.BlockSpec((NE, LANES), lambda i: (0, 0)),
    ]
    shapes = [
        jax.ShapeDtypeStruct((T, D), F32),
        jax.ShapeDtypeStruct((T, HALF), U32),
        jax.ShapeDtypeStruct((TOP_K, T), I32),
        jax.ShapeDtypeStruct((TOP_K, T), I32),
        jax.ShapeDtypeStruct((T, LANES), F32),
        jax.ShapeDtypeStruct((NE, LANES), F32),
    ]
    return specs, shapes


def _tri():
    a = jnp.arange(256)
    return (a[:, None] <= a[None, :]).astype(BF16)


def _post(o, xp, xs, mod, w_o, g_ffn, wr_hl, wr_hi):
    const = lambda shape: pl.BlockSpec(shape, lambda i: (0,) * len(shape))
    out_specs, out_shape = _route_outs(TILE)
    return pl.pallas_call(
        _post_kernel,
        grid=(NT,),
        in_specs=[
            pl.BlockSpec((TILE, DQ), lambda i: (i, 0)),
            pl.BlockSpec((TILE, D), lambda i: (jnp.minimum(i, NT_CTX - 1), 0)),
            pl.BlockSpec((TILE, D), lambda i: (jnp.maximum(i - NT_CTX, 0), 0)),
            pl.BlockSpec((1, N_MOD, D), lambda i: (_cond_index(i, TILE), 0, 0)),
            const((DQ, D)), const((1, D)), const((D, 2 * LANES)), const((D, LANES)), const((256, 256)),
        ],
        out_specs=out_specs,
        out_shape=out_shape,
        scratch_shapes=[pltpu.VMEM((NE, LANES), F32)],
        compiler_params=_vmem_limit(56),
        name="post_attn",
    )(o, xp, xs, mod, w_o, g_ffn, wr_hl, wr_hi, _tri())


def _pool_kernel(y0_ref, y1_ref, xin_ref, wcin_ref, modp_ref, mod_ref, g_ref, pw_ref, ps_ref, gf_ref,
                 wr_hl_ref, wr_hi_ref, tri_ref,
                 x1_ref, h_ref, e_ref, rank_ref, wcol_ref, cnt_out_ref, cnt_ref, hhi_ref, hlo_ref):
    i = pl.program_id(0)

    @pl.when(i == 0)
    def _():
        cnt_ref[...] = jnp.zeros_like(cnt_ref)

    x = _combined(y0_ref, y1_ref, xin_ref, wcin_ref, modp_ref)
    h = _modulate(x, g_ref[...] * (1.0 + mod_ref[0, 1:2, :]), mod_ref[0, 0:1, :])
    @pl.when(i == 0)
    def _():
        zeros = jnp.zeros((PB_HALO, D), BF16)
        for ref in (hhi_ref, hlo_ref):
            ref[0:PB_HALO, :] = zeros
            ref[PB_HALO + PTILE:PB_HALO + PTILE + PB_HALO, :] = zeros

    h_hi = h.astype(BF16)
    hhi_ref[PB_HALO:PB_HALO + PTILE, :] = h_hi
    hlo_ref[PB_HALO:PB_HALO + PTILE, :] = (h - h_hi.astype(F32)).astype(BF16)

    seq = jnp.where(i >= T_CTX // PTILE, DEC_SEQ, SEQ)
    r_io = lax.broadcasted_iota(I32, (PB, PBW), 0)
    d = lax.broadcasted_iota(I32, (PB, PBW), 1) - PB_HALO - r_io
    r_col = lax.broadcasted_iota(I32, (PB, 1), 0)
    sums = [[None] * (PTILE // PB) for _ in POOL_WINDOWS]
    cnts = [[None] * (PTILE // PB) for _ in POOL_WINDOWS]
    for b in range(PTILE // PB):
        tm = (b * PB + r_io) & (seq - 1)
        inseq = jnp.logical_and(tm + d >= 0, tm + d < seq)
        tm_col = (b * PB + r_col) & (seq - 1)
        for gi, win in enumerate(POOL_WINDOWS):
            left = win // 2
            right = win - left - 1
            band = jnp.logical_and(inseq, jnp.logical_and(d >= -left, d <= right))
            bm = jnp.where(band, 1.0, 0.0).astype(BF16)
            cols = slice(gi * PGD, (gi + 1) * PGD)
            sums[gi][b] = (jnp.dot(bm, hhi_ref[b * PB:b * PB + PBW, cols], preferred_element_type=F32)
                           + jnp.dot(bm, hlo_ref[b * PB:b * PB + PBW, cols], preferred_element_type=F32))
            cnt = jnp.minimum(tm_col + right + 1, seq) - jnp.maximum(tm_col - left, 0)
            cnts[gi][b] = cnt.astype(F32)
    outs = []
    for gi in range(len(POOL_WINDOWS)):
        tot = jnp.concatenate(sums[gi], axis=0)
        cnt = jnp.concatenate(cnts[gi], axis=0)
        diff = tot / cnt - h[:, gi * PGD:(gi + 1) * PGD]
        outs.append(jnp.dot(diff.astype(BF16), pw_ref[gi], preferred_element_type=F32))
    m = jnp.concatenate(outs, axis=1) * ps_ref[...]
    x1 = x + mod_ref[0, 2:3, :] * m
    x1_ref[...] = x1
    hf = _modulate(x1, gf_ref[...] * (1.0 + mod_ref[0, 4:5, :]), mod_ref[0, 3:4, :])
    h_ref[...] = _pack_rows(hf)
    _route(hf, wr_hl_ref, wr_hi_ref, tri_ref, cnt_ref, e_ref, rank_ref, wcol_ref, cnt_out_ref)


def _pool(y2, x, wcol, mod_prev, mod, g_mix, pool_w, pool_scale, g_ffn, wr_hl, wr_hi):
    const = lambda shape: pl.BlockSpec(shape, lambda i: (0,) * len(shape))
    out_specs, out_shape = _route_outs(PTILE)
    return pl.pallas_call(
        _pool_kernel,
        grid=(T // PTILE,),
        in_specs=[
            pl.BlockSpec((PTILE, HALF), lambda i: (i, 0)),
            pl.BlockSpec((PTILE, HALF), lambda i: (T // PTILE + i, 0)),
            pl.BlockSpec((PTILE, D), lambda i: (i, 0)),
            pl.BlockSpec((PTILE, LANES), lambda i: (i, 0)),
            pl.BlockSpec((1, N_MOD, D), lambda i: (_cond_index(i, PTILE), 0, 0)),
            pl.BlockSpec((1, N_MOD, D), lambda i: (_cond_index(i, PTILE), 0, 0)),
            const((1, D)), const((len(POOL_WINDOWS), PGD, PGD)), const((1, D)), const((1, D)),
            const((D, 2 * LANES)), const((D, LANES)), const((256, 256)),
        ],
        out_specs=out_specs,
        out_shape=out_shape,
        scratch_shapes=[pltpu.VMEM((NE, LANES), F32), pltpu.VMEM((PTILE + 2 * PB_HALO, D), BF16),
                        pltpu.VMEM((PTILE + 2 * PB_HALO, D), BF16)],
        compiler_params=_vmem_limit(56),
        name="pool_mixer",
    )(y2, y2, x, wcol, mod_prev, mod, g_mix, pool_w, pool_scale, g_ffn, wr_hl, wr_hi, _tri())


def _plan(e, rank, cnt):
    counts = cnt[:, 0].astype(I32)
    n_tiles = (counts + TM - 1) // TM
    tile_end = jnp.cumsum(n_tiles)
    base = (tile_end - n_tiles) * TM
    pos = jnp.sum(jnp.where(e[None] == jnp.arange(NE, dtype=I32)[:, None, None], base[:, None, None], 0),
                  axis=0) + rank
    used = tile_end[-1]
    tile_id = jnp.minimum(jnp.arange(N_ROW_TILES, dtype=I32), used - 1)
    tile_expert = jnp.sum((tile_id[:, None] >= tile_end[None, :]).astype(I32), axis=1)
    experts = jnp.arange(NE, dtype=I32)
    owner = jnp.where(n_tiles > 0, experts, NE)
    later = jnp.flip(lax.cummin(jnp.flip(jnp.concatenate([owner[1:], jnp.full((1,), NE, I32)]))))
    next_owner = jnp.where(later < NE, later, experts)
    first = jnp.concatenate([jnp.ones((1,), bool), tile_expert[1:] != tile_expert[:-1]])
    weight_block = jnp.where(first, tile_expert, next_owner[tile_expert])
    return pos, tile_expert, weight_block, used.reshape(1)


def _sc_mesh():
    return plsc.VectorSubcoreMesh(core_axis_name="core", subcore_axis_name="subcore")


def _sc_worker():
    return lax.axis_index("core") * SC_SUBCORES + lax.axis_index("subcore")


def _sc_dispatch(h, pos):
    per = T // SC_WORKERS
    nwin = per // SC_WIN

    @functools.partial(
        pl.kernel, out_type=jax.ShapeDtypeStruct((NPAD, HALF), U32), mesh=_sc_mesh(),
        scratch_types=[pltpu.VMEM((2, SC_WIN, HALF), U32), pltpu.VMEM((2, SC_WIN), I32),
                       pltpu.VMEM((2, SC_WIN), I32), pltpu.SemaphoreType.DMA((2,)),
                       pltpu.SemaphoreType.DMA((2,))])
    def run(h_hbm, pos_hbm, xs_hbm, buf, idx0, idx1, sem_in, sem_out):
        first = _sc_worker() * per

        def loads(j, s):
            base = first + j * SC_WIN
            return (pltpu.make_async_copy(h_hbm.at[pl.ds(base, SC_WIN)], buf.at[s], sem_in.at[s]),
                    pltpu.make_async_copy(pos_hbm.at[pl.ds(base, SC_WIN)], idx0.at[s], sem_in.at[s]),
                    pltpu.make_async_copy(pos_hbm.at[pl.ds(T + base, SC_WIN)], idx1.at[s], sem_in.at[s]))

        def stores(s):
            return (pltpu.make_async_copy(buf.at[s], xs_hbm.at[idx0.at[s]], sem_out.at[s]),
                    pltpu.make_async_copy(buf.at[s], xs_hbm.at[idx1.at[s]], sem_out.at[s]))

        for c in loads(0, 0):
            c.start()

        @pl.loop(0, nwin // 2)
        def _(jj):
            for s in range(2):
                j = jj * 2 + s
                for c in loads(j, s):
                    c.wait()
                for c in stores(s):
                    c.start()

                @pl.when(j >= 1)
                def _():
                    for c in stores(1 - s):
                        c.wait()

                @pl.when(j + 1 < nwin)
                def _():
                    for c in loads(j + 1, 1 - s):
                        c.start()

        for c in stores((nwin - 1) % 2):
            c.wait()

    return run(h, pos.reshape(TOP_K * T))


def _sc_gather(ys, pos):
    n = TOP_K * T
    per = n // SC_WORKERS
    nwin = per // SC_WIN

    @functools.partial(
        pl.kernel, out_type=jax.ShapeDtypeStruct((n, HALF), U32), mesh=_sc_mesh(),
        scratch_types=[pltpu.VMEM((2, SC_WIN, HALF), U32), pltpu.VMEM((per,), I32),
                       pltpu.SemaphoreType.DMA((2,)), pltpu.SemaphoreType.DMA((2,))])
    def run(ys_hbm, pos_hbm, out_hbm, buf, idx, sem_in, sem_out):
        first = _sc_worker() * per
        pltpu.sync_copy(pos_hbm.at[pl.ds(first, per)], idx)

        def load(j, s):
            return pltpu.make_async_copy(ys_hbm.at[idx.at[pl.ds(j * SC_WIN, SC_WIN)]], buf.at[s], sem_in.at[s])

        def store(j, s):
            return pltpu.make_async_copy(buf.at[s], out_hbm.at[pl.ds(first + j * SC_WIN, SC_WIN)], sem_out.at[s])

        load(0, 0).start()

        @pl.loop(0, nwin // 2)
        def _(jj):
            for s in range(2):
                j = jj * 2 + s
                load(j, s).wait()
                store(j, s).start()

                @pl.when(j >= 1)
                def _():
                    store(j - 1, 1 - s).wait()

                @pl.when(j + 1 < nwin)
                def _():
                    load(j + 1, 1 - s).start()

        store(nwin - 1, (nwin - 1) % 2).wait()

    return run(ys, pos.reshape(n))


def _moe_kernel(te_ref, wb_ref, used_ref, x_ref, wg_ref, wu_ref, wd_ref, y_ref, wg_s, wu_s, wd_s):
    del wb_ref
    i = pl.program_id(0)

    @pl.when(i < used_ref[0])
    def _():
        prev = te_ref[jnp.maximum(i - 1, 0)]

        @pl.when(jnp.logical_or(i == 0, te_ref[i] != prev))
        def _():
            wg_s[...] = wg_ref[0, 0].astype(BF16)
            wu_s[...] = wu_ref[0, 0].astype(BF16)
            wd_s[...] = wd_ref[0, 0].astype(BF16)

        hi, lo = _unpack_rows(x_ref[...])
        x = jnp.concatenate([hi, lo], axis=1).astype(BF16)
        a = jnp.dot(x, wg_s[...], preferred_element_type=F32)
        b = jnp.dot(x, wu_s[...], preferred_element_type=F32)
        hid = a * (1.0 / (1.0 + jnp.exp(-a))) * b
        y_ref[...] = _pack_rows(jnp.dot(hid.astype(BF16), wd_s[...], preferred_element_type=F32))


def _experts(xs, tile_expert, weight_block, used, layer, w_gate, w_up, w_down):
    row = lambda i, te, wb, used: (jnp.minimum(i, used[0] - 1), 0)
    weights = lambda i, te, wb, used: (layer, wb[i], 0, 0)
    grid_spec = pltpu.PrefetchScalarGridSpec(
        num_scalar_prefetch=3,
        grid=(N_ROW_TILES,),
        in_specs=[
            pl.BlockSpec((TM, HALF), row),
            pl.BlockSpec((1, 1, D, DE), weights),
            pl.BlockSpec((1, 1, D, DE), weights),
            pl.BlockSpec((1, 1, DE, D), weights),
        ],
        out_specs=pl.BlockSpec((TM, HALF), row),
        scratch_shapes=[pltpu.VMEM((D, DE), BF16), pltpu.VMEM((D, DE), BF16), pltpu.VMEM((DE, D), BF16)],
    )
    return pl.pallas_call(
        _moe_kernel,
        grid_spec=grid_spec,
        out_shape=jax.ShapeDtypeStruct((NPAD, HALF), U32),
        compiler_params=_vmem_limit(40),
        name="moe_experts",
    )(tile_expert, weight_block, used, xs, w_gate, w_up, w_down)


def _combined(y0_ref, y1_ref, x_ref, wcol_ref, mod_ref):
    w0 = wcol_ref[:, 0:1]
    w1 = wcol_ref[:, 1:2]
    hi0, lo0 = _unpack_rows(y0_ref[...])
    hi1, lo1 = _unpack_rows(y1_ref[...])
    y = jnp.concatenate([w0 * hi0 + w1 * hi1, w0 * lo0 + w1 * lo1], axis=1)
    return x_ref[...] + mod_ref[0, 5:6, :] * y


def _combine_final_kernel(y0_ref, y1_ref, x_ref, wcol_ref, mod_ref, op_ref, os_ref):
    i = pl.program_id(0)
    out = _combined(y0_ref, y1_ref, x_ref, wcol_ref, mod_ref)

    @pl.when(i < T_CTX // CT)
    def _():
        op_ref[...] = out

    @pl.when(i >= T_CTX // CT)
    def _():
        os_ref[...] = out


def _combine(y2, x, wcol, mod):
    n_ctx = T_CTX // CT
    return pl.pallas_call(
        _combine_final_kernel,
        grid=(T // CT,),
        in_specs=[
            pl.BlockSpec((CT, HALF), lambda i: (i, 0)),
            pl.BlockSpec((CT, HALF), lambda i: (T // CT + i, 0)),
            pl.BlockSpec((CT, D), lambda i: (i, 0)),
            pl.BlockSpec((CT, LANES), lambda i: (i, 0)),
            pl.BlockSpec((1, N_MOD, D), lambda i: (_cond_index(i, CT), 0, 0)),
        ],
        out_specs=[
            pl.BlockSpec((CT, D), lambda i: (jnp.minimum(i, n_ctx - 1), 0)),
            pl.BlockSpec((CT, D), lambda i: (jnp.maximum(i - n_ctx, 0), 0)),
        ],
        out_shape=[jax.ShapeDtypeStruct((T_CTX, D), F32), jax.ShapeDtypeStruct((T_LAT, D), F32)],
        name="moe_combine_final",
    )(y2, y2, x, wcol, mod)


def _moe(h, e, rank, cnt, layer, w_gate, w_up, w_down):
    pos, tile_expert, weight_block, used = _plan(e, rank, cnt)
    xs = _sc_dispatch(h, pos)
    ys = _experts(xs, tile_expert, weight_block, used, layer, w_gate, w_up, w_down)
    return _sc_gather(ys, pos)


def kernel(x_prompt, x_sample, c, cache_k, cache_v, c_ctx, norm_mix_g, norm_ffn_g, ada_w, ada_b, attn_w_qkv, attn_q_norm, attn_k_norm, attn_w_o, pool_w, pool_scale, moe_w_group, moe_w_expert, moe_w_gate, moe_w_up, moe_w_down):
    xp = x_prompt.reshape(T_CTX, D)
    xs = x_sample.reshape(T_LAT, D)
    cond = jnp.concatenate([c_ctx[None, :], c, jnp.zeros((N_COND - 1 - DEC_BATCH, D), F32)], axis=0)
    mod = _ada(cond, ada_w, ada_b)

    q, kt2, v2, new_k, new_v = _qkv(xp, xs, mod[0], norm_mix_g[0:1], attn_w_qkv[0].astype(BF16),
                                    attn_q_norm[0:1], attn_k_norm[0:1])
    o = _attention(q, kt2, v2, cache_k[:, 0].reshape(DEC_BATCH, PAST, DKV),
                   cache_v[:, 0].reshape(DEC_BATCH, PAST, DKV))
    wr_hl, wr_hi = _router_weights(moe_w_group[0], moe_w_expert[0])
    x1, h, e, rank, wcol, cnt = _post(o, xp, xs, mod[0], attn_w_o[0].astype(BF16), norm_ffn_g[0:1],
                                      wr_hl, wr_hi)
    y2 = _moe(h, e, rank, cnt, 0, moe_w_gate, moe_w_up, moe_w_down)

    wr_hl, wr_hi = _router_weights(moe_w_group[1], moe_w_expert[1])
    x3, h, e, rank, wcol, cnt = _pool(y2, x1, wcol, mod[0], mod[1], norm_mix_g[1:2], pool_w[0].astype(BF16),
                                      pool_scale[0:1], norm_ffn_g[1:2], wr_hl, wr_hi)
    y2 = _moe(h, e, rank, cnt, 1, moe_w_gate, moe_w_up, moe_w_down)
    y_prompt, y_sample = _combine(y2, x3, wcol, mod[1])

    return (y_prompt.reshape(BATCH, SEQ, D), y_sample.reshape(DEC_BATCH, DEC_SEQ, D),
            new_k.reshape(BATCH, 1, SEQ, N_KV, HD), new_v.reshape(BATCH, 1, SEQ, N_KV, HD))
```

```python
import functools

import jax
import jax.numpy as jnp
from jax import lax
from jax.experimental import pallas as pl
from jax.experimental.pallas import tpu as pltpu
from jax.experimental.pallas import tpu_sc as plsc

F32 = jnp.float32
BF16 = jnp.bfloat16
I32 = jnp.int32
U32 = jnp.uint32

D = 1024
BATCH, SEQ = 32, 256
DEC_BATCH, DEC_SEQ, PAST = 8, 1024, 512
T_CTX = BATCH * SEQ
T_LAT = DEC_BATCH * DEC_SEQ
T = T_CTX + T_LAT
GRID_W = 64
N_HEADS, N_KV, HD = 16, 4, 64
DQ = N_HEADS * HD
DKV = N_KV * HD
ROPE_THETA = 10000.0
POOL_WINDOWS = (2, 4, 8, 16)
PGD = D // len(POOL_WINDOWS)
N_GROUPS, E_PER_G, TOP_K = 4, 8, 2
NE = N_GROUPS * E_PER_G
DE = D // 4
N_MOD = 6
EPS = 1e-6
N_COND = 16

TILE = 512
NT = T // TILE
NT_CTX = T_CTX // TILE
PTILE = 1024
PB = 128
PBW = 256
PB_HALO = (PBW - PB) // 2
TM = 512
N_ROW_TILES = (TOP_K * T) // TM + NE
NPAD = N_ROW_TILES * TM
CT = 512
LANES = 128
NEG_INF = float("-inf")
LOG2_E = 1.4426950408889634
Q_SCALE = HD ** -0.5 * LOG2_E
HALF = D // 2
SC_CORES, SC_SUBCORES = 2, 16
SC_WORKERS = SC_CORES * SC_SUBCORES
SC_WIN = 64


def _vmem_limit(mib):
    return pltpu.CompilerParams(vmem_limit_bytes=mib * 1024 * 1024)


def _modulate(x, gain_scale, shift):
    ms = jnp.mean(x * x, axis=-1, keepdims=True)
    return (x * lax.rsqrt(ms + EPS)) * gain_scale + shift


def _head_norm(z, gain, ind, ind_t2):
    ss = jnp.dot((z * z).astype(BF16), ind, preferred_element_type=F32)
    inv = lax.rsqrt(ss * (1.0 / HD) + EPS)
    inv_hi = inv.astype(BF16)
    inv_lo = (inv - inv_hi.astype(F32)).astype(BF16)
    scale = jnp.dot(jnp.concatenate([inv_hi, inv_lo], axis=1), ind_t2, preferred_element_type=F32)
    return z * scale * gain


def _rope(z, cos_t, sin_t):
    lane = lax.broadcasted_iota(I32, (z.shape[0], LANES), 1)
    low = (lane % 32) < 16
    outs = []
    for c in range(z.shape[1] // LANES):
        zc = z[:, c * LANES:(c + 1) * LANES]
        up = pltpu.roll(zc, 16, axis=1)
        dn = pltpu.roll(zc, LANES - 16, axis=1)
        outs.append(zc * cos_t + jnp.where(low, dn, up) * sin_t)
    return jnp.concatenate(outs, axis=1)


def _pack_rows(z):
    bits = lax.bitcast_convert_type(z.astype(BF16).astype(F32), U32)
    return bits[:, :HALF] | (bits[:, HALF:] >> 16)


def _unpack_rows(p):
    hi = lax.bitcast_convert_type(p & jnp.uint32(0xFFFF0000), F32)
    lo = lax.bitcast_convert_type(p << 16, F32)
    return hi, lo


def _ada_kernel(c_ref, w_ref, b_ref, o_ref):
    c = c_ref[...]
    a = c * (1.0 / (1.0 + jnp.exp(-c)))
    o_ref[0] = jnp.dot(a.astype(BF16), w_ref[0].astype(BF16), preferred_element_type=F32) + b_ref[0]


def _ada(cond, ada_w, ada_b):
    nb = 1536
    depth = ada_w.shape[0]
    out = pl.pallas_call(
        _ada_kernel,
        grid=(depth, (N_MOD * D) // nb),
        in_specs=[
            pl.BlockSpec((N_COND, D), lambda l, j: (0, 0)),
            pl.BlockSpec((1, D, nb), lambda l, j: (l, 0, j)),
            pl.BlockSpec((1, 1, nb), lambda l, j: (l, 0, j)),
        ],
        out_specs=pl.BlockSpec((1, N_COND, nb), lambda l, j: (l, 0, j)),
        out_shape=jax.ShapeDtypeStruct((depth, N_COND, N_MOD * D), F32),
        compiler_params=_vmem_limit(40),
        name="ada",
    )(cond, ada_w, ada_b.reshape(depth, 1, N_MOD * D))
    return out.reshape(depth, N_COND, N_MOD, D)


def _cond_index(i, tile):
    n_ctx = T_CTX // tile
    per_batch = DEC_SEQ // tile
    return jnp.where(i < n_ctx, 0, 1 + (i - n_ctx) // per_batch)


def _qkv_kernel(xp_ref, xs_ref, mod_ref, g_ref, w_ref, qg_ref, kg_ref, indq_ref, indqt_ref,
                indk_ref, indkt_ref, cos_ref, sin_ref,
                qt_ref, k2_ref, vt_ref, nk_ref, nv_ref):
    i = pl.program_id(0)
    is_lat = i >= NT_CTX
    x = jnp.where(is_lat, xs_ref[...], xp_ref[...])
    h = _modulate(x, g_ref[...] * (1.0 + mod_ref[0, 1:2, :]), mod_ref[0, 0:1, :])
    qkv = jnp.dot(h.astype(BF16), w_ref[...], preferred_element_type=F32)
    q = _head_norm(qkv[:, :DQ], qg_ref[...], indq_ref[...], indqt_ref[...])
    k = _head_norm(qkv[:, DQ:DQ + DKV], kg_ref[...], indk_ref[...], indkt_ref[...])
    v = qkv[:, DQ + DKV:]

    vt_ref[...] = v.T.astype(BF16)

    def store(qn, kn):
        qt_ref[...] = (qn * Q_SCALE).T.astype(BF16)
        k2_ref[...] = _dup_heads(kn).astype(BF16)

    @pl.when(is_lat)
    def _():
        store(_rope(q, cos_ref[...], sin_ref[...]), _rope(k, cos_ref[...], sin_ref[...]))

    @pl.when(jnp.logical_not(is_lat))
    def _():
        store(q, k)
        nk_ref[...] = k
        nv_ref[...] = v


def _rope_tables():
    rows = DEC_SEQ // GRID_W
    row = jnp.broadcast_to(jnp.arange(rows, dtype=F32)[:, None], (rows, GRID_W)).reshape(-1)
    col = jnp.broadcast_to(jnp.arange(GRID_W, dtype=F32)[None, :], (rows, GRID_W)).reshape(-1)
    axis_dim = HD // 2
    inv_freq = jnp.power(ROPE_THETA, -jnp.arange(0, axis_dim, 2, dtype=F32) / axis_dim)
    ang = jnp.concatenate([row[:, None] * inv_freq, col[:, None] * inv_freq], axis=-1)
    cos, sin = jnp.cos(ang), jnp.sin(ang)
    quarter = HD // 4
    cos_h = jnp.concatenate([cos[:, :quarter], cos[:, :quarter], cos[:, quarter:], cos[:, quarter:]], axis=1)
    sin_h = jnp.concatenate([-sin[:, :quarter], sin[:, :quarter], -sin[:, quarter:], sin[:, quarter:]], axis=1)
    return jnp.tile(cos_h, (1, LANES // HD)), jnp.tile(sin_h, (1, LANES // HD))


def _head_indicators(width):
    col = jnp.arange(width)[:, None]
    head = jnp.arange(LANES)[None, :]
    ind = (col // HD == head).astype(BF16)
    return ind, jnp.concatenate([ind.T, ind.T], axis=0)


def _dup_heads(z):
    lane = lax.broadcasted_iota(I32, (z.shape[0], LANES), 1)
    cols = []
    for p in range(N_KV // 2):
        blk = z[:, p * LANES:(p + 1) * LANES]
        swp = pltpu.roll(blk, HD, axis=1)
        cols += [jnp.where(lane < HD, blk, swp), jnp.where(lane < HD, swp, blk)]
    return jnp.concatenate(cols, axis=1)


def _qkv(xp, xs, mod, g, w_qkv, q_gain, k_gain):
    indq, indqt = _head_indicators(DQ)
    indk, indkt = _head_indicators(DKV)
    cos_t, sin_t = _rope_tables()
    per_batch = DEC_SEQ // TILE
    const = lambda shape: pl.BlockSpec(shape, lambda i: (0,) * len(shape))
    return pl.pallas_call(
        _qkv_kernel,
        grid=(NT,),
        in_specs=[
            pl.BlockSpec((TILE, D), lambda i: (jnp.minimum(i, NT_CTX - 1), 0)),
            pl.BlockSpec((TILE, D), lambda i: (jnp.maximum(i - NT_CTX, 0), 0)),
            pl.BlockSpec((1, N_MOD, D), lambda i: (_cond_index(i, TILE), 0, 0)),
            const((1, D)),
            const((D, DQ + 2 * DKV)),
            const((1, DQ)), const((1, DKV)),
            const((DQ, LANES)), const((2 * LANES, DQ)),
            const((DKV, LANES)), const((2 * LANES, DKV)),
            pl.BlockSpec((TILE, LANES), lambda i: (jnp.maximum(i - NT_CTX, 0) % per_batch, 0)),
            pl.BlockSpec((TILE, LANES), lambda i: (jnp.maximum(i - NT_CTX, 0) % per_batch, 0)),
        ],
        out_specs=[
            pl.BlockSpec((DQ, TILE), lambda i: (0, i)),
            pl.BlockSpec((TILE, 2 * DKV), lambda i: (i, 0)),
            pl.BlockSpec((DKV, TILE), lambda i: (0, i)),
            pl.BlockSpec((TILE, DKV), lambda i: (jnp.minimum(i, NT_CTX - 1), 0)),
            pl.BlockSpec((TILE, DKV), lambda i: (jnp.minimum(i, NT_CTX - 1), 0)),
        ],
        out_shape=[
            jax.ShapeDtypeStruct((DQ, T), BF16),
            jax.ShapeDtypeStruct((T, 2 * DKV), BF16),
            jax.ShapeDtypeStruct((DKV, T), BF16),
            jax.ShapeDtypeStruct((T_CTX, DKV), F32),
            jax.ShapeDtypeStruct((T_CTX, DKV), F32),
        ],
        compiler_params=_vmem_limit(56),
        name="qkv",
    )(xp, xs, mod, g, w_qkv, jnp.tile(q_gain, (1, N_HEADS)), jnp.tile(k_gain, (1, N_KV)),
      indq, indqt, indk, indkt, cos_t, sin_t)


def _group_scores(g, k2_parts, qt_ref):
    cols = []
    for j in range(2):
        qt_pair = qt_ref[(2 * g + j) * LANES:(2 * g + j + 1) * LANES, :]
        sub = lax.broadcasted_iota(I32, qt_pair.shape, 0)
        zero = jnp.zeros_like(qt_pair)
        cols += [jnp.where(sub < HD, qt_pair, zero), jnp.where(sub >= HD, qt_pair, zero)]
    qt4 = jnp.concatenate(cols, axis=1)
    return [jnp.dot(k2, qt4, preferred_element_type=F32) for k2 in k2_parts]


def _softmax_values(ss, vt_parts):
    m = ss[0].max(axis=0, keepdims=True)
    for s in ss[1:]:
        m = jnp.maximum(m, s.max(axis=0, keepdims=True))
    acc = None
    l = None
    for s, vt in zip(ss, vt_parts):
        e = jnp.exp2(s - m)
        ls = e.sum(axis=0, keepdims=True)
        l = ls if l is None else l + ls
        t = jnp.dot(vt, e.astype(BF16), preferred_element_type=F32)
        acc = t if acc is None else acc + t
    return acc * (1.0 / l)


def _heads_to_rows(groups, lq):
    heads = [o[:, h * lq:(h + 1) * lq] for o in groups for h in range(N_HEADS // N_KV)]
    return jnp.concatenate(heads, axis=0).T


def _attend(keys_of, values_of, qt_ref, o_ref):
    groups = []
    ss = _group_scores(0, keys_of(0), qt_ref)
    for g in range(N_KV):
        ss_next = _group_scores(g + 1, keys_of(g + 1), qt_ref) if g + 1 < N_KV else None
        groups.append(_softmax_values(ss, values_of(g)))
        ss = ss_next
    o_ref[...] = _heads_to_rows(groups, o_ref.shape[0]).astype(BF16)


def _attn_ctx_kernel(qt_ref, k2_ref, vt_ref, o_ref):
    _attend(lambda g: [k2_ref[:, g * LANES:(g + 1) * LANES]],
            lambda g: [vt_ref[g * HD:(g + 1) * HD, :]], qt_ref, o_ref)


def _attn_lat_kernel(qt_ref, k2c_ref, vtc_ref, k2n_ref, vtn_ref, o_in_ref, o_ref):
    del o_in_ref
    _attend(lambda g: [k2c_ref[0, :, g * LANES:(g + 1) * LANES], k2n_ref[:, g * LANES:(g + 1) * LANES]],
            lambda g: [vtc_ref[0, g * HD:(g + 1) * HD, :], vtn_ref[g * HD:(g + 1) * HD, :]], qt_ref, o_ref)


def _cache_kernel(ck_ref, cv_ref, k2_ref, vt_ref):
    k2_ref[0] = _dup_heads(ck_ref[0]).astype(BF16)
    vt_ref[0] = cv_ref[0].T.astype(BF16)


def _attention(qt, k2, vt, cache_k, cache_v):
    k2c, vtc = pl.pallas_call(
        _cache_kernel,
        grid=(DEC_BATCH,),
        in_specs=[
            pl.BlockSpec((1, PAST, DKV), lambda b: (b, 0, 0)),
            pl.BlockSpec((1, PAST, DKV), lambda b: (b, 0, 0)),
        ],
        out_specs=[
            pl.BlockSpec((1, PAST, 2 * DKV), lambda b: (b, 0, 0)),
            pl.BlockSpec((1, DKV, PAST), lambda b: (b, 0, 0)),
        ],
        out_shape=[
            jax.ShapeDtypeStruct((DEC_BATCH, PAST, 2 * DKV), BF16),
            jax.ShapeDtypeStruct((DEC_BATCH, DKV, PAST), BF16),
        ],
        name="cache_prep",
    )(cache_k, cache_v)

    o_ctx = pl.pallas_call(
        _attn_ctx_kernel,
        grid=(BATCH,),
        in_specs=[
            pl.BlockSpec((DQ, SEQ), lambda b: (0, b)),
            pl.BlockSpec((SEQ, 2 * DKV), lambda b: (b, 0)),
            pl.BlockSpec((DKV, SEQ), lambda b: (0, b)),
        ],
        out_specs=pl.BlockSpec((SEQ, DQ), lambda b: (b, 0)),
        out_shape=jax.ShapeDtypeStruct((T, DQ), BF16),
        name="attn_ctx",
    )(qt, k2, vt)

    qb = 256
    nqb = DEC_SEQ // qb
    lat0 = T_CTX // qb
    return pl.pallas_call(
        _attn_lat_kernel,
        grid=(DEC_BATCH, nqb),
        in_specs=[
            pl.BlockSpec((DQ, qb), lambda b, j: (0, lat0 + b * nqb + j)),
            pl.BlockSpec((1, PAST, 2 * DKV), lambda b, j: (b, 0, 0)),
            pl.BlockSpec((1, DKV, PAST), lambda b, j: (b, 0, 0)),
            pl.BlockSpec((DEC_SEQ, 2 * DKV), lambda b, j: (T_CTX // DEC_SEQ + b, 0)),
            pl.BlockSpec((DKV, DEC_SEQ), lambda b, j: (0, T_CTX // DEC_SEQ + b)),
            pl.BlockSpec(memory_space=pl.ANY),
        ],
        out_specs=pl.BlockSpec((qb, DQ), lambda b, j: (lat0 + b * nqb + j, 0)),
        out_shape=jax.ShapeDtypeStruct((T, DQ), BF16),
        input_output_aliases={5: 0},
        compiler_params=_vmem_limit(48),
        name="attn_lat",
    )(qt, k2c, vtc, k2, vt, o_ctx)


def _route(h, wr_hl_ref, wr_hi_ref, tri_ref, cnt_ref, e_ref, rank_ref, wcol_ref, cnt_out_ref):
    tl = h.shape[0]
    h_hi = h.astype(BF16)
    h_lo = (h - h_hi.astype(F32)).astype(BF16)
    lg2 = jnp.dot(h_hi, wr_hl_ref[...], preferred_element_type=F32)
    lg = lg2[:, :LANES] + lg2[:, LANES:] + jnp.dot(h_lo, wr_hi_ref[...], preferred_element_type=F32)
    lt = lg.T

    sub8 = lax.broadcasted_iota(I32, (8, tl), 0).astype(F32)
    gl = jnp.where(sub8 < N_GROUPS, lt[0:8], NEG_INF)
    gmax = gl.max(axis=0, keepdims=True)
    gidx = jnp.min(jnp.where(gl == gmax, sub8, 8.0), axis=0, keepdims=True)
    gsum = jnp.sum(jnp.exp(gl - gmax), axis=0, keepdims=True)
    sel = lt[8:16]
    for g in range(1, N_GROUPS):
        sel = jnp.where(gidx == g, lt[8 + 8 * g:16 + 8 * g], sel)
    t1 = sel.max(axis=0, keepdims=True)
    j1 = jnp.min(jnp.where(sel == t1, sub8, 8.0), axis=0, keepdims=True)
    sel2 = jnp.where(sub8 == j1, NEG_INF, sel)
    t2 = sel2.max(axis=0, keepdims=True)
    j2 = jnp.min(jnp.where(sel2 == t2, sub8, 8.0), axis=0, keepdims=True)
    ex = jnp.exp(t2 - t1)
    den = 1.0 + ex
    gw = 1.0 / gsum
    w0 = gw * (1.0 / den)
    w1 = gw * (ex / den)
    e0 = gidx * E_PER_G + j1
    e1 = gidx * E_PER_G + j2

    sub_e = lax.broadcasted_iota(I32, (NE, tl), 0).astype(F32)
    oh0 = (sub_e == e0).astype(F32)
    oh1 = (sub_e == e1).astype(F32)
    c = oh0 + oh1
    carry = cnt_ref[:, 0:1]
    pieces = []
    for ch in range(tl // 256):
        cc = c[:, ch * 256:(ch + 1) * 256]
        pieces.append(jnp.dot(cc.astype(BF16), tri_ref[...], preferred_element_type=F32) + carry)
        carry = carry + jnp.sum(cc, axis=1, keepdims=True)
    csum = jnp.concatenate(pieces, axis=1)
    cnt_ref[...] = jnp.broadcast_to(carry, (NE, LANES))
    cnt_out_ref[...] = jnp.broadcast_to(carry, (NE, LANES))
    r0 = jnp.sum(oh0 * csum, axis=0, keepdims=True) - 1.0
    r1 = jnp.sum(oh1 * csum, axis=0, keepdims=True) - 1.0
    e_ref[0:1, :] = e0.astype(I32)
    e_ref[1:2, :] = e1.astype(I32)
    rank_ref[0:1, :] = r0.astype(I32)
    rank_ref[1:2, :] = r1.astype(I32)
    sub_w = lax.broadcasted_iota(I32, (LANES, tl), 0)
    w8 = jnp.where(sub_w == 0, w0, jnp.where(sub_w == 1, w1, 0.0))
    wcol_ref[...] = w8.T


def _post_kernel(o_ref, xp_ref, xs_ref, mod_ref, wo_ref, g_ref, wr_hl_ref, wr_hi_ref, tri_ref,
                 x1_ref, h_ref, e_ref, rank_ref, wcol_ref, cnt_out_ref, cnt_ref):
    i = pl.program_id(0)

    @pl.when(i == 0)
    def _():
        cnt_ref[...] = jnp.zeros_like(cnt_ref)

    x = jnp.where(i >= NT_CTX, xs_ref[...], xp_ref[...])
    m = jnp.dot(o_ref[...], wo_ref[...], preferred_element_type=F32)
    x1 = x + mod_ref[0, 2:3, :] * m
    x1_ref[...] = x1
    h = _modulate(x1, g_ref[...] * (1.0 + mod_ref[0, 4:5, :]), mod_ref[0, 3:4, :])
    h_ref[...] = _pack_rows(h)
    _route(h, wr_hl_ref, wr_hi_ref, tri_ref, cnt_ref, e_ref, rank_ref, wcol_ref, cnt_out_ref)


def _router_weights(w_group, w_expert):
    w = jnp.zeros((D, LANES), F32)
    w = w.at[:, :N_GROUPS].set(w_group)
    w = w.at[:, 8:8 + NE].set(jnp.transpose(w_expert, (1, 0, 2)).reshape(D, NE))
    w_hi = w.astype(BF16)
    w_lo = (w - w_hi.astype(F32)).astype(BF16)
    return jnp.concatenate([w_hi, w_lo], axis=1), w_hi


def _route_outs(tile):
    specs = [
        pl.BlockSpec((tile, D), lambda i: (i, 0)),
        pl.BlockSpec((tile, HALF), lambda i: (i, 0)),
        pl.BlockSpec((TOP_K, tile), lambda i: (0, i)),
        pl.BlockSpec((TOP_K, tile), lambda i: (0, i)),
        pl.BlockSpec((tile, LANES), lambda i: (i, 0)),
        pl.BlockSpec((NE, LANES), lambda i: (0, 0)),
    ]
    shapes = [
        jax.ShapeDtypeStruct((T, D), F32),
        jax.ShapeDtypeStruct((T, HALF), U32),
        jax.ShapeDtypeStruct((TOP_K, T), I32),
        jax.ShapeDtypeStruct((TOP_K, T), I32),
        jax.ShapeDtypeStruct((T, LANES), F32),
        jax.ShapeDtypeStruct((NE, LANES), F32),
    ]
    return specs, shapes


def _tri():
    a = jnp.arange(256)
    return (a[:, None] <= a[None, :]).astype(BF16)


def _post(o, xp, xs, mod, w_o, g_ffn, wr_hl, wr_hi):
    const = lambda shape: pl.BlockSpec(shape, lambda i: (0,) * len(shape))
    out_specs, out_shape = _route_outs(TILE)
    return pl.pallas_call(
        _post_kernel,
        grid=(NT,),
        in_specs=[
            pl.BlockSpec((TILE, DQ), lambda i: (i, 0)),
            pl.BlockSpec((TILE, D), lambda i: (jnp.minimum(i, NT_CTX - 1), 0)),
            pl.BlockSpec((TILE, D), lambda i: (jnp.maximum(i - NT_CTX, 0), 0)),
            pl.BlockSpec((1, N_MOD, D), lambda i: (_cond_index(i, TILE), 0, 0)),
            const((DQ, D)), const((1, D)), const((D, 2 * LANES)), const((D, LANES)), const((256, 256)),
        ],
        out_specs=out_specs,
        out_shape=out_shape,
        scratch_shapes=[pltpu.VMEM((NE, LANES), F32)],
        compiler_params=_vmem_limit(56),
        name="post_attn",
    )(o, xp, xs, mod, w_o, g_ffn, wr_hl, wr_hi, _tri())


def _pool_kernel(y0_ref, y1_ref, xin_ref, wcin_ref, modp_ref, mod_ref, g_ref, pw_ref, ps_ref, gf_ref,
                 wr_hl_ref, wr_hi_ref, tri_ref,
                 x1_ref, h_ref, e_ref, rank_ref, wcol_ref, cnt_out_ref, cnt_ref, hhi_ref, hlo_ref):
    i = pl.program_id(0)

    @pl.when(i == 0)
    def _():
        cnt_ref[...] = jnp.zeros_like(cnt_ref)

    x = _combined(y0_ref, y1_ref, xin_ref, wcin_ref, modp_ref)
    h = _modulate(x, g_ref[...] * (1.0 + mod_ref[0, 1:2, :]), mod_ref[0, 0:1, :])
    @pl.when(i == 0)
    def _():
        zeros = jnp.zeros((PB_HALO, D), BF16)
        for ref in (hhi_ref, hlo_ref):
            ref[0:PB_HALO, :] = zeros
            ref[PB_HALO + PTILE:PB_HALO + PTILE + PB_HALO, :] = zeros

    h_hi = h.astype(BF16)
    hhi_ref[PB_HALO:PB_HALO + PTILE, :] = h_hi
    hlo_ref[PB_HALO:PB_HALO + PTILE, :] = (h - h_hi.astype(F32)).astype(BF16)

    seq = jnp.where(i >= T_CTX // PTILE, DEC_SEQ, SEQ)
    r_io = lax.broadcasted_iota(I32, (PB, PBW), 0)
    d = lax.broadcasted_iota(I32, (PB, PBW), 1) - PB_HALO - r_io
    r_col = lax.broadcasted_iota(I32, (PB, 1), 0)
    sums = [[None] * (PTILE // PB) for _ in POOL_WINDOWS]
    cnts = [[None] * (PTILE // PB) for _ in POOL_WINDOWS]
    for b in range(PTILE // PB):
        tm = (b * PB + r_io) & (seq - 1)
        inseq = jnp.logical_and(tm + d >= 0, tm + d < seq)
        tm_col = (b * PB + r_col) & (seq - 1)
        for gi, win in enumerate(POOL_WINDOWS):
            left = win // 2
            right = win - left - 1
            band = jnp.logical_and(inseq, jnp.logical_and(d >= -left, d <= right))
            bm = jnp.where(band, 1.0, 0.0).astype(BF16)
            cols = slice(gi * PGD, (gi + 1) * PGD)
            sums[gi][b] = (jnp.dot(bm, hhi_ref[b * PB:b * PB + PBW, cols], preferred_element_type=F32)
                           + jnp.dot(bm, hlo_ref[b * PB:b * PB + PBW, cols], preferred_element_type=F32))
            cnt = jnp.minimum(tm_col + right + 1, seq) - jnp.maximum(tm_col - left, 0)
            cnts[gi][b] = cnt.astype(F32)
    outs = []
    for gi in range(len(POOL_WINDOWS)):
        tot = jnp.concatenate(sums[gi], axis=0)
        cnt = jnp.concatenate(cnts[gi], axis=0)
        diff = tot / cnt - h[:, gi * PGD:(gi + 1) * PGD]
        outs.append(jnp.dot(diff.astype(BF16), pw_ref[gi], preferred_element_type=F32))
    m = jnp.concatenate(outs, axis=1) * ps_ref[...]
    x1 = x + mod_ref[0, 2:3, :] * m
    x1_ref[...] = x1
    hf = _modulate(x1, gf_ref[...] * (1.0 + mod_ref[0, 4:5, :]), mod_ref[0, 3:4, :])
    h_ref[...] = _pack_rows(hf)
    _route(hf, wr_hl_ref, wr_hi_ref, tri_ref, cnt_ref, e_ref, rank_ref, wcol_ref, cnt_out_ref)


def _pool(y2, x, wcol, mod_prev, mod, g_mix, pool_w, pool_scale, g_ffn, wr_hl, wr_hi):
    const = lambda shape: pl.BlockSpec(shape, lambda i: (0,) * len(shape))
    out_specs, out_shape = _route_outs(PTILE)
    return pl.pallas_call(
        _pool_kernel,
        grid=(T // PTILE,),
        in_specs=[
            pl.BlockSpec((PTILE, HALF), lambda i: (i, 0)),
            pl.BlockSpec((PTILE, HALF), lambda i: (T // PTILE + i, 0)),
            pl.BlockSpec((PTILE, D), lambda i: (i, 0)),
            pl.BlockSpec((PTILE, LANES), lambda i: (i, 0)),
            pl.BlockSpec((1, N_MOD, D), lambda i: (_cond_index(i, PTILE), 0, 0)),
            pl.BlockSpec((1, N_MOD, D), lambda i: (_cond_index(i, PTILE), 0, 0)),
            const((1, D)), const((len(POOL_WINDOWS), PGD, PGD)), const((1, D)), const((1, D)),
            const((D, 2 * LANES)), const((D, LANES)), const((256, 256)),
        ],
        out_specs=out_specs,
        out_shape=out_shape,
        scratch_shapes=[pltpu.VMEM((NE, LANES), F32), pltpu.VMEM((PTILE + 2 * PB_HALO, D), BF16),
                        pltpu.VMEM((PTILE + 2 * PB_HALO, D), BF16)],
        compiler_params=_vmem_limit(56),
        name="pool_mixer",
    )(y2, y2, x, wcol, mod_prev, mod, g_mix, pool_w, pool_scale, g_ffn, wr_hl, wr_hi, _tri())


def _plan(e, rank, cnt):
    counts = cnt[:, 0].astype(I32)
    n_tiles = (counts + TM - 1) // TM
    tile_end = jnp.cumsum(n_tiles)
    base = (tile_end - n_tiles) * TM
    pos = jnp.sum(jnp.where(e[None] == jnp.arange(NE, dtype=I32)[:, None, None], base[:, None, None], 0),
                  axis=0) + rank
    used = tile_end[-1]
    tile_id = jnp.minimum(jnp.arange(N_ROW_TILES, dtype=I32), used - 1)
    tile_expert = jnp.sum((tile_id[:, None] >= tile_end[None, :]).astype(I32), axis=1)
    experts = jnp.arange(NE, dtype=I32)
    owner = jnp.where(n_tiles > 0, experts, NE)
    later = jnp.flip(lax.cummin(jnp.flip(jnp.concatenate([owner[1:], jnp.full((1,), NE, I32)]))))
    next_owner = jnp.where(later < NE, later, experts)
    first = jnp.concatenate([jnp.ones((1,), bool), tile_expert[1:] != tile_expert[:-1]])
    weight_block = jnp.where(first, tile_expert, next_owner[tile_expert])
    return pos, tile_expert, weight_block, used.reshape(1)


def _sc_mesh():
    return plsc.VectorSubcoreMesh(core_axis_name="core", subcore_axis_name="subcore")


def _sc_worker():
    return lax.axis_index("core") * SC_SUBCORES + lax.axis_index("subcore")


def _sc_dispatch(h, pos):
    per = T // SC_WORKERS
    nwin = per // SC_WIN

    @functools.partial(
        pl.kernel, out_type=jax.ShapeDtypeStruct((NPAD, HALF), U32), mesh=_sc_mesh(),
        scratch_types=[pltpu.VMEM((2, SC_WIN, HALF), U32), pltpu.VMEM((2, SC_WIN), I32),
                       pltpu.VMEM((2, SC_WIN), I32), pltpu.SemaphoreType.DMA((2,)),
                       pltpu.SemaphoreType.DMA((2,))])
    def run(h_hbm, pos_hbm, xs_hbm, buf, idx0, idx1, sem_in, sem_out):
        first = _sc_worker() * per

        def loads(j, s):
            base = first + j * SC_WIN
            return (pltpu.make_async_copy(h_hbm.at[pl.ds(base, SC_WIN)], buf.at[s], sem_in.at[s]),
                    pltpu.make_async_copy(pos_hbm.at[pl.ds(base, SC_WIN)], idx0.at[s], sem_in.at[s]),
                    pltpu.make_async_copy(pos_hbm.at[pl.ds(T + base, SC_WIN)], idx1.at[s], sem_in.at[s]))

        def stores(s):
            return (pltpu.make_async_copy(buf.at[s], xs_hbm.at[idx0.at[s]], sem_out.at[s]),
                    pltpu.make_async_copy(buf.at[s], xs_hbm.at[idx1.at[s]], sem_out.at[s]))

        for c in loads(0, 0):
            c.start()

        @pl.loop(0, nwin // 2)
        def _(jj):
            for s in range(2):
                j = jj * 2 + s
                for c in loads(j, s):
                    c.wait()
                for c in stores(s):
                    c.start()

                @pl.when(j >= 1)
                def _():
                    for c in stores(1 - s):
                        c.wait()

                @pl.when(j + 1 < nwin)
                def _():
                    for c in loads(j + 1, 1 - s):
                        c.start()

        for c in stores((nwin - 1) % 2):
            c.wait()

    return run(h, pos.reshape(TOP_K * T))


def _sc_gather(ys, pos):
    n = TOP_K * T
    per = n // SC_WORKERS
    nwin = per // SC_WIN

    @functools.partial(
        pl.kernel, out_type=jax.ShapeDtypeStruct((n, HALF), U32), mesh=_sc_mesh(),
        scratch_types=[pltpu.VMEM((2, SC_WIN, HALF), U32), pltpu.VMEM((per,), I32),
                       pltpu.SemaphoreType.DMA((2,)), pltpu.SemaphoreType.DMA((2,))])
    def run(ys_hbm, pos_hbm, out_hbm, buf, idx, sem_in, sem_out):
        first = _sc_worker() * per
        pltpu.sync_copy(pos_hbm.at[pl.ds(first, per)], idx)

        def load(j, s):
            return pltpu.make_async_copy(ys_hbm.at[idx.at[pl.ds(j * SC_WIN, SC_WIN)]], buf.at[s], sem_in.at[s])

        def store(j, s):
            return pltpu.make_async_copy(buf.at[s], out_hbm.at[pl.ds(first + j * SC_WIN, SC_WIN)], sem_out.at[s])

        load(0, 0).start()

        @pl.loop(0, nwin // 2)
        def _(jj):
            for s in range(2):
                j = jj * 2 + s
                load(j, s).wait()
                store(j, s).start()

                @pl.when(j >= 1)
                def _():
                    store(j - 1, 1 - s).wait()

                @pl.when(j + 1 < nwin)
                def _():
                    load(j + 1, 1 - s).start()

        store(nwin - 1, (nwin - 1) % 2).wait()

    return run(ys, pos.reshape(n))


def _moe_kernel(te_ref, wb_ref, used_ref, x_ref, wg_ref, wu_ref, wd_ref, y_ref, wg_s, wu_s, wd_s):
    del wb_ref
    i = pl.program_id(0)

    @pl.when(i < used_ref[0])
    def _():
        prev = te_ref[jnp.maximum(i - 1, 0)]

        @pl.when(jnp.logical_or(i == 0, te_ref[i] != prev))
        def _():
            wg_s[...] = wg_ref[0, 0].astype(BF16)
            wu_s[...] = wu_ref[0, 0].astype(BF16)
            wd_s[...] = wd_ref[0, 0].astype(BF16)

        hi, lo = _unpack_rows(x_ref[...])
        x = jnp.concatenate([hi, lo], axis=1).astype(BF16)
        a = jnp.dot(x, wg_s[...], preferred_element_type=F32)
        b = jnp.dot(x, wu_s[...], preferred_element_type=F32)
        hid = a * (1.0 / (1.0 + jnp.exp(-a))) * b
        y_ref[...] = _pack_rows(jnp.dot(hid.astype(BF16), wd_s[...], preferred_element_type=F32))


def _experts(xs, tile_expert, weight_block, used, layer, w_gate, w_up, w_down):
    row = lambda i, te, wb, used: (jnp.minimum(i, used[0] - 1), 0)
    weights = lambda i, te, wb, used: (layer, wb[i], 0, 0)
    grid_spec = pltpu.PrefetchScalarGridSpec(
        num_scalar_prefetch=3,
        grid=(N_ROW_TILES,),
        in_specs=[
            pl.BlockSpec((TM, HALF), row),
            pl.BlockSpec((1, 1, D, DE), weights),
            pl.BlockSpec((1, 1, D, DE), weights),
            pl.BlockSpec((1, 1, DE, D), weights),
        ],
        out_specs=pl.BlockSpec((TM, HALF), row),
        scratch_shapes=[pltpu.VMEM((D, DE), BF16), pltpu.VMEM((D, DE), BF16), pltpu.VMEM((DE, D), BF16)],
    )
    return pl.pallas_call(
        _moe_kernel,
        grid_spec=grid_spec,
        out_shape=jax.ShapeDtypeStruct((NPAD, HALF), U32),
        compiler_params=_vmem_limit(40),
        name="moe_experts",
    )(tile_expert, weight_block, used, xs, w_gate, w_up, w_down)


def _combined(y0_ref, y1_ref, x_ref, wcol_ref, mod_ref):
    w0 = wcol_ref[:, 0:1]
    w1 = wcol_ref[:, 1:2]
    hi0, lo0 = _unpack_rows(y0_ref[...])
    hi1, lo1 = _unpack_rows(y1_ref[...])
    y = jnp.concatenate([w0 * hi0 + w1 * hi1, w0 * lo0 + w1 * lo1], axis=1)
    return x_ref[...] + mod_ref[0, 5:6, :] * y


def _combine_final_kernel(y0_ref, y1_ref, x_ref, wcol_ref, mod_ref, op_ref, os_ref):
    i = pl.program_id(0)
    out = _combined(y0_ref, y1_ref, x_ref, wcol_ref, mod_ref)

    @pl.when(i < T_CTX // CT)
    def _():
        op_ref[...] = out

    @pl.when(i >= T_CTX // CT)
    def _():
        os_ref[...] = out


def _combine(y2, x, wcol, mod):
    n_ctx = T_CTX // CT
    return pl.pallas_call(
        _combine_final_kernel,
        grid=(T // CT,),
        in_specs=[
            pl.BlockSpec((CT, HALF), lambda i: (i, 0)),
            pl.BlockSpec((CT, HALF), lambda i: (T // CT + i, 0)),
            pl.BlockSpec((CT, D), lambda i: (i, 0)),
            pl.BlockSpec((CT, LANES), lambda i: (i, 0)),
            pl.BlockSpec((1, N_MOD, D), lambda i: (_cond_index(i, CT), 0, 0)),
        ],
        out_specs=[
            pl.BlockSpec((CT, D), lambda i: (jnp.minimum(i, n_ctx - 1), 0)),
            pl.BlockSpec((CT, D), lambda i: (jnp.maximum(i - n_ctx, 0), 0)),
        ],
        out_shape=[jax.ShapeDtypeStruct((T_CTX, D), F32), jax.ShapeDtypeStruct((T_LAT, D), F32)],
        name="moe_combine_final",
    )(y2, y2, x, wcol, mod)


def _moe(h, e, rank, cnt, layer, w_gate, w_up, w_down):
    pos, tile_expert, weight_block, used = _plan(e, rank, cnt)
    xs = _sc_dispatch(h, pos)
    ys = _experts(xs, tile_expert, weight_block, used, layer, w_gate, w_up, w_down)
    return _sc_gather(ys, pos)


def kernel(x_prompt, x_sample, c, cache_k, cache_v, c_ctx, norm_mix_g, norm_ffn_g, ada_w, ada_b, attn_w_qkv, attn_q_norm, attn_k_norm, attn_w_o, pool_w, pool_scale, moe_w_group, moe_w_expert, moe_w_gate, moe_w_up, moe_w_down):
    xp = x_prompt.reshape(T_CTX, D)
    xs = x_sample.reshape(T_LAT, D)
    cond = jnp.concatenate([c_ctx[None, :], c, jnp.zeros((N_COND - 1 - DEC_BATCH, D), F32)], axis=0)
    mod = _ada(cond, ada_w, ada_b)

    qt, k2, vt, new_k, new_v = _qkv(xp, xs, mod[0], norm_mix_g[0:1], attn_w_qkv[0].astype(BF16),
                                    attn_q_norm[0:1], attn_k_norm[0:1])
    o = _attention(qt, k2, vt, cache_k[:, 0].reshape(DEC_BATCH, PAST, DKV),
                   cache_v[:, 0].reshape(DEC_BATCH, PAST, DKV))
    wr_hl, wr_hi = _router_weights(moe_w_group[0], moe_w_expert[0])
    x1, h, e, rank, wcol, cnt = _post(o, xp, xs, mod[0], attn_w_o[0].astype(BF16), norm_ffn_g[0:1],
                                      wr_hl, wr_hi)
    y2 = _moe(h, e, rank, cnt, 0, moe_w_gate, moe_w_up, moe_w_down)

    wr_hl, wr_hi = _router_weights(moe_w_group[1], moe_w_expert[1])
    x3, h, e, rank, wcol, cnt = _pool(y2, x1, wcol, mod[0], mod[1], norm_mix_g[1:2], pool_w[0].astype(BF16),
                                      pool_scale[0:1], norm_ffn_g[1:2], wr_hl, wr_hi)
    y2 = _moe(h, e, rank, cnt, 1, moe_w_gate, moe_w_up, moe_w_down)
    y_prompt, y_sample = _combine(y2, x3, wcol, mod[1])

    return (y_prompt.reshape(BATCH, SEQ, D), y_sample.reshape(DEC_BATCH, DEC_SEQ, D),
            new_k.reshape(BATCH, 1, SEQ, N_KV, HD), new_v.reshape(BATCH, 1, SEQ, N_KV, HD))
```

```python
import functools

import jax
import jax.numpy as jnp
from jax import lax
from jax.experimental import pallas as pl
from jax.experimental.pallas import tpu as pltpu
from jax.experimental.pallas import tpu_sc as plsc

F32 = jnp.float32
BF16 = jnp.bfloat16
I32 = jnp.int32
U32 = jnp.uint32

D = 1024
BATCH, SEQ = 32, 256
DEC_BATCH, DEC_SEQ, PAST = 8, 1024, 512
T_CTX = BATCH * SEQ
T_LAT = DEC_BATCH * DEC_SEQ
T = T_CTX + T_LAT
GRID_W = 64
N_HEADS, N_KV, HD = 16, 4, 64
DQ = N_HEADS * HD
DKV = N_KV * HD
ROPE_THETA = 10000.0
POOL_WINDOWS = (2, 4, 8, 16)
PGD = D // len(POOL_WINDOWS)
N_GROUPS, E_PER_G, TOP_K = 4, 8, 2
NE = N_GROUPS * E_PER_G
DE = D // 4
N_MOD = 6
EPS = 1e-6
N_COND = 16

TILE = 512
NT = T // TILE
NT_CTX = T_CTX // TILE
PTILE = 1024
PB = 128
PBW = 256
PB_HALO = (PBW - PB) // 2
TM = 512
N_ROW_TILES = (TOP_K * T) // TM + NE
NPAD = N_ROW_TILES * TM
CT = 512
LANES = 128
NEG_INF = float("-inf")
LOG2_E = 1.4426950408889634
Q_SCALE = HD ** -0.5 * LOG2_E
HALF = D // 2
SC_CORES, SC_SUBCORES = 2, 16
SC_WORKERS = SC_CORES * SC_SUBCORES
SC_WIN = 64


def _vmem_limit(mib):
    return pltpu.CompilerParams(vmem_limit_bytes=mib * 1024 * 1024)


def _modulate(x, gain_scale, shift):
    ms = jnp.mean(x * x, axis=-1, keepdims=True)
    return (x * lax.rsqrt(ms + EPS)) * gain_scale + shift


def _head_norm(z, gain, ind, ind_t2):
    ss = jnp.dot((z * z).astype(BF16), ind, preferred_element_type=F32)
    inv = lax.rsqrt(ss * (1.0 / HD) + EPS)
    inv_hi = inv.astype(BF16)
    inv_lo = (inv - inv_hi.astype(F32)).astype(BF16)
    scale = jnp.dot(jnp.concatenate([inv_hi, inv_lo], axis=1), ind_t2, preferred_element_type=F32)
    return z * scale * gain


def _rope(z, cos_t, sin_t):
    lane = lax.broadcasted_iota(I32, (z.shape[0], LANES), 1)
    low = (lane % 32) < 16
    outs = []
    for c in range(z.shape[1] // LANES):
        zc = z[:, c * LANES:(c + 1) * LANES]
        up = pltpu.roll(zc, 16, axis=1)
        dn = pltpu.roll(zc, LANES - 16, axis=1)
        outs.append(zc * cos_t + jnp.where(low, dn, up) * sin_t)
    return jnp.concatenate(outs, axis=1)


def _rope_t(zt, cos_tt, sin_tt):
    sub = lax.broadcasted_iota(I32, (LANES, zt.shape[1]), 0)
    low = (sub % 32) < 16
    outs = []
    for c in range(zt.shape[0] // LANES):
        zc = zt[c * LANES:(c + 1) * LANES]
        up = pltpu.roll(zc, 16, axis=0)
        dn = pltpu.roll(zc, LANES - 16, axis=0)
        outs.append(zc * cos_tt + jnp.where(low, dn, up) * sin_tt)
    return jnp.concatenate(outs, axis=0)


def _pack_rows(z):
    bits = lax.bitcast_convert_type(z.astype(BF16).astype(F32), U32)
    return bits[:, :HALF] | (bits[:, HALF:] >> 16)


def _unpack_rows(p):
    hi = lax.bitcast_convert_type(p & jnp.uint32(0xFFFF0000), F32)
    lo = lax.bitcast_convert_type(p << 16, F32)
    return hi, lo


def _ada_kernel(c_ref, w_ref, b_ref, o_ref):
    c = c_ref[...]
    a = c * (1.0 / (1.0 + jnp.exp(-c)))
    o_ref[0] = jnp.dot(a.astype(BF16), w_ref[0].astype(BF16), preferred_element_type=F32) + b_ref[0]


def _ada(cond, ada_w, ada_b):
    nb = 1536
    depth = ada_w.shape[0]
    out = pl.pallas_call(
        _ada_kernel,
        grid=(depth, (N_MOD * D) // nb),
        in_specs=[
            pl.BlockSpec((N_COND, D), lambda l, j: (0, 0)),
            pl.BlockSpec((1, D, nb), lambda l, j: (l, 0, j)),
            pl.BlockSpec((1, 1, nb), lambda l, j: (l, 0, j)),
        ],
        out_specs=pl.BlockSpec((1, N_COND, nb), lambda l, j: (l, 0, j)),
        out_shape=jax.ShapeDtypeStruct((depth, N_COND, N_MOD * D), F32),
        compiler_params=_vmem_limit(40),
        name="ada",
    )(cond, ada_w, ada_b.reshape(depth, 1, N_MOD * D))
    return out.reshape(depth, N_COND, N_MOD, D)


def _cond_index(i, tile):
    n_ctx = T_CTX // tile
    per_batch = DEC_SEQ // tile
    return jnp.where(i < n_ctx, 0, 1 + (i - n_ctx) // per_batch)


def _qkv_kernel(xp_ref, xs_ref, mod_ref, g_ref, w_ref, qg_ref, kg_ref, indq_ref, indqt_ref,
                indk_ref, indkt_ref, cos_ref, sin_ref, cos_t_ref, sin_t_ref,
                qt_ref, k2_ref, vt_ref, nk_ref, nv_ref):
    i = pl.program_id(0)
    is_lat = i >= NT_CTX
    x = jnp.where(is_lat, xs_ref[...], xp_ref[...])
    h = _modulate(x, g_ref[...] * (1.0 + mod_ref[0, 1:2, :]), mod_ref[0, 0:1, :])
    qkv = jnp.dot(h.astype(BF16), w_ref[...], preferred_element_type=F32)
    q = _head_norm(qkv[:, :DQ], qg_ref[...], indq_ref[...], indqt_ref[...])
    k = _head_norm(qkv[:, DQ:DQ + DKV], kg_ref[...], indk_ref[...], indkt_ref[...])
    v = qkv[:, DQ + DKV:]

    vt_ref[...] = v.T.astype(BF16)

    qt = (q * Q_SCALE).T

    @pl.when(is_lat)
    def _():
        qt_ref[...] = _rope_t(qt, cos_t_ref[...], sin_t_ref[...]).astype(BF16)
        k2_ref[...] = _dup_heads(_rope(k, cos_ref[...], sin_ref[...])).astype(BF16)

    @pl.when(jnp.logical_not(is_lat))
    def _():
        qt_ref[...] = qt.astype(BF16)
        k2_ref[...] = _dup_heads(k).astype(BF16)
        nk_ref[...] = k
        nv_ref[...] = v


def _rope_tables():
    rows = DEC_SEQ // GRID_W
    row = jnp.broadcast_to(jnp.arange(rows, dtype=F32)[:, None], (rows, GRID_W)).reshape(-1)
    col = jnp.broadcast_to(jnp.arange(GRID_W, dtype=F32)[None, :], (rows, GRID_W)).reshape(-1)
    axis_dim = HD // 2
    inv_freq = jnp.power(ROPE_THETA, -jnp.arange(0, axis_dim, 2, dtype=F32) / axis_dim)
    ang = jnp.concatenate([row[:, None] * inv_freq, col[:, None] * inv_freq], axis=-1)
    cos, sin = jnp.cos(ang), jnp.sin(ang)
    quarter = HD // 4
    cos_h = jnp.concatenate([cos[:, :quarter], cos[:, :quarter], cos[:, quarter:], cos[:, quarter:]], axis=1)
    sin_h = jnp.concatenate([-sin[:, :quarter], sin[:, :quarter], -sin[:, quarter:], sin[:, quarter:]], axis=1)
    return jnp.tile(cos_h, (1, LANES // HD)), jnp.tile(sin_h, (1, LANES // HD))


def _head_indicators(width):
    col = jnp.arange(width)[:, None]
    head = jnp.arange(LANES)[None, :]
    ind = (col // HD == head).astype(BF16)
    return ind, jnp.concatenate([ind.T, ind.T], axis=0)


def _dup_heads(z):
    lane = lax.broadcasted_iota(I32, (z.shape[0], LANES), 1)
    cols = []
    for p in range(N_KV // 2):
        blk = z[:, p * LANES:(p + 1) * LANES]
        swp = pltpu.roll(blk, HD, axis=1)
        cols += [jnp.where(lane < HD, blk, swp), jnp.where(lane < HD, swp, blk)]
    return jnp.concatenate(cols, axis=1)


def _qkv(xp, xs, mod, g, w_qkv, q_gain, k_gain):
    indq, indqt = _head_indicators(DQ)
    indk, indkt = _head_indicators(DKV)
    cos_t, sin_t = _rope_tables()
    per_batch = DEC_SEQ // TILE
    const = lambda shape: pl.BlockSpec(shape, lambda i: (0,) * len(shape))
    return pl.pallas_call(
        _qkv_kernel,
        grid=(NT,),
        in_specs=[
            pl.BlockSpec((TILE, D), lambda i: (jnp.minimum(i, NT_CTX - 1), 0)),
            pl.BlockSpec((TILE, D), lambda i: (jnp.maximum(i - NT_CTX, 0), 0)),
            pl.BlockSpec((1, N_MOD, D), lambda i: (_cond_index(i, TILE), 0, 0)),
            const((1, D)),
            const((D, DQ + 2 * DKV)),
            const((1, DQ)), const((1, DKV)),
            const((DQ, LANES)), const((2 * LANES, DQ)),
            const((DKV, LANES)), const((2 * LANES, DKV)),
            pl.BlockSpec((TILE, LANES), lambda i: (jnp.maximum(i - NT_CTX, 0) % per_batch, 0)),
            pl.BlockSpec((TILE, LANES), lambda i: (jnp.maximum(i - NT_CTX, 0) % per_batch, 0)),
            pl.BlockSpec((LANES, TILE), lambda i: (0, jnp.maximum(i - NT_CTX, 0) % per_batch)),
            pl.BlockSpec((LANES, TILE), lambda i: (0, jnp.maximum(i - NT_CTX, 0) % per_batch)),
        ],
        out_specs=[
            pl.BlockSpec((DQ, TILE), lambda i: (0, i)),
            pl.BlockSpec((TILE, 2 * DKV), lambda i: (i, 0)),
            pl.BlockSpec((DKV, TILE), lambda i: (0, i)),
            pl.BlockSpec((TILE, DKV), lambda i: (jnp.minimum(i, NT_CTX - 1), 0)),
            pl.BlockSpec((TILE, DKV), lambda i: (jnp.minimum(i, NT_CTX - 1), 0)),
        ],
        out_shape=[
            jax.ShapeDtypeStruct((DQ, T), BF16),
            jax.ShapeDtypeStruct((T, 2 * DKV), BF16),
            jax.ShapeDtypeStruct((DKV, T), BF16),
            jax.ShapeDtypeStruct((T_CTX, DKV), F32),
            jax.ShapeDtypeStruct((T_CTX, DKV), F32),
        ],
        compiler_params=_vmem_limit(56),
        name="qkv",
    )(xp, xs, mod, g, w_qkv, jnp.tile(q_gain, (1, N_HEADS)), jnp.tile(k_gain, (1, N_KV)),
      indq, indqt, indk, indkt, cos_t, sin_t, cos_t.T, sin_t.T)


def _group_scores(g, k2_parts, qt_ref):
    cols = []
    for j in range(2):
        qt_pair = qt_ref[(2 * g + j) * LANES:(2 * g + j + 1) * LANES, :]
        sub = lax.broadcasted_iota(I32, qt_pair.shape, 0)
        zero = jnp.zeros_like(qt_pair)
        cols += [jnp.where(sub < HD, qt_pair, zero), jnp.where(sub >= HD, qt_pair, zero)]
    qt4 = jnp.concatenate(cols, axis=1)
    return [jnp.dot(k2, qt4, preferred_element_type=F32) for k2 in k2_parts]


def _softmax_values(ss, vt_parts):
    m = ss[0].max(axis=0, keepdims=True)
    for s in ss[1:]:
        m = jnp.maximum(m, s.max(axis=0, keepdims=True))
    acc = None
    l = None
    for s, vt in zip(ss, vt_parts):
        e = jnp.exp2(s - m)
        ls = e.sum(axis=0, keepdims=True)
        l = ls if l is None else l + ls
        t = jnp.dot(vt, e.astype(BF16), preferred_element_type=F32)
        acc = t if acc is None else acc + t
    return acc * (1.0 / l)


def _heads_to_rows(groups, lq):
    heads = [o[:, h * lq:(h + 1) * lq] for o in groups for h in range(N_HEADS // N_KV)]
    return jnp.concatenate(heads, axis=0).T


def _attend(keys_of, values_of, qt_ref, o_ref):
    groups = []
    ss = _group_scores(0, keys_of(0), qt_ref)
    for g in range(N_KV):
        ss_next = _group_scores(g + 1, keys_of(g + 1), qt_ref) if g + 1 < N_KV else None
        groups.append(_softmax_values(ss, values_of(g)))
        ss = ss_next
    o_ref[...] = _heads_to_rows(groups, o_ref.shape[0]).astype(BF16)


def _attn_ctx_kernel(qt_ref, k2_ref, vt_ref, o_ref):
    _attend(lambda g: [k2_ref[:, g * LANES:(g + 1) * LANES]],
            lambda g: [vt_ref[g * HD:(g + 1) * HD, :]], qt_ref, o_ref)


def _attn_lat_kernel(qt_ref, k2c_ref, vtc_ref, k2n_ref, vtn_ref, o_in_ref, o_ref):
    del o_in_ref
    _attend(lambda g: [k2c_ref[0, :, g * LANES:(g + 1) * LANES], k2n_ref[:, g * LANES:(g + 1) * LANES]],
            lambda g: [vtc_ref[0, g * HD:(g + 1) * HD, :], vtn_ref[g * HD:(g + 1) * HD, :]], qt_ref, o_ref)


def _cache_kernel(ck_ref, cv_ref, k2_ref, vt_ref):
    k2_ref[0] = _dup_heads(ck_ref[0]).astype(BF16)
    vt_ref[0] = cv_ref[0].T.astype(BF16)


def _attention(qt, k2, vt, cache_k, cache_v):
    k2c, vtc = pl.pallas_call(
        _cache_kernel,
        grid=(DEC_BATCH,),
        in_specs=[
            pl.BlockSpec((1, PAST, DKV), lambda b: (b, 0, 0)),
            pl.BlockSpec((1, PAST, DKV), lambda b: (b, 0, 0)),
        ],
        out_specs=[
            pl.BlockSpec((1, PAST, 2 * DKV), lambda b: (b, 0, 0)),
            pl.BlockSpec((1, DKV, PAST), lambda b: (b, 0, 0)),
        ],
        out_shape=[
            jax.ShapeDtypeStruct((DEC_BATCH, PAST, 2 * DKV), BF16),
            jax.ShapeDtypeStruct((DEC_BATCH, DKV, PAST), BF16),
        ],
        name="cache_prep",
    )(cache_k, cache_v)

    o_ctx = pl.pallas_call(
        _attn_ctx_kernel,
        grid=(BATCH,),
        in_specs=[
            pl.BlockSpec((DQ, SEQ), lambda b: (0, b)),
            pl.BlockSpec((SEQ, 2 * DKV), lambda b: (b, 0)),
            pl.BlockSpec((DKV, SEQ), lambda b: (0, b)),
        ],
        out_specs=pl.BlockSpec((SEQ, DQ), lambda b: (b, 0)),
        out_shape=jax.ShapeDtypeStruct((T, DQ), BF16),
        name="attn_ctx",
    )(qt, k2, vt)

    qb = 512
    nqb = DEC_SEQ // qb
    lat0 = T_CTX // qb
    return pl.pallas_call(
        _attn_lat_kernel,
        grid=(DEC_BATCH, nqb),
        in_specs=[
            pl.BlockSpec((DQ, qb), lambda b, j: (0, lat0 + b * nqb + j)),
            pl.BlockSpec((1, PAST, 2 * DKV), lambda b, j: (b, 0, 0)),
            pl.BlockSpec((1, DKV, PAST), lambda b, j: (b, 0, 0)),
            pl.BlockSpec((DEC_SEQ, 2 * DKV), lambda b, j: (T_CTX // DEC_SEQ + b, 0)),
            pl.BlockSpec((DKV, DEC_SEQ), lambda b, j: (0, T_CTX // DEC_SEQ + b)),
            pl.BlockSpec(memory_space=pl.ANY),
        ],
        out_specs=pl.BlockSpec((qb, DQ), lambda b, j: (lat0 + b * nqb + j, 0)),
        out_shape=jax.ShapeDtypeStruct((T, DQ), BF16),
        input_output_aliases={5: 0},
        compiler_params=_vmem_limit(56),
        name="attn_lat",
    )(qt, k2c, vtc, k2, vt, o_ctx)


def _route(h, wr_hl_ref, wr_hi_ref, tri_ref, cnt_ref, e_ref, rank_ref, wcol_ref, cnt_out_ref):
    tl = h.shape[0]
    h_hi = h.astype(BF16)
    h_lo = (h - h_hi.astype(F32)).astype(BF16)
    lg2 = jnp.dot(h_hi, wr_hl_ref[...], preferred_element_type=F32)
    lg = lg2[:, :LANES] + lg2[:, LANES:] + jnp.dot(h_lo, wr_hi_ref[...], preferred_element_type=F32)
    lt = lg.T

    sub8 = lax.broadcasted_iota(I32, (8, tl), 0).astype(F32)
    gl = jnp.where(sub8 < N_GROUPS, lt[0:8], NEG_INF)
    gmax = gl.max(axis=0, keepdims=True)
    gidx = jnp.min(jnp.where(gl == gmax, sub8, 8.0), axis=0, keepdims=True)
    gsum = jnp.sum(jnp.exp(gl - gmax), axis=0, keepdims=True)
    sel = lt[8:16]
    for g in range(1, N_GROUPS):
        sel = jnp.where(gidx == g, lt[8 + 8 * g:16 + 8 * g], sel)
    t1 = sel.max(axis=0, keepdims=True)
    j1 = jnp.min(jnp.where(sel == t1, sub8, 8.0), axis=0, keepdims=True)
    sel2 = jnp.where(sub8 == j1, NEG_INF, sel)
    t2 = sel2.max(axis=0, keepdims=True)
    j2 = jnp.min(jnp.where(sel2 == t2, sub8, 8.0), axis=0, keepdims=True)
    ex = jnp.exp(t2 - t1)
    den = 1.0 + ex
    gw = 1.0 / gsum
    w0 = gw * (1.0 / den)
    w1 = gw * (ex / den)
    e0 = gidx * E_PER_G + j1
    e1 = gidx * E_PER_G + j2

    sub_e = lax.broadcasted_iota(I32, (NE, tl), 0).astype(F32)
    oh0 = (sub_e == e0).astype(F32)
    oh1 = (sub_e == e1).astype(F32)
    c = oh0 + oh1
    carry = cnt_ref[:, 0:1]
    pieces = []
    for ch in range(tl // 256):
        cc = c[:, ch * 256:(ch + 1) * 256]
        pieces.append(jnp.dot(cc.astype(BF16), tri_ref[...], preferred_element_type=F32) + carry)
        carry = carry + jnp.sum(cc, axis=1, keepdims=True)
    csum = jnp.concatenate(pieces, axis=1)
    cnt_ref[...] = jnp.broadcast_to(carry, (NE, LANES))
    cnt_out_ref[...] = jnp.broadcast_to(carry, (NE, LANES))
    r0 = jnp.sum(oh0 * csum, axis=0, keepdims=True) - 1.0
    r1 = jnp.sum(oh1 * csum, axis=0, keepdims=True) - 1.0
    e_ref[0:1, :] = e0.astype(I32)
    e_ref[1:2, :] = e1.astype(I32)
    rank_ref[0:1, :] = r0.astype(I32)
    rank_ref[1:2, :] = r1.astype(I32)
    sub_w = lax.broadcasted_iota(I32, (LANES, tl), 0)
    w8 = jnp.where(sub_w == 0, w0, jnp.where(sub_w == 1, w1, 0.0))
    wcol_ref[...] = w8.T


def _post_kernel(o_ref, xp_ref, xs_ref, mod_ref, wo_ref, g_ref, wr_hl_ref, wr_hi_ref, tri_ref,
                 x1_ref, h_ref, e_ref, rank_ref, wcol_ref, cnt_out_ref, cnt_ref):
    i = pl.program_id(0)

    @pl.when(i == 0)
    def _():
        cnt_ref[...] = jnp.zeros_like(cnt_ref)

    x = jnp.where(i >= NT_CTX, xs_ref[...], xp_ref[...])
    m = jnp.dot(o_ref[...], wo_ref[...], preferred_element_type=F32)
    x1 = x + mod_ref[0, 2:3, :] * m
    x1_ref[...] = x1
    h = _modulate(x1, g_ref[...] * (1.0 + mod_ref[0, 4:5, :]), mod_ref[0, 3:4, :])
    h_ref[...] = _pack_rows(h)
    _route(h, wr_hl_ref, wr_hi_ref, tri_ref, cnt_ref, e_ref, rank_ref, wcol_ref, cnt_out_ref)


def _router_weights(w_group, w_expert):
    w = jnp.zeros((D, LANES), F32)
    w = w.at[:, :N_GROUPS].set(w_group)
    w = w.at[:, 8:8 + NE].set(jnp.transpose(w_expert, (1, 0, 2)).reshape(D, NE))
    w_hi = w.astype(BF16)
    w_lo = (w - w_hi.astype(F32)).astype(BF16)
    return jnp.concatenate([w_hi, w_lo], axis=1), w_hi


def _route_outs(tile, first=0):
    specs = [
        pl.BlockSpec((tile, D), lambda i: (first + i, 0)),
        pl.BlockSpec((tile, HALF), lambda i: (first + i, 0)),
        pl.BlockSpec((TOP_K, tile), lambda i: (0, first + i)),
        pl.BlockSpec((TOP_K, tile), lambda i: (0, first + i)),
        pl.BlockSpec((tile, LANES), lambda i: (first + i, 0)),
        pl.BlockSpec((NE, LANES), lambda i: (0, 0)),
    ]
    shapes = [
        jax.ShapeDtypeStruct((T, D), F32),
        jax.ShapeDtypeStruct((T, HALF), U32),
        jax.ShapeDtypeStruct((TOP_K, T), I32),
        jax.ShapeDtypeStruct((TOP_K, T), I32),
        jax.ShapeDtypeStruct((T, LANES), F32),
        jax.ShapeDtypeStruct((NE, LANES), F32),
    ]
    return specs, shapes


def _tri():
    a = jnp.arange(256)
    return (a[:, None] <= a[None, :]).astype(BF16)


def _post(o, xp, xs, mod, w_o, g_ffn, wr_hl, wr_hi):
    const = lambda shape: pl.BlockSpec(shape, lambda i: (0,) * len(shape))
    out_specs, out_shape = _route_outs(TILE)
    return pl.pallas_call(
        _post_kernel,
        grid=(NT,),
        in_specs=[
            pl.BlockSpec((TILE, DQ), lambda i: (i, 0)),
            pl.BlockSpec((TILE, D), lambda i: (jnp.minimum(i, NT_CTX - 1), 0)),
            pl.BlockSpec((TILE, D), lambda i: (jnp.maximum(i - NT_CTX, 0), 0)),
            pl.BlockSpec((1, N_MOD, D), lambda i: (_cond_index(i, TILE), 0, 0)),
            const((DQ, D)), const((1, D)), const((D, 2 * LANES)), const((D, LANES)), const((256, 256)),
        ],
        out_specs=out_specs,
        out_shape=out_shape,
        scratch_shapes=[pltpu.VMEM((NE, LANES), F32)],
        compiler_params=_vmem_limit(56),
        name="post_attn",
    )(o, xp, xs, mod, w_o, g_ffn, wr_hl, wr_hi, _tri())


def _pool_kernel(seq, n_alias, y0_ref, y1_ref, xin_ref, wcin_ref, modp_ref, mod_ref, g_ref, pw_ref, ps_ref,
                 gf_ref, wr_hl_ref, wr_hi_ref, tri_ref, cnt_in_ref, *refs):
    (x1_ref, h_ref, e_ref, rank_ref, wcol_ref, cnt_out_ref, cnt_ref, hhi_ref, hlo_ref) = refs[n_alias:]
    i = pl.program_id(0)

    @pl.when(i == 0)
    def _():
        cnt_ref[...] = cnt_in_ref[...]

    x = _combined(y0_ref, y1_ref, xin_ref, wcin_ref, modp_ref)
    h = _modulate(x, g_ref[...] * (1.0 + mod_ref[0, 1:2, :]), mod_ref[0, 0:1, :])
    @pl.when(i == 0)
    def _():
        zeros = jnp.zeros((PB_HALO, D), BF16)
        for ref in (hhi_ref, hlo_ref):
            ref[0:PB_HALO, :] = zeros
            ref[PB_HALO + PTILE:PB_HALO + PTILE + PB_HALO, :] = zeros

    h_hi = h.astype(BF16)
    hhi_ref[PB_HALO:PB_HALO + PTILE, :] = h_hi
    hlo_ref[PB_HALO:PB_HALO + PTILE, :] = (h - h_hi.astype(F32)).astype(BF16)

    r_io = lax.broadcasted_iota(I32, (PB, PBW), 0)
    d = lax.broadcasted_iota(I32, (PB, PBW), 1) - PB_HALO - r_io
    r_col = lax.broadcasted_iota(I32, (PB, 1), 0)
    sums = [[None] * (PTILE // PB) for _ in POOL_WINDOWS]
    cnts = [[None] * (PTILE // PB) for _ in POOL_WINDOWS]
    for b in range(PTILE // PB):
        tm = (b * PB + r_io) & (seq - 1)
        inseq = jnp.logical_and(tm + d >= 0, tm + d < seq)
        tm_col = (b * PB + r_col) & (seq - 1)
        for gi, win in enumerate(POOL_WINDOWS):
            left = win // 2
            right = win - left - 1
            band = jnp.logical_and(inseq, jnp.logical_and(d >= -left, d <= right))
            bm = jnp.where(band, 1.0, 0.0).astype(BF16)
            cols = slice(gi * PGD, (gi + 1) * PGD)
            sums[gi][b] = (jnp.dot(bm, hhi_ref[b * PB:b * PB + PBW, cols], preferred_element_type=F32)
                           + jnp.dot(bm, hlo_ref[b * PB:b * PB + PBW, cols], preferred_element_type=F32))
            cnt = jnp.minimum(tm_col + right + 1, seq) - jnp.maximum(tm_col - left, 0)
            cnts[gi][b] = cnt.astype(F32)
    outs = []
    for gi in range(len(POOL_WINDOWS)):
        tot = jnp.concatenate(sums[gi], axis=0)
        cnt = jnp.concatenate(cnts[gi], axis=0)
        diff = tot / cnt - h[:, gi * PGD:(gi + 1) * PGD]
        outs.append(jnp.dot(diff.astype(BF16), pw_ref[gi], preferred_element_type=F32))
    m = jnp.concatenate(outs, axis=1) * ps_ref[...]
    x1 = x + mod_ref[0, 2:3, :] * m
    x1_ref[...] = x1
    hf = _modulate(x1, gf_ref[...] * (1.0 + mod_ref[0, 4:5, :]), mod_ref[0, 3:4, :])
    h_ref[...] = _pack_rows(hf)
    _route(hf, wr_hl_ref, wr_hi_ref, tri_ref, cnt_ref, e_ref, rank_ref, wcol_ref, cnt_out_ref)


def _pool(group, y2g, x, wcol, mod_prev, mod, g_mix, pool_w, pool_scale, g_ffn, wr_hl, wr_hi, cnt_in, prev):
    const = lambda shape: pl.BlockSpec(shape, lambda i: (0,) * len(shape))
    tiles = (T_CTX if group == 0 else T_LAT) // PTILE
    first = 0 if group == 0 else T_CTX // PTILE
    out_specs, out_shape = _route_outs(PTILE, first)
    aliases = () if prev is None else tuple(prev)
    n_fixed = 14
    return pl.pallas_call(
        functools.partial(_pool_kernel, SEQ if group == 0 else DEC_SEQ, len(aliases)),
        grid=(tiles,),
        in_specs=[
            pl.BlockSpec((PTILE, HALF), lambda i: (i, 0)),
            pl.BlockSpec((PTILE, HALF), lambda i: (tiles + i, 0)),
            pl.BlockSpec((PTILE, D), lambda i: (first + i, 0)),
            pl.BlockSpec((PTILE, LANES), lambda i: (first + i, 0)),
            pl.BlockSpec((1, N_MOD, D), lambda i: (_cond_index(first + i, PTILE), 0, 0)),
            pl.BlockSpec((1, N_MOD, D), lambda i: (_cond_index(first + i, PTILE), 0, 0)),
            const((1, D)), const((len(POOL_WINDOWS), PGD, PGD)), const((1, D)), const((1, D)),
            const((D, 2 * LANES)), const((D, LANES)), const((256, 256)), const((NE, LANES)),
        ] + [pl.BlockSpec(memory_space=pl.ANY)] * len(aliases),
        out_specs=out_specs,
        out_shape=out_shape,
        input_output_aliases={n_fixed + k: k for k in range(len(aliases))},
        scratch_shapes=[pltpu.VMEM((NE, LANES), F32), pltpu.VMEM((PTILE + 2 * PB_HALO, D), BF16),
                        pltpu.VMEM((PTILE + 2 * PB_HALO, D), BF16)],
        compiler_params=_vmem_limit(56),
        name="pool_mixer",
    )(y2g, y2g, x, wcol, mod_prev, mod, g_mix, pool_w, pool_scale, g_ffn, wr_hl, wr_hi, _tri(), cnt_in, *aliases)


def _plan(e, rank, cnt):
    counts = cnt[:, 0].astype(I32)
    n_tiles = (counts + TM - 1) // TM
    tile_end = jnp.cumsum(n_tiles)
    base = (tile_end - n_tiles) * TM
    pos = jnp.sum(jnp.where(e[None] == jnp.arange(NE, dtype=I32)[:, None, None], base[:, None, None], 0),
                  axis=0) + rank
    used = tile_end[-1]
    tile_id = jnp.minimum(jnp.arange(N_ROW_TILES, dtype=I32), used - 1)
    tile_expert = jnp.sum((tile_id[:, None] >= tile_end[None, :]).astype(I32), axis=1)
    experts = jnp.arange(NE, dtype=I32)
    owner = jnp.where(n_tiles > 0, experts, NE)
    later = jnp.flip(lax.cummin(jnp.flip(jnp.concatenate([owner[1:], jnp.full((1,), NE, I32)]))))
    next_owner = jnp.where(later < NE, later, experts)
    first = jnp.concatenate([jnp.ones((1,), bool), tile_expert[1:] != tile_expert[:-1]])
    weight_block = jnp.where(first, tile_expert, next_owner[tile_expert])
    return pos, tile_expert, weight_block, used.reshape(1)


def _sc_mesh():
    return plsc.VectorSubcoreMesh(core_axis_name="core", subcore_axis_name="subcore")


def _sc_worker():
    return lax.axis_index("core") * SC_SUBCORES + lax.axis_index("subcore")


def _sc_dispatch(h, pos):
    per = T // SC_WORKERS
    nwin = per // SC_WIN

    @functools.partial(
        pl.kernel, out_type=jax.ShapeDtypeStruct((NPAD, HALF), U32), mesh=_sc_mesh(),
        scratch_types=[pltpu.VMEM((2, SC_WIN, HALF), U32), pltpu.VMEM((2, SC_WIN), I32),
                       pltpu.VMEM((2, SC_WIN), I32), pltpu.SemaphoreType.DMA((2,)),
                       pltpu.SemaphoreType.DMA((2,))])
    def run(h_hbm, pos_hbm, xs_hbm, buf, idx0, idx1, sem_in, sem_out):
        first = _sc_worker() * per

        def loads(j, s):
            base = first + j * SC_WIN
            return (pltpu.make_async_copy(h_hbm.at[pl.ds(base, SC_WIN)], buf.at[s], sem_in.at[s]),
                    pltpu.make_async_copy(pos_hbm.at[pl.ds(base, SC_WIN)], idx0.at[s], sem_in.at[s]),
                    pltpu.make_async_copy(pos_hbm.at[pl.ds(T + base, SC_WIN)], idx1.at[s], sem_in.at[s]))

        def stores(s):
            return (pltpu.make_async_copy(buf.at[s], xs_hbm.at[idx0.at[s]], sem_out.at[s]),
                    pltpu.make_async_copy(buf.at[s], xs_hbm.at[idx1.at[s]], sem_out.at[s]))

        for c in loads(0, 0):
            c.start()

        @pl.loop(0, nwin // 2)
        def _(jj):
            for s in range(2):
                j = jj * 2 + s
                for c in loads(j, s):
                    c.wait()
                for c in stores(s):
                    c.start()

                @pl.when(j >= 1)
                def _():
                    for c in stores(1 - s):
                        c.wait()

                @pl.when(j + 1 < nwin)
                def _():
                    for c in loads(j + 1, 1 - s):
                        c.start()

        for c in stores((nwin - 1) % 2):
            c.wait()

    return run(h, pos.reshape(TOP_K * T))


def _sc_gather(ys, pos, group):
    t0, n_g = (0, T_CTX) if group == 0 else (T_CTX, T_LAT)
    idx_all = jnp.concatenate([pos[k, t0:t0 + n_g] for k in range(TOP_K)])
    n = TOP_K * n_g
    per = n // SC_WORKERS
    nwin = per // SC_WIN

    @functools.partial(
        pl.kernel, out_type=jax.ShapeDtypeStruct((n, HALF), U32), mesh=_sc_mesh(),
        scratch_types=[pltpu.VMEM((2, SC_WIN, HALF), U32), pltpu.VMEM((per,), I32),
                       pltpu.SemaphoreType.DMA((2,)), pltpu.SemaphoreType.DMA((2,))])
    def run(ys_hbm, pos_hbm, out_hbm, buf, idx, sem_in, sem_out):
        first = _sc_worker() * per
        pltpu.sync_copy(pos_hbm.at[pl.ds(first, per)], idx)

        def load(j, s):
            return pltpu.make_async_copy(ys_hbm.at[idx.at[pl.ds(j * SC_WIN, SC_WIN)]], buf.at[s], sem_in.at[s])

        def store(j, s):
            return pltpu.make_async_copy(buf.at[s], out_hbm.at[pl.ds(first + j * SC_WIN, SC_WIN)], sem_out.at[s])

        load(0, 0).start()

        @pl.loop(0, nwin // 2)
        def _(jj):
            for s in range(2):
                j = jj * 2 + s
                load(j, s).wait()
                store(j, s).start()

                @pl.when(j >= 1)
                def _():
                    store(j - 1, 1 - s).wait()

                @pl.when(j + 1 < nwin)
                def _():
                    load(j + 1, 1 - s).start()

        store(nwin - 1, (nwin - 1) % 2).wait()

    return run(ys, idx_all)


def _moe_kernel(te_ref, wb_ref, used_ref, x_ref, wg_ref, wu_ref, wd_ref, y_ref, wg_s, wu_s, wd_s):
    del wb_ref
    i = pl.program_id(0)

    @pl.when(i < used_ref[0])
    def _():
        prev = te_ref[jnp.maximum(i - 1, 0)]

        @pl.when(jnp.logical_or(i == 0, te_ref[i] != prev))
        def _():
            wg_s[...] = wg_ref[0, 0].astype(BF16)
            wu_s[...] = wu_ref[0, 0].astype(BF16)
            wd_s[...] = wd_ref[0, 0].astype(BF16)

        hi, lo = _unpack_rows(x_ref[...])
        x = jnp.concatenate([hi, lo], axis=1).astype(BF16)
        a = jnp.dot(x, wg_s[...], preferred_element_type=F32)
        b = jnp.dot(x, wu_s[...], preferred_element_type=F32)
        hid = a * (1.0 / (1.0 + jnp.exp(-a))) * b
        y_ref[...] = _pack_rows(jnp.dot(hid.astype(BF16), wd_s[...], preferred_element_type=F32))


def _experts(xs, tile_expert, weight_block, used, layer, w_gate, w_up, w_down):
    row = lambda i, te, wb, used: (jnp.minimum(i, used[0] - 1), 0)
    weights = lambda i, te, wb, used: (layer, wb[i], 0, 0)
    grid_spec = pltpu.PrefetchScalarGridSpec(
        num_scalar_prefetch=3,
        grid=(N_ROW_TILES,),
        in_specs=[
            pl.BlockSpec((TM, HALF), row),
            pl.BlockSpec((1, 1, D, DE), weights),
            pl.BlockSpec((1, 1, D, DE), weights),
            pl.BlockSpec((1, 1, DE, D), weights),
        ],
        out_specs=pl.BlockSpec((TM, HALF), row),
        scratch_shapes=[pltpu.VMEM((D, DE), BF16), pltpu.VMEM((D, DE), BF16), pltpu.VMEM((DE, D), BF16)],
    )
    return pl.pallas_call(
        _moe_kernel,
        grid_spec=grid_spec,
        out_shape=jax.ShapeDtypeStruct((NPAD, HALF), U32),
        compiler_params=_vmem_limit(40),
        name="moe_experts",
    )(tile_expert, weight_block, used, xs, w_gate, w_up, w_down)


def _combined(y0_ref, y1_ref, x_ref, wcol_ref, mod_ref):
    w0 = wcol_ref[:, 0:1]
    w1 = wcol_ref[:, 1:2]
    hi0, lo0 = _unpack_rows(y0_ref[...])
    hi1, lo1 = _unpack_rows(y1_ref[...])
    y = jnp.concatenate([w0 * hi0 + w1 * hi1, w0 * lo0 + w1 * lo1], axis=1)
    return x_ref[...] + mod_ref[0, 5:6, :] * y


def _combine_kernel(y0_ref, y1_ref, x_ref, wcol_ref, mod_ref, o_ref):
    o_ref[...] = _combined(y0_ref, y1_ref, x_ref, wcol_ref, mod_ref)


def _combine(group, y2g, x, wcol, mod):
    n_g = T_CTX if group == 0 else T_LAT
    tiles = n_g // CT
    first = 0 if group == 0 else T_CTX // CT
    return pl.pallas_call(
        _combine_kernel,
        grid=(tiles,),
        in_specs=[
            pl.BlockSpec((CT, HALF), lambda i: (i, 0)),
            pl.BlockSpec((CT, HALF), lambda i: (tiles + i, 0)),
            pl.BlockSpec((CT, D), lambda i: (first + i, 0)),
            pl.BlockSpec((CT, LANES), lambda i: (first + i, 0)),
            pl.BlockSpec((1, N_MOD, D), lambda i: (_cond_index(first + i, CT), 0, 0)),
        ],
        out_specs=pl.BlockSpec((CT, D), lambda i: (i, 0)),
        out_shape=jax.ShapeDtypeStruct((n_g, D), F32),
        name="moe_combine",
    )(y2g, y2g, x, wcol, mod)


def _moe(h, e, rank, cnt, layer, w_gate, w_up, w_down):
    pos, tile_expert, weight_block, used = _plan(e, rank, cnt)
    xs = _sc_dispatch(h, pos)
    ys = _experts(xs, tile_expert, weight_block, used, layer, w_gate, w_up, w_down)
    return _sc_gather(ys, pos, 0), _sc_gather(ys, pos, 1)


def kernel(x_prompt, x_sample, c, cache_k, cache_v, c_ctx, norm_mix_g, norm_ffn_g, ada_w, ada_b, attn_w_qkv, attn_q_norm, attn_k_norm, attn_w_o, pool_w, pool_scale, moe_w_group, moe_w_expert, moe_w_gate, moe_w_up, moe_w_down):
    xp = x_prompt.reshape(T_CTX, D)
    xs = x_sample.reshape(T_LAT, D)
    cond = jnp.concatenate([c_ctx[None, :], c, jnp.zeros((N_COND - 1 - DEC_BATCH, D), F32)], axis=0)
    mod = _ada(cond, ada_w, ada_b)

    qt, k2, vt, new_k, new_v = _qkv(xp, xs, mod[0], norm_mix_g[0:1], attn_w_qkv[0].astype(BF16),
                                    attn_q_norm[0:1], attn_k_norm[0:1])
    o = _attention(qt, k2, vt, cache_k[:, 0].reshape(DEC_BATCH, PAST, DKV),
                   cache_v[:, 0].reshape(DEC_BATCH, PAST, DKV))
    wr_hl, wr_hi = _router_weights(moe_w_group[0], moe_w_expert[0])
    x1, h, e, rank, wcol, cnt = _post(o, xp, xs, mod[0], attn_w_o[0].astype(BF16), norm_ffn_g[0:1],
                                      wr_hl, wr_hi)
    y2 = _moe(h, e, rank, cnt, 0, moe_w_gate, moe_w_up, moe_w_down)

    wr_hl, wr_hi = _router_weights(moe_w_group[1], moe_w_expert[1])
    pool_args = (x1, wcol, mod[0], mod[1], norm_mix_g[1:2], pool_w[0].astype(BF16), pool_scale[0:1],
                 norm_ffn_g[1:2], wr_hl, wr_hi)
    outs = _pool(0, y2[0], *pool_args, jnp.zeros((NE, LANES), F32), None)
    x3, h, e, rank, wcol, cnt = _pool(1, y2[1], *pool_args, outs[5], outs[:5])
    y2 = _moe(h, e, rank, cnt, 1, moe_w_gate, moe_w_up, moe_w_down)
    y_prompt = _combine(0, y2[0], x3, wcol, mod[1])
    y_sample = _combine(1, y2[1], x3, wcol, mod[1])

    return (y_prompt.reshape(BATCH, SEQ, D), y_sample.reshape(DEC_BATCH, DEC_SEQ, D),
            new_k.reshape(BATCH, 1, SEQ, N_KV, HD), new_v.reshape(BATCH, 1, SEQ, N_KV, HD))
```

```python
import functools

import jax
import jax.numpy as jnp
import numpy as np
from jax import lax
from jax.experimental import pallas as pl
from jax.experimental.pallas import tpu as pltpu
from jax.experimental.pallas import tpu_sc as plsc

F32 = jnp.float32
BF16 = jnp.bfloat16
I32 = jnp.int32
U32 = jnp.uint32

D = 1024
BATCH, SEQ = 32, 256
DEC_BATCH, DEC_SEQ, PAST = 8, 1024, 512
T_CTX = BATCH * SEQ
T_LAT = DEC_BATCH * DEC_SEQ
T = T_CTX + T_LAT
GRID_W = 64
N_HEADS, N_KV, HD = 16, 4, 64
DQ = N_HEADS * HD
DKV = N_KV * HD
ROPE_THETA = 10000.0
POOL_WINDOWS = (2, 4, 8, 16)
PGD = D // len(POOL_WINDOWS)
N_GROUPS, E_PER_G, TOP_K = 4, 8, 2
NE = N_GROUPS * E_PER_G
DE = D // 4
N_MOD = 6
EPS = 1e-6
N_COND = 16

TILE = 512
NT = T // TILE
NT_CTX = T_CTX // TILE
PTILE = 1024
PB = 128
PBW = 256
PB_HALO = (PBW - PB) // 2
TM = 1024
N_ROW_TILES = (TOP_K * T) // TM + NE
NPAD = N_ROW_TILES * TM
CT = 512
LANES = 128
NEG_INF = float("-inf")
LOG2_E = 1.4426950408889634
Q_SCALE = HD ** -0.5 * LOG2_E
HALF = D // 2
SC_CORES, SC_SUBCORES = 2, 16
SC_WORKERS = SC_CORES * SC_SUBCORES
SC_WIN = 64


def _vmem_limit(mib):
    return pltpu.CompilerParams(vmem_limit_bytes=mib * 1024 * 1024)


def _modulate(x, gain_scale, shift):
    ms = jnp.mean(x * x, axis=-1, keepdims=True)
    return (x * lax.rsqrt(ms + EPS)) * gain_scale + shift


def _head_norm(z, gain, ind, ind_t2):
    ss = jnp.dot((z * z).astype(BF16), ind, preferred_element_type=F32)
    inv = lax.rsqrt(ss * (1.0 / HD) + EPS)
    inv_hi = inv.astype(BF16)
    inv_lo = (inv - inv_hi.astype(F32)).astype(BF16)
    scale = jnp.dot(jnp.concatenate([inv_hi, inv_lo], axis=1), ind_t2, preferred_element_type=F32)
    return z * scale * gain


def _rope(z, cos_t, sin_t):
    lane = lax.broadcasted_iota(I32, (z.shape[0], LANES), 1)
    low = (lane % 32) < 16
    outs = []
    for c in range(z.shape[1] // LANES):
        zc = z[:, c * LANES:(c + 1) * LANES]
        up = pltpu.roll(zc, 16, axis=1)
        dn = pltpu.roll(zc, LANES - 16, axis=1)
        outs.append(zc * cos_t + jnp.where(low, dn, up) * sin_t)
    return jnp.concatenate(outs, axis=1)


def _rope_t(zt, cos_tt, sin_tt):
    sub = lax.broadcasted_iota(I32, (LANES, zt.shape[1]), 0)
    low = (sub % 32) < 16
    outs = []
    for c in range(zt.shape[0] // LANES):
        zc = zt[c * LANES:(c + 1) * LANES]
        up = pltpu.roll(zc, 16, axis=0)
        dn = pltpu.roll(zc, LANES - 16, axis=0)
        outs.append(zc * cos_tt + jnp.where(low, dn, up) * sin_tt)
    return jnp.concatenate(outs, axis=0)


def _pack_rows(z):
    bits = lax.bitcast_convert_type(z.astype(BF16).astype(F32), U32)
    return bits[:, :HALF] | (bits[:, HALF:] >> 16)


def _unpack_rows(p):
    hi = lax.bitcast_convert_type(p & jnp.uint32(0xFFFF0000), F32)
    lo = lax.bitcast_convert_type(p << 16, F32)
    return hi, lo


def _ada_kernel(c_ref, w_ref, b_ref, o_ref):
    c = c_ref[...]
    a = c * (1.0 / (1.0 + jnp.exp(-c)))
    o_ref[0] = jnp.dot(a.astype(BF16), w_ref[0].astype(BF16), preferred_element_type=F32) + b_ref[0]


def _ada(cond, ada_w, ada_b):
    nb = 1536
    depth = ada_w.shape[0]
    out = pl.pallas_call(
        _ada_kernel,
        grid=(depth, (N_MOD * D) // nb),
        in_specs=[
            pl.BlockSpec((N_COND, D), lambda l, j: (0, 0)),
            pl.BlockSpec((1, D, nb), lambda l, j: (l, 0, j)),
            pl.BlockSpec((1, 1, nb), lambda l, j: (l, 0, j)),
        ],
        out_specs=pl.BlockSpec((1, N_COND, nb), lambda l, j: (l, 0, j)),
        out_shape=jax.ShapeDtypeStruct((depth, N_COND, N_MOD * D), F32),
        compiler_params=_vmem_limit(40),
        name="ada",
    )(cond, ada_w, ada_b.reshape(depth, 1, N_MOD * D))
    return out.reshape(depth, N_COND, N_MOD, D)


def _cond_index(i, tile):
    n_ctx = T_CTX // tile
    per_batch = DEC_SEQ // tile
    return jnp.where(i < n_ctx, 0, 1 + (i - n_ctx) // per_batch)


def _qkv_kernel(xp_ref, xs_ref, mod_ref, g_ref, w_ref, qg_ref, kg_ref, indq_ref, indqt_ref,
                indk_ref, indkt_ref, cos_ref, sin_ref, cos_t_ref, sin_t_ref,
                qt_ref, k2_ref, vt_ref, nk_ref, nv_ref):
    i = pl.program_id(0)
    is_lat = i >= NT_CTX
    x = jnp.where(is_lat, xs_ref[...], xp_ref[...])
    h = _modulate(x, g_ref[...] * (1.0 + mod_ref[0, 1:2, :]), mod_ref[0, 0:1, :])
    qkv = jnp.dot(h.astype(BF16), w_ref[...], preferred_element_type=F32)
    q = _head_norm(qkv[:, :DQ], qg_ref[...], indq_ref[...], indqt_ref[...])
    k = _head_norm(qkv[:, DQ:DQ + DKV], kg_ref[...], indk_ref[...], indkt_ref[...])
    v = qkv[:, DQ + DKV:]

    vt_ref[...] = v.T.astype(BF16)

    qt = (q * Q_SCALE).T

    @pl.when(is_lat)
    def _():
        qt_ref[...] = _rope_t(qt, cos_t_ref[...], sin_t_ref[...]).astype(BF16)
        k2_ref[...] = _dup_heads(_rope(k, cos_ref[...], sin_ref[...])).astype(BF16)

    @pl.when(jnp.logical_not(is_lat))
    def _():
        qt_ref[...] = qt.astype(BF16)
        k2_ref[...] = _dup_heads(k).astype(BF16)
        nk_ref[...] = k
        nv_ref[...] = v


def _rope_tables():
    rows = DEC_SEQ // GRID_W
    row = np.broadcast_to(np.arange(rows, dtype=np.float32)[:, None], (rows, GRID_W)).reshape(-1)
    col = np.broadcast_to(np.arange(GRID_W, dtype=np.float32)[None, :], (rows, GRID_W)).reshape(-1)
    axis_dim = HD // 2
    inv_freq = np.power(np.float32(ROPE_THETA),
                        -np.arange(0, axis_dim, 2, dtype=np.float32) / np.float32(axis_dim))
    ang = np.concatenate([row[:, None] * inv_freq, col[:, None] * inv_freq], axis=-1).astype(np.float32)
    cos, sin = np.cos(ang), np.sin(ang)
    quarter = HD // 4
    cos_h = np.concatenate([cos[:, :quarter], cos[:, :quarter], cos[:, quarter:], cos[:, quarter:]], axis=1)
    sin_h = np.concatenate([-sin[:, :quarter], sin[:, :quarter], -sin[:, quarter:], sin[:, quarter:]], axis=1)
    return (np.tile(cos_h, (1, LANES // HD)).astype(np.float32),
            np.tile(sin_h, (1, LANES // HD)).astype(np.float32))


def _head_indicators(width):
    col = np.arange(width)[:, None]
    head = np.arange(LANES)[None, :]
    ind = (col // HD == head).astype(np.float32)
    return jnp.asarray(ind, BF16), jnp.asarray(np.concatenate([ind.T, ind.T], axis=0), BF16)


def _dup_heads(z):
    lane = lax.broadcasted_iota(I32, (z.shape[0], LANES), 1)
    cols = []
    for p in range(N_KV // 2):
        blk = z[:, p * LANES:(p + 1) * LANES]
        swp = pltpu.roll(blk, HD, axis=1)
        cols += [jnp.where(lane < HD, blk, swp), jnp.where(lane < HD, swp, blk)]
    return jnp.concatenate(cols, axis=1)


def _qkv(xp, xs, mod, g, w_qkv, q_gain, k_gain):
    indq, indqt = _head_indicators(DQ)
    indk, indkt = _head_indicators(DKV)
    cos_t, sin_t = _rope_tables()
    per_batch = DEC_SEQ // TILE
    const = lambda shape: pl.BlockSpec(shape, lambda i: (0,) * len(shape))
    return pl.pallas_call(
        _qkv_kernel,
        grid=(NT,),
        in_specs=[
            pl.BlockSpec((TILE, D), lambda i: (jnp.minimum(i, NT_CTX - 1), 0)),
            pl.BlockSpec((TILE, D), lambda i: (jnp.maximum(i - NT_CTX, 0), 0)),
            pl.BlockSpec((1, N_MOD, D), lambda i: (_cond_index(i, TILE), 0, 0)),
            const((1, D)),
            const((D, DQ + 2 * DKV)),
            const((1, DQ)), const((1, DKV)),
            const((DQ, LANES)), const((2 * LANES, DQ)),
            const((DKV, LANES)), const((2 * LANES, DKV)),
            pl.BlockSpec((TILE, LANES), lambda i: (jnp.maximum(i - NT_CTX, 0) % per_batch, 0)),
            pl.BlockSpec((TILE, LANES), lambda i: (jnp.maximum(i - NT_CTX, 0) % per_batch, 0)),
            pl.BlockSpec((LANES, TILE), lambda i: (0, jnp.maximum(i - NT_CTX, 0) % per_batch)),
            pl.BlockSpec((LANES, TILE), lambda i: (0, jnp.maximum(i - NT_CTX, 0) % per_batch)),
        ],
        out_specs=[
            pl.BlockSpec((DQ, TILE), lambda i: (0, i)),
            pl.BlockSpec((TILE, 2 * DKV), lambda i: (i, 0)),
            pl.BlockSpec((DKV, TILE), lambda i: (0, i)),
            pl.BlockSpec((TILE, DKV), lambda i: (jnp.minimum(i, NT_CTX - 1), 0)),
            pl.BlockSpec((TILE, DKV), lambda i: (jnp.minimum(i, NT_CTX - 1), 0)),
        ],
        out_shape=[
            jax.ShapeDtypeStruct((DQ, T), BF16),
            jax.ShapeDtypeStruct((T, 2 * DKV), BF16),
            jax.ShapeDtypeStruct((DKV, T), BF16),
            jax.ShapeDtypeStruct((T_CTX, DKV), F32),
            jax.ShapeDtypeStruct((T_CTX, DKV), F32),
        ],
        compiler_params=_vmem_limit(56),
        name="qkv",
    )(xp, xs, mod, g, w_qkv, jnp.tile(q_gain, (1, N_HEADS)), jnp.tile(k_gain, (1, N_KV)),
      indq, indqt, indk, indkt, cos_t, sin_t, cos_t.T, sin_t.T)


def _group_scores(g, k2_parts, qt_ref):
    cols = []
    for j in range(2):
        qt_pair = qt_ref[(2 * g + j) * LANES:(2 * g + j + 1) * LANES, :]
        sub = lax.broadcasted_iota(I32, qt_pair.shape, 0)
        zero = jnp.zeros_like(qt_pair)
        cols += [jnp.where(sub < HD, qt_pair, zero), jnp.where(sub >= HD, qt_pair, zero)]
    qt4 = jnp.concatenate(cols, axis=1)
    return [jnp.dot(k2, qt4, preferred_element_type=F32) for k2 in k2_parts]


def _softmax_values(ss, vt_parts):
    m = ss[0].max(axis=0, keepdims=True)
    for s in ss[1:]:
        m = jnp.maximum(m, s.max(axis=0, keepdims=True))
    acc = None
    l = None
    for s, vt in zip(ss, vt_parts):
        e = jnp.exp2(s - m)
        ls = e.sum(axis=0, keepdims=True)
        l = ls if l is None else l + ls
        t = jnp.dot(vt, e.astype(BF16), preferred_element_type=F32)
        acc = t if acc is None else acc + t
    return acc * (1.0 / l)


def _heads_to_rows(groups, lq):
    heads = [o[:, h * lq:(h + 1) * lq] for o in groups for h in range(N_HEADS // N_KV)]
    return jnp.concatenate(heads, axis=0).T


def _attend(keys_of, values_of, qt_ref, o_ref):
    groups = []
    ss = _group_scores(0, keys_of(0), qt_ref)
    for g in range(N_KV):
        ss_next = _group_scores(g + 1, keys_of(g + 1), qt_ref) if g + 1 < N_KV else None
        groups.append(_softmax_values(ss, values_of(g)))
        ss = ss_next
    o_ref[...] = _heads_to_rows(groups, o_ref.shape[0]).astype(BF16)


def _attn_ctx_kernel(qt_ref, k2_ref, vt_ref, o_ref):
    _attend(lambda g: [k2_ref[:, g * LANES:(g + 1) * LANES]],
            lambda g: [vt_ref[g * HD:(g + 1) * HD, :]], qt_ref, o_ref)


def _attn_lat_kernel(qt_ref, k2c_ref, vtc_ref, k2n_ref, vtn_ref, o_in_ref, o_ref):
    del o_in_ref
    _attend(lambda g: [k2c_ref[0, :, g * LANES:(g + 1) * LANES], k2n_ref[:, g * LANES:(g + 1) * LANES]],
            lambda g: [vtc_ref[0, g * HD:(g + 1) * HD, :], vtn_ref[g * HD:(g + 1) * HD, :]], qt_ref, o_ref)


def _cache_kernel(ck_ref, cv_ref, k2_ref, vt_ref):
    k2_ref[0] = _dup_heads(ck_ref[0]).astype(BF16)
    vt_ref[0] = cv_ref[0].T.astype(BF16)


def _attention(qt, k2, vt, cache_k, cache_v):
    k2c, vtc = pl.pallas_call(
        _cache_kernel,
        grid=(DEC_BATCH,),
        in_specs=[
            pl.BlockSpec((1, PAST, DKV), lambda b: (b, 0, 0)),
            pl.BlockSpec((1, PAST, DKV), lambda b: (b, 0, 0)),
        ],
        out_specs=[
            pl.BlockSpec((1, PAST, 2 * DKV), lambda b: (b, 0, 0)),
            pl.BlockSpec((1, DKV, PAST), lambda b: (b, 0, 0)),
        ],
        out_shape=[
            jax.ShapeDtypeStruct((DEC_BATCH, PAST, 2 * DKV), BF16),
            jax.ShapeDtypeStruct((DEC_BATCH, DKV, PAST), BF16),
        ],
        name="cache_prep",
    )(cache_k, cache_v)

    o_ctx = pl.pallas_call(
        _attn_ctx_kernel,
        grid=(BATCH,),
        in_specs=[
            pl.BlockSpec((DQ, SEQ), lambda b: (0, b)),
            pl.BlockSpec((SEQ, 2 * DKV), lambda b: (b, 0)),
            pl.BlockSpec((DKV, SEQ), lambda b: (0, b)),
        ],
        out_specs=pl.BlockSpec((SEQ, DQ), lambda b: (b, 0)),
        out_shape=jax.ShapeDtypeStruct((T, DQ), BF16),
        name="attn_ctx",
    )(qt, k2, vt)

    qb = 256
    nqb = DEC_SEQ // qb
    lat0 = T_CTX // qb
    return pl.pallas_call(
        _attn_lat_kernel,
        grid=(DEC_BATCH, nqb),
        in_specs=[
            pl.BlockSpec((DQ, qb), lambda b, j: (0, lat0 + b * nqb + j)),
            pl.BlockSpec((1, PAST, 2 * DKV), lambda b, j: (b, 0, 0)),
            pl.BlockSpec((1, DKV, PAST), lambda b, j: (b, 0, 0)),
            pl.BlockSpec((DEC_SEQ, 2 * DKV), lambda b, j: (T_CTX // DEC_SEQ + b, 0)),
            pl.BlockSpec((DKV, DEC_SEQ), lambda b, j: (0, T_CTX // DEC_SEQ + b)),
            pl.BlockSpec(memory_space=pl.ANY),
        ],
        out_specs=pl.BlockSpec((qb, DQ), lambda b, j: (lat0 + b * nqb + j, 0)),
        out_shape=jax.ShapeDtypeStruct((T, DQ), BF16),
        input_output_aliases={5: 0},
        compiler_params=_vmem_limit(56),
        name="attn_lat",
    )(qt, k2c, vtc, k2, vt, o_ctx)


def _route(h, wr_hl_ref, wr_hi_ref, tri_ref, cnt_ref, e_ref, rank_ref, wcol_ref, cnt_out_ref):
    tl = h.shape[0]
    h_hi = h.astype(BF16)
    h_lo = (h - h_hi.astype(F32)).astype(BF16)
    lg2 = jnp.dot(h_hi, wr_hl_ref[...], preferred_element_type=F32)
    lg = lg2[:, :LANES] + lg2[:, LANES:] + jnp.dot(h_lo, wr_hi_ref[...], preferred_element_type=F32)
    lt = lg.T

    sub8 = lax.broadcasted_iota(I32, (8, tl), 0).astype(F32)
    gl = jnp.where(sub8 < N_GROUPS, lt[0:8], NEG_INF)
    gmax = gl.max(axis=0, keepdims=True)
    gidx = jnp.min(jnp.where(gl == gmax, sub8, 8.0), axis=0, keepdims=True)
    gsum = jnp.sum(jnp.exp(gl - gmax), axis=0, keepdims=True)
    sel = lt[8:16]
    for g in range(1, N_GROUPS):
        sel = jnp.where(gidx == g, lt[8 + 8 * g:16 + 8 * g], sel)
    t1 = sel.max(axis=0, keepdims=True)
    j1 = jnp.min(jnp.where(sel == t1, sub8, 8.0), axis=0, keepdims=True)
    sel2 = jnp.where(sub8 == j1, NEG_INF, sel)
    t2 = sel2.max(axis=0, keepdims=True)
    j2 = jnp.min(jnp.where(sel2 == t2, sub8, 8.0), axis=0, keepdims=True)
    ex = jnp.exp(t2 - t1)
    den = 1.0 + ex
    gw = 1.0 / gsum
    w0 = gw * (1.0 / den)
    w1 = gw * (ex / den)
    e0 = gidx * E_PER_G + j1
    e1 = gidx * E_PER_G + j2

    sub_e = lax.broadcasted_iota(I32, (NE, tl), 0).astype(F32)
    oh0 = (sub_e == e0).astype(F32)
    oh1 = (sub_e == e1).astype(F32)
    c = oh0 + oh1
    carry = cnt_ref[:, 0:1]
    pieces = []
    for ch in range(tl // 256):
        cc = c[:, ch * 256:(ch + 1) * 256]
        pieces.append(jnp.dot(cc.astype(BF16), tri_ref[...], preferred_element_type=F32) + carry)
        carry = carry + jnp.sum(cc, axis=1, keepdims=True)
    csum = jnp.concatenate(pieces, axis=1)
    cnt_ref[...] = jnp.broadcast_to(carry, (NE, LANES))
    cnt_out_ref[...] = jnp.broadcast_to(carry, (NE, LANES))
    r0 = jnp.sum(oh0 * csum, axis=0, keepdims=True) - 1.0
    r1 = jnp.sum(oh1 * csum, axis=0, keepdims=True) - 1.0
    e_ref[0:1, :] = e0.astype(I32)
    e_ref[1:2, :] = e1.astype(I32)
    rank_ref[0:1, :] = r0.astype(I32)
    rank_ref[1:2, :] = r1.astype(I32)
    sub_w = lax.broadcasted_iota(I32, (LANES, tl), 0)
    w8 = jnp.where(sub_w == 0, w0, jnp.where(sub_w == 1, w1, 0.0))
    wcol_ref[...] = w8.T


def _post_kernel(o_ref, xp_ref, xs_ref, mod_ref, wo_ref, g_ref, wr_hl_ref, wr_hi_ref, tri_ref,
                 x1_ref, h_ref, e_ref, rank_ref, wcol_ref, cnt_out_ref, cnt_ref):
    i = pl.program_id(0)

    @pl.when(i == 0)
    def _():
        cnt_ref[...] = jnp.zeros_like(cnt_ref)

    x = jnp.where(i >= NT_CTX, xs_ref[...], xp_ref[...])
    m = jnp.dot(o_ref[...], wo_ref[...], preferred_element_type=F32)
    x1 = x + mod_ref[0, 2:3, :] * m
    x1_ref[...] = x1
    h = _modulate(x1, g_ref[...] * (1.0 + mod_ref[0, 4:5, :]), mod_ref[0, 3:4, :])
    h_ref[...] = _pack_rows(h)
    _route(h, wr_hl_ref, wr_hi_ref, tri_ref, cnt_ref, e_ref, rank_ref, wcol_ref, cnt_out_ref)


def _router_weights(w_group, w_expert):
    w = jnp.concatenate([w_group, jnp.zeros((D, 8 - N_GROUPS), F32),
                         jnp.transpose(w_expert, (1, 0, 2)).reshape(D, NE),
                         jnp.zeros((D, LANES - 8 - NE), F32)], axis=1)
    w_hi = w.astype(BF16)
    w_lo = (w - w_hi.astype(F32)).astype(BF16)
    return jnp.concatenate([w_hi, w_lo], axis=1), w_hi


def _route_outs(tile, first=0):
    specs = [
        pl.BlockSpec((tile, D), lambda i: (first + i, 0)),
        pl.BlockSpec((tile, HALF), lambda i: (first + i, 0)),
        pl.BlockSpec((TOP_K, tile), lambda i: (0, first + i)),
        pl.BlockSpec((TOP_K, tile), lambda i: (0, first + i)),
        pl.BlockSpec((tile, LANES), lambda i: (first + i, 0)),
        pl.BlockSpec((NE, LANES), lambda i: (0, 0)),
    ]
    shapes = [
        jax.ShapeDtypeStruct((T, D), F32),
        jax.ShapeDtypeStruct((T, HALF), U32),
        jax.ShapeDtypeStruct((TOP_K, T), I32),
        jax.ShapeDtypeStruct((TOP_K, T), I32),
        jax.ShapeDtypeStruct((T, LANES), F32),
        jax.ShapeDtypeStruct((NE, LANES), F32),
    ]
    return specs, shapes


def _tri():
    a = np.arange(256)
    return jnp.asarray((a[:, None] <= a[None, :]).astype(np.float32), BF16)


def _post(o, xp, xs, mod, w_o, g_ffn, wr_hl, wr_hi):
    const = lambda shape: pl.BlockSpec(shape, lambda i: (0,) * len(shape))
    out_specs, out_shape = _route_outs(TILE)
    return pl.pallas_call(
        _post_kernel,
        grid=(NT,),
        in_specs=[
            pl.BlockSpec((TILE, DQ), lambda i: (i, 0)),
            pl.BlockSpec((TILE, D), lambda i: (jnp.minimum(i, NT_CTX - 1), 0)),
            pl.BlockSpec((TILE, D), lambda i: (jnp.maximum(i - NT_CTX, 0), 0)),
            pl.BlockSpec((1, N_MOD, D), lambda i: (_cond_index(i, TILE), 0, 0)),
            const((DQ, D)), const((1, D)), const((D, 2 * LANES)), const((D, LANES)), const((256, 256)),
        ],
        out_specs=out_specs,
        out_shape=out_shape,
        scratch_shapes=[pltpu.VMEM((NE, LANES), F32)],
        compiler_params=_vmem_limit(56),
        name="post_attn",
    )(o, xp, xs, mod, w_o, g_ffn, wr_hl, wr_hi, _tri())


def _pool_kernel(seq, n_alias, y0_ref, y1_ref, xin_ref, wcin_ref, modp_ref, mod_ref, g_ref, pw_ref, ps_ref,
                 gf_ref, wr_hl_ref, wr_hi_ref, tri_ref, cnt_in_ref, *refs):
    (x1_ref, h_ref, e_ref, rank_ref, wcol_ref, cnt_out_ref, cnt_ref, hhi_ref, hlo_ref) = refs[n_alias:]
    i = pl.program_id(0)

    @pl.when(i == 0)
    def _():
        cnt_ref[...] = cnt_in_ref[...]

    x = _combined(y0_ref, y1_ref, xin_ref, wcin_ref, modp_ref)
    h = _modulate(x, g_ref[...] * (1.0 + mod_ref[0, 1:2, :]), mod_ref[0, 0:1, :])
    @pl.when(i == 0)
    def _():
        zeros = jnp.zeros((PB_HALO, D), BF16)
        for ref in (hhi_ref, hlo_ref):
            ref[0:PB_HALO, :] = zeros
            ref[PB_HALO + PTILE:PB_HALO + PTILE + PB_HALO, :] = zeros

    h_hi = h.astype(BF16)
    hhi_ref[PB_HALO:PB_HALO + PTILE, :] = h_hi
    hlo_ref[PB_HALO:PB_HALO + PTILE, :] = (h - h_hi.astype(F32)).astype(BF16)

    r_io = lax.broadcasted_iota(I32, (PB, PBW), 0)
    d = lax.broadcasted_iota(I32, (PB, PBW), 1) - PB_HALO - r_io
    r_col = lax.broadcasted_iota(I32, (PB, 1), 0)
    sums = [[None] * (PTILE // PB) for _ in POOL_WINDOWS]
    cnts = [[None] * (PTILE // PB) for _ in POOL_WINDOWS]
    for b in range(PTILE // PB):
        tm = (b * PB + r_io) & (seq - 1)
        inseq = jnp.logical_and(tm + d >= 0, tm + d < seq)
        tm_col = (b * PB + r_col) & (seq - 1)
        for gi, win in enumerate(POOL_WINDOWS):
            left = win // 2
            right = win - left - 1
            band = jnp.logical_and(inseq, jnp.logical_and(d >= -left, d <= right))
            bm = jnp.where(band, 1.0, 0.0).astype(BF16)
            cols = slice(gi * PGD, (gi + 1) * PGD)
            sums[gi][b] = (jnp.dot(bm, hhi_ref[b * PB:b * PB + PBW, cols], preferred_element_type=F32)
                           + jnp.dot(bm, hlo_ref[b * PB:b * PB + PBW, cols], preferred_element_type=F32))
            cnt = jnp.minimum(tm_col + right + 1, seq) - jnp.maximum(tm_col - left, 0)
            cnts[gi][b] = cnt.astype(F32)
    outs = []
    for gi in range(len(POOL_WINDOWS)):
        tot = jnp.concatenate(sums[gi], axis=0)
        cnt = jnp.concatenate(cnts[gi], axis=0)
        diff = tot / cnt - h[:, gi * PGD:(gi + 1) * PGD]
        outs.append(jnp.dot(diff.astype(BF16), pw_ref[gi], preferred_element_type=F32))
    m = jnp.concatenate(outs, axis=1) * ps_ref[...]
    x1 = x + mod_ref[0, 2:3, :] * m
    x1_ref[...] = x1
    hf = _modulate(x1, gf_ref[...] * (1.0 + mod_ref[0, 4:5, :]), mod_ref[0, 3:4, :])
    h_ref[...] = _pack_rows(hf)
    _route(hf, wr_hl_ref, wr_hi_ref, tri_ref, cnt_ref, e_ref, rank_ref, wcol_ref, cnt_out_ref)


def _pool(group, y2g, x, wcol, mod_prev, mod, g_mix, pool_w, pool_scale, g_ffn, wr_hl, wr_hi, cnt_in, prev):
    const = lambda shape: pl.BlockSpec(shape, lambda i: (0,) * len(shape))
    tiles = (T_CTX if group == 0 else T_LAT) // PTILE
    first = 0 if group == 0 else T_CTX // PTILE
    out_specs, out_shape = _route_outs(PTILE, first)
    aliases = () if prev is None else tuple(prev)
    n_fixed = 14
    return pl.pallas_call(
        functools.partial(_pool_kernel, SEQ if group == 0 else DEC_SEQ, len(aliases)),
        grid=(tiles,),
        in_specs=[
            pl.BlockSpec((PTILE, HALF), lambda i: (i, 0)),
            pl.BlockSpec((PTILE, HALF), lambda i: (tiles + i, 0)),
            pl.BlockSpec((PTILE, D), lambda i: (first + i, 0)),
            pl.BlockSpec((PTILE, LANES), lambda i: (first + i, 0)),
            pl.BlockSpec((1, N_MOD, D), lambda i: (_cond_index(first + i, PTILE), 0, 0)),
            pl.BlockSpec((1, N_MOD, D), lambda i: (_cond_index(first + i, PTILE), 0, 0)),
            const((1, D)), const((len(POOL_WINDOWS), PGD, PGD)), const((1, D)), const((1, D)),
            const((D, 2 * LANES)), const((D, LANES)), const((256, 256)), const((NE, LANES)),
        ] + [pl.BlockSpec(memory_space=pl.ANY)] * len(aliases),
        out_specs=out_specs,
        out_shape=out_shape,
        input_output_aliases={n_fixed + k: k for k in range(len(aliases))},
        scratch_shapes=[pltpu.VMEM((NE, LANES), F32), pltpu.VMEM((PTILE + 2 * PB_HALO, D), BF16),
                        pltpu.VMEM((PTILE + 2 * PB_HALO, D), BF16)],
        compiler_params=_vmem_limit(56),
        name="pool_mixer",
    )(y2g, y2g, x, wcol, mod_prev, mod, g_mix, pool_w, pool_scale, g_ffn, wr_hl, wr_hi, _tri(), cnt_in, *aliases)


def _plan(e, rank, cnt):
    counts = cnt[:, 0].astype(I32)
    n_tiles = (counts + TM - 1) // TM
    tile_end = jnp.cumsum(n_tiles)
    base = (tile_end - n_tiles) * TM
    pos = jnp.sum(jnp.where(e[None] == jnp.arange(NE, dtype=I32)[:, None, None], base[:, None, None], 0),
                  axis=0) + rank
    used = tile_end[-1]
    tile_id = jnp.minimum(jnp.arange(N_ROW_TILES, dtype=I32), used - 1)
    tile_expert = jnp.sum((tile_id[:, None] >= tile_end[None, :]).astype(I32), axis=1)
    experts = jnp.arange(NE, dtype=I32)
    owner = jnp.where(n_tiles > 0, experts, NE)
    later = jnp.flip(lax.cummin(jnp.flip(jnp.concatenate([owner[1:], jnp.full((1,), NE, I32)]))))
    next_owner = jnp.where(later < NE, later, experts)
    first = jnp.concatenate([jnp.ones((1,), bool), tile_expert[1:] != tile_expert[:-1]])
    weight_block = jnp.where(first, tile_expert, next_owner[tile_expert])
    return pos, tile_expert, weight_block, used.reshape(1)


def _sc_mesh():
    return plsc.VectorSubcoreMesh(core_axis_name="core", subcore_axis_name="subcore")


def _sc_worker():
    return lax.axis_index("core") * SC_SUBCORES + lax.axis_index("subcore")


def _sc_dispatch(h, pos):
    per = T // SC_WORKERS
    nwin = per // SC_WIN

    @functools.partial(
        pl.kernel, out_type=jax.ShapeDtypeStruct((NPAD, HALF), U32), mesh=_sc_mesh(),
        scratch_types=[pltpu.VMEM((2, SC_WIN, HALF), U32), pltpu.VMEM((2, SC_WIN), I32),
                       pltpu.VMEM((2, SC_WIN), I32), pltpu.SemaphoreType.DMA((2,)),
                       pltpu.SemaphoreType.DMA((2,))])
    def run(h_hbm, pos_hbm, xs_hbm, buf, idx0, idx1, sem_in, sem_out):
        first = _sc_worker() * per

        def loads(j, s):
            base = first + j * SC_WIN
            return (pltpu.make_async_copy(h_hbm.at[pl.ds(base, SC_WIN)], buf.at[s], sem_in.at[s]),
                    pltpu.make_async_copy(pos_hbm.at[pl.ds(base, SC_WIN)], idx0.at[s], sem_in.at[s]),
                    pltpu.make_async_copy(pos_hbm.at[pl.ds(T + base, SC_WIN)], idx1.at[s], sem_in.at[s]))

        def stores(s):
            return (pltpu.make_async_copy(buf.at[s], xs_hbm.at[idx0.at[s]], sem_out.at[s]),
                    pltpu.make_async_copy(buf.at[s], xs_hbm.at[idx1.at[s]], sem_out.at[s]))

        for c in loads(0, 0):
            c.start()

        @pl.loop(0, nwin // 2)
        def _(jj):
            for s in range(2):
                j = jj * 2 + s
                for c in loads(j, s):
                    c.wait()
                for c in stores(s):
                    c.start()

                @pl.when(j >= 1)
                def _():
                    for c in stores(1 - s):
                        c.wait()

                @pl.when(j + 1 < nwin)
                def _():
                    for c in loads(j + 1, 1 - s):
                        c.start()

        for c in stores((nwin - 1) % 2):
            c.wait()

    return run(h, pos.reshape(TOP_K * T))


def _sc_gather(ys, pos, group):
    t0, n_g = (0, T_CTX) if group == 0 else (T_CTX, T_LAT)
    idx_all = jnp.concatenate([pos[k, t0:t0 + n_g] for k in range(TOP_K)])
    n = TOP_K * n_g
    per = n // SC_WORKERS
    nwin = per // SC_WIN

    @functools.partial(
        pl.kernel, out_type=jax.ShapeDtypeStruct((n, HALF), U32), mesh=_sc_mesh(),
        scratch_types=[pltpu.VMEM((2, SC_WIN, HALF), U32), pltpu.VMEM((per,), I32),
                       pltpu.SemaphoreType.DMA((2,)), pltpu.SemaphoreType.DMA((2,))])
    def run(ys_hbm, pos_hbm, out_hbm, buf, idx, sem_in, sem_out):
        first = _sc_worker() * per
        pltpu.sync_copy(pos_hbm.at[pl.ds(first, per)], idx)

        def load(j, s):
            return pltpu.make_async_copy(ys_hbm.at[idx.at[pl.ds(j * SC_WIN, SC_WIN)]], buf.at[s], sem_in.at[s])

        def store(j, s):
            return pltpu.make_async_copy(buf.at[s], out_hbm.at[pl.ds(first + j * SC_WIN, SC_WIN)], sem_out.at[s])

        load(0, 0).start()

        @pl.loop(0, nwin // 2)
        def _(jj):
            for s in range(2):
                j = jj * 2 + s
                load(j, s).wait()
                store(j, s).start()

                @pl.when(j >= 1)
                def _():
                    store(j - 1, 1 - s).wait()

                @pl.when(j + 1 < nwin)
                def _():
                    load(j + 1, 1 - s).start()

        store(nwin - 1, (nwin - 1) % 2).wait()

    return run(ys, idx_all)


def _moe_kernel(te_ref, wb_ref, used_ref, x_ref, wg_ref, wu_ref, wd_ref, y_ref, wg_s, wu_s, wd_s):
    del wb_ref
    i = pl.program_id(0)

    @pl.when(i < used_ref[0])
    def _():
        prev = te_ref[jnp.maximum(i - 1, 0)]

        @pl.when(jnp.logical_or(i == 0, te_ref[i] != prev))
        def _():
            wg_s[...] = wg_ref[0, 0].astype(BF16)
            wu_s[...] = wu_ref[0, 0].astype(BF16)
            wd_s[...] = wd_ref[0, 0].astype(BF16)

        hi, lo = _unpack_rows(x_ref[...])
        x = jnp.concatenate([hi, lo], axis=1).astype(BF16)
        a = jnp.dot(x, wg_s[...], preferred_element_type=F32)
        b = jnp.dot(x, wu_s[...], preferred_element_type=F32)
        hid = a * (1.0 / (1.0 + jnp.exp(-a))) * b
        y_ref[...] = _pack_rows(jnp.dot(hid.astype(BF16), wd_s[...], preferred_element_type=F32))


def _experts(xs, tile_expert, weight_block, used, layer, w_gate, w_up, w_down):
    row = lambda i, te, wb, used: (jnp.minimum(i, used[0] - 1), 0)
    weights = lambda i, te, wb, used: (layer, wb[i], 0, 0)
    grid_spec = pltpu.PrefetchScalarGridSpec(
        num_scalar_prefetch=3,
        grid=(N_ROW_TILES,),
        in_specs=[
            pl.BlockSpec((TM, HALF), row),
            pl.BlockSpec((1, 1, D, DE), weights),
            pl.BlockSpec((1, 1, D, DE), weights),
            pl.BlockSpec((1, 1, DE, D), weights),
        ],
        out_specs=pl.BlockSpec((TM, HALF), row),
        scratch_shapes=[pltpu.VMEM((D, DE), BF16), pltpu.VMEM((D, DE), BF16), pltpu.VMEM((DE, D), BF16)],
    )
    return pl.pallas_call(
        _moe_kernel,
        grid_spec=grid_spec,
        out_shape=jax.ShapeDtypeStruct((NPAD, HALF), U32),
        compiler_params=_vmem_limit(40),
        name="moe_experts",
    )(tile_expert, weight_block, used, xs, w_gate, w_up, w_down)


def _combined(y0_ref, y1_ref, x_ref, wcol_ref, mod_ref):
    w0 = wcol_ref[:, 0:1]
    w1 = wcol_ref[:, 1:2]
    hi0, lo0 = _unpack_rows(y0_ref[...])
    hi1, lo1 = _unpack_rows(y1_ref[...])
    y = jnp.concatenate([w0 * hi0 + w1 * hi1, w0 * lo0 + w1 * lo1], axis=1)
    return x_ref[...] + mod_ref[0, 5:6, :] * y


def _combine_kernel(y0_ref, y1_ref, x_ref, wcol_ref, mod_ref, o_ref):
    o_ref[...] = _combined(y0_ref, y1_ref, x_ref, wcol_ref, mod_ref)


def _combine(group, y2g, x, wcol, mod):
    n_g = T_CTX if group == 0 else T_LAT
    tiles = n_g // CT
    first = 0 if group == 0 else T_CTX // CT
    return pl.pallas_call(
        _combine_kernel,
        grid=(tiles,),
        in_specs=[
            pl.BlockSpec((CT, HALF), lambda i: (i, 0)),
            pl.BlockSpec((CT, HALF), lambda i: (tiles + i, 0)),
            pl.BlockSpec((CT, D), lambda i: (first + i, 0)),
            pl.BlockSpec((CT, LANES), lambda i: (first + i, 0)),
            pl.BlockSpec((1, N_MOD, D), lambda i: (_cond_index(first + i, CT), 0, 0)),
        ],
        out_specs=pl.BlockSpec((CT, D), lambda i: (i, 0)),
        out_shape=jax.ShapeDtypeStruct((n_g, D), F32),
        name="moe_combine",
    )(y2g, y2g, x, wcol, mod)


def _moe(h, e, rank, cnt, layer, w_gate, w_up, w_down):
    pos, tile_expert, weight_block, used = _plan(e, rank, cnt)
    xs = _sc_dispatch(h, pos)
    ys = _experts(xs, tile_expert, weight_block, used, layer, w_gate, w_up, w_down)
    return _sc_gather(ys, pos, 0), _sc_gather(ys, pos, 1)


def kernel(x_prompt, x_sample, c, cache_k, cache_v, c_ctx, norm_mix_g, norm_ffn_g, ada_w, ada_b, attn_w_qkv, attn_q_norm, attn_k_norm, attn_w_o, pool_w, pool_scale, moe_w_group, moe_w_expert, moe_w_gate, moe_w_up, moe_w_down):
    xp = x_prompt.reshape(T_CTX, D)
    xs = x_sample.reshape(T_LAT, D)
    cond = jnp.concatenate([c_ctx[None, :], c, jnp.zeros((N_COND - 1 - DEC_BATCH, D), F32)], axis=0)
    mod = _ada(cond, ada_w, ada_b)

    qt, k2, vt, new_k, new_v = _qkv(xp, xs, mod[0], norm_mix_g[0:1], attn_w_qkv[0].astype(BF16),
                                    attn_q_norm[0:1], attn_k_norm[0:1])
    o = _attention(qt, k2, vt, cache_k[:, 0].reshape(DEC_BATCH, PAST, DKV),
                   cache_v[:, 0].reshape(DEC_BATCH, PAST, DKV))
    wr_hl, wr_hi = _router_weights(moe_w_group[0], moe_w_expert[0])
    x1, h, e, rank, wcol, cnt = _post(o, xp, xs, mod[0], attn_w_o[0].astype(BF16), norm_ffn_g[0:1],
                                      wr_hl, wr_hi)
    y2 = _moe(h, e, rank, cnt, 0, moe_w_gate, moe_w_up, moe_w_down)

    wr_hl, wr_hi = _router_weights(moe_w_group[1], moe_w_expert[1])
    pool_args = (x1, wcol, mod[0], mod[1], norm_mix_g[1:2], pool_w[0].astype(BF16), pool_scale[0:1],
                 norm_ffn_g[1:2], wr_hl, wr_hi)
    outs = _pool(0, y2[0], *pool_args, jnp.zeros((NE, LANES), F32), None)
    x3, h, e, rank, wcol, cnt = _pool(1, y2[1], *pool_args, outs[5], outs[:5])
    y2 = _moe(h, e, rank, cnt, 1, moe_w_gate, moe_w_up, moe_w_down)
    y_prompt = _combine(0, y2[0], x3, wcol, mod[1])
    y_sample = _combine(1, y2[1], x3, wcol, mod[1])

    return (y_prompt.reshape(BATCH, SEQ, D), y_sample.reshape(DEC_BATCH, DEC_SEQ, D),
            new_k.reshape(BATCH, 1, SEQ, N_KV, HD), new_v.reshape(BATCH, 1, SEQ, N_KV, HD))
```

```python
import functools

import jax
import jax.numpy as jnp
import numpy as np
from jax import lax
from jax.experimental import pallas as pl
from jax.experimental.pallas import tpu as pltpu
from jax.experimental.pallas import tpu_sc as plsc

F32 = jnp.float32
BF16 = jnp.bfloat16
I32 = jnp.int32
U32 = jnp.uint32

D = 1024
BATCH, SEQ = 32, 256
DEC_BATCH, DEC_SEQ, PAST = 8, 1024, 512
T_CTX = BATCH * SEQ
T_LAT = DEC_BATCH * DEC_SEQ
T = T_CTX + T_LAT
GRID_W = 64
N_HEADS, N_KV, HD = 16, 4, 64
DQ = N_HEADS * HD
DKV = N_KV * HD
ROPE_THETA = 10000.0
POOL_WINDOWS = (2, 4, 8, 16)
PGD = D // len(POOL_WINDOWS)
N_GROUPS, E_PER_G, TOP_K = 4, 8, 2
NE = N_GROUPS * E_PER_G
DE = D // 4
N_MOD = 6
EPS = 1e-6
N_COND = 16

TILE = 512
NT = T // TILE
NT_CTX = T_CTX // TILE
QB = 256
PTILE = 1024
PB = 128
PBW = 256
PB_HALO = (PBW - PB) // 2
TM = 1024
N_ROW_TILES = (TOP_K * T) // TM + NE
NPAD = N_ROW_TILES * TM
CT = 512
LANES = 128
NEG_INF = float("-inf")
LOG2_E = 1.4426950408889634
Q_SCALE = HD ** -0.5 * LOG2_E
HALF = D // 2
SC_CORES, SC_SUBCORES = 2, 16
SC_WORKERS = SC_CORES * SC_SUBCORES
SC_WIN = 64


def _vmem_limit(mib):
    return pltpu.CompilerParams(vmem_limit_bytes=mib * 1024 * 1024)


def _modulate(x, gain_scale, shift):
    ms = jnp.mean(x * x, axis=-1, keepdims=True)
    return (x * lax.rsqrt(ms + EPS)) * gain_scale + shift


def _head_norm(z, gain, ind, ind_t2):
    ss = jnp.dot((z * z).astype(BF16), ind, preferred_element_type=F32)
    inv = lax.rsqrt(ss * (1.0 / HD) + EPS)
    inv_hi = inv.astype(BF16)
    inv_lo = (inv - inv_hi.astype(F32)).astype(BF16)
    scale = jnp.dot(jnp.concatenate([inv_hi, inv_lo], axis=1), ind_t2, preferred_element_type=F32)
    return z * scale * gain


def _rope(z, cos_t, sin_t):
    lane = lax.broadcasted_iota(I32, (z.shape[0], LANES), 1)
    low = (lane % 32) < 16
    outs = []
    for c in range(z.shape[1] // LANES):
        zc = z[:, c * LANES:(c + 1) * LANES]
        up = pltpu.roll(zc, 16, axis=1)
        dn = pltpu.roll(zc, LANES - 16, axis=1)
        outs.append(zc * cos_t + jnp.where(low, dn, up) * sin_t)
    return jnp.concatenate(outs, axis=1)


def _rope_t(zt, cos_tt, sin_tt):
    sub = lax.broadcasted_iota(I32, (LANES, zt.shape[1]), 0)
    low = (sub % 32) < 16
    outs = []
    for c in range(zt.shape[0] // LANES):
        zc = zt[c * LANES:(c + 1) * LANES]
        up = pltpu.roll(zc, 16, axis=0)
        dn = pltpu.roll(zc, LANES - 16, axis=0)
        outs.append(zc * cos_tt + jnp.where(low, dn, up) * sin_tt)
    return jnp.concatenate(outs, axis=0)


def _pack_rows(z):
    bits = lax.bitcast_convert_type(z.astype(BF16).astype(F32), U32)
    return bits[:, :HALF] | (bits[:, HALF:] >> 16)


def _unpack_rows(p):
    hi = lax.bitcast_convert_type(p & jnp.uint32(0xFFFF0000), F32)
    lo = lax.bitcast_convert_type(p << 16, F32)
    return hi, lo


def _ada_kernel(c_ref, w_ref, b_ref, o_ref):
    c = c_ref[...]
    a = c * (1.0 / (1.0 + jnp.exp(-c)))
    o_ref[0] = jnp.dot(a.astype(BF16), w_ref[0].astype(BF16), preferred_element_type=F32) + b_ref[0]


def _ada(cond, ada_w, ada_b):
    nb = 1536
    depth = ada_w.shape[0]
    out = pl.pallas_call(
        _ada_kernel,
        grid=(depth, (N_MOD * D) // nb),
        in_specs=[
            pl.BlockSpec((N_COND, D), lambda l, j: (0, 0)),
            pl.BlockSpec((1, D, nb), lambda l, j: (l, 0, j)),
            pl.BlockSpec((1, 1, nb), lambda l, j: (l, 0, j)),
        ],
        out_specs=pl.BlockSpec((1, N_COND, nb), lambda l, j: (l, 0, j)),
        out_shape=jax.ShapeDtypeStruct((depth, N_COND, N_MOD * D), F32),
        compiler_params=_vmem_limit(40),
        name="ada",
    )(cond, ada_w, ada_b.reshape(depth, 1, N_MOD * D))
    return out.reshape(depth, N_COND, N_MOD, D)


def _cond_index(i, tile):
    n_ctx = T_CTX // tile
    per_batch = DEC_SEQ // tile
    return jnp.where(i < n_ctx, 0, 1 + (i - n_ctx) // per_batch)


def _qkv_kernel(xp_ref, xs_ref, mod_ref, g_ref, w_ref, qg_ref, kg_ref, indq_ref, indqt_ref,
                indk_ref, indkt_ref, cos_ref, sin_ref, cos_t_ref, sin_t_ref,
                qt_ref, k2_ref, vt_ref, nk_ref, nv_ref):
    i = pl.program_id(0)
    is_lat = i >= NT_CTX
    x = jnp.where(is_lat, xs_ref[...], xp_ref[...])
    h = _modulate(x, g_ref[...] * (1.0 + mod_ref[0, 1:2, :]), mod_ref[0, 0:1, :])
    qkv = jnp.dot(h.astype(BF16), w_ref[...], preferred_element_type=F32)
    q = _head_norm(qkv[:, :DQ], qg_ref[...], indq_ref[...], indqt_ref[...])
    k = _head_norm(qkv[:, DQ:DQ + DKV], kg_ref[...], indk_ref[...], indkt_ref[...])
    v = qkv[:, DQ + DKV:]

    vt_ref[...] = v.T.astype(BF16)

    qt = (q * Q_SCALE).T

    def store_qt(z):
        for b in range(TILE // QB):
            qt_ref[b] = z[:, b * QB:(b + 1) * QB].astype(BF16)

    @pl.when(is_lat)
    def _():
        store_qt(_rope_t(qt, cos_t_ref[...], sin_t_ref[...]))
        k2_ref[...] = _dup_heads(_rope(k, cos_ref[...], sin_ref[...])).astype(BF16)

    @pl.when(jnp.logical_not(is_lat))
    def _():
        store_qt(qt)
        k2_ref[...] = _dup_heads(k).astype(BF16)
        nk_ref[...] = k
        nv_ref[...] = v


def _rope_tables():
    rows = DEC_SEQ // GRID_W
    row = np.broadcast_to(np.arange(rows, dtype=np.float32)[:, None], (rows, GRID_W)).reshape(-1)
    col = np.broadcast_to(np.arange(GRID_W, dtype=np.float32)[None, :], (rows, GRID_W)).reshape(-1)
    axis_dim = HD // 2
    inv_freq = np.power(np.float32(ROPE_THETA),
                        -np.arange(0, axis_dim, 2, dtype=np.float32) / np.float32(axis_dim))
    ang = np.concatenate([row[:, None] * inv_freq, col[:, None] * inv_freq], axis=-1).astype(np.float32)
    cos, sin = np.cos(ang), np.sin(ang)
    quarter = HD // 4
    cos_h = np.concatenate([cos[:, :quarter], cos[:, :quarter], cos[:, quarter:], cos[:, quarter:]], axis=1)
    sin_h = np.concatenate([-sin[:, :quarter], sin[:, :quarter], -sin[:, quarter:], sin[:, quarter:]], axis=1)
    return (np.tile(cos_h, (1, LANES // HD)).astype(np.float32),
            np.tile(sin_h, (1, LANES // HD)).astype(np.float32))


def _head_indicators(width):
    col = np.arange(width)[:, None]
    head = np.arange(LANES)[None, :]
    ind = (col // HD == head).astype(np.float32)
    return jnp.asarray(ind, BF16), jnp.asarray(np.concatenate([ind.T, ind.T], axis=0), BF16)


def _dup_heads(z):
    lane = lax.broadcasted_iota(I32, (z.shape[0], LANES), 1)
    cols = []
    for p in range(N_KV // 2):
        blk = z[:, p * LANES:(p + 1) * LANES]
        swp = pltpu.roll(blk, HD, axis=1)
        cols += [jnp.where(lane < HD, blk, swp), jnp.where(lane < HD, swp, blk)]
    return jnp.concatenate(cols, axis=1)


def _qkv(xp, xs, mod, g, w_qkv, q_gain, k_gain):
    indq, indqt = _head_indicators(DQ)
    indk, indkt = _head_indicators(DKV)
    cos_t, sin_t = _rope_tables()
    per_batch = DEC_SEQ // TILE
    const = lambda shape: pl.BlockSpec(shape, lambda i: (0,) * len(shape))
    return pl.pallas_call(
        _qkv_kernel,
        grid=(NT,),
        in_specs=[
            pl.BlockSpec((TILE, D), lambda i: (jnp.minimum(i, NT_CTX - 1), 0)),
            pl.BlockSpec((TILE, D), lambda i: (jnp.maximum(i - NT_CTX, 0), 0)),
            pl.BlockSpec((1, N_MOD, D), lambda i: (_cond_index(i, TILE), 0, 0)),
            const((1, D)),
            const((D, DQ + 2 * DKV)),
            const((1, DQ)), const((1, DKV)),
            const((DQ, LANES)), const((2 * LANES, DQ)),
            const((DKV, LANES)), const((2 * LANES, DKV)),
            pl.BlockSpec((TILE, LANES), lambda i: (jnp.maximum(i - NT_CTX, 0) % per_batch, 0)),
            pl.BlockSpec((TILE, LANES), lambda i: (jnp.maximum(i - NT_CTX, 0) % per_batch, 0)),
            pl.BlockSpec((LANES, TILE), lambda i: (0, jnp.maximum(i - NT_CTX, 0) % per_batch)),
            pl.BlockSpec((LANES, TILE), lambda i: (0, jnp.maximum(i - NT_CTX, 0) % per_batch)),
        ],
        out_specs=[
            pl.BlockSpec((TILE // QB, DQ, QB), lambda i: (i, 0, 0)),
            pl.BlockSpec((TILE, 2 * DKV), lambda i: (i, 0)),
            pl.BlockSpec((DKV, TILE), lambda i: (0, i)),
            pl.BlockSpec((TILE, DKV), lambda i: (jnp.minimum(i, NT_CTX - 1), 0)),
            pl.BlockSpec((TILE, DKV), lambda i: (jnp.minimum(i, NT_CTX - 1), 0)),
        ],
        out_shape=[
            jax.ShapeDtypeStruct((T // QB, DQ, QB), BF16),
            jax.ShapeDtypeStruct((T, 2 * DKV), BF16),
            jax.ShapeDtypeStruct((DKV, T), BF16),
            jax.ShapeDtypeStruct((T_CTX, DKV), F32),
            jax.ShapeDtypeStruct((T_CTX, DKV), F32),
        ],
        compiler_params=_vmem_limit(56),
        name="qkv",
    )(xp, xs, mod, g, w_qkv, jnp.tile(q_gain, (1, N_HEADS)), jnp.tile(k_gain, (1, N_KV)),
      indq, indqt, indk, indkt, cos_t, sin_t, cos_t.T, sin_t.T)


def _group_scores(g, k2_parts, qt_ref):
    cols = []
    for j in range(2):
        qt_pair = qt_ref[(2 * g + j) * LANES:(2 * g + j + 1) * LANES, :]
        sub = lax.broadcasted_iota(I32, qt_pair.shape, 0)
        zero = jnp.zeros_like(qt_pair)
        cols += [jnp.where(sub < HD, qt_pair, zero), jnp.where(sub >= HD, qt_pair, zero)]
    qt4 = jnp.concatenate(cols, axis=1)
    return [jnp.dot(k2, qt4, preferred_element_type=F32) for k2 in k2_parts]


def _softmax_values(ss, vt_parts):
    m = ss[0].max(axis=0, keepdims=True)
    for s in ss[1:]:
        m = jnp.maximum(m, s.max(axis=0, keepdims=True))
    acc = None
    l = None
    for s, vt in zip(ss, vt_parts):
        e = jnp.exp2(s - m)
        ls = e.sum(axis=0, keepdims=True)
        l = ls if l is None else l + ls
        t = jnp.dot(vt, e.astype(BF16), preferred_element_type=F32)
        acc = t if acc is None else acc + t
    return acc * (1.0 / l)


def _heads_to_rows(groups, lq):
    heads = [o[:, h * lq:(h + 1) * lq] for o in groups for h in range(N_HEADS // N_KV)]
    return jnp.concatenate(heads, axis=0).T


def _attend(keys_of, values_of, qt_ref, o_ref):
    groups = []
    ss = _group_scores(0, keys_of(0), qt_ref)
    for g in range(N_KV):
        ss_next = _group_scores(g + 1, keys_of(g + 1), qt_ref) if g + 1 < N_KV else None
        groups.append(_softmax_values(ss, values_of(g)))
        ss = ss_next
    o_ref[...] = _heads_to_rows(groups, o_ref.shape[0]).astype(BF16)


def _attn_ctx_kernel(qt_ref, k2_ref, vt_ref, o_ref):
    _attend(lambda g: [k2_ref[:, g * LANES:(g + 1) * LANES]],
            lambda g: [vt_ref[g * HD:(g + 1) * HD, :]], qt_ref.at[0], o_ref)


def _attn_lat_kernel(qt_ref, k2c_ref, vtc_ref, k2n_ref, vtn_ref, wg_ref, wu_ref, wd_ref, o_in_ref,
                     o_ref, wgb_ref, wub_ref, wdb_ref):
    del o_in_ref
    wgb_ref[...] = wg_ref[...].astype(BF16)
    wub_ref[...] = wu_ref[...].astype(BF16)
    wdb_ref[...] = wd_ref[...].astype(BF16)
    _attend(lambda g: [k2c_ref[0, :, g * LANES:(g + 1) * LANES], k2n_ref[:, g * LANES:(g + 1) * LANES]],
            lambda g: [vtc_ref[0, g * HD:(g + 1) * HD, :], vtn_ref[g * HD:(g + 1) * HD, :]], qt_ref.at[0],
            o_ref)


def _cache_kernel(ck_ref, cv_ref, k2_ref, vt_ref):
    k2_ref[0] = _dup_heads(ck_ref[0]).astype(BF16)
    vt_ref[0] = cv_ref[0].T.astype(BF16)


def _attention(qt, k2, vt, cache_k, cache_v, w_gate, w_up, w_down):
    k2c, vtc = pl.pallas_call(
        _cache_kernel,
        grid=(DEC_BATCH,),
        in_specs=[
            pl.BlockSpec((1, PAST, DKV), lambda b: (b, 0, 0)),
            pl.BlockSpec((1, PAST, DKV), lambda b: (b, 0, 0)),
        ],
        out_specs=[
            pl.BlockSpec((1, PAST, 2 * DKV), lambda b: (b, 0, 0)),
            pl.BlockSpec((1, DKV, PAST), lambda b: (b, 0, 0)),
        ],
        out_shape=[
            jax.ShapeDtypeStruct((DEC_BATCH, PAST, 2 * DKV), BF16),
            jax.ShapeDtypeStruct((DEC_BATCH, DKV, PAST), BF16),
        ],
        name="cache_prep",
    )(cache_k, cache_v)

    o_ctx = pl.pallas_call(
        _attn_ctx_kernel,
        grid=(BATCH,),
        in_specs=[
            pl.BlockSpec((1, DQ, QB), lambda b: (b, 0, 0)),
            pl.BlockSpec((SEQ, 2 * DKV), lambda b: (b, 0)),
            pl.BlockSpec((DKV, SEQ), lambda b: (0, b)),
        ],
        out_specs=pl.BlockSpec((SEQ, DQ), lambda b: (b, 0)),
        out_shape=jax.ShapeDtypeStruct((T, DQ), BF16),
        name="attn_ctx",
    )(qt, k2, vt)

    qb = QB
    nqb = DEC_SEQ // qb
    lat0 = T_CTX // qb
    depth = w_gate.shape[0]
    eps = (depth * NE) // (DEC_BATCH * nqb)
    per_layer = NE // eps
    wblock = lambda b, j: ((b * nqb + j) // per_layer, (b * nqb + j) % per_layer, 0, 0)
    return pl.pallas_call(
        _attn_lat_kernel,
        grid=(DEC_BATCH, nqb),
        in_specs=[
            pl.BlockSpec((1, DQ, QB), lambda b, j: (lat0 + b * nqb + j, 0, 0)),
            pl.BlockSpec((1, PAST, 2 * DKV), lambda b, j: (b, 0, 0)),
            pl.BlockSpec((1, DKV, PAST), lambda b, j: (b, 0, 0)),
            pl.BlockSpec((DEC_SEQ, 2 * DKV), lambda b, j: (T_CTX // DEC_SEQ + b, 0)),
            pl.BlockSpec((DKV, DEC_SEQ), lambda b, j: (0, T_CTX // DEC_SEQ + b)),
            pl.BlockSpec((1, eps, D, DE), wblock),
            pl.BlockSpec((1, eps, D, DE), wblock),
            pl.BlockSpec((1, eps, DE, D), wblock),
            pl.BlockSpec(memory_space=pl.ANY),
        ],
        out_specs=[
            pl.BlockSpec((qb, DQ), lambda b, j: (lat0 + b * nqb + j, 0)),
            pl.BlockSpec((1, eps, D, DE), wblock),
            pl.BlockSpec((1, eps, D, DE), wblock),
            pl.BlockSpec((1, eps, DE, D), wblock),
        ],
        out_shape=[
            jax.ShapeDtypeStruct((T, DQ), BF16),
            jax.ShapeDtypeStruct(w_gate.shape, BF16),
            jax.ShapeDtypeStruct(w_up.shape, BF16),
            jax.ShapeDtypeStruct(w_down.shape, BF16),
        ],
        input_output_aliases={8: 0},
        compiler_params=_vmem_limit(56),
        name="attn_lat",
    )(qt, k2c, vtc, k2, vt, w_gate, w_up, w_down, o_ctx)


def _route(h, wr_hl_ref, wr_hi_ref, tri_ref, cnt_ref, e_ref, rank_ref, wcol_ref, cnt_out_ref):
    tl = h.shape[0]
    h_hi = h.astype(BF16)
    h_lo = (h - h_hi.astype(F32)).astype(BF16)
    lg2 = jnp.dot(h_hi, wr_hl_ref[...], preferred_element_type=F32)
    lg = lg2[:, :LANES] + lg2[:, LANES:] + jnp.dot(h_lo, wr_hi_ref[...], preferred_element_type=F32)
    lt = lg.T

    sub8 = lax.broadcasted_iota(I32, (8, tl), 0).astype(F32)
    gl = jnp.where(sub8 < N_GROUPS, lt[0:8], NEG_INF)
    gmax = gl.max(axis=0, keepdims=True)
    gidx = jnp.min(jnp.where(gl == gmax, sub8, 8.0), axis=0, keepdims=True)
    gsum = jnp.sum(jnp.exp(gl - gmax), axis=0, keepdims=True)
    sel = lt[8:16]
    for g in range(1, N_GROUPS):
        sel = jnp.where(gidx == g, lt[8 + 8 * g:16 + 8 * g], sel)
    t1 = sel.max(axis=0, keepdims=True)
    j1 = jnp.min(jnp.where(sel == t1, sub8, 8.0), axis=0, keepdims=True)
    sel2 = jnp.where(sub8 == j1, NEG_INF, sel)
    t2 = sel2.max(axis=0, keepdims=True)
    j2 = jnp.min(jnp.where(sel2 == t2, sub8, 8.0), axis=0, keepdims=True)
    ex = jnp.exp(t2 - t1)
    den = 1.0 + ex
    gw = 1.0 / gsum
    w0 = gw * (1.0 / den)
    w1 = gw * (ex / den)
    e0 = gidx * E_PER_G + j1
    e1 = gidx * E_PER_G + j2

    sub_e = lax.broadcasted_iota(I32, (NE, tl), 0).astype(F32)
    oh0 = (sub_e == e0).astype(F32)
    oh1 = (sub_e == e1).astype(F32)
    c = oh0 + oh1
    carry = cnt_ref[:, 0:1]
    pieces = []
    for ch in range(tl // 256):
        cc = c[:, ch * 256:(ch + 1) * 256]
        pieces.append(jnp.dot(cc.astype(BF16), tri_ref[...], preferred_element_type=F32) + carry)
        carry = carry + jnp.sum(cc, axis=1, keepdims=True)
    csum = jnp.concatenate(pieces, axis=1)
    cnt_ref[...] = jnp.broadcast_to(carry, (NE, LANES))
    cnt_out_ref[...] = jnp.broadcast_to(carry, (NE, LANES))
    r0 = jnp.sum(oh0 * csum, axis=0, keepdims=True) - 1.0
    r1 = jnp.sum(oh1 * csum, axis=0, keepdims=True) - 1.0
    e_ref[0:1, :] = e0.astype(I32)
    e_ref[1:2, :] = e1.astype(I32)
    rank_ref[0:1, :] = r0.astype(I32)
    rank_ref[1:2, :] = r1.astype(I32)
    sub_w = lax.broadcasted_iota(I32, (LANES, tl), 0)
    w8 = jnp.where(sub_w == 0, w0, jnp.where(sub_w == 1, w1, 0.0))
    wcol_ref[...] = w8.T


def _post_kernel(o_ref, xp_ref, xs_ref, mod_ref, wo_ref, g_ref, wr_hl_ref, wr_hi_ref, tri_ref,
                 x1_ref, h_ref, e_ref, rank_ref, wcol_ref, cnt_out_ref, cnt_ref):
    i = pl.program_id(0)

    @pl.when(i == 0)
    def _():
        cnt_ref[...] = jnp.zeros_like(cnt_ref)

    x = jnp.where(i >= NT_CTX, xs_ref[...], xp_ref[...])
    m = jnp.dot(o_ref[...], wo_ref[...], preferred_element_type=F32)
    x1 = x + mod_ref[0, 2:3, :] * m
    x1_ref[...] = x1
    h = _modulate(x1, g_ref[...] * (1.0 + mod_ref[0, 4:5, :]), mod_ref[0, 3:4, :])
    h_ref[...] = _pack_rows(h)
    _route(h, wr_hl_ref, wr_hi_ref, tri_ref, cnt_ref, e_ref, rank_ref, wcol_ref, cnt_out_ref)


def _router_weights(w_group, w_expert):
    w = jnp.concatenate([w_group, jnp.zeros((D, 8 - N_GROUPS), F32),
                         jnp.transpose(w_expert, (1, 0, 2)).reshape(D, NE),
                         jnp.zeros((D, LANES - 8 - NE), F32)], axis=1)
    w_hi = w.astype(BF16)
    w_lo = (w - w_hi.astype(F32)).astype(BF16)
    return jnp.concatenate([w_hi, w_lo], axis=1), w_hi


def _route_outs(tile, first=0):
    specs = [
        pl.BlockSpec((tile, D), lambda i: (first + i, 0)),
        pl.BlockSpec((tile, HALF), lambda i: (first + i, 0)),
        pl.BlockSpec((TOP_K, tile), lambda i: (0, first + i)),
        pl.BlockSpec((TOP_K, tile), lambda i: (0, first + i)),
        pl.BlockSpec((tile, LANES), lambda i: (first + i, 0)),
        pl.BlockSpec((NE, LANES), lambda i: (0, 0)),
    ]
    shapes = [
        jax.ShapeDtypeStruct((T, D), F32),
        jax.ShapeDtypeStruct((T, HALF), U32),
        jax.ShapeDtypeStruct((TOP_K, T), I32),
        jax.ShapeDtypeStruct((TOP_K, T), I32),
        jax.ShapeDtypeStruct((T, LANES), F32),
        jax.ShapeDtypeStruct((NE, LANES), F32),
    ]
    return specs, shapes


def _tri():
    a = np.arange(256)
    return jnp.asarray((a[:, None] <= a[None, :]).astype(np.float32), BF16)


def _post(o, xp, xs, mod, w_o, g_ffn, wr_hl, wr_hi):
    const = lambda shape: pl.BlockSpec(shape, lambda i: (0,) * len(shape))
    out_specs, out_shape = _route_outs(TILE)
    return pl.pallas_call(
        _post_kernel,
        grid=(NT,),
        in_specs=[
            pl.BlockSpec((TILE, DQ), lambda i: (i, 0)),
            pl.BlockSpec((TILE, D), lambda i: (jnp.minimum(i, NT_CTX - 1), 0)),
            pl.BlockSpec((TILE, D), lambda i: (jnp.maximum(i - NT_CTX, 0), 0)),
            pl.BlockSpec((1, N_MOD, D), lambda i: (_cond_index(i, TILE), 0, 0)),
            const((DQ, D)), const((1, D)), const((D, 2 * LANES)), const((D, LANES)), const((256, 256)),
        ],
        out_specs=out_specs,
        out_shape=out_shape,
        scratch_shapes=[pltpu.VMEM((NE, LANES), F32)],
        compiler_params=_vmem_limit(56),
        name="post_attn",
    )(o, xp, xs, mod, w_o, g_ffn, wr_hl, wr_hi, _tri())


def _pool_kernel(seq, n_alias, y0_ref, y1_ref, xin_ref, wcin_ref, modp_ref, mod_ref, g_ref, pw_ref, ps_ref,
                 gf_ref, wr_hl_ref, wr_hi_ref, tri_ref, cnt_in_ref, *refs):
    (x1_ref, h_ref, e_ref, rank_ref, wcol_ref, cnt_out_ref, cnt_ref, hhi_ref, hlo_ref) = refs[n_alias:]
    i = pl.program_id(0)

    @pl.when(i == 0)
    def _():
        cnt_ref[...] = cnt_in_ref[...]

    x = _combined(y0_ref, y1_ref, xin_ref, wcin_ref, modp_ref)
    h = _modulate(x, g_ref[...] * (1.0 + mod_ref[0, 1:2, :]), mod_ref[0, 0:1, :])
    @pl.when(i == 0)
    def _():
        zeros = jnp.zeros((PB_HALO, D), BF16)
        for ref in (hhi_ref, hlo_ref):
            ref[0:PB_HALO, :] = zeros
            ref[PB_HALO + PTILE:PB_HALO + PTILE + PB_HALO, :] = zeros

    h_hi = h.astype(BF16)
    hhi_ref[PB_HALO:PB_HALO + PTILE, :] = h_hi
    hlo_ref[PB_HALO:PB_HALO + PTILE, :] = (h - h_hi.astype(F32)).astype(BF16)

    r_io = lax.broadcasted_iota(I32, (PB, PBW), 0)
    d = lax.broadcasted_iota(I32, (PB, PBW), 1) - PB_HALO - r_io
    r_col = lax.broadcasted_iota(I32, (PB, 1), 0)
    sums = [[None] * (PTILE // PB) for _ in POOL_WINDOWS]
    cnts = [[None] * (PTILE // PB) for _ in POOL_WINDOWS]
    for b in range(PTILE // PB):
        tm = (b * PB + r_io) & (seq - 1)
        inseq = jnp.logical_and(tm + d >= 0, tm + d < seq)
        tm_col = (b * PB + r_col) & (seq - 1)
        for gi, win in enumerate(POOL_WINDOWS):
            left = win // 2
            right = win - left - 1
            band = jnp.logical_and(inseq, jnp.logical_and(d >= -left, d <= right))
            bm = jnp.where(band, 1.0, 0.0).astype(BF16)
            cols = slice(gi * PGD, (gi + 1) * PGD)
            sums[gi][b] = (jnp.dot(bm, hhi_ref[b * PB:b * PB + PBW, cols], preferred_element_type=F32)
                           + jnp.dot(bm, hlo_ref[b * PB:b * PB + PBW, cols], preferred_element_type=F32))
            cnt = jnp.minimum(tm_col + right + 1, seq) - jnp.maximum(tm_col - left, 0)
            cnts[gi][b] = cnt.astype(F32)
    outs = []
    for gi in range(len(POOL_WINDOWS)):
        tot = jnp.concatenate(sums[gi], axis=0)
        cnt = jnp.concatenate(cnts[gi], axis=0)
        diff = tot / cnt - h[:, gi * PGD:(gi + 1) * PGD]
        outs.append(jnp.dot(diff.astype(BF16), pw_ref[gi], preferred_element_type=F32))
    m = jnp.concatenate(outs, axis=1) * ps_ref[...]
    x1 = x + mod_ref[0, 2:3, :] * m
    x1_ref[...] = x1
    hf = _modulate(x1, gf_ref[...] * (1.0 + mod_ref[0, 4:5, :]), mod_ref[0, 3:4, :])
    h_ref[...] = _pack_rows(hf)
    _route(hf, wr_hl_ref, wr_hi_ref, tri_ref, cnt_ref, e_ref, rank_ref, wcol_ref, cnt_out_ref)


def _pool(group, y2g, x, wcol, mod_prev, mod, g_mix, pool_w, pool_scale, g_ffn, wr_hl, wr_hi, cnt_in, prev):
    const = lambda shape: pl.BlockSpec(shape, lambda i: (0,) * len(shape))
    tiles = (T_CTX if group == 0 else T_LAT) // PTILE
    first = 0 if group == 0 else T_CTX // PTILE
    out_specs, out_shape = _route_outs(PTILE, first)
    aliases = () if prev is None else tuple(prev)
    n_fixed = 14
    return pl.pallas_call(
        functools.partial(_pool_kernel, SEQ if group == 0 else DEC_SEQ, len(aliases)),
        grid=(tiles,),
        in_specs=[
            pl.BlockSpec((PTILE, HALF), lambda i: (i, 0)),
            pl.BlockSpec((PTILE, HALF), lambda i: (tiles + i, 0)),
            pl.BlockSpec((PTILE, D), lambda i: (first + i, 0)),
            pl.BlockSpec((PTILE, LANES), lambda i: (first + i, 0)),
            pl.BlockSpec((1, N_MOD, D), lambda i: (_cond_index(first + i, PTILE), 0, 0)),
            pl.BlockSpec((1, N_MOD, D), lambda i: (_cond_index(first + i, PTILE), 0, 0)),
            const((1, D)), const((len(POOL_WINDOWS), PGD, PGD)), const((1, D)), const((1, D)),
            const((D, 2 * LANES)), const((D, LANES)), const((256, 256)), const((NE, LANES)),
        ] + [pl.BlockSpec(memory_space=pl.ANY)] * len(aliases),
        out_specs=out_specs,
        out_shape=out_shape,
        input_output_aliases={n_fixed + k: k for k in range(len(aliases))},
        scratch_shapes=[pltpu.VMEM((NE, LANES), F32), pltpu.VMEM((PTILE + 2 * PB_HALO, D), BF16),
                        pltpu.VMEM((PTILE + 2 * PB_HALO, D), BF16)],
        compiler_params=_vmem_limit(56),
        name="pool_mixer",
    )(y2g, y2g, x, wcol, mod_prev, mod, g_mix, pool_w, pool_scale, g_ffn, wr_hl, wr_hi, _tri(), cnt_in, *aliases)


def _plan(e, rank, cnt):
    counts = cnt[:, 0].astype(I32)
    n_tiles = (counts + TM - 1) // TM
    tile_end = jnp.cumsum(n_tiles)
    base = (tile_end - n_tiles) * TM
    pos = jnp.sum(jnp.where(e[None] == jnp.arange(NE, dtype=I32)[:, None, None], base[:, None, None], 0),
                  axis=0) + rank
    used = tile_end[-1]
    tile_id = jnp.minimum(jnp.arange(N_ROW_TILES, dtype=I32), used - 1)
    tile_expert = jnp.sum((tile_id[:, None] >= tile_end[None, :]).astype(I32), axis=1)
    return pos, tile_expert, used.reshape(1)


def _sc_mesh():
    return plsc.VectorSubcoreMesh(core_axis_name="core", subcore_axis_name="subcore")


def _sc_worker():
    return lax.axis_index("core") * SC_SUBCORES + lax.axis_index("subcore")


def _sc_dispatch(h, pos):
    per = T // SC_WORKERS
    nwin = per // SC_WIN

    @functools.partial(
        pl.kernel, out_type=jax.ShapeDtypeStruct((NPAD, HALF), U32), mesh=_sc_mesh(),
        scratch_types=[pltpu.VMEM((2, SC_WIN, HALF), U32), pltpu.VMEM((2, SC_WIN), I32),
                       pltpu.VMEM((2, SC_WIN), I32), pltpu.SemaphoreType.DMA((2,)),
                       pltpu.SemaphoreType.DMA((2,))])
    def run(h_hbm, pos_hbm, xs_hbm, buf, idx0, idx1, sem_in, sem_out):
        first = _sc_worker() * per

        def loads(j, s):
            base = first + j * SC_WIN
            return (pltpu.make_async_copy(h_hbm.at[pl.ds(base, SC_WIN)], buf.at[s], sem_in.at[s]),
                    pltpu.make_async_copy(pos_hbm.at[pl.ds(base, SC_WIN)], idx0.at[s], sem_in.at[s]),
                    pltpu.make_async_copy(pos_hbm.at[pl.ds(T + base, SC_WIN)], idx1.at[s], sem_in.at[s]))

        def stores(s):
            return (pltpu.make_async_copy(buf.at[s], xs_hbm.at[idx0.at[s]], sem_out.at[s]),
                    pltpu.make_async_copy(buf.at[s], xs_hbm.at[idx1.at[s]], sem_out.at[s]))

        for c in loads(0, 0):
            c.start()

        @pl.loop(0, nwin // 2)
        def _(jj):
            for s in range(2):
                j = jj * 2 + s
                for c in loads(j, s):
                    c.wait()
                for c in stores(s):
                    c.start()

                @pl.when(j >= 1)
                def _():
                    for c in stores(1 - s):
                        c.wait()

                @pl.when(j + 1 < nwin)
                def _():
                    for c in loads(j + 1, 1 - s):
                        c.start()

        for c in stores((nwin - 1) % 2):
            c.wait()

    return run(h, pos.reshape(TOP_K * T))


def _sc_gather(ys, pos, group):
    t0, n_g = (0, T_CTX) if group == 0 else (T_CTX, T_LAT)
    idx_all = jnp.concatenate([pos[k, t0:t0 + n_g] for k in range(TOP_K)])
    n = TOP_K * n_g
    per = n // SC_WORKERS
    nwin = per // SC_WIN

    @functools.partial(
        pl.kernel, out_type=jax.ShapeDtypeStruct((n, HALF), U32), mesh=_sc_mesh(),
        scratch_types=[pltpu.VMEM((2, SC_WIN, HALF), U32), pltpu.VMEM((per,), I32),
                       pltpu.SemaphoreType.DMA((2,)), pltpu.SemaphoreType.DMA((2,))])
    def run(ys_hbm, pos_hbm, out_hbm, buf, idx, sem_in, sem_out):
        first = _sc_worker() * per
        pltpu.sync_copy(pos_hbm.at[pl.ds(first, per)], idx)

        def load(j, s):
            return pltpu.make_async_copy(ys_hbm.at[idx.at[pl.ds(j * SC_WIN, SC_WIN)]], buf.at[s], sem_in.at[s])

        def store(j, s):
            return pltpu.make_async_copy(buf.at[s], out_hbm.at[pl.ds(first + j * SC_WIN, SC_WIN)], sem_out.at[s])

        load(0, 0).start()

        @pl.loop(0, nwin // 2)
        def _(jj):
            for s in range(2):
                j = jj * 2 + s
                load(j, s).wait()
                store(j, s).start()

                @pl.when(j >= 1)
                def _():
                    store(j - 1, 1 - s).wait()

                @pl.when(j + 1 < nwin)
                def _():
                    load(j + 1, 1 - s).start()

        store(nwin - 1, (nwin - 1) % 2).wait()

    return run(ys, idx_all)


def _moe_kernel(te_ref, used_ref, x_ref, wg_ref, wu_ref, wd_ref, y_ref):
    del te_ref
    i = pl.program_id(0)

    @pl.when(i < used_ref[0])
    def _():
        hi, lo = _unpack_rows(x_ref[...])
        x = jnp.concatenate([hi, lo], axis=1).astype(BF16)
        a = jnp.dot(x, wg_ref[0, 0], preferred_element_type=F32)
        b = jnp.dot(x, wu_ref[0, 0], preferred_element_type=F32)
        hid = a * (1.0 / (1.0 + jnp.exp(-a))) * b
        y_ref[...] = _pack_rows(jnp.dot(hid.astype(BF16), wd_ref[0, 0], preferred_element_type=F32))


def _experts(xs, tile_expert, used, layer, w_gate, w_up, w_down):
    row = lambda i, te, used: (jnp.minimum(i, used[0] - 1), 0)
    weights = lambda i, te, used: (layer, te[i], 0, 0)
    grid_spec = pltpu.PrefetchScalarGridSpec(
        num_scalar_prefetch=2,
        grid=(N_ROW_TILES,),
        in_specs=[
            pl.BlockSpec((TM, HALF), row),
            pl.BlockSpec((1, 1, D, DE), weights),
            pl.BlockSpec((1, 1, D, DE), weights),
            pl.BlockSpec((1, 1, DE, D), weights),
        ],
        out_specs=pl.BlockSpec((TM, HALF), row),
    )
    return pl.pallas_call(
        _moe_kernel,
        grid_spec=grid_spec,
        out_shape=jax.ShapeDtypeStruct((NPAD, HALF), U32),
        compiler_params=_vmem_limit(40),
        name="moe_experts",
    )(tile_expert, used, xs, w_gate, w_up, w_down)


def _combined(y0_ref, y1_ref, x_ref, wcol_ref, mod_ref):
    w0 = wcol_ref[:, 0:1]
    w1 = wcol_ref[:, 1:2]
    hi0, lo0 = _unpack_rows(y0_ref[...])
    hi1, lo1 = _unpack_rows(y1_ref[...])
    y = jnp.concatenate([w0 * hi0 + w1 * hi1, w0 * lo0 + w1 * lo1], axis=1)
    return x_ref[...] + mod_ref[0, 5:6, :] * y


def _combine_kernel(y0_ref, y1_ref, x_ref, wcol_ref, mod_ref, o_ref):
    o_ref[...] = _combined(y0_ref, y1_ref, x_ref, wcol_ref, mod_ref)


def _combine(group, y2g, x, wcol, mod):
    n_g = T_CTX if group == 0 else T_LAT
    tiles = n_g // CT
    first = 0 if group == 0 else T_CTX // CT
    return pl.pallas_call(
        _combine_kernel,
        grid=(tiles,),
        in_specs=[
            pl.BlockSpec((CT, HALF), lambda i: (i, 0)),
            pl.BlockSpec((CT, HALF), lambda i: (tiles + i, 0)),
            pl.BlockSpec((CT, D), lambda i: (first + i, 0)),
            pl.BlockSpec((CT, LANES), lambda i: (first + i, 0)),
            pl.BlockSpec((1, N_MOD, D), lambda i: (_cond_index(first + i, CT), 0, 0)),
        ],
        out_specs=pl.BlockSpec((CT, D), lambda i: (i, 0)),
        out_shape=jax.ShapeDtypeStruct((n_g, D), F32),
        name="moe_combine",
    )(y2g, y2g, x, wcol, mod)


def _moe(h, e, rank, cnt, layer, w_gate, w_up, w_down):
    pos, tile_expert, used = _plan(e, rank, cnt)
    xs = _sc_dispatch(h, pos)
    ys = _experts(xs, tile_expert, used, layer, w_gate, w_up, w_down)
    return _sc_gather(ys, pos, 0), _sc_gather(ys, pos, 1)


def kernel(x_prompt, x_sample, c, cache_k, cache_v, c_ctx, norm_mix_g, norm_ffn_g, ada_w, ada_b, attn_w_qkv, attn_q_norm, attn_k_norm, attn_w_o, pool_w, pool_scale, moe_w_group, moe_w_expert, moe_w_gate, moe_w_up, moe_w_down):
    xp = x_prompt.reshape(T_CTX, D)
    xs = x_sample.reshape(T_LAT, D)
    cond = jnp.concatenate([c_ctx[None, :], c, jnp.zeros((N_COND - 1 - DEC_BATCH, D), F32)], axis=0)
    mod = _ada(cond, ada_w, ada_b)

    qt, k2, vt, new_k, new_v = _qkv(xp, xs, mod[0], norm_mix_g[0:1], attn_w_qkv[0].astype(BF16),
                                    attn_q_norm[0:1], attn_k_norm[0:1])
    o, w_gate, w_up, w_down = _attention(qt, k2, vt, cache_k[:, 0].reshape(DEC_BATCH, PAST, DKV),
                                         cache_v[:, 0].reshape(DEC_BATCH, PAST, DKV),
                                         moe_w_gate, moe_w_up, moe_w_down)
    wr_hl, wr_hi = _router_weights(moe_w_group[0], moe_w_expert[0])
    x1, h, e, rank, wcol, cnt = _post(o, xp, xs, mod[0], attn_w_o[0].astype(BF16), norm_ffn_g[0:1],
                                      wr_hl, wr_hi)
    y2 = _moe(h, e, rank, cnt, 0, w_gate, w_up, w_down)

    wr_hl, wr_hi = _router_weights(moe_w_group[1], moe_w_expert[1])
    pool_args = (x1, wcol, mod[0], mod[1], norm_mix_g[1:2], pool_w[0].astype(BF16), pool_scale[0:1],
                 norm_ffn_g[1:2], wr_hl, wr_hi)
    outs = _pool(0, y2[0], *pool_args, jnp.zeros((NE, LANES), F32), None)
    x3, h, e, rank, wcol, cnt = _pool(1, y2[1], *pool_args, outs[5], outs[:5])
    y2 = _moe(h, e, rank, cnt, 1, w_gate, w_up, w_down)
    y_prompt = _combine(0, y2[0], x3, wcol, mod[1])
    y_sample = _combine(1, y2[1], x3, wcol, mod[1])

    return (y_prompt.reshape(BATCH, SEQ, D), y_sample.reshape(DEC_BATCH, DEC_SEQ, D),
            new_k.reshape(BATCH, 1, SEQ, N_KV, HD), new_v.reshape(BATCH, 1, SEQ, N_KV, HD))
```

```python
import functools

import jax
import jax.numpy as jnp
import numpy as np
from jax import lax
from jax.experimental import pallas as pl
from jax.experimental.pallas import tpu as pltpu
from jax.experimental.pallas import tpu_sc as plsc

F32 = jnp.float32
BF16 = jnp.bfloat16
I32 = jnp.int32
U32 = jnp.uint32

D = 1024
BATCH, SEQ = 32, 256
DEC_BATCH, DEC_SEQ, PAST = 8, 1024, 512
T_CTX = BATCH * SEQ
T_LAT = DEC_BATCH * DEC_SEQ
T = T_CTX + T_LAT
GRID_W = 64
N_HEADS, N_KV, HD = 16, 4, 64
DQ = N_HEADS * HD
DKV = N_KV * HD
ROPE_THETA = 10000.0
POOL_WINDOWS = (2, 4, 8, 16)
PGD = D // len(POOL_WINDOWS)
N_GROUPS, E_PER_G, TOP_K = 4, 8, 2
NE = N_GROUPS * E_PER_G
DE = D // 4
N_MOD = 6
EPS = 1e-6
N_COND = 16

TILE = 512
NT = T // TILE
NT_CTX = T_CTX // TILE
QB = 256
PTILE = 1024
PB = 128
PBW = 256
PB_HALO = (PBW - PB) // 2
TM = 1024
N_ROW_TILES = (TOP_K * T) // TM + NE
NPAD = N_ROW_TILES * TM
CT = 1024
LANES = 128
NEG_INF = float("-inf")
LOG2_E = 1.4426950408889634
Q_SCALE = HD ** -0.5 * LOG2_E
HALF = D // 2
SC_CORES, SC_SUBCORES = 2, 16
SC_WORKERS = SC_CORES * SC_SUBCORES
SC_WIN = 64


def _vmem_limit(mib):
    return pltpu.CompilerParams(vmem_limit_bytes=mib * 1024 * 1024)


def _modulate(x, gain_scale, shift):
    ms = jnp.mean(x * x, axis=-1, keepdims=True)
    return (x * lax.rsqrt(ms + EPS)) * gain_scale + shift


def _head_norm(z, gain, ind, ind_t2):
    ss = jnp.dot((z * z).astype(BF16), ind, preferred_element_type=F32)
    inv = lax.rsqrt(ss * (1.0 / HD) + EPS)
    inv_hi = inv.astype(BF16)
    inv_lo = (inv - inv_hi.astype(F32)).astype(BF16)
    scale = jnp.dot(jnp.concatenate([inv_hi, inv_lo], axis=1), ind_t2, preferred_element_type=F32)
    return z * scale * gain


def _rope(z, cos_t, sin_t):
    lane = lax.broadcasted_iota(I32, (z.shape[0], LANES), 1)
    low = (lane % 32) < 16
    outs = []
    for c in range(z.shape[1] // LANES):
        zc = z[:, c * LANES:(c + 1) * LANES]
        up = pltpu.roll(zc, 16, axis=1)
        dn = pltpu.roll(zc, LANES - 16, axis=1)
        outs.append(zc * cos_t + jnp.where(low, dn, up) * sin_t)
    return jnp.concatenate(outs, axis=1)


def _rope_t(zt, cos_tt, sin_tt):
    sub = lax.broadcasted_iota(I32, (LANES, zt.shape[1]), 0)
    low = (sub % 32) < 16
    outs = []
    for c in range(zt.shape[0] // LANES):
        zc = zt[c * LANES:(c + 1) * LANES]
        up = pltpu.roll(zc, 16, axis=0)
        dn = pltpu.roll(zc, LANES - 16, axis=0)
        outs.append(zc * cos_tt + jnp.where(low, dn, up) * sin_tt)
    return jnp.concatenate(outs, axis=0)


def _pack_rows(z):
    bits = lax.bitcast_convert_type(z.astype(BF16).astype(F32), U32)
    return bits[:, :HALF] | (bits[:, HALF:] >> 16)


def _unpack_rows(p):
    hi = lax.bitcast_convert_type(p & jnp.uint32(0xFFFF0000), F32)
    lo = lax.bitcast_convert_type(p << 16, F32)
    return hi, lo


def _ada_kernel(c_ref, w_ref, b_ref, o_ref):
    c = c_ref[...]
    a = c * (1.0 / (1.0 + jnp.exp(-c)))
    o_ref[0] = jnp.dot(a.astype(BF16), w_ref[0].astype(BF16), preferred_element_type=F32) + b_ref[0]


def _ada(cond, ada_w, ada_b):
    nb = 1536
    depth = ada_w.shape[0]
    out = pl.pallas_call(
        _ada_kernel,
        grid=(depth, (N_MOD * D) // nb),
        in_specs=[
            pl.BlockSpec((N_COND, D), lambda l, j: (0, 0)),
            pl.BlockSpec((1, D, nb), lambda l, j: (l, 0, j)),
            pl.BlockSpec((1, 1, nb), lambda l, j: (l, 0, j)),
        ],
        out_specs=pl.BlockSpec((1, N_COND, nb), lambda l, j: (l, 0, j)),
        out_shape=jax.ShapeDtypeStruct((depth, N_COND, N_MOD * D), F32),
        compiler_params=_vmem_limit(40),
        name="ada",
    )(cond, ada_w, ada_b.reshape(depth, 1, N_MOD * D))
    return out.reshape(depth, N_COND, N_MOD, D)


def _cond_index(i, tile):
    n_ctx = T_CTX // tile
    per_batch = DEC_SEQ // tile
    return jnp.where(i < n_ctx, 0, 1 + (i - n_ctx) // per_batch)


def _qkv_kernel(xp_ref, xs_ref, mod_ref, g_ref, w_ref, qg_ref, kg_ref, indq_ref, indqt_ref,
                indk_ref, indkt_ref, cos_ref, sin_ref, cos_t_ref, sin_t_ref,
                qt_ref, k2_ref, vt_ref, nk_ref, nv_ref):
    i = pl.program_id(0)
    is_lat = i >= NT_CTX
    x = jnp.where(is_lat, xs_ref[...], xp_ref[...])
    h = _modulate(x, g_ref[...] * (1.0 + mod_ref[0, 1:2, :]), mod_ref[0, 0:1, :])
    qkv = jnp.dot(h.astype(BF16), w_ref[...], preferred_element_type=F32)
    q = _head_norm(qkv[:, :DQ], qg_ref[...], indq_ref[...], indqt_ref[...])
    k = _head_norm(qkv[:, DQ:DQ + DKV], kg_ref[...], indk_ref[...], indkt_ref[...])
    v = qkv[:, DQ + DKV:]

    vt_ref[...] = v.T.astype(BF16)

    qt = (q * Q_SCALE).T

    def store_qt(z):
        for b in range(TILE // QB):
            qt_ref[b] = z[:, b * QB:(b + 1) * QB].astype(BF16)

    @pl.when(is_lat)
    def _():
        store_qt(_rope_t(qt, cos_t_ref[...], sin_t_ref[...]))
        k2_ref[...] = _dup_heads(_rope(k, cos_ref[...], sin_ref[...])).astype(BF16)

    @pl.when(jnp.logical_not(is_lat))
    def _():
        store_qt(qt)
        k2_ref[...] = _dup_heads(k).astype(BF16)
        nk_ref[...] = k
        nv_ref[...] = v


def _rope_tables():
    rows = DEC_SEQ // GRID_W
    row = np.broadcast_to(np.arange(rows, dtype=np.float32)[:, None], (rows, GRID_W)).reshape(-1)
    col = np.broadcast_to(np.arange(GRID_W, dtype=np.float32)[None, :], (rows, GRID_W)).reshape(-1)
    axis_dim = HD // 2
    inv_freq = np.power(np.float32(ROPE_THETA),
                        -np.arange(0, axis_dim, 2, dtype=np.float32) / np.float32(axis_dim))
    ang = np.concatenate([row[:, None] * inv_freq, col[:, None] * inv_freq], axis=-1).astype(np.float32)
    cos, sin = np.cos(ang), np.sin(ang)
    quarter = HD // 4
    cos_h = np.concatenate([cos[:, :quarter], cos[:, :quarter], cos[:, quarter:], cos[:, quarter:]], axis=1)
    sin_h = np.concatenate([-sin[:, :quarter], sin[:, :quarter], -sin[:, quarter:], sin[:, quarter:]], axis=1)
    return (np.tile(cos_h, (1, LANES // HD)).astype(np.float32),
            np.tile(sin_h, (1, LANES // HD)).astype(np.float32))


def _head_indicators(width):
    col = np.arange(width)[:, None]
    head = np.arange(LANES)[None, :]
    ind = (col // HD == head).astype(np.float32)
    return jnp.asarray(ind, BF16), jnp.asarray(np.concatenate([ind.T, ind.T], axis=0), BF16)


def _dup_heads(z):
    lane = lax.broadcasted_iota(I32, (z.shape[0], LANES), 1)
    cols = []
    for p in range(N_KV // 2):
        blk = z[:, p * LANES:(p + 1) * LANES]
        swp = pltpu.roll(blk, HD, axis=1)
        cols += [jnp.where(lane < HD, blk, swp), jnp.where(lane < HD, swp, blk)]
    return jnp.concatenate(cols, axis=1)


def _qkv(xp, xs, mod, g, w_qkv, q_gain, k_gain):
    indq, indqt = _head_indicators(DQ)
    indk, indkt = _head_indicators(DKV)
    cos_t, sin_t = _rope_tables()
    per_batch = DEC_SEQ // TILE
    const = lambda shape: pl.BlockSpec(shape, lambda i: (0,) * len(shape))
    return pl.pallas_call(
        _qkv_kernel,
        grid=(NT,),
        in_specs=[
            pl.BlockSpec((TILE, D), lambda i: (jnp.minimum(i, NT_CTX - 1), 0)),
            pl.BlockSpec((TILE, D), lambda i: (jnp.maximum(i - NT_CTX, 0), 0)),
            pl.BlockSpec((1, N_MOD, D), lambda i: (_cond_index(i, TILE), 0, 0)),
            const((1, D)),
            const((D, DQ + 2 * DKV)),
            const((1, DQ)), const((1, DKV)),
            const((DQ, LANES)), const((2 * LANES, DQ)),
            const((DKV, LANES)), const((2 * LANES, DKV)),
            pl.BlockSpec((TILE, LANES), lambda i: (jnp.maximum(i - NT_CTX, 0) % per_batch, 0)),
            pl.BlockSpec((TILE, LANES), lambda i: (jnp.maximum(i - NT_CTX, 0) % per_batch, 0)),
            pl.BlockSpec((LANES, TILE), lambda i: (0, jnp.maximum(i - NT_CTX, 0) % per_batch)),
            pl.BlockSpec((LANES, TILE), lambda i: (0, jnp.maximum(i - NT_CTX, 0) % per_batch)),
        ],
        out_specs=[
            pl.BlockSpec((TILE // QB, DQ, QB), lambda i: (i, 0, 0)),
            pl.BlockSpec((TILE, 2 * DKV), lambda i: (i, 0)),
            pl.BlockSpec((DKV, TILE), lambda i: (0, i)),
            pl.BlockSpec((TILE, DKV), lambda i: (jnp.minimum(i, NT_CTX - 1), 0)),
            pl.BlockSpec((TILE, DKV), lambda i: (jnp.minimum(i, NT_CTX - 1), 0)),
        ],
        out_shape=[
            jax.ShapeDtypeStruct((T // QB, DQ, QB), BF16),
            jax.ShapeDtypeStruct((T, 2 * DKV), BF16),
            jax.ShapeDtypeStruct((DKV, T), BF16),
            jax.ShapeDtypeStruct((T_CTX, DKV), F32),
            jax.ShapeDtypeStruct((T_CTX, DKV), F32),
        ],
        compiler_params=_vmem_limit(56),
        name="qkv",
    )(xp, xs, mod, g, w_qkv, jnp.tile(q_gain, (1, N_HEADS)), jnp.tile(k_gain, (1, N_KV)),
      indq, indqt, indk, indkt, cos_t, sin_t, cos_t.T, sin_t.T)


def _unit_scores(u, pairs, k2_parts, qt_ref):
    cols = []
    for j in range(pairs):
        pair = u * pairs + j
        qt_pair = qt_ref[pair * LANES:(pair + 1) * LANES, :]
        sub = lax.broadcasted_iota(I32, qt_pair.shape, 0)
        zero = jnp.zeros_like(qt_pair)
        cols += [jnp.where(sub < HD, qt_pair, zero), jnp.where(sub >= HD, qt_pair, zero)]
    qtu = jnp.concatenate(cols, axis=1)
    return [jnp.dot(k2, qtu, preferred_element_type=F32) for k2 in k2_parts]


def _softmax_values(ss, vt_parts):
    m = ss[0].max(axis=0, keepdims=True)
    for s in ss[1:]:
        m = jnp.maximum(m, s.max(axis=0, keepdims=True))
    acc = None
    l = None
    for s, vt in zip(ss, vt_parts):
        e = jnp.exp2(s - m)
        ls = e.sum(axis=0, keepdims=True)
        l = ls if l is None else l + ls
        t = jnp.dot(vt, e.astype(BF16), preferred_element_type=F32)
        acc = t if acc is None else acc + t
    return acc * (1.0 / l)


def _attend(keys_of, values_of, qt_ref, o_ref, pairs, ahead):
    n_units = N_HEADS // (2 * pairs)
    group = lambda u: (u * 2 * pairs) // (N_HEADS // N_KV)
    scores = lambda u: _unit_scores(u, pairs, keys_of(group(u)), qt_ref)
    pending = [scores(u) for u in range(min(ahead, n_units))]
    outs = []
    for u in range(n_units):
        if u + ahead < n_units:
            pending.append(scores(u + ahead))
        outs.append(_softmax_values(pending.pop(0), values_of(group(u))))
    lq = o_ref.shape[0]
    heads = [o[:, h * lq:(h + 1) * lq] for o in outs for h in range(2 * pairs)]
    o_ref[...] = jnp.concatenate(heads, axis=0).T.astype(BF16)


def _attn_ctx_kernel(qt_ref, k2_ref, vt_ref, o_ref):
    _attend(lambda g: [k2_ref[:, g * LANES:(g + 1) * LANES]],
            lambda g: [vt_ref[g * HD:(g + 1) * HD, :]], qt_ref.at[0], o_ref, pairs=2, ahead=1)


def _attn_lat_kernel(qt_ref, k2c_ref, vtc_ref, k2n_ref, vtn_ref, wg_ref, wu_ref, wd_ref, o_in_ref,
                     o_ref, wgb_ref, wub_ref, wdb_ref):
    del o_in_ref
    wgb_ref[...] = wg_ref[...].astype(BF16)
    wub_ref[...] = wu_ref[...].astype(BF16)
    wdb_ref[...] = wd_ref[...].astype(BF16)
    _attend(lambda g: [k2c_ref[0, :, g * LANES:(g + 1) * LANES], k2n_ref[:, g * LANES:(g + 1) * LANES]],
            lambda g: [vtc_ref[0, g * HD:(g + 1) * HD, :], vtn_ref[g * HD:(g + 1) * HD, :]], qt_ref.at[0],
            o_ref, pairs=1, ahead=2)


def _cache_kernel(ck_ref, cv_ref, k2_ref, vt_ref):
    k2_ref[0] = _dup_heads(ck_ref[0]).astype(BF16)
    vt_ref[0] = cv_ref[0].T.astype(BF16)


def _attention(qt, k2, vt, cache_k, cache_v, w_gate, w_up, w_down):
    k2c, vtc = pl.pallas_call(
        _cache_kernel,
        grid=(DEC_BATCH,),
        in_specs=[
            pl.BlockSpec((1, PAST, DKV), lambda b: (b, 0, 0)),
            pl.BlockSpec((1, PAST, DKV), lambda b: (b, 0, 0)),
        ],
        out_specs=[
            pl.BlockSpec((1, PAST, 2 * DKV), lambda b: (b, 0, 0)),
            pl.BlockSpec((1, DKV, PAST), lambda b: (b, 0, 0)),
        ],
        out_shape=[
            jax.ShapeDtypeStruct((DEC_BATCH, PAST, 2 * DKV), BF16),
            jax.ShapeDtypeStruct((DEC_BATCH, DKV, PAST), BF16),
        ],
        name="cache_prep",
    )(cache_k, cache_v)

    o_ctx = pl.pallas_call(
        _attn_ctx_kernel,
        grid=(BATCH,),
        in_specs=[
            pl.BlockSpec((1, DQ, QB), lambda b: (b, 0, 0)),
            pl.BlockSpec((SEQ, 2 * DKV), lambda b: (b, 0)),
            pl.BlockSpec((DKV, SEQ), lambda b: (0, b)),
        ],
        out_specs=pl.BlockSpec((SEQ, DQ), lambda b: (b, 0)),
        out_shape=jax.ShapeDtypeStruct((T, DQ), BF16),
        name="attn_ctx",
    )(qt, k2, vt)

    qb = QB
    nqb = DEC_SEQ // qb
    lat0 = T_CTX // qb
    depth = w_gate.shape[0]
    eps = (depth * NE) // (DEC_BATCH * nqb)
    per_layer = NE // eps
    wblock = lambda b, j: ((b * nqb + j) // per_layer, (b * nqb + j) % per_layer, 0, 0)
    return pl.pallas_call(
        _attn_lat_kernel,
        grid=(DEC_BATCH, nqb),
        in_specs=[
            pl.BlockSpec((1, DQ, QB), lambda b, j: (lat0 + b * nqb + j, 0, 0)),
            pl.BlockSpec((1, PAST, 2 * DKV), lambda b, j: (b, 0, 0)),
            pl.BlockSpec((1, DKV, PAST), lambda b, j: (b, 0, 0)),
            pl.BlockSpec((DEC_SEQ, 2 * DKV), lambda b, j: (T_CTX // DEC_SEQ + b, 0)),
            pl.BlockSpec((DKV, DEC_SEQ), lambda b, j: (0, T_CTX // DEC_SEQ + b)),
            pl.BlockSpec((1, eps, D, DE), wblock),
            pl.BlockSpec((1, eps, D, DE), wblock),
            pl.BlockSpec((1, eps, DE, D), wblock),
            pl.BlockSpec(memory_space=pl.ANY),
        ],
        out_specs=[
            pl.BlockSpec((qb, DQ), lambda b, j: (lat0 + b * nqb + j, 0)),
            pl.BlockSpec((1, eps, D, DE), wblock),
            pl.BlockSpec((1, eps, D, DE), wblock),
            pl.BlockSpec((1, eps, DE, D), wblock),
        ],
        out_shape=[
            jax.ShapeDtypeStruct((T, DQ), BF16),
            jax.ShapeDtypeStruct(w_gate.shape, BF16),
            jax.ShapeDtypeStruct(w_up.shape, BF16),
            jax.ShapeDtypeStruct(w_down.shape, BF16),
        ],
        input_output_aliases={8: 0},
        compiler_params=_vmem_limit(56),
        name="attn_lat",
    )(qt, k2c, vtc, k2, vt, w_gate, w_up, w_down, o_ctx)


def _route(h, wr_hl_ref, wr_hi_ref, tri_ref, cnt_ref, e_ref, rank_ref, wcol_ref, cnt_out_ref):
    tl = h.shape[0]
    h_hi = h.astype(BF16)
    h_lo = (h - h_hi.astype(F32)).astype(BF16)
    lg2 = jnp.dot(h_hi, wr_hl_ref[...], preferred_element_type=F32)
    lg = lg2[:, :LANES] + lg2[:, LANES:] + jnp.dot(h_lo, wr_hi_ref[...], preferred_element_type=F32)
    lt = lg.T

    sub8 = lax.broadcasted_iota(I32, (8, tl), 0).astype(F32)
    gl = jnp.where(sub8 < N_GROUPS, lt[0:8], NEG_INF)
    gmax = gl.max(axis=0, keepdims=True)
    gidx = jnp.min(jnp.where(gl == gmax, sub8, 8.0), axis=0, keepdims=True)
    gsum = jnp.sum(jnp.exp(gl - gmax), axis=0, keepdims=True)
    sel = lt[8:16]
    for g in range(1, N_GROUPS):
        sel = jnp.where(gidx == g, lt[8 + 8 * g:16 + 8 * g], sel)
    t1 = sel.max(axis=0, keepdims=True)
    j1 = jnp.min(jnp.where(sel == t1, sub8, 8.0), axis=0, keepdims=True)
    sel2 = jnp.where(sub8 == j1, NEG_INF, sel)
    t2 = sel2.max(axis=0, keepdims=True)
    j2 = jnp.min(jnp.where(sel2 == t2, sub8, 8.0), axis=0, keepdims=True)
    ex = jnp.exp(t2 - t1)
    den = 1.0 + ex
    gw = 1.0 / gsum
    w0 = gw * (1.0 / den)
    w1 = gw * (ex / den)
    e0 = gidx * E_PER_G + j1
    e1 = gidx * E_PER_G + j2

    sub_e = lax.broadcasted_iota(I32, (NE, tl), 0).astype(F32)
    oh0 = (sub_e == e0).astype(F32)
    oh1 = (sub_e == e1).astype(F32)
    c = oh0 + oh1
    carry = cnt_ref[:, 0:1]
    pieces = []
    for ch in range(tl // 256):
        cc = c[:, ch * 256:(ch + 1) * 256]
        pieces.append(jnp.dot(cc.astype(BF16), tri_ref[...], preferred_element_type=F32) + carry)
        carry = carry + jnp.sum(cc, axis=1, keepdims=True)
    csum = jnp.concatenate(pieces, axis=1)
    cnt_ref[...] = jnp.broadcast_to(carry, (NE, LANES))
    cnt_out_ref[...] = jnp.broadcast_to(carry, (NE, LANES))
    r0 = jnp.sum(oh0 * csum, axis=0, keepdims=True) - 1.0
    r1 = jnp.sum(oh1 * csum, axis=0, keepdims=True) - 1.0
    e_ref[0:1, :] = e0.astype(I32)
    e_ref[1:2, :] = e1.astype(I32)
    rank_ref[0:1, :] = r0.astype(I32)
    rank_ref[1:2, :] = r1.astype(I32)
    sub_w = lax.broadcasted_iota(I32, (LANES, tl), 0)
    w8 = jnp.where(sub_w == 0, w0, jnp.where(sub_w == 1, w1, 0.0))
    wcol_ref[...] = w8.T


def _post_kernel(o_ref, xp_ref, xs_ref, mod_ref, wo_ref, g_ref, wr_hl_ref, wr_hi_ref, tri_ref,
                 x1_ref, h_ref, e_ref, rank_ref, wcol_ref, cnt_out_ref, cnt_ref):
    i = pl.program_id(0)

    @pl.when(i == 0)
    def _():
        cnt_ref[...] = jnp.zeros_like(cnt_ref)

    x = jnp.where(i >= NT_CTX, xs_ref[...], xp_ref[...])
    m = jnp.dot(o_ref[...], wo_ref[...], preferred_element_type=F32)
    x1 = x + mod_ref[0, 2:3, :] * m
    x1_ref[...] = x1
    h = _modulate(x1, g_ref[...] * (1.0 + mod_ref[0, 4:5, :]), mod_ref[0, 3:4, :])
    h_ref[...] = _pack_rows(h)
    _route(h, wr_hl_ref, wr_hi_ref, tri_ref, cnt_ref, e_ref, rank_ref, wcol_ref, cnt_out_ref)


def _router_weights(w_group, w_expert):
    w = jnp.concatenate([w_group, jnp.zeros((D, 8 - N_GROUPS), F32),
                         jnp.transpose(w_expert, (1, 0, 2)).reshape(D, NE),
                         jnp.zeros((D, LANES - 8 - NE), F32)], axis=1)
    w_hi = w.astype(BF16)
    w_lo = (w - w_hi.astype(F32)).astype(BF16)
    return jnp.concatenate([w_hi, w_lo], axis=1), w_hi


def _route_outs(tile, first=0):
    specs = [
        pl.BlockSpec((tile, D), lambda i: (first + i, 0)),
        pl.BlockSpec((tile, HALF), lambda i: (first + i, 0)),
        pl.BlockSpec((TOP_K, tile), lambda i: (0, first + i)),
        pl.BlockSpec((TOP_K, tile), lambda i: (0, first + i)),
        pl.BlockSpec((tile, LANES), lambda i: (first + i, 0)),
        pl.BlockSpec((NE, LANES), lambda i: (0, 0)),
    ]
    shapes = [
        jax.ShapeDtypeStruct((T, D), F32),
        jax.ShapeDtypeStruct((T, HALF), U32),
        jax.ShapeDtypeStruct((TOP_K, T), I32),
        jax.ShapeDtypeStruct((TOP_K, T), I32),
        jax.ShapeDtypeStruct((T, LANES), F32),
        jax.ShapeDtypeStruct((NE, LANES), F32),
    ]
    return specs, shapes


def _tri():
    a = np.arange(256)
    return jnp.asarray((a[:, None] <= a[None, :]).astype(np.float32), BF16)


def _post(o, xp, xs, mod, w_o, g_ffn, wr_hl, wr_hi):
    const = lambda shape: pl.BlockSpec(shape, lambda i: (0,) * len(shape))
    out_specs, out_shape = _route_outs(TILE)
    return pl.pallas_call(
        _post_kernel,
        grid=(NT,),
        in_specs=[
            pl.BlockSpec((TILE, DQ), lambda i: (i, 0)),
            pl.BlockSpec((TILE, D), lambda i: (jnp.minimum(i, NT_CTX - 1), 0)),
            pl.BlockSpec((TILE, D), lambda i: (jnp.maximum(i - NT_CTX, 0), 0)),
            pl.BlockSpec((1, N_MOD, D), lambda i: (_cond_index(i, TILE), 0, 0)),
            const((DQ, D)), const((1, D)), const((D, 2 * LANES)), const((D, LANES)), const((256, 256)),
        ],
        out_specs=out_specs,
        out_shape=out_shape,
        scratch_shapes=[pltpu.VMEM((NE, LANES), F32)],
        compiler_params=_vmem_limit(56),
        name="post_attn",
    )(o, xp, xs, mod, w_o, g_ffn, wr_hl, wr_hi, _tri())


def _pool_kernel(seq, n_alias, y0_ref, y1_ref, xin_ref, wcin_ref, modp_ref, mod_ref, g_ref, pw_ref, ps_ref,
                 gf_ref, wr_hl_ref, wr_hi_ref, tri_ref, cnt_in_ref, *refs):
    (x1_ref, h_ref, e_ref, rank_ref, wcol_ref, cnt_out_ref, cnt_ref, hhi_ref, hlo_ref) = refs[n_alias:]
    i = pl.program_id(0)

    @pl.when(i == 0)
    def _():
        cnt_ref[...] = cnt_in_ref[...]

    x = _combined(y0_ref, y1_ref, xin_ref, wcin_ref, modp_ref)
    h = _modulate(x, g_ref[...] * (1.0 + mod_ref[0, 1:2, :]), mod_ref[0, 0:1, :])
    @pl.when(i == 0)
    def _():
        zeros = jnp.zeros((PB_HALO, D), BF16)
        for ref in (hhi_ref, hlo_ref):
            ref[0:PB_HALO, :] = zeros
            ref[PB_HALO + PTILE:PB_HALO + PTILE + PB_HALO, :] = zeros

    h_hi = h.astype(BF16)
    hhi_ref[PB_HALO:PB_HALO + PTILE, :] = h_hi
    hlo_ref[PB_HALO:PB_HALO + PTILE, :] = (h - h_hi.astype(F32)).astype(BF16)

    r_io = lax.broadcasted_iota(I32, (PB, PBW), 0)
    d = lax.broadcasted_iota(I32, (PB, PBW), 1) - PB_HALO - r_io
    r_col = lax.broadcasted_iota(I32, (PB, 1), 0)
    sums = [[None] * (PTILE // PB) for _ in POOL_WINDOWS]
    cnts = [[None] * (PTILE // PB) for _ in POOL_WINDOWS]
    for b in range(PTILE // PB):
        tm = (b * PB + r_io) & (seq - 1)
        inseq = jnp.logical_and(tm + d >= 0, tm + d < seq)
        tm_col = (b * PB + r_col) & (seq - 1)
        for gi, win in enumerate(POOL_WINDOWS):
            left = win // 2
            right = win - left - 1
            band = jnp.logical_and(inseq, jnp.logical_and(d >= -left, d <= right))
            bm = jnp.where(band, 1.0, 0.0).astype(BF16)
            cols = slice(gi * PGD, (gi + 1) * PGD)
            sums[gi][b] = (jnp.dot(bm, hhi_ref[b * PB:b * PB + PBW, cols], preferred_element_type=F32)
                           + jnp.dot(bm, hlo_ref[b * PB:b * PB + PBW, cols], preferred_element_type=F32))
            cnt = jnp.minimum(tm_col + right + 1, seq) - jnp.maximum(tm_col - left, 0)
            cnts[gi][b] = cnt.astype(F32)
    outs = []
    for gi in range(len(POOL_WINDOWS)):
        tot = jnp.concatenate(sums[gi], axis=0)
        cnt = jnp.concatenate(cnts[gi], axis=0)
        diff = tot / cnt - h[:, gi * PGD:(gi + 1) * PGD]
        outs.append(jnp.dot(diff.astype(BF16), pw_ref[gi], preferred_element_type=F32))
    m = jnp.concatenate(outs, axis=1) * ps_ref[...]
    x1 = x + mod_ref[0, 2:3, :] * m
    x1_ref[...] = x1
    hf = _modulate(x1, gf_ref[...] * (1.0 + mod_ref[0, 4:5, :]), mod_ref[0, 3:4, :])
    h_ref[...] = _pack_rows(hf)
    _route(hf, wr_hl_ref, wr_hi_ref, tri_ref, cnt_ref, e_ref, rank_ref, wcol_ref, cnt_out_ref)


def _pool(group, y2g, x, wcol, mod_prev, mod, g_mix, pool_w, pool_scale, g_ffn, wr_hl, wr_hi, cnt_in, prev):
    const = lambda shape: pl.BlockSpec(shape, lambda i: (0,) * len(shape))
    tiles = (T_CTX if group == 0 else T_LAT) // PTILE
    first = 0 if group == 0 else T_CTX // PTILE
    out_specs, out_shape = _route_outs(PTILE, first)
    aliases = () if prev is None else tuple(prev)
    n_fixed = 14
    return pl.pallas_call(
        functools.partial(_pool_kernel, SEQ if group == 0 else DEC_SEQ, len(aliases)),
        grid=(tiles,),
        in_specs=[
            pl.BlockSpec((PTILE, HALF), lambda i: (i, 0)),
            pl.BlockSpec((PTILE, HALF), lambda i: (tiles + i, 0)),
            pl.BlockSpec((PTILE, D), lambda i: (first + i, 0)),
            pl.BlockSpec((PTILE, LANES), lambda i: (first + i, 0)),
            pl.BlockSpec((1, N_MOD, D), lambda i: (_cond_index(first + i, PTILE), 0, 0)),
            pl.BlockSpec((1, N_MOD, D), lambda i: (_cond_index(first + i, PTILE), 0, 0)),
            const((1, D)), const((len(POOL_WINDOWS), PGD, PGD)), const((1, D)), const((1, D)),
            const((D, 2 * LANES)), const((D, LANES)), const((256, 256)), const((NE, LANES)),
        ] + [pl.BlockSpec(memory_space=pl.ANY)] * len(aliases),
        out_specs=out_specs,
        out_shape=out_shape,
        input_output_aliases={n_fixed + k: k for k in range(len(aliases))},
        scratch_shapes=[pltpu.VMEM((NE, LANES), F32), pltpu.VMEM((PTILE + 2 * PB_HALO, D), BF16),
                        pltpu.VMEM((PTILE + 2 * PB_HALO, D), BF16)],
        compiler_params=_vmem_limit(56),
        name="pool_mixer",
    )(y2g, y2g, x, wcol, mod_prev, mod, g_mix, pool_w, pool_scale, g_ffn, wr_hl, wr_hi, _tri(), cnt_in, *aliases)


def _plan(e, rank, cnt):
    counts = cnt[:, 0].astype(I32)
    n_tiles = (counts + TM - 1) // TM
    tile_end = jnp.cumsum(n_tiles)
    base = (tile_end - n_tiles) * TM
    pos = jnp.sum(jnp.where(e[None] == jnp.arange(NE, dtype=I32)[:, None, None], base[:, None, None], 0),
                  axis=0) + rank
    used = tile_end[-1]
    tile_id = jnp.minimum(jnp.arange(N_ROW_TILES, dtype=I32), used - 1)
    tile_expert = jnp.sum((tile_id[:, None] >= tile_end[None, :]).astype(I32), axis=1)
    return pos, tile_expert, used.reshape(1)


def _sc_mesh():
    return plsc.VectorSubcoreMesh(core_axis_name="core", subcore_axis_name="subcore")


def _sc_worker():
    return lax.axis_index("core") * SC_SUBCORES + lax.axis_index("subcore")


def _sc_dispatch(h, pos):
    per = T // SC_WORKERS
    nwin = per // SC_WIN

    @functools.partial(
        pl.kernel, out_type=jax.ShapeDtypeStruct((NPAD, HALF), U32), mesh=_sc_mesh(),
        scratch_types=[pltpu.VMEM((2, SC_WIN, HALF), U32), pltpu.VMEM((2, SC_WIN), I32),
                       pltpu.VMEM((2, SC_WIN), I32), pltpu.SemaphoreType.DMA((2,)),
                       pltpu.SemaphoreType.DMA((2,))])
    def run(h_hbm, pos_hbm, xs_hbm, buf, idx0, idx1, sem_in, sem_out):
        first = _sc_worker() * per

        def loads(j, s):
            base = first + j * SC_WIN
            return (pltpu.make_async_copy(h_hbm.at[pl.ds(base, SC_WIN)], buf.at[s], sem_in.at[s]),
                    pltpu.make_async_copy(pos_hbm.at[pl.ds(base, SC_WIN)], idx0.at[s], sem_in.at[s]),
                    pltpu.make_async_copy(pos_hbm.at[pl.ds(T + base, SC_WIN)], idx1.at[s], sem_in.at[s]))

        def stores(s):
            return (pltpu.make_async_copy(buf.at[s], xs_hbm.at[idx0.at[s]], sem_out.at[s]),
                    pltpu.make_async_copy(buf.at[s], xs_hbm.at[idx1.at[s]], sem_out.at[s]))

        for c in loads(0, 0):
            c.start()

        @pl.loop(0, nwin // 2)
        def _(jj):
            for s in range(2):
                j = jj * 2 + s
                for c in loads(j, s):
                    c.wait()
                for c in stores(s):
                    c.start()

                @pl.when(j >= 1)
                def _():
                    for c in stores(1 - s):
                        c.wait()

                @pl.when(j + 1 < nwin)
                def _():
                    for c in loads(j + 1, 1 - s):
                        c.start()

        for c in stores((nwin - 1) % 2):
            c.wait()

    return run(h, pos.reshape(TOP_K * T))


def _sc_gather(ys, pos, group):
    t0, n_g = (0, T_CTX) if group == 0 else (T_CTX, T_LAT)
    idx_all = jnp.concatenate([pos[k, t0:t0 + n_g] for k in range(TOP_K)])
    n = TOP_K * n_g
    per = n // SC_WORKERS
    nwin = per // SC_WIN

    @functools.partial(
        pl.kernel, out_type=jax.ShapeDtypeStruct((n, HALF), U32), mesh=_sc_mesh(),
        scratch_types=[pltpu.VMEM((2, SC_WIN, HALF), U32), pltpu.VMEM((per,), I32),
                       pltpu.SemaphoreType.DMA((2,)), pltpu.SemaphoreType.DMA((2,))])
    def run(ys_hbm, pos_hbm, out_hbm, buf, idx, sem_in, sem_out):
        first = _sc_worker() * per
        pltpu.sync_copy(pos_hbm.at[pl.ds(first, per)], idx)

        def load(j, s):
            return pltpu.make_async_copy(ys_hbm.at[idx.at[pl.ds(j * SC_WIN, SC_WIN)]], buf.at[s], sem_in.at[s])

        def store(j, s):
            return pltpu.make_async_copy(buf.at[s], out_hbm.at[pl.ds(first + j * SC_WIN, SC_WIN)], sem_out.at[s])

        load(0, 0).start()

        @pl.loop(0, nwin // 2)
        def _(jj):
            for s in range(2):
                j = jj * 2 + s
                load(j, s).wait()
                store(j, s).start()

                @pl.when(j >= 1)
                def _():
                    store(j - 1, 1 - s).wait()

                @pl.when(j + 1 < nwin)
                def _():
                    load(j + 1, 1 - s).start()

        store(nwin - 1, (nwin - 1) % 2).wait()

    return run(ys, idx_all)


def _moe_kernel(te_ref, used_ref, x_ref, wg_ref, wu_ref, wd_ref, y_ref):
    del te_ref
    i = pl.program_id(0)

    @pl.when(i < used_ref[0])
    def _():
        hi, lo = _unpack_rows(x_ref[...])
        x = jnp.concatenate([hi, lo], axis=1).astype(BF16)
        a = jnp.dot(x, wg_ref[0, 0], preferred_element_type=F32)
        b = jnp.dot(x, wu_ref[0, 0], preferred_element_type=F32)
        hid = a * (1.0 / (1.0 + jnp.exp(-a))) * b
        y_ref[...] = _pack_rows(jnp.dot(hid.astype(BF16), wd_ref[0, 0], preferred_element_type=F32))


def _experts(xs, tile_expert, used, layer, w_gate, w_up, w_down):
    row = lambda i, te, used: (jnp.minimum(i, used[0] - 1), 0)
    weights = lambda i, te, used: (layer, te[i], 0, 0)
    grid_spec = pltpu.PrefetchScalarGridSpec(
        num_scalar_prefetch=2,
        grid=(N_ROW_TILES,),
        in_specs=[
            pl.BlockSpec((TM, HALF), row),
            pl.BlockSpec((1, 1, D, DE), weights),
            pl.BlockSpec((1, 1, D, DE), weights),
            pl.BlockSpec((1, 1, DE, D), weights),
        ],
        out_specs=pl.BlockSpec((TM, HALF), row),
    )
    return pl.pallas_call(
        _moe_kernel,
        grid_spec=grid_spec,
        out_shape=jax.ShapeDtypeStruct((NPAD, HALF), U32),
        compiler_params=_vmem_limit(40),
        name="moe_experts",
    )(tile_expert, used, xs, w_gate, w_up, w_down)


def _combined(y0_ref, y1_ref, x_ref, wcol_ref, mod_ref):
    w0 = wcol_ref[:, 0:1]
    w1 = wcol_ref[:, 1:2]
    hi0, lo0 = _unpack_rows(y0_ref[...])
    hi1, lo1 = _unpack_rows(y1_ref[...])
    y = jnp.concatenate([w0 * hi0 + w1 * hi1, w0 * lo0 + w1 * lo1], axis=1)
    return x_ref[...] + mod_ref[0, 5:6, :] * y


def _combine_kernel(y0_ref, y1_ref, x_ref, wcol_ref, mod_ref, o_ref):
    o_ref[...] = _combined(y0_ref, y1_ref, x_ref, wcol_ref, mod_ref)


def _combine(group, y2g, x, wcol, mod):
    n_g = T_CTX if group == 0 else T_LAT
    tiles = n_g // CT
    first = 0 if group == 0 else T_CTX // CT
    return pl.pallas_call(
        _combine_kernel,
        grid=(tiles,),
        in_specs=[
            pl.BlockSpec((CT, HALF), lambda i: (i, 0)),
            pl.BlockSpec((CT, HALF), lambda i: (tiles + i, 0)),
            pl.BlockSpec((CT, D), lambda i: (first + i, 0)),
            pl.BlockSpec((CT, LANES), lambda i: (first + i, 0)),
            pl.BlockSpec((1, N_MOD, D), lambda i: (_cond_index(first + i, CT), 0, 0)),
        ],
        out_specs=pl.BlockSpec((CT, D), lambda i: (i, 0)),
        out_shape=jax.ShapeDtypeStruct((n_g, D), F32),
        name="moe_combine",
    )(y2g, y2g, x, wcol, mod)


def _moe(h, e, rank, cnt, layer, w_gate, w_up, w_down):
    pos, tile_expert, used = _plan(e, rank, cnt)
    xs = _sc_dispatch(h, pos)
    ys = _experts(xs, tile_expert, used, layer, w_gate, w_up, w_down)
    return _sc_gather(ys, pos, 0), _sc_gather(ys, pos, 1)


def kernel(x_prompt, x_sample, c, cache_k, cache_v, c_ctx, norm_mix_g, norm_ffn_g, ada_w, ada_b, attn_w_qkv, attn_q_norm, attn_k_norm, attn_w_o, pool_w, pool_scale, moe_w_group, moe_w_expert, moe_w_gate, moe_w_up, moe_w_down):
    xp = x_prompt.reshape(T_CTX, D)
    xs = x_sample.reshape(T_LAT, D)
    cond = jnp.concatenate([c_ctx[None, :], c, jnp.zeros((N_COND - 1 - DEC_BATCH, D), F32)], axis=0)
    mod = _ada(cond, ada_w, ada_b)

    qt, k2, vt, new_k, new_v = _qkv(xp, xs, mod[0], norm_mix_g[0:1], attn_w_qkv[0].astype(BF16),
                                    attn_q_norm[0:1], attn_k_norm[0:1])
    o, w_gate, w_up, w_down = _attention(qt, k2, vt, cache_k[:, 0].reshape(DEC_BATCH, PAST, DKV),
                                         cache_v[:, 0].reshape(DEC_BATCH, PAST, DKV),
                                         moe_w_gate, moe_w_up, moe_w_down)
    wr_hl, wr_hi = _router_weights(moe_w_group[0], moe_w_expert[0])
    x1, h, e, rank, wcol, cnt = _post(o, xp, xs, mod[0], attn_w_o[0].astype(BF16), norm_ffn_g[0:1],
                                      wr_hl, wr_hi)
    y2 = _moe(h, e, rank, cnt, 0, w_gate, w_up, w_down)

    wr_hl, wr_hi = _router_weights(moe_w_group[1], moe_w_expert[1])
    pool_args = (x1, wcol, mod[0], mod[1], norm_mix_g[1:2], pool_w[0].astype(BF16), pool_scale[0:1],
                 norm_ffn_g[1:2], wr_hl, wr_hi)
    outs = _pool(0, y2[0], *pool_args, jnp.zeros((NE, LANES), F32), None)
    x3, h, e, rank, wcol, cnt = _pool(1, y2[1], *pool_args, outs[5], outs[:5])
    y2 = _moe(h, e, rank, cnt, 1, w_gate, w_up, w_down)
    y_prompt = _combine(0, y2[0], x3, wcol, mod[1])
    y_sample = _combine(1, y2[1], x3, wcol, mod[1])

    return (y_prompt.reshape(BATCH, SEQ, D), y_sample.reshape(DEC_BATCH, DEC_SEQ, D),
            new_k.reshape(BATCH, 1, SEQ, N_KV, HD), new_v.reshape(BATCH, 1, SEQ, N_KV, HD))
```

```python
import functools

import jax
import jax.numpy as jnp
import numpy as np
from jax import lax
from jax.experimental import pallas as pl
from jax.experimental.pallas import tpu as pltpu
from jax.experimental.pallas import tpu_sc as plsc

F32 = jnp.float32
BF16 = jnp.bfloat16
I32 = jnp.int32
U32 = jnp.uint32

D = 1024
BATCH, SEQ = 32, 256
DEC_BATCH, DEC_SEQ, PAST = 8, 1024, 512
T_CTX = BATCH * SEQ
T_LAT = DEC_BATCH * DEC_SEQ
T = T_CTX + T_LAT
GRID_W = 64
N_HEADS, N_KV, HD = 16, 4, 64
DQ = N_HEADS * HD
DKV = N_KV * HD
ROPE_THETA = 10000.0
POOL_WINDOWS = (2, 4, 8, 16)
PGD = D // len(POOL_WINDOWS)
N_GROUPS, E_PER_G, TOP_K = 4, 8, 2
NE = N_GROUPS * E_PER_G
DE = D // 4
N_MOD = 6
EPS = 1e-6
N_COND = 16

TILE = 512
NT = T // TILE
NT_CTX = T_CTX // TILE
QB = 256
PTILE = 1024
PB = 128
PBW = 256
PB_HALO = (PBW - PB) // 2
TM = 1024
N_ROW_TILES = (TOP_K * T) // TM + NE
NPAD = N_ROW_TILES * TM
CT = 1024
LANES = 128
NEG_INF = float("-inf")
LOG2_E = 1.4426950408889634
Q_SCALE = HD ** -0.5 * LOG2_E
HALF = D // 2
SC_CORES, SC_SUBCORES = 2, 16
SC_WORKERS = SC_CORES * SC_SUBCORES
SC_WIN = 64


def _vmem_limit(mib):
    return pltpu.CompilerParams(vmem_limit_bytes=mib * 1024 * 1024)


def _modulate(x, gain_scale, shift):
    ms = jnp.mean(x * x, axis=-1, keepdims=True)
    return (x * lax.rsqrt(ms + EPS)) * gain_scale + shift


def _head_norm(z, gain, ind, ind_t2):
    ss = jnp.dot((z * z).astype(BF16), ind, preferred_element_type=F32)
    inv = lax.rsqrt(ss * (1.0 / HD) + EPS)
    inv_hi = inv.astype(BF16)
    inv_lo = (inv - inv_hi.astype(F32)).astype(BF16)
    scale = jnp.dot(jnp.concatenate([inv_hi, inv_lo], axis=1), ind_t2, preferred_element_type=F32)
    return z * scale * gain


def _rope(z, cos_t, sin_t):
    lane = lax.broadcasted_iota(I32, (z.shape[0], LANES), 1)
    low = (lane % 32) < 16
    outs = []
    for c in range(z.shape[1] // LANES):
        zc = z[:, c * LANES:(c + 1) * LANES]
        up = pltpu.roll(zc, 16, axis=1)
        dn = pltpu.roll(zc, LANES - 16, axis=1)
        outs.append(zc * cos_t + jnp.where(low, dn, up) * sin_t)
    return jnp.concatenate(outs, axis=1)


def _rope_t(zt, cos_tt, sin_tt):
    sub = lax.broadcasted_iota(I32, (LANES, zt.shape[1]), 0)
    low = (sub % 32) < 16
    outs = []
    for c in range(zt.shape[0] // LANES):
        zc = zt[c * LANES:(c + 1) * LANES]
        up = pltpu.roll(zc, 16, axis=0)
        dn = pltpu.roll(zc, LANES - 16, axis=0)
        outs.append(zc * cos_tt + jnp.where(low, dn, up) * sin_tt)
    return jnp.concatenate(outs, axis=0)


def _pack_rows(z):
    bits = lax.bitcast_convert_type(z.astype(BF16).astype(F32), U32)
    return bits[:, :HALF] | (bits[:, HALF:] >> 16)


def _unpack_rows(p):
    hi = lax.bitcast_convert_type(p & jnp.uint32(0xFFFF0000), F32)
    lo = lax.bitcast_convert_type(p << 16, F32)
    return hi, lo


def _ada_kernel(c_ref, w_ref, b_ref, o_ref):
    c = c_ref[...]
    a = c * (1.0 / (1.0 + jnp.exp(-c)))
    o_ref[0] = jnp.dot(a.astype(BF16), w_ref[0].astype(BF16), preferred_element_type=F32) + b_ref[0]


def _ada(cond, ada_w, ada_b):
    nb = 1536
    depth = ada_w.shape[0]
    out = pl.pallas_call(
        _ada_kernel,
        grid=(depth, (N_MOD * D) // nb),
        in_specs=[
            pl.BlockSpec((N_COND, D), lambda l, j: (0, 0)),
            pl.BlockSpec((1, D, nb), lambda l, j: (l, 0, j)),
            pl.BlockSpec((1, 1, nb), lambda l, j: (l, 0, j)),
        ],
        out_specs=pl.BlockSpec((1, N_COND, nb), lambda l, j: (l, 0, j)),
        out_shape=jax.ShapeDtypeStruct((depth, N_COND, N_MOD * D), F32),
        compiler_params=_vmem_limit(40),
        name="ada",
    )(cond, ada_w, ada_b.reshape(depth, 1, N_MOD * D))
    return out.reshape(depth, N_COND, N_MOD, D)


def _cond_index(i, tile):
    n_ctx = T_CTX // tile
    per_batch = DEC_SEQ // tile
    return jnp.where(i < n_ctx, 0, 1 + (i - n_ctx) // per_batch)


def _qkv_kernel(xp_ref, xs_ref, mod_ref, g_ref, w_ref, qg_ref, kg_ref, indq_ref, indqt_ref,
                indk_ref, indkt_ref, cos_ref, sin_ref, cos_t_ref, sin_t_ref,
                qt_ref, k2_ref, vt_ref, nk_ref, nv_ref):
    i = pl.program_id(0)
    is_lat = i >= NT_CTX
    gain_scale = g_ref[...] * (1.0 + mod_ref[0, 1:2, :])
    subs = [slice(b * QB, (b + 1) * QB) for b in range(TILE // QB)]
    hs = [_modulate(jnp.where(is_lat, xs_ref[r, :], xp_ref[r, :]), gain_scale, mod_ref[0, 0:1, :]).astype(BF16)
          for r in subs]
    qkvs = [jnp.dot(h, w_ref[...], preferred_element_type=F32) for h in hs]
    qs = [_head_norm(z[:, :DQ], qg_ref[...], indq_ref[...], indqt_ref[...]) for z in qkvs]
    ks = [_head_norm(z[:, DQ:DQ + DKV], kg_ref[...], indk_ref[...], indkt_ref[...]) for z in qkvs]
    vs = [z[:, DQ + DKV:] for z in qkvs]
    qts = [(q * Q_SCALE).T for q in qs]
    for r, v in zip(subs, vs):
        vt_ref[:, r] = v.T.astype(BF16)

    @pl.when(is_lat)
    def _():
        for b, r in enumerate(subs):
            qt_ref[b] = _rope_t(qts[b], cos_t_ref[:, r], sin_t_ref[:, r]).astype(BF16)
            k2_ref[r, :] = _dup_heads(_rope(ks[b], cos_ref[r, :], sin_ref[r, :])).astype(BF16)

    @pl.when(jnp.logical_not(is_lat))
    def _():
        for b, r in enumerate(subs):
            qt_ref[b] = qts[b].astype(BF16)
            k2_ref[r, :] = _dup_heads(ks[b]).astype(BF16)
            nk_ref[r, :] = ks[b]
            nv_ref[r, :] = vs[b]


def _rope_tables():
    rows = DEC_SEQ // GRID_W
    row = np.broadcast_to(np.arange(rows, dtype=np.float32)[:, None], (rows, GRID_W)).reshape(-1)
    col = np.broadcast_to(np.arange(GRID_W, dtype=np.float32)[None, :], (rows, GRID_W)).reshape(-1)
    axis_dim = HD // 2
    inv_freq = np.power(np.float32(ROPE_THETA),
                        -np.arange(0, axis_dim, 2, dtype=np.float32) / np.float32(axis_dim))
    ang = np.concatenate([row[:, None] * inv_freq, col[:, None] * inv_freq], axis=-1).astype(np.float32)
    cos, sin = np.cos(ang), np.sin(ang)
    quarter = HD // 4
    cos_h = np.concatenate([cos[:, :quarter], cos[:, :quarter], cos[:, quarter:], cos[:, quarter:]], axis=1)
    sin_h = np.concatenate([-sin[:, :quarter], sin[:, :quarter], -sin[:, quarter:], sin[:, quarter:]], axis=1)
    return (np.tile(cos_h, (1, LANES // HD)).astype(np.float32),
            np.tile(sin_h, (1, LANES // HD)).astype(np.float32))


def _head_indicators(width):
    col = np.arange(width)[:, None]
    head = np.arange(LANES)[None, :]
    ind = (col // HD == head).astype(np.float32)
    return jnp.asarray(ind, BF16), jnp.asarray(np.concatenate([ind.T, ind.T], axis=0), BF16)


def _dup_heads(z):
    lane = lax.broadcasted_iota(I32, (z.shape[0], LANES), 1)
    cols = []
    for p in range(N_KV // 2):
        blk = z[:, p * LANES:(p + 1) * LANES]
        swp = pltpu.roll(blk, HD, axis=1)
        cols += [jnp.where(lane < HD, blk, swp), jnp.where(lane < HD, swp, blk)]
    return jnp.concatenate(cols, axis=1)


def _qkv(xp, xs, mod, g, w_qkv, q_gain, k_gain):
    indq, indqt = _head_indicators(DQ)
    indk, indkt = _head_indicators(DKV)
    cos_t, sin_t = _rope_tables()
    per_batch = DEC_SEQ // TILE
    const = lambda shape: pl.BlockSpec(shape, lambda i: (0,) * len(shape))
    return pl.pallas_call(
        _qkv_kernel,
        grid=(NT,),
        in_specs=[
            pl.BlockSpec((TILE, D), lambda i: (jnp.minimum(i, NT_CTX - 1), 0)),
            pl.BlockSpec((TILE, D), lambda i: (jnp.maximum(i - NT_CTX, 0), 0)),
            pl.BlockSpec((1, N_MOD, D), lambda i: (_cond_index(i, TILE), 0, 0)),
            const((1, D)),
            const((D, DQ + 2 * DKV)),
            const((1, DQ)), const((1, DKV)),
            const((DQ, LANES)), const((2 * LANES, DQ)),
            const((DKV, LANES)), const((2 * LANES, DKV)),
            pl.BlockSpec((TILE, LANES), lambda i: (jnp.maximum(i - NT_CTX, 0) % per_batch, 0)),
            pl.BlockSpec((TILE, LANES), lambda i: (jnp.maximum(i - NT_CTX, 0) % per_batch, 0)),
            pl.BlockSpec((LANES, TILE), lambda i: (0, jnp.maximum(i - NT_CTX, 0) % per_batch)),
            pl.BlockSpec((LANES, TILE), lambda i: (0, jnp.maximum(i - NT_CTX, 0) % per_batch)),
        ],
        out_specs=[
            pl.BlockSpec((TILE // QB, DQ, QB), lambda i: (i, 0, 0)),
            pl.BlockSpec((TILE, 2 * DKV), lambda i: (i, 0)),
            pl.BlockSpec((DKV, TILE), lambda i: (0, i)),
            pl.BlockSpec((TILE, DKV), lambda i: (jnp.minimum(i, NT_CTX - 1), 0)),
            pl.BlockSpec((TILE, DKV), lambda i: (jnp.minimum(i, NT_CTX - 1), 0)),
        ],
        out_shape=[
            jax.ShapeDtypeStruct((T // QB, DQ, QB), BF16),
            jax.ShapeDtypeStruct((T, 2 * DKV), BF16),
            jax.ShapeDtypeStruct((DKV, T), BF16),
            jax.ShapeDtypeStruct((T_CTX, DKV), F32),
            jax.ShapeDtypeStruct((T_CTX, DKV), F32),
        ],
        compiler_params=_vmem_limit(56),
        name="qkv",
    )(xp, xs, mod, g, w_qkv, jnp.tile(q_gain, (1, N_HEADS)), jnp.tile(k_gain, (1, N_KV)),
      indq, indqt, indk, indkt, cos_t, sin_t, cos_t.T, sin_t.T)


def _unit_scores(u, pairs, k2_parts, qt_ref):
    cols = []
    for j in range(pairs):
        pair = u * pairs + j
        qt_pair = qt_ref[pair * LANES:(pair + 1) * LANES, :]
        sub = lax.broadcasted_iota(I32, qt_pair.shape, 0)
        zero = jnp.zeros_like(qt_pair)
        cols += [jnp.where(sub < HD, qt_pair, zero), jnp.where(sub >= HD, qt_pair, zero)]
    qtu = jnp.concatenate(cols, axis=1)
    return [jnp.dot(k2, qtu, preferred_element_type=F32) for k2 in k2_parts]


def _softmax_values(ss, vt_parts):
    m = ss[0].max(axis=0, keepdims=True)
    for s in ss[1:]:
        m = jnp.maximum(m, s.max(axis=0, keepdims=True))
    acc = None
    l = None
    for s, vt in zip(ss, vt_parts):
        e = jnp.exp2(s - m)
        ls = e.sum(axis=0, keepdims=True)
        l = ls if l is None else l + ls
        t = jnp.dot(vt, e.astype(BF16), preferred_element_type=F32)
        acc = t if acc is None else acc + t
    return acc * (1.0 / l)


def _attend(keys_of, values_of, qt_ref, o_ref, pairs, ahead):
    n_units = N_HEADS // (2 * pairs)
    group = lambda u: (u * 2 * pairs) // (N_HEADS // N_KV)
    scores = lambda u: _unit_scores(u, pairs, keys_of(group(u)), qt_ref)
    pending = [scores(u) for u in range(min(ahead, n_units))]
    outs = []
    for u in range(n_units):
        if u + ahead < n_units:
            pending.append(scores(u + ahead))
        outs.append(_softmax_values(pending.pop(0), values_of(group(u))))
    lq = o_ref.shape[0]
    heads = [o[:, h * lq:(h + 1) * lq] for o in outs for h in range(2 * pairs)]
    o_ref[...] = jnp.concatenate(heads, axis=0).T.astype(BF16)


def _attn_ctx_kernel(qt_ref, k2_ref, vt_ref, o_ref):
    _attend(lambda g: [k2_ref[:, g * LANES:(g + 1) * LANES]],
            lambda g: [vt_ref[g * HD:(g + 1) * HD, :]], qt_ref.at[0], o_ref, pairs=2, ahead=1)


def _attn_lat_kernel(qt_ref, k2c_ref, vtc_ref, k2n_ref, vtn_ref, wg_ref, wu_ref, wd_ref, o_in_ref,
                     o_ref, wgb_ref, wub_ref, wdb_ref):
    del o_in_ref
    wgb_ref[...] = wg_ref[...].astype(BF16)
    wub_ref[...] = wu_ref[...].astype(BF16)
    wdb_ref[...] = wd_ref[...].astype(BF16)
    _attend(lambda g: [k2c_ref[0, :, g * LANES:(g + 1) * LANES], k2n_ref[:, g * LANES:(g + 1) * LANES]],
            lambda g: [vtc_ref[0, g * HD:(g + 1) * HD, :], vtn_ref[g * HD:(g + 1) * HD, :]], qt_ref.at[0],
            o_ref, pairs=1, ahead=2)


def _cache_kernel(ck_ref, cv_ref, k2_ref, vt_ref):
    k2_ref[0] = _dup_heads(ck_ref[0]).astype(BF16)
    vt_ref[0] = cv_ref[0].T.astype(BF16)


def _attention(qt, k2, vt, cache_k, cache_v, w_gate, w_up, w_down):
    k2c, vtc = pl.pallas_call(
        _cache_kernel,
        grid=(DEC_BATCH,),
        in_specs=[
            pl.BlockSpec((1, PAST, DKV), lambda b: (b, 0, 0)),
            pl.BlockSpec((1, PAST, DKV), lambda b: (b, 0, 0)),
        ],
        out_specs=[
            pl.BlockSpec((1, PAST, 2 * DKV), lambda b: (b, 0, 0)),
            pl.BlockSpec((1, DKV, PAST), lambda b: (b, 0, 0)),
        ],
        out_shape=[
            jax.ShapeDtypeStruct((DEC_BATCH, PAST, 2 * DKV), BF16),
            jax.ShapeDtypeStruct((DEC_BATCH, DKV, PAST), BF16),
        ],
        name="cache_prep",
    )(cache_k, cache_v)

    o_ctx = pl.pallas_call(
        _attn_ctx_kernel,
        grid=(BATCH,),
        in_specs=[
            pl.BlockSpec((1, DQ, QB), lambda b: (b, 0, 0)),
            pl.BlockSpec((SEQ, 2 * DKV), lambda b: (b, 0)),
            pl.BlockSpec((DKV, SEQ), lambda b: (0, b)),
        ],
        out_specs=pl.BlockSpec((SEQ, DQ), lambda b: (b, 0)),
        out_shape=jax.ShapeDtypeStruct((T, DQ), BF16),
        name="attn_ctx",
    )(qt, k2, vt)

    qb = QB
    nqb = DEC_SEQ // qb
    lat0 = T_CTX // qb
    depth = w_gate.shape[0]
    eps = (depth * NE) // (DEC_BATCH * nqb)
    per_layer = NE // eps
    wblock = lambda b, j: ((b * nqb + j) // per_layer, (b * nqb + j) % per_layer, 0, 0)
    return pl.pallas_call(
        _attn_lat_kernel,
        grid=(DEC_BATCH, nqb),
        in_specs=[
            pl.BlockSpec((1, DQ, QB), lambda b, j: (lat0 + b * nqb + j, 0, 0)),
            pl.BlockSpec((1, PAST, 2 * DKV), lambda b, j: (b, 0, 0)),
            pl.BlockSpec((1, DKV, PAST), lambda b, j: (b, 0, 0)),
            pl.BlockSpec((DEC_SEQ, 2 * DKV), lambda b, j: (T_CTX // DEC_SEQ + b, 0)),
            pl.BlockSpec((DKV, DEC_SEQ), lambda b, j: (0, T_CTX // DEC_SEQ + b)),
            pl.BlockSpec((1, eps, D, DE), wblock),
            pl.BlockSpec((1, eps, D, DE), wblock),
            pl.BlockSpec((1, eps, DE, D), wblock),
            pl.BlockSpec(memory_space=pl.ANY),
        ],
        out_specs=[
            pl.BlockSpec((qb, DQ), lambda b, j: (lat0 + b * nqb + j, 0)),
            pl.BlockSpec((1, eps, D, DE), wblock),
            pl.BlockSpec((1, eps, D, DE), wblock),
            pl.BlockSpec((1, eps, DE, D), wblock),
        ],
        out_shape=[
            jax.ShapeDtypeStruct((T, DQ), BF16),
            jax.ShapeDtypeStruct(w_gate.shape, BF16),
            jax.ShapeDtypeStruct(w_up.shape, BF16),
            jax.ShapeDtypeStruct(w_down.shape, BF16),
        ],
        input_output_aliases={8: 0},
        compiler_params=_vmem_limit(56),
        name="attn_lat",
    )(qt, k2c, vtc, k2, vt, w_gate, w_up, w_down, o_ctx)


def _route(h, wr_hl_ref, wr_hi_ref, tri_ref, cnt_ref, e_ref, rank_ref, wcol_ref, cnt_out_ref):
    tl = h.shape[0]
    h_hi = h.astype(BF16)
    h_lo = (h - h_hi.astype(F32)).astype(BF16)
    lg2 = jnp.dot(h_hi, wr_hl_ref[...], preferred_element_type=F32)
    lg = lg2[:, :LANES] + lg2[:, LANES:] + jnp.dot(h_lo, wr_hi_ref[...], preferred_element_type=F32)
    lt = lg.T

    sub8 = lax.broadcasted_iota(I32, (8, tl), 0).astype(F32)
    gl = jnp.where(sub8 < N_GROUPS, lt[0:8], NEG_INF)
    gmax = gl.max(axis=0, keepdims=True)
    gidx = jnp.min(jnp.where(gl == gmax, sub8, 8.0), axis=0, keepdims=True)
    gsum = jnp.sum(jnp.exp(gl - gmax), axis=0, keepdims=True)
    sel = lt[8:16]
    for g in range(1, N_GROUPS):
        sel = jnp.where(gidx == g, lt[8 + 8 * g:16 + 8 * g], sel)
    t1 = sel.max(axis=0, keepdims=True)
    j1 = jnp.min(jnp.where(sel == t1, sub8, 8.0), axis=0, keepdims=True)
    sel2 = jnp.where(sub8 == j1, NEG_INF, sel)
    t2 = sel2.max(axis=0, keepdims=True)
    j2 = jnp.min(jnp.where(sel2 == t2, sub8, 8.0), axis=0, keepdims=True)
    ex = jnp.exp(t2 - t1)
    den = 1.0 + ex
    gw = 1.0 / gsum
    w0 = gw * (1.0 / den)
    w1 = gw * (ex / den)
    e0 = gidx * E_PER_G + j1
    e1 = gidx * E_PER_G + j2

    sub_e = lax.broadcasted_iota(I32, (NE, tl), 0).astype(F32)
    oh0 = (sub_e == e0).astype(F32)
    oh1 = (sub_e == e1).astype(F32)
    c = oh0 + oh1
    carry = cnt_ref[:, 0:1]
    pieces = []
    for ch in range(tl // 256):
        cc = c[:, ch * 256:(ch + 1) * 256]
        pieces.append(jnp.dot(cc.astype(BF16), tri_ref[...], preferred_element_type=F32) + carry)
        carry = carry + jnp.sum(cc, axis=1, keepdims=True)
    csum = jnp.concatenate(pieces, axis=1)
    cnt_ref[...] = jnp.broadcast_to(carry, (NE, LANES))
    cnt_out_ref[...] = jnp.broadcast_to(carry, (NE, LANES))
    r0 = jnp.sum(oh0 * csum, axis=0, keepdims=True) - 1.0
    r1 = jnp.sum(oh1 * csum, axis=0, keepdims=True) - 1.0
    e_ref[0:1, :] = e0.astype(I32)
    e_ref[1:2, :] = e1.astype(I32)
    rank_ref[0:1, :] = r0.astype(I32)
    rank_ref[1:2, :] = r1.astype(I32)
    sub_w = lax.broadcasted_iota(I32, (LANES, tl), 0)
    w8 = jnp.where(sub_w == 0, w0, jnp.where(sub_w == 1, w1, 0.0))
    wcol_ref[...] = w8.T


def _post_kernel(o_ref, xp_ref, xs_ref, mod_ref, wo_ref, g_ref, wr_hl_ref, wr_hi_ref, tri_ref,
                 x1_ref, h_ref, e_ref, rank_ref, wcol_ref, cnt_out_ref, cnt_ref):
    i = pl.program_id(0)

    @pl.when(i == 0)
    def _():
        cnt_ref[...] = jnp.zeros_like(cnt_ref)

    x = jnp.where(i >= NT_CTX, xs_ref[...], xp_ref[...])
    m = jnp.dot(o_ref[...], wo_ref[...], preferred_element_type=F32)
    x1 = x + mod_ref[0, 2:3, :] * m
    x1_ref[...] = x1
    h = _modulate(x1, g_ref[...] * (1.0 + mod_ref[0, 4:5, :]), mod_ref[0, 3:4, :])
    h_ref[...] = _pack_rows(h)
    _route(h, wr_hl_ref, wr_hi_ref, tri_ref, cnt_ref, e_ref, rank_ref, wcol_ref, cnt_out_ref)


def _router_weights(w_group, w_expert):
    w = jnp.concatenate([w_group, jnp.zeros((D, 8 - N_GROUPS), F32),
                         jnp.transpose(w_expert, (1, 0, 2)).reshape(D, NE),
                         jnp.zeros((D, LANES - 8 - NE), F32)], axis=1)
    w_hi = w.astype(BF16)
    w_lo = (w - w_hi.astype(F32)).astype(BF16)
    return jnp.concatenate([w_hi, w_lo], axis=1), w_hi


def _route_outs(tile, first=0):
    specs = [
        pl.BlockSpec((tile, D), lambda i: (first + i, 0)),
        pl.BlockSpec((tile, HALF), lambda i: (first + i, 0)),
        pl.BlockSpec((TOP_K, tile), lambda i: (0, first + i)),
        pl.BlockSpec((TOP_K, tile), lambda i: (0, first + i)),
        pl.BlockSpec((tile, LANES), lambda i: (first + i, 0)),
        pl.BlockSpec((NE, LANES), lambda i: (0, 0)),
    ]
    shapes = [
        jax.ShapeDtypeStruct((T, D), F32),
        jax.ShapeDtypeStruct((T, HALF), U32),
        jax.ShapeDtypeStruct((TOP_K, T), I32),
        jax.ShapeDtypeStruct((TOP_K, T), I32),
        jax.ShapeDtypeStruct((T, LANES), F32),
        jax.ShapeDtypeStruct((NE, LANES), F32),
    ]
    return specs, shapes


def _tri():
    a = np.arange(256)
    return jnp.asarray((a[:, None] <= a[None, :]).astype(np.float32), BF16)


def _post(o, xp, xs, mod, w_o, g_ffn, wr_hl, wr_hi):
    const = lambda shape: pl.BlockSpec(shape, lambda i: (0,) * len(shape))
    out_specs, out_shape = _route_outs(TILE)
    return pl.pallas_call(
        _post_kernel,
        grid=(NT,),
        in_specs=[
            pl.BlockSpec((TILE, DQ), lambda i: (i, 0)),
            pl.BlockSpec((TILE, D), lambda i: (jnp.minimum(i, NT_CTX - 1), 0)),
            pl.BlockSpec((TILE, D), lambda i: (jnp.maximum(i - NT_CTX, 0), 0)),
            pl.BlockSpec((1, N_MOD, D), lambda i: (_cond_index(i, TILE), 0, 0)),
            const((DQ, D)), const((1, D)), const((D, 2 * LANES)), const((D, LANES)), const((256, 256)),
        ],
        out_specs=out_specs,
        out_shape=out_shape,
        scratch_shapes=[pltpu.VMEM((NE, LANES), F32)],
        compiler_params=_vmem_limit(56),
        name="post_attn",
    )(o, xp, xs, mod, w_o, g_ffn, wr_hl, wr_hi, _tri())


def _pool_kernel(block_kinds, n_alias, y0_ref, y1_ref, xin_ref, wcin_ref, modp_ref, mod_ref, g_ref, pw_ref,
                 ps_ref, gf_ref, wr_hl_ref, wr_hi_ref, tri_ref, band_ref, inv_cnt_ref, cnt_in_ref, *refs):
    (x1_ref, h_ref, e_ref, rank_ref, wcol_ref, cnt_out_ref, cnt_ref, hhi_ref, hlo_ref) = refs[n_alias:]
    i = pl.program_id(0)

    @pl.when(i == 0)
    def _():
        cnt_ref[...] = cnt_in_ref[...]

    x = _combined(y0_ref, y1_ref, xin_ref, wcin_ref, modp_ref)
    h = _modulate(x, g_ref[...] * (1.0 + mod_ref[0, 1:2, :]), mod_ref[0, 0:1, :])
    @pl.when(i == 0)
    def _():
        zeros = jnp.zeros((PB_HALO, D), BF16)
        for ref in (hhi_ref, hlo_ref):
            ref[0:PB_HALO, :] = zeros
            ref[PB_HALO + PTILE:PB_HALO + PTILE + PB_HALO, :] = zeros

    h_hi = h.astype(BF16)
    hhi_ref[PB_HALO:PB_HALO + PTILE, :] = h_hi
    hlo_ref[PB_HALO:PB_HALO + PTILE, :] = (h - h_hi.astype(F32)).astype(BF16)

    outs = []
    for gi in range(len(POOL_WINDOWS)):
        cols = slice(gi * PGD, (gi + 1) * PGD)
        sums = []
        for b in range(PTILE // PB):
            bm = band_ref[block_kinds[b], gi]
            sums.append(jnp.dot(bm, hhi_ref[b * PB:b * PB + PBW, cols], preferred_element_type=F32)
                        + jnp.dot(bm, hlo_ref[b * PB:b * PB + PBW, cols], preferred_element_type=F32))
        diff = jnp.concatenate(sums, axis=0) * inv_cnt_ref[:, gi:gi + 1] - h[:, cols]
        outs.append(jnp.dot(diff.astype(BF16), pw_ref[gi], preferred_element_type=F32))
    m = jnp.concatenate(outs, axis=1) * ps_ref[...]
    x1 = x + mod_ref[0, 2:3, :] * m
    x1_ref[...] = x1
    hf = _modulate(x1, gf_ref[...] * (1.0 + mod_ref[0, 4:5, :]), mod_ref[0, 3:4, :])
    h_ref[...] = _pack_rows(hf)
    _route(hf, wr_hl_ref, wr_hi_ref, tri_ref, cnt_ref, e_ref, rank_ref, wcol_ref, cnt_out_ref)


def _pool_constants(seq):
    r = np.arange(PB)[:, None]
    d = np.arange(PBW)[None, :] - PB_HALO - r
    found, kinds = {}, []
    for b in range(PTILE // PB):
        tm = (b * PB + r) % seq
        inseq = (tm + d >= 0) & (tm + d < seq)
        mats = np.stack([(inseq & (d >= -(w // 2)) & (d <= w - w // 2 - 1)).astype(np.float32)
                         for w in POOL_WINDOWS])
        kinds.append(found.setdefault(mats.tobytes(), (len(found), mats))[0])
    bands = np.stack([m for _, m in sorted(found.values(), key=lambda kv: kv[0])])
    tm = np.arange(PTILE) % seq
    inv_cnt = np.zeros((PTILE, LANES), np.float32)
    for g, w in enumerate(POOL_WINDOWS):
        left, right = w // 2, w - w // 2 - 1
        inv_cnt[:, g] = 1.0 / (np.minimum(tm + right + 1, seq) - np.maximum(tm - left, 0)).astype(np.float32)
    return jnp.asarray(bands, BF16), tuple(kinds), inv_cnt


def _pool(group, y2g, x, wcol, mod_prev, mod, g_mix, pool_w, pool_scale, g_ffn, wr_hl, wr_hi, cnt_in, prev):
    const = lambda shape: pl.BlockSpec(shape, lambda i: (0,) * len(shape))
    tiles = (T_CTX if group == 0 else T_LAT) // PTILE
    first = 0 if group == 0 else T_CTX // PTILE
    out_specs, out_shape = _route_outs(PTILE, first)
    aliases = () if prev is None else tuple(prev)
    bands, block_kinds, inv_cnt = _pool_constants(SEQ if group == 0 else DEC_SEQ)
    n_fixed = 16
    return pl.pallas_call(
        functools.partial(_pool_kernel, block_kinds, len(aliases)),
        grid=(tiles,),
        in_specs=[
            pl.BlockSpec((PTILE, HALF), lambda i: (i, 0)),
            pl.BlockSpec((PTILE, HALF), lambda i: (tiles + i, 0)),
            pl.BlockSpec((PTILE, D), lambda i: (first + i, 0)),
            pl.BlockSpec((PTILE, LANES), lambda i: (first + i, 0)),
            pl.BlockSpec((1, N_MOD, D), lambda i: (_cond_index(first + i, PTILE), 0, 0)),
            pl.BlockSpec((1, N_MOD, D), lambda i: (_cond_index(first + i, PTILE), 0, 0)),
            const((1, D)), const((len(POOL_WINDOWS), PGD, PGD)), const((1, D)), const((1, D)),
            const((D, 2 * LANES)), const((D, LANES)), const((256, 256)), const(bands.shape),
            const((PTILE, LANES)), const((NE, LANES)),
        ] + [pl.BlockSpec(memory_space=pl.ANY)] * len(aliases),
        out_specs=out_specs,
        out_shape=out_shape,
        input_output_aliases={n_fixed + k: k for k in range(len(aliases))},
        scratch_shapes=[pltpu.VMEM((NE, LANES), F32), pltpu.VMEM((PTILE + 2 * PB_HALO, D), BF16),
                        pltpu.VMEM((PTILE + 2 * PB_HALO, D), BF16)],
        compiler_params=_vmem_limit(56),
        name="pool_mixer",
    )(y2g, y2g, x, wcol, mod_prev, mod, g_mix, pool_w, pool_scale, g_ffn, wr_hl, wr_hi, _tri(), bands, inv_cnt,
      cnt_in, *aliases)


def _plan(e, rank, cnt):
    counts = cnt[:, 0].astype(I32)
    n_tiles = (counts + TM - 1) // TM
    tile_end = jnp.cumsum(n_tiles)
    base = (tile_end - n_tiles) * TM
    pos = jnp.sum(jnp.where(e[None] == jnp.arange(NE, dtype=I32)[:, None, None], base[:, None, None], 0),
                  axis=0) + rank
    used = tile_end[-1]
    tile_id = jnp.minimum(jnp.arange(N_ROW_TILES, dtype=I32), used - 1)
    tile_expert = jnp.sum((tile_id[:, None] >= tile_end[None, :]).astype(I32), axis=1)
    return pos, tile_expert, used.reshape(1)


def _sc_mesh():
    return plsc.VectorSubcoreMesh(core_axis_name="core", subcore_axis_name="subcore")


def _sc_worker():
    return lax.axis_index("core") * SC_SUBCORES + lax.axis_index("subcore")


def _sc_dispatch(h, pos):
    per = T // SC_WORKERS
    nwin = per // SC_WIN

    @functools.partial(
        pl.kernel, out_type=jax.ShapeDtypeStruct((NPAD, HALF), U32), mesh=_sc_mesh(),
        scratch_types=[pltpu.VMEM((2, SC_WIN, HALF), U32), pltpu.VMEM((2, SC_WIN), I32),
                       pltpu.VMEM((2, SC_WIN), I32), pltpu.SemaphoreType.DMA((2,)),
                       pltpu.SemaphoreType.DMA((2,))])
    def run(h_hbm, pos_hbm, xs_hbm, buf, idx0, idx1, sem_in, sem_out):
        first = _sc_worker() * per

        def loads(j, s):
            base = first + j * SC_WIN
            return (pltpu.make_async_copy(h_hbm.at[pl.ds(base, SC_WIN)], buf.at[s], sem_in.at[s]),
                    pltpu.make_async_copy(pos_hbm.at[pl.ds(base, SC_WIN)], idx0.at[s], sem_in.at[s]),
                    pltpu.make_async_copy(pos_hbm.at[pl.ds(T + base, SC_WIN)], idx1.at[s], sem_in.at[s]))

        def stores(s):
            return (pltpu.make_async_copy(buf.at[s], xs_hbm.at[idx0.at[s]], sem_out.at[s]),
                    pltpu.make_async_copy(buf.at[s], xs_hbm.at[idx1.at[s]], sem_out.at[s]))

        for c in loads(0, 0):
            c.start()

        @pl.loop(0, nwin // 2)
        def _(jj):
            for s in range(2):
                j = jj * 2 + s
                for c in loads(j, s):
                    c.wait()
                for c in stores(s):
                    c.start()

                @pl.when(j >= 1)
                def _():
                    for c in stores(1 - s):
                        c.wait()

                @pl.when(j + 1 < nwin)
                def _():
                    for c in loads(j + 1, 1 - s):
                        c.start()

        for c in stores((nwin - 1) % 2):
            c.wait()

    return run(h, pos.reshape(TOP_K * T))


def _sc_gather(ys, pos, group):
    t0, n_g = (0, T_CTX) if group == 0 else (T_CTX, T_LAT)
    idx_all = jnp.concatenate([pos[k, t0:t0 + n_g] for k in range(TOP_K)])
    n = TOP_K * n_g
    per = n // SC_WORKERS
    nwin = per // SC_WIN

    @functools.partial(
        pl.kernel, out_type=jax.ShapeDtypeStruct((n, HALF), U32), mesh=_sc_mesh(),
        scratch_types=[pltpu.VMEM((2, SC_WIN, HALF), U32), pltpu.VMEM((per,), I32),
                       pltpu.SemaphoreType.DMA((2,)), pltpu.SemaphoreType.DMA((2,))])
    def run(ys_hbm, pos_hbm, out_hbm, buf, idx, sem_in, sem_out):
        first = _sc_worker() * per
        pltpu.sync_copy(pos_hbm.at[pl.ds(first, per)], idx)

        def load(j, s):
            return pltpu.make_async_copy(ys_hbm.at[idx.at[pl.ds(j * SC_WIN, SC_WIN)]], buf.at[s], sem_in.at[s])

        def store(j, s):
            return pltpu.make_async_copy(buf.at[s], out_hbm.at[pl.ds(first + j * SC_WIN, SC_WIN)], sem_out.at[s])

        load(0, 0).start()

        @pl.loop(0, nwin // 2)
        def _(jj):
            for s in range(2):
                j = jj * 2 + s
                load(j, s).wait()
                store(j, s).start()

                @pl.when(j >= 1)
                def _():
                    store(j - 1, 1 - s).wait()

                @pl.when(j + 1 < nwin)
                def _():
                    load(j + 1, 1 - s).start()

        store(nwin - 1, (nwin - 1) % 2).wait()

    return run(ys, idx_all)


def _moe_kernel(te_ref, used_ref, x_ref, wg_ref, wu_ref, wd_ref, y_ref):
    del te_ref
    i = pl.program_id(0)

    @pl.when(i < used_ref[0])
    def _():
        hi, lo = _unpack_rows(x_ref[...])
        x = jnp.concatenate([hi, lo], axis=1).astype(BF16)
        a = jnp.dot(x, wg_ref[0, 0], preferred_element_type=F32)
        b = jnp.dot(x, wu_ref[0, 0], preferred_element_type=F32)
        hid = a * (1.0 / (1.0 + jnp.exp(-a))) * b
        y_ref[...] = _pack_rows(jnp.dot(hid.astype(BF16), wd_ref[0, 0], preferred_element_type=F32))


def _experts(xs, tile_expert, used, layer, w_gate, w_up, w_down):
    row = lambda i, te, used: (jnp.minimum(i, used[0] - 1), 0)
    weights = lambda i, te, used: (layer, te[i], 0, 0)
    grid_spec = pltpu.PrefetchScalarGridSpec(
        num_scalar_prefetch=2,
        grid=(N_ROW_TILES,),
        in_specs=[
            pl.BlockSpec((TM, HALF), row),
            pl.BlockSpec((1, 1, D, DE), weights),
            pl.BlockSpec((1, 1, D, DE), weights),
            pl.BlockSpec((1, 1, DE, D), weights),
        ],
        out_specs=pl.BlockSpec((TM, HALF), row),
    )
    return pl.pallas_call(
        _moe_kernel,
        grid_spec=grid_spec,
        out_shape=jax.ShapeDtypeStruct((NPAD, HALF), U32),
        compiler_params=_vmem_limit(40),
        name="moe_experts",
    )(tile_expert, used, xs, w_gate, w_up, w_down)


def _combined(y0_ref, y1_ref, x_ref, wcol_ref, mod_ref):
    w0 = wcol_ref[:, 0:1]
    w1 = wcol_ref[:, 1:2]
    hi0, lo0 = _unpack_rows(y0_ref[...])
    hi1, lo1 = _unpack_rows(y1_ref[...])
    y = jnp.concatenate([w0 * hi0 + w1 * hi1, w0 * lo0 + w1 * lo1], axis=1)
    return x_ref[...] + mod_ref[0, 5:6, :] * y


def _combine_kernel(y0_ref, y1_ref, x_ref, wcol_ref, mod_ref, o_ref):
    o_ref[...] = _combined(y0_ref, y1_ref, x_ref, wcol_ref, mod_ref)


def _combine(group, y2g, x, wcol, mod):
    n_g = T_CTX if group == 0 else T_LAT
    tiles = n_g // CT
    first = 0 if group == 0 else T_CTX // CT
    return pl.pallas_call(
        _combine_kernel,
        grid=(tiles,),
        in_specs=[
            pl.BlockSpec((CT, HALF), lambda i: (i, 0)),
            pl.BlockSpec((CT, HALF), lambda i: (tiles + i, 0)),
            pl.BlockSpec((CT, D), lambda i: (first + i, 0)),
            pl.BlockSpec((CT, LANES), lambda i: (first + i, 0)),
            pl.BlockSpec((1, N_MOD, D), lambda i: (_cond_index(first + i, CT), 0, 0)),
        ],
        out_specs=pl.BlockSpec((CT, D), lambda i: (i, 0)),
        out_shape=jax.ShapeDtypeStruct((n_g, D), F32),
        name="moe_combine",
    )(y2g, y2g, x, wcol, mod)


def _moe(h, e, rank, cnt, layer, w_gate, w_up, w_down):
    pos, tile_expert, used = _plan(e, rank, cnt)
    xs = _sc_dispatch(h, pos)
    ys = _experts(xs, tile_expert, used, layer, w_gate, w_up, w_down)
    return _sc_gather(ys, pos, 0), _sc_gather(ys, pos, 1)


def kernel(x_prompt, x_sample, c, cache_k, cache_v, c_ctx, norm_mix_g, norm_ffn_g, ada_w, ada_b, attn_w_qkv, attn_q_norm, attn_k_norm, attn_w_o, pool_w, pool_scale, moe_w_group, moe_w_expert, moe_w_gate, moe_w_up, moe_w_down):
    xp = x_prompt.reshape(T_CTX, D)
    xs = x_sample.reshape(T_LAT, D)
    cond = jnp.concatenate([c_ctx[None, :], c, jnp.zeros((N_COND - 1 - DEC_BATCH, D), F32)], axis=0)
    mod = _ada(cond, ada_w, ada_b)

    qt, k2, vt, new_k, new_v = _qkv(xp, xs, mod[0], norm_mix_g[0:1], attn_w_qkv[0].astype(BF16),
                                    attn_q_norm[0:1], attn_k_norm[0:1])
    o, w_gate, w_up, w_down = _attention(qt, k2, vt, cache_k[:, 0].reshape(DEC_BATCH, PAST, DKV),
                                         cache_v[:, 0].reshape(DEC_BATCH, PAST, DKV),
                                         moe_w_gate, moe_w_up, moe_w_down)
    wr_hl, wr_hi = _router_weights(moe_w_group[0], moe_w_expert[0])
    x1, h, e, rank, wcol, cnt = _post(o, xp, xs, mod[0], attn_w_o[0].astype(BF16), norm_ffn_g[0:1],
                                      wr_hl, wr_hi)
    y2 = _moe(h, e, rank, cnt, 0, w_gate, w_up, w_down)

    wr_hl, wr_hi = _router_weights(moe_w_group[1], moe_w_expert[1])
    pool_args = (x1, wcol, mod[0], mod[1], norm_mix_g[1:2], pool_w[0].astype(BF16), pool_scale[0:1],
                 norm_ffn_g[1:2], wr_hl, wr_hi)
    outs = _pool(0, y2[0], *pool_args, jnp.zeros((NE, LANES), F32), None)
    x3, h, e, rank, wcol, cnt = _pool(1, y2[1], *pool_args, outs[5], outs[:5])
    y2 = _moe(h, e, rank, cnt, 1, w_gate, w_up, w_down)
    y_prompt = _combine(0, y2[0], x3, wcol, mod[1])
    y_sample = _combine(1, y2[1], x3, wcol, mod[1])

    return (y_prompt.reshape(BATCH, SEQ, D), y_sample.reshape(DEC_BATCH, DEC_SEQ, D),
            new_k.reshape(BATCH, 1, SEQ, N_KV, HD), new_v.reshape(BATCH, 1, SEQ, N_KV, HD))
```

```python
import functools

import jax
import jax.numpy as jnp
import numpy as np
from jax import lax
from jax.experimental import pallas as pl
from jax.experimental.pallas import tpu as pltpu
from jax.experimental.pallas import tpu_sc as plsc

F32 = jnp.float32
BF16 = jnp.bfloat16
I32 = jnp.int32
U32 = jnp.uint32

D = 1024
BATCH, SEQ = 32, 256
DEC_BATCH, DEC_SEQ, PAST = 8, 1024, 512
T_CTX = BATCH * SEQ
T_LAT = DEC_BATCH * DEC_SEQ
T = T_CTX + T_LAT
GRID_W = 64
N_HEADS, N_KV, HD = 16, 4, 64
DQ = N_HEADS * HD
DKV = N_KV * HD
ROPE_THETA = 10000.0
POOL_WINDOWS = (2, 4, 8, 16)
PGD = D // len(POOL_WINDOWS)
N_GROUPS, E_PER_G, TOP_K = 4, 8, 2
NE = N_GROUPS * E_PER_G
DE = D // 4
N_MOD = 6
EPS = 1e-6
N_COND = 16

TILE = 512
NT = T // TILE
NT_CTX = T_CTX // TILE
QB = 256
PTILE = 1024
PB = 128
PBW = 256
PB_HALO = (PBW - PB) // 2
TM = 1024
N_ROW_TILES = (TOP_K * T) // TM + NE
NPAD = N_ROW_TILES * TM
CT = 1024
LANES = 128
NEG_INF = float("-inf")
LOG2_E = 1.4426950408889634
Q_SCALE = HD ** -0.5 * LOG2_E
HALF = D // 2
SC_CORES, SC_SUBCORES = 2, 16
SC_WORKERS = SC_CORES * SC_SUBCORES
SC_WIN = 64


def _vmem_limit(mib):
    return pltpu.CompilerParams(vmem_limit_bytes=mib * 1024 * 1024)


def _modulate(x, gain_scale, shift):
    ms = jnp.mean(x * x, axis=-1, keepdims=True)
    return (x * lax.rsqrt(ms + EPS)) * gain_scale + shift


def _head_norm(z, gain, ind, ind_t2):
    ss = jnp.dot((z * z).astype(BF16), ind, preferred_element_type=F32)
    inv = lax.rsqrt(ss * (1.0 / HD) + EPS)
    inv_hi = inv.astype(BF16)
    inv_lo = (inv - inv_hi.astype(F32)).astype(BF16)
    scale = jnp.dot(jnp.concatenate([inv_hi, inv_lo], axis=1), ind_t2, preferred_element_type=F32)
    return z * scale * gain


def _rope(z, cos_t, sin_t):
    lane = lax.broadcasted_iota(I32, (z.shape[0], LANES), 1)
    low = (lane % 32) < 16
    outs = []
    for c in range(z.shape[1] // LANES):
        zc = z[:, c * LANES:(c + 1) * LANES]
        up = pltpu.roll(zc, 16, axis=1)
        dn = pltpu.roll(zc, LANES - 16, axis=1)
        outs.append(zc * cos_t + jnp.where(low, dn, up) * sin_t)
    return jnp.concatenate(outs, axis=1)


def _rope_t(zt, cos_tt, sin_tt):
    sub = lax.broadcasted_iota(I32, (LANES, zt.shape[1]), 0)
    low = (sub % 32) < 16
    outs = []
    for c in range(zt.shape[0] // LANES):
        zc = zt[c * LANES:(c + 1) * LANES]
        up = pltpu.roll(zc, 16, axis=0)
        dn = pltpu.roll(zc, LANES - 16, axis=0)
        outs.append(zc * cos_tt + jnp.where(low, dn, up) * sin_tt)
    return jnp.concatenate(outs, axis=0)


def _pack_rows(z):
    bits = lax.bitcast_convert_type(z.astype(BF16).astype(F32), U32)
    return bits[:, :HALF] | (bits[:, HALF:] >> 16)


def _unpack_rows(p):
    hi = lax.bitcast_convert_type(p & jnp.uint32(0xFFFF0000), F32)
    lo = lax.bitcast_convert_type(p << 16, F32)
    return hi, lo


def _ada_kernel(c_ref, w_ref, b_ref, o_ref):
    c = c_ref[...]
    a = c * (1.0 / (1.0 + jnp.exp(-c)))
    o_ref[0] = jnp.dot(a.astype(BF16), w_ref[0].astype(BF16), preferred_element_type=F32) + b_ref[0]


def _ada(cond, ada_w, ada_b):
    nb = 1536
    depth = ada_w.shape[0]
    out = pl.pallas_call(
        _ada_kernel,
        grid=(depth, (N_MOD * D) // nb),
        in_specs=[
            pl.BlockSpec((N_COND, D), lambda l, j: (0, 0)),
            pl.BlockSpec((1, D, nb), lambda l, j: (l, 0, j)),
            pl.BlockSpec((1, 1, nb), lambda l, j: (l, 0, j)),
        ],
        out_specs=pl.BlockSpec((1, N_COND, nb), lambda l, j: (l, 0, j)),
        out_shape=jax.ShapeDtypeStruct((depth, N_COND, N_MOD * D), F32),
        compiler_params=_vmem_limit(40),
        name="ada",
    )(cond, ada_w, ada_b.reshape(depth, 1, N_MOD * D))
    return out.reshape(depth, N_COND, N_MOD, D)


def _cond_index(i, tile):
    n_ctx = T_CTX // tile
    per_batch = DEC_SEQ // tile
    return jnp.where(i < n_ctx, 0, 1 + (i - n_ctx) // per_batch)


def _qkv_kernel(xp_ref, xs_ref, mod_ref, g_ref, w_ref, qg_ref, kg_ref, indq_ref, indqt_ref,
                indk_ref, indkt_ref, cos_ref, sin_ref, cos_t_ref, sin_t_ref,
                qt_ref, k2_ref, vt_ref, nk_ref, nv_ref):
    i = pl.program_id(0)
    is_lat = i >= NT_CTX
    gain_scale = g_ref[...] * (1.0 + mod_ref[0, 1:2, :])
    subs = [slice(b * QB, (b + 1) * QB) for b in range(TILE // QB)]
    hs = [_modulate(jnp.where(is_lat, xs_ref[r, :], xp_ref[r, :]), gain_scale, mod_ref[0, 0:1, :]).astype(BF16)
          for r in subs]
    qkvs = [jnp.dot(h, w_ref[...], preferred_element_type=F32) for h in hs]
    qs = [_head_norm(z[:, :DQ], qg_ref[...], indq_ref[...], indqt_ref[...]) for z in qkvs]
    ks = [_head_norm(z[:, DQ:DQ + DKV], kg_ref[...], indk_ref[...], indkt_ref[...]) for z in qkvs]
    vs = [z[:, DQ + DKV:] for z in qkvs]
    qts = [(q * Q_SCALE).T for q in qs]
    for r, v in zip(subs, vs):
        vt_ref[:, r] = v.T.astype(BF16)

    @pl.when(is_lat)
    def _():
        for b, r in enumerate(subs):
            qt_ref[b] = _rope_t(qts[b], cos_t_ref[:, r], sin_t_ref[:, r]).astype(BF16)
            k2_ref[r, :] = _dup_heads(_rope(ks[b], cos_ref[r, :], sin_ref[r, :])).astype(BF16)

    @pl.when(jnp.logical_not(is_lat))
    def _():
        for b, r in enumerate(subs):
            qt_ref[b] = qts[b].astype(BF16)
            k2_ref[r, :] = _dup_heads(ks[b]).astype(BF16)
            nk_ref[r, :] = ks[b]
            nv_ref[r, :] = vs[b]


def _rope_tables():
    rows = DEC_SEQ // GRID_W
    row = np.broadcast_to(np.arange(rows, dtype=np.float32)[:, None], (rows, GRID_W)).reshape(-1)
    col = np.broadcast_to(np.arange(GRID_W, dtype=np.float32)[None, :], (rows, GRID_W)).reshape(-1)
    axis_dim = HD // 2
    inv_freq = np.power(np.float32(ROPE_THETA),
                        -np.arange(0, axis_dim, 2, dtype=np.float32) / np.float32(axis_dim))
    ang = np.concatenate([row[:, None] * inv_freq, col[:, None] * inv_freq], axis=-1).astype(np.float32)
    cos, sin = np.cos(ang), np.sin(ang)
    quarter = HD // 4
    cos_h = np.concatenate([cos[:, :quarter], cos[:, :quarter], cos[:, quarter:], cos[:, quarter:]], axis=1)
    sin_h = np.concatenate([-sin[:, :quarter], sin[:, :quarter], -sin[:, quarter:], sin[:, quarter:]], axis=1)
    return (np.tile(cos_h, (1, LANES // HD)).astype(np.float32),
            np.tile(sin_h, (1, LANES // HD)).astype(np.float32))


def _head_indicators(width):
    col = np.arange(width)[:, None]
    head = np.arange(LANES)[None, :]
    ind = (col // HD == head).astype(np.float32)
    return jnp.asarray(ind, BF16), jnp.asarray(np.concatenate([ind.T, ind.T], axis=0), BF16)


def _dup_heads(z):
    lane = lax.broadcasted_iota(I32, (z.shape[0], LANES), 1)
    cols = []
    for p in range(N_KV // 2):
        blk = z[:, p * LANES:(p + 1) * LANES]
        swp = pltpu.roll(blk, HD, axis=1)
        cols += [jnp.where(lane < HD, blk, swp), jnp.where(lane < HD, swp, blk)]
    return jnp.concatenate(cols, axis=1)


def _qkv(xp, xs, mod, g, w_qkv, q_gain, k_gain):
    indq, indqt = _head_indicators(DQ)
    indk, indkt = _head_indicators(DKV)
    cos_t, sin_t = _rope_tables()
    per_batch = DEC_SEQ // TILE
    const = lambda shape: pl.BlockSpec(shape, lambda i: (0,) * len(shape))
    return pl.pallas_call(
        _qkv_kernel,
        grid=(NT,),
        in_specs=[
            pl.BlockSpec((TILE, D), lambda i: (jnp.minimum(i, NT_CTX - 1), 0)),
            pl.BlockSpec((TILE, D), lambda i: (jnp.maximum(i - NT_CTX, 0), 0)),
            pl.BlockSpec((1, N_MOD, D), lambda i: (_cond_index(i, TILE), 0, 0)),
            const((1, D)),
            const((D, DQ + 2 * DKV)),
            const((1, DQ)), const((1, DKV)),
            const((DQ, LANES)), const((2 * LANES, DQ)),
            const((DKV, LANES)), const((2 * LANES, DKV)),
            pl.BlockSpec((TILE, LANES), lambda i: (jnp.maximum(i - NT_CTX, 0) % per_batch, 0)),
            pl.BlockSpec((TILE, LANES), lambda i: (jnp.maximum(i - NT_CTX, 0) % per_batch, 0)),
            pl.BlockSpec((LANES, TILE), lambda i: (0, jnp.maximum(i - NT_CTX, 0) % per_batch)),
            pl.BlockSpec((LANES, TILE), lambda i: (0, jnp.maximum(i - NT_CTX, 0) % per_batch)),
        ],
        out_specs=[
            pl.BlockSpec((TILE // QB, DQ, QB), lambda i: (i, 0, 0)),
            pl.BlockSpec((TILE, 2 * DKV), lambda i: (i, 0)),
            pl.BlockSpec((DKV, TILE), lambda i: (0, i)),
            pl.BlockSpec((TILE, DKV), lambda i: (jnp.minimum(i, NT_CTX - 1), 0)),
            pl.BlockSpec((TILE, DKV), lambda i: (jnp.minimum(i, NT_CTX - 1), 0)),
        ],
        out_shape=[
            jax.ShapeDtypeStruct((T // QB, DQ, QB), BF16),
            jax.ShapeDtypeStruct((T, 2 * DKV), BF16),
            jax.ShapeDtypeStruct((DKV, T), BF16),
            jax.ShapeDtypeStruct((T_CTX, DKV), F32),
            jax.ShapeDtypeStruct((T_CTX, DKV), F32),
        ],
        compiler_params=_vmem_limit(56),
        name="qkv",
    )(xp, xs, mod, g, w_qkv, jnp.tile(q_gain, (1, N_HEADS)), jnp.tile(k_gain, (1, N_KV)),
      indq, indqt, indk, indkt, cos_t, sin_t, cos_t.T, sin_t.T)


def _unit_scores(u, pairs, k2_parts, qt_ref):
    cols = []
    for j in range(pairs):
        pair = u * pairs + j
        qt_pair = qt_ref[pair * LANES:(pair + 1) * LANES, :]
        sub = lax.broadcasted_iota(I32, qt_pair.shape, 0)
        zero = jnp.zeros_like(qt_pair)
        cols += [jnp.where(sub < HD, qt_pair, zero), jnp.where(sub >= HD, qt_pair, zero)]
    qtu = jnp.concatenate(cols, axis=1)
    return [jnp.dot(k2, qtu, preferred_element_type=F32) for k2 in k2_parts]


def _softmax_values(ss, vt_parts):
    m = ss[0].max(axis=0, keepdims=True)
    for s in ss[1:]:
        m = jnp.maximum(m, s.max(axis=0, keepdims=True))
    acc = None
    l = None
    for s, vt in zip(ss, vt_parts):
        e = jnp.exp2(s - m)
        ls = e.sum(axis=0, keepdims=True)
        l = ls if l is None else l + ls
        t = jnp.dot(vt, e.astype(BF16), preferred_element_type=F32)
        acc = t if acc is None else acc + t
    return acc * (1.0 / l)


def _attend(keys_of, values_of, qt_ref, o_ref, pairs, ahead):
    n_units = N_HEADS // (2 * pairs)
    group = lambda u: (u * 2 * pairs) // (N_HEADS // N_KV)
    scores = lambda u: _unit_scores(u, pairs, keys_of(group(u)), qt_ref)
    pending = [scores(u) for u in range(min(ahead, n_units))]
    outs = []
    for u in range(n_units):
        if u + ahead < n_units:
            pending.append(scores(u + ahead))
        outs.append(_softmax_values(pending.pop(0), values_of(group(u))))
    lq = o_ref.shape[0]
    heads = [o[:, h * lq:(h + 1) * lq] for o in outs for h in range(2 * pairs)]
    o_ref[...] = jnp.concatenate(heads, axis=0).T.astype(BF16)


def _attn_ctx_kernel(qt_ref, k2_ref, vt_ref, o_ref):
    _attend(lambda g: [k2_ref[:, g * LANES:(g + 1) * LANES]],
            lambda g: [vt_ref[g * HD:(g + 1) * HD, :]], qt_ref.at[0], o_ref, pairs=2, ahead=1)


def _attn_lat_kernel(qt_ref, k2c_ref, vtc_ref, k2n_ref, vtn_ref, wg_ref, wu_ref, wd_ref, o_in_ref,
                     o_ref, wgb_ref, wub_ref, wdb_ref):
    del o_in_ref
    wgb_ref[...] = wg_ref[...].astype(BF16)
    wub_ref[...] = wu_ref[...].astype(BF16)
    wdb_ref[...] = wd_ref[...].astype(BF16)
    _attend(lambda g: [k2c_ref[0, :, g * LANES:(g + 1) * LANES], k2n_ref[:, g * LANES:(g + 1) * LANES]],
            lambda g: [vtc_ref[0, g * HD:(g + 1) * HD, :], vtn_ref[g * HD:(g + 1) * HD, :]], qt_ref.at[0],
            o_ref, pairs=1, ahead=2)


def _cache_kernel(ck_ref, cv_ref, k2_ref, vt_ref):
    k2_ref[0] = _dup_heads(ck_ref[0]).astype(BF16)
    vt_ref[0] = cv_ref[0].T.astype(BF16)


def _attention(qt, k2, vt, cache_k, cache_v, w_gate, w_up, w_down):
    k2c, vtc = pl.pallas_call(
        _cache_kernel,
        grid=(DEC_BATCH,),
        in_specs=[
            pl.BlockSpec((1, PAST, DKV), lambda b: (b, 0, 0)),
            pl.BlockSpec((1, PAST, DKV), lambda b: (b, 0, 0)),
        ],
        out_specs=[
            pl.BlockSpec((1, PAST, 2 * DKV), lambda b: (b, 0, 0)),
            pl.BlockSpec((1, DKV, PAST), lambda b: (b, 0, 0)),
        ],
        out_shape=[
            jax.ShapeDtypeStruct((DEC_BATCH, PAST, 2 * DKV), BF16),
            jax.ShapeDtypeStruct((DEC_BATCH, DKV, PAST), BF16),
        ],
        name="cache_prep",
    )(cache_k, cache_v)

    o_ctx = pl.pallas_call(
        _attn_ctx_kernel,
        grid=(BATCH,),
        in_specs=[
            pl.BlockSpec((1, DQ, QB), lambda b: (b, 0, 0)),
            pl.BlockSpec((SEQ, 2 * DKV), lambda b: (b, 0)),
            pl.BlockSpec((DKV, SEQ), lambda b: (0, b)),
        ],
        out_specs=pl.BlockSpec((SEQ, DQ), lambda b: (b, 0)),
        out_shape=jax.ShapeDtypeStruct((T, DQ), BF16),
        name="attn_ctx",
    )(qt, k2, vt)

    qb = QB
    nqb = DEC_SEQ // qb
    lat0 = T_CTX // qb
    depth = w_gate.shape[0]
    eps = (depth * NE) // (DEC_BATCH * nqb)
    per_layer = NE // eps
    wblock = lambda b, j: ((b * nqb + j) // per_layer, (b * nqb + j) % per_layer, 0, 0)
    return pl.pallas_call(
        _attn_lat_kernel,
        grid=(DEC_BATCH, nqb),
        in_specs=[
            pl.BlockSpec((1, DQ, QB), lambda b, j: (lat0 + b * nqb + j, 0, 0)),
            pl.BlockSpec((1, PAST, 2 * DKV), lambda b, j: (b, 0, 0)),
            pl.BlockSpec((1, DKV, PAST), lambda b, j: (b, 0, 0)),
            pl.BlockSpec((DEC_SEQ, 2 * DKV), lambda b, j: (T_CTX // DEC_SEQ + b, 0)),
            pl.BlockSpec((DKV, DEC_SEQ), lambda b, j: (0, T_CTX // DEC_SEQ + b)),
            pl.BlockSpec((1, eps, D, DE), wblock),
            pl.BlockSpec((1, eps, D, DE), wblock),
            pl.BlockSpec((1, eps, DE, D), wblock),
            pl.BlockSpec(memory_space=pl.ANY),
        ],
        out_specs=[
            pl.BlockSpec((qb, DQ), lambda b, j: (lat0 + b * nqb + j, 0)),
            pl.BlockSpec((1, eps, D, DE), wblock),
            pl.BlockSpec((1, eps, D, DE), wblock),
            pl.BlockSpec((1, eps, DE, D), wblock),
        ],
        out_shape=[
            jax.ShapeDtypeStruct((T, DQ), BF16),
            jax.ShapeDtypeStruct(w_gate.shape, BF16),
            jax.ShapeDtypeStruct(w_up.shape, BF16),
            jax.ShapeDtypeStruct(w_down.shape, BF16),
        ],
        input_output_aliases={8: 0},
        compiler_params=_vmem_limit(56),
        name="attn_lat",
    )(qt, k2c, vtc, k2, vt, w_gate, w_up, w_down, o_ctx)


def _route(h, wr_hl_ref, wr_hi_ref, tri_ref, cnt_ref, e_ref, rank_ref, wcol_ref, cnt_out_ref):
    tl = h.shape[0]
    h_hi = h.astype(BF16)
    h_lo = (h - h_hi.astype(F32)).astype(BF16)
    lg2 = jnp.dot(h_hi, wr_hl_ref[...], preferred_element_type=F32)
    lg = lg2[:, :LANES] + lg2[:, LANES:] + jnp.dot(h_lo, wr_hi_ref[...], preferred_element_type=F32)
    lt = lg.T

    sub8 = lax.broadcasted_iota(I32, (8, tl), 0).astype(F32)
    gl = jnp.where(sub8 < N_GROUPS, lt[0:8], NEG_INF)
    gmax = gl.max(axis=0, keepdims=True)
    gidx = jnp.min(jnp.where(gl == gmax, sub8, 8.0), axis=0, keepdims=True)
    gsum = jnp.sum(jnp.exp(gl - gmax), axis=0, keepdims=True)
    sel = lt[8:16]
    for g in range(1, N_GROUPS):
        sel = jnp.where(gidx == g, lt[8 + 8 * g:16 + 8 * g], sel)
    t1 = sel.max(axis=0, keepdims=True)
    j1 = jnp.min(jnp.where(sel == t1, sub8, 8.0), axis=0, keepdims=True)
    sel2 = jnp.where(sub8 == j1, NEG_INF, sel)
    t2 = sel2.max(axis=0, keepdims=True)
    j2 = jnp.min(jnp.where(sel2 == t2, sub8, 8.0), axis=0, keepdims=True)
    ex = jnp.exp(t2 - t1)
    den = 1.0 + ex
    gw = 1.0 / gsum
    w0 = gw * (1.0 / den)
    w1 = gw * (ex / den)
    e0 = gidx * E_PER_G + j1
    e1 = gidx * E_PER_G + j2

    sub_e = lax.broadcasted_iota(I32, (NE, tl), 0).astype(F32)
    oh0 = (sub_e == e0).astype(F32)
    oh1 = (sub_e == e1).astype(F32)
    c = oh0 + oh1
    carry = cnt_ref[:, 0:1]
    pieces = []
    for ch in range(tl // 256):
        cc = c[:, ch * 256:(ch + 1) * 256]
        pieces.append(jnp.dot(cc.astype(BF16), tri_ref[...], preferred_element_type=F32) + carry)
        carry = carry + jnp.sum(cc, axis=1, keepdims=True)
    csum = jnp.concatenate(pieces, axis=1)
    cnt_ref[...] = jnp.broadcast_to(carry, (NE, LANES))
    cnt_out_ref[...] = jnp.broadcast_to(carry, (NE, LANES))
    r0 = jnp.sum(oh0 * csum, axis=0, keepdims=True) - 1.0
    r1 = jnp.sum(oh1 * csum, axis=0, keepdims=True) - 1.0
    e_ref[0:1, :] = e0.astype(I32)
    e_ref[1:2, :] = e1.astype(I32)
    rank_ref[0:1, :] = r0.astype(I32)
    rank_ref[1:2, :] = r1.astype(I32)
    wcol_ref[0:1, :] = w0
    wcol_ref[1:2, :] = w1


def _post_kernel(o_ref, xp_ref, xs_ref, mod_ref, wo_ref, g_ref, wr_hl_ref, wr_hi_ref, tri_ref,
                 x1_ref, h_ref, e_ref, rank_ref, wcol_ref, cnt_out_ref, cnt_ref):
    i = pl.program_id(0)

    @pl.when(i == 0)
    def _():
        cnt_ref[...] = jnp.zeros_like(cnt_ref)

    x = jnp.where(i >= NT_CTX, xs_ref[...], xp_ref[...])
    m = jnp.dot(o_ref[...], wo_ref[...], preferred_element_type=F32)
    x1 = x + mod_ref[0, 2:3, :] * m
    x1_ref[...] = x1
    h = _modulate(x1, g_ref[...] * (1.0 + mod_ref[0, 4:5, :]), mod_ref[0, 3:4, :])
    h_ref[...] = _pack_rows(h)
    _route(h, wr_hl_ref, wr_hi_ref, tri_ref, cnt_ref, e_ref, rank_ref, wcol_ref, cnt_out_ref)


def _router_weights(w_group, w_expert):
    w = jnp.concatenate([w_group, jnp.zeros((D, 8 - N_GROUPS), F32),
                         jnp.transpose(w_expert, (1, 0, 2)).reshape(D, NE),
                         jnp.zeros((D, LANES - 8 - NE), F32)], axis=1)
    w_hi = w.astype(BF16)
    w_lo = (w - w_hi.astype(F32)).astype(BF16)
    return jnp.concatenate([w_hi, w_lo], axis=1), w_hi


def _route_outs(tile, first=0):
    specs = [
        pl.BlockSpec((tile, D), lambda i: (first + i, 0)),
        pl.BlockSpec((tile, HALF), lambda i: (first + i, 0)),
        pl.BlockSpec((TOP_K, tile), lambda i: (0, first + i)),
        pl.BlockSpec((TOP_K, tile), lambda i: (0, first + i)),
        pl.BlockSpec((TOP_K, tile), lambda i: (0, first + i)),
        pl.BlockSpec((NE, LANES), lambda i: (0, 0)),
    ]
    shapes = [
        jax.ShapeDtypeStruct((T, D), F32),
        jax.ShapeDtypeStruct((T, HALF), U32),
        jax.ShapeDtypeStruct((TOP_K, T), I32),
        jax.ShapeDtypeStruct((TOP_K, T), I32),
        jax.ShapeDtypeStruct((TOP_K, T), F32),
        jax.ShapeDtypeStruct((NE, LANES), F32),
    ]
    return specs, shapes


def _tri():
    a = np.arange(256)
    return jnp.asarray((a[:, None] <= a[None, :]).astype(np.float32), BF16)


def _post(o, xp, xs, mod, w_o, g_ffn, wr_hl, wr_hi):
    const = lambda shape: pl.BlockSpec(shape, lambda i: (0,) * len(shape))
    out_specs, out_shape = _route_outs(TILE)
    return pl.pallas_call(
        _post_kernel,
        grid=(NT,),
        in_specs=[
            pl.BlockSpec((TILE, DQ), lambda i: (i, 0)),
            pl.BlockSpec((TILE, D), lambda i: (jnp.minimum(i, NT_CTX - 1), 0)),
            pl.BlockSpec((TILE, D), lambda i: (jnp.maximum(i - NT_CTX, 0), 0)),
            pl.BlockSpec((1, N_MOD, D), lambda i: (_cond_index(i, TILE), 0, 0)),
            const((DQ, D)), const((1, D)), const((D, 2 * LANES)), const((D, LANES)), const((256, 256)),
        ],
        out_specs=out_specs,
        out_shape=out_shape,
        scratch_shapes=[pltpu.VMEM((NE, LANES), F32)],
        compiler_params=_vmem_limit(56),
        name="post_attn",
    )(o, xp, xs, mod, w_o, g_ffn, wr_hl, wr_hi, _tri())


def _pool_kernel(block_kinds, n_alias, y0_ref, y1_ref, xin_ref, wcin_ref, modp_ref, mod_ref, g_ref, pw_ref,
                 ps_ref, gf_ref, wr_hl_ref, wr_hi_ref, tri_ref, band_ref, inv_cnt_ref, cnt_in_ref, *refs):
    (x1_ref, h_ref, e_ref, rank_ref, wcol_ref, cnt_out_ref, cnt_ref, hhi_ref, hlo_ref) = refs[n_alias:]
    i = pl.program_id(0)

    @pl.when(i == 0)
    def _():
        cnt_ref[...] = cnt_in_ref[...]

    x = _combined(y0_ref, y1_ref, xin_ref, wcin_ref, modp_ref)
    h = _modulate(x, g_ref[...] * (1.0 + mod_ref[0, 1:2, :]), mod_ref[0, 0:1, :])
    @pl.when(i == 0)
    def _():
        zeros = jnp.zeros((PB_HALO, D), BF16)
        for ref in (hhi_ref, hlo_ref):
            ref[0:PB_HALO, :] = zeros
            ref[PB_HALO + PTILE:PB_HALO + PTILE + PB_HALO, :] = zeros

    h_hi = h.astype(BF16)
    hhi_ref[PB_HALO:PB_HALO + PTILE, :] = h_hi
    hlo_ref[PB_HALO:PB_HALO + PTILE, :] = (h - h_hi.astype(F32)).astype(BF16)

    outs = []
    for gi in range(len(POOL_WINDOWS)):
        cols = slice(gi * PGD, (gi + 1) * PGD)
        sums = []
        for b in range(PTILE // PB):
            bm = band_ref[block_kinds[b], gi]
            sums.append(jnp.dot(bm, hhi_ref[b * PB:b * PB + PBW, cols], preferred_element_type=F32)
                        + jnp.dot(bm, hlo_ref[b * PB:b * PB + PBW, cols], preferred_element_type=F32))
        diff = jnp.concatenate(sums, axis=0) * inv_cnt_ref[:, gi:gi + 1] - h[:, cols]
        outs.append(jnp.dot(diff.astype(BF16), pw_ref[gi], preferred_element_type=F32))
    m = jnp.concatenate(outs, axis=1) * ps_ref[...]
    x1 = x + mod_ref[0, 2:3, :] * m
    x1_ref[...] = x1
    hf = _modulate(x1, gf_ref[...] * (1.0 + mod_ref[0, 4:5, :]), mod_ref[0, 3:4, :])
    h_ref[...] = _pack_rows(hf)
    _route(hf, wr_hl_ref, wr_hi_ref, tri_ref, cnt_ref, e_ref, rank_ref, wcol_ref, cnt_out_ref)


def _pool_constants(seq):
    r = np.arange(PB)[:, None]
    d = np.arange(PBW)[None, :] - PB_HALO - r
    found, kinds = {}, []
    for b in range(PTILE // PB):
        tm = (b * PB + r) % seq
        inseq = (tm + d >= 0) & (tm + d < seq)
        mats = np.stack([(inseq & (d >= -(w // 2)) & (d <= w - w // 2 - 1)).astype(np.float32)
                         for w in POOL_WINDOWS])
        kinds.append(found.setdefault(mats.tobytes(), (len(found), mats))[0])
    bands = np.stack([m for _, m in sorted(found.values(), key=lambda kv: kv[0])])
    tm = np.arange(PTILE) % seq
    inv_cnt = np.zeros((PTILE, LANES), np.float32)
    for g, w in enumerate(POOL_WINDOWS):
        left, right = w // 2, w - w // 2 - 1
        inv_cnt[:, g] = 1.0 / (np.minimum(tm + right + 1, seq) - np.maximum(tm - left, 0)).astype(np.float32)
    return jnp.asarray(bands, BF16), tuple(kinds), inv_cnt


def _pool(group, y2g, x, wcol, mod_prev, mod, g_mix, pool_w, pool_scale, g_ffn, wr_hl, wr_hi, cnt_in, prev):
    const = lambda shape: pl.BlockSpec(shape, lambda i: (0,) * len(shape))
    tiles = (T_CTX if group == 0 else T_LAT) // PTILE
    first = 0 if group == 0 else T_CTX // PTILE
    out_specs, out_shape = _route_outs(PTILE, first)
    aliases = () if prev is None else tuple(prev)
    bands, block_kinds, inv_cnt = _pool_constants(SEQ if group == 0 else DEC_SEQ)
    n_fixed = 16
    return pl.pallas_call(
        functools.partial(_pool_kernel, block_kinds, len(aliases)),
        grid=(tiles,),
        in_specs=[
            pl.BlockSpec((PTILE, HALF), lambda i: (i, 0)),
            pl.BlockSpec((PTILE, HALF), lambda i: (tiles + i, 0)),
            pl.BlockSpec((PTILE, D), lambda i: (first + i, 0)),
            pl.BlockSpec((TOP_K, PTILE), lambda i: (0, first + i)),
            pl.BlockSpec((1, N_MOD, D), lambda i: (_cond_index(first + i, PTILE), 0, 0)),
            pl.BlockSpec((1, N_MOD, D), lambda i: (_cond_index(first + i, PTILE), 0, 0)),
            const((1, D)), const((len(POOL_WINDOWS), PGD, PGD)), const((1, D)), const((1, D)),
            const((D, 2 * LANES)), const((D, LANES)), const((256, 256)), const(bands.shape),
            const((PTILE, LANES)), const((NE, LANES)),
        ] + [pl.BlockSpec(memory_space=pl.ANY)] * len(aliases),
        out_specs=out_specs,
        out_shape=out_shape,
        input_output_aliases={n_fixed + k: k for k in range(len(aliases))},
        scratch_shapes=[pltpu.VMEM((NE, LANES), F32), pltpu.VMEM((PTILE + 2 * PB_HALO, D), BF16),
                        pltpu.VMEM((PTILE + 2 * PB_HALO, D), BF16)],
        compiler_params=_vmem_limit(56),
        name="pool_mixer",
    )(y2g, y2g, x, wcol, mod_prev, mod, g_mix, pool_w, pool_scale, g_ffn, wr_hl, wr_hi, _tri(), bands, inv_cnt,
      cnt_in, *aliases)


def _plan_kernel(e_ref, rank_ref, cnt_ref, pos_ref, te_ref, used_ref):
    counts = cnt_ref[:, 0:1]
    tiles_col = jnp.floor((counts + (TM - 1)) * (1.0 / TM))
    sub = lax.broadcasted_iota(I32, (NE, NE), 0)
    lane = lax.broadcasted_iota(I32, (NE, NE), 1)
    tiles_row = jnp.sum(jnp.where(sub == lane, tiles_col, 0.0), axis=0, keepdims=True)
    first_col = jnp.sum(jnp.where(lane < sub, tiles_row, 0.0), axis=1, keepdims=True)
    end_col = first_col + tiles_col
    used = jnp.sum(tiles_col, axis=0, keepdims=True)
    base_col = first_col * TM
    sub_e = lax.broadcasted_iota(I32, (NE, T), 0)
    for k in range(TOP_K):
        hit = sub_e == e_ref[k:k + 1, :]
        base = jnp.sum(jnp.where(hit, base_col, 0.0), axis=0, keepdims=True)
        pos_ref[k:k + 1, :] = base.astype(I32) + rank_ref[k:k + 1, :]
    tile = jnp.minimum(lax.broadcasted_iota(I32, (NE, LANES), 1).astype(F32), used - 1.0)
    te_ref[...] = jnp.sum(jnp.where(tile >= end_col, 1.0, 0.0), axis=0, keepdims=True).astype(I32)
    used_ref[...] = jnp.broadcast_to(used, (1, LANES)).astype(I32)


def _plan(e, rank, cnt):
    assert N_ROW_TILES <= LANES
    pos, te, used = pl.pallas_call(
        _plan_kernel,
        out_shape=[jax.ShapeDtypeStruct((TOP_K, T), I32), jax.ShapeDtypeStruct((1, LANES), I32),
                   jax.ShapeDtypeStruct((1, LANES), I32)],
        name="moe_plan",
    )(e, rank, cnt)
    return pos, te[0, :N_ROW_TILES], used[0, :1]


def _sc_mesh():
    return plsc.VectorSubcoreMesh(core_axis_name="core", subcore_axis_name="subcore")


def _sc_worker():
    return lax.axis_index("core") * SC_SUBCORES + lax.axis_index("subcore")


def _sc_dispatch(h, pos):
    per = T // SC_WORKERS
    nwin = per // SC_WIN

    @functools.partial(
        pl.kernel, out_type=jax.ShapeDtypeStruct((NPAD, HALF), U32), mesh=_sc_mesh(),
        scratch_types=[pltpu.VMEM((2, SC_WIN, HALF), U32), pltpu.VMEM((2, SC_WIN), I32),
                       pltpu.VMEM((2, SC_WIN), I32), pltpu.SemaphoreType.DMA((2,)),
                       pltpu.SemaphoreType.DMA((2,))])
    def run(h_hbm, pos_hbm, xs_hbm, buf, idx0, idx1, sem_in, sem_out):
        first = _sc_worker() * per

        def loads(j, s):
            base = first + j * SC_WIN
            return (pltpu.make_async_copy(h_hbm.at[pl.ds(base, SC_WIN)], buf.at[s], sem_in.at[s]),
                    pltpu.make_async_copy(pos_hbm.at[pl.ds(base, SC_WIN)], idx0.at[s], sem_in.at[s]),
                    pltpu.make_async_copy(pos_hbm.at[pl.ds(T + base, SC_WIN)], idx1.at[s], sem_in.at[s]))

        def stores(s):
            return (pltpu.make_async_copy(buf.at[s], xs_hbm.at[idx0.at[s]], sem_out.at[s]),
                    pltpu.make_async_copy(buf.at[s], xs_hbm.at[idx1.at[s]], sem_out.at[s]))

        for c in loads(0, 0):
            c.start()

        @pl.loop(0, nwin // 2)
        def _(jj):
            for s in range(2):
                j = jj * 2 + s
                for c in loads(j, s):
                    c.wait()
                for c in stores(s):
                    c.start()

                @pl.when(j >= 1)
                def _():
                    for c in stores(1 - s):
                        c.wait()

                @pl.when(j + 1 < nwin)
                def _():
                    for c in loads(j + 1, 1 - s):
                        c.start()

        for c in stores((nwin - 1) % 2):
            c.wait()

    return run(h, pos.reshape(TOP_K * T))


def _sc_gather(ys, pos, group):
    t0, n_g = (0, T_CTX) if group == 0 else (T_CTX, T_LAT)
    idx_all = jnp.concatenate([pos[k, t0:t0 + n_g] for k in range(TOP_K)])
    n = TOP_K * n_g
    per = n // SC_WORKERS
    nwin = per // SC_WIN

    @functools.partial(
        pl.kernel, out_type=jax.ShapeDtypeStruct((n, HALF), U32), mesh=_sc_mesh(),
        scratch_types=[pltpu.VMEM((2, SC_WIN, HALF), U32), pltpu.VMEM((per,), I32),
                       pltpu.SemaphoreType.DMA((2,)), pltpu.SemaphoreType.DMA((2,))])
    def run(ys_hbm, pos_hbm, out_hbm, buf, idx, sem_in, sem_out):
        first = _sc_worker() * per
        pltpu.sync_copy(pos_hbm.at[pl.ds(first, per)], idx)

        def load(j, s):
            return pltpu.make_async_copy(ys_hbm.at[idx.at[pl.ds(j * SC_WIN, SC_WIN)]], buf.at[s], sem_in.at[s])

        def store(j, s):
            return pltpu.make_async_copy(buf.at[s], out_hbm.at[pl.ds(first + j * SC_WIN, SC_WIN)], sem_out.at[s])

        load(0, 0).start()

        @pl.loop(0, nwin // 2)
        def _(jj):
            for s in range(2):
                j = jj * 2 + s
                load(j, s).wait()
                store(j, s).start()

                @pl.when(j >= 1)
                def _():
                    store(j - 1, 1 - s).wait()

                @pl.when(j + 1 < nwin)
                def _():
                    load(j + 1, 1 - s).start()

        store(nwin - 1, (nwin - 1) % 2).wait()

    return run(ys, idx_all)


def _moe_kernel(te_ref, used_ref, x_ref, wg_ref, wu_ref, wd_ref, y_ref):
    del te_ref
    i = pl.program_id(0)

    @pl.when(i < used_ref[0])
    def _():
        hi, lo = _unpack_rows(x_ref[...])
        x = jnp.concatenate([hi, lo], axis=1).astype(BF16)
        a = jnp.dot(x, wg_ref[0, 0], preferred_element_type=F32)
        b = jnp.dot(x, wu_ref[0, 0], preferred_element_type=F32)
        hid = a * (1.0 / (1.0 + jnp.exp(-a))) * b
        y_ref[...] = _pack_rows(jnp.dot(hid.astype(BF16), wd_ref[0, 0], preferred_element_type=F32))


def _experts(xs, tile_expert, used, layer, w_gate, w_up, w_down):
    row = lambda i, te, used: (jnp.minimum(i, used[0] - 1), 0)
    weights = lambda i, te, used: (layer, te[i], 0, 0)
    grid_spec = pltpu.PrefetchScalarGridSpec(
        num_scalar_prefetch=2,
        grid=(N_ROW_TILES,),
        in_specs=[
            pl.BlockSpec((TM, HALF), row),
            pl.BlockSpec((1, 1, D, DE), weights),
            pl.BlockSpec((1, 1, D, DE), weights),
            pl.BlockSpec((1, 1, DE, D), weights),
        ],
        out_specs=pl.BlockSpec((TM, HALF), row),
    )
    return pl.pallas_call(
        _moe_kernel,
        grid_spec=grid_spec,
        out_shape=jax.ShapeDtypeStruct((NPAD, HALF), U32),
        compiler_params=_vmem_limit(40),
        name="moe_experts",
    )(tile_expert, used, xs, w_gate, w_up, w_down)


def _combined(y0_ref, y1_ref, x_ref, wcol_ref, mod_ref):
    sub = lax.broadcasted_iota(I32, (LANES, wcol_ref.shape[1]), 0)
    wt = jnp.where(sub == 0, wcol_ref[0:1, :], jnp.where(sub == 1, wcol_ref[1:2, :], 0.0)).T
    w0 = wt[:, 0:1]
    w1 = wt[:, 1:2]
    hi0, lo0 = _unpack_rows(y0_ref[...])
    hi1, lo1 = _unpack_rows(y1_ref[...])
    y = jnp.concatenate([w0 * hi0 + w1 * hi1, w0 * lo0 + w1 * lo1], axis=1)
    return x_ref[...] + mod_ref[0, 5:6, :] * y


def _combine_kernel(y0_ref, y1_ref, x_ref, wcol_ref, mod_ref, o_ref):
    o_ref[...] = _combined(y0_ref, y1_ref, x_ref, wcol_ref, mod_ref)


def _combine(group, y2g, x, wcol, mod):
    n_g = T_CTX if group == 0 else T_LAT
    tiles = n_g // CT
    first = 0 if group == 0 else T_CTX // CT
    return pl.pallas_call(
        _combine_kernel,
        grid=(tiles,),
        in_specs=[
            pl.BlockSpec((CT, HALF), lambda i: (i, 0)),
            pl.BlockSpec((CT, HALF), lambda i: (tiles + i, 0)),
            pl.BlockSpec((CT, D), lambda i: (first + i, 0)),
            pl.BlockSpec((TOP_K, CT), lambda i: (0, first + i)),
            pl.BlockSpec((1, N_MOD, D), lambda i: (_cond_index(first + i, CT), 0, 0)),
        ],
        out_specs=pl.BlockSpec((CT, D), lambda i: (i, 0)),
        out_shape=jax.ShapeDtypeStruct((n_g, D), F32),
        name="moe_combine",
    )(y2g, y2g, x, wcol, mod)


def _moe(h, e, rank, cnt, layer, w_gate, w_up, w_down):
    pos, tile_expert, used = _plan(e, rank, cnt)
    xs = _sc_dispatch(h, pos)
    ys = _experts(xs, tile_expert, used, layer, w_gate, w_up, w_down)
    return _sc_gather(ys, pos, 0), _sc_gather(ys, pos, 1)


def kernel(x_prompt, x_sample, c, cache_k, cache_v, c_ctx, norm_mix_g, norm_ffn_g, ada_w, ada_b, attn_w_qkv, attn_q_norm, attn_k_norm, attn_w_o, pool_w, pool_scale, moe_w_group, moe_w_expert, moe_w_gate, moe_w_up, moe_w_down):
    xp = x_prompt.reshape(T_CTX, D)
    xs = x_sample.reshape(T_LAT, D)
    cond = jnp.concatenate([c_ctx[None, :], c, jnp.zeros((N_COND - 1 - DEC_BATCH, D), F32)], axis=0)
    mod = _ada(cond, ada_w, ada_b)

    qt, k2, vt, new_k, new_v = _qkv(xp, xs, mod[0], norm_mix_g[0:1], attn_w_qkv[0].astype(BF16),
                                    attn_q_norm[0:1], attn_k_norm[0:1])
    o, w_gate, w_up, w_down = _attention(qt, k2, vt, cache_k[:, 0].reshape(DEC_BATCH, PAST, DKV),
                                         cache_v[:, 0].reshape(DEC_BATCH, PAST, DKV),
                                         moe_w_gate, moe_w_up, moe_w_down)
    wr_hl, wr_hi = _router_weights(moe_w_group[0], moe_w_expert[0])
    x1, h, e, rank, wcol, cnt = _post(o, xp, xs, mod[0], attn_w_o[0].astype(BF16), norm_ffn_g[0:1],
                                      wr_hl, wr_hi)
    y2 = _moe(h, e, rank, cnt, 0, w_gate, w_up, w_down)

    wr_hl, wr_hi = _router_weights(moe_w_group[1], moe_w_expert[1])
    pool_args = (x1, wcol, mod[0], mod[1], norm_mix_g[1:2], pool_w[0].astype(BF16), pool_scale[0:1],
                 norm_ffn_g[1:2], wr_hl, wr_hi)
    outs = _pool(0, y2[0], *pool_args, jnp.zeros((NE, LANES), F32), None)
    x3, h, e, rank, wcol, cnt = _pool(1, y2[1], *pool_args, outs[5], outs[:5])
    y2 = _moe(h, e, rank, cnt, 1, w_gate, w_up, w_down)
    y_prompt = _combine(0, y2[0], x3, wcol, mod[1])
    y_sample = _combine(1, y2[1], x3, wcol, mod[1])

    return (y_prompt.reshape(BATCH, SEQ, D), y_sample.reshape(DEC_BATCH, DEC_SEQ, D),
            new_k.reshape(BATCH, 1, SEQ, N_KV, HD), new_v.reshape(BATCH, 1, SEQ, N_KV, HD))
```

```python
import functools

import jax
import jax.numpy as jnp
import numpy as np
from jax import lax
from jax.experimental import pallas as pl
from jax.experimental.pallas import tpu as pltpu
from jax.experimental.pallas import tpu_sc as plsc

F32 = jnp.float32
BF16 = jnp.bfloat16
I32 = jnp.int32
U32 = jnp.uint32

D = 1024
BATCH, SEQ = 32, 256
DEC_BATCH, DEC_SEQ, PAST = 8, 1024, 512
T_CTX = BATCH * SEQ
T_LAT = DEC_BATCH * DEC_SEQ
T = T_CTX + T_LAT
GRID_W = 64
N_HEADS, N_KV, HD = 16, 4, 64
DQ = N_HEADS * HD
DKV = N_KV * HD
ROPE_THETA = 10000.0
POOL_WINDOWS = (2, 4, 8, 16)
PGD = D // len(POOL_WINDOWS)
N_GROUPS, E_PER_G, TOP_K = 4, 8, 2
NE = N_GROUPS * E_PER_G
DE = D // 4
N_MOD = 6
EPS = 1e-6
N_COND = 16

TILE = 512
NT = T // TILE
NT_CTX = T_CTX // TILE
QB = 256
PTILE = 1024
PB = 128
PBW = 256
PB_HALO = (PBW - PB) // 2
TM = 1024
N_ROW_TILES = (TOP_K * T) // TM + NE
NPAD = N_ROW_TILES * TM
CT = 1024
LANES = 128
NEG_INF = float("-inf")
LOG2_E = 1.4426950408889634
Q_SCALE = HD ** -0.5 * LOG2_E
HALF = D // 2
SC_CORES, SC_SUBCORES = 2, 16
SC_WORKERS = SC_CORES * SC_SUBCORES
SC_WIN = 64


def _vmem_limit(mib):
    return pltpu.CompilerParams(vmem_limit_bytes=mib * 1024 * 1024)


def _modulate(x, gain_scale, shift):
    ms = jnp.mean(x * x, axis=-1, keepdims=True)
    return (x * lax.rsqrt(ms + EPS)) * gain_scale + shift


def _head_norm(z, gain, ind, ind_t2):
    ss = jnp.dot((z * z).astype(BF16), ind, preferred_element_type=F32)
    inv = lax.rsqrt(ss * (1.0 / HD) + EPS)
    inv_hi = inv.astype(BF16)
    inv_lo = (inv - inv_hi.astype(F32)).astype(BF16)
    scale = jnp.dot(jnp.concatenate([inv_hi, inv_lo], axis=1), ind_t2, preferred_element_type=F32)
    return z * scale * gain


def _rope(z, cos_t, sin_t):
    lane = lax.broadcasted_iota(I32, (z.shape[0], LANES), 1)
    low = (lane % 32) < 16
    outs = []
    for c in range(z.shape[1] // LANES):
        zc = z[:, c * LANES:(c + 1) * LANES]
        up = pltpu.roll(zc, 16, axis=1)
        dn = pltpu.roll(zc, LANES - 16, axis=1)
        outs.append(zc * cos_t + jnp.where(low, dn, up) * sin_t)
    return jnp.concatenate(outs, axis=1)


def _rope_t(zt, cos_tt, sin_tt):
    sub = lax.broadcasted_iota(I32, (LANES, zt.shape[1]), 0)
    low = (sub % 32) < 16
    outs = []
    for c in range(zt.shape[0] // LANES):
        zc = zt[c * LANES:(c + 1) * LANES]
        up = pltpu.roll(zc, 16, axis=0)
        dn = pltpu.roll(zc, LANES - 16, axis=0)
        outs.append(zc * cos_tt + jnp.where(low, dn, up) * sin_tt)
    return jnp.concatenate(outs, axis=0)


def _pack_rows(z):
    bits = lax.bitcast_convert_type(z.astype(BF16).astype(F32), U32)
    return bits[:, :HALF] | (bits[:, HALF:] >> 16)


def _unpack_rows(p):
    hi = lax.bitcast_convert_type(p & jnp.uint32(0xFFFF0000), F32)
    lo = lax.bitcast_convert_type(p << 16, F32)
    return hi, lo


def _ada_kernel(c_ref, w_ref, b_ref, o_ref):
    c = c_ref[...]
    a = c * (1.0 / (1.0 + jnp.exp(-c)))
    o_ref[0] = jnp.dot(a.astype(BF16), w_ref[0].astype(BF16), preferred_element_type=F32) + b_ref[0]


def _ada(cond, ada_w, ada_b):
    nb = 1536
    depth = ada_w.shape[0]
    out = pl.pallas_call(
        _ada_kernel,
        grid=(depth, (N_MOD * D) // nb),
        in_specs=[
            pl.BlockSpec((N_COND, D), lambda l, j: (0, 0)),
            pl.BlockSpec((1, D, nb), lambda l, j: (l, 0, j)),
            pl.BlockSpec((1, 1, nb), lambda l, j: (l, 0, j)),
        ],
        out_specs=pl.BlockSpec((1, N_COND, nb), lambda l, j: (l, 0, j)),
        out_shape=jax.ShapeDtypeStruct((depth, N_COND, N_MOD * D), F32),
        compiler_params=_vmem_limit(40),
        name="ada",
    )(cond, ada_w, ada_b.reshape(depth, 1, N_MOD * D))
    return out.reshape(depth, N_COND, N_MOD, D)


def _cond_index(i, tile):
    n_ctx = T_CTX // tile
    per_batch = DEC_SEQ // tile
    return jnp.where(i < n_ctx, 0, 1 + (i - n_ctx) // per_batch)


def _qkv_kernel(xp_ref, xs_ref, mod_ref, g_ref, w_ref, qg_ref, kg_ref, indq_ref, indqt_ref,
                indk_ref, indkt_ref, cos_ref, sin_ref, cos_t_ref, sin_t_ref,
                qt_ref, k2_ref, vt_ref, nk_ref, nv_ref):
    i = pl.program_id(0)
    is_lat = i >= NT_CTX
    gain_scale = g_ref[...] * (1.0 + mod_ref[0, 1:2, :])
    subs = [slice(b * QB, (b + 1) * QB) for b in range(TILE // QB)]
    hs = [_modulate(jnp.where(is_lat, xs_ref[r, :], xp_ref[r, :]), gain_scale, mod_ref[0, 0:1, :]).astype(BF16)
          for r in subs]
    qkvs = [jnp.dot(h, w_ref[...], preferred_element_type=F32) for h in hs]
    qs = [_head_norm(z[:, :DQ], qg_ref[...], indq_ref[...], indqt_ref[...]) for z in qkvs]
    ks = [_head_norm(z[:, DQ:DQ + DKV], kg_ref[...], indk_ref[...], indkt_ref[...]) for z in qkvs]
    vs = [z[:, DQ + DKV:] for z in qkvs]
    qts = [(q * Q_SCALE).T for q in qs]
    for r, v in zip(subs, vs):
        vt_ref[:, r] = v.T.astype(BF16)

    @pl.when(is_lat)
    def _():
        for b, r in enumerate(subs):
            qt_ref[b] = _rope_t(qts[b], cos_t_ref[:, r], sin_t_ref[:, r]).astype(BF16)
            k2_ref[r, :] = _dup_heads(_rope(ks[b], cos_ref[r, :], sin_ref[r, :])).astype(BF16)

    @pl.when(jnp.logical_not(is_lat))
    def _():
        for b, r in enumerate(subs):
            qt_ref[b] = qts[b].astype(BF16)
            k2_ref[r, :] = _dup_heads(ks[b]).astype(BF16)
            nk_ref[r, :] = ks[b]
            nv_ref[r, :] = vs[b]


def _rope_tables():
    rows = DEC_SEQ // GRID_W
    row = np.broadcast_to(np.arange(rows, dtype=np.float32)[:, None], (rows, GRID_W)).reshape(-1)
    col = np.broadcast_to(np.arange(GRID_W, dtype=np.float32)[None, :], (rows, GRID_W)).reshape(-1)
    axis_dim = HD // 2
    inv_freq = np.power(np.float32(ROPE_THETA),
                        -np.arange(0, axis_dim, 2, dtype=np.float32) / np.float32(axis_dim))
    ang = np.concatenate([row[:, None] * inv_freq, col[:, None] * inv_freq], axis=-1).astype(np.float32)
    cos, sin = np.cos(ang), np.sin(ang)
    quarter = HD // 4
    cos_h = np.concatenate([cos[:, :quarter], cos[:, :quarter], cos[:, quarter:], cos[:, quarter:]], axis=1)
    sin_h = np.concatenate([-sin[:, :quarter], sin[:, :quarter], -sin[:, quarter:], sin[:, quarter:]], axis=1)
    return (np.tile(cos_h, (1, LANES // HD)).astype(np.float32),
            np.tile(sin_h, (1, LANES // HD)).astype(np.float32))


def _head_indicators(width):
    col = np.arange(width)[:, None]
    head = np.arange(LANES)[None, :]
    ind = (col // HD == head).astype(np.float32)
    return jnp.asarray(ind, BF16), jnp.asarray(np.concatenate([ind.T, ind.T], axis=0), BF16)


def _dup_heads(z):
    lane = lax.broadcasted_iota(I32, (z.shape[0], LANES), 1)
    cols = []
    for p in range(N_KV // 2):
        blk = z[:, p * LANES:(p + 1) * LANES]
        swp = pltpu.roll(blk, HD, axis=1)
        cols += [jnp.where(lane < HD, blk, swp), jnp.where(lane < HD, swp, blk)]
    return jnp.concatenate(cols, axis=1)


def _qkv(xp, xs, mod, g, w_qkv, q_gain, k_gain):
    indq, indqt = _head_indicators(DQ)
    indk, indkt = _head_indicators(DKV)
    cos_t, sin_t = _rope_tables()
    per_batch = DEC_SEQ // TILE
    const = lambda shape: pl.BlockSpec(shape, lambda i: (0,) * len(shape))
    return pl.pallas_call(
        _qkv_kernel,
        grid=(NT,),
        in_specs=[
            pl.BlockSpec((TILE, D), lambda i: (jnp.minimum(i, NT_CTX - 1), 0)),
            pl.BlockSpec((TILE, D), lambda i: (jnp.maximum(i - NT_CTX, 0), 0)),
            pl.BlockSpec((1, N_MOD, D), lambda i: (_cond_index(i, TILE), 0, 0)),
            const((1, D)),
            const((D, DQ + 2 * DKV)),
            const((1, DQ)), const((1, DKV)),
            const((DQ, LANES)), const((2 * LANES, DQ)),
            const((DKV, LANES)), const((2 * LANES, DKV)),
            pl.BlockSpec((TILE, LANES), lambda i: (jnp.maximum(i - NT_CTX, 0) % per_batch, 0)),
            pl.BlockSpec((TILE, LANES), lambda i: (jnp.maximum(i - NT_CTX, 0) % per_batch, 0)),
            pl.BlockSpec((LANES, TILE), lambda i: (0, jnp.maximum(i - NT_CTX, 0) % per_batch)),
            pl.BlockSpec((LANES, TILE), lambda i: (0, jnp.maximum(i - NT_CTX, 0) % per_batch)),
        ],
        out_specs=[
            pl.BlockSpec((TILE // QB, DQ, QB), lambda i: (i, 0, 0)),
            pl.BlockSpec((TILE, 2 * DKV), lambda i: (i, 0)),
            pl.BlockSpec((DKV, TILE), lambda i: (0, i)),
            pl.BlockSpec((TILE, DKV), lambda i: (jnp.minimum(i, NT_CTX - 1), 0)),
            pl.BlockSpec((TILE, DKV), lambda i: (jnp.minimum(i, NT_CTX - 1), 0)),
        ],
        out_shape=[
            jax.ShapeDtypeStruct((T // QB, DQ, QB), BF16),
            jax.ShapeDtypeStruct((T, 2 * DKV), BF16),
            jax.ShapeDtypeStruct((DKV, T), BF16),
            jax.ShapeDtypeStruct((T_CTX, DKV), F32),
            jax.ShapeDtypeStruct((T_CTX, DKV), F32),
        ],
        compiler_params=_vmem_limit(56),
        name="qkv",
    )(xp, xs, mod, g, w_qkv, jnp.tile(q_gain, (1, N_HEADS)), jnp.tile(k_gain, (1, N_KV)),
      indq, indqt, indk, indkt, cos_t, sin_t, cos_t.T, sin_t.T)


def _unit_scores(u, pairs, k2_parts, qt_ref):
    cols = []
    for j in range(pairs):
        pair = u * pairs + j
        qt_pair = qt_ref[pair * LANES:(pair + 1) * LANES, :]
        sub = lax.broadcasted_iota(I32, qt_pair.shape, 0)
        zero = jnp.zeros_like(qt_pair)
        cols += [jnp.where(sub < HD, qt_pair, zero), jnp.where(sub >= HD, qt_pair, zero)]
    qtu = jnp.concatenate(cols, axis=1)
    return [jnp.dot(k2, qtu, preferred_element_type=F32) for k2 in k2_parts]


def _softmax_values(ss, vt_parts):
    m = ss[0].max(axis=0, keepdims=True)
    for s in ss[1:]:
        m = jnp.maximum(m, s.max(axis=0, keepdims=True))
    acc = None
    l = None
    for s, vt in zip(ss, vt_parts):
        e = jnp.exp2(s - m)
        ls = e.sum(axis=0, keepdims=True)
        l = ls if l is None else l + ls
        t = jnp.dot(vt, e.astype(BF16), preferred_element_type=F32)
        acc = t if acc is None else acc + t
    return acc * (1.0 / l)


def _attend(keys_of, values_of, qt_ref, o_ref, pairs, ahead):
    n_units = N_HEADS // (2 * pairs)
    group = lambda u: (u * 2 * pairs) // (N_HEADS // N_KV)
    scores = lambda u: _unit_scores(u, pairs, keys_of(group(u)), qt_ref)
    pending = [scores(u) for u in range(min(ahead, n_units))]
    outs = []
    for u in range(n_units):
        if u + ahead < n_units:
            pending.append(scores(u + ahead))
        outs.append(_softmax_values(pending.pop(0), values_of(group(u))))
    lq = o_ref.shape[0]
    heads = [o[:, h * lq:(h + 1) * lq] for o in outs for h in range(2 * pairs)]
    o_ref[...] = jnp.concatenate(heads, axis=0).T.astype(BF16)


def _attn_ctx_kernel(qt_ref, k2_ref, vt_ref, o_ref):
    _attend(lambda g: [k2_ref[:, g * LANES:(g + 1) * LANES]],
            lambda g: [vt_ref[g * HD:(g + 1) * HD, :]], qt_ref.at[0], o_ref, pairs=2, ahead=1)


def _attn_lat_kernel(qt_ref, k2c_ref, vtc_ref, k2n_ref, vtn_ref, wg_ref, wu_ref, wd_ref, o_in_ref,
                     o_ref, wgb_ref, wub_ref, wdb_ref):
    del o_in_ref
    wgb_ref[...] = wg_ref[...].astype(BF16)
    wub_ref[...] = wu_ref[...].astype(BF16)
    wdb_ref[...] = wd_ref[...].astype(BF16)
    _attend(lambda g: [k2c_ref[0, :, g * LANES:(g + 1) * LANES], k2n_ref[:, g * LANES:(g + 1) * LANES]],
            lambda g: [vtc_ref[0, g * HD:(g + 1) * HD, :], vtn_ref[g * HD:(g + 1) * HD, :]], qt_ref.at[0],
            o_ref, pairs=1, ahead=2)


def _cache_kernel(ck_ref, cv_ref, k2_ref, vt_ref):
    k2_ref[0] = _dup_heads(ck_ref[0, 0].reshape(PAST, DKV)).astype(BF16)
    vt_ref[0] = cv_ref[0, 0].reshape(PAST, DKV).T.astype(BF16)


def _attention(qt, k2, vt, cache_k, cache_v, w_gate, w_up, w_down):
    k2c, vtc = pl.pallas_call(
        _cache_kernel,
        grid=(DEC_BATCH,),
        in_specs=[
            pl.BlockSpec((1, 1, PAST, N_KV, HD), lambda b: (b, 0, 0, 0, 0)),
            pl.BlockSpec((1, 1, PAST, N_KV, HD), lambda b: (b, 0, 0, 0, 0)),
        ],
        out_specs=[
            pl.BlockSpec((1, PAST, 2 * DKV), lambda b: (b, 0, 0)),
            pl.BlockSpec((1, DKV, PAST), lambda b: (b, 0, 0)),
        ],
        out_shape=[
            jax.ShapeDtypeStruct((DEC_BATCH, PAST, 2 * DKV), BF16),
            jax.ShapeDtypeStruct((DEC_BATCH, DKV, PAST), BF16),
        ],
        name="cache_prep",
    )(cache_k, cache_v)

    o_ctx = pl.pallas_call(
        _attn_ctx_kernel,
        grid=(BATCH,),
        in_specs=[
            pl.BlockSpec((1, DQ, QB), lambda b: (b, 0, 0)),
            pl.BlockSpec((SEQ, 2 * DKV), lambda b: (b, 0)),
            pl.BlockSpec((DKV, SEQ), lambda b: (0, b)),
        ],
        out_specs=pl.BlockSpec((SEQ, DQ), lambda b: (b, 0)),
        out_shape=jax.ShapeDtypeStruct((T, DQ), BF16),
        name="attn_ctx",
    )(qt, k2, vt)

    qb = QB
    nqb = DEC_SEQ // qb
    lat0 = T_CTX // qb
    depth = w_gate.shape[0]
    eps = (depth * NE) // (DEC_BATCH * nqb)
    per_layer = NE // eps
    wblock = lambda b, j: ((b * nqb + j) // per_layer, (b * nqb + j) % per_layer, 0, 0)
    return pl.pallas_call(
        _attn_lat_kernel,
        grid=(DEC_BATCH, nqb),
        in_specs=[
            pl.BlockSpec((1, DQ, QB), lambda b, j: (lat0 + b * nqb + j, 0, 0)),
            pl.BlockSpec((1, PAST, 2 * DKV), lambda b, j: (b, 0, 0)),
            pl.BlockSpec((1, DKV, PAST), lambda b, j: (b, 0, 0)),
            pl.BlockSpec((DEC_SEQ, 2 * DKV), lambda b, j: (T_CTX // DEC_SEQ + b, 0)),
            pl.BlockSpec((DKV, DEC_SEQ), lambda b, j: (0, T_CTX // DEC_SEQ + b)),
            pl.BlockSpec((1, eps, D, DE), wblock),
            pl.BlockSpec((1, eps, D, DE), wblock),
            pl.BlockSpec((1, eps, DE, D), wblock),
            pl.BlockSpec(memory_space=pl.ANY),
        ],
        out_specs=[
            pl.BlockSpec((qb, DQ), lambda b, j: (lat0 + b * nqb + j, 0)),
            pl.BlockSpec((1, eps, D, DE), wblock),
            pl.BlockSpec((1, eps, D, DE), wblock),
            pl.BlockSpec((1, eps, DE, D), wblock),
        ],
        out_shape=[
            jax.ShapeDtypeStruct((T, DQ), BF16),
            jax.ShapeDtypeStruct(w_gate.shape, BF16),
            jax.ShapeDtypeStruct(w_up.shape, BF16),
            jax.ShapeDtypeStruct(w_down.shape, BF16),
        ],
        input_output_aliases={8: 0},
        compiler_params=_vmem_limit(56),
        name="attn_lat",
    )(qt, k2c, vtc, k2, vt, w_gate, w_up, w_down, o_ctx)


def _route(h, wr_hl_ref, wr_hi_ref, tri_ref, cnt_ref, e_ref, rank_ref, wcol_ref, cnt_out_ref):
    tl = h.shape[0]
    h_hi = h.astype(BF16)
    h_lo = (h - h_hi.astype(F32)).astype(BF16)
    lg2 = jnp.dot(h_hi, wr_hl_ref[...], preferred_element_type=F32)
    lg = lg2[:, :LANES] + lg2[:, LANES:] + jnp.dot(h_lo, wr_hi_ref[...], preferred_element_type=F32)
    lt = lg.T

    sub8 = lax.broadcasted_iota(I32, (8, tl), 0).astype(F32)
    gl = jnp.where(sub8 < N_GROUPS, lt[0:8], NEG_INF)
    gmax = gl.max(axis=0, keepdims=True)
    gidx = jnp.min(jnp.where(gl == gmax, sub8, 8.0), axis=0, keepdims=True)
    gsum = jnp.sum(jnp.exp(gl - gmax), axis=0, keepdims=True)
    sel = lt[8:16]
    for g in range(1, N_GROUPS):
        sel = jnp.where(gidx == g, lt[8 + 8 * g:16 + 8 * g], sel)
    t1 = sel.max(axis=0, keepdims=True)
    j1 = jnp.min(jnp.where(sel == t1, sub8, 8.0), axis=0, keepdims=True)
    sel2 = jnp.where(sub8 == j1, NEG_INF, sel)
    t2 = sel2.max(axis=0, keepdims=True)
    j2 = jnp.min(jnp.where(sel2 == t2, sub8, 8.0), axis=0, keepdims=True)
    ex = jnp.exp(t2 - t1)
    den = 1.0 + ex
    gw = 1.0 / gsum
    w0 = gw * (1.0 / den)
    w1 = gw * (ex / den)
    e0 = gidx * E_PER_G + j1
    e1 = gidx * E_PER_G + j2

    sub_e = lax.broadcasted_iota(I32, (NE, tl), 0).astype(F32)
    oh0 = (sub_e == e0).astype(F32)
    oh1 = (sub_e == e1).astype(F32)
    c = oh0 + oh1
    carry = cnt_ref[:, 0:1]
    pieces = []
    for ch in range(tl // 256):
        cc = c[:, ch * 256:(ch + 1) * 256]
        pieces.append(jnp.dot(cc.astype(BF16), tri_ref[...], preferred_element_type=F32) + carry)
        carry = carry + jnp.sum(cc, axis=1, keepdims=True)
    csum = jnp.concatenate(pieces, axis=1)
    cnt_ref[...] = jnp.broadcast_to(carry, (NE, LANES))
    cnt_out_ref[...] = jnp.broadcast_to(carry, (NE, LANES))
    r0 = jnp.sum(oh0 * csum, axis=0, keepdims=True) - 1.0
    r1 = jnp.sum(oh1 * csum, axis=0, keepdims=True) - 1.0
    e_ref[0:1, :] = e0.astype(I32)
    e_ref[1:2, :] = e1.astype(I32)
    rank_ref[0:1, :] = r0.astype(I32)
    rank_ref[1:2, :] = r1.astype(I32)
    wcol_ref[0:1, :] = w0
    wcol_ref[1:2, :] = w1


def _post_kernel(o_ref, xp_ref, xs_ref, mod_ref, wo_ref, g_ref, wr_hl_ref, wr_hi_ref, tri_ref,
                 x1_ref, h_ref, e_ref, rank_ref, wcol_ref, cnt_out_ref, cnt_ref):
    i = pl.program_id(0)

    @pl.when(i == 0)
    def _():
        cnt_ref[...] = jnp.zeros_like(cnt_ref)

    x = jnp.where(i >= NT_CTX, xs_ref[...], xp_ref[...])
    m = jnp.dot(o_ref[...], wo_ref[...], preferred_element_type=F32)
    x1 = x + mod_ref[0, 2:3, :] * m
    x1_ref[...] = x1
    h = _modulate(x1, g_ref[...] * (1.0 + mod_ref[0, 4:5, :]), mod_ref[0, 3:4, :])
    h_ref[...] = _pack_rows(h)
    _route(h, wr_hl_ref, wr_hi_ref, tri_ref, cnt_ref, e_ref, rank_ref, wcol_ref, cnt_out_ref)


def _router_weights(w_group, w_expert):
    w = jnp.concatenate([w_group, jnp.zeros((D, 8 - N_GROUPS), F32),
                         jnp.transpose(w_expert, (1, 0, 2)).reshape(D, NE),
                         jnp.zeros((D, LANES - 8 - NE), F32)], axis=1)
    w_hi = w.astype(BF16)
    w_lo = (w - w_hi.astype(F32)).astype(BF16)
    return jnp.concatenate([w_hi, w_lo], axis=1), w_hi


def _route_outs(tile, first=0):
    specs = [
        pl.BlockSpec((tile, D), lambda i: (first + i, 0)),
        pl.BlockSpec((tile, HALF), lambda i: (first + i, 0)),
        pl.BlockSpec((TOP_K, tile), lambda i: (0, first + i)),
        pl.BlockSpec((TOP_K, tile), lambda i: (0, first + i)),
        pl.BlockSpec((TOP_K, tile), lambda i: (0, first + i)),
        pl.BlockSpec((NE, LANES), lambda i: (0, 0)),
    ]
    shapes = [
        jax.ShapeDtypeStruct((T, D), F32),
        jax.ShapeDtypeStruct((T, HALF), U32),
        jax.ShapeDtypeStruct((TOP_K, T), I32),
        jax.ShapeDtypeStruct((TOP_K, T), I32),
        jax.ShapeDtypeStruct((TOP_K, T), F32),
        jax.ShapeDtypeStruct((NE, LANES), F32),
    ]
    return specs, shapes


def _tri():
    a = np.arange(256)
    return jnp.asarray((a[:, None] <= a[None, :]).astype(np.float32), BF16)


def _post(o, xp, xs, mod, w_o, g_ffn, wr_hl, wr_hi):
    const = lambda shape: pl.BlockSpec(shape, lambda i: (0,) * len(shape))
    out_specs, out_shape = _route_outs(TILE)
    return pl.pallas_call(
        _post_kernel,
        grid=(NT,),
        in_specs=[
            pl.BlockSpec((TILE, DQ), lambda i: (i, 0)),
            pl.BlockSpec((TILE, D), lambda i: (jnp.minimum(i, NT_CTX - 1), 0)),
            pl.BlockSpec((TILE, D), lambda i: (jnp.maximum(i - NT_CTX, 0), 0)),
            pl.BlockSpec((1, N_MOD, D), lambda i: (_cond_index(i, TILE), 0, 0)),
            const((DQ, D)), const((1, D)), const((D, 2 * LANES)), const((D, LANES)), const((256, 256)),
        ],
        out_specs=out_specs,
        out_shape=out_shape,
        scratch_shapes=[pltpu.VMEM((NE, LANES), F32)],
        compiler_params=_vmem_limit(56),
        name="post_attn",
    )(o, xp, xs, mod, w_o, g_ffn, wr_hl, wr_hi, _tri())


def _pool_kernel(block_kinds, n_alias, y0_ref, y1_ref, xin_ref, wcin_ref, modp_ref, mod_ref, g_ref, pw_ref,
                 ps_ref, gf_ref, wr_hl_ref, wr_hi_ref, tri_ref, band_ref, inv_cnt_ref, cnt_in_ref, *refs):
    (x1_ref, h_ref, e_ref, rank_ref, wcol_ref, cnt_out_ref, cnt_ref, hhi_ref, hlo_ref) = refs[n_alias:]
    i = pl.program_id(0)

    @pl.when(i == 0)
    def _():
        cnt_ref[...] = cnt_in_ref[...]

    x = _combined(y0_ref, y1_ref, xin_ref, wcin_ref, modp_ref)
    h = _modulate(x, g_ref[...] * (1.0 + mod_ref[0, 1:2, :]), mod_ref[0, 0:1, :])
    @pl.when(i == 0)
    def _():
        zeros = jnp.zeros((PB_HALO, D), BF16)
        for ref in (hhi_ref, hlo_ref):
            ref[0:PB_HALO, :] = zeros
            ref[PB_HALO + PTILE:PB_HALO + PTILE + PB_HALO, :] = zeros

    h_hi = h.astype(BF16)
    hhi_ref[PB_HALO:PB_HALO + PTILE, :] = h_hi
    hlo_ref[PB_HALO:PB_HALO + PTILE, :] = (h - h_hi.astype(F32)).astype(BF16)

    outs = []
    for gi in range(len(POOL_WINDOWS)):
        cols = slice(gi * PGD, (gi + 1) * PGD)
        sums = []
        for b in range(PTILE // PB):
            bm = band_ref[block_kinds[b], gi]
            sums.append(jnp.dot(bm, hhi_ref[b * PB:b * PB + PBW, cols], preferred_element_type=F32)
                        + jnp.dot(bm, hlo_ref[b * PB:b * PB + PBW, cols], preferred_element_type=F32))
        diff = jnp.concatenate(sums, axis=0) * inv_cnt_ref[:, gi:gi + 1] - h[:, cols]
        outs.append(jnp.dot(diff.astype(BF16), pw_ref[gi], preferred_element_type=F32))
    m = jnp.concatenate(outs, axis=1) * ps_ref[...]
    x1 = x + mod_ref[0, 2:3, :] * m
    x1_ref[...] = x1
    hf = _modulate(x1, gf_ref[...] * (1.0 + mod_ref[0, 4:5, :]), mod_ref[0, 3:4, :])
    h_ref[...] = _pack_rows(hf)
    _route(hf, wr_hl_ref, wr_hi_ref, tri_ref, cnt_ref, e_ref, rank_ref, wcol_ref, cnt_out_ref)


def _pool_constants(seq):
    r = np.arange(PB)[:, None]
    d = np.arange(PBW)[None, :] - PB_HALO - r
    found, kinds = {}, []
    for b in range(PTILE // PB):
        tm = (b * PB + r) % seq
        inseq = (tm + d >= 0) & (tm + d < seq)
        mats = np.stack([(inseq & (d >= -(w // 2)) & (d <= w - w // 2 - 1)).astype(np.float32)
                         for w in POOL_WINDOWS])
        kinds.append(found.setdefault(mats.tobytes(), (len(found), mats))[0])
    bands = np.stack([m for _, m in sorted(found.values(), key=lambda kv: kv[0])])
    tm = np.arange(PTILE) % seq
    inv_cnt = np.zeros((PTILE, LANES), np.float32)
    for g, w in enumerate(POOL_WINDOWS):
        left, right = w // 2, w - w // 2 - 1
        inv_cnt[:, g] = 1.0 / (np.minimum(tm + right + 1, seq) - np.maximum(tm - left, 0)).astype(np.float32)
    return jnp.asarray(bands, BF16), tuple(kinds), inv_cnt


def _pool(group, y2g, x, wcol, mod_prev, mod, g_mix, pool_w, pool_scale, g_ffn, wr_hl, wr_hi, cnt_in, prev):
    const = lambda shape: pl.BlockSpec(shape, lambda i: (0,) * len(shape))
    tiles = (T_CTX if group == 0 else T_LAT) // PTILE
    first = 0 if group == 0 else T_CTX // PTILE
    out_specs, out_shape = _route_outs(PTILE, first)
    aliases = () if prev is None else tuple(prev)
    bands, block_kinds, inv_cnt = _pool_constants(SEQ if group == 0 else DEC_SEQ)
    n_fixed = 16
    return pl.pallas_call(
        functools.partial(_pool_kernel, block_kinds, len(aliases)),
        grid=(tiles,),
        in_specs=[
            pl.BlockSpec((PTILE, HALF), lambda i: (i, 0)),
            pl.BlockSpec((PTILE, HALF), lambda i: (tiles + i, 0)),
            pl.BlockSpec((PTILE, D), lambda i: (first + i, 0)),
            pl.BlockSpec((TOP_K, PTILE), lambda i: (0, first + i)),
            pl.BlockSpec((1, N_MOD, D), lambda i: (_cond_index(first + i, PTILE), 0, 0)),
            pl.BlockSpec((1, N_MOD, D), lambda i: (_cond_index(first + i, PTILE), 0, 0)),
            const((1, D)), const((len(POOL_WINDOWS), PGD, PGD)), const((1, D)), const((1, D)),
            const((D, 2 * LANES)), const((D, LANES)), const((256, 256)), const(bands.shape),
            const((PTILE, LANES)), const((NE, LANES)),
        ] + [pl.BlockSpec(memory_space=pl.ANY)] * len(aliases),
        out_specs=out_specs,
        out_shape=out_shape,
        input_output_aliases={n_fixed + k: k for k in range(len(aliases))},
        scratch_shapes=[pltpu.VMEM((NE, LANES), F32), pltpu.VMEM((PTILE + 2 * PB_HALO, D), BF16),
                        pltpu.VMEM((PTILE + 2 * PB_HALO, D), BF16)],
        compiler_params=_vmem_limit(56),
        name="pool_mixer",
    )(y2g, y2g, x, wcol, mod_prev, mod, g_mix, pool_w, pool_scale, g_ffn, wr_hl, wr_hi, _tri(), bands, inv_cnt,
      cnt_in, *aliases)


def _plan_kernel(e_ref, rank_ref, cnt_ref, pos_ref, te_ref, used_ref):
    counts = cnt_ref[:, 0:1]
    tiles_col = jnp.floor((counts + (TM - 1)) * (1.0 / TM))
    sub = lax.broadcasted_iota(I32, (NE, NE), 0)
    lane = lax.broadcasted_iota(I32, (NE, NE), 1)
    tiles_row = jnp.sum(jnp.where(sub == lane, tiles_col, 0.0), axis=0, keepdims=True)
    first_col = jnp.sum(jnp.where(lane < sub, tiles_row, 0.0), axis=1, keepdims=True)
    end_col = first_col + tiles_col
    used = jnp.sum(tiles_col, axis=0, keepdims=True)
    base_col = first_col * TM
    sub_e = lax.broadcasted_iota(I32, (NE, T), 0)
    for k in range(TOP_K):
        hit = sub_e == e_ref[k:k + 1, :]
        base = jnp.sum(jnp.where(hit, base_col, 0.0), axis=0, keepdims=True)
        pos_ref[k:k + 1, :] = base.astype(I32) + rank_ref[k:k + 1, :]
    tile = jnp.minimum(lax.broadcasted_iota(I32, (NE, LANES), 1).astype(F32), used - 1.0)
    te_ref[...] = jnp.sum(jnp.where(tile >= end_col, 1.0, 0.0), axis=0, keepdims=True).astype(I32)
    used_ref[...] = jnp.broadcast_to(used, (1, LANES)).astype(I32)


def _plan(e, rank, cnt):
    assert N_ROW_TILES <= LANES
    pos, te, used = pl.pallas_call(
        _plan_kernel,
        out_shape=[jax.ShapeDtypeStruct((TOP_K, T), I32), jax.ShapeDtypeStruct((1, LANES), I32),
                   jax.ShapeDtypeStruct((1, LANES), I32)],
        name="moe_plan",
    )(e, rank, cnt)
    return pos, te[0, :N_ROW_TILES], used[0, :1]


def _sc_mesh():
    return plsc.VectorSubcoreMesh(core_axis_name="core", subcore_axis_name="subcore")


def _sc_worker():
    return lax.axis_index("core") * SC_SUBCORES + lax.axis_index("subcore")


def _sc_dispatch(h, pos):
    per = T // SC_WORKERS
    nwin = per // SC_WIN

    @functools.partial(
        pl.kernel, out_type=jax.ShapeDtypeStruct((NPAD, HALF), U32), mesh=_sc_mesh(),
        scratch_types=[pltpu.VMEM((2, SC_WIN, HALF), U32), pltpu.VMEM((2, SC_WIN), I32),
                       pltpu.VMEM((2, SC_WIN), I32), pltpu.SemaphoreType.DMA((2,)),
                       pltpu.SemaphoreType.DMA((2,))])
    def run(h_hbm, pos_hbm, xs_hbm, buf, idx0, idx1, sem_in, sem_out):
        first = _sc_worker() * per

        def loads(j, s):
            base = first + j * SC_WIN
            return (pltpu.make_async_copy(h_hbm.at[pl.ds(base, SC_WIN)], buf.at[s], sem_in.at[s]),
                    pltpu.make_async_copy(pos_hbm.at[pl.ds(base, SC_WIN)], idx0.at[s], sem_in.at[s]),
                    pltpu.make_async_copy(pos_hbm.at[pl.ds(T + base, SC_WIN)], idx1.at[s], sem_in.at[s]))

        def stores(s):
            return (pltpu.make_async_copy(buf.at[s], xs_hbm.at[idx0.at[s]], sem_out.at[s]),
                    pltpu.make_async_copy(buf.at[s], xs_hbm.at[idx1.at[s]], sem_out.at[s]))

        for c in loads(0, 0):
            c.start()

        @pl.loop(0, nwin // 2)
        def _(jj):
            for s in range(2):
                j = jj * 2 + s
                for c in loads(j, s):
                    c.wait()
                for c in stores(s):
                    c.start()

                @pl.when(j >= 1)
                def _():
                    for c in stores(1 - s):
                        c.wait()

                @pl.when(j + 1 < nwin)
                def _():
                    for c in loads(j + 1, 1 - s):
                        c.start()

        for c in stores((nwin - 1) % 2):
            c.wait()

    return run(h, pos.reshape(TOP_K * T))


def _sc_gather(ys, pos, group):
    t0, n_g = (0, T_CTX) if group == 0 else (T_CTX, T_LAT)
    idx_all = jnp.concatenate([pos[k, t0:t0 + n_g] for k in range(TOP_K)])
    n = TOP_K * n_g
    per = n // SC_WORKERS
    nwin = per // SC_WIN

    @functools.partial(
        pl.kernel, out_type=jax.ShapeDtypeStruct((n, HALF), U32), mesh=_sc_mesh(),
        scratch_types=[pltpu.VMEM((2, SC_WIN, HALF), U32), pltpu.VMEM((per,), I32),
                       pltpu.SemaphoreType.DMA((2,)), pltpu.SemaphoreType.DMA((2,))])
    def run(ys_hbm, pos_hbm, out_hbm, buf, idx, sem_in, sem_out):
        first = _sc_worker() * per
        pltpu.sync_copy(pos_hbm.at[pl.ds(first, per)], idx)

        def load(j, s):
            return pltpu.make_async_copy(ys_hbm.at[idx.at[pl.ds(j * SC_WIN, SC_WIN)]], buf.at[s], sem_in.at[s])

        def store(j, s):
            return pltpu.make_async_copy(buf.at[s], out_hbm.at[pl.ds(first + j * SC_WIN, SC_WIN)], sem_out.at[s])

        load(0, 0).start()

        @pl.loop(0, nwin // 2)
        def _(jj):
            for s in range(2):
                j = jj * 2 + s
                load(j, s).wait()
                store(j, s).start()

                @pl.when(j >= 1)
                def _():
                    store(j - 1, 1 - s).wait()

                @pl.when(j + 1 < nwin)
                def _():
                    load(j + 1, 1 - s).start()

        store(nwin - 1, (nwin - 1) % 2).wait()

    return run(ys, idx_all)


def _moe_kernel(te_ref, used_ref, x_ref, wg_ref, wu_ref, wd_ref, y_ref):
    del te_ref
    i = pl.program_id(0)

    @pl.when(i < used_ref[0])
    def _():
        hi, lo = _unpack_rows(x_ref[...])
        x = jnp.concatenate([hi, lo], axis=1).astype(BF16)
        a = jnp.dot(x, wg_ref[0, 0], preferred_element_type=F32)
        b = jnp.dot(x, wu_ref[0, 0], preferred_element_type=F32)
        hid = a * (1.0 / (1.0 + jnp.exp(-a))) * b
        y_ref[...] = _pack_rows(jnp.dot(hid.astype(BF16), wd_ref[0, 0], preferred_element_type=F32))


def _experts(xs, tile_expert, used, layer, w_gate, w_up, w_down):
    row = lambda i, te, used: (jnp.minimum(i, used[0] - 1), 0)
    weights = lambda i, te, used: (layer, te[i], 0, 0)
    grid_spec = pltpu.PrefetchScalarGridSpec(
        num_scalar_prefetch=2,
        grid=(N_ROW_TILES,),
        in_specs=[
            pl.BlockSpec((TM, HALF), row),
            pl.BlockSpec((1, 1, D, DE), weights),
            pl.BlockSpec((1, 1, D, DE), weights),
            pl.BlockSpec((1, 1, DE, D), weights),
        ],
        out_specs=pl.BlockSpec((TM, HALF), row),
    )
    return pl.pallas_call(
        _moe_kernel,
        grid_spec=grid_spec,
        out_shape=jax.ShapeDtypeStruct((NPAD, HALF), U32),
        compiler_params=_vmem_limit(40),
        name="moe_experts",
    )(tile_expert, used, xs, w_gate, w_up, w_down)


def _combined(y0_ref, y1_ref, x_ref, wcol_ref, mod_ref):
    sub = lax.broadcasted_iota(I32, (LANES, wcol_ref.shape[1]), 0)
    wt = jnp.where(sub == 0, wcol_ref[0:1, :], jnp.where(sub == 1, wcol_ref[1:2, :], 0.0)).T
    w0 = wt[:, 0:1]
    w1 = wt[:, 1:2]
    hi0, lo0 = _unpack_rows(y0_ref[...])
    hi1, lo1 = _unpack_rows(y1_ref[...])
    y = jnp.concatenate([w0 * hi0 + w1 * hi1, w0 * lo0 + w1 * lo1], axis=1)
    return x_ref[...] + mod_ref[0, 5:6, :] * y


def _combine_kernel(y0_ref, y1_ref, x_ref, wcol_ref, mod_ref, o_ref):
    o_ref[...] = _combined(y0_ref, y1_ref, x_ref, wcol_ref, mod_ref)


def _combine(group, y2g, x, wcol, mod):
    n_g = T_CTX if group == 0 else T_LAT
    tiles = n_g // CT
    first = 0 if group == 0 else T_CTX // CT
    return pl.pallas_call(
        _combine_kernel,
        grid=(tiles,),
        in_specs=[
            pl.BlockSpec((CT, HALF), lambda i: (i, 0)),
            pl.BlockSpec((CT, HALF), lambda i: (tiles + i, 0)),
            pl.BlockSpec((CT, D), lambda i: (first + i, 0)),
            pl.BlockSpec((TOP_K, CT), lambda i: (0, first + i)),
            pl.BlockSpec((1, N_MOD, D), lambda i: (_cond_index(first + i, CT), 0, 0)),
        ],
        out_specs=pl.BlockSpec((CT, D), lambda i: (i, 0)),
        out_shape=jax.ShapeDtypeStruct((n_g, D), F32),
        name="moe_combine",
    )(y2g, y2g, x, wcol, mod)


def _moe(h, e, rank, cnt, layer, w_gate, w_up, w_down):
    pos, tile_expert, used = _plan(e, rank, cnt)
    xs = _sc_dispatch(h, pos)
    ys = _experts(xs, tile_expert, used, layer, w_gate, w_up, w_down)
    return _sc_gather(ys, pos, 0), _sc_gather(ys, pos, 1)


def kernel(x_prompt, x_sample, c, cache_k, cache_v, c_ctx, norm_mix_g, norm_ffn_g, ada_w, ada_b, attn_w_qkv, attn_q_norm, attn_k_norm, attn_w_o, pool_w, pool_scale, moe_w_group, moe_w_expert, moe_w_gate, moe_w_up, moe_w_down):
    xp = x_prompt.reshape(T_CTX, D)
    xs = x_sample.reshape(T_LAT, D)
    cond = jnp.concatenate([c_ctx[None, :], c, jnp.zeros((N_COND - 1 - DEC_BATCH, D), F32)], axis=0)
    mod = _ada(cond, ada_w, ada_b)

    qt, k2, vt, new_k, new_v = _qkv(xp, xs, mod[0], norm_mix_g[0:1], attn_w_qkv[0].astype(BF16),
                                    attn_q_norm[0:1], attn_k_norm[0:1])
    o, w_gate, w_up, w_down = _attention(qt, k2, vt, cache_k, cache_v,
                                         moe_w_gate, moe_w_up, moe_w_down)
    wr_hl, wr_hi = _router_weights(moe_w_group[0], moe_w_expert[0])
    x1, h, e, rank, wcol, cnt = _post(o, xp, xs, mod[0], attn_w_o[0].astype(BF16), norm_ffn_g[0:1],
                                      wr_hl, wr_hi)
    y2 = _moe(h, e, rank, cnt, 0, w_gate, w_up, w_down)

    wr_hl, wr_hi = _router_weights(moe_w_group[1], moe_w_expert[1])
    pool_args = (x1, wcol, mod[0], mod[1], norm_mix_g[1:2], pool_w[0].astype(BF16), pool_scale[0:1],
                 norm_ffn_g[1:2], wr_hl, wr_hi)
    outs = _pool(0, y2[0], *pool_args, jnp.zeros((NE, LANES), F32), None)
    x3, h, e, rank, wcol, cnt = _pool(1, y2[1], *pool_args, outs[5], outs[:5])
    y2 = _moe(h, e, rank, cnt, 1, w_gate, w_up, w_down)
    y_prompt = _combine(0, y2[0], x3, wcol, mod[1])
    y_sample = _combine(1, y2[1], x3, wcol, mod[1])

    return (y_prompt.reshape(BATCH, SEQ, D), y_sample.reshape(DEC_BATCH, DEC_SEQ, D),
            new_k.reshape(BATCH, 1, SEQ, N_KV, HD), new_v.reshape(BATCH, 1, SEQ, N_KV, HD))
```

```python
import functools

import jax
import jax.numpy as jnp
import numpy as np
from jax import lax
from jax.experimental import pallas as pl
from jax.experimental.pallas import tpu as pltpu
from jax.experimental.pallas import tpu_sc as plsc

F32 = jnp.float32
BF16 = jnp.bfloat16
I32 = jnp.int32
U32 = jnp.uint32

D = 1024
BATCH, SEQ = 32, 256
DEC_BATCH, DEC_SEQ, PAST = 8, 1024, 512
T_CTX = BATCH * SEQ
T_LAT = DEC_BATCH * DEC_SEQ
T = T_CTX + T_LAT
GRID_W = 64
N_HEADS, N_KV, HD = 16, 4, 64
DQ = N_HEADS * HD
DKV = N_KV * HD
ROPE_THETA = 10000.0
POOL_WINDOWS = (2, 4, 8, 16)
PGD = D // len(POOL_WINDOWS)
N_GROUPS, E_PER_G, TOP_K = 4, 8, 2
NE = N_GROUPS * E_PER_G
DE = D // 4
N_MOD = 6
EPS = 1e-6
N_COND = 16

TILE = 512
NT = T // TILE
NT_CTX = T_CTX // TILE
QB = 256
KEY_PART = 256
PTILE = 1024
PB = 128
PBW = 256
PB_HALO = (PBW - PB) // 2
TM = 1024
N_ROW_TILES = (TOP_K * T) // TM + NE
NPAD = N_ROW_TILES * TM
CT = 1024
LANES = 128
NEG_INF = float("-inf")
LOG2_E = 1.4426950408889634
Q_SCALE = HD ** -0.5 * LOG2_E
HALF = D // 2
SC_CORES, SC_SUBCORES = 2, 16
SC_WORKERS = SC_CORES * SC_SUBCORES
SC_WIN = 64


def _vmem_limit(mib):
    return pltpu.CompilerParams(vmem_limit_bytes=mib * 1024 * 1024)


def _modulate(x, gain_scale, shift):
    ms = jnp.mean(x * x, axis=-1, keepdims=True)
    return (x * lax.rsqrt(ms + EPS)) * gain_scale + shift


def _head_norm(z, gain, ind, ind_t2):
    ss = jnp.dot((z * z).astype(BF16), ind, preferred_element_type=F32)
    inv = lax.rsqrt(ss * (1.0 / HD) + EPS)
    inv_hi = inv.astype(BF16)
    inv_lo = (inv - inv_hi.astype(F32)).astype(BF16)
    scale = jnp.dot(jnp.concatenate([inv_hi, inv_lo], axis=1), ind_t2, preferred_element_type=F32)
    return z * scale * gain


def _rope(z, cos_t, sin_t):
    lane = lax.broadcasted_iota(I32, (z.shape[0], LANES), 1)
    low = (lane % 32) < 16
    outs = []
    for c in range(z.shape[1] // LANES):
        zc = z[:, c * LANES:(c + 1) * LANES]
        up = pltpu.roll(zc, 16, axis=1)
        dn = pltpu.roll(zc, LANES - 16, axis=1)
        outs.append(zc * cos_t + jnp.where(low, dn, up) * sin_t)
    return jnp.concatenate(outs, axis=1)


def _rope_t(zt, cos_tt, sin_tt):
    sub = lax.broadcasted_iota(I32, (LANES, zt.shape[1]), 0)
    low = (sub % 32) < 16
    outs = []
    for c in range(zt.shape[0] // LANES):
        zc = zt[c * LANES:(c + 1) * LANES]
        up = pltpu.roll(zc, 16, axis=0)
        dn = pltpu.roll(zc, LANES - 16, axis=0)
        outs.append(zc * cos_tt + jnp.where(low, dn, up) * sin_tt)
    return jnp.concatenate(outs, axis=0)


def _pack_rows(z):
    bits = lax.bitcast_convert_type(z.astype(BF16).astype(F32), U32)
    return bits[:, :HALF] | (bits[:, HALF:] >> 16)


def _unpack_rows(p):
    hi = lax.bitcast_convert_type(p & jnp.uint32(0xFFFF0000), F32)
    lo = lax.bitcast_convert_type(p << 16, F32)
    return hi, lo


def _ada_kernel(c_ref, w_ref, b_ref, o_ref):
    c = c_ref[...]
    a = c * (1.0 / (1.0 + jnp.exp(-c)))
    o_ref[0] = jnp.dot(a.astype(BF16), w_ref[0].astype(BF16), preferred_element_type=F32) + b_ref[0]


def _ada(cond, ada_w, ada_b):
    nb = 1536
    depth = ada_w.shape[0]
    out = pl.pallas_call(
        _ada_kernel,
        grid=(depth, (N_MOD * D) // nb),
        in_specs=[
            pl.BlockSpec((N_COND, D), lambda l, j: (0, 0)),
            pl.BlockSpec((1, D, nb), lambda l, j: (l, 0, j)),
            pl.BlockSpec((1, 1, nb), lambda l, j: (l, 0, j)),
        ],
        out_specs=pl.BlockSpec((1, N_COND, nb), lambda l, j: (l, 0, j)),
        out_shape=jax.ShapeDtypeStruct((depth, N_COND, N_MOD * D), F32),
        compiler_params=_vmem_limit(40),
        name="ada",
    )(cond, ada_w, ada_b.reshape(depth, 1, N_MOD * D))
    return out.reshape(depth, N_COND, N_MOD, D)


def _cond_index(i, tile):
    n_ctx = T_CTX // tile
    per_batch = DEC_SEQ // tile
    return jnp.where(i < n_ctx, 0, 1 + (i - n_ctx) // per_batch)


def _qkv_kernel(xp_ref, xs_ref, mod_ref, g_ref, w_ref, qg_ref, kg_ref, indq_ref, indqt_ref,
                indk_ref, indkt_ref, cos_ref, sin_ref, cos_t_ref, sin_t_ref,
                qt_ref, k2_ref, vt_ref, nk_ref, nv_ref):
    i = pl.program_id(0)
    is_lat = i >= NT_CTX
    gain_scale = g_ref[...] * (1.0 + mod_ref[0, 1:2, :])
    subs = [slice(b * QB, (b + 1) * QB) for b in range(TILE // QB)]
    hs = [_modulate(jnp.where(is_lat, xs_ref[r, :], xp_ref[r, :]), gain_scale, mod_ref[0, 0:1, :]).astype(BF16)
          for r in subs]
    qkvs = [jnp.dot(h, w_ref[...], preferred_element_type=F32) for h in hs]
    qs = [_head_norm(z[:, :DQ], qg_ref[...], indq_ref[...], indqt_ref[...]) for z in qkvs]
    ks = [_head_norm(z[:, DQ:DQ + DKV], kg_ref[...], indk_ref[...], indkt_ref[...]) for z in qkvs]
    vs = [z[:, DQ + DKV:] for z in qkvs]
    qts = [(q * Q_SCALE).T for q in qs]
    for r, v in zip(subs, vs):
        vt_ref[:, r] = v.T.astype(BF16)

    @pl.when(is_lat)
    def _():
        for b, r in enumerate(subs):
            qt_ref[b] = _rope_t(qts[b], cos_t_ref[:, r], sin_t_ref[:, r]).astype(BF16)
            k2_ref[r, :] = _dup_heads(_rope(ks[b], cos_ref[r, :], sin_ref[r, :])).astype(BF16)

    @pl.when(jnp.logical_not(is_lat))
    def _():
        for b, r in enumerate(subs):
            qt_ref[b] = qts[b].astype(BF16)
            k2_ref[r, :] = _dup_heads(ks[b]).astype(BF16)
            nk_ref[r, :] = ks[b]
            nv_ref[r, :] = vs[b]


def _rope_tables():
    rows = DEC_SEQ // GRID_W
    row = np.broadcast_to(np.arange(rows, dtype=np.float32)[:, None], (rows, GRID_W)).reshape(-1)
    col = np.broadcast_to(np.arange(GRID_W, dtype=np.float32)[None, :], (rows, GRID_W)).reshape(-1)
    axis_dim = HD // 2
    inv_freq = np.power(np.float32(ROPE_THETA),
                        -np.arange(0, axis_dim, 2, dtype=np.float32) / np.float32(axis_dim))
    ang = np.concatenate([row[:, None] * inv_freq, col[:, None] * inv_freq], axis=-1).astype(np.float32)
    cos, sin = np.cos(ang), np.sin(ang)
    quarter = HD // 4
    cos_h = np.concatenate([cos[:, :quarter], cos[:, :quarter], cos[:, quarter:], cos[:, quarter:]], axis=1)
    sin_h = np.concatenate([-sin[:, :quarter], sin[:, :quarter], -sin[:, quarter:], sin[:, quarter:]], axis=1)
    return (np.tile(cos_h, (1, LANES // HD)).astype(np.float32),
            np.tile(sin_h, (1, LANES // HD)).astype(np.float32))


def _head_indicators(width):
    col = np.arange(width)[:, None]
    head = np.arange(LANES)[None, :]
    ind = (col // HD == head).astype(np.float32)
    return jnp.asarray(ind, BF16), jnp.asarray(np.concatenate([ind.T, ind.T], axis=0), BF16)


def _dup_heads(z):
    lane = lax.broadcasted_iota(I32, (z.shape[0], LANES), 1)
    cols = []
    for p in range(N_KV // 2):
        blk = z[:, p * LANES:(p + 1) * LANES]
        swp = pltpu.roll(blk, HD, axis=1)
        cols += [jnp.where(lane < HD, blk, swp), jnp.where(lane < HD, swp, blk)]
    return jnp.concatenate(cols, axis=1)


def _qkv(xp, xs, mod, g, w_qkv, q_gain, k_gain):
    indq, indqt = _head_indicators(DQ)
    indk, indkt = _head_indicators(DKV)
    cos_t, sin_t = _rope_tables()
    per_batch = DEC_SEQ // TILE
    const = lambda shape: pl.BlockSpec(shape, lambda i: (0,) * len(shape))
    return pl.pallas_call(
        _qkv_kernel,
        grid=(NT,),
        in_specs=[
            pl.BlockSpec((TILE, D), lambda i: (jnp.minimum(i, NT_CTX - 1), 0)),
            pl.BlockSpec((TILE, D), lambda i: (jnp.maximum(i - NT_CTX, 0), 0)),
            pl.BlockSpec((1, N_MOD, D), lambda i: (_cond_index(i, TILE), 0, 0)),
            const((1, D)),
            const((D, DQ + 2 * DKV)),
            const((1, DQ)), const((1, DKV)),
            const((DQ, LANES)), const((2 * LANES, DQ)),
            const((DKV, LANES)), const((2 * LANES, DKV)),
            pl.BlockSpec((TILE, LANES), lambda i: (jnp.maximum(i - NT_CTX, 0) % per_batch, 0)),
            pl.BlockSpec((TILE, LANES), lambda i: (jnp.maximum(i - NT_CTX, 0) % per_batch, 0)),
            pl.BlockSpec((LANES, TILE), lambda i: (0, jnp.maximum(i - NT_CTX, 0) % per_batch)),
            pl.BlockSpec((LANES, TILE), lambda i: (0, jnp.maximum(i - NT_CTX, 0) % per_batch)),
        ],
        out_specs=[
            pl.BlockSpec((TILE // QB, DQ, QB), lambda i: (i, 0, 0)),
            pl.BlockSpec((TILE, 2 * DKV), lambda i: (i, 0)),
            pl.BlockSpec((DKV, TILE), lambda i: (0, i)),
            pl.BlockSpec((TILE, DKV), lambda i: (jnp.minimum(i, NT_CTX - 1), 0)),
            pl.BlockSpec((TILE, DKV), lambda i: (jnp.minimum(i, NT_CTX - 1), 0)),
        ],
        out_shape=[
            jax.ShapeDtypeStruct((T // QB, DQ, QB), BF16),
            jax.ShapeDtypeStruct((T, 2 * DKV), BF16),
            jax.ShapeDtypeStruct((DKV, T), BF16),
            jax.ShapeDtypeStruct((T_CTX, DKV), F32),
            jax.ShapeDtypeStruct((T_CTX, DKV), F32),
        ],
        compiler_params=_vmem_limit(56),
        name="qkv",
    )(xp, xs, mod, g, w_qkv, jnp.tile(q_gain, (1, N_HEADS)), jnp.tile(k_gain, (1, N_KV)),
      indq, indqt, indk, indkt, cos_t, sin_t, cos_t.T, sin_t.T)


def _unit_scores(u, pairs, k2_parts, qt_ref):
    cols = []
    for j in range(pairs):
        pair = u * pairs + j
        qt_pair = qt_ref[pair * LANES:(pair + 1) * LANES, :]
        sub = lax.broadcasted_iota(I32, qt_pair.shape, 0)
        zero = jnp.zeros_like(qt_pair)
        cols += [jnp.where(sub < HD, qt_pair, zero), jnp.where(sub >= HD, qt_pair, zero)]
    qtu = jnp.concatenate(cols, axis=1)
    return [jnp.dot(k2, qtu, preferred_element_type=F32) for k2 in k2_parts]


def _softmax_values(ss, vt_parts):
    m = ss[0].max(axis=0, keepdims=True)
    for s in ss[1:]:
        m = jnp.maximum(m, s.max(axis=0, keepdims=True))
    acc = None
    l = None
    for s, vt in zip(ss, vt_parts):
        e = jnp.exp2(s - m)
        ls = e.sum(axis=0, keepdims=True)
        l = ls if l is None else l + ls
        t = jnp.dot(vt, e.astype(BF16), preferred_element_type=F32)
        acc = t if acc is None else acc + t
    return acc * (1.0 / l)


def _attend(keys_of, values_of, qt_ref, o_ref, pairs, ahead):
    n_units = N_HEADS // (2 * pairs)
    group = lambda u: (u * 2 * pairs) // (N_HEADS // N_KV)
    scores = lambda u: _unit_scores(u, pairs, keys_of(group(u)), qt_ref)
    pending = [scores(u) for u in range(min(ahead, n_units))]
    outs = []
    for u in range(n_units):
        if u + ahead < n_units:
            pending.append(scores(u + ahead))
        outs.append(_softmax_values(pending.pop(0), values_of(group(u))))
    lq = o_ref.shape[0]
    heads = [o[:, h * lq:(h + 1) * lq] for o in outs for h in range(2 * pairs)]
    o_ref[...] = jnp.concatenate(heads, axis=0).T.astype(BF16)


def _attn_ctx_kernel(qt_ref, k2_ref, vt_ref, o_ref):
    _attend(lambda g: [k2_ref[:, g * LANES:(g + 1) * LANES]],
            lambda g: [vt_ref[g * HD:(g + 1) * HD, :]], qt_ref.at[0], o_ref, pairs=2, ahead=1)


def _attn_lat_kernel(qt_ref, k2c_ref, vtc_ref, k2n_ref, vtn_ref, wg_ref, wu_ref, wd_ref, o_in_ref,
                     o_ref, wgb_ref, wub_ref, wdb_ref):
    del o_in_ref
    wgb_ref[...] = wg_ref[...].astype(BF16)
    wub_ref[...] = wu_ref[...].astype(BF16)
    wdb_ref[...] = wd_ref[...].astype(BF16)
    pc = [slice(h * KEY_PART, (h + 1) * KEY_PART) for h in range(PAST // KEY_PART)]
    pn = [slice(h * KEY_PART, (h + 1) * KEY_PART) for h in range(DEC_SEQ // KEY_PART)]
    _attend(lambda g: [k2c_ref[0, r, g * LANES:(g + 1) * LANES] for r in pc]
            + [k2n_ref[r, g * LANES:(g + 1) * LANES] for r in pn],
            lambda g: [vtc_ref[0, g * HD:(g + 1) * HD, r] for r in pc]
            + [vtn_ref[g * HD:(g + 1) * HD, r] for r in pn],
            qt_ref.at[0], o_ref, pairs=1, ahead=2)


def _cache_kernel(ck_ref, cv_ref, k2_ref, vt_ref):
    k2_ref[0] = _dup_heads(ck_ref[0]).astype(BF16)
    vt_ref[0] = cv_ref[0].T.astype(BF16)


def _attention(qt, k2, vt, cache_k, cache_v, w_gate, w_up, w_down):
    k2c, vtc = pl.pallas_call(
        _cache_kernel,
        grid=(DEC_BATCH,),
        in_specs=[
            pl.BlockSpec((1, PAST, DKV), lambda b: (b, 0, 0)),
            pl.BlockSpec((1, PAST, DKV), lambda b: (b, 0, 0)),
        ],
        out_specs=[
            pl.BlockSpec((1, PAST, 2 * DKV), lambda b: (b, 0, 0)),
            pl.BlockSpec((1, DKV, PAST), lambda b: (b, 0, 0)),
        ],
        out_shape=[
            jax.ShapeDtypeStruct((DEC_BATCH, PAST, 2 * DKV), BF16),
            jax.ShapeDtypeStruct((DEC_BATCH, DKV, PAST), BF16),
        ],
        name="cache_prep",
    )(cache_k, cache_v)

    o_ctx = pl.pallas_call(
        _attn_ctx_kernel,
        grid=(BATCH,),
        in_specs=[
            pl.BlockSpec((1, DQ, QB), lambda b: (b, 0, 0)),
            pl.BlockSpec((SEQ, 2 * DKV), lambda b: (b, 0)),
            pl.BlockSpec((DKV, SEQ), lambda b: (0, b)),
        ],
        out_specs=pl.BlockSpec((SEQ, DQ), lambda b: (b, 0)),
        out_shape=jax.ShapeDtypeStruct((T, DQ), BF16),
        name="attn_ctx",
    )(qt, k2, vt)

    qb = QB
    nqb = DEC_SEQ // qb
    lat0 = T_CTX // qb
    depth = w_gate.shape[0]
    eps = (depth * NE) // (DEC_BATCH * nqb)
    per_layer = NE // eps
    wblock = lambda b, j: ((b * nqb + j) // per_layer, (b * nqb + j) % per_layer, 0, 0)
    return pl.pallas_call(
        _attn_lat_kernel,
        grid=(DEC_BATCH, nqb),
        in_specs=[
            pl.BlockSpec((1, DQ, QB), lambda b, j: (lat0 + b * nqb + j, 0, 0)),
            pl.BlockSpec((1, PAST, 2 * DKV), lambda b, j: (b, 0, 0)),
            pl.BlockSpec((1, DKV, PAST), lambda b, j: (b, 0, 0)),
            pl.BlockSpec((DEC_SEQ, 2 * DKV), lambda b, j: (T_CTX // DEC_SEQ + b, 0)),
            pl.BlockSpec((DKV, DEC_SEQ), lambda b, j: (0, T_CTX // DEC_SEQ + b)),
            pl.BlockSpec((1, eps, D, DE), wblock),
            pl.BlockSpec((1, eps, D, DE), wblock),
            pl.BlockSpec((1, eps, DE, D), wblock),
            pl.BlockSpec(memory_space=pl.ANY),
        ],
        out_specs=[
            pl.BlockSpec((qb, DQ), lambda b, j: (lat0 + b * nqb + j, 0)),
            pl.BlockSpec((1, eps, D, DE), wblock),
            pl.BlockSpec((1, eps, D, DE), wblock),
            pl.BlockSpec((1, eps, DE, D), wblock),
        ],
        out_shape=[
            jax.ShapeDtypeStruct((T, DQ), BF16),
            jax.ShapeDtypeStruct(w_gate.shape, BF16),
            jax.ShapeDtypeStruct(w_up.shape, BF16),
            jax.ShapeDtypeStruct(w_down.shape, BF16),
        ],
        input_output_aliases={8: 0},
        compiler_params=_vmem_limit(56),
        name="attn_lat",
    )(qt, k2c, vtc, k2, vt, w_gate, w_up, w_down, o_ctx)


def _route(h, wr_hl_ref, wr_hi_ref, tri_ref, cnt_ref, e_ref, rank_ref, wcol_ref, cnt_out_ref):
    tl = h.shape[0]
    h_hi = h.astype(BF16)
    h_lo = (h - h_hi.astype(F32)).astype(BF16)
    lg2 = jnp.dot(h_hi, wr_hl_ref[...], preferred_element_type=F32)
    lg = lg2[:, :LANES] + lg2[:, LANES:] + jnp.dot(h_lo, wr_hi_ref[...], preferred_element_type=F32)
    lt = lg.T

    sub8 = lax.broadcasted_iota(I32, (8, tl), 0).astype(F32)
    gl = jnp.where(sub8 < N_GROUPS, lt[0:8], NEG_INF)
    gmax = gl.max(axis=0, keepdims=True)
    gidx = jnp.min(jnp.where(gl == gmax, sub8, 8.0), axis=0, keepdims=True)
    gsum = jnp.sum(jnp.exp(gl - gmax), axis=0, keepdims=True)
    sel = lt[8:16]
    for g in range(1, N_GROUPS):
        sel = jnp.where(gidx == g, lt[8 + 8 * g:16 + 8 * g], sel)
    t1 = sel.max(axis=0, keepdims=True)
    j1 = jnp.min(jnp.where(sel == t1, sub8, 8.0), axis=0, keepdims=True)
    sel2 = jnp.where(sub8 == j1, NEG_INF, sel)
    t2 = sel2.max(axis=0, keepdims=True)
    j2 = jnp.min(jnp.where(sel2 == t2, sub8, 8.0), axis=0, keepdims=True)
    ex = jnp.exp(t2 - t1)
    den = 1.0 + ex
    gw = 1.0 / gsum
    w0 = gw * (1.0 / den)
    w1 = gw * (ex / den)
    e0 = gidx * E_PER_G + j1
    e1 = gidx * E_PER_G + j2

    sub_e = lax.broadcasted_iota(I32, (NE, tl), 0).astype(F32)
    oh0 = (sub_e == e0).astype(F32)
    oh1 = (sub_e == e1).astype(F32)
    c = oh0 + oh1
    carry = cnt_ref[:, 0:1]
    pieces = []
    for ch in range(tl // 256):
        cc = c[:, ch * 256:(ch + 1) * 256]
        pieces.append(jnp.dot(cc.astype(BF16), tri_ref[...], preferred_element_type=F32) + carry)
        carry = carry + jnp.sum(cc, axis=1, keepdims=True)
    csum = jnp.concatenate(pieces, axis=1)
    cnt_ref[...] = jnp.broadcast_to(carry, (NE, LANES))
    cnt_out_ref[...] = jnp.broadcast_to(carry, (NE, LANES))
    r0 = jnp.sum(oh0 * csum, axis=0, keepdims=True) - 1.0
    r1 = jnp.sum(oh1 * csum, axis=0, keepdims=True) - 1.0
    e_ref[0:1, :] = e0.astype(I32)
    e_ref[1:2, :] = e1.astype(I32)
    rank_ref[0:1, :] = r0.astype(I32)
    rank_ref[1:2, :] = r1.astype(I32)
    wcol_ref[0:1, :] = w0
    wcol_ref[1:2, :] = w1


def _post_kernel(o_ref, xp_ref, xs_ref, mod_ref, wo_ref, g_ref, wr_hl_ref, wr_hi_ref, tri_ref,
                 x1_ref, h_ref, e_ref, rank_ref, wcol_ref, cnt_out_ref, cnt_ref):
    i = pl.program_id(0)

    @pl.when(i == 0)
    def _():
        cnt_ref[...] = jnp.zeros_like(cnt_ref)

    x = jnp.where(i >= NT_CTX, xs_ref[...], xp_ref[...])
    m = jnp.dot(o_ref[...], wo_ref[...], preferred_element_type=F32)
    x1 = x + mod_ref[0, 2:3, :] * m
    x1_ref[...] = x1
    h = _modulate(x1, g_ref[...] * (1.0 + mod_ref[0, 4:5, :]), mod_ref[0, 3:4, :])
    h_ref[...] = _pack_rows(h)
    _route(h, wr_hl_ref, wr_hi_ref, tri_ref, cnt_ref, e_ref, rank_ref, wcol_ref, cnt_out_ref)


def _router_weights(w_group, w_expert):
    w = jnp.concatenate([w_group, jnp.zeros((D, 8 - N_GROUPS), F32),
                         jnp.transpose(w_expert, (1, 0, 2)).reshape(D, NE),
                         jnp.zeros((D, LANES - 8 - NE), F32)], axis=1)
    w_hi = w.astype(BF16)
    w_lo = (w - w_hi.astype(F32)).astype(BF16)
    return jnp.concatenate([w_hi, w_lo], axis=1), w_hi


def _route_outs(tile, first=0):
    specs = [
        pl.BlockSpec((tile, D), lambda i: (first + i, 0)),
        pl.BlockSpec((tile, HALF), lambda i: (first + i, 0)),
        pl.BlockSpec((TOP_K, tile), lambda i: (0, first + i)),
        pl.BlockSpec((TOP_K, tile), lambda i: (0, first + i)),
        pl.BlockSpec((TOP_K, tile), lambda i: (0, first + i)),
        pl.BlockSpec((NE, LANES), lambda i: (0, 0)),
    ]
    shapes = [
        jax.ShapeDtypeStruct((T, D), F32),
        jax.ShapeDtypeStruct((T, HALF), U32),
        jax.ShapeDtypeStruct((TOP_K, T), I32),
        jax.ShapeDtypeStruct((TOP_K, T), I32),
        jax.ShapeDtypeStruct((TOP_K, T), F32),
        jax.ShapeDtypeStruct((NE, LANES), F32),
    ]
    return specs, shapes


def _tri():
    a = np.arange(256)
    return jnp.asarray((a[:, None] <= a[None, :]).astype(np.float32), BF16)


def _post(o, xp, xs, mod, w_o, g_ffn, wr_hl, wr_hi):
    const = lambda shape: pl.BlockSpec(shape, lambda i: (0,) * len(shape))
    out_specs, out_shape = _route_outs(TILE)
    return pl.pallas_call(
        _post_kernel,
        grid=(NT,),
        in_specs=[
            pl.BlockSpec((TILE, DQ), lambda i: (i, 0)),
            pl.BlockSpec((TILE, D), lambda i: (jnp.minimum(i, NT_CTX - 1), 0)),
            pl.BlockSpec((TILE, D), lambda i: (jnp.maximum(i - NT_CTX, 0), 0)),
            pl.BlockSpec((1, N_MOD, D), lambda i: (_cond_index(i, TILE), 0, 0)),
            const((DQ, D)), const((1, D)), const((D, 2 * LANES)), const((D, LANES)), const((256, 256)),
        ],
        out_specs=out_specs,
        out_shape=out_shape,
        scratch_shapes=[pltpu.VMEM((NE, LANES), F32)],
        compiler_params=_vmem_limit(56),
        name="post_attn",
    )(o, xp, xs, mod, w_o, g_ffn, wr_hl, wr_hi, _tri())


def _pool_kernel(block_kinds, n_alias, y0_ref, y1_ref, xin_ref, wcin_ref, modp_ref, mod_ref, g_ref, pw_ref,
                 ps_ref, gf_ref, wr_hl_ref, wr_hi_ref, tri_ref, band_ref, inv_cnt_ref, cnt_in_ref, *refs):
    (x1_ref, h_ref, e_ref, rank_ref, wcol_ref, cnt_out_ref, cnt_ref, hhi_ref, hlo_ref) = refs[n_alias:]
    i = pl.program_id(0)

    @pl.when(i == 0)
    def _():
        cnt_ref[...] = cnt_in_ref[...]

    x = _combined(y0_ref, y1_ref, xin_ref, wcin_ref, modp_ref)
    h = _modulate(x, g_ref[...] * (1.0 + mod_ref[0, 1:2, :]), mod_ref[0, 0:1, :])
    @pl.when(i == 0)
    def _():
        zeros = jnp.zeros((PB_HALO, D), BF16)
        for ref in (hhi_ref, hlo_ref):
            ref[0:PB_HALO, :] = zeros
            ref[PB_HALO + PTILE:PB_HALO + PTILE + PB_HALO, :] = zeros

    h_hi = h.astype(BF16)
    hhi_ref[PB_HALO:PB_HALO + PTILE, :] = h_hi
    hlo_ref[PB_HALO:PB_HALO + PTILE, :] = (h - h_hi.astype(F32)).astype(BF16)

    outs = []
    for gi in range(len(POOL_WINDOWS)):
        cols = slice(gi * PGD, (gi + 1) * PGD)
        sums = []
        for b in range(PTILE // PB):
            bm = band_ref[block_kinds[b], gi]
            sums.append(jnp.dot(bm, hhi_ref[b * PB:b * PB + PBW, cols], preferred_element_type=F32)
                        + jnp.dot(bm, hlo_ref[b * PB:b * PB + PBW, cols], preferred_element_type=F32))
        diff = jnp.concatenate(sums, axis=0) * inv_cnt_ref[:, gi:gi + 1] - h[:, cols]
        outs.append(jnp.dot(diff.astype(BF16), pw_ref[gi], preferred_element_type=F32))
    m = jnp.concatenate(outs, axis=1) * ps_ref[...]
    x1 = x + mod_ref[0, 2:3, :] * m
    x1_ref[...] = x1
    hf = _modulate(x1, gf_ref[...] * (1.0 + mod_ref[0, 4:5, :]), mod_ref[0, 3:4, :])
    h_ref[...] = _pack_rows(hf)
    _route(hf, wr_hl_ref, wr_hi_ref, tri_ref, cnt_ref, e_ref, rank_ref, wcol_ref, cnt_out_ref)


def _pool_constants(seq):
    r = np.arange(PB)[:, None]
    d = np.arange(PBW)[None, :] - PB_HALO - r
    found, kinds = {}, []
    for b in range(PTILE // PB):
        tm = (b * PB + r) % seq
        inseq = (tm + d >= 0) & (tm + d < seq)
        mats = np.stack([(inseq & (d >= -(w // 2)) & (d <= w - w // 2 - 1)).astype(np.float32)
                         for w in POOL_WINDOWS])
        kinds.append(found.setdefault(mats.tobytes(), (len(found), mats))[0])
    bands = np.stack([m for _, m in sorted(found.values(), key=lambda kv: kv[0])])
    tm = np.arange(PTILE) % seq
    inv_cnt = np.zeros((PTILE, LANES), np.float32)
    for g, w in enumerate(POOL_WINDOWS):
        left, right = w // 2, w - w // 2 - 1
        inv_cnt[:, g] = 1.0 / (np.minimum(tm + right + 1, seq) - np.maximum(tm - left, 0)).astype(np.float32)
    return jnp.asarray(bands, BF16), tuple(kinds), inv_cnt


def _pool(group, y2g, x, wcol, mod_prev, mod, g_mix, pool_w, pool_scale, g_ffn, wr_hl, wr_hi, cnt_in, prev):
    const = lambda shape: pl.BlockSpec(shape, lambda i: (0,) * len(shape))
    tiles = (T_CTX if group == 0 else T_LAT) // PTILE
    first = 0 if group == 0 else T_CTX // PTILE
    out_specs, out_shape = _route_outs(PTILE, first)
    aliases = () if prev is None else tuple(prev)
    bands, block_kinds, inv_cnt = _pool_constants(SEQ if group == 0 else DEC_SEQ)
    n_fixed = 16
    return pl.pallas_call(
        functools.partial(_pool_kernel, block_kinds, len(aliases)),
        grid=(tiles,),
        in_specs=[
            pl.BlockSpec((PTILE, HALF), lambda i: (i, 0)),
            pl.BlockSpec((PTILE, HALF), lambda i: (tiles + i, 0)),
            pl.BlockSpec((PTILE, D), lambda i: (first + i, 0)),
            pl.BlockSpec((TOP_K, PTILE), lambda i: (0, first + i)),
            pl.BlockSpec((1, N_MOD, D), lambda i: (_cond_index(first + i, PTILE), 0, 0)),
            pl.BlockSpec((1, N_MOD, D), lambda i: (_cond_index(first + i, PTILE), 0, 0)),
            const((1, D)), const((len(POOL_WINDOWS), PGD, PGD)), const((1, D)), const((1, D)),
            const((D, 2 * LANES)), const((D, LANES)), const((256, 256)), const(bands.shape),
            const((PTILE, LANES)), const((NE, LANES)),
        ] + [pl.BlockSpec(memory_space=pl.ANY)] * len(aliases),
        out_specs=out_specs,
        out_shape=out_shape,
        input_output_aliases={n_fixed + k: k for k in range(len(aliases))},
        scratch_shapes=[pltpu.VMEM((NE, LANES), F32), pltpu.VMEM((PTILE + 2 * PB_HALO, D), BF16),
                        pltpu.VMEM((PTILE + 2 * PB_HALO, D), BF16)],
        compiler_params=_vmem_limit(56),
        name="pool_mixer",
    )(y2g, y2g, x, wcol, mod_prev, mod, g_mix, pool_w, pool_scale, g_ffn, wr_hl, wr_hi, _tri(), bands, inv_cnt,
      cnt_in, *aliases)


def _plan_kernel(e_ref, rank_ref, cnt_ref, pos_ref, te_ref, used_ref):
    counts = cnt_ref[:, 0:1]
    tiles_col = jnp.floor((counts + (TM - 1)) * (1.0 / TM))
    sub = lax.broadcasted_iota(I32, (NE, NE), 0)
    lane = lax.broadcasted_iota(I32, (NE, NE), 1)
    tiles_row = jnp.sum(jnp.where(sub == lane, tiles_col, 0.0), axis=0, keepdims=True)
    first_col = jnp.sum(jnp.where(lane < sub, tiles_row, 0.0), axis=1, keepdims=True)
    end_col = first_col + tiles_col
    used = jnp.sum(tiles_col, axis=0, keepdims=True)
    base_col = first_col * TM
    sub_e = lax.broadcasted_iota(I32, (NE, T), 0)
    for k in range(TOP_K):
        hit = sub_e == e_ref[k:k + 1, :]
        base = jnp.sum(jnp.where(hit, base_col, 0.0), axis=0, keepdims=True)
        pos_ref[k:k + 1, :] = base.astype(I32) + rank_ref[k:k + 1, :]
    tile = jnp.minimum(lax.broadcasted_iota(I32, (NE, LANES), 1).astype(F32), used - 1.0)
    te_ref[...] = jnp.sum(jnp.where(tile >= end_col, 1.0, 0.0), axis=0, keepdims=True).astype(I32)
    used_ref[...] = jnp.broadcast_to(used, (1, LANES)).astype(I32)


def _plan(e, rank, cnt):
    assert N_ROW_TILES <= LANES
    pos, te, used = pl.pallas_call(
        _plan_kernel,
        out_shape=[jax.ShapeDtypeStruct((TOP_K, T), I32), jax.ShapeDtypeStruct((1, LANES), I32),
                   jax.ShapeDtypeStruct((1, LANES), I32)],
        name="moe_plan",
    )(e, rank, cnt)
    return pos, te[0, :N_ROW_TILES], used[0, :1]


def _sc_mesh():
    return plsc.VectorSubcoreMesh(core_axis_name="core", subcore_axis_name="subcore")


def _sc_worker():
    return lax.axis_index("core") * SC_SUBCORES + lax.axis_index("subcore")


def _sc_dispatch(h, pos):
    per = T // SC_WORKERS
    nwin = per // SC_WIN

    @functools.partial(
        pl.kernel, out_type=jax.ShapeDtypeStruct((NPAD, HALF), U32), mesh=_sc_mesh(),
        scratch_types=[pltpu.VMEM((2, SC_WIN, HALF), U32), pltpu.VMEM((2, SC_WIN), I32),
                       pltpu.VMEM((2, SC_WIN), I32), pltpu.SemaphoreType.DMA((2,)),
                       pltpu.SemaphoreType.DMA((2,))])
    def run(h_hbm, pos_hbm, xs_hbm, buf, idx0, idx1, sem_in, sem_out):
        first = _sc_worker() * per

        def loads(j, s):
            base = first + j * SC_WIN
            return (pltpu.make_async_copy(h_hbm.at[pl.ds(base, SC_WIN)], buf.at[s], sem_in.at[s]),
                    pltpu.make_async_copy(pos_hbm.at[pl.ds(base, SC_WIN)], idx0.at[s], sem_in.at[s]),
                    pltpu.make_async_copy(pos_hbm.at[pl.ds(T + base, SC_WIN)], idx1.at[s], sem_in.at[s]))

        def stores(s):
            return (pltpu.make_async_copy(buf.at[s], xs_hbm.at[idx0.at[s]], sem_out.at[s]),
                    pltpu.make_async_copy(buf.at[s], xs_hbm.at[idx1.at[s]], sem_out.at[s]))

        for c in loads(0, 0):
            c.start()

        @pl.loop(0, nwin // 2)
        def _(jj):
            for s in range(2):
                j = jj * 2 + s
                for c in loads(j, s):
                    c.wait()
                for c in stores(s):
                    c.start()

                @pl.when(j >= 1)
                def _():
                    for c in stores(1 - s):
                        c.wait()

                @pl.when(j + 1 < nwin)
                def _():
                    for c in loads(j + 1, 1 - s):
                        c.start()

        for c in stores((nwin - 1) % 2):
            c.wait()

    return run(h, pos.reshape(TOP_K * T))


def _sc_gather(ys, pos, group):
    t0, n_g = (0, T_CTX) if group == 0 else (T_CTX, T_LAT)
    idx_all = jnp.concatenate([pos[k, t0:t0 + n_g] for k in range(TOP_K)])
    n = TOP_K * n_g
    per = n // SC_WORKERS
    nwin = per // SC_WIN

    @functools.partial(
        pl.kernel, out_type=jax.ShapeDtypeStruct((n, HALF), U32), mesh=_sc_mesh(),
        scratch_types=[pltpu.VMEM((2, SC_WIN, HALF), U32), pltpu.VMEM((per,), I32),
                       pltpu.SemaphoreType.DMA((2,)), pltpu.SemaphoreType.DMA((2,))])
    def run(ys_hbm, pos_hbm, out_hbm, buf, idx, sem_in, sem_out):
        first = _sc_worker() * per
        pltpu.sync_copy(pos_hbm.at[pl.ds(first, per)], idx)

        def load(j, s):
            return pltpu.make_async_copy(ys_hbm.at[idx.at[pl.ds(j * SC_WIN, SC_WIN)]], buf.at[s], sem_in.at[s])

        def store(j, s):
            return pltpu.make_async_copy(buf.at[s], out_hbm.at[pl.ds(first + j * SC_WIN, SC_WIN)], sem_out.at[s])

        load(0, 0).start()

        @pl.loop(0, nwin // 2)
        def _(jj):
            for s in range(2):
                j = jj * 2 + s
                load(j, s).wait()
                store(j, s).start()

                @pl.when(j >= 1)
                def _():
                    store(j - 1, 1 - s).wait()

                @pl.when(j + 1 < nwin)
                def _():
                    load(j + 1, 1 - s).start()

        store(nwin - 1, (nwin - 1) % 2).wait()

    return run(ys, idx_all)


def _moe_kernel(te_ref, used_ref, x_ref, wg_ref, wu_ref, wd_ref, y_ref):
    del te_ref
    i = pl.program_id(0)

    @pl.when(i < used_ref[0])
    def _():
        hi, lo = _unpack_rows(x_ref[...])
        x = jnp.concatenate([hi, lo], axis=1).astype(BF16)
        a = jnp.dot(x, wg_ref[0, 0], preferred_element_type=F32)
        b = jnp.dot(x, wu_ref[0, 0], preferred_element_type=F32)
        hid = a * (1.0 / (1.0 + jnp.exp(-a))) * b
        y_ref[...] = _pack_rows(jnp.dot(hid.astype(BF16), wd_ref[0, 0], preferred_element_type=F32))


def _experts(xs, tile_expert, used, layer, w_gate, w_up, w_down):
    row = lambda i, te, used: (jnp.minimum(i, used[0] - 1), 0)
    weights = lambda i, te, used: (layer, te[i], 0, 0)
    grid_spec = pltpu.PrefetchScalarGridSpec(
        num_scalar_prefetch=2,
        grid=(N_ROW_TILES,),
        in_specs=[
            pl.BlockSpec((TM, HALF), row),
            pl.BlockSpec((1, 1, D, DE), weights),
            pl.BlockSpec((1, 1, D, DE), weights),
            pl.BlockSpec((1, 1, DE, D), weights),
        ],
        out_specs=pl.BlockSpec((TM, HALF), row),
    )
    return pl.pallas_call(
        _moe_kernel,
        grid_spec=grid_spec,
        out_shape=jax.ShapeDtypeStruct((NPAD, HALF), U32),
        compiler_params=_vmem_limit(40),
        name="moe_experts",
    )(tile_expert, used, xs, w_gate, w_up, w_down)


def _combined(y0_ref, y1_ref, x_ref, wcol_ref, mod_ref):
    sub = lax.broadcasted_iota(I32, (LANES, wcol_ref.shape[1]), 0)
    wt = jnp.where(sub == 0, wcol_ref[0:1, :], jnp.where(sub == 1, wcol_ref[1:2, :], 0.0)).T
    w0 = wt[:, 0:1]
    w1 = wt[:, 1:2]
    hi0, lo0 = _unpack_rows(y0_ref[...])
    hi1, lo1 = _unpack_rows(y1_ref[...])
    y = jnp.concatenate([w0 * hi0 + w1 * hi1, w0 * lo0 + w1 * lo1], axis=1)
    return x_ref[...] + mod_ref[0, 5:6, :] * y


def _combine_kernel(y0_ref, y1_ref, x_ref, wcol_ref, mod_ref, o_ref):
    o_ref[...] = _combined(y0_ref, y1_ref, x_ref, wcol_ref, mod_ref)


def _combine(group, y2g, x, wcol, mod):
    n_g = T_CTX if group == 0 else T_LAT
    tiles = n_g // CT
    first = 0 if group == 0 else T_CTX // CT
    return pl.pallas_call(
        _combine_kernel,
        grid=(tiles,),
        in_specs=[
            pl.BlockSpec((CT, HALF), lambda i: (i, 0)),
            pl.BlockSpec((CT, HALF), lambda i: (tiles + i, 0)),
            pl.BlockSpec((CT, D), lambda i: (first + i, 0)),
            pl.BlockSpec((TOP_K, CT), lambda i: (0, first + i)),
            pl.BlockSpec((1, N_MOD, D), lambda i: (_cond_index(first + i, CT), 0, 0)),
        ],
        out_specs=pl.BlockSpec((CT, D), lambda i: (i, 0)),
        out_shape=jax.ShapeDtypeStruct((n_g, D), F32),
        name="moe_combine",
    )(y2g, y2g, x, wcol, mod)


def _moe(h, e, rank, cnt, layer, w_gate, w_up, w_down):
    pos, tile_expert, used = _plan(e, rank, cnt)
    xs = _sc_dispatch(h, pos)
    ys = _experts(xs, tile_expert, used, layer, w_gate, w_up, w_down)
    return _sc_gather(ys, pos, 0), _sc_gather(ys, pos, 1)


def kernel(x_prompt, x_sample, c, cache_k, cache_v, c_ctx, norm_mix_g, norm_ffn_g, ada_w, ada_b, attn_w_qkv, attn_q_norm, attn_k_norm, attn_w_o, pool_w, pool_scale, moe_w_group, moe_w_expert, moe_w_gate, moe_w_up, moe_w_down):
    xp = x_prompt.reshape(T_CTX, D)
    xs = x_sample.reshape(T_LAT, D)
    cond = jnp.concatenate([c_ctx[None, :], c, jnp.zeros((N_COND - 1 - DEC_BATCH, D), F32)], axis=0)
    mod = _ada(cond, ada_w, ada_b)

    qt, k2, vt, new_k, new_v = _qkv(xp, xs, mod[0], norm_mix_g[0:1], attn_w_qkv[0].astype(BF16),
                                    attn_q_norm[0:1], attn_k_norm[0:1])
    o, w_gate, w_up, w_down = _attention(qt, k2, vt, cache_k[:, 0].reshape(DEC_BATCH, PAST, DKV),
                                         cache_v[:, 0].reshape(DEC_BATCH, PAST, DKV),
                                         moe_w_gate, moe_w_up, moe_w_down)
    wr_hl, wr_hi = _router_weights(moe_w_group[0], moe_w_expert[0])
    x1, h, e, rank, wcol, cnt = _post(o, xp, xs, mod[0], attn_w_o[0].astype(BF16), norm_ffn_g[0:1],
                                      wr_hl, wr_hi)
    y2 = _moe(h, e, rank, cnt, 0, w_gate, w_up, w_down)

    wr_hl, wr_hi = _router_weights(moe_w_group[1], moe_w_expert[1])
    pool_args = (x1, wcol, mod[0], mod[1], norm_mix_g[1:2], pool_w[0].astype(BF16), pool_scale[0:1],
                 norm_ffn_g[1:2], wr_hl, wr_hi)
    outs = _pool(0, y2[0], *pool_args, jnp.zeros((NE, LANES), F32), None)
    x3, h, e, rank, wcol, cnt = _pool(1, y2[1], *pool_args, outs[5], outs[:5])
    y2 = _moe(h, e, rank, cnt, 1, w_gate, w_up, w_down)
    y_prompt = _combine(0, y2[0], x3, wcol, mod[1])
    y_sample = _combine(1, y2[1], x3, wcol, mod[1])

    return (y_prompt.reshape(BATCH, SEQ, D), y_sample.reshape(DEC_BATCH, DEC_SEQ, D),
            new_k.reshape(BATCH, 1, SEQ, N_KV, HD), new_v.reshape(BATCH, 1, SEQ, N_KV, HD))
```

```python
import functools

import jax
import jax.numpy as jnp
import numpy as np
from jax import lax
from jax.experimental import pallas as pl
from jax.experimental.pallas import tpu as pltpu
from jax.experimental.pallas import tpu_sc as plsc

F32 = jnp.float32
BF16 = jnp.bfloat16
I32 = jnp.int32
U32 = jnp.uint32

D = 1024
BATCH, SEQ = 32, 256
DEC_BATCH, DEC_SEQ, PAST = 8, 1024, 512
T_CTX = BATCH * SEQ
T_LAT = DEC_BATCH * DEC_SEQ
T = T_CTX + T_LAT
GRID_W = 64
N_HEADS, N_KV, HD = 16, 4, 64
DQ = N_HEADS * HD
DKV = N_KV * HD
ROPE_THETA = 10000.0
POOL_WINDOWS = (2, 4, 8, 16)
PGD = D // len(POOL_WINDOWS)
N_GROUPS, E_PER_G, TOP_K = 4, 8, 2
NE = N_GROUPS * E_PER_G
DE = D // 4
N_MOD = 6
EPS = 1e-6
N_COND = 16

TILE = 512
NT = T // TILE
NT_CTX = T_CTX // TILE
QB = 256
KEY_PART = 256
PTILE = 1024
PB = 128
PBW = 256
PB_HALO = (PBW - PB) // 2
TM = 1152
N_ROW_TILES = (TOP_K * T) // TM + NE
NPAD = N_ROW_TILES * TM
CT = 1024
LANES = 128
NEG_INF = float("-inf")
LOG2_E = 1.4426950408889634
Q_SCALE = HD ** -0.5 * LOG2_E
HALF = D // 2
SC_CORES, SC_SUBCORES = 2, 16
SC_WORKERS = SC_CORES * SC_SUBCORES
SC_WIN = 64


def _vmem_limit(mib):
    return pltpu.CompilerParams(vmem_limit_bytes=mib * 1024 * 1024)


def _modulate(x, gain_scale, shift):
    ms = jnp.mean(x * x, axis=-1, keepdims=True)
    return (x * lax.rsqrt(ms + EPS)) * gain_scale + shift


def _head_norm(z, gain, ind, ind_t2):
    ss = jnp.dot((z * z).astype(BF16), ind, preferred_element_type=F32)
    inv = lax.rsqrt(ss * (1.0 / HD) + EPS)
    inv_hi = inv.astype(BF16)
    inv_lo = (inv - inv_hi.astype(F32)).astype(BF16)
    scale = jnp.dot(jnp.concatenate([inv_hi, inv_lo], axis=1), ind_t2, preferred_element_type=F32)
    return z * scale * gain


def _rope(z, cos_t, sin_t):
    lane = lax.broadcasted_iota(I32, (z.shape[0], LANES), 1)
    low = (lane % 32) < 16
    outs = []
    for c in range(z.shape[1] // LANES):
        zc = z[:, c * LANES:(c + 1) * LANES]
        up = pltpu.roll(zc, 16, axis=1)
        dn = pltpu.roll(zc, LANES - 16, axis=1)
        outs.append(zc * cos_t + jnp.where(low, dn, up) * sin_t)
    return jnp.concatenate(outs, axis=1)


def _rope_t(zt, cos_tt, sin_tt):
    sub = lax.broadcasted_iota(I32, (LANES, zt.shape[1]), 0)
    low = (sub % 32) < 16
    outs = []
    for c in range(zt.shape[0] // LANES):
        zc = zt[c * LANES:(c + 1) * LANES]
        up = pltpu.roll(zc, 16, axis=0)
        dn = pltpu.roll(zc, LANES - 16, axis=0)
        outs.append(zc * cos_tt + jnp.where(low, dn, up) * sin_tt)
    return jnp.concatenate(outs, axis=0)


def _pack_rows(z):
    bits = lax.bitcast_convert_type(z.astype(BF16).astype(F32), U32)
    return bits[:, :HALF] | (bits[:, HALF:] >> 16)


def _unpack_rows(p):
    hi = lax.bitcast_convert_type(p & jnp.uint32(0xFFFF0000), F32)
    lo = lax.bitcast_convert_type(p << 16, F32)
    return hi, lo


def _ada_kernel(c_ref, w_ref, b_ref, o_ref):
    c = c_ref[...]
    a = c * (1.0 / (1.0 + jnp.exp(-c)))
    o_ref[0] = jnp.dot(a.astype(BF16), w_ref[0].astype(BF16), preferred_element_type=F32) + b_ref[0]


def _ada(cond, ada_w, ada_b):
    nb = 1536
    depth = ada_w.shape[0]
    out = pl.pallas_call(
        _ada_kernel,
        grid=(depth, (N_MOD * D) // nb),
        in_specs=[
            pl.BlockSpec((N_COND, D), lambda l, j: (0, 0)),
            pl.BlockSpec((1, D, nb), lambda l, j: (l, 0, j)),
            pl.BlockSpec((1, 1, nb), lambda l, j: (l, 0, j)),
        ],
        out_specs=pl.BlockSpec((1, N_COND, nb), lambda l, j: (l, 0, j)),
        out_shape=jax.ShapeDtypeStruct((depth, N_COND, N_MOD * D), F32),
        compiler_params=_vmem_limit(40),
        name="ada",
    )(cond, ada_w, ada_b.reshape(depth, 1, N_MOD * D))
    return out.reshape(depth, N_COND, N_MOD, D)


def _cond_index(i, tile):
    n_ctx = T_CTX // tile
    per_batch = DEC_SEQ // tile
    return jnp.where(i < n_ctx, 0, 1 + (i - n_ctx) // per_batch)


def _qkv_kernel(xp_ref, xs_ref, mod_ref, g_ref, w_ref, qg_ref, kg_ref, indq_ref, indqt_ref,
                indk_ref, indkt_ref, cos_ref, sin_ref, cos_t_ref, sin_t_ref,
                qt_ref, k2_ref, vt_ref, nk_ref, nv_ref):
    i = pl.program_id(0)
    is_lat = i >= NT_CTX
    gain_scale = g_ref[...] * (1.0 + mod_ref[0, 1:2, :])
    subs = [slice(b * QB, (b + 1) * QB) for b in range(TILE // QB)]
    hs = [_modulate(jnp.where(is_lat, xs_ref[r, :], xp_ref[r, :]), gain_scale, mod_ref[0, 0:1, :]).astype(BF16)
          for r in subs]
    qkvs = [jnp.dot(h, w_ref[...], preferred_element_type=F32) for h in hs]
    qs = [_head_norm(z[:, :DQ], qg_ref[...], indq_ref[...], indqt_ref[...]) for z in qkvs]
    ks = [_head_norm(z[:, DQ:DQ + DKV], kg_ref[...], indk_ref[...], indkt_ref[...]) for z in qkvs]
    vs = [z[:, DQ + DKV:] for z in qkvs]
    qts = [(q * Q_SCALE).T for q in qs]
    for r, v in zip(subs, vs):
        vt_ref[:, r] = v.T.astype(BF16)

    @pl.when(is_lat)
    def _():
        for b, r in enumerate(subs):
            qt_ref[b] = _rope_t(qts[b], cos_t_ref[:, r], sin_t_ref[:, r]).astype(BF16)
            k2_ref[r, :] = _dup_heads(_rope(ks[b], cos_ref[r, :], sin_ref[r, :])).astype(BF16)

    @pl.when(jnp.logical_not(is_lat))
    def _():
        for b, r in enumerate(subs):
            qt_ref[b] = qts[b].astype(BF16)
            k2_ref[r, :] = _dup_heads(ks[b]).astype(BF16)
            nk_ref[r, :] = ks[b]
            nv_ref[r, :] = vs[b]


def _rope_tables():
    rows = DEC_SEQ // GRID_W
    row = np.broadcast_to(np.arange(rows, dtype=np.float32)[:, None], (rows, GRID_W)).reshape(-1)
    col = np.broadcast_to(np.arange(GRID_W, dtype=np.float32)[None, :], (rows, GRID_W)).reshape(-1)
    axis_dim = HD // 2
    inv_freq = np.power(np.float32(ROPE_THETA),
                        -np.arange(0, axis_dim, 2, dtype=np.float32) / np.float32(axis_dim))
    ang = np.concatenate([row[:, None] * inv_freq, col[:, None] * inv_freq], axis=-1).astype(np.float32)
    cos, sin = np.cos(ang), np.sin(ang)
    quarter = HD // 4
    cos_h = np.concatenate([cos[:, :quarter], cos[:, :quarter], cos[:, quarter:], cos[:, quarter:]], axis=1)
    sin_h = np.concatenate([-sin[:, :quarter], sin[:, :quarter], -sin[:, quarter:], sin[:, quarter:]], axis=1)
    return (np.tile(cos_h, (1, LANES // HD)).astype(np.float32),
            np.tile(sin_h, (1, LANES // HD)).astype(np.float32))


def _head_indicators(width):
    col = np.arange(width)[:, None]
    head = np.arange(LANES)[None, :]
    ind = (col // HD == head).astype(np.float32)
    return jnp.asarray(ind, BF16), jnp.asarray(np.concatenate([ind.T, ind.T], axis=0), BF16)


def _dup_heads(z):
    lane = lax.broadcasted_iota(I32, (z.shape[0], LANES), 1)
    cols = []
    for p in range(N_KV // 2):
        blk = z[:, p * LANES:(p + 1) * LANES]
        swp = pltpu.roll(blk, HD, axis=1)
        cols += [jnp.where(lane < HD, blk, swp), jnp.where(lane < HD, swp, blk)]
    return jnp.concatenate(cols, axis=1)


def _qkv(xp, xs, mod, g, w_qkv, q_gain, k_gain):
    indq, indqt = _head_indicators(DQ)
    indk, indkt = _head_indicators(DKV)
    cos_t, sin_t = _rope_tables()
    per_batch = DEC_SEQ // TILE
    const = lambda shape: pl.BlockSpec(shape, lambda i: (0,) * len(shape))
    return pl.pallas_call(
        _qkv_kernel,
        grid=(NT,),
        in_specs=[
            pl.BlockSpec((TILE, D), lambda i: (jnp.minimum(i, NT_CTX - 1), 0)),
            pl.BlockSpec((TILE, D), lambda i: (jnp.maximum(i - NT_CTX, 0), 0)),
            pl.BlockSpec((1, N_MOD, D), lambda i: (_cond_index(i, TILE), 0, 0)),
            const((1, D)),
            const((D, DQ + 2 * DKV)),
            const((1, DQ)), const((1, DKV)),
            const((DQ, LANES)), const((2 * LANES, DQ)),
            const((DKV, LANES)), const((2 * LANES, DKV)),
            pl.BlockSpec((TILE, LANES), lambda i: (jnp.maximum(i - NT_CTX, 0) % per_batch, 0)),
            pl.BlockSpec((TILE, LANES), lambda i: (jnp.maximum(i - NT_CTX, 0) % per_batch, 0)),
            pl.BlockSpec((LANES, TILE), lambda i: (0, jnp.maximum(i - NT_CTX, 0) % per_batch)),
            pl.BlockSpec((LANES, TILE), lambda i: (0, jnp.maximum(i - NT_CTX, 0) % per_batch)),
        ],
        out_specs=[
            pl.BlockSpec((TILE // QB, DQ, QB), lambda i: (i, 0, 0)),
            pl.BlockSpec((TILE, 2 * DKV), lambda i: (i, 0)),
            pl.BlockSpec((DKV, TILE), lambda i: (0, i)),
            pl.BlockSpec((TILE, DKV), lambda i: (jnp.minimum(i, NT_CTX - 1), 0)),
            pl.BlockSpec((TILE, DKV), lambda i: (jnp.minimum(i, NT_CTX - 1), 0)),
        ],
        out_shape=[
            jax.ShapeDtypeStruct((T // QB, DQ, QB), BF16),
            jax.ShapeDtypeStruct((T, 2 * DKV), BF16),
            jax.ShapeDtypeStruct((DKV, T), BF16),
            jax.ShapeDtypeStruct((T_CTX, DKV), F32),
            jax.ShapeDtypeStruct((T_CTX, DKV), F32),
        ],
        compiler_params=_vmem_limit(56),
        name="qkv",
    )(xp, xs, mod, g, w_qkv, jnp.tile(q_gain, (1, N_HEADS)), jnp.tile(k_gain, (1, N_KV)),
      indq, indqt, indk, indkt, cos_t, sin_t, cos_t.T, sin_t.T)


def _unit_scores(u, pairs, k2_parts, qt_ref):
    cols = []
    for j in range(pairs):
        pair = u * pairs + j
        qt_pair = qt_ref[pair * LANES:(pair + 1) * LANES, :]
        sub = lax.broadcasted_iota(I32, qt_pair.shape, 0)
        zero = jnp.zeros_like(qt_pair)
        cols += [jnp.where(sub < HD, qt_pair, zero), jnp.where(sub >= HD, qt_pair, zero)]
    qtu = jnp.concatenate(cols, axis=1)
    return [jnp.dot(k2, qtu, preferred_element_type=F32) for k2 in k2_parts]


def _softmax_values(ss, vt_parts):
    m = ss[0].max(axis=0, keepdims=True)
    for s in ss[1:]:
        m = jnp.maximum(m, s.max(axis=0, keepdims=True))
    acc = None
    l = None
    for s, vt in zip(ss, vt_parts):
        e = jnp.exp2(s - m)
        ls = e.sum(axis=0, keepdims=True)
        l = ls if l is None else l + ls
        t = jnp.dot(vt, e.astype(BF16), preferred_element_type=F32)
        acc = t if acc is None else acc + t
    return acc * (1.0 / l)


def _attend(keys_of, values_of, qt_ref, o_ref, pairs, ahead):
    n_units = N_HEADS // (2 * pairs)
    group = lambda u: (u * 2 * pairs) // (N_HEADS // N_KV)
    scores = lambda u: _unit_scores(u, pairs, keys_of(group(u)), qt_ref)
    pending = [scores(u) for u in range(min(ahead, n_units))]
    outs = []
    for u in range(n_units):
        if u + ahead < n_units:
            pending.append(scores(u + ahead))
        outs.append(_softmax_values(pending.pop(0), values_of(group(u))))
    lq = o_ref.shape[0]
    heads = [o[:, h * lq:(h + 1) * lq] for o in outs for h in range(2 * pairs)]
    o_ref[...] = jnp.concatenate(heads, axis=0).T.astype(BF16)


def _attn_ctx_kernel(qt_ref, k2_ref, vt_ref, o_ref):
    _attend(lambda g: [k2_ref[:, g * LANES:(g + 1) * LANES]],
            lambda g: [vt_ref[g * HD:(g + 1) * HD, :]], qt_ref.at[0], o_ref, pairs=2, ahead=1)


def _attn_lat_kernel(qt_ref, k2c_ref, vtc_ref, k2n_ref, vtn_ref, wg_ref, wu_ref, wd_ref, o_in_ref,
                     o_ref, wgb_ref, wub_ref, wdb_ref):
    del o_in_ref
    wgb_ref[...] = wg_ref[...].astype(BF16)
    wub_ref[...] = wu_ref[...].astype(BF16)
    wdb_ref[...] = wd_ref[...].astype(BF16)
    pc = [slice(h * KEY_PART, (h + 1) * KEY_PART) for h in range(PAST // KEY_PART)]
    pn = [slice(h * KEY_PART, (h + 1) * KEY_PART) for h in range(DEC_SEQ // KEY_PART)]
    _attend(lambda g: [k2c_ref[0, r, g * LANES:(g + 1) * LANES] for r in pc]
            + [k2n_ref[r, g * LANES:(g + 1) * LANES] for r in pn],
            lambda g: [vtc_ref[0, g * HD:(g + 1) * HD, r] for r in pc]
            + [vtn_ref[g * HD:(g + 1) * HD, r] for r in pn],
            qt_ref.at[0], o_ref, pairs=1, ahead=2)


def _cache_kernel(ck_ref, cv_ref, k2_ref, vt_ref):
    k2_ref[0] = _dup_heads(ck_ref[0]).astype(BF16)
    vt_ref[0] = cv_ref[0].T.astype(BF16)


def _attention(qt, k2, vt, cache_k, cache_v, w_gate, w_up, w_down):
    k2c, vtc = pl.pallas_call(
        _cache_kernel,
        grid=(DEC_BATCH,),
        in_specs=[
            pl.BlockSpec((1, PAST, DKV), lambda b: (b, 0, 0)),
            pl.BlockSpec((1, PAST, DKV), lambda b: (b, 0, 0)),
        ],
        out_specs=[
            pl.BlockSpec((1, PAST, 2 * DKV), lambda b: (b, 0, 0)),
            pl.BlockSpec((1, DKV, PAST), lambda b: (b, 0, 0)),
        ],
        out_shape=[
            jax.ShapeDtypeStruct((DEC_BATCH, PAST, 2 * DKV), BF16),
            jax.ShapeDtypeStruct((DEC_BATCH, DKV, PAST), BF16),
        ],
        name="cache_prep",
    )(cache_k, cache_v)

    o_ctx = pl.pallas_call(
        _attn_ctx_kernel,
        grid=(BATCH,),
        in_specs=[
            pl.BlockSpec((1, DQ, QB), lambda b: (b, 0, 0)),
            pl.BlockSpec((SEQ, 2 * DKV), lambda b: (b, 0)),
            pl.BlockSpec((DKV, SEQ), lambda b: (0, b)),
        ],
        out_specs=pl.BlockSpec((SEQ, DQ), lambda b: (b, 0)),
        out_shape=jax.ShapeDtypeStruct((T, DQ), BF16),
        name="attn_ctx",
    )(qt, k2, vt)

    qb = QB
    nqb = DEC_SEQ // qb
    lat0 = T_CTX // qb
    depth = w_gate.shape[0]
    eps = (depth * NE) // (DEC_BATCH * nqb)
    per_layer = NE // eps
    wblock = lambda b, j: ((b * nqb + j) // per_layer, (b * nqb + j) % per_layer, 0, 0)
    return pl.pallas_call(
        _attn_lat_kernel,
        grid=(DEC_BATCH, nqb),
        in_specs=[
            pl.BlockSpec((1, DQ, QB), lambda b, j: (lat0 + b * nqb + j, 0, 0)),
            pl.BlockSpec((1, PAST, 2 * DKV), lambda b, j: (b, 0, 0)),
            pl.BlockSpec((1, DKV, PAST), lambda b, j: (b, 0, 0)),
            pl.BlockSpec((DEC_SEQ, 2 * DKV), lambda b, j: (T_CTX // DEC_SEQ + b, 0)),
            pl.BlockSpec((DKV, DEC_SEQ), lambda b, j: (0, T_CTX // DEC_SEQ + b)),
            pl.BlockSpec((1, eps, D, DE), wblock),
            pl.BlockSpec((1, eps, D, DE), wblock),
            pl.BlockSpec((1, eps, DE, D), wblock),
            pl.BlockSpec(memory_space=pl.ANY),
        ],
        out_specs=[
            pl.BlockSpec((qb, DQ), lambda b, j: (lat0 + b * nqb + j, 0)),
            pl.BlockSpec((1, eps, D, DE), wblock),
            pl.BlockSpec((1, eps, D, DE), wblock),
            pl.BlockSpec((1, eps, DE, D), wblock),
        ],
        out_shape=[
            jax.ShapeDtypeStruct((T, DQ), BF16),
            jax.ShapeDtypeStruct(w_gate.shape, BF16),
            jax.ShapeDtypeStruct(w_up.shape, BF16),
            jax.ShapeDtypeStruct(w_down.shape, BF16),
        ],
        input_output_aliases={8: 0},
        compiler_params=_vmem_limit(56),
        name="attn_lat",
    )(qt, k2c, vtc, k2, vt, w_gate, w_up, w_down, o_ctx)


def _route(h, wr_hl_ref, wr_hi_ref, tri_ref, cnt_ref, e_ref, rank_ref, wcol_ref, cnt_out_ref):
    tl = h.shape[0]
    h_hi = h.astype(BF16)
    h_lo = (h - h_hi.astype(F32)).astype(BF16)
    lg2 = jnp.dot(h_hi, wr_hl_ref[...], preferred_element_type=F32)
    lg = lg2[:, :LANES] + lg2[:, LANES:] + jnp.dot(h_lo, wr_hi_ref[...], preferred_element_type=F32)
    lt = lg.T

    sub8 = lax.broadcasted_iota(I32, (8, tl), 0).astype(F32)
    gl = jnp.where(sub8 < N_GROUPS, lt[0:8], NEG_INF)
    gmax = gl.max(axis=0, keepdims=True)
    gidx = jnp.min(jnp.where(gl == gmax, sub8, 8.0), axis=0, keepdims=True)
    gsum = jnp.sum(jnp.exp(gl - gmax), axis=0, keepdims=True)
    sel = lt[8:16]
    for g in range(1, N_GROUPS):
        sel = jnp.where(gidx == g, lt[8 + 8 * g:16 + 8 * g], sel)
    t1 = sel.max(axis=0, keepdims=True)
    j1 = jnp.min(jnp.where(sel == t1, sub8, 8.0), axis=0, keepdims=True)
    sel2 = jnp.where(sub8 == j1, NEG_INF, sel)
    t2 = sel2.max(axis=0, keepdims=True)
    j2 = jnp.min(jnp.where(sel2 == t2, sub8, 8.0), axis=0, keepdims=True)
    ex = jnp.exp(t2 - t1)
    den = 1.0 + ex
    gw = 1.0 / gsum
    w0 = gw * (1.0 / den)
    w1 = gw * (ex / den)
    e0 = gidx * E_PER_G + j1
    e1 = gidx * E_PER_G + j2

    sub_e = lax.broadcasted_iota(I32, (NE, tl), 0).astype(F32)
    oh0 = (sub_e == e0).astype(F32)
    oh1 = (sub_e == e1).astype(F32)
    c = oh0 + oh1
    carry = cnt_ref[:, 0:1]
    pieces = []
    for ch in range(tl // 256):
        cc = c[:, ch * 256:(ch + 1) * 256]
        pieces.append(jnp.dot(cc.astype(BF16), tri_ref[...], preferred_element_type=F32) + carry)
        carry = carry + jnp.sum(cc, axis=1, keepdims=True)
    csum = jnp.concatenate(pieces, axis=1)
    cnt_ref[...] = jnp.broadcast_to(carry, (NE, LANES))
    cnt_out_ref[...] = jnp.broadcast_to(carry, (NE, LANES))
    r0 = jnp.sum(oh0 * csum, axis=0, keepdims=True) - 1.0
    r1 = jnp.sum(oh1 * csum, axis=0, keepdims=True) - 1.0
    e_ref[0:1, :] = e0.astype(I32)
    e_ref[1:2, :] = e1.astype(I32)
    rank_ref[0:1, :] = r0.astype(I32)
    rank_ref[1:2, :] = r1.astype(I32)
    wcol_ref[0:1, :] = w0
    wcol_ref[1:2, :] = w1


def _post_kernel(o_ref, xp_ref, xs_ref, mod_ref, wo_ref, g_ref, wr_hl_ref, wr_hi_ref, tri_ref,
                 x1_ref, h_ref, e_ref, rank_ref, wcol_ref, cnt_out_ref, cnt_ref):
    i = pl.program_id(0)

    @pl.when(i == 0)
    def _():
        cnt_ref[...] = jnp.zeros_like(cnt_ref)

    x = jnp.where(i >= NT_CTX, xs_ref[...], xp_ref[...])
    m = jnp.dot(o_ref[...], wo_ref[...], preferred_element_type=F32)
    x1 = x + mod_ref[0, 2:3, :] * m
    x1_ref[...] = x1
    h = _modulate(x1, g_ref[...] * (1.0 + mod_ref[0, 4:5, :]), mod_ref[0, 3:4, :])
    h_ref[...] = _pack_rows(h)
    _route(h, wr_hl_ref, wr_hi_ref, tri_ref, cnt_ref, e_ref, rank_ref, wcol_ref, cnt_out_ref)


def _router_weights(w_group, w_expert):
    w = jnp.concatenate([w_group, jnp.zeros((D, 8 - N_GROUPS), F32),
                         jnp.transpose(w_expert, (1, 0, 2)).reshape(D, NE),
                         jnp.zeros((D, LANES - 8 - NE), F32)], axis=1)
    w_hi = w.astype(BF16)
    w_lo = (w - w_hi.astype(F32)).astype(BF16)
    return jnp.concatenate([w_hi, w_lo], axis=1), w_hi


def _route_outs(tile, first=0):
    specs = [
        pl.BlockSpec((tile, D), lambda i: (first + i, 0)),
        pl.BlockSpec((tile, HALF), lambda i: (first + i, 0)),
        pl.BlockSpec((TOP_K, tile), lambda i: (0, first + i)),
        pl.BlockSpec((TOP_K, tile), lambda i: (0, first + i)),
        pl.BlockSpec((TOP_K, tile), lambda i: (0, first + i)),
        pl.BlockSpec((NE, LANES), lambda i: (0, 0)),
    ]
    shapes = [
        jax.ShapeDtypeStruct((T, D), F32),
        jax.ShapeDtypeStruct((T, HALF), U32),
        jax.ShapeDtypeStruct((TOP_K, T), I32),
        jax.ShapeDtypeStruct((TOP_K, T), I32),
        jax.ShapeDtypeStruct((TOP_K, T), F32),
        jax.ShapeDtypeStruct((NE, LANES), F32),
    ]
    return specs, shapes


def _tri():
    a = np.arange(256)
    return jnp.asarray((a[:, None] <= a[None, :]).astype(np.float32), BF16)


def _post(o, xp, xs, mod, w_o, g_ffn, wr_hl, wr_hi):
    const = lambda shape: pl.BlockSpec(shape, lambda i: (0,) * len(shape))
    out_specs, out_shape = _route_outs(TILE)
    return pl.pallas_call(
        _post_kernel,
        grid=(NT,),
        in_specs=[
            pl.BlockSpec((TILE, DQ), lambda i: (i, 0)),
            pl.BlockSpec((TILE, D), lambda i: (jnp.minimum(i, NT_CTX - 1), 0)),
            pl.BlockSpec((TILE, D), lambda i: (jnp.maximum(i - NT_CTX, 0), 0)),
            pl.BlockSpec((1, N_MOD, D), lambda i: (_cond_index(i, TILE), 0, 0)),
            const((DQ, D)), const((1, D)), const((D, 2 * LANES)), const((D, LANES)), const((256, 256)),
        ],
        out_specs=out_specs,
        out_shape=out_shape,
        scratch_shapes=[pltpu.VMEM((NE, LANES), F32)],
        compiler_params=_vmem_limit(56),
        name="post_attn",
    )(o, xp, xs, mod, w_o, g_ffn, wr_hl, wr_hi, _tri())


def _pool_kernel(block_kinds, n_alias, y0_ref, y1_ref, xin_ref, wcin_ref, modp_ref, mod_ref, g_ref, pw_ref,
                 ps_ref, gf_ref, wr_hl_ref, wr_hi_ref, tri_ref, band_ref, inv_cnt_ref, cnt_in_ref, *refs):
    (x1_ref, h_ref, e_ref, rank_ref, wcol_ref, cnt_out_ref, cnt_ref, hhi_ref, hlo_ref) = refs[n_alias:]
    i = pl.program_id(0)

    @pl.when(i == 0)
    def _():
        cnt_ref[...] = cnt_in_ref[...]

    x = _combined(y0_ref, y1_ref, xin_ref, wcin_ref, modp_ref)
    h = _modulate(x, g_ref[...] * (1.0 + mod_ref[0, 1:2, :]), mod_ref[0, 0:1, :])
    @pl.when(i == 0)
    def _():
        zeros = jnp.zeros((PB_HALO, D), BF16)
        for ref in (hhi_ref, hlo_ref):
            ref[0:PB_HALO, :] = zeros
            ref[PB_HALO + PTILE:PB_HALO + PTILE + PB_HALO, :] = zeros

    h_hi = h.astype(BF16)
    hhi_ref[PB_HALO:PB_HALO + PTILE, :] = h_hi
    hlo_ref[PB_HALO:PB_HALO + PTILE, :] = (h - h_hi.astype(F32)).astype(BF16)

    outs = []
    for gi in range(len(POOL_WINDOWS)):
        cols = slice(gi * PGD, (gi + 1) * PGD)
        sums = []
        for b in range(PTILE // PB):
            bm = band_ref[block_kinds[b], gi]
            sums.append(jnp.dot(bm, hhi_ref[b * PB:b * PB + PBW, cols], preferred_element_type=F32)
                        + jnp.dot(bm, hlo_ref[b * PB:b * PB + PBW, cols], preferred_element_type=F32))
        diff = jnp.concatenate(sums, axis=0) * inv_cnt_ref[:, gi:gi + 1] - h[:, cols]
        outs.append(jnp.dot(diff.astype(BF16), pw_ref[gi], preferred_element_type=F32))
    m = jnp.concatenate(outs, axis=1) * ps_ref[...]
    x1 = x + mod_ref[0, 2:3, :] * m
    x1_ref[...] = x1
    hf = _modulate(x1, gf_ref[...] * (1.0 + mod_ref[0, 4:5, :]), mod_ref[0, 3:4, :])
    h_ref[...] = _pack_rows(hf)
    _route(hf, wr_hl_ref, wr_hi_ref, tri_ref, cnt_ref, e_ref, rank_ref, wcol_ref, cnt_out_ref)


def _pool_constants(seq):
    r = np.arange(PB)[:, None]
    d = np.arange(PBW)[None, :] - PB_HALO - r
    found, kinds = {}, []
    for b in range(PTILE // PB):
        tm = (b * PB + r) % seq
        inseq = (tm + d >= 0) & (tm + d < seq)
        mats = np.stack([(inseq & (d >= -(w // 2)) & (d <= w - w // 2 - 1)).astype(np.float32)
                         for w in POOL_WINDOWS])
        kinds.append(found.setdefault(mats.tobytes(), (len(found), mats))[0])
    bands = np.stack([m for _, m in sorted(found.values(), key=lambda kv: kv[0])])
    tm = np.arange(PTILE) % seq
    inv_cnt = np.zeros((PTILE, LANES), np.float32)
    for g, w in enumerate(POOL_WINDOWS):
        left, right = w // 2, w - w // 2 - 1
        inv_cnt[:, g] = 1.0 / (np.minimum(tm + right + 1, seq) - np.maximum(tm - left, 0)).astype(np.float32)
    return jnp.asarray(bands, BF16), tuple(kinds), inv_cnt


def _pool(group, y2g, x, wcol, mod_prev, mod, g_mix, pool_w, pool_scale, g_ffn, wr_hl, wr_hi, cnt_in, prev):
    const = lambda shape: pl.BlockSpec(shape, lambda i: (0,) * len(shape))
    tiles = (T_CTX if group == 0 else T_LAT) // PTILE
    first = 0 if group == 0 else T_CTX // PTILE
    out_specs, out_shape = _route_outs(PTILE, first)
    aliases = () if prev is None else tuple(prev)
    bands, block_kinds, inv_cnt = _pool_constants(SEQ if group == 0 else DEC_SEQ)
    n_fixed = 16
    return pl.pallas_call(
        functools.partial(_pool_kernel, block_kinds, len(aliases)),
        grid=(tiles,),
        in_specs=[
            pl.BlockSpec((PTILE, HALF), lambda i: (i, 0)),
            pl.BlockSpec((PTILE, HALF), lambda i: (tiles + i, 0)),
            pl.BlockSpec((PTILE, D), lambda i: (first + i, 0)),
            pl.BlockSpec((TOP_K, PTILE), lambda i: (0, first + i)),
            pl.BlockSpec((1, N_MOD, D), lambda i: (_cond_index(first + i, PTILE), 0, 0)),
            pl.BlockSpec((1, N_MOD, D), lambda i: (_cond_index(first + i, PTILE), 0, 0)),
            const((1, D)), const((len(POOL_WINDOWS), PGD, PGD)), const((1, D)), const((1, D)),
            const((D, 2 * LANES)), const((D, LANES)), const((256, 256)), const(bands.shape),
            const((PTILE, LANES)), const((NE, LANES)),
        ] + [pl.BlockSpec(memory_space=pl.ANY)] * len(aliases),
        out_specs=out_specs,
        out_shape=out_shape,
        input_output_aliases={n_fixed + k: k for k in range(len(aliases))},
        scratch_shapes=[pltpu.VMEM((NE, LANES), F32), pltpu.VMEM((PTILE + 2 * PB_HALO, D), BF16),
                        pltpu.VMEM((PTILE + 2 * PB_HALO, D), BF16)],
        compiler_params=_vmem_limit(56),
        name="pool_mixer",
    )(y2g, y2g, x, wcol, mod_prev, mod, g_mix, pool_w, pool_scale, g_ffn, wr_hl, wr_hi, _tri(), bands, inv_cnt,
      cnt_in, *aliases)


def _plan_kernel(e_ref, rank_ref, cnt_ref, pos_ref, te_ref, used_ref):
    counts = cnt_ref[:, 0:1]
    num = counts + (TM - 1)
    guess = jnp.floor(num * (1.0 / TM))
    tiles_col = (guess + jnp.where((guess + 1.0) * TM <= num, 1.0, 0.0)
                 - jnp.where(guess * TM > num, 1.0, 0.0))
    sub = lax.broadcasted_iota(I32, (NE, NE), 0)
    lane = lax.broadcasted_iota(I32, (NE, NE), 1)
    tiles_row = jnp.sum(jnp.where(sub == lane, tiles_col, 0.0), axis=0, keepdims=True)
    first_col = jnp.sum(jnp.where(lane < sub, tiles_row, 0.0), axis=1, keepdims=True)
    end_col = first_col + tiles_col
    used = jnp.sum(tiles_col, axis=0, keepdims=True)
    base_col = first_col * TM
    sub_e = lax.broadcasted_iota(I32, (NE, T), 0)
    for k in range(TOP_K):
        hit = sub_e == e_ref[k:k + 1, :]
        base = jnp.sum(jnp.where(hit, base_col, 0.0), axis=0, keepdims=True)
        pos_ref[k:k + 1, :] = base.astype(I32) + rank_ref[k:k + 1, :]
    tile = jnp.minimum(lax.broadcasted_iota(I32, (NE, LANES), 1).astype(F32), used - 1.0)
    te_ref[...] = jnp.sum(jnp.where(tile >= end_col, 1.0, 0.0), axis=0, keepdims=True).astype(I32)
    used_ref[...] = jnp.broadcast_to(used, (1, LANES)).astype(I32)


def _plan(e, rank, cnt):
    assert N_ROW_TILES <= LANES
    pos, te, used = pl.pallas_call(
        _plan_kernel,
        out_shape=[jax.ShapeDtypeStruct((TOP_K, T), I32), jax.ShapeDtypeStruct((1, LANES), I32),
                   jax.ShapeDtypeStruct((1, LANES), I32)],
        name="moe_plan",
    )(e, rank, cnt)
    return pos, te[0, :N_ROW_TILES], used[0, :1]


def _sc_mesh():
    return plsc.VectorSubcoreMesh(core_axis_name="core", subcore_axis_name="subcore")


def _sc_worker():
    return lax.axis_index("core") * SC_SUBCORES + lax.axis_index("subcore")


def _sc_dispatch(h, pos):
    per = T // SC_WORKERS
    nwin = per // SC_WIN

    @functools.partial(
        pl.kernel, out_type=jax.ShapeDtypeStruct((NPAD, HALF), U32), mesh=_sc_mesh(),
        scratch_types=[pltpu.VMEM((2, SC_WIN, HALF), U32), pltpu.VMEM((2, SC_WIN), I32),
                       pltpu.VMEM((2, SC_WIN), I32), pltpu.SemaphoreType.DMA((2,)),
                       pltpu.SemaphoreType.DMA((2,))])
    def run(h_hbm, pos_hbm, xs_hbm, buf, idx0, idx1, sem_in, sem_out):
        first = _sc_worker() * per

        def loads(j, s):
            base = first + j * SC_WIN
            return (pltpu.make_async_copy(h_hbm.at[pl.ds(base, SC_WIN)], buf.at[s], sem_in.at[s]),
                    pltpu.make_async_copy(pos_hbm.at[pl.ds(base, SC_WIN)], idx0.at[s], sem_in.at[s]),
                    pltpu.make_async_copy(pos_hbm.at[pl.ds(T + base, SC_WIN)], idx1.at[s], sem_in.at[s]))

        def stores(s):
            return (pltpu.make_async_copy(buf.at[s], xs_hbm.at[idx0.at[s]], sem_out.at[s]),
                    pltpu.make_async_copy(buf.at[s], xs_hbm.at[idx1.at[s]], sem_out.at[s]))

        for c in loads(0, 0):
            c.start()

        @pl.loop(0, nwin // 2)
        def _(jj):
            for s in range(2):
                j = jj * 2 + s
                for c in loads(j, s):
                    c.wait()
                for c in stores(s):
                    c.start()

                @pl.when(j >= 1)
                def _():
                    for c in stores(1 - s):
                        c.wait()

                @pl.when(j + 1 < nwin)
                def _():
                    for c in loads(j + 1, 1 - s):
                        c.start()

        for c in stores((nwin - 1) % 2):
            c.wait()

    return run(h, pos.reshape(TOP_K * T))


def _sc_gather(ys, pos, group):
    t0, n_g = (0, T_CTX) if group == 0 else (T_CTX, T_LAT)
    idx_all = jnp.concatenate([pos[k, t0:t0 + n_g] for k in range(TOP_K)])
    n = TOP_K * n_g
    per = n // SC_WORKERS
    nwin = per // SC_WIN

    @functools.partial(
        pl.kernel, out_type=jax.ShapeDtypeStruct((n, HALF), U32), mesh=_sc_mesh(),
        scratch_types=[pltpu.VMEM((2, SC_WIN, HALF), U32), pltpu.VMEM((per,), I32),
                       pltpu.SemaphoreType.DMA((2,)), pltpu.SemaphoreType.DMA((2,))])
    def run(ys_hbm, pos_hbm, out_hbm, buf, idx, sem_in, sem_out):
        first = _sc_worker() * per
        pltpu.sync_copy(pos_hbm.at[pl.ds(first, per)], idx)

        def load(j, s):
            return pltpu.make_async_copy(ys_hbm.at[idx.at[pl.ds(j * SC_WIN, SC_WIN)]], buf.at[s], sem_in.at[s])

        def store(j, s):
            return pltpu.make_async_copy(buf.at[s], out_hbm.at[pl.ds(first + j * SC_WIN, SC_WIN)], sem_out.at[s])

        load(0, 0).start()

        @pl.loop(0, nwin // 2)
        def _(jj):
            for s in range(2):
                j = jj * 2 + s
                load(j, s).wait()
                store(j, s).start()

                @pl.when(j >= 1)
                def _():
                    store(j - 1, 1 - s).wait()

                @pl.when(j + 1 < nwin)
                def _():
                    load(j + 1, 1 - s).start()

        store(nwin - 1, (nwin - 1) % 2).wait()

    return run(ys, idx_all)


def _moe_kernel(te_ref, used_ref, x_ref, wg_ref, wu_ref, wd_ref, y_ref):
    del te_ref
    i = pl.program_id(0)

    @pl.when(i < used_ref[0])
    def _():
        hi, lo = _unpack_rows(x_ref[...])
        x = jnp.concatenate([hi, lo], axis=1).astype(BF16)
        a = jnp.dot(x, wg_ref[0, 0], preferred_element_type=F32)
        b = jnp.dot(x, wu_ref[0, 0], preferred_element_type=F32)
        hid = a * (1.0 / (1.0 + jnp.exp(-a))) * b
        y_ref[...] = _pack_rows(jnp.dot(hid.astype(BF16), wd_ref[0, 0], preferred_element_type=F32))


def _experts(xs, tile_expert, used, layer, w_gate, w_up, w_down):
    row = lambda i, te, used: (jnp.minimum(i, used[0] - 1), 0)
    weights = lambda i, te, used: (layer, te[i], 0, 0)
    grid_spec = pltpu.PrefetchScalarGridSpec(
        num_scalar_prefetch=2,
        grid=(N_ROW_TILES,),
        in_specs=[
            pl.BlockSpec((TM, HALF), row),
            pl.BlockSpec((1, 1, D, DE), weights),
            pl.BlockSpec((1, 1, D, DE), weights),
            pl.BlockSpec((1, 1, DE, D), weights),
        ],
        out_specs=pl.BlockSpec((TM, HALF), row),
    )
    return pl.pallas_call(
        _moe_kernel,
        grid_spec=grid_spec,
        out_shape=jax.ShapeDtypeStruct((NPAD, HALF), U32),
        compiler_params=_vmem_limit(40),
        name="moe_experts",
    )(tile_expert, used, xs, w_gate, w_up, w_down)


def _combined(y0_ref, y1_ref, x_ref, wcol_ref, mod_ref):
    sub = lax.broadcasted_iota(I32, (LANES, wcol_ref.shape[1]), 0)
    wt = jnp.where(sub == 0, wcol_ref[0:1, :], jnp.where(sub == 1, wcol_ref[1:2, :], 0.0)).T
    w0 = wt[:, 0:1]
    w1 = wt[:, 1:2]
    hi0, lo0 = _unpack_rows(y0_ref[...])
    hi1, lo1 = _unpack_rows(y1_ref[...])
    y = jnp.concatenate([w0 * hi0 + w1 * hi1, w0 * lo0 + w1 * lo1], axis=1)
    return x_ref[...] + mod_ref[0, 5:6, :] * y


def _combine_kernel(y0_ref, y1_ref, x_ref, wcol_ref, mod_ref, o_ref):
    o_ref[...] = _combined(y0_ref, y1_ref, x_ref, wcol_ref, mod_ref)


def _combine(group, y2g, x, wcol, mod):
    n_g = T_CTX if group == 0 else T_LAT
    tiles = n_g // CT
    first = 0 if group == 0 else T_CTX // CT
    return pl.pallas_call(
        _combine_kernel,
        grid=(tiles,),
        in_specs=[
            pl.BlockSpec((CT, HALF), lambda i: (i, 0)),
            pl.BlockSpec((CT, HALF), lambda i: (tiles + i, 0)),
            pl.BlockSpec((CT, D), lambda i: (first + i, 0)),
            pl.BlockSpec((TOP_K, CT), lambda i: (0, first + i)),
            pl.BlockSpec((1, N_MOD, D), lambda i: (_cond_index(first + i, CT), 0, 0)),
        ],
        out_specs=pl.BlockSpec((CT, D), lambda i: (i, 0)),
        out_shape=jax.ShapeDtypeStruct((n_g, D), F32),
        name="moe_combine",
    )(y2g, y2g, x, wcol, mod)


def _moe(h, e, rank, cnt, layer, w_gate, w_up, w_down):
    pos, tile_expert, used = _plan(e, rank, cnt)
    xs = _sc_dispatch(h, pos)
    ys = _experts(xs, tile_expert, used, layer, w_gate, w_up, w_down)
    return _sc_gather(ys, pos, 0), _sc_gather(ys, pos, 1)


def kernel(x_prompt, x_sample, c, cache_k, cache_v, c_ctx, norm_mix_g, norm_ffn_g, ada_w, ada_b, attn_w_qkv, attn_q_norm, attn_k_norm, attn_w_o, pool_w, pool_scale, moe_w_group, moe_w_expert, moe_w_gate, moe_w_up, moe_w_down):
    xp = x_prompt.reshape(T_CTX, D)
    xs = x_sample.reshape(T_LAT, D)
    cond = jnp.concatenate([c_ctx[None, :], c, jnp.zeros((N_COND - 1 - DEC_BATCH, D), F32)], axis=0)
    mod = _ada(cond, ada_w, ada_b)

    qt, k2, vt, new_k, new_v = _qkv(xp, xs, mod[0], norm_mix_g[0:1], attn_w_qkv[0].astype(BF16),
                                    attn_q_norm[0:1], attn_k_norm[0:1])
    o, w_gate, w_up, w_down = _attention(qt, k2, vt, cache_k[:, 0].reshape(DEC_BATCH, PAST, DKV),
                                         cache_v[:, 0].reshape(DEC_BATCH, PAST, DKV),
                                         moe_w_gate, moe_w_up, moe_w_down)
    wr_hl, wr_hi = _router_weights(moe_w_group[0], moe_w_expert[0])
    x1, h, e, rank, wcol, cnt = _post(o, xp, xs, mod[0], attn_w_o[0].astype(BF16), norm_ffn_g[0:1],
                                      wr_hl, wr_hi)
    y2 = _moe(h, e, rank, cnt, 0, w_gate, w_up, w_down)

    wr_hl, wr_hi = _router_weights(moe_w_group[1], moe_w_expert[1])
    pool_args = (x1, wcol, mod[0], mod[1], norm_mix_g[1:2], pool_w[0].astype(BF16), pool_scale[0:1],
                 norm_ffn_g[1:2], wr_hl, wr_hi)
    outs = _pool(0, y2[0], *pool_args, jnp.zeros((NE, LANES), F32), None)
    x3, h, e, rank, wcol, cnt = _pool(1, y2[1], *pool_args, outs[5], outs[:5])
    y2 = _moe(h, e, rank, cnt, 1, w_gate, w_up, w_down)
    y_prompt = _combine(0, y2[0], x3, wcol, mod[1])
    y_sample = _combine(1, y2[1], x3, wcol, mod[1])

    return (y_prompt.reshape(BATCH, SEQ, D), y_sample.reshape(DEC_BATCH, DEC_SEQ, D),
            new_k.reshape(BATCH, 1, SEQ, N_KV, HD), new_v.reshape(BATCH, 1, SEQ, N_KV, HD))
```

```python
import functools

import jax
import jax.numpy as jnp
import numpy as np
from jax import lax
from jax.experimental import pallas as pl
from jax.experimental.pallas import tpu as pltpu
from jax.experimental.pallas import tpu_sc as plsc

F32 = jnp.float32
BF16 = jnp.bfloat16
I32 = jnp.int32
U32 = jnp.uint32

D = 1024
BATCH, SEQ = 32, 256
DEC_BATCH, DEC_SEQ, PAST = 8, 1024, 512
T_CTX = BATCH * SEQ
T_LAT = DEC_BATCH * DEC_SEQ
T = T_CTX + T_LAT
GRID_W = 64
N_HEADS, N_KV, HD = 16, 4, 64
DQ = N_HEADS * HD
DKV = N_KV * HD
ROPE_THETA = 10000.0
POOL_WINDOWS = (2, 4, 8, 16)
PGD = D // len(POOL_WINDOWS)
N_GROUPS, E_PER_G, TOP_K = 4, 8, 2
NE = N_GROUPS * E_PER_G
DE = D // 4
N_MOD = 6
EPS = 1e-6
N_COND = 16

TILE = 512
NT = T // TILE
NT_CTX = T_CTX // TILE
QB = 256
KEY_PART = 256
PTILE = 1024
PB = 128
PBW = 256
PB_HALO = (PBW - PB) // 2
TM = 1024
X_SLOTS = 3
N_ROW_TILES = (TOP_K * T) // TM + NE
NPAD = N_ROW_TILES * TM
CT = 1024
LANES = 128
NEG_INF = float("-inf")
LOG2_E = 1.4426950408889634
Q_SCALE = HD ** -0.5 * LOG2_E
HALF = D // 2
SC_CORES, SC_SUBCORES = 2, 16
SC_WORKERS = SC_CORES * SC_SUBCORES
SC_WIN = 64


def _vmem_limit(mib):
    return pltpu.CompilerParams(vmem_limit_bytes=mib * 1024 * 1024)


def _modulate(x, gain_scale, shift):
    ms = jnp.mean(x * x, axis=-1, keepdims=True)
    return (x * lax.rsqrt(ms + EPS)) * gain_scale + shift


def _head_norm(z, gain, ind, ind_t2):
    ss = jnp.dot((z * z).astype(BF16), ind, preferred_element_type=F32)
    inv = lax.rsqrt(ss * (1.0 / HD) + EPS)
    inv_hi = inv.astype(BF16)
    inv_lo = (inv - inv_hi.astype(F32)).astype(BF16)
    scale = jnp.dot(jnp.concatenate([inv_hi, inv_lo], axis=1), ind_t2, preferred_element_type=F32)
    return z * scale * gain


def _rope(z, cos_t, sin_t):
    lane = lax.broadcasted_iota(I32, (z.shape[0], LANES), 1)
    low = (lane % 32) < 16
    outs = []
    for c in range(z.shape[1] // LANES):
        zc = z[:, c * LANES:(c + 1) * LANES]
        up = pltpu.roll(zc, 16, axis=1)
        dn = pltpu.roll(zc, LANES - 16, axis=1)
        outs.append(zc * cos_t + jnp.where(low, dn, up) * sin_t)
    return jnp.concatenate(outs, axis=1)


def _rope_t(zt, cos_tt, sin_tt):
    sub = lax.broadcasted_iota(I32, (LANES, zt.shape[1]), 0)
    low = (sub % 32) < 16
    outs = []
    for c in range(zt.shape[0] // LANES):
        zc = zt[c * LANES:(c + 1) * LANES]
        up = pltpu.roll(zc, 16, axis=0)
        dn = pltpu.roll(zc, LANES - 16, axis=0)
        outs.append(zc * cos_tt + jnp.where(low, dn, up) * sin_tt)
    return jnp.concatenate(outs, axis=0)


def _pack_rows(z):
    bits = lax.bitcast_convert_type(z.astype(BF16).astype(F32), U32)
    return bits[:, :HALF] | (bits[:, HALF:] >> 16)


def _unpack_rows(p):
    hi = lax.bitcast_convert_type(p & jnp.uint32(0xFFFF0000), F32)
    lo = lax.bitcast_convert_type(p << 16, F32)
    return hi, lo


def _ada_kernel(c_ref, w_ref, b_ref, o_ref):
    c = c_ref[...]
    a = c * (1.0 / (1.0 + jnp.exp(-c)))
    o_ref[0] = jnp.dot(a.astype(BF16), w_ref[0].astype(BF16), preferred_element_type=F32) + b_ref[0]


def _ada(cond, ada_w, ada_b):
    nb = 1536
    depth = ada_w.shape[0]
    out = pl.pallas_call(
        _ada_kernel,
        grid=(depth, (N_MOD * D) // nb),
        in_specs=[
            pl.BlockSpec((N_COND, D), lambda l, j: (0, 0)),
            pl.BlockSpec((1, D, nb), lambda l, j: (l, 0, j)),
            pl.BlockSpec((1, 1, nb), lambda l, j: (l, 0, j)),
        ],
        out_specs=pl.BlockSpec((1, N_COND, nb), lambda l, j: (l, 0, j)),
        out_shape=jax.ShapeDtypeStruct((depth, N_COND, N_MOD * D), F32),
        compiler_params=_vmem_limit(40),
        name="ada",
    )(cond, ada_w, ada_b.reshape(depth, 1, N_MOD * D))
    return out.reshape(depth, N_COND, N_MOD, D)


def _cond_index(i, tile):
    n_ctx = T_CTX // tile
    per_batch = DEC_SEQ // tile
    return jnp.where(i < n_ctx, 0, 1 + (i - n_ctx) // per_batch)


def _qkv_kernel(xp_ref, xs_ref, mod_ref, g_ref, w_ref, qg_ref, kg_ref, indq_ref, indqt_ref,
                indk_ref, indkt_ref, cos_ref, sin_ref, cos_t_ref, sin_t_ref,
                qt_ref, k2_ref, vt_ref, nk_ref, nv_ref):
    i = pl.program_id(0)
    is_lat = i >= NT_CTX
    gain_scale = g_ref[...] * (1.0 + mod_ref[0, 1:2, :])
    subs = [slice(b * QB, (b + 1) * QB) for b in range(TILE // QB)]
    hs = [_modulate(jnp.where(is_lat, xs_ref[r, :], xp_ref[r, :]), gain_scale, mod_ref[0, 0:1, :]).astype(BF16)
          for r in subs]
    qkvs = [jnp.dot(h, w_ref[...], preferred_element_type=F32) for h in hs]
    qs = [_head_norm(z[:, :DQ], qg_ref[...], indq_ref[...], indqt_ref[...]) for z in qkvs]
    ks = [_head_norm(z[:, DQ:DQ + DKV], kg_ref[...], indk_ref[...], indkt_ref[...]) for z in qkvs]
    vs = [z[:, DQ + DKV:] for z in qkvs]
    qts = [(q * Q_SCALE).T for q in qs]
    for r, v in zip(subs, vs):
        vt_ref[:, r] = v.T.astype(BF16)

    @pl.when(is_lat)
    def _():
        for b, r in enumerate(subs):
            qt_ref[b] = _rope_t(qts[b], cos_t_ref[:, r], sin_t_ref[:, r]).astype(BF16)
            k2_ref[r, :] = _dup_heads(_rope(ks[b], cos_ref[r, :], sin_ref[r, :])).astype(BF16)

    @pl.when(jnp.logical_not(is_lat))
    def _():
        for b, r in enumerate(subs):
            qt_ref[b] = qts[b].astype(BF16)
            k2_ref[r, :] = _dup_heads(ks[b]).astype(BF16)
            nk_ref[r, :] = ks[b]
            nv_ref[r, :] = vs[b]


def _rope_tables():
    rows = DEC_SEQ // GRID_W
    row = np.broadcast_to(np.arange(rows, dtype=np.float32)[:, None], (rows, GRID_W)).reshape(-1)
    col = np.broadcast_to(np.arange(GRID_W, dtype=np.float32)[None, :], (rows, GRID_W)).reshape(-1)
    axis_dim = HD // 2
    inv_freq = np.power(np.float32(ROPE_THETA),
                        -np.arange(0, axis_dim, 2, dtype=np.float32) / np.float32(axis_dim))
    ang = np.concatenate([row[:, None] * inv_freq, col[:, None] * inv_freq], axis=-1).astype(np.float32)
    cos, sin = np.cos(ang), np.sin(ang)
    quarter = HD // 4
    cos_h = np.concatenate([cos[:, :quarter], cos[:, :quarter], cos[:, quarter:], cos[:, quarter:]], axis=1)
    sin_h = np.concatenate([-sin[:, :quarter], sin[:, :quarter], -sin[:, quarter:], sin[:, quarter:]], axis=1)
    return (np.tile(cos_h, (1, LANES // HD)).astype(np.float32),
            np.tile(sin_h, (1, LANES // HD)).astype(np.float32))


def _head_indicators(width):
    col = np.arange(width)[:, None]
    head = np.arange(LANES)[None, :]
    ind = (col // HD == head).astype(np.float32)
    return jnp.asarray(ind, BF16), jnp.asarray(np.concatenate([ind.T, ind.T], axis=0), BF16)


def _dup_heads(z):
    lane = lax.broadcasted_iota(I32, (z.shape[0], LANES), 1)
    cols = []
    for p in range(N_KV // 2):
        blk = z[:, p * LANES:(p + 1) * LANES]
        swp = pltpu.roll(blk, HD, axis=1)
        cols += [jnp.where(lane < HD, blk, swp), jnp.where(lane < HD, swp, blk)]
    return jnp.concatenate(cols, axis=1)


def _qkv(xp, xs, mod, g, w_qkv, q_gain, k_gain):
    indq, indqt = _head_indicators(DQ)
    indk, indkt = _head_indicators(DKV)
    cos_t, sin_t = _rope_tables()
    per_batch = DEC_SEQ // TILE
    const = lambda shape: pl.BlockSpec(shape, lambda i: (0,) * len(shape))
    return pl.pallas_call(
        _qkv_kernel,
        grid=(NT,),
        in_specs=[
            pl.BlockSpec((TILE, D), lambda i: (jnp.minimum(i, NT_CTX - 1), 0)),
            pl.BlockSpec((TILE, D), lambda i: (jnp.maximum(i - NT_CTX, 0), 0)),
            pl.BlockSpec((1, N_MOD, D), lambda i: (_cond_index(i, TILE), 0, 0)),
            const((1, D)),
            const((D, DQ + 2 * DKV)),
            const((1, DQ)), const((1, DKV)),
            const((DQ, LANES)), const((2 * LANES, DQ)),
            const((DKV, LANES)), const((2 * LANES, DKV)),
            pl.BlockSpec((TILE, LANES), lambda i: (jnp.maximum(i - NT_CTX, 0) % per_batch, 0)),
            pl.BlockSpec((TILE, LANES), lambda i: (jnp.maximum(i - NT_CTX, 0) % per_batch, 0)),
            pl.BlockSpec((LANES, TILE), lambda i: (0, jnp.maximum(i - NT_CTX, 0) % per_batch)),
            pl.BlockSpec((LANES, TILE), lambda i: (0, jnp.maximum(i - NT_CTX, 0) % per_batch)),
        ],
        out_specs=[
            pl.BlockSpec((TILE // QB, DQ, QB), lambda i: (i, 0, 0)),
            pl.BlockSpec((TILE, 2 * DKV), lambda i: (i, 0)),
            pl.BlockSpec((DKV, TILE), lambda i: (0, i)),
            pl.BlockSpec((TILE, DKV), lambda i: (jnp.minimum(i, NT_CTX - 1), 0)),
            pl.BlockSpec((TILE, DKV), lambda i: (jnp.minimum(i, NT_CTX - 1), 0)),
        ],
        out_shape=[
            jax.ShapeDtypeStruct((T // QB, DQ, QB), BF16),
            jax.ShapeDtypeStruct((T, 2 * DKV), BF16),
            jax.ShapeDtypeStruct((DKV, T), BF16),
            jax.ShapeDtypeStruct((T_CTX, DKV), F32),
            jax.ShapeDtypeStruct((T_CTX, DKV), F32),
        ],
        compiler_params=_vmem_limit(56),
        name="qkv",
    )(xp, xs, mod, g, w_qkv, jnp.tile(q_gain, (1, N_HEADS)), jnp.tile(k_gain, (1, N_KV)),
      indq, indqt, indk, indkt, cos_t, sin_t, cos_t.T, sin_t.T)


def _unit_scores(u, pairs, k2_parts, qt_ref):
    cols = []
    for j in range(pairs):
        pair = u * pairs + j
        qt_pair = qt_ref[pair * LANES:(pair + 1) * LANES, :]
        sub = lax.broadcasted_iota(I32, qt_pair.shape, 0)
        zero = jnp.zeros_like(qt_pair)
        cols += [jnp.where(sub < HD, qt_pair, zero), jnp.where(sub >= HD, qt_pair, zero)]
    qtu = jnp.concatenate(cols, axis=1)
    return [jnp.dot(k2, qtu, preferred_element_type=F32) for k2 in k2_parts]


def _softmax_values(ss, vt_parts):
    m = ss[0].max(axis=0, keepdims=True)
    for s in ss[1:]:
        m = jnp.maximum(m, s.max(axis=0, keepdims=True))
    acc = None
    l = None
    for s, vt in zip(ss, vt_parts):
        e = jnp.exp2(s - m)
        ls = e.sum(axis=0, keepdims=True)
        l = ls if l is None else l + ls
        t = jnp.dot(vt, e.astype(BF16), preferred_element_type=F32)
        acc = t if acc is None else acc + t
    return acc * (1.0 / l)


def _attend(keys_of, values_of, qt_ref, o_ref, pairs, ahead):
    n_units = N_HEADS // (2 * pairs)
    group = lambda u: (u * 2 * pairs) // (N_HEADS // N_KV)
    scores = lambda u: _unit_scores(u, pairs, keys_of(group(u)), qt_ref)
    pending = [scores(u) for u in range(min(ahead, n_units))]
    outs = []
    for u in range(n_units):
        if u + ahead < n_units:
            pending.append(scores(u + ahead))
        outs.append(_softmax_values(pending.pop(0), values_of(group(u))))
    lq = o_ref.shape[0]
    heads = [o[:, h * lq:(h + 1) * lq] for o in outs for h in range(2 * pairs)]
    o_ref[...] = jnp.concatenate(heads, axis=0).T.astype(BF16)


def _attn_ctx_kernel(qt_ref, k2_ref, vt_ref, o_ref):
    _attend(lambda g: [k2_ref[:, g * LANES:(g + 1) * LANES]],
            lambda g: [vt_ref[g * HD:(g + 1) * HD, :]], qt_ref.at[0], o_ref, pairs=2, ahead=1)


def _attn_lat_kernel(qt_ref, k2c_ref, vtc_ref, k2n_ref, vtn_ref, wg_ref, wu_ref, wd_ref, o_in_ref,
                     o_ref, wgb_ref, wub_ref, wdb_ref):
    del o_in_ref
    wgb_ref[...] = wg_ref[...].astype(BF16)
    wub_ref[...] = wu_ref[...].astype(BF16)
    wdb_ref[...] = wd_ref[...].astype(BF16)
    pc = [slice(h * KEY_PART, (h + 1) * KEY_PART) for h in range(PAST // KEY_PART)]
    pn = [slice(h * KEY_PART, (h + 1) * KEY_PART) for h in range(DEC_SEQ // KEY_PART)]
    _attend(lambda g: [k2c_ref[0, r, g * LANES:(g + 1) * LANES] for r in pc]
            + [k2n_ref[r, g * LANES:(g + 1) * LANES] for r in pn],
            lambda g: [vtc_ref[0, g * HD:(g + 1) * HD, r] for r in pc]
            + [vtn_ref[g * HD:(g + 1) * HD, r] for r in pn],
            qt_ref.at[0], o_ref, pairs=1, ahead=2)


def _cache_kernel(ck_ref, cv_ref, k2_ref, vt_ref):
    k2_ref[0] = _dup_heads(ck_ref[0]).astype(BF16)
    vt_ref[0] = cv_ref[0].T.astype(BF16)


def _attention(qt, k2, vt, cache_k, cache_v, w_gate, w_up, w_down):
    k2c, vtc = pl.pallas_call(
        _cache_kernel,
        grid=(DEC_BATCH,),
        in_specs=[
            pl.BlockSpec((1, PAST, DKV), lambda b: (b, 0, 0)),
            pl.BlockSpec((1, PAST, DKV), lambda b: (b, 0, 0)),
        ],
        out_specs=[
            pl.BlockSpec((1, PAST, 2 * DKV), lambda b: (b, 0, 0)),
            pl.BlockSpec((1, DKV, PAST), lambda b: (b, 0, 0)),
        ],
        out_shape=[
            jax.ShapeDtypeStruct((DEC_BATCH, PAST, 2 * DKV), BF16),
            jax.ShapeDtypeStruct((DEC_BATCH, DKV, PAST), BF16),
        ],
        name="cache_prep",
    )(cache_k, cache_v)

    o_ctx = pl.pallas_call(
        _attn_ctx_kernel,
        grid=(BATCH,),
        in_specs=[
            pl.BlockSpec((1, DQ, QB), lambda b: (b, 0, 0)),
            pl.BlockSpec((SEQ, 2 * DKV), lambda b: (b, 0)),
            pl.BlockSpec((DKV, SEQ), lambda b: (0, b)),
        ],
        out_specs=pl.BlockSpec((SEQ, DQ), lambda b: (b, 0)),
        out_shape=jax.ShapeDtypeStruct((T, DQ), BF16),
        name="attn_ctx",
    )(qt, k2, vt)

    qb = QB
    nqb = DEC_SEQ // qb
    lat0 = T_CTX // qb
    depth = w_gate.shape[0]
    eps = (depth * NE) // (DEC_BATCH * nqb)
    per_layer = NE // eps
    wblock = lambda b, j: ((b * nqb + j) // per_layer, (b * nqb + j) % per_layer, 0, 0)
    return pl.pallas_call(
        _attn_lat_kernel,
        grid=(DEC_BATCH, nqb),
        in_specs=[
            pl.BlockSpec((1, DQ, QB), lambda b, j: (lat0 + b * nqb + j, 0, 0)),
            pl.BlockSpec((1, PAST, 2 * DKV), lambda b, j: (b, 0, 0)),
            pl.BlockSpec((1, DKV, PAST), lambda b, j: (b, 0, 0)),
            pl.BlockSpec((DEC_SEQ, 2 * DKV), lambda b, j: (T_CTX // DEC_SEQ + b, 0)),
            pl.BlockSpec((DKV, DEC_SEQ), lambda b, j: (0, T_CTX // DEC_SEQ + b)),
            pl.BlockSpec((1, eps, D, DE), wblock),
            pl.BlockSpec((1, eps, D, DE), wblock),
            pl.BlockSpec((1, eps, DE, D), wblock),
            pl.BlockSpec(memory_space=pl.ANY),
        ],
        out_specs=[
            pl.BlockSpec((qb, DQ), lambda b, j: (lat0 + b * nqb + j, 0)),
            pl.BlockSpec((1, eps, D, DE), wblock),
            pl.BlockSpec((1, eps, D, DE), wblock),
            pl.BlockSpec((1, eps, DE, D), wblock),
        ],
        out_shape=[
            jax.ShapeDtypeStruct((T, DQ), BF16),
            jax.ShapeDtypeStruct(w_gate.shape, BF16),
            jax.ShapeDtypeStruct(w_up.shape, BF16),
            jax.ShapeDtypeStruct(w_down.shape, BF16),
        ],
        input_output_aliases={8: 0},
        compiler_params=_vmem_limit(56),
        name="attn_lat",
    )(qt, k2c, vtc, k2, vt, w_gate, w_up, w_down, o_ctx)


def _route(h, wr_hl_ref, wr_hi_ref, tri_ref, cnt_ref, e_ref, rank_ref, wcol_ref, cnt_out_ref):
    tl = h.shape[0]
    h_hi = h.astype(BF16)
    h_lo = (h - h_hi.astype(F32)).astype(BF16)
    lg2 = jnp.dot(h_hi, wr_hl_ref[...], preferred_element_type=F32)
    lg = lg2[:, :LANES] + lg2[:, LANES:] + jnp.dot(h_lo, wr_hi_ref[...], preferred_element_type=F32)
    lt = lg.T

    sub8 = lax.broadcasted_iota(I32, (8, tl), 0).astype(F32)
    gl = jnp.where(sub8 < N_GROUPS, lt[0:8], NEG_INF)
    gmax = gl.max(axis=0, keepdims=True)
    gidx = jnp.min(jnp.where(gl == gmax, sub8, 8.0), axis=0, keepdims=True)
    gsum = jnp.sum(jnp.exp(gl - gmax), axis=0, keepdims=True)
    sel = lt[8:16]
    for g in range(1, N_GROUPS):
        sel = jnp.where(gidx == g, lt[8 + 8 * g:16 + 8 * g], sel)
    t1 = sel.max(axis=0, keepdims=True)
    j1 = jnp.min(jnp.where(sel == t1, sub8, 8.0), axis=0, keepdims=True)
    sel2 = jnp.where(sub8 == j1, NEG_INF, sel)
    t2 = sel2.max(axis=0, keepdims=True)
    j2 = jnp.min(jnp.where(sel2 == t2, sub8, 8.0), axis=0, keepdims=True)
    ex = jnp.exp(t2 - t1)
    den = 1.0 + ex
    gw = 1.0 / gsum
    w0 = gw * (1.0 / den)
    w1 = gw * (ex / den)
    e0 = gidx * E_PER_G + j1
    e1 = gidx * E_PER_G + j2

    sub_e = lax.broadcasted_iota(I32, (NE, tl), 0).astype(F32)
    oh0 = (sub_e == e0).astype(F32)
    oh1 = (sub_e == e1).astype(F32)
    c = oh0 + oh1
    carry = cnt_ref[:, 0:1]
    pieces = []
    for ch in range(tl // 256):
        cc = c[:, ch * 256:(ch + 1) * 256]
        pieces.append(jnp.dot(cc.astype(BF16), tri_ref[...], preferred_element_type=F32) + carry)
        carry = carry + jnp.sum(cc, axis=1, keepdims=True)
    csum = jnp.concatenate(pieces, axis=1)
    cnt_ref[...] = jnp.broadcast_to(carry, (NE, LANES))
    cnt_out_ref[...] = jnp.broadcast_to(carry, (NE, LANES))
    r0 = jnp.sum(oh0 * csum, axis=0, keepdims=True) - 1.0
    r1 = jnp.sum(oh1 * csum, axis=0, keepdims=True) - 1.0
    e_ref[0:1, :] = e0.astype(I32)
    e_ref[1:2, :] = e1.astype(I32)
    rank_ref[0:1, :] = r0.astype(I32)
    rank_ref[1:2, :] = r1.astype(I32)
    wcol_ref[0:1, :] = w0
    wcol_ref[1:2, :] = w1


def _post_kernel(o_ref, xp_ref, xs_ref, mod_ref, wo_ref, g_ref, wr_hl_ref, wr_hi_ref, tri_ref,
                 x1_ref, h_ref, e_ref, rank_ref, wcol_ref, cnt_out_ref, cnt_ref):
    i = pl.program_id(0)

    @pl.when(i == 0)
    def _():
        cnt_ref[...] = jnp.zeros_like(cnt_ref)

    x = jnp.where(i >= NT_CTX, xs_ref[...], xp_ref[...])
    m = jnp.dot(o_ref[...], wo_ref[...], preferred_element_type=F32)
    x1 = x + mod_ref[0, 2:3, :] * m
    x1_ref[...] = x1
    h = _modulate(x1, g_ref[...] * (1.0 + mod_ref[0, 4:5, :]), mod_ref[0, 3:4, :])
    h_ref[...] = _pack_rows(h)
    _route(h, wr_hl_ref, wr_hi_ref, tri_ref, cnt_ref, e_ref, rank_ref, wcol_ref, cnt_out_ref)


def _router_weights(w_group, w_expert):
    w = jnp.concatenate([w_group, jnp.zeros((D, 8 - N_GROUPS), F32),
                         jnp.transpose(w_expert, (1, 0, 2)).reshape(D, NE),
                         jnp.zeros((D, LANES - 8 - NE), F32)], axis=1)
    w_hi = w.astype(BF16)
    w_lo = (w - w_hi.astype(F32)).astype(BF16)
    return jnp.concatenate([w_hi, w_lo], axis=1), w_hi


def _route_outs(tile, first=0):
    specs = [
        pl.BlockSpec((tile, D), lambda i: (first + i, 0)),
        pl.BlockSpec((tile, HALF), lambda i: (first + i, 0)),
        pl.BlockSpec((TOP_K, tile), lambda i: (0, first + i)),
        pl.BlockSpec((TOP_K, tile), lambda i: (0, first + i)),
        pl.BlockSpec((TOP_K, tile), lambda i: (0, first + i)),
        pl.BlockSpec((NE, LANES), lambda i: (0, 0)),
    ]
    shapes = [
        jax.ShapeDtypeStruct((T, D), F32),
        jax.ShapeDtypeStruct((T, HALF), U32),
        jax.ShapeDtypeStruct((TOP_K, T), I32),
        jax.ShapeDtypeStruct((TOP_K, T), I32),
        jax.ShapeDtypeStruct((TOP_K, T), F32),
        jax.ShapeDtypeStruct((NE, LANES), F32),
    ]
    return specs, shapes


def _tri():
    a = np.arange(256)
    return jnp.asarray((a[:, None] <= a[None, :]).astype(np.float32), BF16)


def _post(o, xp, xs, mod, w_o, g_ffn, wr_hl, wr_hi):
    const = lambda shape: pl.BlockSpec(shape, lambda i: (0,) * len(shape))
    out_specs, out_shape = _route_outs(TILE)
    return pl.pallas_call(
        _post_kernel,
        grid=(NT,),
        in_specs=[
            pl.BlockSpec((TILE, DQ), lambda i: (i, 0)),
            pl.BlockSpec((TILE, D), lambda i: (jnp.minimum(i, NT_CTX - 1), 0)),
            pl.BlockSpec((TILE, D), lambda i: (jnp.maximum(i - NT_CTX, 0), 0)),
            pl.BlockSpec((1, N_MOD, D), lambda i: (_cond_index(i, TILE), 0, 0)),
            const((DQ, D)), const((1, D)), const((D, 2 * LANES)), const((D, LANES)), const((256, 256)),
        ],
        out_specs=out_specs,
        out_shape=out_shape,
        scratch_shapes=[pltpu.VMEM((NE, LANES), F32)],
        compiler_params=_vmem_limit(56),
        name="post_attn",
    )(o, xp, xs, mod, w_o, g_ffn, wr_hl, wr_hi, _tri())


def _pool_kernel(block_kinds, n_alias, y0_ref, y1_ref, xin_ref, wcin_ref, modp_ref, mod_ref, g_ref, pw_ref,
                 ps_ref, gf_ref, wr_hl_ref, wr_hi_ref, tri_ref, band_ref, inv_cnt_ref, cnt_in_ref, *refs):
    (x1_ref, h_ref, e_ref, rank_ref, wcol_ref, cnt_out_ref, cnt_ref, hhi_ref, hlo_ref) = refs[n_alias:]
    i = pl.program_id(0)

    @pl.when(i == 0)
    def _():
        cnt_ref[...] = cnt_in_ref[...]

    x = _combined(y0_ref, y1_ref, xin_ref, wcin_ref, modp_ref)
    h = _modulate(x, g_ref[...] * (1.0 + mod_ref[0, 1:2, :]), mod_ref[0, 0:1, :])
    @pl.when(i == 0)
    def _():
        zeros = jnp.zeros((PB_HALO, D), BF16)
        for ref in (hhi_ref, hlo_ref):
            ref[0:PB_HALO, :] = zeros
            ref[PB_HALO + PTILE:PB_HALO + PTILE + PB_HALO, :] = zeros

    h_hi = h.astype(BF16)
    hhi_ref[PB_HALO:PB_HALO + PTILE, :] = h_hi
    hlo_ref[PB_HALO:PB_HALO + PTILE, :] = (h - h_hi.astype(F32)).astype(BF16)

    outs = []
    for gi in range(len(POOL_WINDOWS)):
        cols = slice(gi * PGD, (gi + 1) * PGD)
        sums = []
        for b in range(PTILE // PB):
            bm = band_ref[block_kinds[b], gi]
            sums.append(jnp.dot(bm, hhi_ref[b * PB:b * PB + PBW, cols], preferred_element_type=F32)
                        + jnp.dot(bm, hlo_ref[b * PB:b * PB + PBW, cols], preferred_element_type=F32))
        diff = jnp.concatenate(sums, axis=0) * inv_cnt_ref[:, gi:gi + 1] - h[:, cols]
        outs.append(jnp.dot(diff.astype(BF16), pw_ref[gi], preferred_element_type=F32))
    m = jnp.concatenate(outs, axis=1) * ps_ref[...]
    x1 = x + mod_ref[0, 2:3, :] * m
    x1_ref[...] = x1
    hf = _modulate(x1, gf_ref[...] * (1.0 + mod_ref[0, 4:5, :]), mod_ref[0, 3:4, :])
    h_ref[...] = _pack_rows(hf)
    _route(hf, wr_hl_ref, wr_hi_ref, tri_ref, cnt_ref, e_ref, rank_ref, wcol_ref, cnt_out_ref)


def _pool_constants(seq):
    r = np.arange(PB)[:, None]
    d = np.arange(PBW)[None, :] - PB_HALO - r
    found, kinds = {}, []
    for b in range(PTILE // PB):
        tm = (b * PB + r) % seq
        inseq = (tm + d >= 0) & (tm + d < seq)
        mats = np.stack([(inseq & (d >= -(w // 2)) & (d <= w - w // 2 - 1)).astype(np.float32)
                         for w in POOL_WINDOWS])
        kinds.append(found.setdefault(mats.tobytes(), (len(found), mats))[0])
    bands = np.stack([m for _, m in sorted(found.values(), key=lambda kv: kv[0])])
    tm = np.arange(PTILE) % seq
    inv_cnt = np.zeros((PTILE, LANES), np.float32)
    for g, w in enumerate(POOL_WINDOWS):
        left, right = w // 2, w - w // 2 - 1
        inv_cnt[:, g] = 1.0 / (np.minimum(tm + right + 1, seq) - np.maximum(tm - left, 0)).astype(np.float32)
    return jnp.asarray(bands, BF16), tuple(kinds), inv_cnt


def _pool(group, y2g, x, wcol, mod_prev, mod, g_mix, pool_w, pool_scale, g_ffn, wr_hl, wr_hi, cnt_in, prev):
    const = lambda shape: pl.BlockSpec(shape, lambda i: (0,) * len(shape))
    tiles = (T_CTX if group == 0 else T_LAT) // PTILE
    first = 0 if group == 0 else T_CTX // PTILE
    out_specs, out_shape = _route_outs(PTILE, first)
    aliases = () if prev is None else tuple(prev)
    bands, block_kinds, inv_cnt = _pool_constants(SEQ if group == 0 else DEC_SEQ)
    n_fixed = 16
    return pl.pallas_call(
        functools.partial(_pool_kernel, block_kinds, len(aliases)),
        grid=(tiles,),
        in_specs=[
            pl.BlockSpec((PTILE, HALF), lambda i: (i, 0)),
            pl.BlockSpec((PTILE, HALF), lambda i: (tiles + i, 0)),
            pl.BlockSpec((PTILE, D), lambda i: (first + i, 0)),
            pl.BlockSpec((TOP_K, PTILE), lambda i: (0, first + i)),
            pl.BlockSpec((1, N_MOD, D), lambda i: (_cond_index(first + i, PTILE), 0, 0)),
            pl.BlockSpec((1, N_MOD, D), lambda i: (_cond_index(first + i, PTILE), 0, 0)),
            const((1, D)), const((len(POOL_WINDOWS), PGD, PGD)), const((1, D)), const((1, D)),
            const((D, 2 * LANES)), const((D, LANES)), const((256, 256)), const(bands.shape),
            const((PTILE, LANES)), const((NE, LANES)),
        ] + [pl.BlockSpec(memory_space=pl.ANY)] * len(aliases),
        out_specs=out_specs,
        out_shape=out_shape,
        input_output_aliases={n_fixed + k: k for k in range(len(aliases))},
        scratch_shapes=[pltpu.VMEM((NE, LANES), F32), pltpu.VMEM((PTILE + 2 * PB_HALO, D), BF16),
                        pltpu.VMEM((PTILE + 2 * PB_HALO, D), BF16)],
        compiler_params=_vmem_limit(56),
        name="pool_mixer",
    )(y2g, y2g, x, wcol, mod_prev, mod, g_mix, pool_w, pool_scale, g_ffn, wr_hl, wr_hi, _tri(), bands, inv_cnt,
      cnt_in, *aliases)


def _plan_kernel(e_ref, rank_ref, cnt_ref, pos_ref, te_ref, used_ref):
    counts = cnt_ref[:, 0:1]
    tiles_col = jnp.floor((counts + (TM - 1)) * (1.0 / TM))
    sub = lax.broadcasted_iota(I32, (NE, NE), 0)
    lane = lax.broadcasted_iota(I32, (NE, NE), 1)
    tiles_row = jnp.sum(jnp.where(sub == lane, tiles_col, 0.0), axis=0, keepdims=True)
    first_col = jnp.sum(jnp.where(lane < sub, tiles_row, 0.0), axis=1, keepdims=True)
    end_col = first_col + tiles_col
    used = jnp.sum(tiles_col, axis=0, keepdims=True)
    base_col = first_col * TM
    sub_e = lax.broadcasted_iota(I32, (NE, T), 0)
    for k in range(TOP_K):
        hit = sub_e == e_ref[k:k + 1, :]
        base = jnp.sum(jnp.where(hit, base_col, 0.0), axis=0, keepdims=True)
        pos_ref[k:k + 1, :] = base.astype(I32) + rank_ref[k:k + 1, :]
    tile = jnp.minimum(lax.broadcasted_iota(I32, (NE, LANES), 1).astype(F32), used - 1.0)
    te_ref[...] = jnp.sum(jnp.where(tile >= end_col, 1.0, 0.0), axis=0, keepdims=True).astype(I32)
    used_ref[...] = jnp.broadcast_to(used, (1, LANES)).astype(I32)


def _plan(e, rank, cnt):
    assert N_ROW_TILES <= LANES
    pos, te, used = pl.pallas_call(
        _plan_kernel,
        out_shape=[jax.ShapeDtypeStruct((TOP_K, T), I32), jax.ShapeDtypeStruct((1, LANES), I32),
                   jax.ShapeDtypeStruct((1, LANES), I32)],
        name="moe_plan",
    )(e, rank, cnt)
    return pos, te[0, :N_ROW_TILES], used[0, :1]


def _sc_mesh():
    return plsc.VectorSubcoreMesh(core_axis_name="core", subcore_axis_name="subcore")


def _sc_worker():
    return lax.axis_index("core") * SC_SUBCORES + lax.axis_index("subcore")


def _sc_dispatch(h, pos):
    per = T // SC_WORKERS
    nwin = per // SC_WIN

    @functools.partial(
        pl.kernel, out_type=jax.ShapeDtypeStruct((NPAD, HALF), U32), mesh=_sc_mesh(),
        scratch_types=[pltpu.VMEM((2, SC_WIN, HALF), U32), pltpu.VMEM((2, SC_WIN), I32),
                       pltpu.VMEM((2, SC_WIN), I32), pltpu.SemaphoreType.DMA((2,)),
                       pltpu.SemaphoreType.DMA((2,))])
    def run(h_hbm, pos_hbm, xs_hbm, buf, idx0, idx1, sem_in, sem_out):
        first = _sc_worker() * per

        def loads(j, s):
            base = first + j * SC_WIN
            return (pltpu.make_async_copy(h_hbm.at[pl.ds(base, SC_WIN)], buf.at[s], sem_in.at[s]),
                    pltpu.make_async_copy(pos_hbm.at[pl.ds(base, SC_WIN)], idx0.at[s], sem_in.at[s]),
                    pltpu.make_async_copy(pos_hbm.at[pl.ds(T + base, SC_WIN)], idx1.at[s], sem_in.at[s]))

        def stores(s):
            return (pltpu.make_async_copy(buf.at[s], xs_hbm.at[idx0.at[s]], sem_out.at[s]),
                    pltpu.make_async_copy(buf.at[s], xs_hbm.at[idx1.at[s]], sem_out.at[s]))

        for c in loads(0, 0):
            c.start()

        @pl.loop(0, nwin // 2)
        def _(jj):
            for s in range(2):
                j = jj * 2 + s
                for c in loads(j, s):
                    c.wait()
                for c in stores(s):
                    c.start()

                @pl.when(j >= 1)
                def _():
                    for c in stores(1 - s):
                        c.wait()

                @pl.when(j + 1 < nwin)
                def _():
                    for c in loads(j + 1, 1 - s):
                        c.start()

        for c in stores((nwin - 1) % 2):
            c.wait()

    return run(h, pos.reshape(TOP_K * T))


def _sc_gather(ys, pos, group):
    t0, n_g = (0, T_CTX) if group == 0 else (T_CTX, T_LAT)
    idx_all = jnp.concatenate([pos[k, t0:t0 + n_g] for k in range(TOP_K)])
    n = TOP_K * n_g
    per = n // SC_WORKERS
    nwin = per // SC_WIN

    @functools.partial(
        pl.kernel, out_type=jax.ShapeDtypeStruct((n, HALF), U32), mesh=_sc_mesh(),
        scratch_types=[pltpu.VMEM((2, SC_WIN, HALF), U32), pltpu.VMEM((per,), I32),
                       pltpu.SemaphoreType.DMA((2,)), pltpu.SemaphoreType.DMA((2,))])
    def run(ys_hbm, pos_hbm, out_hbm, buf, idx, sem_in, sem_out):
        first = _sc_worker() * per
        pltpu.sync_copy(pos_hbm.at[pl.ds(first, per)], idx)

        def load(j, s):
            return pltpu.make_async_copy(ys_hbm.at[idx.at[pl.ds(j * SC_WIN, SC_WIN)]], buf.at[s], sem_in.at[s])

        def store(j, s):
            return pltpu.make_async_copy(buf.at[s], out_hbm.at[pl.ds(first + j * SC_WIN, SC_WIN)], sem_out.at[s])

        load(0, 0).start()

        @pl.loop(0, nwin // 2)
        def _(jj):
            for s in range(2):
                j = jj * 2 + s
                load(j, s).wait()
                store(j, s).start()

                @pl.when(j >= 1)
                def _():
                    store(j - 1, 1 - s).wait()

                @pl.when(j + 1 < nwin)
                def _():
                    load(j + 1, 1 - s).start()

        store(nwin - 1, (nwin - 1) % 2).wait()

    return run(ys, idx_all)


def _moe_kernel(te_ref, used_ref, x_hbm, wg_ref, wu_ref, wd_ref, y_ref, xbuf, sem):
    del te_ref
    i = pl.program_id(0)
    used = used_ref[0]

    def tile_copy(t):
        slot = lax.rem(t, X_SLOTS)
        return pltpu.make_async_copy(x_hbm.at[pl.ds(t * TM, TM)], xbuf.at[slot], sem.at[slot])

    @pl.when(i == 0)
    def _():
        for t in range(X_SLOTS - 1):
            @pl.when(t < used)
            def _():
                tile_copy(t).start()

    @pl.when(i + (X_SLOTS - 1) < used)
    def _():
        tile_copy(i + (X_SLOTS - 1)).start()

    @pl.when(i < used)
    def _():
        tile_copy(i).wait()
        hi, lo = _unpack_rows(xbuf[lax.rem(i, X_SLOTS)])
        x = jnp.concatenate([hi, lo], axis=1).astype(BF16)
        a = jnp.dot(x, wg_ref[0, 0], preferred_element_type=F32)
        b = jnp.dot(x, wu_ref[0, 0], preferred_element_type=F32)
        hid = a * (1.0 / (1.0 + jnp.exp(-a))) * b
        y_ref[...] = _pack_rows(jnp.dot(hid.astype(BF16), wd_ref[0, 0], preferred_element_type=F32))


def _experts(xs, tile_expert, used, layer, w_gate, w_up, w_down):
    row = lambda i, te, used: (jnp.minimum(i, used[0] - 1), 0)
    weights = lambda i, te, used: (layer, te[i], 0, 0)
    grid_spec = pltpu.PrefetchScalarGridSpec(
        num_scalar_prefetch=2,
        grid=(N_ROW_TILES,),
        in_specs=[
            pl.BlockSpec(memory_space=pl.ANY),
            pl.BlockSpec((1, 1, D, DE), weights),
            pl.BlockSpec((1, 1, D, DE), weights),
            pl.BlockSpec((1, 1, DE, D), weights),
        ],
        out_specs=pl.BlockSpec((TM, HALF), row),
        scratch_shapes=[pltpu.VMEM((X_SLOTS, TM, HALF), U32), pltpu.SemaphoreType.DMA((X_SLOTS,))],
    )
    return pl.pallas_call(
        _moe_kernel,
        grid_spec=grid_spec,
        out_shape=jax.ShapeDtypeStruct((NPAD, HALF), U32),
        compiler_params=pltpu.CompilerParams(dimension_semantics=("arbitrary",),
                                             vmem_limit_bytes=40 * 1024 * 1024),
        name="moe_experts",
    )(tile_expert, used, xs, w_gate, w_up, w_down)


def _combined(y0_ref, y1_ref, x_ref, wcol_ref, mod_ref):
    sub = lax.broadcasted_iota(I32, (LANES, wcol_ref.shape[1]), 0)
    wt = jnp.where(sub == 0, wcol_ref[0:1, :], jnp.where(sub == 1, wcol_ref[1:2, :], 0.0)).T
    w0 = wt[:, 0:1]
    w1 = wt[:, 1:2]
    hi0, lo0 = _unpack_rows(y0_ref[...])
    hi1, lo1 = _unpack_rows(y1_ref[...])
    y = jnp.concatenate([w0 * hi0 + w1 * hi1, w0 * lo0 + w1 * lo1], axis=1)
    return x_ref[...] + mod_ref[0, 5:6, :] * y


def _combine_kernel(y0_ref, y1_ref, x_ref, wcol_ref, mod_ref, o_ref):
    o_ref[...] = _combined(y0_ref, y1_ref, x_ref, wcol_ref, mod_ref)


def _combine(group, y2g, x, wcol, mod):
    n_g = T_CTX if group == 0 else T_LAT
    tiles = n_g // CT
    first = 0 if group == 0 else T_CTX // CT
    return pl.pallas_call(
        _combine_kernel,
        grid=(tiles,),
        in_specs=[
            pl.BlockSpec((CT, HALF), lambda i: (i, 0)),
            pl.BlockSpec((CT, HALF), lambda i: (tiles + i, 0)),
            pl.BlockSpec((CT, D), lambda i: (first + i, 0)),
            pl.BlockSpec((TOP_K, CT), lambda i: (0, first + i)),
            pl.BlockSpec((1, N_MOD, D), lambda i: (_cond_index(first + i, CT), 0, 0)),
        ],
        out_specs=pl.BlockSpec((CT, D), lambda i: (i, 0)),
        out_shape=jax.ShapeDtypeStruct((n_g, D), F32),
        name="moe_combine",
    )(y2g, y2g, x, wcol, mod)


def _moe(h, e, rank, cnt, layer, w_gate, w_up, w_down):
    pos, tile_expert, used = _plan(e, rank, cnt)
    xs = _sc_dispatch(h, pos)
    ys = _experts(xs, tile_expert, used, layer, w_gate, w_up, w_down)
    return _sc_gather(ys, pos, 0), _sc_gather(ys, pos, 1)


def kernel(x_prompt, x_sample, c, cache_k, cache_v, c_ctx, norm_mix_g, norm_ffn_g, ada_w, ada_b, attn_w_qkv, attn_q_norm, attn_k_norm, attn_w_o, pool_w, pool_scale, moe_w_group, moe_w_expert, moe_w_gate, moe_w_up, moe_w_down):
    xp = x_prompt.reshape(T_CTX, D)
    xs = x_sample.reshape(T_LAT, D)
    cond = jnp.concatenate([c_ctx[None, :], c, jnp.zeros((N_COND - 1 - DEC_BATCH, D), F32)], axis=0)
    mod = _ada(cond, ada_w, ada_b)

    qt, k2, vt, new_k, new_v = _qkv(xp, xs, mod[0], norm_mix_g[0:1], attn_w_qkv[0].astype(BF16),
                                    attn_q_norm[0:1], attn_k_norm[0:1])
    o, w_gate, w_up, w_down = _attention(qt, k2, vt, cache_k[:, 0].reshape(DEC_BATCH, PAST, DKV),
                                         cache_v[:, 0].reshape(DEC_BATCH, PAST, DKV),
                                         moe_w_gate, moe_w_up, moe_w_down)
    wr_hl, wr_hi = _router_weights(moe_w_group[0], moe_w_expert[0])
    x1, h, e, rank, wcol, cnt = _post(o, xp, xs, mod[0], attn_w_o[0].astype(BF16), norm_ffn_g[0:1],
                                      wr_hl, wr_hi)
    y2 = _moe(h, e, rank, cnt, 0, w_gate, w_up, w_down)

    wr_hl, wr_hi = _router_weights(moe_w_group[1], moe_w_expert[1])
    pool_args = (x1, wcol, mod[0], mod[1], norm_mix_g[1:2], pool_w[0].astype(BF16), pool_scale[0:1],
                 norm_ffn_g[1:2], wr_hl, wr_hi)
    outs = _pool(0, y2[0], *pool_args, jnp.zeros((NE, LANES), F32), None)
    x3, h, e, rank, wcol, cnt = _pool(1, y2[1], *pool_args, outs[5], outs[:5])
    y2 = _moe(h, e, rank, cnt, 1, w_gate, w_up, w_down)
    y_prompt = _combine(0, y2[0], x3, wcol, mod[1])
    y_sample = _combine(1, y2[1], x3, wcol, mod[1])

    return (y_prompt.reshape(BATCH, SEQ, D), y_sample.reshape(DEC_BATCH, DEC_SEQ, D),
            new_k.reshape(BATCH, 1, SEQ, N_KV, HD), new_v.reshape(BATCH, 1, SEQ, N_KV, HD))
```

```python
import functools

import jax
import jax.numpy as jnp
import numpy as np
from jax import lax
from jax.experimental import pallas as pl
from jax.experimental.pallas import tpu as pltpu
from jax.experimental.pallas import tpu_sc as plsc

F32 = jnp.float32
BF16 = jnp.bfloat16
I32 = jnp.int32
U32 = jnp.uint32

D = 1024
BATCH, SEQ = 32, 256
DEC_BATCH, DEC_SEQ, PAST = 8, 1024, 512
T_CTX = BATCH * SEQ
T_LAT = DEC_BATCH * DEC_SEQ
T = T_CTX + T_LAT
GRID_W = 64
N_HEADS, N_KV, HD = 16, 4, 64
DQ = N_HEADS * HD
DKV = N_KV * HD
ROPE_THETA = 10000.0
POOL_WINDOWS = (2, 4, 8, 16)
PGD = D // len(POOL_WINDOWS)
N_GROUPS, E_PER_G, TOP_K = 4, 8, 2
NE = N_GROUPS * E_PER_G
DE = D // 4
N_MOD = 6
EPS = 1e-6
N_COND = 16

TILE = 512
NT = T // TILE
NT_CTX = T_CTX // TILE
QB = 256
KEY_PART = 256
PTILE = 1024
PB = 128
PBW = 256
PB_HALO = (PBW - PB) // 2
TM = 1152
X_SLOTS = 3
N_ROW_TILES = (TOP_K * T) // TM + NE
NPAD = N_ROW_TILES * TM
CT = 1024
LANES = 128
NEG_INF = float("-inf")
LOG2_E = 1.4426950408889634
Q_SCALE = HD ** -0.5 * LOG2_E
HALF = D // 2
SC_CORES, SC_SUBCORES = 2, 16
SC_WORKERS = SC_CORES * SC_SUBCORES
SC_WIN = 64


def _vmem_limit(mib):
    return pltpu.CompilerParams(vmem_limit_bytes=mib * 1024 * 1024)


def _modulate(x, gain_scale, shift):
    ms = jnp.mean(x * x, axis=-1, keepdims=True)
    return (x * lax.rsqrt(ms + EPS)) * gain_scale + shift


def _head_norm(z, gain, ind, ind_t2):
    ss = jnp.dot((z * z).astype(BF16), ind, preferred_element_type=F32)
    inv = lax.rsqrt(ss * (1.0 / HD) + EPS)
    inv_hi = inv.astype(BF16)
    inv_lo = (inv - inv_hi.astype(F32)).astype(BF16)
    scale = jnp.dot(jnp.concatenate([inv_hi, inv_lo], axis=1), ind_t2, preferred_element_type=F32)
    return z * scale * gain


def _rope(z, cos_t, sin_t):
    lane = lax.broadcasted_iota(I32, (z.shape[0], LANES), 1)
    low = (lane % 32) < 16
    outs = []
    for c in range(z.shape[1] // LANES):
        zc = z[:, c * LANES:(c + 1) * LANES]
        up = pltpu.roll(zc, 16, axis=1)
        dn = pltpu.roll(zc, LANES - 16, axis=1)
        outs.append(zc * cos_t + jnp.where(low, dn, up) * sin_t)
    return jnp.concatenate(outs, axis=1)


def _rope_t(zt, cos_tt, sin_tt):
    sub = lax.broadcasted_iota(I32, (LANES, zt.shape[1]), 0)
    low = (sub % 32) < 16
    outs = []
    for c in range(zt.shape[0] // LANES):
        zc = zt[c * LANES:(c + 1) * LANES]
        up = pltpu.roll(zc, 16, axis=0)
        dn = pltpu.roll(zc, LANES - 16, axis=0)
        outs.append(zc * cos_tt + jnp.where(low, dn, up) * sin_tt)
    return jnp.concatenate(outs, axis=0)


def _pack_rows(z):
    bits = lax.bitcast_convert_type(z.astype(BF16).astype(F32), U32)
    return bits[:, :HALF] | (bits[:, HALF:] >> 16)


def _unpack_rows(p):
    hi = lax.bitcast_convert_type(p & jnp.uint32(0xFFFF0000), F32)
    lo = lax.bitcast_convert_type(p << 16, F32)
    return hi, lo


def _ada_kernel(c_ref, w_ref, b_ref, o_ref):
    c = c_ref[...]
    a = c * (1.0 / (1.0 + jnp.exp(-c)))
    o_ref[0] = jnp.dot(a.astype(BF16), w_ref[0].astype(BF16), preferred_element_type=F32) + b_ref[0]


def _ada(cond, ada_w, ada_b):
    nb = 1536
    depth = ada_w.shape[0]
    out = pl.pallas_call(
        _ada_kernel,
        grid=(depth, (N_MOD * D) // nb),
        in_specs=[
            pl.BlockSpec((N_COND, D), lambda l, j: (0, 0)),
            pl.BlockSpec((1, D, nb), lambda l, j: (l, 0, j)),
            pl.BlockSpec((1, 1, nb), lambda l, j: (l, 0, j)),
        ],
        out_specs=pl.BlockSpec((1, N_COND, nb), lambda l, j: (l, 0, j)),
        out_shape=jax.ShapeDtypeStruct((depth, N_COND, N_MOD * D), F32),
        compiler_params=_vmem_limit(40),
        name="ada",
    )(cond, ada_w, ada_b.reshape(depth, 1, N_MOD * D))
    return out.reshape(depth, N_COND, N_MOD, D)


def _cond_index(i, tile):
    n_ctx = T_CTX // tile
    per_batch = DEC_SEQ // tile
    return jnp.where(i < n_ctx, 0, 1 + (i - n_ctx) // per_batch)


def _qkv_kernel(xp_ref, xs_ref, mod_ref, g_ref, w_ref, qg_ref, kg_ref, indq_ref, indqt_ref,
                indk_ref, indkt_ref, cos_ref, sin_ref, cos_t_ref, sin_t_ref,
                qt_ref, k2_ref, vt_ref, nk_ref, nv_ref):
    i = pl.program_id(0)
    is_lat = i >= NT_CTX
    gain_scale = g_ref[...] * (1.0 + mod_ref[0, 1:2, :])
    subs = [slice(b * QB, (b + 1) * QB) for b in range(TILE // QB)]
    hs = [_modulate(jnp.where(is_lat, xs_ref[r, :], xp_ref[r, :]), gain_scale, mod_ref[0, 0:1, :]).astype(BF16)
          for r in subs]
    qkvs = [jnp.dot(h, w_ref[...], preferred_element_type=F32) for h in hs]
    qs = [_head_norm(z[:, :DQ], qg_ref[...], indq_ref[...], indqt_ref[...]) for z in qkvs]
    ks = [_head_norm(z[:, DQ:DQ + DKV], kg_ref[...], indk_ref[...], indkt_ref[...]) for z in qkvs]
    vs = [z[:, DQ + DKV:] for z in qkvs]
    qts = [(q * Q_SCALE).T for q in qs]
    for r, v in zip(subs, vs):
        vt_ref[:, r] = v.T.astype(BF16)

    @pl.when(is_lat)
    def _():
        for b, r in enumerate(subs):
            qt_ref[b] = _rope_t(qts[b], cos_t_ref[:, r], sin_t_ref[:, r]).astype(BF16)
            k2_ref[r, :] = _dup_heads(_rope(ks[b], cos_ref[r, :], sin_ref[r, :])).astype(BF16)

    @pl.when(jnp.logical_not(is_lat))
    def _():
        for b, r in enumerate(subs):
            qt_ref[b] = qts[b].astype(BF16)
            k2_ref[r, :] = _dup_heads(ks[b]).astype(BF16)
            nk_ref[r, :] = ks[b]
            nv_ref[r, :] = vs[b]


def _rope_tables():
    rows = DEC_SEQ // GRID_W
    row = np.broadcast_to(np.arange(rows, dtype=np.float32)[:, None], (rows, GRID_W)).reshape(-1)
    col = np.broadcast_to(np.arange(GRID_W, dtype=np.float32)[None, :], (rows, GRID_W)).reshape(-1)
    axis_dim = HD // 2
    inv_freq = np.power(np.float32(ROPE_THETA),
                        -np.arange(0, axis_dim, 2, dtype=np.float32) / np.float32(axis_dim))
    ang = np.concatenate([row[:, None] * inv_freq, col[:, None] * inv_freq], axis=-1).astype(np.float32)
    cos, sin = np.cos(ang), np.sin(ang)
    quarter = HD // 4
    cos_h = np.concatenate([cos[:, :quarter], cos[:, :quarter], cos[:, quarter:], cos[:, quarter:]], axis=1)
    sin_h = np.concatenate([-sin[:, :quarter], sin[:, :quarter], -sin[:, quarter:], sin[:, quarter:]], axis=1)
    return (np.tile(cos_h, (1, LANES // HD)).astype(np.float32),
            np.tile(sin_h, (1, LANES // HD)).astype(np.float32))


def _head_indicators(width):
    col = np.arange(width)[:, None]
    head = np.arange(LANES)[None, :]
    ind = (col // HD == head).astype(np.float32)
    return jnp.asarray(ind, BF16), jnp.asarray(np.concatenate([ind.T, ind.T], axis=0), BF16)


def _dup_heads(z):
    lane = lax.broadcasted_iota(I32, (z.shape[0], LANES), 1)
    cols = []
    for p in range(N_KV // 2):
        blk = z[:, p * LANES:(p + 1) * LANES]
        swp = pltpu.roll(blk, HD, axis=1)
        cols += [jnp.where(lane < HD, blk, swp), jnp.where(lane < HD, swp, blk)]
    return jnp.concatenate(cols, axis=1)


def _qkv(xp, xs, mod, g, w_qkv, q_gain, k_gain):
    indq, indqt = _head_indicators(DQ)
    indk, indkt = _head_indicators(DKV)
    cos_t, sin_t = _rope_tables()
    per_batch = DEC_SEQ // TILE
    const = lambda shape: pl.BlockSpec(shape, lambda i: (0,) * len(shape))
    return pl.pallas_call(
        _qkv_kernel,
        grid=(NT,),
        in_specs=[
            pl.BlockSpec((TILE, D), lambda i: (jnp.minimum(i, NT_CTX - 1), 0)),
            pl.BlockSpec((TILE, D), lambda i: (jnp.maximum(i - NT_CTX, 0), 0)),
            pl.BlockSpec((1, N_MOD, D), lambda i: (_cond_index(i, TILE), 0, 0)),
            const((1, D)),
            const((D, DQ + 2 * DKV)),
            const((1, DQ)), const((1, DKV)),
            const((DQ, LANES)), const((2 * LANES, DQ)),
            const((DKV, LANES)), const((2 * LANES, DKV)),
            pl.BlockSpec((TILE, LANES), lambda i: (jnp.maximum(i - NT_CTX, 0) % per_batch, 0)),
            pl.BlockSpec((TILE, LANES), lambda i: (jnp.maximum(i - NT_CTX, 0) % per_batch, 0)),
            pl.BlockSpec((LANES, TILE), lambda i: (0, jnp.maximum(i - NT_CTX, 0) % per_batch)),
            pl.BlockSpec((LANES, TILE), lambda i: (0, jnp.maximum(i - NT_CTX, 0) % per_batch)),
        ],
        out_specs=[
            pl.BlockSpec((TILE // QB, DQ, QB), lambda i: (i, 0, 0)),
            pl.BlockSpec((TILE, 2 * DKV), lambda i: (i, 0)),
            pl.BlockSpec((DKV, TILE), lambda i: (0, i)),
            pl.BlockSpec((TILE, DKV), lambda i: (jnp.minimum(i, NT_CTX - 1), 0)),
            pl.BlockSpec((TILE, DKV), lambda i: (jnp.minimum(i, NT_CTX - 1), 0)),
        ],
        out_shape=[
            jax.ShapeDtypeStruct((T // QB, DQ, QB), BF16),
            jax.ShapeDtypeStruct((T, 2 * DKV), BF16),
            jax.ShapeDtypeStruct((DKV, T), BF16),
            jax.ShapeDtypeStruct((T_CTX, DKV), F32),
            jax.ShapeDtypeStruct((T_CTX, DKV), F32),
        ],
        compiler_params=_vmem_limit(56),
        name="qkv",
    )(xp, xs, mod, g, w_qkv, jnp.tile(q_gain, (1, N_HEADS)), jnp.tile(k_gain, (1, N_KV)),
      indq, indqt, indk, indkt, cos_t, sin_t, cos_t.T, sin_t.T)


def _unit_scores(u, pairs, k2_parts, qt_ref):
    cols = []
    for j in range(pairs):
        pair = u * pairs + j
        qt_pair = qt_ref[pair * LANES:(pair + 1) * LANES, :]
        sub = lax.broadcasted_iota(I32, qt_pair.shape, 0)
        zero = jnp.zeros_like(qt_pair)
        cols += [jnp.where(sub < HD, qt_pair, zero), jnp.where(sub >= HD, qt_pair, zero)]
    qtu = jnp.concatenate(cols, axis=1)
    return [jnp.dot(k2, qtu, preferred_element_type=F32) for k2 in k2_parts]


def _softmax_values(ss, vt_parts):
    m = ss[0].max(axis=0, keepdims=True)
    for s in ss[1:]:
        m = jnp.maximum(m, s.max(axis=0, keepdims=True))
    acc = None
    l = None
    for s, vt in zip(ss, vt_parts):
        e = jnp.exp2(s - m)
        ls = e.sum(axis=0, keepdims=True)
        l = ls if l is None else l + ls
        t = jnp.dot(vt, e.astype(BF16), preferred_element_type=F32)
        acc = t if acc is None else acc + t
    return acc * (1.0 / l)


def _attend(keys_of, values_of, qt_ref, o_ref, pairs, ahead):
    n_units = N_HEADS // (2 * pairs)
    group = lambda u: (u * 2 * pairs) // (N_HEADS // N_KV)
    scores = lambda u: _unit_scores(u, pairs, keys_of(group(u)), qt_ref)
    pending = [scores(u) for u in range(min(ahead, n_units))]
    outs = []
    for u in range(n_units):
        if u + ahead < n_units:
            pending.append(scores(u + ahead))
        outs.append(_softmax_values(pending.pop(0), values_of(group(u))))
    lq = o_ref.shape[0]
    heads = [o[:, h * lq:(h + 1) * lq] for o in outs for h in range(2 * pairs)]
    o_ref[...] = jnp.concatenate(heads, axis=0).T.astype(BF16)


def _attn_ctx_kernel(qt_ref, k2_ref, vt_ref, o_ref):
    _attend(lambda g: [k2_ref[:, g * LANES:(g + 1) * LANES]],
            lambda g: [vt_ref[g * HD:(g + 1) * HD, :]], qt_ref.at[0], o_ref, pairs=2, ahead=1)


def _attn_lat_kernel(qt_ref, k2c_ref, vtc_ref, k2n_ref, vtn_ref, wg_ref, wu_ref, wd_ref, o_in_ref,
                     o_ref, wgb_ref, wub_ref, wdb_ref):
    del o_in_ref
    wgb_ref[...] = wg_ref[...].astype(BF16)
    wub_ref[...] = wu_ref[...].astype(BF16)
    wdb_ref[...] = wd_ref[...].astype(BF16)
    pc = [slice(h * KEY_PART, (h + 1) * KEY_PART) for h in range(PAST // KEY_PART)]
    pn = [slice(h * KEY_PART, (h + 1) * KEY_PART) for h in range(DEC_SEQ // KEY_PART)]
    _attend(lambda g: [k2c_ref[0, r, g * LANES:(g + 1) * LANES] for r in pc]
            + [k2n_ref[r, g * LANES:(g + 1) * LANES] for r in pn],
            lambda g: [vtc_ref[0, g * HD:(g + 1) * HD, r] for r in pc]
            + [vtn_ref[g * HD:(g + 1) * HD, r] for r in pn],
            qt_ref.at[0], o_ref, pairs=1, ahead=2)


def _cache_kernel(ck_ref, cv_ref, k2_ref, vt_ref):
    k2_ref[0] = _dup_heads(ck_ref[0]).astype(BF16)
    vt_ref[0] = cv_ref[0].T.astype(BF16)


def _attention(qt, k2, vt, cache_k, cache_v, w_gate, w_up, w_down):
    k2c, vtc = pl.pallas_call(
        _cache_kernel,
        grid=(DEC_BATCH,),
        in_specs=[
            pl.BlockSpec((1, PAST, DKV), lambda b: (b, 0, 0)),
            pl.BlockSpec((1, PAST, DKV), lambda b: (b, 0, 0)),
        ],
        out_specs=[
            pl.BlockSpec((1, PAST, 2 * DKV), lambda b: (b, 0, 0)),
            pl.BlockSpec((1, DKV, PAST), lambda b: (b, 0, 0)),
        ],
        out_shape=[
            jax.ShapeDtypeStruct((DEC_BATCH, PAST, 2 * DKV), BF16),
            jax.ShapeDtypeStruct((DEC_BATCH, DKV, PAST), BF16),
        ],
        name="cache_prep",
    )(cache_k, cache_v)

    o_ctx = pl.pallas_call(
        _attn_ctx_kernel,
        grid=(BATCH,),
        in_specs=[
            pl.BlockSpec((1, DQ, QB), lambda b: (b, 0, 0)),
            pl.BlockSpec((SEQ, 2 * DKV), lambda b: (b, 0)),
            pl.BlockSpec((DKV, SEQ), lambda b: (0, b)),
        ],
        out_specs=pl.BlockSpec((SEQ, DQ), lambda b: (b, 0)),
        out_shape=jax.ShapeDtypeStruct((T, DQ), BF16),
        name="attn_ctx",
    )(qt, k2, vt)

    qb = QB
    nqb = DEC_SEQ // qb
    lat0 = T_CTX // qb
    depth = w_gate.shape[0]
    eps = (depth * NE) // (DEC_BATCH * nqb)
    per_layer = NE // eps
    wblock = lambda b, j: ((b * nqb + j) // per_layer, (b * nqb + j) % per_layer, 0, 0)
    return pl.pallas_call(
        _attn_lat_kernel,
        grid=(DEC_BATCH, nqb),
        in_specs=[
            pl.BlockSpec((1, DQ, QB), lambda b, j: (lat0 + b * nqb + j, 0, 0)),
            pl.BlockSpec((1, PAST, 2 * DKV), lambda b, j: (b, 0, 0)),
            pl.BlockSpec((1, DKV, PAST), lambda b, j: (b, 0, 0)),
            pl.BlockSpec((DEC_SEQ, 2 * DKV), lambda b, j: (T_CTX // DEC_SEQ + b, 0)),
            pl.BlockSpec((DKV, DEC_SEQ), lambda b, j: (0, T_CTX // DEC_SEQ + b)),
            pl.BlockSpec((1, eps, D, DE), wblock),
            pl.BlockSpec((1, eps, D, DE), wblock),
            pl.BlockSpec((1, eps, DE, D), wblock),
            pl.BlockSpec(memory_space=pl.ANY),
        ],
        out_specs=[
            pl.BlockSpec((qb, DQ), lambda b, j: (lat0 + b * nqb + j, 0)),
            pl.BlockSpec((1, eps, D, DE), wblock),
            pl.BlockSpec((1, eps, D, DE), wblock),
            pl.BlockSpec((1, eps, DE, D), wblock),
        ],
        out_shape=[
            jax.ShapeDtypeStruct((T, DQ), BF16),
            jax.ShapeDtypeStruct(w_gate.shape, BF16),
            jax.ShapeDtypeStruct(w_up.shape, BF16),
            jax.ShapeDtypeStruct(w_down.shape, BF16),
        ],
        input_output_aliases={8: 0},
        compiler_params=_vmem_limit(56),
        name="attn_lat",
    )(qt, k2c, vtc, k2, vt, w_gate, w_up, w_down, o_ctx)


def _route(h, wr_hl_ref, wr_hi_ref, tri_ref, cnt_ref, e_ref, rank_ref, wcol_ref, cnt_out_ref):
    tl = h.shape[0]
    h_hi = h.astype(BF16)
    h_lo = (h - h_hi.astype(F32)).astype(BF16)
    lg2 = jnp.dot(h_hi, wr_hl_ref[...], preferred_element_type=F32)
    lg = lg2[:, :LANES] + lg2[:, LANES:] + jnp.dot(h_lo, wr_hi_ref[...], preferred_element_type=F32)
    lt = lg.T

    sub8 = lax.broadcasted_iota(I32, (8, tl), 0).astype(F32)
    gl = jnp.where(sub8 < N_GROUPS, lt[0:8], NEG_INF)
    gmax = gl.max(axis=0, keepdims=True)
    gidx = jnp.min(jnp.where(gl == gmax, sub8, 8.0), axis=0, keepdims=True)
    gsum = jnp.sum(jnp.exp(gl - gmax), axis=0, keepdims=True)
    sel = lt[8:16]
    for g in range(1, N_GROUPS):
        sel = jnp.where(gidx == g, lt[8 + 8 * g:16 + 8 * g], sel)
    t1 = sel.max(axis=0, keepdims=True)
    j1 = jnp.min(jnp.where(sel == t1, sub8, 8.0), axis=0, keepdims=True)
    sel2 = jnp.where(sub8 == j1, NEG_INF, sel)
    t2 = sel2.max(axis=0, keepdims=True)
    j2 = jnp.min(jnp.where(sel2 == t2, sub8, 8.0), axis=0, keepdims=True)
    ex = jnp.exp(t2 - t1)
    den = 1.0 + ex
    gw = 1.0 / gsum
    w0 = gw * (1.0 / den)
    w1 = gw * (ex / den)
    e0 = gidx * E_PER_G + j1
    e1 = gidx * E_PER_G + j2

    sub_e = lax.broadcasted_iota(I32, (NE, tl), 0).astype(F32)
    oh0 = (sub_e == e0).astype(F32)
    oh1 = (sub_e == e1).astype(F32)
    c = oh0 + oh1
    carry = cnt_ref[:, 0:1]
    pieces = []
    for ch in range(tl // 256):
        cc = c[:, ch * 256:(ch + 1) * 256]
        pieces.append(jnp.dot(cc.astype(BF16), tri_ref[...], preferred_element_type=F32) + carry)
        carry = carry + jnp.sum(cc, axis=1, keepdims=True)
    csum = jnp.concatenate(pieces, axis=1)
    cnt_ref[...] = jnp.broadcast_to(carry, (NE, LANES))
    cnt_out_ref[...] = jnp.broadcast_to(carry, (NE, LANES))
    r0 = jnp.sum(oh0 * csum, axis=0, keepdims=True) - 1.0
    r1 = jnp.sum(oh1 * csum, axis=0, keepdims=True) - 1.0
    e_ref[0:1, :] = e0.astype(I32)
    e_ref[1:2, :] = e1.astype(I32)
    rank_ref[0:1, :] = r0.astype(I32)
    rank_ref[1:2, :] = r1.astype(I32)
    wcol_ref[0:1, :] = w0
    wcol_ref[1:2, :] = w1


def _post_kernel(o_ref, xp_ref, xs_ref, mod_ref, wo_ref, g_ref, wr_hl_ref, wr_hi_ref, tri_ref,
                 x1_ref, h_ref, e_ref, rank_ref, wcol_ref, cnt_out_ref, cnt_ref):
    i = pl.program_id(0)

    @pl.when(i == 0)
    def _():
        cnt_ref[...] = jnp.zeros_like(cnt_ref)

    x = jnp.where(i >= NT_CTX, xs_ref[...], xp_ref[...])
    m = jnp.dot(o_ref[...], wo_ref[...], preferred_element_type=F32)
    x1 = x + mod_ref[0, 2:3, :] * m
    x1_ref[...] = x1
    h = _modulate(x1, g_ref[...] * (1.0 + mod_ref[0, 4:5, :]), mod_ref[0, 3:4, :])
    h_ref[...] = _pack_rows(h)
    _route(h, wr_hl_ref, wr_hi_ref, tri_ref, cnt_ref, e_ref, rank_ref, wcol_ref, cnt_out_ref)


def _router_weights(w_group, w_expert):
    w = jnp.concatenate([w_group, jnp.zeros((D, 8 - N_GROUPS), F32),
                         jnp.transpose(w_expert, (1, 0, 2)).reshape(D, NE),
                         jnp.zeros((D, LANES - 8 - NE), F32)], axis=1)
    w_hi = w.astype(BF16)
    w_lo = (w - w_hi.astype(F32)).astype(BF16)
    return jnp.concatenate([w_hi, w_lo], axis=1), w_hi


def _route_outs(tile, first=0):
    specs = [
        pl.BlockSpec((tile, D), lambda i: (first + i, 0)),
        pl.BlockSpec((tile, HALF), lambda i: (first + i, 0)),
        pl.BlockSpec((TOP_K, tile), lambda i: (0, first + i)),
        pl.BlockSpec((TOP_K, tile), lambda i: (0, first + i)),
        pl.BlockSpec((TOP_K, tile), lambda i: (0, first + i)),
        pl.BlockSpec((NE, LANES), lambda i: (0, 0)),
    ]
    shapes = [
        jax.ShapeDtypeStruct((T, D), F32),
        jax.ShapeDtypeStruct((T, HALF), U32),
        jax.ShapeDtypeStruct((TOP_K, T), I32),
        jax.ShapeDtypeStruct((TOP_K, T), I32),
        jax.ShapeDtypeStruct((TOP_K, T), F32),
        jax.ShapeDtypeStruct((NE, LANES), F32),
    ]
    return specs, shapes


def _tri():
    a = np.arange(256)
    return jnp.asarray((a[:, None] <= a[None, :]).astype(np.float32), BF16)


def _post(o, xp, xs, mod, w_o, g_ffn, wr_hl, wr_hi):
    const = lambda shape: pl.BlockSpec(shape, lambda i: (0,) * len(shape))
    out_specs, out_shape = _route_outs(TILE)
    return pl.pallas_call(
        _post_kernel,
        grid=(NT,),
        in_specs=[
            pl.BlockSpec((TILE, DQ), lambda i: (i, 0)),
            pl.BlockSpec((TILE, D), lambda i: (jnp.minimum(i, NT_CTX - 1), 0)),
            pl.BlockSpec((TILE, D), lambda i: (jnp.maximum(i - NT_CTX, 0), 0)),
            pl.BlockSpec((1, N_MOD, D), lambda i: (_cond_index(i, TILE), 0, 0)),
            const((DQ, D)), const((1, D)), const((D, 2 * LANES)), const((D, LANES)), const((256, 256)),
        ],
        out_specs=out_specs,
        out_shape=out_shape,
        scratch_shapes=[pltpu.VMEM((NE, LANES), F32)],
        compiler_params=_vmem_limit(56),
        name="post_attn",
    )(o, xp, xs, mod, w_o, g_ffn, wr_hl, wr_hi, _tri())


def _pool_kernel(block_kinds, n_alias, y0_ref, y1_ref, xin_ref, wcin_ref, modp_ref, mod_ref, g_ref, pw_ref,
                 ps_ref, gf_ref, wr_hl_ref, wr_hi_ref, tri_ref, band_ref, inv_cnt_ref, cnt_in_ref, *refs):
    (x1_ref, h_ref, e_ref, rank_ref, wcol_ref, cnt_out_ref, cnt_ref, hhi_ref, hlo_ref) = refs[n_alias:]
    i = pl.program_id(0)

    @pl.when(i == 0)
    def _():
        cnt_ref[...] = cnt_in_ref[...]

    x = _combined(y0_ref, y1_ref, xin_ref, wcin_ref, modp_ref)
    h = _modulate(x, g_ref[...] * (1.0 + mod_ref[0, 1:2, :]), mod_ref[0, 0:1, :])
    @pl.when(i == 0)
    def _():
        zeros = jnp.zeros((PB_HALO, D), BF16)
        for ref in (hhi_ref, hlo_ref):
            ref[0:PB_HALO, :] = zeros
            ref[PB_HALO + PTILE:PB_HALO + PTILE + PB_HALO, :] = zeros

    h_hi = h.astype(BF16)
    hhi_ref[PB_HALO:PB_HALO + PTILE, :] = h_hi
    hlo_ref[PB_HALO:PB_HALO + PTILE, :] = (h - h_hi.astype(F32)).astype(BF16)

    outs = []
    for gi in range(len(POOL_WINDOWS)):
        cols = slice(gi * PGD, (gi + 1) * PGD)
        sums = []
        for b in range(PTILE // PB):
            bm = band_ref[block_kinds[b], gi]
            sums.append(jnp.dot(bm, hhi_ref[b * PB:b * PB + PBW, cols], preferred_element_type=F32)
                        + jnp.dot(bm, hlo_ref[b * PB:b * PB + PBW, cols], preferred_element_type=F32))
        diff = jnp.concatenate(sums, axis=0) * inv_cnt_ref[:, gi:gi + 1] - h[:, cols]
        outs.append(jnp.dot(diff.astype(BF16), pw_ref[gi], preferred_element_type=F32))
    m = jnp.concatenate(outs, axis=1) * ps_ref[...]
    x1 = x + mod_ref[0, 2:3, :] * m
    x1_ref[...] = x1
    hf = _modulate(x1, gf_ref[...] * (1.0 + mod_ref[0, 4:5, :]), mod_ref[0, 3:4, :])
    h_ref[...] = _pack_rows(hf)
    _route(hf, wr_hl_ref, wr_hi_ref, tri_ref, cnt_ref, e_ref, rank_ref, wcol_ref, cnt_out_ref)


def _pool_constants(seq):
    r = np.arange(PB)[:, None]
    d = np.arange(PBW)[None, :] - PB_HALO - r
    found, kinds = {}, []
    for b in range(PTILE // PB):
        tm = (b * PB + r) % seq
        inseq = (tm + d >= 0) & (tm + d < seq)
        mats = np.stack([(inseq & (d >= -(w // 2)) & (d <= w - w // 2 - 1)).astype(np.float32)
                         for w in POOL_WINDOWS])
        kinds.append(found.setdefault(mats.tobytes(), (len(found), mats))[0])
    bands = np.stack([m for _, m in sorted(found.values(), key=lambda kv: kv[0])])
    tm = np.arange(PTILE) % seq
    inv_cnt = np.zeros((PTILE, LANES), np.float32)
    for g, w in enumerate(POOL_WINDOWS):
        left, right = w // 2, w - w // 2 - 1
        inv_cnt[:, g] = 1.0 / (np.minimum(tm + right + 1, seq) - np.maximum(tm - left, 0)).astype(np.float32)
    return jnp.asarray(bands, BF16), tuple(kinds), inv_cnt


def _pool(group, y2g, x, wcol, mod_prev, mod, g_mix, pool_w, pool_scale, g_ffn, wr_hl, wr_hi, cnt_in, prev):
    const = lambda shape: pl.BlockSpec(shape, lambda i: (0,) * len(shape))
    tiles = (T_CTX if group == 0 else T_LAT) // PTILE
    first = 0 if group == 0 else T_CTX // PTILE
    out_specs, out_shape = _route_outs(PTILE, first)
    aliases = () if prev is None else tuple(prev)
    bands, block_kinds, inv_cnt = _pool_constants(SEQ if group == 0 else DEC_SEQ)
    n_fixed = 16
    return pl.pallas_call(
        functools.partial(_pool_kernel, block_kinds, len(aliases)),
        grid=(tiles,),
        in_specs=[
            pl.BlockSpec((PTILE, HALF), lambda i: (i, 0)),
            pl.BlockSpec((PTILE, HALF), lambda i: (tiles + i, 0)),
            pl.BlockSpec((PTILE, D), lambda i: (first + i, 0)),
            pl.BlockSpec((TOP_K, PTILE), lambda i: (0, first + i)),
            pl.BlockSpec((1, N_MOD, D), lambda i: (_cond_index(first + i, PTILE), 0, 0)),
            pl.BlockSpec((1, N_MOD, D), lambda i: (_cond_index(first + i, PTILE), 0, 0)),
            const((1, D)), const((len(POOL_WINDOWS), PGD, PGD)), const((1, D)), const((1, D)),
            const((D, 2 * LANES)), const((D, LANES)), const((256, 256)), const(bands.shape),
            const((PTILE, LANES)), const((NE, LANES)),
        ] + [pl.BlockSpec(memory_space=pl.ANY)] * len(aliases),
        out_specs=out_specs,
        out_shape=out_shape,
        input_output_aliases={n_fixed + k: k for k in range(len(aliases))},
        scratch_shapes=[pltpu.VMEM((NE, LANES), F32), pltpu.VMEM((PTILE + 2 * PB_HALO, D), BF16),
                        pltpu.VMEM((PTILE + 2 * PB_HALO, D), BF16)],
        compiler_params=_vmem_limit(56),
        name="pool_mixer",
    )(y2g, y2g, x, wcol, mod_prev, mod, g_mix, pool_w, pool_scale, g_ffn, wr_hl, wr_hi, _tri(), bands, inv_cnt,
      cnt_in, *aliases)


def _plan_kernel(e_ref, rank_ref, cnt_ref, pos_ref, te_ref, used_ref):
    counts = cnt_ref[:, 0:1]
    num = counts + (TM - 1)
    guess = jnp.floor(num * (1.0 / TM))
    tiles_col = (guess + jnp.where((guess + 1.0) * TM <= num, 1.0, 0.0)
                 - jnp.where(guess * TM > num, 1.0, 0.0))
    sub = lax.broadcasted_iota(I32, (NE, NE), 0)
    lane = lax.broadcasted_iota(I32, (NE, NE), 1)
    tiles_row = jnp.sum(jnp.where(sub == lane, tiles_col, 0.0), axis=0, keepdims=True)
    first_col = jnp.sum(jnp.where(lane < sub, tiles_row, 0.0), axis=1, keepdims=True)
    end_col = first_col + tiles_col
    used = jnp.sum(tiles_col, axis=0, keepdims=True)
    base_col = first_col * TM
    sub_e = lax.broadcasted_iota(I32, (NE, T), 0)
    for k in range(TOP_K):
        hit = sub_e == e_ref[k:k + 1, :]
        base = jnp.sum(jnp.where(hit, base_col, 0.0), axis=0, keepdims=True)
        pos_ref[k:k + 1, :] = base.astype(I32) + rank_ref[k:k + 1, :]
    tile = jnp.minimum(lax.broadcasted_iota(I32, (NE, LANES), 1).astype(F32), used - 1.0)
    te_ref[...] = jnp.sum(jnp.where(tile >= end_col, 1.0, 0.0), axis=0, keepdims=True).astype(I32)
    used_ref[...] = jnp.broadcast_to(used, (1, LANES)).astype(I32)


def _plan(e, rank, cnt):
    assert N_ROW_TILES <= LANES
    pos, te, used = pl.pallas_call(
        _plan_kernel,
        out_shape=[jax.ShapeDtypeStruct((TOP_K, T), I32), jax.ShapeDtypeStruct((1, LANES), I32),
                   jax.ShapeDtypeStruct((1, LANES), I32)],
        name="moe_plan",
    )(e, rank, cnt)
    return pos, te[0, :N_ROW_TILES], used[0, :1]


def _sc_mesh():
    return plsc.VectorSubcoreMesh(core_axis_name="core", subcore_axis_name="subcore")


def _sc_worker():
    return lax.axis_index("core") * SC_SUBCORES + lax.axis_index("subcore")


def _sc_dispatch(h, pos):
    per = T // SC_WORKERS
    nwin = per // SC_WIN

    @functools.partial(
        pl.kernel, out_type=jax.ShapeDtypeStruct((NPAD, HALF), U32), mesh=_sc_mesh(),
        scratch_types=[pltpu.VMEM((2, SC_WIN, HALF), U32), pltpu.VMEM((2, SC_WIN), I32),
                       pltpu.VMEM((2, SC_WIN), I32), pltpu.SemaphoreType.DMA((2,)),
                       pltpu.SemaphoreType.DMA((2,))])
    def run(h_hbm, pos_hbm, xs_hbm, buf, idx0, idx1, sem_in, sem_out):
        first = _sc_worker() * per

        def loads(j, s):
            base = first + j * SC_WIN
            return (pltpu.make_async_copy(h_hbm.at[pl.ds(base, SC_WIN)], buf.at[s], sem_in.at[s]),
                    pltpu.make_async_copy(pos_hbm.at[pl.ds(base, SC_WIN)], idx0.at[s], sem_in.at[s]),
                    pltpu.make_async_copy(pos_hbm.at[pl.ds(T + base, SC_WIN)], idx1.at[s], sem_in.at[s]))

        def stores(s):
            return (pltpu.make_async_copy(buf.at[s], xs_hbm.at[idx0.at[s]], sem_out.at[s]),
                    pltpu.make_async_copy(buf.at[s], xs_hbm.at[idx1.at[s]], sem_out.at[s]))

        for c in loads(0, 0):
            c.start()

        @pl.loop(0, nwin // 2)
        def _(jj):
            for s in range(2):
                j = jj * 2 + s
                for c in loads(j, s):
                    c.wait()
                for c in stores(s):
                    c.start()

                @pl.when(j >= 1)
                def _():
                    for c in stores(1 - s):
                        c.wait()

                @pl.when(j + 1 < nwin)
                def _():
                    for c in loads(j + 1, 1 - s):
                        c.start()

        for c in stores((nwin - 1) % 2):
            c.wait()

    return run(h, pos.reshape(TOP_K * T))


def _sc_gather(ys, pos, group):
    t0, n_g = (0, T_CTX) if group == 0 else (T_CTX, T_LAT)
    idx_all = jnp.concatenate([pos[k, t0:t0 + n_g] for k in range(TOP_K)])
    n = TOP_K * n_g
    per = n // SC_WORKERS
    nwin = per // SC_WIN

    @functools.partial(
        pl.kernel, out_type=jax.ShapeDtypeStruct((n, HALF), U32), mesh=_sc_mesh(),
        scratch_types=[pltpu.VMEM((2, SC_WIN, HALF), U32), pltpu.VMEM((per,), I32),
                       pltpu.SemaphoreType.DMA((2,)), pltpu.SemaphoreType.DMA((2,))])
    def run(ys_hbm, pos_hbm, out_hbm, buf, idx, sem_in, sem_out):
        first = _sc_worker() * per
        pltpu.sync_copy(pos_hbm.at[pl.ds(first, per)], idx)

        def load(j, s):
            return pltpu.make_async_copy(ys_hbm.at[idx.at[pl.ds(j * SC_WIN, SC_WIN)]], buf.at[s], sem_in.at[s])

        def store(j, s):
            return pltpu.make_async_copy(buf.at[s], out_hbm.at[pl.ds(first + j * SC_WIN, SC_WIN)], sem_out.at[s])

        load(0, 0).start()

        @pl.loop(0, nwin // 2)
        def _(jj):
            for s in range(2):
                j = jj * 2 + s
                load(j, s).wait()
                store(j, s).start()

                @pl.when(j >= 1)
                def _():
                    store(j - 1, 1 - s).wait()

                @pl.when(j + 1 < nwin)
                def _():
                    load(j + 1, 1 - s).start()

        store(nwin - 1, (nwin - 1) % 2).wait()

    return run(ys, idx_all)


def _moe_kernel(te_ref, used_ref, x_hbm, wg_ref, wu_ref, wd_ref, y_ref, xbuf, sem):
    del te_ref
    i = pl.program_id(0)
    used = used_ref[0]

    def tile_copy(t):
        slot = lax.rem(t, X_SLOTS)
        return pltpu.make_async_copy(x_hbm.at[pl.ds(t * TM, TM)], xbuf.at[slot], sem.at[slot])

    @pl.when(i == 0)
    def _():
        for t in range(X_SLOTS - 1):
            @pl.when(t < used)
            def _():
                tile_copy(t).start()

    @pl.when(i + (X_SLOTS - 1) < used)
    def _():
        tile_copy(i + (X_SLOTS - 1)).start()

    @pl.when(i < used)
    def _():
        tile_copy(i).wait()
        hi, lo = _unpack_rows(xbuf[lax.rem(i, X_SLOTS)])
        x = jnp.concatenate([hi, lo], axis=1).astype(BF16)
        a = jnp.dot(x, wg_ref[0, 0], preferred_element_type=F32)
        b = jnp.dot(x, wu_ref[0, 0], preferred_element_type=F32)
        hid = a * (1.0 / (1.0 + jnp.exp(-a))) * b
        y_ref[...] = _pack_rows(jnp.dot(hid.astype(BF16), wd_ref[0, 0], preferred_element_type=F32))


def _experts(xs, tile_expert, used, layer, w_gate, w_up, w_down):
    row = lambda i, te, used: (jnp.minimum(i, used[0] - 1), 0)
    weights = lambda i, te, used: (layer, te[i], 0, 0)
    grid_spec = pltpu.PrefetchScalarGridSpec(
        num_scalar_prefetch=2,
        grid=(N_ROW_TILES,),
        in_specs=[
            pl.BlockSpec(memory_space=pl.ANY),
            pl.BlockSpec((1, 1, D, DE), weights),
            pl.BlockSpec((1, 1, D, DE), weights),
            pl.BlockSpec((1, 1, DE, D), weights),
        ],
        out_specs=pl.BlockSpec((TM, HALF), row),
        scratch_shapes=[pltpu.VMEM((X_SLOTS, TM, HALF), U32), pltpu.SemaphoreType.DMA((X_SLOTS,))],
    )
    return pl.pallas_call(
        _moe_kernel,
        grid_spec=grid_spec,
        out_shape=jax.ShapeDtypeStruct((NPAD, HALF), U32),
        compiler_params=pltpu.CompilerParams(dimension_semantics=("arbitrary",),
                                             vmem_limit_bytes=40 * 1024 * 1024),
        name="moe_experts",
    )(tile_expert, used, xs, w_gate, w_up, w_down)


def _combined(y0_ref, y1_ref, x_ref, wcol_ref, mod_ref):
    sub = lax.broadcasted_iota(I32, (LANES, wcol_ref.shape[1]), 0)
    wt = jnp.where(sub == 0, wcol_ref[0:1, :], jnp.where(sub == 1, wcol_ref[1:2, :], 0.0)).T
    w0 = wt[:, 0:1]
    w1 = wt[:, 1:2]
    hi0, lo0 = _unpack_rows(y0_ref[...])
    hi1, lo1 = _unpack_rows(y1_ref[...])
    y = jnp.concatenate([w0 * hi0 + w1 * hi1, w0 * lo0 + w1 * lo1], axis=1)
    return x_ref[...] + mod_ref[0, 5:6, :] * y


def _combine_kernel(y0_ref, y1_ref, x_ref, wcol_ref, mod_ref, o_ref):
    o_ref[...] = _combined(y0_ref, y1_ref, x_ref, wcol_ref, mod_ref)


def _combine(group, y2g, x, wcol, mod):
    n_g = T_CTX if group == 0 else T_LAT
    tiles = n_g // CT
    first = 0 if group == 0 else T_CTX // CT
    return pl.pallas_call(
        _combine_kernel,
        grid=(tiles,),
        in_specs=[
            pl.BlockSpec((CT, HALF), lambda i: (i, 0)),
            pl.BlockSpec((CT, HALF), lambda i: (tiles + i, 0)),
            pl.BlockSpec((CT, D), lambda i: (first + i, 0)),
            pl.BlockSpec((TOP_K, CT), lambda i: (0, first + i)),
            pl.BlockSpec((1, N_MOD, D), lambda i: (_cond_index(first + i, CT), 0, 0)),
        ],
        out_specs=pl.BlockSpec((CT, D), lambda i: (i, 0)),
        out_shape=jax.ShapeDtypeStruct((n_g, D), F32),
        name="moe_combine",
    )(y2g, y2g, x, wcol, mod)


def _moe(h, e, rank, cnt, layer, w_gate, w_up, w_down):
    pos, tile_expert, used = _plan(e, rank, cnt)
    xs = _sc_dispatch(h, pos)
    ys = _experts(xs, tile_expert, used, layer, w_gate, w_up, w_down)
    return _sc_gather(ys, pos, 0), _sc_gather(ys, pos, 1)


def kernel(x_prompt, x_sample, c, cache_k, cache_v, c_ctx, norm_mix_g, norm_ffn_g, ada_w, ada_b, attn_w_qkv, attn_q_norm, attn_k_norm, attn_w_o, pool_w, pool_scale, moe_w_group, moe_w_expert, moe_w_gate, moe_w_up, moe_w_down):
    xp = x_prompt.reshape(T_CTX, D)
    xs = x_sample.reshape(T_LAT, D)
    cond = jnp.concatenate([c_ctx[None, :], c, jnp.zeros((N_COND - 1 - DEC_BATCH, D), F32)], axis=0)
    mod = _ada(cond, ada_w, ada_b)

    qt, k2, vt, new_k, new_v = _qkv(xp, xs, mod[0], norm_mix_g[0:1], attn_w_qkv[0].astype(BF16),
                                    attn_q_norm[0:1], attn_k_norm[0:1])
    o, w_gate, w_up, w_down = _attention(qt, k2, vt, cache_k[:, 0].reshape(DEC_BATCH, PAST, DKV),
                                         cache_v[:, 0].reshape(DEC_BATCH, PAST, DKV),
                                         moe_w_gate, moe_w_up, moe_w_down)
    wr_hl, wr_hi = _router_weights(moe_w_group[0], moe_w_expert[0])
    x1, h, e, rank, wcol, cnt = _post(o, xp, xs, mod[0], attn_w_o[0].astype(BF16), norm_ffn_g[0:1],
                                      wr_hl, wr_hi)
    y2 = _moe(h, e, rank, cnt, 0, w_gate, w_up, w_down)

    wr_hl, wr_hi = _router_weights(moe_w_group[1], moe_w_expert[1])
    pool_args = (x1, wcol, mod[0], mod[1], norm_mix_g[1:2], pool_w[0].astype(BF16), pool_scale[0:1],
                 norm_ffn_g[1:2], wr_hl, wr_hi)
    outs = _pool(0, y2[0], *pool_args, jnp.zeros((NE, LANES), F32), None)
    x3, h, e, rank, wcol, cnt = _pool(1, y2[1], *pool_args, outs[5], outs[:5])
    y2 = _moe(h, e, rank, cnt, 1, w_gate, w_up, w_down)
    y_prompt = _combine(0, y2[0], x3, wcol, mod[1])
    y_sample = _combine(1, y2[1], x3, wcol, mod[1])

    return (y_prompt.reshape(BATCH, SEQ, D), y_sample.reshape(DEC_BATCH, DEC_SEQ, D),
            new_k.reshape(BATCH, 1, SEQ, N_KV, HD), new_v.reshape(BATCH, 1, SEQ, N_KV, HD))
```

```python
import functools

import jax
import jax.numpy as jnp
import numpy as np
from jax import lax
from jax.experimental import pallas as pl
from jax.experimental.pallas import tpu as pltpu
from jax.experimental.pallas import tpu_sc as plsc

F32 = jnp.float32
BF16 = jnp.bfloat16
I32 = jnp.int32
U32 = jnp.uint32

D = 1024
BATCH, SEQ = 32, 256
DEC_BATCH, DEC_SEQ, PAST = 8, 1024, 512
T_CTX = BATCH * SEQ
T_LAT = DEC_BATCH * DEC_SEQ
T = T_CTX + T_LAT
GRID_W = 64
N_HEADS, N_KV, HD = 16, 4, 64
DQ = N_HEADS * HD
DKV = N_KV * HD
ROPE_THETA = 10000.0
POOL_WINDOWS = (2, 4, 8, 16)
PGD = D // len(POOL_WINDOWS)
N_GROUPS, E_PER_G, TOP_K = 4, 8, 2
NE = N_GROUPS * E_PER_G
DE = D // 4
N_MOD = 6
EPS = 1e-6
N_COND = 16

TILE = 512
NT = T // TILE
NT_CTX = T_CTX // TILE
QB = 256
KEY_PART = 256
PTILE = 1024
PB = 128
PBW = 256
PB_HALO = (PBW - PB) // 2
TM = 1024
X_SLOTS = 3
W_SLOTS = 3
N_ROW_TILES = (TOP_K * T) // TM + NE
NPAD = N_ROW_TILES * TM
CT = 1024
LANES = 128
NEG_INF = float("-inf")
LOG2_E = 1.4426950408889634
Q_SCALE = HD ** -0.5 * LOG2_E
HALF = D // 2
SC_CORES, SC_SUBCORES = 2, 16
SC_WORKERS = SC_CORES * SC_SUBCORES
SC_WIN = 64


def _vmem_limit(mib):
    return pltpu.CompilerParams(vmem_limit_bytes=mib * 1024 * 1024)


def _modulate(x, gain_scale, shift):
    ms = jnp.mean(x * x, axis=-1, keepdims=True)
    return (x * lax.rsqrt(ms + EPS)) * gain_scale + shift


def _head_norm(z, gain, ind, ind_t2):
    ss = jnp.dot((z * z).astype(BF16), ind, preferred_element_type=F32)
    inv = lax.rsqrt(ss * (1.0 / HD) + EPS)
    inv_hi = inv.astype(BF16)
    inv_lo = (inv - inv_hi.astype(F32)).astype(BF16)
    scale = jnp.dot(jnp.concatenate([inv_hi, inv_lo], axis=1), ind_t2, preferred_element_type=F32)
    return z * scale * gain


def _rope(z, cos_t, sin_t):
    lane = lax.broadcasted_iota(I32, (z.shape[0], LANES), 1)
    low = (lane % 32) < 16
    outs = []
    for c in range(z.shape[1] // LANES):
        zc = z[:, c * LANES:(c + 1) * LANES]
        up = pltpu.roll(zc, 16, axis=1)
        dn = pltpu.roll(zc, LANES - 16, axis=1)
        outs.append(zc * cos_t + jnp.where(low, dn, up) * sin_t)
    return jnp.concatenate(outs, axis=1)


def _rope_t(zt, cos_tt, sin_tt):
    sub = lax.broadcasted_iota(I32, (LANES, zt.shape[1]), 0)
    low = (sub % 32) < 16
    outs = []
    for c in range(zt.shape[0] // LANES):
        zc = zt[c * LANES:(c + 1) * LANES]
        up = pltpu.roll(zc, 16, axis=0)
        dn = pltpu.roll(zc, LANES - 16, axis=0)
        outs.append(zc * cos_tt + jnp.where(low, dn, up) * sin_tt)
    return jnp.concatenate(outs, axis=0)


def _pack_rows(z):
    bits = lax.bitcast_convert_type(z.astype(BF16).astype(F32), U32)
    return bits[:, :HALF] | (bits[:, HALF:] >> 16)


def _unpack_rows(p):
    hi = lax.bitcast_convert_type(p & jnp.uint32(0xFFFF0000), F32)
    lo = lax.bitcast_convert_type(p << 16, F32)
    return hi, lo


def _ada_kernel(c_ref, w_ref, b_ref, o_ref):
    c = c_ref[...]
    a = c * (1.0 / (1.0 + jnp.exp(-c)))
    o_ref[0] = jnp.dot(a.astype(BF16), w_ref[0].astype(BF16), preferred_element_type=F32) + b_ref[0]


def _ada(cond, ada_w, ada_b):
    nb = 1536
    depth = ada_w.shape[0]
    out = pl.pallas_call(
        _ada_kernel,
        grid=(depth, (N_MOD * D) // nb),
        in_specs=[
            pl.BlockSpec((N_COND, D), lambda l, j: (0, 0)),
            pl.BlockSpec((1, D, nb), lambda l, j: (l, 0, j)),
            pl.BlockSpec((1, 1, nb), lambda l, j: (l, 0, j)),
        ],
        out_specs=pl.BlockSpec((1, N_COND, nb), lambda l, j: (l, 0, j)),
        out_shape=jax.ShapeDtypeStruct((depth, N_COND, N_MOD * D), F32),
        compiler_params=_vmem_limit(40),
        name="ada",
    )(cond, ada_w, ada_b.reshape(depth, 1, N_MOD * D))
    return out.reshape(depth, N_COND, N_MOD, D)


def _cond_index(i, tile):
    n_ctx = T_CTX // tile
    per_batch = DEC_SEQ // tile
    return jnp.where(i < n_ctx, 0, 1 + (i - n_ctx) // per_batch)


def _qkv_kernel(xp_ref, xs_ref, mod_ref, g_ref, w_ref, qg_ref, kg_ref, indq_ref, indqt_ref,
                indk_ref, indkt_ref, cos_ref, sin_ref, cos_t_ref, sin_t_ref,
                qt_ref, k2_ref, vt_ref, nk_ref, nv_ref):
    i = pl.program_id(0)
    is_lat = i >= NT_CTX
    gain_scale = g_ref[...] * (1.0 + mod_ref[0, 1:2, :])
    subs = [slice(b * QB, (b + 1) * QB) for b in range(TILE // QB)]
    hs = [_modulate(jnp.where(is_lat, xs_ref[r, :], xp_ref[r, :]), gain_scale, mod_ref[0, 0:1, :]).astype(BF16)
          for r in subs]
    qkvs = [jnp.dot(h, w_ref[...], preferred_element_type=F32) for h in hs]
    qs = [_head_norm(z[:, :DQ], qg_ref[...], indq_ref[...], indqt_ref[...]) for z in qkvs]
    ks = [_head_norm(z[:, DQ:DQ + DKV], kg_ref[...], indk_ref[...], indkt_ref[...]) for z in qkvs]
    vs = [z[:, DQ + DKV:] for z in qkvs]
    qts = [(q * Q_SCALE).T for q in qs]
    for r, v in zip(subs, vs):
        vt_ref[:, r] = v.T.astype(BF16)

    @pl.when(is_lat)
    def _():
        for b, r in enumerate(subs):
            qt_ref[b] = _rope_t(qts[b], cos_t_ref[:, r], sin_t_ref[:, r]).astype(BF16)
            k2_ref[r, :] = _dup_heads(_rope(ks[b], cos_ref[r, :], sin_ref[r, :])).astype(BF16)

    @pl.when(jnp.logical_not(is_lat))
    def _():
        for b, r in enumerate(subs):
            qt_ref[b] = qts[b].astype(BF16)
            k2_ref[r, :] = _dup_heads(ks[b]).astype(BF16)
            nk_ref[r, :] = ks[b]
            nv_ref[r, :] = vs[b]


def _rope_tables():
    rows = DEC_SEQ // GRID_W
    row = np.broadcast_to(np.arange(rows, dtype=np.float32)[:, None], (rows, GRID_W)).reshape(-1)
    col = np.broadcast_to(np.arange(GRID_W, dtype=np.float32)[None, :], (rows, GRID_W)).reshape(-1)
    axis_dim = HD // 2
    inv_freq = np.power(np.float32(ROPE_THETA),
                        -np.arange(0, axis_dim, 2, dtype=np.float32) / np.float32(axis_dim))
    ang = np.concatenate([row[:, None] * inv_freq, col[:, None] * inv_freq], axis=-1).astype(np.float32)
    cos, sin = np.cos(ang), np.sin(ang)
    quarter = HD // 4
    cos_h = np.concatenate([cos[:, :quarter], cos[:, :quarter], cos[:, quarter:], cos[:, quarter:]], axis=1)
    sin_h = np.concatenate([-sin[:, :quarter], sin[:, :quarter], -sin[:, quarter:], sin[:, quarter:]], axis=1)
    return (np.tile(cos_h, (1, LANES // HD)).astype(np.float32),
            np.tile(sin_h, (1, LANES // HD)).astype(np.float32))


def _head_indicators(width):
    col = np.arange(width)[:, None]
    head = np.arange(LANES)[None, :]
    ind = (col // HD == head).astype(np.float32)
    return jnp.asarray(ind, BF16), jnp.asarray(np.concatenate([ind.T, ind.T], axis=0), BF16)


def _dup_heads(z):
    lane = lax.broadcasted_iota(I32, (z.shape[0], LANES), 1)
    cols = []
    for p in range(N_KV // 2):
        blk = z[:, p * LANES:(p + 1) * LANES]
        swp = pltpu.roll(blk, HD, axis=1)
        cols += [jnp.where(lane < HD, blk, swp), jnp.where(lane < HD, swp, blk)]
    return jnp.concatenate(cols, axis=1)


def _qkv(xp, xs, mod, g, w_qkv, q_gain, k_gain):
    indq, indqt = _head_indicators(DQ)
    indk, indkt = _head_indicators(DKV)
    cos_t, sin_t = _rope_tables()
    per_batch = DEC_SEQ // TILE
    const = lambda shape: pl.BlockSpec(shape, lambda i: (0,) * len(shape))
    return pl.pallas_call(
        _qkv_kernel,
        grid=(NT,),
        in_specs=[
            pl.BlockSpec((TILE, D), lambda i: (jnp.minimum(i, NT_CTX - 1), 0)),
            pl.BlockSpec((TILE, D), lambda i: (jnp.maximum(i - NT_CTX, 0), 0)),
            pl.BlockSpec((1, N_MOD, D), lambda i: (_cond_index(i, TILE), 0, 0)),
            const((1, D)),
            const((D, DQ + 2 * DKV)),
            const((1, DQ)), const((1, DKV)),
            const((DQ, LANES)), const((2 * LANES, DQ)),
            const((DKV, LANES)), const((2 * LANES, DKV)),
            pl.BlockSpec((TILE, LANES), lambda i: (jnp.maximum(i - NT_CTX, 0) % per_batch, 0)),
            pl.BlockSpec((TILE, LANES), lambda i: (jnp.maximum(i - NT_CTX, 0) % per_batch, 0)),
            pl.BlockSpec((LANES, TILE), lambda i: (0, jnp.maximum(i - NT_CTX, 0) % per_batch)),
            pl.BlockSpec((LANES, TILE), lambda i: (0, jnp.maximum(i - NT_CTX, 0) % per_batch)),
        ],
        out_specs=[
            pl.BlockSpec((TILE // QB, DQ, QB), lambda i: (i, 0, 0)),
            pl.BlockSpec((TILE, 2 * DKV), lambda i: (i, 0)),
            pl.BlockSpec((DKV, TILE), lambda i: (0, i)),
            pl.BlockSpec((TILE, DKV), lambda i: (jnp.minimum(i, NT_CTX - 1), 0)),
            pl.BlockSpec((TILE, DKV), lambda i: (jnp.minimum(i, NT_CTX - 1), 0)),
        ],
        out_shape=[
            jax.ShapeDtypeStruct((T // QB, DQ, QB), BF16),
            jax.ShapeDtypeStruct((T, 2 * DKV), BF16),
            jax.ShapeDtypeStruct((DKV, T), BF16),
            jax.ShapeDtypeStruct((T_CTX, DKV), F32),
            jax.ShapeDtypeStruct((T_CTX, DKV), F32),
        ],
        compiler_params=_vmem_limit(56),
        name="qkv",
    )(xp, xs, mod, g, w_qkv, jnp.tile(q_gain, (1, N_HEADS)), jnp.tile(k_gain, (1, N_KV)),
      indq, indqt, indk, indkt, cos_t, sin_t, cos_t.T, sin_t.T)


def _unit_scores(u, pairs, k2_parts, qt_ref):
    cols = []
    for j in range(pairs):
        pair = u * pairs + j
        qt_pair = qt_ref[pair * LANES:(pair + 1) * LANES, :]
        sub = lax.broadcasted_iota(I32, qt_pair.shape, 0)
        zero = jnp.zeros_like(qt_pair)
        cols += [jnp.where(sub < HD, qt_pair, zero), jnp.where(sub >= HD, qt_pair, zero)]
    qtu = jnp.concatenate(cols, axis=1)
    return [jnp.dot(k2, qtu, preferred_element_type=F32) for k2 in k2_parts]


def _softmax_values(ss, vt_parts):
    m = ss[0].max(axis=0, keepdims=True)
    for s in ss[1:]:
        m = jnp.maximum(m, s.max(axis=0, keepdims=True))
    acc = None
    l = None
    for s, vt in zip(ss, vt_parts):
        e = jnp.exp2(s - m)
        ls = e.sum(axis=0, keepdims=True)
        l = ls if l is None else l + ls
        t = jnp.dot(vt, e.astype(BF16), preferred_element_type=F32)
        acc = t if acc is None else acc + t
    return acc * (1.0 / l)


def _attend(keys_of, values_of, qt_ref, o_ref, pairs, ahead):
    n_units = N_HEADS // (2 * pairs)
    group = lambda u: (u * 2 * pairs) // (N_HEADS // N_KV)
    scores = lambda u: _unit_scores(u, pairs, keys_of(group(u)), qt_ref)
    pending = [scores(u) for u in range(min(ahead, n_units))]
    outs = []
    for u in range(n_units):
        if u + ahead < n_units:
            pending.append(scores(u + ahead))
        outs.append(_softmax_values(pending.pop(0), values_of(group(u))))
    lq = o_ref.shape[0]
    heads = [o[:, h * lq:(h + 1) * lq] for o in outs for h in range(2 * pairs)]
    o_ref[...] = jnp.concatenate(heads, axis=0).T.astype(BF16)


def _attn_ctx_kernel(qt_ref, k2_ref, vt_ref, o_ref):
    _attend(lambda g: [k2_ref[:, g * LANES:(g + 1) * LANES]],
            lambda g: [vt_ref[g * HD:(g + 1) * HD, :]], qt_ref.at[0], o_ref, pairs=2, ahead=1)


def _attn_lat_kernel(qt_ref, k2c_ref, vtc_ref, k2n_ref, vtn_ref, wg_ref, wu_ref, wd_ref, o_in_ref,
                     o_ref, wgb_ref, wub_ref, wdb_ref):
    del o_in_ref
    wgb_ref[...] = wg_ref[...].astype(BF16)
    wub_ref[...] = wu_ref[...].astype(BF16)
    wdb_ref[...] = wd_ref[...].astype(BF16)
    pc = [slice(h * KEY_PART, (h + 1) * KEY_PART) for h in range(PAST // KEY_PART)]
    pn = [slice(h * KEY_PART, (h + 1) * KEY_PART) for h in range(DEC_SEQ // KEY_PART)]
    _attend(lambda g: [k2c_ref[0, r, g * LANES:(g + 1) * LANES] for r in pc]
            + [k2n_ref[r, g * LANES:(g + 1) * LANES] for r in pn],
            lambda g: [vtc_ref[0, g * HD:(g + 1) * HD, r] for r in pc]
            + [vtn_ref[g * HD:(g + 1) * HD, r] for r in pn],
            qt_ref.at[0], o_ref, pairs=1, ahead=2)


def _cache_kernel(ck_ref, cv_ref, k2_ref, vt_ref):
    k2_ref[0] = _dup_heads(ck_ref[0]).astype(BF16)
    vt_ref[0] = cv_ref[0].T.astype(BF16)


def _attention(qt, k2, vt, cache_k, cache_v, w_gate, w_up, w_down):
    k2c, vtc = pl.pallas_call(
        _cache_kernel,
        grid=(DEC_BATCH,),
        in_specs=[
            pl.BlockSpec((1, PAST, DKV), lambda b: (b, 0, 0)),
            pl.BlockSpec((1, PAST, DKV), lambda b: (b, 0, 0)),
        ],
        out_specs=[
            pl.BlockSpec((1, PAST, 2 * DKV), lambda b: (b, 0, 0)),
            pl.BlockSpec((1, DKV, PAST), lambda b: (b, 0, 0)),
        ],
        out_shape=[
            jax.ShapeDtypeStruct((DEC_BATCH, PAST, 2 * DKV), BF16),
            jax.ShapeDtypeStruct((DEC_BATCH, DKV, PAST), BF16),
        ],
        name="cache_prep",
    )(cache_k, cache_v)

    o_ctx = pl.pallas_call(
        _attn_ctx_kernel,
        grid=(BATCH,),
        in_specs=[
            pl.BlockSpec((1, DQ, QB), lambda b: (b, 0, 0)),
            pl.BlockSpec((SEQ, 2 * DKV), lambda b: (b, 0)),
            pl.BlockSpec((DKV, SEQ), lambda b: (0, b)),
        ],
        out_specs=pl.BlockSpec((SEQ, DQ), lambda b: (b, 0)),
        out_shape=jax.ShapeDtypeStruct((T, DQ), BF16),
        name="attn_ctx",
    )(qt, k2, vt)

    qb = QB
    nqb = DEC_SEQ // qb
    lat0 = T_CTX // qb
    depth = w_gate.shape[0]
    eps = (depth * NE) // (DEC_BATCH * nqb)
    per_layer = NE // eps
    wblock = lambda b, j: ((b * nqb + j) // per_layer, (b * nqb + j) % per_layer, 0, 0)
    return pl.pallas_call(
        _attn_lat_kernel,
        grid=(DEC_BATCH, nqb),
        in_specs=[
            pl.BlockSpec((1, DQ, QB), lambda b, j: (lat0 + b * nqb + j, 0, 0)),
            pl.BlockSpec((1, PAST, 2 * DKV), lambda b, j: (b, 0, 0)),
            pl.BlockSpec((1, DKV, PAST), lambda b, j: (b, 0, 0)),
            pl.BlockSpec((DEC_SEQ, 2 * DKV), lambda b, j: (T_CTX // DEC_SEQ + b, 0)),
            pl.BlockSpec((DKV, DEC_SEQ), lambda b, j: (0, T_CTX // DEC_SEQ + b)),
            pl.BlockSpec((1, eps, D, DE), wblock),
            pl.BlockSpec((1, eps, D, DE), wblock),
            pl.BlockSpec((1, eps, DE, D), wblock),
            pl.BlockSpec(memory_space=pl.ANY),
        ],
        out_specs=[
            pl.BlockSpec((qb, DQ), lambda b, j: (lat0 + b * nqb + j, 0)),
            pl.BlockSpec((1, eps, D, DE), wblock),
            pl.BlockSpec((1, eps, D, DE), wblock),
            pl.BlockSpec((1, eps, DE, D), wblock),
        ],
        out_shape=[
            jax.ShapeDtypeStruct((T, DQ), BF16),
            jax.ShapeDtypeStruct(w_gate.shape, BF16),
            jax.ShapeDtypeStruct(w_up.shape, BF16),
            jax.ShapeDtypeStruct(w_down.shape, BF16),
        ],
        input_output_aliases={8: 0},
        compiler_params=_vmem_limit(56),
        name="attn_lat",
    )(qt, k2c, vtc, k2, vt, w_gate, w_up, w_down, o_ctx)


def _route(h, wr_hl_ref, wr_hi_ref, tri_ref, cnt_ref, e_ref, rank_ref, wcol_ref, cnt_out_ref):
    tl = h.shape[0]
    h_hi = h.astype(BF16)
    h_lo = (h - h_hi.astype(F32)).astype(BF16)
    lg2 = jnp.dot(h_hi, wr_hl_ref[...], preferred_element_type=F32)
    lg = lg2[:, :LANES] + lg2[:, LANES:] + jnp.dot(h_lo, wr_hi_ref[...], preferred_element_type=F32)
    lt = lg.T

    sub8 = lax.broadcasted_iota(I32, (8, tl), 0).astype(F32)
    gl = jnp.where(sub8 < N_GROUPS, lt[0:8], NEG_INF)
    gmax = gl.max(axis=0, keepdims=True)
    gidx = jnp.min(jnp.where(gl == gmax, sub8, 8.0), axis=0, keepdims=True)
    gsum = jnp.sum(jnp.exp(gl - gmax), axis=0, keepdims=True)
    sel = lt[8:16]
    for g in range(1, N_GROUPS):
        sel = jnp.where(gidx == g, lt[8 + 8 * g:16 + 8 * g], sel)
    t1 = sel.max(axis=0, keepdims=True)
    j1 = jnp.min(jnp.where(sel == t1, sub8, 8.0), axis=0, keepdims=True)
    sel2 = jnp.where(sub8 == j1, NEG_INF, sel)
    t2 = sel2.max(axis=0, keepdims=True)
    j2 = jnp.min(jnp.where(sel2 == t2, sub8, 8.0), axis=0, keepdims=True)
    ex = jnp.exp(t2 - t1)
    den = 1.0 + ex
    gw = 1.0 / gsum
    w0 = gw * (1.0 / den)
    w1 = gw * (ex / den)
    e0 = gidx * E_PER_G + j1
    e1 = gidx * E_PER_G + j2

    sub_e = lax.broadcasted_iota(I32, (NE, tl), 0).astype(F32)
    oh0 = (sub_e == e0).astype(F32)
    oh1 = (sub_e == e1).astype(F32)
    c = oh0 + oh1
    carry = cnt_ref[:, 0:1]
    pieces = []
    for ch in range(tl // 256):
        cc = c[:, ch * 256:(ch + 1) * 256]
        pieces.append(jnp.dot(cc.astype(BF16), tri_ref[...], preferred_element_type=F32) + carry)
        carry = carry + jnp.sum(cc, axis=1, keepdims=True)
    csum = jnp.concatenate(pieces, axis=1)
    cnt_ref[...] = jnp.broadcast_to(carry, (NE, LANES))
    cnt_out_ref[...] = jnp.broadcast_to(carry, (NE, LANES))
    r0 = jnp.sum(oh0 * csum, axis=0, keepdims=True) - 1.0
    r1 = jnp.sum(oh1 * csum, axis=0, keepdims=True) - 1.0
    e_ref[0:1, :] = e0.astype(I32)
    e_ref[1:2, :] = e1.astype(I32)
    rank_ref[0:1, :] = r0.astype(I32)
    rank_ref[1:2, :] = r1.astype(I32)
    wcol_ref[0:1, :] = w0
    wcol_ref[1:2, :] = w1


def _post_kernel(o_ref, xp_ref, xs_ref, mod_ref, wo_ref, g_ref, wr_hl_ref, wr_hi_ref, tri_ref,
                 x1_ref, h_ref, e_ref, rank_ref, wcol_ref, cnt_out_ref, cnt_ref):
    i = pl.program_id(0)

    @pl.when(i == 0)
    def _():
        cnt_ref[...] = jnp.zeros_like(cnt_ref)

    x = jnp.where(i >= NT_CTX, xs_ref[...], xp_ref[...])
    m = jnp.dot(o_ref[...], wo_ref[...], preferred_element_type=F32)
    x1 = x + mod_ref[0, 2:3, :] * m
    x1_ref[...] = x1
    h = _modulate(x1, g_ref[...] * (1.0 + mod_ref[0, 4:5, :]), mod_ref[0, 3:4, :])
    h_ref[...] = _pack_rows(h)
    _route(h, wr_hl_ref, wr_hi_ref, tri_ref, cnt_ref, e_ref, rank_ref, wcol_ref, cnt_out_ref)


def _router_weights(w_group, w_expert):
    w = jnp.concatenate([w_group, jnp.zeros((D, 8 - N_GROUPS), F32),
                         jnp.transpose(w_expert, (1, 0, 2)).reshape(D, NE),
                         jnp.zeros((D, LANES - 8 - NE), F32)], axis=1)
    w_hi = w.astype(BF16)
    w_lo = (w - w_hi.astype(F32)).astype(BF16)
    return jnp.concatenate([w_hi, w_lo], axis=1), w_hi


def _route_outs(tile, first=0):
    specs = [
        pl.BlockSpec((tile, D), lambda i: (first + i, 0)),
        pl.BlockSpec((tile, HALF), lambda i: (first + i, 0)),
        pl.BlockSpec((TOP_K, tile), lambda i: (0, first + i)),
        pl.BlockSpec((TOP_K, tile), lambda i: (0, first + i)),
        pl.BlockSpec((TOP_K, tile), lambda i: (0, first + i)),
        pl.BlockSpec((NE, LANES), lambda i: (0, 0)),
    ]
    shapes = [
        jax.ShapeDtypeStruct((T, D), F32),
        jax.ShapeDtypeStruct((T, HALF), U32),
        jax.ShapeDtypeStruct((TOP_K, T), I32),
        jax.ShapeDtypeStruct((TOP_K, T), I32),
        jax.ShapeDtypeStruct((TOP_K, T), F32),
        jax.ShapeDtypeStruct((NE, LANES), F32),
    ]
    return specs, shapes


def _tri():
    a = np.arange(256)
    return jnp.asarray((a[:, None] <= a[None, :]).astype(np.float32), BF16)


def _post(o, xp, xs, mod, w_o, g_ffn, wr_hl, wr_hi):
    const = lambda shape: pl.BlockSpec(shape, lambda i: (0,) * len(shape))
    out_specs, out_shape = _route_outs(TILE)
    return pl.pallas_call(
        _post_kernel,
        grid=(NT,),
        in_specs=[
            pl.BlockSpec((TILE, DQ), lambda i: (i, 0)),
            pl.BlockSpec((TILE, D), lambda i: (jnp.minimum(i, NT_CTX - 1), 0)),
            pl.BlockSpec((TILE, D), lambda i: (jnp.maximum(i - NT_CTX, 0), 0)),
            pl.BlockSpec((1, N_MOD, D), lambda i: (_cond_index(i, TILE), 0, 0)),
            const((DQ, D)), const((1, D)), const((D, 2 * LANES)), const((D, LANES)), const((256, 256)),
        ],
        out_specs=out_specs,
        out_shape=out_shape,
        scratch_shapes=[pltpu.VMEM((NE, LANES), F32)],
        compiler_params=_vmem_limit(56),
        name="post_attn",
    )(o, xp, xs, mod, w_o, g_ffn, wr_hl, wr_hi, _tri())


def _pool_kernel(block_kinds, n_alias, y0_ref, y1_ref, xin_ref, wcin_ref, modp_ref, mod_ref, g_ref, pw_ref,
                 ps_ref, gf_ref, wr_hl_ref, wr_hi_ref, tri_ref, band_ref, inv_cnt_ref, cnt_in_ref, *refs):
    (x1_ref, h_ref, e_ref, rank_ref, wcol_ref, cnt_out_ref, cnt_ref, hhi_ref, hlo_ref) = refs[n_alias:]
    i = pl.program_id(0)

    @pl.when(i == 0)
    def _():
        cnt_ref[...] = cnt_in_ref[...]

    x = _combined(y0_ref, y1_ref, xin_ref, wcin_ref, modp_ref)
    h = _modulate(x, g_ref[...] * (1.0 + mod_ref[0, 1:2, :]), mod_ref[0, 0:1, :])
    @pl.when(i == 0)
    def _():
        zeros = jnp.zeros((PB_HALO, D), BF16)
        for ref in (hhi_ref, hlo_ref):
            ref[0:PB_HALO, :] = zeros
            ref[PB_HALO + PTILE:PB_HALO + PTILE + PB_HALO, :] = zeros

    h_hi = h.astype(BF16)
    hhi_ref[PB_HALO:PB_HALO + PTILE, :] = h_hi
    hlo_ref[PB_HALO:PB_HALO + PTILE, :] = (h - h_hi.astype(F32)).astype(BF16)

    outs = []
    for gi in range(len(POOL_WINDOWS)):
        cols = slice(gi * PGD, (gi + 1) * PGD)
        sums = []
        for b in range(PTILE // PB):
            bm = band_ref[block_kinds[b], gi]
            sums.append(jnp.dot(bm, hhi_ref[b * PB:b * PB + PBW, cols], preferred_element_type=F32)
                        + jnp.dot(bm, hlo_ref[b * PB:b * PB + PBW, cols], preferred_element_type=F32))
        diff = jnp.concatenate(sums, axis=0) * inv_cnt_ref[:, gi:gi + 1] - h[:, cols]
        outs.append(jnp.dot(diff.astype(BF16), pw_ref[gi], preferred_element_type=F32))
    m = jnp.concatenate(outs, axis=1) * ps_ref[...]
    x1 = x + mod_ref[0, 2:3, :] * m
    x1_ref[...] = x1
    hf = _modulate(x1, gf_ref[...] * (1.0 + mod_ref[0, 4:5, :]), mod_ref[0, 3:4, :])
    h_ref[...] = _pack_rows(hf)
    _route(hf, wr_hl_ref, wr_hi_ref, tri_ref, cnt_ref, e_ref, rank_ref, wcol_ref, cnt_out_ref)


def _pool_constants(seq):
    r = np.arange(PB)[:, None]
    d = np.arange(PBW)[None, :] - PB_HALO - r
    found, kinds = {}, []
    for b in range(PTILE // PB):
        tm = (b * PB + r) % seq
        inseq = (tm + d >= 0) & (tm + d < seq)
        mats = np.stack([(inseq & (d >= -(w // 2)) & (d <= w - w // 2 - 1)).astype(np.float32)
                         for w in POOL_WINDOWS])
        kinds.append(found.setdefault(mats.tobytes(), (len(found), mats))[0])
    bands = np.stack([m for _, m in sorted(found.values(), key=lambda kv: kv[0])])
    tm = np.arange(PTILE) % seq
    inv_cnt = np.zeros((PTILE, LANES), np.float32)
    for g, w in enumerate(POOL_WINDOWS):
        left, right = w // 2, w - w // 2 - 1
        inv_cnt[:, g] = 1.0 / (np.minimum(tm + right + 1, seq) - np.maximum(tm - left, 0)).astype(np.float32)
    return jnp.asarray(bands, BF16), tuple(kinds), inv_cnt


def _pool(group, y2g, x, wcol, mod_prev, mod, g_mix, pool_w, pool_scale, g_ffn, wr_hl, wr_hi, cnt_in, prev):
    const = lambda shape: pl.BlockSpec(shape, lambda i: (0,) * len(shape))
    tiles = (T_CTX if group == 0 else T_LAT) // PTILE
    first = 0 if group == 0 else T_CTX // PTILE
    out_specs, out_shape = _route_outs(PTILE, first)
    aliases = () if prev is None else tuple(prev)
    bands, block_kinds, inv_cnt = _pool_constants(SEQ if group == 0 else DEC_SEQ)
    n_fixed = 16
    return pl.pallas_call(
        functools.partial(_pool_kernel, block_kinds, len(aliases)),
        grid=(tiles,),
        in_specs=[
            pl.BlockSpec((PTILE, HALF), lambda i: (i, 0)),
            pl.BlockSpec((PTILE, HALF), lambda i: (tiles + i, 0)),
            pl.BlockSpec((PTILE, D), lambda i: (first + i, 0)),
            pl.BlockSpec((TOP_K, PTILE), lambda i: (0, first + i)),
            pl.BlockSpec((1, N_MOD, D), lambda i: (_cond_index(first + i, PTILE), 0, 0)),
            pl.BlockSpec((1, N_MOD, D), lambda i: (_cond_index(first + i, PTILE), 0, 0)),
            const((1, D)), const((len(POOL_WINDOWS), PGD, PGD)), const((1, D)), const((1, D)),
            const((D, 2 * LANES)), const((D, LANES)), const((256, 256)), const(bands.shape),
            const((PTILE, LANES)), const((NE, LANES)),
        ] + [pl.BlockSpec(memory_space=pl.ANY)] * len(aliases),
        out_specs=out_specs,
        out_shape=out_shape,
        input_output_aliases={n_fixed + k: k for k in range(len(aliases))},
        scratch_shapes=[pltpu.VMEM((NE, LANES), F32), pltpu.VMEM((PTILE + 2 * PB_HALO, D), BF16),
                        pltpu.VMEM((PTILE + 2 * PB_HALO, D), BF16)],
        compiler_params=_vmem_limit(56),
        name="pool_mixer",
    )(y2g, y2g, x, wcol, mod_prev, mod, g_mix, pool_w, pool_scale, g_ffn, wr_hl, wr_hi, _tri(), bands, inv_cnt,
      cnt_in, *aliases)


def _plan_kernel(e_ref, rank_ref, cnt_ref, pos_ref, te_ref, used_ref):
    counts = cnt_ref[:, 0:1]
    tiles_col = jnp.floor((counts + (TM - 1)) * (1.0 / TM))
    sub = lax.broadcasted_iota(I32, (NE, NE), 0)
    lane = lax.broadcasted_iota(I32, (NE, NE), 1)
    tiles_row = jnp.sum(jnp.where(sub == lane, tiles_col, 0.0), axis=0, keepdims=True)
    first_col = jnp.sum(jnp.where(lane < sub, tiles_row, 0.0), axis=1, keepdims=True)
    end_col = first_col + tiles_col
    used = jnp.sum(tiles_col, axis=0, keepdims=True)
    base_col = first_col * TM
    sub_e = lax.broadcasted_iota(I32, (NE, T), 0)
    for k in range(TOP_K):
        hit = sub_e == e_ref[k:k + 1, :]
        base = jnp.sum(jnp.where(hit, base_col, 0.0), axis=0, keepdims=True)
        pos_ref[k:k + 1, :] = base.astype(I32) + rank_ref[k:k + 1, :]
    tile = jnp.minimum(lax.broadcasted_iota(I32, (NE, LANES), 1).astype(F32), used - 1.0)
    te_ref[...] = jnp.sum(jnp.where(tile >= end_col, 1.0, 0.0), axis=0, keepdims=True).astype(I32)
    used_ref[...] = jnp.broadcast_to(used, (1, LANES)).astype(I32)


def _plan(e, rank, cnt):
    assert N_ROW_TILES <= LANES
    pos, te, used = pl.pallas_call(
        _plan_kernel,
        out_shape=[jax.ShapeDtypeStruct((TOP_K, T), I32), jax.ShapeDtypeStruct((1, LANES), I32),
                   jax.ShapeDtypeStruct((1, LANES), I32)],
        name="moe_plan",
    )(e, rank, cnt)
    return pos, te[0, :N_ROW_TILES], used[0, :1]


def _sc_mesh():
    return plsc.VectorSubcoreMesh(core_axis_name="core", subcore_axis_name="subcore")


def _sc_worker():
    return lax.axis_index("core") * SC_SUBCORES + lax.axis_index("subcore")


def _sc_dispatch(h, pos):
    per = T // SC_WORKERS
    nwin = per // SC_WIN

    @functools.partial(
        pl.kernel, out_type=jax.ShapeDtypeStruct((NPAD, HALF), U32), mesh=_sc_mesh(),
        scratch_types=[pltpu.VMEM((2, SC_WIN, HALF), U32), pltpu.VMEM((2, SC_WIN), I32),
                       pltpu.VMEM((2, SC_WIN), I32), pltpu.SemaphoreType.DMA((2,)),
                       pltpu.SemaphoreType.DMA((2,))])
    def run(h_hbm, pos_hbm, xs_hbm, buf, idx0, idx1, sem_in, sem_out):
        first = _sc_worker() * per

        def loads(j, s):
            base = first + j * SC_WIN
            return (pltpu.make_async_copy(h_hbm.at[pl.ds(base, SC_WIN)], buf.at[s], sem_in.at[s]),
                    pltpu.make_async_copy(pos_hbm.at[pl.ds(base, SC_WIN)], idx0.at[s], sem_in.at[s]),
                    pltpu.make_async_copy(pos_hbm.at[pl.ds(T + base, SC_WIN)], idx1.at[s], sem_in.at[s]))

        def stores(s):
            return (pltpu.make_async_copy(buf.at[s], xs_hbm.at[idx0.at[s]], sem_out.at[s]),
                    pltpu.make_async_copy(buf.at[s], xs_hbm.at[idx1.at[s]], sem_out.at[s]))

        for c in loads(0, 0):
            c.start()

        @pl.loop(0, nwin // 2)
        def _(jj):
            for s in range(2):
                j = jj * 2 + s
                for c in loads(j, s):
                    c.wait()
                for c in stores(s):
                    c.start()

                @pl.when(j >= 1)
                def _():
                    for c in stores(1 - s):
                        c.wait()

                @pl.when(j + 1 < nwin)
                def _():
                    for c in loads(j + 1, 1 - s):
                        c.start()

        for c in stores((nwin - 1) % 2):
            c.wait()

    return run(h, pos.reshape(TOP_K * T))


def _sc_gather(ys, pos, group):
    t0, n_g = (0, T_CTX) if group == 0 else (T_CTX, T_LAT)
    idx_all = jnp.concatenate([pos[k, t0:t0 + n_g] for k in range(TOP_K)])
    n = TOP_K * n_g
    per = n // SC_WORKERS
    nwin = per // SC_WIN

    @functools.partial(
        pl.kernel, out_type=jax.ShapeDtypeStruct((n, HALF), U32), mesh=_sc_mesh(),
        scratch_types=[pltpu.VMEM((2, SC_WIN, HALF), U32), pltpu.VMEM((per,), I32),
                       pltpu.SemaphoreType.DMA((2,)), pltpu.SemaphoreType.DMA((2,))])
    def run(ys_hbm, pos_hbm, out_hbm, buf, idx, sem_in, sem_out):
        first = _sc_worker() * per
        pltpu.sync_copy(pos_hbm.at[pl.ds(first, per)], idx)

        def load(j, s):
            return pltpu.make_async_copy(ys_hbm.at[idx.at[pl.ds(j * SC_WIN, SC_WIN)]], buf.at[s], sem_in.at[s])

        def store(j, s):
            return pltpu.make_async_copy(buf.at[s], out_hbm.at[pl.ds(first + j * SC_WIN, SC_WIN)], sem_out.at[s])

        load(0, 0).start()

        @pl.loop(0, nwin // 2)
        def _(jj):
            for s in range(2):
                j = jj * 2 + s
                load(j, s).wait()
                store(j, s).start()

                @pl.when(j >= 1)
                def _():
                    store(j - 1, 1 - s).wait()

                @pl.when(j + 1 < nwin)
                def _():
                    load(j + 1, 1 - s).start()

        store(nwin - 1, (nwin - 1) % 2).wait()

    return run(ys, idx_all)


def _moe_kernel(layer, used_ref, first_ref, run_ref, run_expert_ref, n_runs_ref,
                x_hbm, wg_hbm, wu_hbm, wd_hbm, y_ref, xbuf, wgb, wub, wdb, sem_x, sem_w):
    i = pl.program_id(0)
    used = used_ref[0]
    n_runs = n_runs_ref[0]
    run = run_ref[i]
    starts_run = first_ref[i] == 1

    def tile_copy(t):
        slot = lax.rem(t, X_SLOTS)
        return pltpu.make_async_copy(x_hbm.at[pl.ds(t * TM, TM)], xbuf.at[slot], sem_x.at[slot])

    def weight_copies(r):
        slot = lax.rem(r, W_SLOTS)
        e = run_expert_ref[r]
        return [pltpu.make_async_copy(w_hbm.at[layer, e], buf.at[slot], sem_w.at[slot])
                for w_hbm, buf in ((wg_hbm, wgb), (wu_hbm, wub), (wd_hbm, wdb))]

    @pl.when(i == 0)
    def _():
        for t in range(X_SLOTS - 1):
            @pl.when(t < used)
            def _():
                tile_copy(t).start()
        for r in range(W_SLOTS - 1):
            @pl.when(r < n_runs)
            def _():
                for c in weight_copies(r):
                    c.start()

    @pl.when(i + (X_SLOTS - 1) < used)
    def _():
        tile_copy(i + (X_SLOTS - 1)).start()

    @pl.when(jnp.logical_and(starts_run, run + (W_SLOTS - 1) < n_runs))
    def _():
        for c in weight_copies(run + (W_SLOTS - 1)):
            c.start()

    @pl.when(i < used)
    def _():
        @pl.when(starts_run)
        def _():
            for c in weight_copies(run):
                c.wait()

        tile_copy(i).wait()
        wslot = lax.rem(run, W_SLOTS)
        hi, lo = _unpack_rows(xbuf[lax.rem(i, X_SLOTS)])
        x = jnp.concatenate([hi, lo], axis=1).astype(BF16)
        a = jnp.dot(x, wgb[wslot], preferred_element_type=F32)
        b = jnp.dot(x, wub[wslot], preferred_element_type=F32)
        hid = a * (1.0 / (1.0 + jnp.exp(-a))) * b
        y_ref[...] = _pack_rows(jnp.dot(hid.astype(BF16), wdb[wslot], preferred_element_type=F32))


def _experts(xs, tile_expert, used, layer, w_gate, w_up, w_down):
    tile = jnp.arange(N_ROW_TILES, dtype=I32)
    changed = jnp.concatenate([jnp.ones((1,), bool), tile_expert[1:] != tile_expert[:-1]])
    first = jnp.logical_and(tile < used[0], changed)
    run = jnp.cumsum(first.astype(I32)) - 1
    run_expert = jnp.zeros((N_ROW_TILES,), I32).at[jnp.where(first, run, N_ROW_TILES)].set(
        tile_expert, mode="drop")
    n_runs = jnp.sum(first.astype(I32)).reshape(1)
    row = lambda i, used, *_: (jnp.minimum(i, used[0] - 1), 0)
    grid_spec = pltpu.PrefetchScalarGridSpec(
        num_scalar_prefetch=5,
        grid=(N_ROW_TILES,),
        in_specs=[pl.BlockSpec(memory_space=pl.ANY)] * 4,
        out_specs=pl.BlockSpec((TM, HALF), row),
        scratch_shapes=[pltpu.VMEM((X_SLOTS, TM, HALF), U32),
                        pltpu.VMEM((W_SLOTS, D, DE), BF16), pltpu.VMEM((W_SLOTS, D, DE), BF16),
                        pltpu.VMEM((W_SLOTS, DE, D), BF16),
                        pltpu.SemaphoreType.DMA((X_SLOTS,)), pltpu.SemaphoreType.DMA((W_SLOTS,))],
    )
    return pl.pallas_call(
        functools.partial(_moe_kernel, layer),
        grid_spec=grid_spec,
        out_shape=jax.ShapeDtypeStruct((NPAD, HALF), U32),
        compiler_params=pltpu.CompilerParams(dimension_semantics=("arbitrary",),
                                             vmem_limit_bytes=40 * 1024 * 1024),
        name="moe_experts",
    )(used, first.astype(I32), run, run_expert, n_runs, xs, w_gate, w_up, w_down)


def _combined(y0_ref, y1_ref, x_ref, wcol_ref, mod_ref):
    sub = lax.broadcasted_iota(I32, (LANES, wcol_ref.shape[1]), 0)
    wt = jnp.where(sub == 0, wcol_ref[0:1, :], jnp.where(sub == 1, wcol_ref[1:2, :], 0.0)).T
    w0 = wt[:, 0:1]
    w1 = wt[:, 1:2]
    hi0, lo0 = _unpack_rows(y0_ref[...])
    hi1, lo1 = _unpack_rows(y1_ref[...])
    y = jnp.concatenate([w0 * hi0 + w1 * hi1, w0 * lo0 + w1 * lo1], axis=1)
    return x_ref[...] + mod_ref[0, 5:6, :] * y


def _combine_kernel(y0_ref, y1_ref, x_ref, wcol_ref, mod_ref, o_ref):
    o_ref[...] = _combined(y0_ref, y1_ref, x_ref, wcol_ref, mod_ref)


def _combine(group, y2g, x, wcol, mod):
    n_g = T_CTX if group == 0 else T_LAT
    tiles = n_g // CT
    first = 0 if group == 0 else T_CTX // CT
    return pl.pallas_call(
        _combine_kernel,
        grid=(tiles,),
        in_specs=[
            pl.BlockSpec((CT, HALF), lambda i: (i, 0)),
            pl.BlockSpec((CT, HALF), lambda i: (tiles + i, 0)),
            pl.BlockSpec((CT, D), lambda i: (first + i, 0)),
            pl.BlockSpec((TOP_K, CT), lambda i: (0, first + i)),
            pl.BlockSpec((1, N_MOD, D), lambda i: (_cond_index(first + i, CT), 0, 0)),
        ],
        out_specs=pl.BlockSpec((CT, D), lambda i: (i, 0)),
        out_shape=jax.ShapeDtypeStruct((n_g, D), F32),
        name="moe_combine",
    )(y2g, y2g, x, wcol, mod)


def _moe(h, e, rank, cnt, layer, w_gate, w_up, w_down):
    pos, tile_expert, used = _plan(e, rank, cnt)
    xs = _sc_dispatch(h, pos)
    ys = _experts(xs, tile_expert, used, layer, w_gate, w_up, w_down)
    return _sc_gather(ys, pos, 0), _sc_gather(ys, pos, 1)


def kernel(x_prompt, x_sample, c, cache_k, cache_v, c_ctx, norm_mix_g, norm_ffn_g, ada_w, ada_b, attn_w_qkv, attn_q_norm, attn_k_norm, attn_w_o, pool_w, pool_scale, moe_w_group, moe_w_expert, moe_w_gate, moe_w_up, moe_w_down):
    xp = x_prompt.reshape(T_CTX, D)
    xs = x_sample.reshape(T_LAT, D)
    cond = jnp.concatenate([c_ctx[None, :], c, jnp.zeros((N_COND - 1 - DEC_BATCH, D), F32)], axis=0)
    mod = _ada(cond, ada_w, ada_b)

    qt, k2, vt, new_k, new_v = _qkv(xp, xs, mod[0], norm_mix_g[0:1], attn_w_qkv[0].astype(BF16),
                                    attn_q_norm[0:1], attn_k_norm[0:1])
    o, w_gate, w_up, w_down = _attention(qt, k2, vt, cache_k[:, 0].reshape(DEC_BATCH, PAST, DKV),
                                         cache_v[:, 0].reshape(DEC_BATCH, PAST, DKV),
                                         moe_w_gate, moe_w_up, moe_w_down)
    wr_hl, wr_hi = _router_weights(moe_w_group[0], moe_w_expert[0])
    x1, h, e, rank, wcol, cnt = _post(o, xp, xs, mod[0], attn_w_o[0].astype(BF16), norm_ffn_g[0:1],
                                      wr_hl, wr_hi)
    y2 = _moe(h, e, rank, cnt, 0, w_gate, w_up, w_down)

    wr_hl, wr_hi = _router_weights(moe_w_group[1], moe_w_expert[1])
    pool_args = (x1, wcol, mod[0], mod[1], norm_mix_g[1:2], pool_w[0].astype(BF16), pool_scale[0:1],
                 norm_ffn_g[1:2], wr_hl, wr_hi)
    outs = _pool(0, y2[0], *pool_args, jnp.zeros((NE, LANES), F32), None)
    x3, h, e, rank, wcol, cnt = _pool(1, y2[1], *pool_args, outs[5], outs[:5])
    y2 = _moe(h, e, rank, cnt, 1, w_gate, w_up, w_down)
    y_prompt = _combine(0, y2[0], x3, wcol, mod[1])
    y_sample = _combine(1, y2[1], x3, wcol, mod[1])

    return (y_prompt.reshape(BATCH, SEQ, D), y_sample.reshape(DEC_BATCH, DEC_SEQ, D),
            new_k.reshape(BATCH, 1, SEQ, N_KV, HD), new_v.reshape(BATCH, 1, SEQ, N_KV, HD))
```

```python
import functools

import jax
import jax.numpy as jnp
import numpy as np
from jax import lax
from jax.experimental import pallas as pl
from jax.experimental.pallas import tpu as pltpu
from jax.experimental.pallas import tpu_sc as plsc

F32 = jnp.float32
BF16 = jnp.bfloat16
I32 = jnp.int32
U32 = jnp.uint32

D = 1024
BATCH, SEQ = 32, 256
DEC_BATCH, DEC_SEQ, PAST = 8, 1024, 512
T_CTX = BATCH * SEQ
T_LAT = DEC_BATCH * DEC_SEQ
T = T_CTX + T_LAT
GRID_W = 64
N_HEADS, N_KV, HD = 16, 4, 64
DQ = N_HEADS * HD
DKV = N_KV * HD
ROPE_THETA = 10000.0
POOL_WINDOWS = (2, 4, 8, 16)
PGD = D // len(POOL_WINDOWS)
N_GROUPS, E_PER_G, TOP_K = 4, 8, 2
NE = N_GROUPS * E_PER_G
DE = D // 4
N_MOD = 6
EPS = 1e-6
N_COND = 16

TILE = 512
NT = T // TILE
NT_CTX = T_CTX // TILE
QB = 256
KEY_PART = 256
PTILE = 1024
PB = 128
PBW = 256
PB_HALO = (PBW - PB) // 2
TM = 1024
X_SLOTS = 3
N_ROW_TILES = (TOP_K * T) // TM + NE
NPAD = N_ROW_TILES * TM
CT = 1024
LANES = 128
NEG_INF = float("-inf")
LOG2_E = 1.4426950408889634
Q_SCALE = HD ** -0.5 * LOG2_E
HALF = D // 2
SC_CORES, SC_SUBCORES = 2, 16
SC_WORKERS = SC_CORES * SC_SUBCORES
SC_WIN = 64


def _vmem_limit(mib):
    return pltpu.CompilerParams(vmem_limit_bytes=mib * 1024 * 1024)


def _modulate(x, gain_scale, shift):
    ms = jnp.mean(x * x, axis=-1, keepdims=True)
    return (x * lax.rsqrt(ms + EPS)) * gain_scale + shift


def _head_norm(z, gain, ind, ind_t2):
    ss = jnp.dot((z * z).astype(BF16), ind, preferred_element_type=F32)
    inv = lax.rsqrt(ss * (1.0 / HD) + EPS)
    inv_hi = inv.astype(BF16)
    inv_lo = (inv - inv_hi.astype(F32)).astype(BF16)
    scale = jnp.dot(jnp.concatenate([inv_hi, inv_lo], axis=1), ind_t2, preferred_element_type=F32)
    return z * scale * gain


def _rope(z, cos_t, sin_t):
    lane = lax.broadcasted_iota(I32, (z.shape[0], LANES), 1)
    low = (lane % 32) < 16
    outs = []
    for c in range(z.shape[1] // LANES):
        zc = z[:, c * LANES:(c + 1) * LANES]
        up = pltpu.roll(zc, 16, axis=1)
        dn = pltpu.roll(zc, LANES - 16, axis=1)
        outs.append(zc * cos_t + jnp.where(low, dn, up) * sin_t)
    return jnp.concatenate(outs, axis=1)


def _rope_t(zt, cos_tt, sin_tt):
    sub = lax.broadcasted_iota(I32, (LANES, zt.shape[1]), 0)
    low = (sub % 32) < 16
    outs = []
    for c in range(zt.shape[0] // LANES):
        zc = zt[c * LANES:(c + 1) * LANES]
        up = pltpu.roll(zc, 16, axis=0)
        dn = pltpu.roll(zc, LANES - 16, axis=0)
        outs.append(zc * cos_tt + jnp.where(low, dn, up) * sin_tt)
    return jnp.concatenate(outs, axis=0)


def _pack_rows(z):
    bits = lax.bitcast_convert_type(z.astype(BF16).astype(F32), U32)
    return bits[:, :HALF] | (bits[:, HALF:] >> 16)


def _unpack_rows(p):
    hi = lax.bitcast_convert_type(p & jnp.uint32(0xFFFF0000), F32)
    lo = lax.bitcast_convert_type(p << 16, F32)
    return hi, lo


def _ada_kernel(c_ref, w_ref, b_ref, o_ref):
    c = c_ref[...]
    a = c * (1.0 / (1.0 + jnp.exp(-c)))
    o_ref[0] = jnp.dot(a.astype(BF16), w_ref[0].astype(BF16), preferred_element_type=F32) + b_ref[0]


def _ada(cond, ada_w, ada_b, layer):
    nb = 1536
    depth = ada_w.shape[0]
    out = pl.pallas_call(
        _ada_kernel,
        grid=((N_MOD * D) // nb,),
        in_specs=[
            pl.BlockSpec((N_COND, D), lambda j: (0, 0)),
            pl.BlockSpec((1, D, nb), lambda j: (layer, 0, j)),
            pl.BlockSpec((1, 1, nb), lambda j: (layer, 0, j)),
        ],
        out_specs=pl.BlockSpec((1, N_COND, nb), lambda j: (0, 0, j)),
        out_shape=jax.ShapeDtypeStruct((1, N_COND, N_MOD * D), F32),
        compiler_params=_vmem_limit(40),
        name="ada",
    )(cond, ada_w, ada_b.reshape(depth, 1, N_MOD * D))
    return out.reshape(N_COND, N_MOD, D)


def _cond_index(i, tile):
    n_ctx = T_CTX // tile
    per_batch = DEC_SEQ // tile
    return jnp.where(i < n_ctx, 0, 1 + (i - n_ctx) // per_batch)


def _qkv_kernel(xp_ref, xs_ref, mod_ref, g_ref, w_ref, qg_ref, kg_ref, indq_ref, indqt_ref,
                indk_ref, indkt_ref, cos_ref, sin_ref, cos_t_ref, sin_t_ref,
                qt_ref, k2_ref, vt_ref, nk_ref, nv_ref):
    i = pl.program_id(0)
    is_lat = i >= NT_CTX
    gain_scale = g_ref[...] * (1.0 + mod_ref[0, 1:2, :])
    subs = [slice(b * QB, (b + 1) * QB) for b in range(TILE // QB)]
    hs = [_modulate(jnp.where(is_lat, xs_ref[r, :], xp_ref[r, :]), gain_scale, mod_ref[0, 0:1, :]).astype(BF16)
          for r in subs]
    qkvs = [jnp.dot(h, w_ref[...], preferred_element_type=F32) for h in hs]
    qs = [_head_norm(z[:, :DQ], qg_ref[...], indq_ref[...], indqt_ref[...]) for z in qkvs]
    ks = [_head_norm(z[:, DQ:DQ + DKV], kg_ref[...], indk_ref[...], indkt_ref[...]) for z in qkvs]
    vs = [z[:, DQ + DKV:] for z in qkvs]
    qts = [(q * Q_SCALE).T for q in qs]
    for r, v in zip(subs, vs):
        vt_ref[:, r] = v.T.astype(BF16)

    @pl.when(is_lat)
    def _():
        for b, r in enumerate(subs):
            qt_ref[b] = _rope_t(qts[b], cos_t_ref[:, r], sin_t_ref[:, r]).astype(BF16)
            k2_ref[r, :] = _dup_heads(_rope(ks[b], cos_ref[r, :], sin_ref[r, :])).astype(BF16)

    @pl.when(jnp.logical_not(is_lat))
    def _():
        for b, r in enumerate(subs):
            qt_ref[b] = qts[b].astype(BF16)
            k2_ref[r, :] = _dup_heads(ks[b]).astype(BF16)
            nk_ref[r, :] = ks[b]
            nv_ref[r, :] = vs[b]


def _rope_tables():
    rows = DEC_SEQ // GRID_W
    row = np.broadcast_to(np.arange(rows, dtype=np.float32)[:, None], (rows, GRID_W)).reshape(-1)
    col = np.broadcast_to(np.arange(GRID_W, dtype=np.float32)[None, :], (rows, GRID_W)).reshape(-1)
    axis_dim = HD // 2
    inv_freq = np.power(np.float32(ROPE_THETA),
                        -np.arange(0, axis_dim, 2, dtype=np.float32) / np.float32(axis_dim))
    ang = np.concatenate([row[:, None] * inv_freq, col[:, None] * inv_freq], axis=-1).astype(np.float32)
    cos, sin = np.cos(ang), np.sin(ang)
    quarter = HD // 4
    cos_h = np.concatenate([cos[:, :quarter], cos[:, :quarter], cos[:, quarter:], cos[:, quarter:]], axis=1)
    sin_h = np.concatenate([-sin[:, :quarter], sin[:, :quarter], -sin[:, quarter:], sin[:, quarter:]], axis=1)
    return (np.tile(cos_h, (1, LANES // HD)).astype(np.float32),
            np.tile(sin_h, (1, LANES // HD)).astype(np.float32))


def _head_indicators(width):
    col = np.arange(width)[:, None]
    head = np.arange(LANES)[None, :]
    ind = (col // HD == head).astype(np.float32)
    return jnp.asarray(ind, BF16), jnp.asarray(np.concatenate([ind.T, ind.T], axis=0), BF16)


def _dup_heads(z):
    lane = lax.broadcasted_iota(I32, (z.shape[0], LANES), 1)
    cols = []
    for p in range(N_KV // 2):
        blk = z[:, p * LANES:(p + 1) * LANES]
        swp = pltpu.roll(blk, HD, axis=1)
        cols += [jnp.where(lane < HD, blk, swp), jnp.where(lane < HD, swp, blk)]
    return jnp.concatenate(cols, axis=1)


def _qkv(xp, xs, mod, g, w_qkv, q_gain, k_gain):
    indq, indqt = _head_indicators(DQ)
    indk, indkt = _head_indicators(DKV)
    cos_t, sin_t = _rope_tables()
    per_batch = DEC_SEQ // TILE
    const = lambda shape: pl.BlockSpec(shape, lambda i: (0,) * len(shape))
    return pl.pallas_call(
        _qkv_kernel,
        grid=(NT,),
        in_specs=[
            pl.BlockSpec((TILE, D), lambda i: (jnp.minimum(i, NT_CTX - 1), 0)),
            pl.BlockSpec((TILE, D), lambda i: (jnp.maximum(i - NT_CTX, 0), 0)),
            pl.BlockSpec((1, N_MOD, D), lambda i: (_cond_index(i, TILE), 0, 0)),
            const((1, D)),
            const((D, DQ + 2 * DKV)),
            const((1, DQ)), const((1, DKV)),
            const((DQ, LANES)), const((2 * LANES, DQ)),
            const((DKV, LANES)), const((2 * LANES, DKV)),
            pl.BlockSpec((TILE, LANES), lambda i: (jnp.maximum(i - NT_CTX, 0) % per_batch, 0)),
            pl.BlockSpec((TILE, LANES), lambda i: (jnp.maximum(i - NT_CTX, 0) % per_batch, 0)),
            pl.BlockSpec((LANES, TILE), lambda i: (0, jnp.maximum(i - NT_CTX, 0) % per_batch)),
            pl.BlockSpec((LANES, TILE), lambda i: (0, jnp.maximum(i - NT_CTX, 0) % per_batch)),
        ],
        out_specs=[
            pl.BlockSpec((TILE // QB, DQ, QB), lambda i: (i, 0, 0)),
            pl.BlockSpec((TILE, 2 * DKV), lambda i: (i, 0)),
            pl.BlockSpec((DKV, TILE), lambda i: (0, i)),
            pl.BlockSpec((TILE, DKV), lambda i: (jnp.minimum(i, NT_CTX - 1), 0)),
            pl.BlockSpec((TILE, DKV), lambda i: (jnp.minimum(i, NT_CTX - 1), 0)),
        ],
        out_shape=[
            jax.ShapeDtypeStruct((T // QB, DQ, QB), BF16),
            jax.ShapeDtypeStruct((T, 2 * DKV), BF16),
            jax.ShapeDtypeStruct((DKV, T), BF16),
            jax.ShapeDtypeStruct((T_CTX, DKV), F32),
            jax.ShapeDtypeStruct((T_CTX, DKV), F32),
        ],
        compiler_params=_vmem_limit(56),
        name="qkv",
    )(xp, xs, mod, g, w_qkv, jnp.tile(q_gain, (1, N_HEADS)), jnp.tile(k_gain, (1, N_KV)),
      indq, indqt, indk, indkt, cos_t, sin_t, cos_t.T, sin_t.T)


def _unit_scores(u, pairs, k2_parts, qt_ref):
    cols = []
    for j in range(pairs):
        pair = u * pairs + j
        qt_pair = qt_ref[pair * LANES:(pair + 1) * LANES, :]
        sub = lax.broadcasted_iota(I32, qt_pair.shape, 0)
        zero = jnp.zeros_like(qt_pair)
        cols += [jnp.where(sub < HD, qt_pair, zero), jnp.where(sub >= HD, qt_pair, zero)]
    qtu = jnp.concatenate(cols, axis=1)
    return [jnp.dot(k2, qtu, preferred_element_type=F32) for k2 in k2_parts]


def _softmax_values(ss, vt_parts):
    m = ss[0].max(axis=0, keepdims=True)
    for s in ss[1:]:
        m = jnp.maximum(m, s.max(axis=0, keepdims=True))
    acc = None
    l = None
    for s, vt in zip(ss, vt_parts):
        e = jnp.exp2(s - m)
        ls = e.sum(axis=0, keepdims=True)
        l = ls if l is None else l + ls
        t = jnp.dot(vt, e.astype(BF16), preferred_element_type=F32)
        acc = t if acc is None else acc + t
    return acc * (1.0 / l)


def _attend(keys_of, values_of, qt_ref, o_ref, pairs, ahead):
    n_units = N_HEADS // (2 * pairs)
    group = lambda u: (u * 2 * pairs) // (N_HEADS // N_KV)
    scores = lambda u: _unit_scores(u, pairs, keys_of(group(u)), qt_ref)
    pending = [scores(u) for u in range(min(ahead, n_units))]
    outs = []
    for u in range(n_units):
        if u + ahead < n_units:
            pending.append(scores(u + ahead))
        outs.append(_softmax_values(pending.pop(0), values_of(group(u))))
    lq = o_ref.shape[0]
    heads = [o[:, h * lq:(h + 1) * lq] for o in outs for h in range(2 * pairs)]
    o_ref[...] = jnp.concatenate(heads, axis=0).T.astype(BF16)


def _attn_ctx_kernel(qt_ref, k2_ref, vt_ref, o_ref):
    _attend(lambda g: [k2_ref[:, g * LANES:(g + 1) * LANES]],
            lambda g: [vt_ref[g * HD:(g + 1) * HD, :]], qt_ref.at[0], o_ref, pairs=2, ahead=1)


def _attn_lat_kernel(qt_ref, k2c_ref, vtc_ref, k2n_ref, vtn_ref, wg_ref, wu_ref, wd_ref, o_in_ref,
                     o_ref, wgb_ref, wub_ref, wdb_ref):
    del o_in_ref
    wgb_ref[...] = wg_ref[...].astype(BF16)
    wub_ref[...] = wu_ref[...].astype(BF16)
    wdb_ref[...] = wd_ref[...].astype(BF16)
    pc = [slice(h * KEY_PART, (h + 1) * KEY_PART) for h in range(PAST // KEY_PART)]
    pn = [slice(h * KEY_PART, (h + 1) * KEY_PART) for h in range(DEC_SEQ // KEY_PART)]
    _attend(lambda g: [k2c_ref[0, r, g * LANES:(g + 1) * LANES] for r in pc]
            + [k2n_ref[r, g * LANES:(g + 1) * LANES] for r in pn],
            lambda g: [vtc_ref[0, g * HD:(g + 1) * HD, r] for r in pc]
            + [vtn_ref[g * HD:(g + 1) * HD, r] for r in pn],
            qt_ref.at[0], o_ref, pairs=1, ahead=2)


def _cache_kernel(ck_ref, cv_ref, k2_ref, vt_ref):
    k2_ref[0] = _dup_heads(ck_ref[0]).astype(BF16)
    vt_ref[0] = cv_ref[0].T.astype(BF16)


def _attention(qt, k2, vt, cache_k, cache_v, w_gate, w_up, w_down):
    k2c, vtc = pl.pallas_call(
        _cache_kernel,
        grid=(DEC_BATCH,),
        in_specs=[
            pl.BlockSpec((1, PAST, DKV), lambda b: (b, 0, 0)),
            pl.BlockSpec((1, PAST, DKV), lambda b: (b, 0, 0)),
        ],
        out_specs=[
            pl.BlockSpec((1, PAST, 2 * DKV), lambda b: (b, 0, 0)),
            pl.BlockSpec((1, DKV, PAST), lambda b: (b, 0, 0)),
        ],
        out_shape=[
            jax.ShapeDtypeStruct((DEC_BATCH, PAST, 2 * DKV), BF16),
            jax.ShapeDtypeStruct((DEC_BATCH, DKV, PAST), BF16),
        ],
        name="cache_prep",
    )(cache_k, cache_v)

    o_ctx = pl.pallas_call(
        _attn_ctx_kernel,
        grid=(BATCH,),
        in_specs=[
            pl.BlockSpec((1, DQ, QB), lambda b: (b, 0, 0)),
            pl.BlockSpec((SEQ, 2 * DKV), lambda b: (b, 0)),
            pl.BlockSpec((DKV, SEQ), lambda b: (0, b)),
        ],
        out_specs=pl.BlockSpec((SEQ, DQ), lambda b: (b, 0)),
        out_shape=jax.ShapeDtypeStruct((T, DQ), BF16),
        name="attn_ctx",
    )(qt, k2, vt)

    qb = QB
    nqb = DEC_SEQ // qb
    lat0 = T_CTX // qb
    depth = w_gate.shape[0]
    eps = (depth * NE) // (DEC_BATCH * nqb)
    per_layer = NE // eps
    wblock = lambda b, j: ((b * nqb + j) // per_layer, (b * nqb + j) % per_layer, 0, 0)
    return pl.pallas_call(
        _attn_lat_kernel,
        grid=(DEC_BATCH, nqb),
        in_specs=[
            pl.BlockSpec((1, DQ, QB), lambda b, j: (lat0 + b * nqb + j, 0, 0)),
            pl.BlockSpec((1, PAST, 2 * DKV), lambda b, j: (b, 0, 0)),
            pl.BlockSpec((1, DKV, PAST), lambda b, j: (b, 0, 0)),
            pl.BlockSpec((DEC_SEQ, 2 * DKV), lambda b, j: (T_CTX // DEC_SEQ + b, 0)),
            pl.BlockSpec((DKV, DEC_SEQ), lambda b, j: (0, T_CTX // DEC_SEQ + b)),
            pl.BlockSpec((1, eps, D, DE), wblock),
            pl.BlockSpec((1, eps, D, DE), wblock),
            pl.BlockSpec((1, eps, DE, D), wblock),
            pl.BlockSpec(memory_space=pl.ANY),
        ],
        out_specs=[
            pl.BlockSpec((qb, DQ), lambda b, j: (lat0 + b * nqb + j, 0)),
            pl.BlockSpec((1, eps, D, DE), wblock),
            pl.BlockSpec((1, eps, D, DE), wblock),
            pl.BlockSpec((1, eps, DE, D), wblock),
        ],
        out_shape=[
            jax.ShapeDtypeStruct((T, DQ), BF16),
            jax.ShapeDtypeStruct(w_gate.shape, BF16),
            jax.ShapeDtypeStruct(w_up.shape, BF16),
            jax.ShapeDtypeStruct(w_down.shape, BF16),
        ],
        input_output_aliases={8: 0},
        compiler_params=_vmem_limit(56),
        name="attn_lat",
    )(qt, k2c, vtc, k2, vt, w_gate, w_up, w_down, o_ctx)


def _route(h, wr_hl_ref, wr_hi_ref, tri_ref, cnt_ref, e_ref, rank_ref, wcol_ref, cnt_out_ref):
    tl = h.shape[0]
    h_hi = h.astype(BF16)
    h_lo = (h - h_hi.astype(F32)).astype(BF16)
    lg2 = jnp.dot(h_hi, wr_hl_ref[...], preferred_element_type=F32)
    lg = lg2[:, :LANES] + lg2[:, LANES:] + jnp.dot(h_lo, wr_hi_ref[...], preferred_element_type=F32)
    lt = lg.T

    sub8 = lax.broadcasted_iota(I32, (8, tl), 0).astype(F32)
    gl = jnp.where(sub8 < N_GROUPS, lt[0:8], NEG_INF)
    gmax = gl.max(axis=0, keepdims=True)
    gidx = jnp.min(jnp.where(gl == gmax, sub8, 8.0), axis=0, keepdims=True)
    gsum = jnp.sum(jnp.exp(gl - gmax), axis=0, keepdims=True)
    sel = lt[8:16]
    for g in range(1, N_GROUPS):
        sel = jnp.where(gidx == g, lt[8 + 8 * g:16 + 8 * g], sel)
    t1 = sel.max(axis=0, keepdims=True)
    j1 = jnp.min(jnp.where(sel == t1, sub8, 8.0), axis=0, keepdims=True)
    sel2 = jnp.where(sub8 == j1, NEG_INF, sel)
    t2 = sel2.max(axis=0, keepdims=True)
    j2 = jnp.min(jnp.where(sel2 == t2, sub8, 8.0), axis=0, keepdims=True)
    ex = jnp.exp(t2 - t1)
    den = 1.0 + ex
    gw = 1.0 / gsum
    w0 = gw * (1.0 / den)
    w1 = gw * (ex / den)
    e0 = gidx * E_PER_G + j1
    e1 = gidx * E_PER_G + j2

    sub_e = lax.broadcasted_iota(I32, (NE, tl), 0).astype(F32)
    oh0 = (sub_e == e0).astype(F32)
    oh1 = (sub_e == e1).astype(F32)
    c = oh0 + oh1
    carry = cnt_ref[:, 0:1]
    pieces = []
    for ch in range(tl // 256):
        cc = c[:, ch * 256:(ch + 1) * 256]
        pieces.append(jnp.dot(cc.astype(BF16), tri_ref[...], preferred_element_type=F32) + carry)
        carry = carry + jnp.sum(cc, axis=1, keepdims=True)
    csum = jnp.concatenate(pieces, axis=1)
    cnt_ref[...] = jnp.broadcast_to(carry, (NE, LANES))
    cnt_out_ref[...] = jnp.broadcast_to(carry, (NE, LANES))
    r0 = jnp.sum(oh0 * csum, axis=0, keepdims=True) - 1.0
    r1 = jnp.sum(oh1 * csum, axis=0, keepdims=True) - 1.0
    e_ref[0:1, :] = e0.astype(I32)
    e_ref[1:2, :] = e1.astype(I32)
    rank_ref[0:1, :] = r0.astype(I32)
    rank_ref[1:2, :] = r1.astype(I32)
    wcol_ref[0:1, :] = w0
    wcol_ref[1:2, :] = w1


def _post_kernel(o_ref, xp_ref, xs_ref, mod_ref, wo_ref, g_ref, wr_hl_ref, wr_hi_ref, tri_ref,
                 x1_ref, h_ref, e_ref, rank_ref, wcol_ref, cnt_out_ref, cnt_ref):
    i = pl.program_id(0)

    @pl.when(i == 0)
    def _():
        cnt_ref[...] = jnp.zeros_like(cnt_ref)

    x = jnp.where(i >= NT_CTX, xs_ref[...], xp_ref[...])
    m = jnp.dot(o_ref[...], wo_ref[...], preferred_element_type=F32)
    x1 = x + mod_ref[0, 2:3, :] * m
    x1_ref[...] = x1
    h = _modulate(x1, g_ref[...] * (1.0 + mod_ref[0, 4:5, :]), mod_ref[0, 3:4, :])
    h_ref[...] = _pack_rows(h)
    _route(h, wr_hl_ref, wr_hi_ref, tri_ref, cnt_ref, e_ref, rank_ref, wcol_ref, cnt_out_ref)


def _router_weights(w_group, w_expert):
    w = jnp.concatenate([w_group, jnp.zeros((D, 8 - N_GROUPS), F32),
                         jnp.transpose(w_expert, (1, 0, 2)).reshape(D, NE),
                         jnp.zeros((D, LANES - 8 - NE), F32)], axis=1)
    w_hi = w.astype(BF16)
    w_lo = (w - w_hi.astype(F32)).astype(BF16)
    return jnp.concatenate([w_hi, w_lo], axis=1), w_hi


def _route_outs(tile, first=0):
    specs = [
        pl.BlockSpec((tile, D), lambda i: (first + i, 0)),
        pl.BlockSpec((tile, HALF), lambda i: (first + i, 0)),
        pl.BlockSpec((TOP_K, tile), lambda i: (0, first + i)),
        pl.BlockSpec((TOP_K, tile), lambda i: (0, first + i)),
        pl.BlockSpec((TOP_K, tile), lambda i: (0, first + i)),
        pl.BlockSpec((NE, LANES), lambda i: (0, 0)),
    ]
    shapes = [
        jax.ShapeDtypeStruct((T, D), F32),
        jax.ShapeDtypeStruct((T, HALF), U32),
        jax.ShapeDtypeStruct((TOP_K, T), I32),
        jax.ShapeDtypeStruct((TOP_K, T), I32),
        jax.ShapeDtypeStruct((TOP_K, T), F32),
        jax.ShapeDtypeStruct((NE, LANES), F32),
    ]
    return specs, shapes


def _tri():
    a = np.arange(256)
    return jnp.asarray((a[:, None] <= a[None, :]).astype(np.float32), BF16)


def _post(o, xp, xs, mod, w_o, g_ffn, wr_hl, wr_hi):
    const = lambda shape: pl.BlockSpec(shape, lambda i: (0,) * len(shape))
    out_specs, out_shape = _route_outs(TILE)
    return pl.pallas_call(
        _post_kernel,
        grid=(NT,),
        in_specs=[
            pl.BlockSpec((TILE, DQ), lambda i: (i, 0)),
            pl.BlockSpec((TILE, D), lambda i: (jnp.minimum(i, NT_CTX - 1), 0)),
            pl.BlockSpec((TILE, D), lambda i: (jnp.maximum(i - NT_CTX, 0), 0)),
            pl.BlockSpec((1, N_MOD, D), lambda i: (_cond_index(i, TILE), 0, 0)),
            const((DQ, D)), const((1, D)), const((D, 2 * LANES)), const((D, LANES)), const((256, 256)),
        ],
        out_specs=out_specs,
        out_shape=out_shape,
        scratch_shapes=[pltpu.VMEM((NE, LANES), F32)],
        compiler_params=_vmem_limit(56),
        name="post_attn",
    )(o, xp, xs, mod, w_o, g_ffn, wr_hl, wr_hi, _tri())


def _pool_kernel(block_kinds, n_alias, y0_ref, y1_ref, xin_ref, wcin_ref, modp_ref, mod_ref, g_ref, pw_ref,
                 ps_ref, gf_ref, wr_hl_ref, wr_hi_ref, tri_ref, band_ref, inv_cnt_ref, cnt_in_ref, *refs):
    (x1_ref, h_ref, e_ref, rank_ref, wcol_ref, cnt_out_ref, cnt_ref, hhi_ref, hlo_ref) = refs[n_alias:]
    i = pl.program_id(0)

    @pl.when(i == 0)
    def _():
        cnt_ref[...] = cnt_in_ref[...]

    x = _combined(y0_ref, y1_ref, xin_ref, wcin_ref, modp_ref)
    h = _modulate(x, g_ref[...] * (1.0 + mod_ref[0, 1:2, :]), mod_ref[0, 0:1, :])
    @pl.when(i == 0)
    def _():
        zeros = jnp.zeros((PB_HALO, D), BF16)
        for ref in (hhi_ref, hlo_ref):
            ref[0:PB_HALO, :] = zeros
            ref[PB_HALO + PTILE:PB_HALO + PTILE + PB_HALO, :] = zeros

    h_hi = h.astype(BF16)
    hhi_ref[PB_HALO:PB_HALO + PTILE, :] = h_hi
    hlo_ref[PB_HALO:PB_HALO + PTILE, :] = (h - h_hi.astype(F32)).astype(BF16)

    outs = []
    for gi in range(len(POOL_WINDOWS)):
        cols = slice(gi * PGD, (gi + 1) * PGD)
        sums = []
        for b in range(PTILE // PB):
            bm = band_ref[block_kinds[b], gi]
            sums.append(jnp.dot(bm, hhi_ref[b * PB:b * PB + PBW, cols], preferred_element_type=F32)
                        + jnp.dot(bm, hlo_ref[b * PB:b * PB + PBW, cols], preferred_element_type=F32))
        diff = jnp.concatenate(sums, axis=0) * inv_cnt_ref[:, gi:gi + 1] - h[:, cols]
        outs.append(jnp.dot(diff.astype(BF16), pw_ref[gi], preferred_element_type=F32))
    m = jnp.concatenate(outs, axis=1) * ps_ref[...]
    x1 = x + mod_ref[0, 2:3, :] * m
    x1_ref[...] = x1
    hf = _modulate(x1, gf_ref[...] * (1.0 + mod_ref[0, 4:5, :]), mod_ref[0, 3:4, :])
    h_ref[...] = _pack_rows(hf)
    _route(hf, wr_hl_ref, wr_hi_ref, tri_ref, cnt_ref, e_ref, rank_ref, wcol_ref, cnt_out_ref)


def _pool_constants(seq):
    r = np.arange(PB)[:, None]
    d = np.arange(PBW)[None, :] - PB_HALO - r
    found, kinds = {}, []
    for b in range(PTILE // PB):
        tm = (b * PB + r) % seq
        inseq = (tm + d >= 0) & (tm + d < seq)
        mats = np.stack([(inseq & (d >= -(w // 2)) & (d <= w - w // 2 - 1)).astype(np.float32)
                         for w in POOL_WINDOWS])
        kinds.append(found.setdefault(mats.tobytes(), (len(found), mats))[0])
    bands = np.stack([m for _, m in sorted(found.values(), key=lambda kv: kv[0])])
    tm = np.arange(PTILE) % seq
    inv_cnt = np.zeros((PTILE, LANES), np.float32)
    for g, w in enumerate(POOL_WINDOWS):
        left, right = w // 2, w - w // 2 - 1
        inv_cnt[:, g] = 1.0 / (np.minimum(tm + right + 1, seq) - np.maximum(tm - left, 0)).astype(np.float32)
    return jnp.asarray(bands, BF16), tuple(kinds), inv_cnt


def _pool(group, y2g, x, wcol, mod_prev, mod, g_mix, pool_w, pool_scale, g_ffn, wr_hl, wr_hi, cnt_in, prev):
    const = lambda shape: pl.BlockSpec(shape, lambda i: (0,) * len(shape))
    tiles = (T_CTX if group == 0 else T_LAT) // PTILE
    first = 0 if group == 0 else T_CTX // PTILE
    out_specs, out_shape = _route_outs(PTILE, first)
    aliases = () if prev is None else tuple(prev)
    bands, block_kinds, inv_cnt = _pool_constants(SEQ if group == 0 else DEC_SEQ)
    n_fixed = 16
    return pl.pallas_call(
        functools.partial(_pool_kernel, block_kinds, len(aliases)),
        grid=(tiles,),
        in_specs=[
            pl.BlockSpec((PTILE, HALF), lambda i: (i, 0)),
            pl.BlockSpec((PTILE, HALF), lambda i: (tiles + i, 0)),
            pl.BlockSpec((PTILE, D), lambda i: (first + i, 0)),
            pl.BlockSpec((TOP_K, PTILE), lambda i: (0, first + i)),
            pl.BlockSpec((1, N_MOD, D), lambda i: (_cond_index(first + i, PTILE), 0, 0)),
            pl.BlockSpec((1, N_MOD, D), lambda i: (_cond_index(first + i, PTILE), 0, 0)),
            const((1, D)), const((len(POOL_WINDOWS), PGD, PGD)), const((1, D)), const((1, D)),
            const((D, 2 * LANES)), const((D, LANES)), const((256, 256)), const(bands.shape),
            const((PTILE, LANES)), const((NE, LANES)),
        ] + [pl.BlockSpec(memory_space=pl.ANY)] * len(aliases),
        out_specs=out_specs,
        out_shape=out_shape,
        input_output_aliases={n_fixed + k: k for k in range(len(aliases))},
        scratch_shapes=[pltpu.VMEM((NE, LANES), F32), pltpu.VMEM((PTILE + 2 * PB_HALO, D), BF16),
                        pltpu.VMEM((PTILE + 2 * PB_HALO, D), BF16)],
        compiler_params=_vmem_limit(56),
        name="pool_mixer",
    )(y2g, y2g, x, wcol, mod_prev, mod, g_mix, pool_w, pool_scale, g_ffn, wr_hl, wr_hi, _tri(), bands, inv_cnt,
      cnt_in, *aliases)


def _plan_kernel(e_ref, rank_ref, cnt_ref, pos_ref, te_ref, used_ref):
    counts = cnt_ref[:, 0:1]
    tiles_col = jnp.floor((counts + (TM - 1)) * (1.0 / TM))
    sub = lax.broadcasted_iota(I32, (NE, NE), 0)
    lane = lax.broadcasted_iota(I32, (NE, NE), 1)
    tiles_row = jnp.sum(jnp.where(sub == lane, tiles_col, 0.0), axis=0, keepdims=True)
    first_col = jnp.sum(jnp.where(lane < sub, tiles_row, 0.0), axis=1, keepdims=True)
    end_col = first_col + tiles_col
    used = jnp.sum(tiles_col, axis=0, keepdims=True)
    base_col = first_col * TM
    sub_e = lax.broadcasted_iota(I32, (NE, T), 0)
    for k in range(TOP_K):
        hit = sub_e == e_ref[k:k + 1, :]
        base = jnp.sum(jnp.where(hit, base_col, 0.0), axis=0, keepdims=True)
        pos_ref[k:k + 1, :] = base.astype(I32) + rank_ref[k:k + 1, :]
    tile = jnp.minimum(lax.broadcasted_iota(I32, (NE, LANES), 1).astype(F32), used - 1.0)
    te_ref[...] = jnp.sum(jnp.where(tile >= end_col, 1.0, 0.0), axis=0, keepdims=True).astype(I32)
    used_ref[...] = jnp.broadcast_to(used, (1, LANES)).astype(I32)


def _plan(e, rank, cnt):
    assert N_ROW_TILES <= LANES
    pos, te, used = pl.pallas_call(
        _plan_kernel,
        out_shape=[jax.ShapeDtypeStruct((TOP_K, T), I32), jax.ShapeDtypeStruct((1, LANES), I32),
                   jax.ShapeDtypeStruct((1, LANES), I32)],
        name="moe_plan",
    )(e, rank, cnt)
    return pos, te[0, :N_ROW_TILES], used[0, :1]


def _sc_mesh():
    return plsc.VectorSubcoreMesh(core_axis_name="core", subcore_axis_name="subcore")


def _sc_worker():
    return lax.axis_index("core") * SC_SUBCORES + lax.axis_index("subcore")


def _sc_dispatch(h, pos):
    per = T // SC_WORKERS
    nwin = per // SC_WIN

    @functools.partial(
        pl.kernel, out_type=jax.ShapeDtypeStruct((NPAD, HALF), U32), mesh=_sc_mesh(),
        scratch_types=[pltpu.VMEM((2, SC_WIN, HALF), U32), pltpu.VMEM((2, SC_WIN), I32),
                       pltpu.VMEM((2, SC_WIN), I32), pltpu.SemaphoreType.DMA((2,)),
                       pltpu.SemaphoreType.DMA((2,))])
    def run(h_hbm, pos_hbm, xs_hbm, buf, idx0, idx1, sem_in, sem_out):
        first = _sc_worker() * per

        def loads(j, s):
            base = first + j * SC_WIN
            return (pltpu.make_async_copy(h_hbm.at[pl.ds(base, SC_WIN)], buf.at[s], sem_in.at[s]),
                    pltpu.make_async_copy(pos_hbm.at[pl.ds(base, SC_WIN)], idx0.at[s], sem_in.at[s]),
                    pltpu.make_async_copy(pos_hbm.at[pl.ds(T + base, SC_WIN)], idx1.at[s], sem_in.at[s]))

        def stores(s):
            return (pltpu.make_async_copy(buf.at[s], xs_hbm.at[idx0.at[s]], sem_out.at[s]),
                    pltpu.make_async_copy(buf.at[s], xs_hbm.at[idx1.at[s]], sem_out.at[s]))

        for c in loads(0, 0):
            c.start()

        @pl.loop(0, nwin // 2)
        def _(jj):
            for s in range(2):
                j = jj * 2 + s
                for c in loads(j, s):
                    c.wait()
                for c in stores(s):
                    c.start()

                @pl.when(j >= 1)
                def _():
                    for c in stores(1 - s):
                        c.wait()

                @pl.when(j + 1 < nwin)
                def _():
                    for c in loads(j + 1, 1 - s):
                        c.start()

        for c in stores((nwin - 1) % 2):
            c.wait()

    return run(h, pos.reshape(TOP_K * T))


def _sc_gather(ys, pos, group):
    t0, n_g = (0, T_CTX) if group == 0 else (T_CTX, T_LAT)
    idx_all = jnp.concatenate([pos[k, t0:t0 + n_g] for k in range(TOP_K)])
    n = TOP_K * n_g
    per = n // SC_WORKERS
    nwin = per // SC_WIN

    @functools.partial(
        pl.kernel, out_type=jax.ShapeDtypeStruct((n, HALF), U32), mesh=_sc_mesh(),
        scratch_types=[pltpu.VMEM((2, SC_WIN, HALF), U32), pltpu.VMEM((per,), I32),
                       pltpu.SemaphoreType.DMA((2,)), pltpu.SemaphoreType.DMA((2,))])
    def run(ys_hbm, pos_hbm, out_hbm, buf, idx, sem_in, sem_out):
        first = _sc_worker() * per
        pltpu.sync_copy(pos_hbm.at[pl.ds(first, per)], idx)

        def load(j, s):
            return pltpu.make_async_copy(ys_hbm.at[idx.at[pl.ds(j * SC_WIN, SC_WIN)]], buf.at[s], sem_in.at[s])

        def store(j, s):
            return pltpu.make_async_copy(buf.at[s], out_hbm.at[pl.ds(first + j * SC_WIN, SC_WIN)], sem_out.at[s])

        load(0, 0).start()

        @pl.loop(0, nwin // 2)
        def _(jj):
            for s in range(2):
                j = jj * 2 + s
                load(j, s).wait()
                store(j, s).start()

                @pl.when(j >= 1)
                def _():
                    store(j - 1, 1 - s).wait()

                @pl.when(j + 1 < nwin)
                def _():
                    load(j + 1, 1 - s).start()

        store(nwin - 1, (nwin - 1) % 2).wait()

    return run(ys, idx_all)


def _moe_kernel(te_ref, used_ref, x_hbm, wg_ref, wu_ref, wd_ref, y_ref, xbuf, sem):
    del te_ref
    i = pl.program_id(0)
    used = used_ref[0]

    def tile_copy(t):
        slot = lax.rem(t, X_SLOTS)
        return pltpu.make_async_copy(x_hbm.at[pl.ds(t * TM, TM)], xbuf.at[slot], sem.at[slot])

    @pl.when(i == 0)
    def _():
        for t in range(X_SLOTS - 1):
            @pl.when(t < used)
            def _():
                tile_copy(t).start()

    @pl.when(i + (X_SLOTS - 1) < used)
    def _():
        tile_copy(i + (X_SLOTS - 1)).start()

    @pl.when(i < used)
    def _():
        tile_copy(i).wait()
        hi, lo = _unpack_rows(xbuf[lax.rem(i, X_SLOTS)])
        x = jnp.concatenate([hi, lo], axis=1).astype(BF16)
        a = jnp.dot(x, wg_ref[0, 0], preferred_element_type=F32)
        b = jnp.dot(x, wu_ref[0, 0], preferred_element_type=F32)
        hid = a * (1.0 / (1.0 + jnp.exp(-a))) * b
        y_ref[...] = _pack_rows(jnp.dot(hid.astype(BF16), wd_ref[0, 0], preferred_element_type=F32))


def _experts(xs, tile_expert, used, layer, w_gate, w_up, w_down):
    row = lambda i, te, used: (jnp.minimum(i, used[0] - 1), 0)
    weights = lambda i, te, used: (layer, te[i], 0, 0)
    grid_spec = pltpu.PrefetchScalarGridSpec(
        num_scalar_prefetch=2,
        grid=(N_ROW_TILES,),
        in_specs=[
            pl.BlockSpec(memory_space=pl.ANY),
            pl.BlockSpec((1, 1, D, DE), weights),
            pl.BlockSpec((1, 1, D, DE), weights),
            pl.BlockSpec((1, 1, DE, D), weights),
        ],
        out_specs=pl.BlockSpec((TM, HALF), row),
        scratch_shapes=[pltpu.VMEM((X_SLOTS, TM, HALF), U32), pltpu.SemaphoreType.DMA((X_SLOTS,))],
    )
    return pl.pallas_call(
        _moe_kernel,
        grid_spec=grid_spec,
        out_shape=jax.ShapeDtypeStruct((NPAD, HALF), U32),
        compiler_params=pltpu.CompilerParams(dimension_semantics=("arbitrary",),
                                             vmem_limit_bytes=40 * 1024 * 1024),
        name="moe_experts",
    )(tile_expert, used, xs, w_gate, w_up, w_down)


def _combined(y0_ref, y1_ref, x_ref, wcol_ref, mod_ref):
    sub = lax.broadcasted_iota(I32, (LANES, wcol_ref.shape[1]), 0)
    wt = jnp.where(sub == 0, wcol_ref[0:1, :], jnp.where(sub == 1, wcol_ref[1:2, :], 0.0)).T
    w0 = wt[:, 0:1]
    w1 = wt[:, 1:2]
    hi0, lo0 = _unpack_rows(y0_ref[...])
    hi1, lo1 = _unpack_rows(y1_ref[...])
    y = jnp.concatenate([w0 * hi0 + w1 * hi1, w0 * lo0 + w1 * lo1], axis=1)
    return x_ref[...] + mod_ref[0, 5:6, :] * y


def _combine_kernel(y0_ref, y1_ref, x_ref, wcol_ref, mod_ref, o_ref):
    o_ref[...] = _combined(y0_ref, y1_ref, x_ref, wcol_ref, mod_ref)


def _combine(group, y2g, x, wcol, mod):
    n_g = T_CTX if group == 0 else T_LAT
    tiles = n_g // CT
    first = 0 if group == 0 else T_CTX // CT
    return pl.pallas_call(
        _combine_kernel,
        grid=(tiles,),
        in_specs=[
            pl.BlockSpec((CT, HALF), lambda i: (i, 0)),
            pl.BlockSpec((CT, HALF), lambda i: (tiles + i, 0)),
            pl.BlockSpec((CT, D), lambda i: (first + i, 0)),
            pl.BlockSpec((TOP_K, CT), lambda i: (0, first + i)),
            pl.BlockSpec((1, N_MOD, D), lambda i: (_cond_index(first + i, CT), 0, 0)),
        ],
        out_specs=pl.BlockSpec((CT, D), lambda i: (i, 0)),
        out_shape=jax.ShapeDtypeStruct((n_g, D), F32),
        name="moe_combine",
    )(y2g, y2g, x, wcol, mod)


def _moe(h, e, rank, cnt, layer, w_gate, w_up, w_down, rider):
    pos, tile_expert, used = _plan(e, rank, cnt)
    xs, rider = lax.optimization_barrier((_sc_dispatch(h, pos), rider))
    ys = _experts(xs, tile_expert, used, layer, w_gate, w_up, w_down)
    return (_sc_gather(ys, pos, 0), _sc_gather(ys, pos, 1)), rider


def kernel(x_prompt, x_sample, c, cache_k, cache_v, c_ctx, norm_mix_g, norm_ffn_g, ada_w, ada_b, attn_w_qkv, attn_q_norm, attn_k_norm, attn_w_o, pool_w, pool_scale, moe_w_group, moe_w_expert, moe_w_gate, moe_w_up, moe_w_down):
    xp = x_prompt.reshape(T_CTX, D)
    xs = x_sample.reshape(T_LAT, D)
    cond = jnp.concatenate([c_ctx[None, :], c, jnp.zeros((N_COND - 1 - DEC_BATCH, D), F32)], axis=0)
    mod = [_ada(cond, ada_w, ada_b, layer) for layer in range(ada_w.shape[0])]

    qt, k2, vt, new_k, new_v = _qkv(xp, xs, mod[0], norm_mix_g[0:1], attn_w_qkv[0].astype(BF16),
                                    attn_q_norm[0:1], attn_k_norm[0:1])
    o, w_gate, w_up, w_down = _attention(qt, k2, vt, cache_k[:, 0].reshape(DEC_BATCH, PAST, DKV),
                                         cache_v[:, 0].reshape(DEC_BATCH, PAST, DKV),
                                         moe_w_gate, moe_w_up, moe_w_down)
    wr_hl, wr_hi = _router_weights(moe_w_group[0], moe_w_expert[0])
    x1, h, e, rank, wcol, cnt = _post(o, xp, xs, mod[0], attn_w_o[0].astype(BF16), norm_ffn_g[0:1],
                                      wr_hl, wr_hi)
    y2, new_k = _moe(h, e, rank, cnt, 0, w_gate, w_up, w_down, new_k.reshape(BATCH, 1, SEQ, N_KV, HD))

    wr_hl, wr_hi = _router_weights(moe_w_group[1], moe_w_expert[1])
    pool_args = (x1, wcol, mod[0], mod[1], norm_mix_g[1:2], pool_w[0].astype(BF16), pool_scale[0:1],
                 norm_ffn_g[1:2], wr_hl, wr_hi)
    outs = _pool(0, y2[0], *pool_args, jnp.zeros((NE, LANES), F32), None)
    x3, h, e, rank, wcol, cnt = _pool(1, y2[1], *pool_args, outs[5], outs[:5])
    y2, new_v = _moe(h, e, rank, cnt, 1, w_gate, w_up, w_down, new_v.reshape(BATCH, 1, SEQ, N_KV, HD))
    y_prompt = _combine(0, y2[0], x3, wcol, mod[1])
    y_sample = _combine(1, y2[1], x3, wcol, mod[1])

    return (y_prompt.reshape(BATCH, SEQ, D), y_sample.reshape(DEC_BATCH, DEC_SEQ, D), new_k, new_v)
```

```python
import functools

import jax
import jax.numpy as jnp
import numpy as np
from jax import lax
from jax.experimental import pallas as pl
from jax.experimental.pallas import tpu as pltpu
from jax.experimental.pallas import tpu_sc as plsc

F32 = jnp.float32
BF16 = jnp.bfloat16
I32 = jnp.int32
U32 = jnp.uint32

D = 1024
BATCH, SEQ = 32, 256
DEC_BATCH, DEC_SEQ, PAST = 8, 1024, 512
T_CTX = BATCH * SEQ
T_LAT = DEC_BATCH * DEC_SEQ
T = T_CTX + T_LAT
GRID_W = 64
N_HEADS, N_KV, HD = 16, 4, 64
DQ = N_HEADS * HD
DKV = N_KV * HD
ROPE_THETA = 10000.0
POOL_WINDOWS = (2, 4, 8, 16)
PGD = D // len(POOL_WINDOWS)
N_GROUPS, E_PER_G, TOP_K = 4, 8, 2
NE = N_GROUPS * E_PER_G
DE = D // 4
N_MOD = 6
EPS = 1e-6
N_COND = 16

TILE = 1024
NT = T // TILE
NT_CTX = T_CTX // TILE
QB = 256
KEY_PART = 256
PTILE = 1024
PB = 128
PBW = 256
PB_HALO = (PBW - PB) // 2
TM = 1024
X_SLOTS = 3
N_ROW_TILES = (TOP_K * T) // TM + NE
NPAD = N_ROW_TILES * TM
CT = 1024
LANES = 128
NEG_INF = float("-inf")
LOG2_E = 1.4426950408889634
Q_SCALE = HD ** -0.5 * LOG2_E
HALF = D // 2
SC_CORES, SC_SUBCORES = 2, 16
SC_WORKERS = SC_CORES * SC_SUBCORES
SC_WIN = 64


def _vmem_limit(mib):
    return pltpu.CompilerParams(vmem_limit_bytes=mib * 1024 * 1024)


def _modulate(x, gain_scale, shift):
    ms = jnp.mean(x * x, axis=-1, keepdims=True)
    return (x * lax.rsqrt(ms + EPS)) * gain_scale + shift


def _head_norm(z, gain, ind, ind_t2):
    ss = jnp.dot((z * z).astype(BF16), ind, preferred_element_type=F32)
    inv = lax.rsqrt(ss * (1.0 / HD) + EPS)
    inv_hi = inv.astype(BF16)
    inv_lo = (inv - inv_hi.astype(F32)).astype(BF16)
    scale = jnp.dot(jnp.concatenate([inv_hi, inv_lo], axis=1), ind_t2, preferred_element_type=F32)
    return z * scale * gain


def _rope(z, cos_t, sin_t):
    lane = lax.broadcasted_iota(I32, (z.shape[0], LANES), 1)
    low = (lane % 32) < 16
    outs = []
    for c in range(z.shape[1] // LANES):
        zc = z[:, c * LANES:(c + 1) * LANES]
        up = pltpu.roll(zc, 16, axis=1)
        dn = pltpu.roll(zc, LANES - 16, axis=1)
        outs.append(zc * cos_t + jnp.where(low, dn, up) * sin_t)
    return jnp.concatenate(outs, axis=1)


def _rope_t(zt, cos_tt, sin_tt):
    sub = lax.broadcasted_iota(I32, (LANES, zt.shape[1]), 0)
    low = (sub % 32) < 16
    outs = []
    for c in range(zt.shape[0] // LANES):
        zc = zt[c * LANES:(c + 1) * LANES]
        up = pltpu.roll(zc, 16, axis=0)
        dn = pltpu.roll(zc, LANES - 16, axis=0)
        outs.append(zc * cos_tt + jnp.where(low, dn, up) * sin_tt)
    return jnp.concatenate(outs, axis=0)


def _pack_rows(z):
    bits = lax.bitcast_convert_type(z.astype(BF16).astype(F32), U32)
    return bits[:, :HALF] | (bits[:, HALF:] >> 16)


def _unpack_rows(p):
    hi = lax.bitcast_convert_type(p & jnp.uint32(0xFFFF0000), F32)
    lo = lax.bitcast_convert_type(p << 16, F32)
    return hi, lo


def _ada_kernel(c_ref, w_ref, b_ref, o_ref):
    c = c_ref[...]
    a = c * (1.0 / (1.0 + jnp.exp(-c)))
    o_ref[0] = jnp.dot(a.astype(BF16), w_ref[0].astype(BF16), preferred_element_type=F32) + b_ref[0]


def _ada(cond, ada_w, ada_b, layer):
    nb = 1536
    depth = ada_w.shape[0]
    out = pl.pallas_call(
        _ada_kernel,
        grid=((N_MOD * D) // nb,),
        in_specs=[
            pl.BlockSpec((N_COND, D), lambda j: (0, 0)),
            pl.BlockSpec((1, D, nb), lambda j: (layer, 0, j)),
            pl.BlockSpec((1, 1, nb), lambda j: (layer, 0, j)),
        ],
        out_specs=pl.BlockSpec((1, N_COND, nb), lambda j: (0, 0, j)),
        out_shape=jax.ShapeDtypeStruct((1, N_COND, N_MOD * D), F32),
        compiler_params=_vmem_limit(40),
        name="ada",
    )(cond, ada_w, ada_b.reshape(depth, 1, N_MOD * D))
    return out.reshape(N_COND, N_MOD, D)


def _cond_index(i, tile):
    n_ctx = T_CTX // tile
    per_batch = DEC_SEQ // tile
    return jnp.where(i < n_ctx, 0, 1 + (i - n_ctx) // per_batch)


def _qkv_kernel(xp_ref, xs_ref, mod_ref, g_ref, w_ref, qg_ref, kg_ref, indq_ref, indqt_ref,
                indk_ref, indkt_ref, cos_ref, sin_ref, cos_t_ref, sin_t_ref,
                qt_ref, k2_ref, vt_ref, nk_ref, nv_ref):
    i = pl.program_id(0)
    is_lat = i >= NT_CTX
    gain_scale = g_ref[...] * (1.0 + mod_ref[0, 1:2, :])
    subs = [slice(b * QB, (b + 1) * QB) for b in range(TILE // QB)]
    hs = [_modulate(jnp.where(is_lat, xs_ref[r, :], xp_ref[r, :]), gain_scale, mod_ref[0, 0:1, :]).astype(BF16)
          for r in subs]
    qkvs = [jnp.dot(h, w_ref[...], preferred_element_type=F32) for h in hs]
    qs = [_head_norm(z[:, :DQ], qg_ref[...], indq_ref[...], indqt_ref[...]) for z in qkvs]
    ks = [_head_norm(z[:, DQ:DQ + DKV], kg_ref[...], indk_ref[...], indkt_ref[...]) for z in qkvs]
    vs = [z[:, DQ + DKV:] for z in qkvs]
    qts = [(q * Q_SCALE).T for q in qs]
    for r, v in zip(subs, vs):
        vt_ref[:, r] = v.T.astype(BF16)

    @pl.when(is_lat)
    def _():
        for b, r in enumerate(subs):
            qt_ref[b] = _rope_t(qts[b], cos_t_ref[:, r], sin_t_ref[:, r]).astype(BF16)
            k2_ref[r, :] = _dup_heads(_rope(ks[b], cos_ref[r, :], sin_ref[r, :])).astype(BF16)

    @pl.when(jnp.logical_not(is_lat))
    def _():
        for b, r in enumerate(subs):
            qt_ref[b] = qts[b].astype(BF16)
            k2_ref[r, :] = _dup_heads(ks[b]).astype(BF16)
            nk_ref[r, :] = ks[b]
            nv_ref[r, :] = vs[b]


def _rope_tables():
    rows = DEC_SEQ // GRID_W
    row = np.broadcast_to(np.arange(rows, dtype=np.float32)[:, None], (rows, GRID_W)).reshape(-1)
    col = np.broadcast_to(np.arange(GRID_W, dtype=np.float32)[None, :], (rows, GRID_W)).reshape(-1)
    axis_dim = HD // 2
    inv_freq = np.power(np.float32(ROPE_THETA),
                        -np.arange(0, axis_dim, 2, dtype=np.float32) / np.float32(axis_dim))
    ang = np.concatenate([row[:, None] * inv_freq, col[:, None] * inv_freq], axis=-1).astype(np.float32)
    cos, sin = np.cos(ang), np.sin(ang)
    quarter = HD // 4
    cos_h = np.concatenate([cos[:, :quarter], cos[:, :quarter], cos[:, quarter:], cos[:, quarter:]], axis=1)
    sin_h = np.concatenate([-sin[:, :quarter], sin[:, :quarter], -sin[:, quarter:], sin[:, quarter:]], axis=1)
    return (np.tile(cos_h, (1, LANES // HD)).astype(np.float32),
            np.tile(sin_h, (1, LANES // HD)).astype(np.float32))


def _head_indicators(width):
    col = np.arange(width)[:, None]
    head = np.arange(LANES)[None, :]
    ind = (col // HD == head).astype(np.float32)
    return jnp.asarray(ind, BF16), jnp.asarray(np.concatenate([ind.T, ind.T], axis=0), BF16)


def _dup_heads(z):
    lane = lax.broadcasted_iota(I32, (z.shape[0], LANES), 1)
    cols = []
    for p in range(N_KV // 2):
        blk = z[:, p * LANES:(p + 1) * LANES]
        swp = pltpu.roll(blk, HD, axis=1)
        cols += [jnp.where(lane < HD, blk, swp), jnp.where(lane < HD, swp, blk)]
    return jnp.concatenate(cols, axis=1)


def _qkv(xp, xs, mod, g, w_qkv, q_gain, k_gain):
    indq, indqt = _head_indicators(DQ)
    indk, indkt = _head_indicators(DKV)
    cos_t, sin_t = _rope_tables()
    per_batch = DEC_SEQ // TILE
    const = lambda shape: pl.BlockSpec(shape, lambda i: (0,) * len(shape))
    return pl.pallas_call(
        _qkv_kernel,
        grid=(NT,),
        in_specs=[
            pl.BlockSpec((TILE, D), lambda i: (jnp.minimum(i, NT_CTX - 1), 0)),
            pl.BlockSpec((TILE, D), lambda i: (jnp.maximum(i - NT_CTX, 0), 0)),
            pl.BlockSpec((1, N_MOD, D), lambda i: (_cond_index(i, TILE), 0, 0)),
            const((1, D)),
            const((D, DQ + 2 * DKV)),
            const((1, DQ)), const((1, DKV)),
            const((DQ, LANES)), const((2 * LANES, DQ)),
            const((DKV, LANES)), const((2 * LANES, DKV)),
            pl.BlockSpec((TILE, LANES), lambda i: (jnp.maximum(i - NT_CTX, 0) % per_batch, 0)),
            pl.BlockSpec((TILE, LANES), lambda i: (jnp.maximum(i - NT_CTX, 0) % per_batch, 0)),
            pl.BlockSpec((LANES, TILE), lambda i: (0, jnp.maximum(i - NT_CTX, 0) % per_batch)),
            pl.BlockSpec((LANES, TILE), lambda i: (0, jnp.maximum(i - NT_CTX, 0) % per_batch)),
        ],
        out_specs=[
            pl.BlockSpec((TILE // QB, DQ, QB), lambda i: (i, 0, 0)),
            pl.BlockSpec((TILE, 2 * DKV), lambda i: (i, 0)),
            pl.BlockSpec((DKV, TILE), lambda i: (0, i)),
            pl.BlockSpec((TILE, DKV), lambda i: (jnp.minimum(i, NT_CTX - 1), 0)),
            pl.BlockSpec((TILE, DKV), lambda i: (jnp.minimum(i, NT_CTX - 1), 0)),
        ],
        out_shape=[
            jax.ShapeDtypeStruct((T // QB, DQ, QB), BF16),
            jax.ShapeDtypeStruct((T, 2 * DKV), BF16),
            jax.ShapeDtypeStruct((DKV, T), BF16),
            jax.ShapeDtypeStruct((T_CTX, DKV), F32),
            jax.ShapeDtypeStruct((T_CTX, DKV), F32),
        ],
        compiler_params=_vmem_limit(56),
        name="qkv",
    )(xp, xs, mod, g, w_qkv, jnp.tile(q_gain, (1, N_HEADS)), jnp.tile(k_gain, (1, N_KV)),
      indq, indqt, indk, indkt, cos_t, sin_t, cos_t.T, sin_t.T)


def _unit_scores(u, pairs, k2_parts, qt_ref):
    cols = []
    for j in range(pairs):
        pair = u * pairs + j
        qt_pair = qt_ref[pair * LANES:(pair + 1) * LANES, :]
        sub = lax.broadcasted_iota(I32, qt_pair.shape, 0)
        zero = jnp.zeros_like(qt_pair)
        cols += [jnp.where(sub < HD, qt_pair, zero), jnp.where(sub >= HD, qt_pair, zero)]
    qtu = jnp.concatenate(cols, axis=1)
    return [jnp.dot(k2, qtu, preferred_element_type=F32) for k2 in k2_parts]


def _softmax_values(ss, vt_parts):
    m = ss[0].max(axis=0, keepdims=True)
    for s in ss[1:]:
        m = jnp.maximum(m, s.max(axis=0, keepdims=True))
    acc = None
    l = None
    for s, vt in zip(ss, vt_parts):
        e = jnp.exp2(s - m)
        ls = e.sum(axis=0, keepdims=True)
        l = ls if l is None else l + ls
        t = jnp.dot(vt, e.astype(BF16), preferred_element_type=F32)
        acc = t if acc is None else acc + t
    return acc * (1.0 / l)


def _attend(keys_of, values_of, qt_ref, o_ref, pairs, ahead):
    n_units = N_HEADS // (2 * pairs)
    group = lambda u: (u * 2 * pairs) // (N_HEADS // N_KV)
    scores = lambda u: _unit_scores(u, pairs, keys_of(group(u)), qt_ref)
    pending = [scores(u) for u in range(min(ahead, n_units))]
    outs = []
    for u in range(n_units):
        if u + ahead < n_units:
            pending.append(scores(u + ahead))
        outs.append(_softmax_values(pending.pop(0), values_of(group(u))))
    lq = o_ref.shape[0]
    heads = [o[:, h * lq:(h + 1) * lq] for o in outs for h in range(2 * pairs)]
    o_ref[...] = jnp.concatenate(heads, axis=0).T.astype(BF16)


def _attn_ctx_kernel(qt_ref, k2_ref, vt_ref, o_ref):
    _attend(lambda g: [k2_ref[:, g * LANES:(g + 1) * LANES]],
            lambda g: [vt_ref[g * HD:(g + 1) * HD, :]], qt_ref.at[0], o_ref, pairs=2, ahead=1)


def _attn_lat_kernel(qt_ref, k2c_ref, vtc_ref, k2n_ref, vtn_ref, wg_ref, wu_ref, wd_ref, o_in_ref,
                     o_ref, wgb_ref, wub_ref, wdb_ref):
    del o_in_ref
    wgb_ref[...] = wg_ref[...].astype(BF16)
    wub_ref[...] = wu_ref[...].astype(BF16)
    wdb_ref[...] = wd_ref[...].astype(BF16)
    pc = [slice(h * KEY_PART, (h + 1) * KEY_PART) for h in range(PAST // KEY_PART)]
    pn = [slice(h * KEY_PART, (h + 1) * KEY_PART) for h in range(DEC_SEQ // KEY_PART)]
    _attend(lambda g: [k2c_ref[0, r, g * LANES:(g + 1) * LANES] for r in pc]
            + [k2n_ref[r, g * LANES:(g + 1) * LANES] for r in pn],
            lambda g: [vtc_ref[0, g * HD:(g + 1) * HD, r] for r in pc]
            + [vtn_ref[g * HD:(g + 1) * HD, r] for r in pn],
            qt_ref.at[0], o_ref, pairs=1, ahead=2)


def _cache_kernel(ck_ref, cv_ref, k2_ref, vt_ref):
    k2_ref[0] = _dup_heads(ck_ref[0]).astype(BF16)
    vt_ref[0] = cv_ref[0].T.astype(BF16)


def _attention(qt, k2, vt, cache_k, cache_v, w_gate, w_up, w_down):
    k2c, vtc = pl.pallas_call(
        _cache_kernel,
        grid=(DEC_BATCH,),
        in_specs=[
            pl.BlockSpec((1, PAST, DKV), lambda b: (b, 0, 0)),
            pl.BlockSpec((1, PAST, DKV), lambda b: (b, 0, 0)),
        ],
        out_specs=[
            pl.BlockSpec((1, PAST, 2 * DKV), lambda b: (b, 0, 0)),
            pl.BlockSpec((1, DKV, PAST), lambda b: (b, 0, 0)),
        ],
        out_shape=[
            jax.ShapeDtypeStruct((DEC_BATCH, PAST, 2 * DKV), BF16),
            jax.ShapeDtypeStruct((DEC_BATCH, DKV, PAST), BF16),
        ],
        name="cache_prep",
    )(cache_k, cache_v)

    o_ctx = pl.pallas_call(
        _attn_ctx_kernel,
        grid=(BATCH,),
        in_specs=[
            pl.BlockSpec((1, DQ, QB), lambda b: (b, 0, 0)),
            pl.BlockSpec((SEQ, 2 * DKV), lambda b: (b, 0)),
            pl.BlockSpec((DKV, SEQ), lambda b: (0, b)),
        ],
        out_specs=pl.BlockSpec((SEQ, DQ), lambda b: (b, 0)),
        out_shape=jax.ShapeDtypeStruct((T, DQ), BF16),
        name="attn_ctx",
    )(qt, k2, vt)

    qb = QB
    nqb = DEC_SEQ // qb
    lat0 = T_CTX // qb
    depth = w_gate.shape[0]
    eps = (depth * NE) // (DEC_BATCH * nqb)
    per_layer = NE // eps
    wblock = lambda b, j: ((b * nqb + j) // per_layer, (b * nqb + j) % per_layer, 0, 0)
    return pl.pallas_call(
        _attn_lat_kernel,
        grid=(DEC_BATCH, nqb),
        in_specs=[
            pl.BlockSpec((1, DQ, QB), lambda b, j: (lat0 + b * nqb + j, 0, 0)),
            pl.BlockSpec((1, PAST, 2 * DKV), lambda b, j: (b, 0, 0)),
            pl.BlockSpec((1, DKV, PAST), lambda b, j: (b, 0, 0)),
            pl.BlockSpec((DEC_SEQ, 2 * DKV), lambda b, j: (T_CTX // DEC_SEQ + b, 0)),
            pl.BlockSpec((DKV, DEC_SEQ), lambda b, j: (0, T_CTX // DEC_SEQ + b)),
            pl.BlockSpec((1, eps, D, DE), wblock),
            pl.BlockSpec((1, eps, D, DE), wblock),
            pl.BlockSpec((1, eps, DE, D), wblock),
            pl.BlockSpec(memory_space=pl.ANY),
        ],
        out_specs=[
            pl.BlockSpec((qb, DQ), lambda b, j: (lat0 + b * nqb + j, 0)),
            pl.BlockSpec((1, eps, D, DE), wblock),
            pl.BlockSpec((1, eps, D, DE), wblock),
            pl.BlockSpec((1, eps, DE, D), wblock),
        ],
        out_shape=[
            jax.ShapeDtypeStruct((T, DQ), BF16),
            jax.ShapeDtypeStruct(w_gate.shape, BF16),
            jax.ShapeDtypeStruct(w_up.shape, BF16),
            jax.ShapeDtypeStruct(w_down.shape, BF16),
        ],
        input_output_aliases={8: 0},
        compiler_params=_vmem_limit(56),
        name="attn_lat",
    )(qt, k2c, vtc, k2, vt, w_gate, w_up, w_down, o_ctx)


def _route(h, wr_hl_ref, wr_hi_ref, tri_ref, cnt_ref, e_ref, rank_ref, wcol_ref, cnt_out_ref):
    tl = h.shape[0]
    h_hi = h.astype(BF16)
    h_lo = (h - h_hi.astype(F32)).astype(BF16)
    lg2 = jnp.dot(h_hi, wr_hl_ref[...], preferred_element_type=F32)
    lg = lg2[:, :LANES] + lg2[:, LANES:] + jnp.dot(h_lo, wr_hi_ref[...], preferred_element_type=F32)
    lt = lg.T

    sub8 = lax.broadcasted_iota(I32, (8, tl), 0).astype(F32)
    gl = jnp.where(sub8 < N_GROUPS, lt[0:8], NEG_INF)
    gmax = gl.max(axis=0, keepdims=True)
    gidx = jnp.min(jnp.where(gl == gmax, sub8, 8.0), axis=0, keepdims=True)
    gsum = jnp.sum(jnp.exp(gl - gmax), axis=0, keepdims=True)
    sel = lt[8:16]
    for g in range(1, N_GROUPS):
        sel = jnp.where(gidx == g, lt[8 + 8 * g:16 + 8 * g], sel)
    t1 = sel.max(axis=0, keepdims=True)
    j1 = jnp.min(jnp.where(sel == t1, sub8, 8.0), axis=0, keepdims=True)
    sel2 = jnp.where(sub8 == j1, NEG_INF, sel)
    t2 = sel2.max(axis=0, keepdims=True)
    j2 = jnp.min(jnp.where(sel2 == t2, sub8, 8.0), axis=0, keepdims=True)
    ex = jnp.exp(t2 - t1)
    den = 1.0 + ex
    gw = 1.0 / gsum
    w0 = gw * (1.0 / den)
    w1 = gw * (ex / den)
    e0 = gidx * E_PER_G + j1
    e1 = gidx * E_PER_G + j2

    sub_e = lax.broadcasted_iota(I32, (NE, tl), 0).astype(F32)
    oh0 = (sub_e == e0).astype(F32)
    oh1 = (sub_e == e1).astype(F32)
    c = oh0 + oh1
    carry = cnt_ref[:, 0:1]
    pieces = []
    for ch in range(tl // 256):
        cc = c[:, ch * 256:(ch + 1) * 256]
        pieces.append(jnp.dot(cc.astype(BF16), tri_ref[...], preferred_element_type=F32) + carry)
        carry = carry + jnp.sum(cc, axis=1, keepdims=True)
    csum = jnp.concatenate(pieces, axis=1)
    cnt_ref[...] = jnp.broadcast_to(carry, (NE, LANES))
    cnt_out_ref[...] = jnp.broadcast_to(carry, (NE, LANES))
    r0 = jnp.sum(oh0 * csum, axis=0, keepdims=True) - 1.0
    r1 = jnp.sum(oh1 * csum, axis=0, keepdims=True) - 1.0
    e_ref[0:1, :] = e0.astype(I32)
    e_ref[1:2, :] = e1.astype(I32)
    rank_ref[0:1, :] = r0.astype(I32)
    rank_ref[1:2, :] = r1.astype(I32)
    wcol_ref[0:1, :] = w0
    wcol_ref[1:2, :] = w1


def _post_kernel(o_ref, xp_ref, xs_ref, mod_ref, wo_ref, g_ref, wr_hl_ref, wr_hi_ref, tri_ref,
                 x1_ref, h_ref, e_ref, rank_ref, wcol_ref, cnt_out_ref, cnt_ref):
    i = pl.program_id(0)

    @pl.when(i == 0)
    def _():
        cnt_ref[...] = jnp.zeros_like(cnt_ref)

    x = jnp.where(i >= NT_CTX, xs_ref[...], xp_ref[...])
    m = jnp.dot(o_ref[...], wo_ref[...], preferred_element_type=F32)
    x1 = x + mod_ref[0, 2:3, :] * m
    x1_ref[...] = x1
    h = _modulate(x1, g_ref[...] * (1.0 + mod_ref[0, 4:5, :]), mod_ref[0, 3:4, :])
    h_ref[...] = _pack_rows(h)
    _route(h, wr_hl_ref, wr_hi_ref, tri_ref, cnt_ref, e_ref, rank_ref, wcol_ref, cnt_out_ref)


def _router_weights(w_group, w_expert):
    w = jnp.concatenate([w_group, jnp.zeros((D, 8 - N_GROUPS), F32),
                         jnp.transpose(w_expert, (1, 0, 2)).reshape(D, NE),
                         jnp.zeros((D, LANES - 8 - NE), F32)], axis=1)
    w_hi = w.astype(BF16)
    w_lo = (w - w_hi.astype(F32)).astype(BF16)
    return jnp.concatenate([w_hi, w_lo], axis=1), w_hi


def _route_outs(tile, first=0):
    specs = [
        pl.BlockSpec((tile, D), lambda i: (first + i, 0)),
        pl.BlockSpec((tile, HALF), lambda i: (first + i, 0)),
        pl.BlockSpec((TOP_K, tile), lambda i: (0, first + i)),
        pl.BlockSpec((TOP_K, tile), lambda i: (0, first + i)),
        pl.BlockSpec((TOP_K, tile), lambda i: (0, first + i)),
        pl.BlockSpec((NE, LANES), lambda i: (0, 0)),
    ]
    shapes = [
        jax.ShapeDtypeStruct((T, D), F32),
        jax.ShapeDtypeStruct((T, HALF), U32),
        jax.ShapeDtypeStruct((TOP_K, T), I32),
        jax.ShapeDtypeStruct((TOP_K, T), I32),
        jax.ShapeDtypeStruct((TOP_K, T), F32),
        jax.ShapeDtypeStruct((NE, LANES), F32),
    ]
    return specs, shapes


def _tri():
    a = np.arange(256)
    return jnp.asarray((a[:, None] <= a[None, :]).astype(np.float32), BF16)


def _post(o, xp, xs, mod, w_o, g_ffn, wr_hl, wr_hi):
    const = lambda shape: pl.BlockSpec(shape, lambda i: (0,) * len(shape))
    out_specs, out_shape = _route_outs(TILE)
    return pl.pallas_call(
        _post_kernel,
        grid=(NT,),
        in_specs=[
            pl.BlockSpec((TILE, DQ), lambda i: (i, 0)),
            pl.BlockSpec((TILE, D), lambda i: (jnp.minimum(i, NT_CTX - 1), 0)),
            pl.BlockSpec((TILE, D), lambda i: (jnp.maximum(i - NT_CTX, 0), 0)),
            pl.BlockSpec((1, N_MOD, D), lambda i: (_cond_index(i, TILE), 0, 0)),
            const((DQ, D)), const((1, D)), const((D, 2 * LANES)), const((D, LANES)), const((256, 256)),
        ],
        out_specs=out_specs,
        out_shape=out_shape,
        scratch_shapes=[pltpu.VMEM((NE, LANES), F32)],
        compiler_params=_vmem_limit(56),
        name="post_attn",
    )(o, xp, xs, mod, w_o, g_ffn, wr_hl, wr_hi, _tri())


def _pool_kernel(block_kinds, n_alias, y0_ref, y1_ref, xin_ref, wcin_ref, modp_ref, mod_ref, g_ref, pw_ref,
                 ps_ref, gf_ref, wr_hl_ref, wr_hi_ref, tri_ref, band_ref, inv_cnt_ref, cnt_in_ref, *refs):
    (x1_ref, h_ref, e_ref, rank_ref, wcol_ref, cnt_out_ref, cnt_ref, hhi_ref, hlo_ref) = refs[n_alias:]
    i = pl.program_id(0)

    @pl.when(i == 0)
    def _():
        cnt_ref[...] = cnt_in_ref[...]

    x = _combined(y0_ref, y1_ref, xin_ref, wcin_ref, modp_ref)
    h = _modulate(x, g_ref[...] * (1.0 + mod_ref[0, 1:2, :]), mod_ref[0, 0:1, :])
    @pl.when(i == 0)
    def _():
        zeros = jnp.zeros((PB_HALO, D), BF16)
        for ref in (hhi_ref, hlo_ref):
            ref[0:PB_HALO, :] = zeros
            ref[PB_HALO + PTILE:PB_HALO + PTILE + PB_HALO, :] = zeros

    h_hi = h.astype(BF16)
    hhi_ref[PB_HALO:PB_HALO + PTILE, :] = h_hi
    hlo_ref[PB_HALO:PB_HALO + PTILE, :] = (h - h_hi.astype(F32)).astype(BF16)

    outs = []
    for gi in range(len(POOL_WINDOWS)):
        cols = slice(gi * PGD, (gi + 1) * PGD)
        sums = []
        for b in range(PTILE // PB):
            bm = band_ref[block_kinds[b], gi]
            sums.append(jnp.dot(bm, hhi_ref[b * PB:b * PB + PBW, cols], preferred_element_type=F32)
                        + jnp.dot(bm, hlo_ref[b * PB:b * PB + PBW, cols], preferred_element_type=F32))
        diff = jnp.concatenate(sums, axis=0) * inv_cnt_ref[:, gi:gi + 1] - h[:, cols]
        outs.append(jnp.dot(diff.astype(BF16), pw_ref[gi], preferred_element_type=F32))
    m = jnp.concatenate(outs, axis=1) * ps_ref[...]
    x1 = x + mod_ref[0, 2:3, :] * m
    x1_ref[...] = x1
    hf = _modulate(x1, gf_ref[...] * (1.0 + mod_ref[0, 4:5, :]), mod_ref[0, 3:4, :])
    h_ref[...] = _pack_rows(hf)
    _route(hf, wr_hl_ref, wr_hi_ref, tri_ref, cnt_ref, e_ref, rank_ref, wcol_ref, cnt_out_ref)


def _pool_constants(seq):
    r = np.arange(PB)[:, None]
    d = np.arange(PBW)[None, :] - PB_HALO - r
    found, kinds = {}, []
    for b in range(PTILE // PB):
        tm = (b * PB + r) % seq
        inseq = (tm + d >= 0) & (tm + d < seq)
        mats = np.stack([(inseq & (d >= -(w // 2)) & (d <= w - w // 2 - 1)).astype(np.float32)
                         for w in POOL_WINDOWS])
        kinds.append(found.setdefault(mats.tobytes(), (len(found), mats))[0])
    bands = np.stack([m for _, m in sorted(found.values(), key=lambda kv: kv[0])])
    tm = np.arange(PTILE) % seq
    inv_cnt = np.zeros((PTILE, LANES), np.float32)
    for g, w in enumerate(POOL_WINDOWS):
        left, right = w // 2, w - w // 2 - 1
        inv_cnt[:, g] = 1.0 / (np.minimum(tm + right + 1, seq) - np.maximum(tm - left, 0)).astype(np.float32)
    return jnp.asarray(bands, BF16), tuple(kinds), inv_cnt


def _pool(group, y2g, x, wcol, mod_prev, mod, g_mix, pool_w, pool_scale, g_ffn, wr_hl, wr_hi, cnt_in, prev):
    const = lambda shape: pl.BlockSpec(shape, lambda i: (0,) * len(shape))
    tiles = (T_CTX if group == 0 else T_LAT) // PTILE
    first = 0 if group == 0 else T_CTX // PTILE
    out_specs, out_shape = _route_outs(PTILE, first)
    aliases = () if prev is None else tuple(prev)
    bands, block_kinds, inv_cnt = _pool_constants(SEQ if group == 0 else DEC_SEQ)
    n_fixed = 16
    return pl.pallas_call(
        functools.partial(_pool_kernel, block_kinds, len(aliases)),
        grid=(tiles,),
        in_specs=[
            pl.BlockSpec((PTILE, HALF), lambda i: (i, 0)),
            pl.BlockSpec((PTILE, HALF), lambda i: (tiles + i, 0)),
            pl.BlockSpec((PTILE, D), lambda i: (first + i, 0)),
            pl.BlockSpec((TOP_K, PTILE), lambda i: (0, first + i)),
            pl.BlockSpec((1, N_MOD, D), lambda i: (_cond_index(first + i, PTILE), 0, 0)),
            pl.BlockSpec((1, N_MOD, D), lambda i: (_cond_index(first + i, PTILE), 0, 0)),
            const((1, D)), const((len(POOL_WINDOWS), PGD, PGD)), const((1, D)), const((1, D)),
            const((D, 2 * LANES)), const((D, LANES)), const((256, 256)), const(bands.shape),
            const((PTILE, LANES)), const((NE, LANES)),
        ] + [pl.BlockSpec(memory_space=pl.ANY)] * len(aliases),
        out_specs=out_specs,
        out_shape=out_shape,
        input_output_aliases={n_fixed + k: k for k in range(len(aliases))},
        scratch_shapes=[pltpu.VMEM((NE, LANES), F32), pltpu.VMEM((PTILE + 2 * PB_HALO, D), BF16),
                        pltpu.VMEM((PTILE + 2 * PB_HALO, D), BF16)],
        compiler_params=_vmem_limit(56),
        name="pool_mixer",
    )(y2g, y2g, x, wcol, mod_prev, mod, g_mix, pool_w, pool_scale, g_ffn, wr_hl, wr_hi, _tri(), bands, inv_cnt,
      cnt_in, *aliases)


def _plan_kernel(e_ref, rank_ref, cnt_ref, pos_ref, te_ref, used_ref):
    counts = cnt_ref[:, 0:1]
    tiles_col = jnp.floor((counts + (TM - 1)) * (1.0 / TM))
    sub = lax.broadcasted_iota(I32, (NE, NE), 0)
    lane = lax.broadcasted_iota(I32, (NE, NE), 1)
    tiles_row = jnp.sum(jnp.where(sub == lane, tiles_col, 0.0), axis=0, keepdims=True)
    first_col = jnp.sum(jnp.where(lane < sub, tiles_row, 0.0), axis=1, keepdims=True)
    end_col = first_col + tiles_col
    used = jnp.sum(tiles_col, axis=0, keepdims=True)
    base_col = first_col * TM
    sub_e = lax.broadcasted_iota(I32, (NE, T), 0)
    for k in range(TOP_K):
        hit = sub_e == e_ref[k:k + 1, :]
        base = jnp.sum(jnp.where(hit, base_col, 0.0), axis=0, keepdims=True)
        pos_ref[k:k + 1, :] = base.astype(I32) + rank_ref[k:k + 1, :]
    tile = jnp.minimum(lax.broadcasted_iota(I32, (NE, LANES), 1).astype(F32), used - 1.0)
    te_ref[...] = jnp.sum(jnp.where(tile >= end_col, 1.0, 0.0), axis=0, keepdims=True).astype(I32)
    used_ref[...] = jnp.broadcast_to(used, (1, LANES)).astype(I32)


def _plan(e, rank, cnt):
    assert N_ROW_TILES <= LANES
    pos, te, used = pl.pallas_call(
        _plan_kernel,
        out_shape=[jax.ShapeDtypeStruct((TOP_K, T), I32), jax.ShapeDtypeStruct((1, LANES), I32),
                   jax.ShapeDtypeStruct((1, LANES), I32)],
        name="moe_plan",
    )(e, rank, cnt)
    return pos, te[0, :N_ROW_TILES], used[0, :1]


def _sc_mesh():
    return plsc.VectorSubcoreMesh(core_axis_name="core", subcore_axis_name="subcore")


def _sc_worker():
    return lax.axis_index("core") * SC_SUBCORES + lax.axis_index("subcore")


def _sc_dispatch(h, pos):
    per = T // SC_WORKERS
    nwin = per // SC_WIN

    @functools.partial(
        pl.kernel, out_type=jax.ShapeDtypeStruct((NPAD, HALF), U32), mesh=_sc_mesh(),
        scratch_types=[pltpu.VMEM((2, SC_WIN, HALF), U32), pltpu.VMEM((2, SC_WIN), I32),
                       pltpu.VMEM((2, SC_WIN), I32), pltpu.SemaphoreType.DMA((2,)),
                       pltpu.SemaphoreType.DMA((2,))])
    def run(h_hbm, pos_hbm, xs_hbm, buf, idx0, idx1, sem_in, sem_out):
        first = _sc_worker() * per

        def loads(j, s):
            base = first + j * SC_WIN
            return (pltpu.make_async_copy(h_hbm.at[pl.ds(base, SC_WIN)], buf.at[s], sem_in.at[s]),
                    pltpu.make_async_copy(pos_hbm.at[pl.ds(base, SC_WIN)], idx0.at[s], sem_in.at[s]),
                    pltpu.make_async_copy(pos_hbm.at[pl.ds(T + base, SC_WIN)], idx1.at[s], sem_in.at[s]))

        def stores(s):
            return (pltpu.make_async_copy(buf.at[s], xs_hbm.at[idx0.at[s]], sem_out.at[s]),
                    pltpu.make_async_copy(buf.at[s], xs_hbm.at[idx1.at[s]], sem_out.at[s]))

        for c in loads(0, 0):
            c.start()

        @pl.loop(0, nwin // 2)
        def _(jj):
            for s in range(2):
                j = jj * 2 + s
                for c in loads(j, s):
                    c.wait()
                for c in stores(s):
                    c.start()

                @pl.when(j >= 1)
                def _():
                    for c in stores(1 - s):
                        c.wait()

                @pl.when(j + 1 < nwin)
                def _():
                    for c in loads(j + 1, 1 - s):
                        c.start()

        for c in stores((nwin - 1) % 2):
            c.wait()

    return run(h, pos.reshape(TOP_K * T))


def _sc_gather(ys, pos, group):
    t0, n_g = (0, T_CTX) if group == 0 else (T_CTX, T_LAT)
    idx_all = jnp.concatenate([pos[k, t0:t0 + n_g] for k in range(TOP_K)])
    n = TOP_K * n_g
    per = n // SC_WORKERS
    nwin = per // SC_WIN

    @functools.partial(
        pl.kernel, out_type=jax.ShapeDtypeStruct((n, HALF), U32), mesh=_sc_mesh(),
        scratch_types=[pltpu.VMEM((2, SC_WIN, HALF), U32), pltpu.VMEM((per,), I32),
                       pltpu.SemaphoreType.DMA((2,)), pltpu.SemaphoreType.DMA((2,))])
    def run(ys_hbm, pos_hbm, out_hbm, buf, idx, sem_in, sem_out):
        first = _sc_worker() * per
        pltpu.sync_copy(pos_hbm.at[pl.ds(first, per)], idx)

        def load(j, s):
            return pltpu.make_async_copy(ys_hbm.at[idx.at[pl.ds(j * SC_WIN, SC_WIN)]], buf.at[s], sem_in.at[s])

        def store(j, s):
            return pltpu.make_async_copy(buf.at[s], out_hbm.at[pl.ds(first + j * SC_WIN, SC_WIN)], sem_out.at[s])

        load(0, 0).start()

        @pl.loop(0, nwin // 2)
        def _(jj):
            for s in range(2):
                j = jj * 2 + s
                load(j, s).wait()
                store(j, s).start()

                @pl.when(j >= 1)
                def _():
                    store(j - 1, 1 - s).wait()

                @pl.when(j + 1 < nwin)
                def _():
                    load(j + 1, 1 - s).start()

        store(nwin - 1, (nwin - 1) % 2).wait()

    return run(ys, idx_all)


def _moe_kernel(te_ref, used_ref, x_hbm, wg_ref, wu_ref, wd_ref, y_ref, xbuf, sem):
    del te_ref
    i = pl.program_id(0)
    used = used_ref[0]

    def tile_copy(t):
        slot = lax.rem(t, X_SLOTS)
        return pltpu.make_async_copy(x_hbm.at[pl.ds(t * TM, TM)], xbuf.at[slot], sem.at[slot])

    @pl.when(i == 0)
    def _():
        for t in range(X_SLOTS - 1):
            @pl.when(t < used)
            def _():
                tile_copy(t).start()

    @pl.when(i + (X_SLOTS - 1) < used)
    def _():
        tile_copy(i + (X_SLOTS - 1)).start()

    @pl.when(i < used)
    def _():
        tile_copy(i).wait()
        hi, lo = _unpack_rows(xbuf[lax.rem(i, X_SLOTS)])
        x = jnp.concatenate([hi, lo], axis=1).astype(BF16)
        a = jnp.dot(x, wg_ref[0, 0], preferred_element_type=F32)
        b = jnp.dot(x, wu_ref[0, 0], preferred_element_type=F32)
        hid = a * (1.0 / (1.0 + jnp.exp(-a))) * b
        y_ref[...] = _pack_rows(jnp.dot(hid.astype(BF16), wd_ref[0, 0], preferred_element_type=F32))


def _experts(xs, tile_expert, used, layer, w_gate, w_up, w_down):
    row = lambda i, te, used: (jnp.minimum(i, used[0] - 1), 0)
    weights = lambda i, te, used: (layer, te[i], 0, 0)
    grid_spec = pltpu.PrefetchScalarGridSpec(
        num_scalar_prefetch=2,
        grid=(N_ROW_TILES,),
        in_specs=[
            pl.BlockSpec(memory_space=pl.ANY),
            pl.BlockSpec((1, 1, D, DE), weights),
            pl.BlockSpec((1, 1, D, DE), weights),
            pl.BlockSpec((1, 1, DE, D), weights),
        ],
        out_specs=pl.BlockSpec((TM, HALF), row),
        scratch_shapes=[pltpu.VMEM((X_SLOTS, TM, HALF), U32), pltpu.SemaphoreType.DMA((X_SLOTS,))],
    )
    return pl.pallas_call(
        _moe_kernel,
        grid_spec=grid_spec,
        out_shape=jax.ShapeDtypeStruct((NPAD, HALF), U32),
        compiler_params=pltpu.CompilerParams(dimension_semantics=("arbitrary",),
                                             vmem_limit_bytes=40 * 1024 * 1024),
        name="moe_experts",
    )(tile_expert, used, xs, w_gate, w_up, w_down)


def _combined(y0_ref, y1_ref, x_ref, wcol_ref, mod_ref):
    sub = lax.broadcasted_iota(I32, (LANES, wcol_ref.shape[1]), 0)
    wt = jnp.where(sub == 0, wcol_ref[0:1, :], jnp.where(sub == 1, wcol_ref[1:2, :], 0.0)).T
    w0 = wt[:, 0:1]
    w1 = wt[:, 1:2]
    hi0, lo0 = _unpack_rows(y0_ref[...])
    hi1, lo1 = _unpack_rows(y1_ref[...])
    y = jnp.concatenate([w0 * hi0 + w1 * hi1, w0 * lo0 + w1 * lo1], axis=1)
    return x_ref[...] + mod_ref[0, 5:6, :] * y


def _combine_kernel(y0_ref, y1_ref, x_ref, wcol_ref, mod_ref, o_ref):
    o_ref[...] = _combined(y0_ref, y1_ref, x_ref, wcol_ref, mod_ref)


def _combine(group, y2g, x, wcol, mod):
    n_g = T_CTX if group == 0 else T_LAT
    tiles = n_g // CT
    first = 0 if group == 0 else T_CTX // CT
    return pl.pallas_call(
        _combine_kernel,
        grid=(tiles,),
        in_specs=[
            pl.BlockSpec((CT, HALF), lambda i: (i, 0)),
            pl.BlockSpec((CT, HALF), lambda i: (tiles + i, 0)),
            pl.BlockSpec((CT, D), lambda i: (first + i, 0)),
            pl.BlockSpec((TOP_K, CT), lambda i: (0, first + i)),
            pl.BlockSpec((1, N_MOD, D), lambda i: (_cond_index(first + i, CT), 0, 0)),
        ],
        out_specs=pl.BlockSpec((CT, D), lambda i: (i, 0)),
        out_shape=jax.ShapeDtypeStruct((n_g, D), F32),
        name="moe_combine",
    )(y2g, y2g, x, wcol, mod)


def _moe(h, e, rank, cnt, layer, w_gate, w_up, w_down):
    pos, tile_expert, used = _plan(e, rank, cnt)
    xs = _sc_dispatch(h, pos)
    ys = _experts(xs, tile_expert, used, layer, w_gate, w_up, w_down)
    return _sc_gather(ys, pos, 0), _sc_gather(ys, pos, 1)


def kernel(x_prompt, x_sample, c, cache_k, cache_v, c_ctx, norm_mix_g, norm_ffn_g, ada_w, ada_b, attn_w_qkv, attn_q_norm, attn_k_norm, attn_w_o, pool_w, pool_scale, moe_w_group, moe_w_expert, moe_w_gate, moe_w_up, moe_w_down):
    xp = x_prompt.reshape(T_CTX, D)
    xs = x_sample.reshape(T_LAT, D)
    cond = jnp.concatenate([c_ctx[None, :], c, jnp.zeros((N_COND - 1 - DEC_BATCH, D), F32)], axis=0)
    mod = [_ada(cond, ada_w, ada_b, layer) for layer in range(ada_w.shape[0])]

    qt, k2, vt, new_k, new_v = _qkv(xp, xs, mod[0], norm_mix_g[0:1], attn_w_qkv[0].astype(BF16),
                                    attn_q_norm[0:1], attn_k_norm[0:1])
    o, w_gate, w_up, w_down = _attention(qt, k2, vt, cache_k[:, 0].reshape(DEC_BATCH, PAST, DKV),
                                         cache_v[:, 0].reshape(DEC_BATCH, PAST, DKV),
                                         moe_w_gate, moe_w_up, moe_w_down)
    wr_hl, wr_hi = _router_weights(moe_w_group[0], moe_w_expert[0])
    x1, h, e, rank, wcol, cnt = _post(o, xp, xs, mod[0], attn_w_o[0].astype(BF16), norm_ffn_g[0:1],
                                      wr_hl, wr_hi)
    y2 = _moe(h, e, rank, cnt, 0, w_gate, w_up, w_down)

    wr_hl, wr_hi = _router_weights(moe_w_group[1], moe_w_expert[1])
    pool_args = (x1, wcol, mod[0], mod[1], norm_mix_g[1:2], pool_w[0].astype(BF16), pool_scale[0:1],
                 norm_ffn_g[1:2], wr_hl, wr_hi)
    outs = _pool(0, y2[0], *pool_args, jnp.zeros((NE, LANES), F32), None)
    x3, h, e, rank, wcol, cnt = _pool(1, y2[1], *pool_args, outs[5], outs[:5])
    y2 = _moe(h, e, rank, cnt, 1, w_gate, w_up, w_down)
    y_prompt = _combine(0, y2[0], x3, wcol, mod[1])
    y_sample = _combine(1, y2[1], x3, wcol, mod[1])

    return (y_prompt.reshape(BATCH, SEQ, D), y_sample.reshape(DEC_BATCH, DEC_SEQ, D),
            new_k.reshape(BATCH, 1, SEQ, N_KV, HD), new_v.reshape(BATCH, 1, SEQ, N_KV, HD))
```

```python
import functools

import jax
import jax.numpy as jnp
import numpy as np
from jax import lax
from jax.experimental import pallas as pl
from jax.experimental.pallas import tpu as pltpu
from jax.experimental.pallas import tpu_sc as plsc

F32 = jnp.float32
BF16 = jnp.bfloat16
I32 = jnp.int32
U32 = jnp.uint32

D = 1024
BATCH, SEQ = 32, 256
DEC_BATCH, DEC_SEQ, PAST = 8, 1024, 512
T_CTX = BATCH * SEQ
T_LAT = DEC_BATCH * DEC_SEQ
T = T_CTX + T_LAT
GRID_W = 64
N_HEADS, N_KV, HD = 16, 4, 64
DQ = N_HEADS * HD
DKV = N_KV * HD
ROPE_THETA = 10000.0
POOL_WINDOWS = (2, 4, 8, 16)
PGD = D // len(POOL_WINDOWS)
N_GROUPS, E_PER_G, TOP_K = 4, 8, 2
NE = N_GROUPS * E_PER_G
DE = D // 4
N_MOD = 6
EPS = 1e-6
N_COND = 16

TILE = 1024
NT = T // TILE
NT_CTX = T_CTX // TILE
QB = 256
CTX_PER_STEP = 4
LAT_PER_STEP = 2
KEY_PART = 256
PTILE = 1024
PB = 128
PBW = 256
PB_HALO = (PBW - PB) // 2
TM = 1024
X_SLOTS = 3
N_ROW_TILES = (TOP_K * T) // TM + NE
NPAD = N_ROW_TILES * TM
CT = 1024
LANES = 128
NEG_INF = float("-inf")
LOG2_E = 1.4426950408889634
Q_SCALE = HD ** -0.5 * LOG2_E
HALF = D // 2
SC_CORES, SC_SUBCORES = 2, 16
SC_WORKERS = SC_CORES * SC_SUBCORES
SC_WIN = 64


def _vmem_limit(mib):
    return pltpu.CompilerParams(vmem_limit_bytes=mib * 1024 * 1024)


def _modulate(x, gain_scale, shift):
    ms = jnp.mean(x * x, axis=-1, keepdims=True)
    return (x * lax.rsqrt(ms + EPS)) * gain_scale + shift


def _head_norm(z, gain, ind, ind_t2):
    ss = jnp.dot((z * z).astype(BF16), ind, preferred_element_type=F32)
    inv = lax.rsqrt(ss * (1.0 / HD) + EPS)
    inv_hi = inv.astype(BF16)
    inv_lo = (inv - inv_hi.astype(F32)).astype(BF16)
    scale = jnp.dot(jnp.concatenate([inv_hi, inv_lo], axis=1), ind_t2, preferred_element_type=F32)
    return z * scale * gain


def _rope(z, cos_t, sin_t):
    lane = lax.broadcasted_iota(I32, (z.shape[0], LANES), 1)
    low = (lane % 32) < 16
    outs = []
    for c in range(z.shape[1] // LANES):
        zc = z[:, c * LANES:(c + 1) * LANES]
        up = pltpu.roll(zc, 16, axis=1)
        dn = pltpu.roll(zc, LANES - 16, axis=1)
        outs.append(zc * cos_t + jnp.where(low, dn, up) * sin_t)
    return jnp.concatenate(outs, axis=1)


def _rope_t(zt, cos_tt, sin_tt):
    sub = lax.broadcasted_iota(I32, (LANES, zt.shape[1]), 0)
    low = (sub % 32) < 16
    outs = []
    for c in range(zt.shape[0] // LANES):
        zc = zt[c * LANES:(c + 1) * LANES]
        up = pltpu.roll(zc, 16, axis=0)
        dn = pltpu.roll(zc, LANES - 16, axis=0)
        outs.append(zc * cos_tt + jnp.where(low, dn, up) * sin_tt)
    return jnp.concatenate(outs, axis=0)


def _pack_rows(z):
    bits = lax.bitcast_convert_type(z.astype(BF16).astype(F32), U32)
    return bits[:, :HALF] | (bits[:, HALF:] >> 16)


def _unpack_rows(p):
    hi = lax.bitcast_convert_type(p & jnp.uint32(0xFFFF0000), F32)
    lo = lax.bitcast_convert_type(p << 16, F32)
    return hi, lo


def _ada_kernel(c_ref, w_ref, b_ref, o_ref):
    c = c_ref[...]
    a = c * (1.0 / (1.0 + jnp.exp(-c)))
    o_ref[0] = jnp.dot(a.astype(BF16), w_ref[0].astype(BF16), preferred_element_type=F32) + b_ref[0]


def _ada(cond, ada_w, ada_b, layer):
    nb = 1536
    depth = ada_w.shape[0]
    out = pl.pallas_call(
        _ada_kernel,
        grid=((N_MOD * D) // nb,),
        in_specs=[
            pl.BlockSpec((N_COND, D), lambda j: (0, 0)),
            pl.BlockSpec((1, D, nb), lambda j: (layer, 0, j)),
            pl.BlockSpec((1, 1, nb), lambda j: (layer, 0, j)),
        ],
        out_specs=pl.BlockSpec((1, N_COND, nb), lambda j: (0, 0, j)),
        out_shape=jax.ShapeDtypeStruct((1, N_COND, N_MOD * D), F32),
        compiler_params=_vmem_limit(40),
        name="ada",
    )(cond, ada_w, ada_b.reshape(depth, 1, N_MOD * D))
    return out.reshape(N_COND, N_MOD, D)


def _cond_index(i, tile):
    n_ctx = T_CTX // tile
    per_batch = DEC_SEQ // tile
    return jnp.where(i < n_ctx, 0, 1 + (i - n_ctx) // per_batch)


def _qkv_kernel(xp_ref, xs_ref, mod_ref, g_ref, w_ref, qg_ref, kg_ref, indq_ref, indqt_ref,
                indk_ref, indkt_ref, cos_ref, sin_ref, cos_t_ref, sin_t_ref,
                qt_ref, k2_ref, vt_ref, nk_ref, nv_ref):
    i = pl.program_id(0)
    is_lat = i >= NT_CTX
    gain_scale = g_ref[...] * (1.0 + mod_ref[0, 1:2, :])
    subs = [slice(b * QB, (b + 1) * QB) for b in range(TILE // QB)]
    hs = [_modulate(jnp.where(is_lat, xs_ref[r, :], xp_ref[r, :]), gain_scale, mod_ref[0, 0:1, :]).astype(BF16)
          for r in subs]
    qkvs = [jnp.dot(h, w_ref[...], preferred_element_type=F32) for h in hs]
    qs = [_head_norm(z[:, :DQ], qg_ref[...], indq_ref[...], indqt_ref[...]) for z in qkvs]
    ks = [_head_norm(z[:, DQ:DQ + DKV], kg_ref[...], indk_ref[...], indkt_ref[...]) for z in qkvs]
    vs = [z[:, DQ + DKV:] for z in qkvs]
    qts = [(q * Q_SCALE).T for q in qs]
    for r, v in zip(subs, vs):
        vt_ref[:, r] = v.T.astype(BF16)

    @pl.when(is_lat)
    def _():
        for b, r in enumerate(subs):
            qt_ref[b] = _rope_t(qts[b], cos_t_ref[:, r], sin_t_ref[:, r]).astype(BF16)
            k2_ref[r, :] = _dup_heads(_rope(ks[b], cos_ref[r, :], sin_ref[r, :])).astype(BF16)

    @pl.when(jnp.logical_not(is_lat))
    def _():
        for b, r in enumerate(subs):
            qt_ref[b] = qts[b].astype(BF16)
            k2_ref[r, :] = _dup_heads(ks[b]).astype(BF16)
            nk_ref[r, :] = ks[b]
            nv_ref[r, :] = vs[b]


def _rope_tables():
    rows = DEC_SEQ // GRID_W
    row = np.broadcast_to(np.arange(rows, dtype=np.float32)[:, None], (rows, GRID_W)).reshape(-1)
    col = np.broadcast_to(np.arange(GRID_W, dtype=np.float32)[None, :], (rows, GRID_W)).reshape(-1)
    axis_dim = HD // 2
    inv_freq = np.power(np.float32(ROPE_THETA),
                        -np.arange(0, axis_dim, 2, dtype=np.float32) / np.float32(axis_dim))
    ang = np.concatenate([row[:, None] * inv_freq, col[:, None] * inv_freq], axis=-1).astype(np.float32)
    cos, sin = np.cos(ang), np.sin(ang)
    quarter = HD // 4
    cos_h = np.concatenate([cos[:, :quarter], cos[:, :quarter], cos[:, quarter:], cos[:, quarter:]], axis=1)
    sin_h = np.concatenate([-sin[:, :quarter], sin[:, :quarter], -sin[:, quarter:], sin[:, quarter:]], axis=1)
    return (np.tile(cos_h, (1, LANES // HD)).astype(np.float32),
            np.tile(sin_h, (1, LANES // HD)).astype(np.float32))


def _head_indicators(width):
    col = np.arange(width)[:, None]
    head = np.arange(LANES)[None, :]
    ind = (col // HD == head).astype(np.float32)
    return jnp.asarray(ind, BF16), jnp.asarray(np.concatenate([ind.T, ind.T], axis=0), BF16)


def _dup_heads(z):
    lane = lax.broadcasted_iota(I32, (z.shape[0], LANES), 1)
    cols = []
    for p in range(N_KV // 2):
        blk = z[:, p * LANES:(p + 1) * LANES]
        swp = pltpu.roll(blk, HD, axis=1)
        cols += [jnp.where(lane < HD, blk, swp), jnp.where(lane < HD, swp, blk)]
    return jnp.concatenate(cols, axis=1)


def _qkv(xp, xs, mod, g, w_qkv, q_gain, k_gain):
    indq, indqt = _head_indicators(DQ)
    indk, indkt = _head_indicators(DKV)
    cos_t, sin_t = _rope_tables()
    per_batch = DEC_SEQ // TILE
    const = lambda shape: pl.BlockSpec(shape, lambda i: (0,) * len(shape))
    return pl.pallas_call(
        _qkv_kernel,
        grid=(NT,),
        in_specs=[
            pl.BlockSpec((TILE, D), lambda i: (jnp.minimum(i, NT_CTX - 1), 0)),
            pl.BlockSpec((TILE, D), lambda i: (jnp.maximum(i - NT_CTX, 0), 0)),
            pl.BlockSpec((1, N_MOD, D), lambda i: (_cond_index(i, TILE), 0, 0)),
            const((1, D)),
            const((D, DQ + 2 * DKV)),
            const((1, DQ)), const((1, DKV)),
            const((DQ, LANES)), const((2 * LANES, DQ)),
            const((DKV, LANES)), const((2 * LANES, DKV)),
            pl.BlockSpec((TILE, LANES), lambda i: (jnp.maximum(i - NT_CTX, 0) % per_batch, 0)),
            pl.BlockSpec((TILE, LANES), lambda i: (jnp.maximum(i - NT_CTX, 0) % per_batch, 0)),
            pl.BlockSpec((LANES, TILE), lambda i: (0, jnp.maximum(i - NT_CTX, 0) % per_batch)),
            pl.BlockSpec((LANES, TILE), lambda i: (0, jnp.maximum(i - NT_CTX, 0) % per_batch)),
        ],
        out_specs=[
            pl.BlockSpec((TILE // QB, DQ, QB), lambda i: (i, 0, 0)),
            pl.BlockSpec((TILE, 2 * DKV), lambda i: (i, 0)),
            pl.BlockSpec((DKV, TILE), lambda i: (0, i)),
            pl.BlockSpec((TILE, DKV), lambda i: (jnp.minimum(i, NT_CTX - 1), 0)),
            pl.BlockSpec((TILE, DKV), lambda i: (jnp.minimum(i, NT_CTX - 1), 0)),
        ],
        out_shape=[
            jax.ShapeDtypeStruct((T // QB, DQ, QB), BF16),
            jax.ShapeDtypeStruct((T, 2 * DKV), BF16),
            jax.ShapeDtypeStruct((DKV, T), BF16),
            jax.ShapeDtypeStruct((T_CTX, DKV), F32),
            jax.ShapeDtypeStruct((T_CTX, DKV), F32),
        ],
        compiler_params=_vmem_limit(56),
        name="qkv",
    )(xp, xs, mod, g, w_qkv, jnp.tile(q_gain, (1, N_HEADS)), jnp.tile(k_gain, (1, N_KV)),
      indq, indqt, indk, indkt, cos_t, sin_t, cos_t.T, sin_t.T)


def _unit_scores(u, pairs, k2_parts, qt_ref):
    cols = []
    for j in range(pairs):
        pair = u * pairs + j
        qt_pair = qt_ref[pair * LANES:(pair + 1) * LANES, :]
        sub = lax.broadcasted_iota(I32, qt_pair.shape, 0)
        zero = jnp.zeros_like(qt_pair)
        cols += [jnp.where(sub < HD, qt_pair, zero), jnp.where(sub >= HD, qt_pair, zero)]
    qtu = jnp.concatenate(cols, axis=1)
    return [jnp.dot(k2, qtu, preferred_element_type=F32) for k2 in k2_parts]


def _softmax_values(ss, vt_parts):
    m = ss[0].max(axis=0, keepdims=True)
    for s in ss[1:]:
        m = jnp.maximum(m, s.max(axis=0, keepdims=True))
    acc = None
    l = None
    for s, vt in zip(ss, vt_parts):
        e = jnp.exp2(s - m)
        ls = e.sum(axis=0, keepdims=True)
        l = ls if l is None else l + ls
        t = jnp.dot(vt, e.astype(BF16), preferred_element_type=F32)
        acc = t if acc is None else acc + t
    return acc * (1.0 / l)


def _attend(jobs, pairs, ahead):
    n_units = N_HEADS // (2 * pairs)
    group = lambda u: (u * 2 * pairs) // (N_HEADS // N_KV)
    units = [(job, u) for job in jobs for u in range(n_units)]
    scores = lambda job, u: _unit_scores(u, pairs, job[0](group(u)), job[2])
    pending = [scores(*unit) for unit in units[:ahead]]
    outs = []
    for i, (job, u) in enumerate(units):
        if i + ahead < len(units):
            pending.append(scores(*units[i + ahead]))
        outs.append(_softmax_values(pending.pop(0), job[1](group(u))))
        if u == n_units - 1:
            o_ref = job[3]
            lq = o_ref.shape[0]
            heads = [o[:, h * lq:(h + 1) * lq] for o in outs for h in range(2 * pairs)]
            o_ref[...] = jnp.concatenate(heads, axis=0).T.astype(BF16)
            outs = []


def _attn_ctx_kernel(qt_ref, k2_ref, vt_ref, o_ref):
    def job(b):
        rows = slice(b * SEQ, (b + 1) * SEQ)
        return (lambda g: [k2_ref[rows, g * LANES:(g + 1) * LANES]],
                lambda g: [vt_ref[g * HD:(g + 1) * HD, rows]], qt_ref.at[b], o_ref.at[rows])
    _attend([job(b) for b in range(CTX_PER_STEP)], pairs=2, ahead=1)


def _attn_lat_kernel(qt_ref, k2c_ref, vtc_ref, k2n_ref, vtn_ref, wg_ref, wu_ref, wd_ref, o_in_ref,
                     o_ref, wgb_ref, wub_ref, wdb_ref):
    del o_in_ref
    wgb_ref[...] = wg_ref[...].astype(BF16)
    wub_ref[...] = wu_ref[...].astype(BF16)
    wdb_ref[...] = wd_ref[...].astype(BF16)
    pc = [slice(h * KEY_PART, (h + 1) * KEY_PART) for h in range(PAST // KEY_PART)]
    pn = [slice(h * KEY_PART, (h + 1) * KEY_PART) for h in range(DEC_SEQ // KEY_PART)]
    keys_of = lambda g: ([k2c_ref[0, r, g * LANES:(g + 1) * LANES] for r in pc]
                         + [k2n_ref[r, g * LANES:(g + 1) * LANES] for r in pn])
    values_of = lambda g: ([vtc_ref[0, g * HD:(g + 1) * HD, r] for r in pc]
                           + [vtn_ref[g * HD:(g + 1) * HD, r] for r in pn])
    _attend([(keys_of, values_of, qt_ref.at[q], o_ref.at[q * QB:(q + 1) * QB]) for q in range(LAT_PER_STEP)],
            pairs=1, ahead=2)


def _cache_kernel(ck_ref, cv_ref, k2_ref, vt_ref):
    k2_ref[0] = _dup_heads(ck_ref[0]).astype(BF16)
    vt_ref[0] = cv_ref[0].T.astype(BF16)


def _attention(qt, k2, vt, cache_k, cache_v, w_gate, w_up, w_down):
    k2c, vtc = pl.pallas_call(
        _cache_kernel,
        grid=(DEC_BATCH,),
        in_specs=[
            pl.BlockSpec((1, PAST, DKV), lambda b: (b, 0, 0)),
            pl.BlockSpec((1, PAST, DKV), lambda b: (b, 0, 0)),
        ],
        out_specs=[
            pl.BlockSpec((1, PAST, 2 * DKV), lambda b: (b, 0, 0)),
            pl.BlockSpec((1, DKV, PAST), lambda b: (b, 0, 0)),
        ],
        out_shape=[
            jax.ShapeDtypeStruct((DEC_BATCH, PAST, 2 * DKV), BF16),
            jax.ShapeDtypeStruct((DEC_BATCH, DKV, PAST), BF16),
        ],
        name="cache_prep",
    )(cache_k, cache_v)

    o_ctx = pl.pallas_call(
        _attn_ctx_kernel,
        grid=(BATCH // CTX_PER_STEP,),
        in_specs=[
            pl.BlockSpec((CTX_PER_STEP, DQ, QB), lambda b: (b, 0, 0)),
            pl.BlockSpec((CTX_PER_STEP * SEQ, 2 * DKV), lambda b: (b, 0)),
            pl.BlockSpec((DKV, CTX_PER_STEP * SEQ), lambda b: (0, b)),
        ],
        out_specs=pl.BlockSpec((CTX_PER_STEP * SEQ, DQ), lambda b: (b, 0)),
        out_shape=jax.ShapeDtypeStruct((T, DQ), BF16),
        name="attn_ctx",
    )(qt, k2, vt)

    qb = LAT_PER_STEP * QB
    nqb = DEC_SEQ // qb
    lat0 = T_CTX // qb
    depth = w_gate.shape[0]
    eps = (depth * NE) // (DEC_BATCH * nqb)
    per_layer = NE // eps
    wblock = lambda b, j: ((b * nqb + j) // per_layer, (b * nqb + j) % per_layer, 0, 0)
    return pl.pallas_call(
        _attn_lat_kernel,
        grid=(DEC_BATCH, nqb),
        in_specs=[
            pl.BlockSpec((LAT_PER_STEP, DQ, QB), lambda b, j: (lat0 + b * nqb + j, 0, 0)),
            pl.BlockSpec((1, PAST, 2 * DKV), lambda b, j: (b, 0, 0)),
            pl.BlockSpec((1, DKV, PAST), lambda b, j: (b, 0, 0)),
            pl.BlockSpec((DEC_SEQ, 2 * DKV), lambda b, j: (T_CTX // DEC_SEQ + b, 0)),
            pl.BlockSpec((DKV, DEC_SEQ), lambda b, j: (0, T_CTX // DEC_SEQ + b)),
            pl.BlockSpec((1, eps, D, DE), wblock),
            pl.BlockSpec((1, eps, D, DE), wblock),
            pl.BlockSpec((1, eps, DE, D), wblock),
            pl.BlockSpec(memory_space=pl.ANY),
        ],
        out_specs=[
            pl.BlockSpec((qb, DQ), lambda b, j: (lat0 + b * nqb + j, 0)),
            pl.BlockSpec((1, eps, D, DE), wblock),
            pl.BlockSpec((1, eps, D, DE), wblock),
            pl.BlockSpec((1, eps, DE, D), wblock),
        ],
        out_shape=[
            jax.ShapeDtypeStruct((T, DQ), BF16),
            jax.ShapeDtypeStruct(w_gate.shape, BF16),
            jax.ShapeDtypeStruct(w_up.shape, BF16),
            jax.ShapeDtypeStruct(w_down.shape, BF16),
        ],
        input_output_aliases={8: 0},
        compiler_params=_vmem_limit(56),
        name="attn_lat",
    )(qt, k2c, vtc, k2, vt, w_gate, w_up, w_down, o_ctx)


def _route(h, wr_hl_ref, wr_hi_ref, tri_ref, cnt_ref, e_ref, rank_ref, wcol_ref, cnt_out_ref):
    tl = h.shape[0]
    h_hi = h.astype(BF16)
    h_lo = (h - h_hi.astype(F32)).astype(BF16)
    lg2 = jnp.dot(h_hi, wr_hl_ref[...], preferred_element_type=F32)
    lg = lg2[:, :LANES] + lg2[:, LANES:] + jnp.dot(h_lo, wr_hi_ref[...], preferred_element_type=F32)
    lt = lg.T

    sub8 = lax.broadcasted_iota(I32, (8, tl), 0).astype(F32)
    gl = jnp.where(sub8 < N_GROUPS, lt[0:8], NEG_INF)
    gmax = gl.max(axis=0, keepdims=True)
    gidx = jnp.min(jnp.where(gl == gmax, sub8, 8.0), axis=0, keepdims=True)
    gsum = jnp.sum(jnp.exp(gl - gmax), axis=0, keepdims=True)
    sel = lt[8:16]
    for g in range(1, N_GROUPS):
        sel = jnp.where(gidx == g, lt[8 + 8 * g:16 + 8 * g], sel)
    t1 = sel.max(axis=0, keepdims=True)
    j1 = jnp.min(jnp.where(sel == t1, sub8, 8.0), axis=0, keepdims=True)
    sel2 = jnp.where(sub8 == j1, NEG_INF, sel)
    t2 = sel2.max(axis=0, keepdims=True)
    j2 = jnp.min(jnp.where(sel2 == t2, sub8, 8.0), axis=0, keepdims=True)
    ex = jnp.exp(t2 - t1)
    den = 1.0 + ex
    gw = 1.0 / gsum
    w0 = gw * (1.0 / den)
    w1 = gw * (ex / den)
    e0 = gidx * E_PER_G + j1
    e1 = gidx * E_PER_G + j2

    sub_e = lax.broadcasted_iota(I32, (NE, tl), 0).astype(F32)
    oh0 = (sub_e == e0).astype(F32)
    oh1 = (sub_e == e1).astype(F32)
    c = oh0 + oh1
    carry = cnt_ref[:, 0:1]
    pieces = []
    for ch in range(tl // 256):
        cc = c[:, ch * 256:(ch + 1) * 256]
        pieces.append(jnp.dot(cc.astype(BF16), tri_ref[...], preferred_element_type=F32) + carry)
        carry = carry + jnp.sum(cc, axis=1, keepdims=True)
    csum = jnp.concatenate(pieces, axis=1)
    cnt_ref[...] = jnp.broadcast_to(carry, (NE, LANES))
    cnt_out_ref[...] = jnp.broadcast_to(carry, (NE, LANES))
    r0 = jnp.sum(oh0 * csum, axis=0, keepdims=True) - 1.0
    r1 = jnp.sum(oh1 * csum, axis=0, keepdims=True) - 1.0
    e_ref[0:1, :] = e0.astype(I32)
    e_ref[1:2, :] = e1.astype(I32)
    rank_ref[0:1, :] = r0.astype(I32)
    rank_ref[1:2, :] = r1.astype(I32)
    wcol_ref[0:1, :] = w0
    wcol_ref[1:2, :] = w1


def _post_kernel(o_ref, xp_ref, xs_ref, mod_ref, wo_ref, g_ref, wr_hl_ref, wr_hi_ref, tri_ref,
                 x1_ref, h_ref, e_ref, rank_ref, wcol_ref, cnt_out_ref, cnt_ref):
    i = pl.program_id(0)

    @pl.when(i == 0)
    def _():
        cnt_ref[...] = jnp.zeros_like(cnt_ref)

    x = jnp.where(i >= NT_CTX, xs_ref[...], xp_ref[...])
    m = jnp.dot(o_ref[...], wo_ref[...], preferred_element_type=F32)
    x1 = x + mod_ref[0, 2:3, :] * m
    x1_ref[...] = x1
    h = _modulate(x1, g_ref[...] * (1.0 + mod_ref[0, 4:5, :]), mod_ref[0, 3:4, :])
    h_ref[...] = _pack_rows(h)
    _route(h, wr_hl_ref, wr_hi_ref, tri_ref, cnt_ref, e_ref, rank_ref, wcol_ref, cnt_out_ref)


def _router_weights(w_group, w_expert):
    w = jnp.concatenate([w_group, jnp.zeros((D, 8 - N_GROUPS), F32),
                         jnp.transpose(w_expert, (1, 0, 2)).reshape(D, NE),
                         jnp.zeros((D, LANES - 8 - NE), F32)], axis=1)
    w_hi = w.astype(BF16)
    w_lo = (w - w_hi.astype(F32)).astype(BF16)
    return jnp.concatenate([w_hi, w_lo], axis=1), w_hi


def _route_outs(tile, first=0):
    specs = [
        pl.BlockSpec((tile, D), lambda i: (first + i, 0)),
        pl.BlockSpec((tile, HALF), lambda i: (first + i, 0)),
        pl.BlockSpec((TOP_K, tile), lambda i: (0, first + i)),
        pl.BlockSpec((TOP_K, tile), lambda i: (0, first + i)),
        pl.BlockSpec((TOP_K, tile), lambda i: (0, first + i)),
        pl.BlockSpec((NE, LANES), lambda i: (0, 0)),
    ]
    shapes = [
        jax.ShapeDtypeStruct((T, D), F32),
        jax.ShapeDtypeStruct((T, HALF), U32),
        jax.ShapeDtypeStruct((TOP_K, T), I32),
        jax.ShapeDtypeStruct((TOP_K, T), I32),
        jax.ShapeDtypeStruct((TOP_K, T), F32),
        jax.ShapeDtypeStruct((NE, LANES), F32),
    ]
    return specs, shapes


def _tri():
    a = np.arange(256)
    return jnp.asarray((a[:, None] <= a[None, :]).astype(np.float32), BF16)


def _post(o, xp, xs, mod, w_o, g_ffn, wr_hl, wr_hi):
    const = lambda shape: pl.BlockSpec(shape, lambda i: (0,) * len(shape))
    out_specs, out_shape = _route_outs(TILE)
    return pl.pallas_call(
        _post_kernel,
        grid=(NT,),
        in_specs=[
            pl.BlockSpec((TILE, DQ), lambda i: (i, 0)),
            pl.BlockSpec((TILE, D), lambda i: (jnp.minimum(i, NT_CTX - 1), 0)),
            pl.BlockSpec((TILE, D), lambda i: (jnp.maximum(i - NT_CTX, 0), 0)),
            pl.BlockSpec((1, N_MOD, D), lambda i: (_cond_index(i, TILE), 0, 0)),
            const((DQ, D)), const((1, D)), const((D, 2 * LANES)), const((D, LANES)), const((256, 256)),
        ],
        out_specs=out_specs,
        out_shape=out_shape,
        scratch_shapes=[pltpu.VMEM((NE, LANES), F32)],
        compiler_params=_vmem_limit(56),
        name="post_attn",
    )(o, xp, xs, mod, w_o, g_ffn, wr_hl, wr_hi, _tri())


def _pool_kernel(block_kinds, n_alias, y0_ref, y1_ref, xin_ref, wcin_ref, modp_ref, mod_ref, g_ref, pw_ref,
                 ps_ref, gf_ref, wr_hl_ref, wr_hi_ref, tri_ref, band_ref, inv_cnt_ref, cnt_in_ref, *refs):
    (x1_ref, h_ref, e_ref, rank_ref, wcol_ref, cnt_out_ref, cnt_ref, hhi_ref, hlo_ref) = refs[n_alias:]
    i = pl.program_id(0)

    @pl.when(i == 0)
    def _():
        cnt_ref[...] = cnt_in_ref[...]

    x = _combined(y0_ref, y1_ref, xin_ref, wcin_ref, modp_ref)
    h = _modulate(x, g_ref[...] * (1.0 + mod_ref[0, 1:2, :]), mod_ref[0, 0:1, :])
    @pl.when(i == 0)
    def _():
        zeros = jnp.zeros((PB_HALO, D), BF16)
        for ref in (hhi_ref, hlo_ref):
            ref[0:PB_HALO, :] = zeros
            ref[PB_HALO + PTILE:PB_HALO + PTILE + PB_HALO, :] = zeros

    h_hi = h.astype(BF16)
    hhi_ref[PB_HALO:PB_HALO + PTILE, :] = h_hi
    hlo_ref[PB_HALO:PB_HALO + PTILE, :] = (h - h_hi.astype(F32)).astype(BF16)

    outs = []
    for gi in range(len(POOL_WINDOWS)):
        cols = slice(gi * PGD, (gi + 1) * PGD)
        sums = []
        for b in range(PTILE // PB):
            bm = band_ref[block_kinds[b], gi]
            sums.append(jnp.dot(bm, hhi_ref[b * PB:b * PB + PBW, cols], preferred_element_type=F32)
                        + jnp.dot(bm, hlo_ref[b * PB:b * PB + PBW, cols], preferred_element_type=F32))
        diff = jnp.concatenate(sums, axis=0) * inv_cnt_ref[:, gi:gi + 1] - h[:, cols]
        outs.append(jnp.dot(diff.astype(BF16), pw_ref[gi], preferred_element_type=F32))
    m = jnp.concatenate(outs, axis=1) * ps_ref[...]
    x1 = x + mod_ref[0, 2:3, :] * m
    x1_ref[...] = x1
    hf = _modulate(x1, gf_ref[...] * (1.0 + mod_ref[0, 4:5, :]), mod_ref[0, 3:4, :])
    h_ref[...] = _pack_rows(hf)
    _route(hf, wr_hl_ref, wr_hi_ref, tri_ref, cnt_ref, e_ref, rank_ref, wcol_ref, cnt_out_ref)


def _pool_constants(seq):
    r = np.arange(PB)[:, None]
    d = np.arange(PBW)[None, :] - PB_HALO - r
    found, kinds = {}, []
    for b in range(PTILE // PB):
        tm = (b * PB + r) % seq
        inseq = (tm + d >= 0) & (tm + d < seq)
        mats = np.stack([(inseq & (d >= -(w // 2)) & (d <= w - w // 2 - 1)).astype(np.float32)
                         for w in POOL_WINDOWS])
        kinds.append(found.setdefault(mats.tobytes(), (len(found), mats))[0])
    bands = np.stack([m for _, m in sorted(found.values(), key=lambda kv: kv[0])])
    tm = np.arange(PTILE) % seq
    inv_cnt = np.zeros((PTILE, LANES), np.float32)
    for g, w in enumerate(POOL_WINDOWS):
        left, right = w // 2, w - w // 2 - 1
        inv_cnt[:, g] = 1.0 / (np.minimum(tm + right + 1, seq) - np.maximum(tm - left, 0)).astype(np.float32)
    return jnp.asarray(bands, BF16), tuple(kinds), inv_cnt


def _pool(group, y2g, x, wcol, mod_prev, mod, g_mix, pool_w, pool_scale, g_ffn, wr_hl, wr_hi, cnt_in, prev):
    const = lambda shape: pl.BlockSpec(shape, lambda i: (0,) * len(shape))
    tiles = (T_CTX if group == 0 else T_LAT) // PTILE
    first = 0 if group == 0 else T_CTX // PTILE
    out_specs, out_shape = _route_outs(PTILE, first)
    aliases = () if prev is None else tuple(prev)
    bands, block_kinds, inv_cnt = _pool_constants(SEQ if group == 0 else DEC_SEQ)
    n_fixed = 16
    return pl.pallas_call(
        functools.partial(_pool_kernel, block_kinds, len(aliases)),
        grid=(tiles,),
        in_specs=[
            pl.BlockSpec((PTILE, HALF), lambda i: (i, 0)),
            pl.BlockSpec((PTILE, HALF), lambda i: (tiles + i, 0)),
            pl.BlockSpec((PTILE, D), lambda i: (first + i, 0)),
            pl.BlockSpec((TOP_K, PTILE), lambda i: (0, first + i)),
            pl.BlockSpec((1, N_MOD, D), lambda i: (_cond_index(first + i, PTILE), 0, 0)),
            pl.BlockSpec((1, N_MOD, D), lambda i: (_cond_index(first + i, PTILE), 0, 0)),
            const((1, D)), const((len(POOL_WINDOWS), PGD, PGD)), const((1, D)), const((1, D)),
            const((D, 2 * LANES)), const((D, LANES)), const((256, 256)), const(bands.shape),
            const((PTILE, LANES)), const((NE, LANES)),
        ] + [pl.BlockSpec(memory_space=pl.ANY)] * len(aliases),
        out_specs=out_specs,
        out_shape=out_shape,
        input_output_aliases={n_fixed + k: k for k in range(len(aliases))},
        scratch_shapes=[pltpu.VMEM((NE, LANES), F32), pltpu.VMEM((PTILE + 2 * PB_HALO, D), BF16),
                        pltpu.VMEM((PTILE + 2 * PB_HALO, D), BF16)],
        compiler_params=_vmem_limit(56),
        name="pool_mixer",
    )(y2g, y2g, x, wcol, mod_prev, mod, g_mix, pool_w, pool_scale, g_ffn, wr_hl, wr_hi, _tri(), bands, inv_cnt,
      cnt_in, *aliases)


def _plan_kernel(e_ref, rank_ref, cnt_ref, pos_ref, te_ref, used_ref):
    counts = cnt_ref[:, 0:1]
    tiles_col = jnp.floor((counts + (TM - 1)) * (1.0 / TM))
    sub = lax.broadcasted_iota(I32, (NE, NE), 0)
    lane = lax.broadcasted_iota(I32, (NE, NE), 1)
    tiles_row = jnp.sum(jnp.where(sub == lane, tiles_col, 0.0), axis=0, keepdims=True)
    first_col = jnp.sum(jnp.where(lane < sub, tiles_row, 0.0), axis=1, keepdims=True)
    end_col = first_col + tiles_col
    used = jnp.sum(tiles_col, axis=0, keepdims=True)
    base_col = first_col * TM
    sub_e = lax.broadcasted_iota(I32, (NE, T), 0)
    for k in range(TOP_K):
        hit = sub_e == e_ref[k:k + 1, :]
        base = jnp.sum(jnp.where(hit, base_col, 0.0), axis=0, keepdims=True)
        pos_ref[k:k + 1, :] = base.astype(I32) + rank_ref[k:k + 1, :]
    tile = jnp.minimum(lax.broadcasted_iota(I32, (NE, LANES), 1).astype(F32), used - 1.0)
    te_ref[...] = jnp.sum(jnp.where(tile >= end_col, 1.0, 0.0), axis=0, keepdims=True).astype(I32)
    used_ref[...] = jnp.broadcast_to(used, (1, LANES)).astype(I32)


def _plan(e, rank, cnt):
    assert N_ROW_TILES <= LANES
    pos, te, used = pl.pallas_call(
        _plan_kernel,
        out_shape=[jax.ShapeDtypeStruct((TOP_K, T), I32), jax.ShapeDtypeStruct((1, LANES), I32),
                   jax.ShapeDtypeStruct((1, LANES), I32)],
        name="moe_plan",
    )(e, rank, cnt)
    return pos, te[0, :N_ROW_TILES], used[0, :1]


def _sc_mesh():
    return plsc.VectorSubcoreMesh(core_axis_name="core", subcore_axis_name="subcore")


def _sc_worker():
    return lax.axis_index("core") * SC_SUBCORES + lax.axis_index("subcore")


def _sc_dispatch(h, pos):
    per = T // SC_WORKERS
    nwin = per // SC_WIN

    @functools.partial(
        pl.kernel, out_type=jax.ShapeDtypeStruct((NPAD, HALF), U32), mesh=_sc_mesh(),
        scratch_types=[pltpu.VMEM((2, SC_WIN, HALF), U32), pltpu.VMEM((2, SC_WIN), I32),
                       pltpu.VMEM((2, SC_WIN), I32), pltpu.SemaphoreType.DMA((2,)),
                       pltpu.SemaphoreType.DMA((2,))])
    def run(h_hbm, pos_hbm, xs_hbm, buf, idx0, idx1, sem_in, sem_out):
        first = _sc_worker() * per

        def loads(j, s):
            base = first + j * SC_WIN
            return (pltpu.make_async_copy(h_hbm.at[pl.ds(base, SC_WIN)], buf.at[s], sem_in.at[s]),
                    pltpu.make_async_copy(pos_hbm.at[pl.ds(base, SC_WIN)], idx0.at[s], sem_in.at[s]),
                    pltpu.make_async_copy(pos_hbm.at[pl.ds(T + base, SC_WIN)], idx1.at[s], sem_in.at[s]))

        def stores(s):
            return (pltpu.make_async_copy(buf.at[s], xs_hbm.at[idx0.at[s]], sem_out.at[s]),
                    pltpu.make_async_copy(buf.at[s], xs_hbm.at[idx1.at[s]], sem_out.at[s]))

        for c in loads(0, 0):
            c.start()

        @pl.loop(0, nwin // 2)
        def _(jj):
            for s in range(2):
                j = jj * 2 + s
                for c in loads(j, s):
                    c.wait()
                for c in stores(s):
                    c.start()

                @pl.when(j >= 1)
                def _():
                    for c in stores(1 - s):
                        c.wait()

                @pl.when(j + 1 < nwin)
                def _():
                    for c in loads(j + 1, 1 - s):
                        c.start()

        for c in stores((nwin - 1) % 2):
            c.wait()

    return run(h, pos.reshape(TOP_K * T))


def _sc_gather(ys, pos, group):
    t0, n_g = (0, T_CTX) if group == 0 else (T_CTX, T_LAT)
    idx_all = jnp.concatenate([pos[k, t0:t0 + n_g] for k in range(TOP_K)])
    n = TOP_K * n_g
    per = n // SC_WORKERS
    nwin = per // SC_WIN

    @functools.partial(
        pl.kernel, out_type=jax.ShapeDtypeStruct((n, HALF), U32), mesh=_sc_mesh(),
        scratch_types=[pltpu.VMEM((2, SC_WIN, HALF), U32), pltpu.VMEM((per,), I32),
                       pltpu.SemaphoreType.DMA((2,)), pltpu.SemaphoreType.DMA((2,))])
    def run(ys_hbm, pos_hbm, out_hbm, buf, idx, sem_in, sem_out):
        first = _sc_worker() * per
        pltpu.sync_copy(pos_hbm.at[pl.ds(first, per)], idx)

        def load(j, s):
            return pltpu.make_async_copy(ys_hbm.at[idx.at[pl.ds(j * SC_WIN, SC_WIN)]], buf.at[s], sem_in.at[s])

        def store(j, s):
            return pltpu.make_async_copy(buf.at[s], out_hbm.at[pl.ds(first + j * SC_WIN, SC_WIN)], sem_out.at[s])

        load(0, 0).start()

        @pl.loop(0, nwin // 2)
        def _(jj):
            for s in range(2):
                j = jj * 2 + s
                load(j, s).wait()
                store(j, s).start()

                @pl.when(j >= 1)
                def _():
                    store(j - 1, 1 - s).wait()

                @pl.when(j + 1 < nwin)
                def _():
                    load(j + 1, 1 - s).start()

        store(nwin - 1, (nwin - 1) % 2).wait()

    return run(ys, idx_all)


def _moe_kernel(te_ref, used_ref, x_hbm, wg_ref, wu_ref, wd_ref, y_ref, xbuf, sem):
    del te_ref
    i = pl.program_id(0)
    used = used_ref[0]

    def tile_copy(t):
        slot = lax.rem(t, X_SLOTS)
        return pltpu.make_async_copy(x_hbm.at[pl.ds(t * TM, TM)], xbuf.at[slot], sem.at[slot])

    @pl.when(i == 0)
    def _():
        for t in range(X_SLOTS - 1):
            @pl.when(t < used)
            def _():
                tile_copy(t).start()

    @pl.when(i + (X_SLOTS - 1) < used)
    def _():
        tile_copy(i + (X_SLOTS - 1)).start()

    @pl.when(i < used)
    def _():
        tile_copy(i).wait()
        hi, lo = _unpack_rows(xbuf[lax.rem(i, X_SLOTS)])
        x = jnp.concatenate([hi, lo], axis=1).astype(BF16)
        a = jnp.dot(x, wg_ref[0, 0], preferred_element_type=F32)
        b = jnp.dot(x, wu_ref[0, 0], preferred_element_type=F32)
        hid = a * (1.0 / (1.0 + jnp.exp(-a))) * b
        y_ref[...] = _pack_rows(jnp.dot(hid.astype(BF16), wd_ref[0, 0], preferred_element_type=F32))


def _experts(xs, tile_expert, used, layer, w_gate, w_up, w_down):
    row = lambda i, te, used: (jnp.minimum(i, used[0] - 1), 0)
    weights = lambda i, te, used: (layer, te[i], 0, 0)
    grid_spec = pltpu.PrefetchScalarGridSpec(
        num_scalar_prefetch=2,
        grid=(N_ROW_TILES,),
        in_specs=[
            pl.BlockSpec(memory_space=pl.ANY),
            pl.BlockSpec((1, 1, D, DE), weights),
            pl.BlockSpec((1, 1, D, DE), weights),
            pl.BlockSpec((1, 1, DE, D), weights),
        ],
        out_specs=pl.BlockSpec((TM, HALF), row),
        scratch_shapes=[pltpu.VMEM((X_SLOTS, TM, HALF), U32), pltpu.SemaphoreType.DMA((X_SLOTS,))],
    )
    return pl.pallas_call(
        _moe_kernel,
        grid_spec=grid_spec,
        out_shape=jax.ShapeDtypeStruct((NPAD, HALF), U32),
        compiler_params=pltpu.CompilerParams(dimension_semantics=("arbitrary",),
                                             vmem_limit_bytes=40 * 1024 * 1024),
        name="moe_experts",
    )(tile_expert, used, xs, w_gate, w_up, w_down)


def _combined(y0_ref, y1_ref, x_ref, wcol_ref, mod_ref):
    sub = lax.broadcasted_iota(I32, (LANES, wcol_ref.shape[1]), 0)
    wt = jnp.where(sub == 0, wcol_ref[0:1, :], jnp.where(sub == 1, wcol_ref[1:2, :], 0.0)).T
    w0 = wt[:, 0:1]
    w1 = wt[:, 1:2]
    hi0, lo0 = _unpack_rows(y0_ref[...])
    hi1, lo1 = _unpack_rows(y1_ref[...])
    y = jnp.concatenate([w0 * hi0 + w1 * hi1, w0 * lo0 + w1 * lo1], axis=1)
    return x_ref[...] + mod_ref[0, 5:6, :] * y


def _combine_kernel(y0_ref, y1_ref, x_ref, wcol_ref, mod_ref, o_ref):
    o_ref[...] = _combined(y0_ref, y1_ref, x_ref, wcol_ref, mod_ref)


def _combine(group, y2g, x, wcol, mod):
    n_g = T_CTX if group == 0 else T_LAT
    tiles = n_g // CT
    first = 0 if group == 0 else T_CTX // CT
    return pl.pallas_call(
        _combine_kernel,
        grid=(tiles,),
        in_specs=[
            pl.BlockSpec((CT, HALF), lambda i: (i, 0)),
            pl.BlockSpec((CT, HALF), lambda i: (tiles + i, 0)),
            pl.BlockSpec((CT, D), lambda i: (first + i, 0)),
            pl.BlockSpec((TOP_K, CT), lambda i: (0, first + i)),
            pl.BlockSpec((1, N_MOD, D), lambda i: (_cond_index(first + i, CT), 0, 0)),
        ],
        out_specs=pl.BlockSpec((CT, D), lambda i: (i, 0)),
        out_shape=jax.ShapeDtypeStruct((n_g, D), F32),
        name="moe_combine",
    )(y2g, y2g, x, wcol, mod)


def _moe(h, e, rank, cnt, layer, w_gate, w_up, w_down):
    pos, tile_expert, used = _plan(e, rank, cnt)
    xs = _sc_dispatch(h, pos)
    ys = _experts(xs, tile_expert, used, layer, w_gate, w_up, w_down)
    return _sc_gather(ys, pos, 0), _sc_gather(ys, pos, 1)


def kernel(x_prompt, x_sample, c, cache_k, cache_v, c_ctx, norm_mix_g, norm_ffn_g, ada_w, ada_b, attn_w_qkv, attn_q_norm, attn_k_norm, attn_w_o, pool_w, pool_scale, moe_w_group, moe_w_expert, moe_w_gate, moe_w_up, moe_w_down):
    xp = x_prompt.reshape(T_CTX, D)
    xs = x_sample.reshape(T_LAT, D)
    cond = jnp.concatenate([c_ctx[None, :], c, jnp.zeros((N_COND - 1 - DEC_BATCH, D), F32)], axis=0)
    mod = [_ada(cond, ada_w, ada_b, layer) for layer in range(ada_w.shape[0])]

    qt, k2, vt, new_k, new_v = _qkv(xp, xs, mod[0], norm_mix_g[0:1], attn_w_qkv[0].astype(BF16),
                                    attn_q_norm[0:1], attn_k_norm[0:1])
    o, w_gate, w_up, w_down = _attention(qt, k2, vt, cache_k[:, 0].reshape(DEC_BATCH, PAST, DKV),
                                         cache_v[:, 0].reshape(DEC_BATCH, PAST, DKV),
                                         moe_w_gate, moe_w_up, moe_w_down)
    wr_hl, wr_hi = _router_weights(moe_w_group[0], moe_w_expert[0])
    x1, h, e, rank, wcol, cnt = _post(o, xp, xs, mod[0], attn_w_o[0].astype(BF16), norm_ffn_g[0:1],
                                      wr_hl, wr_hi)
    y2 = _moe(h, e, rank, cnt, 0, w_gate, w_up, w_down)

    wr_hl, wr_hi = _router_weights(moe_w_group[1], moe_w_expert[1])
    pool_args = (x1, wcol, mod[0], mod[1], norm_mix_g[1:2], pool_w[0].astype(BF16), pool_scale[0:1],
                 norm_ffn_g[1:2], wr_hl, wr_hi)
    outs = _pool(0, y2[0], *pool_args, jnp.zeros((NE, LANES), F32), None)
    x3, h, e, rank, wcol, cnt = _pool(1, y2[1], *pool_args, outs[5], outs[:5])
    y2 = _moe(h, e, rank, cnt, 1, w_gate, w_up, w_down)
    y_prompt = _combine(0, y2[0], x3, wcol, mod[1])
    y_sample = _combine(1, y2[1], x3, wcol, mod[1])

    return (y_prompt.reshape(BATCH, SEQ, D), y_sample.reshape(DEC_BATCH, DEC_SEQ, D),
            new_k.reshape(BATCH, 1, SEQ, N_KV, HD), new_v.reshape(BATCH, 1, SEQ, N_KV, HD))
```

```python
import functools

import jax
import jax.numpy as jnp
import numpy as np
from jax import lax
from jax.experimental import pallas as pl
from jax.experimental.pallas import tpu as pltpu
from jax.experimental.pallas import tpu_sc as plsc

F32 = jnp.float32
BF16 = jnp.bfloat16
I32 = jnp.int32
U32 = jnp.uint32

D = 1024
BATCH, SEQ = 32, 256
DEC_BATCH, DEC_SEQ, PAST = 8, 1024, 512
T_CTX = BATCH * SEQ
T_LAT = DEC_BATCH * DEC_SEQ
T = T_CTX + T_LAT
GRID_W = 64
N_HEADS, N_KV, HD = 16, 4, 64
DQ = N_HEADS * HD
DKV = N_KV * HD
ROPE_THETA = 10000.0
POOL_WINDOWS = (2, 4, 8, 16)
PGD = D // len(POOL_WINDOWS)
N_GROUPS, E_PER_G, TOP_K = 4, 8, 2
NE = N_GROUPS * E_PER_G
DE = D // 4
N_MOD = 6
EPS = 1e-6
N_COND = 16

TILE = 1024
NT = T // TILE
NT_CTX = T_CTX // TILE
QB = 256
CTX_PER_STEP = 4
KEY_PART = 256
PTILE = 1024
PB = 128
PBW = 256
PB_HALO = (PBW - PB) // 2
TM = 1024
X_SLOTS = 3
N_ROW_TILES = (TOP_K * T) // TM + NE
NPAD = N_ROW_TILES * TM
CT = 1024
LANES = 128
NEG_INF = float("-inf")
LOG2_E = 1.4426950408889634
Q_SCALE = HD ** -0.5 * LOG2_E
HALF = D // 2
SC_CORES, SC_SUBCORES = 2, 16
SC_WORKERS = SC_CORES * SC_SUBCORES
SC_WIN = 64


def _vmem_limit(mib):
    return pltpu.CompilerParams(vmem_limit_bytes=mib * 1024 * 1024)


def _modulate(x, gain_scale, shift):
    ms = jnp.mean(x * x, axis=-1, keepdims=True)
    return (x * lax.rsqrt(ms + EPS)) * gain_scale + shift


def _head_norm(z, gain, ind, ind_t2):
    ss = jnp.dot((z * z).astype(BF16), ind, preferred_element_type=F32)
    inv = lax.rsqrt(ss * (1.0 / HD) + EPS)
    inv_hi = inv.astype(BF16)
    inv_lo = (inv - inv_hi.astype(F32)).astype(BF16)
    scale = jnp.dot(jnp.concatenate([inv_hi, inv_lo], axis=1), ind_t2, preferred_element_type=F32)
    return z * scale * gain


def _rope(z, cos_t, sin_t):
    lane = lax.broadcasted_iota(I32, (z.shape[0], LANES), 1)
    low = (lane % 32) < 16
    outs = []
    for c in range(z.shape[1] // LANES):
        zc = z[:, c * LANES:(c + 1) * LANES]
        up = pltpu.roll(zc, 16, axis=1)
        dn = pltpu.roll(zc, LANES - 16, axis=1)
        outs.append(zc * cos_t + jnp.where(low, dn, up) * sin_t)
    return jnp.concatenate(outs, axis=1)


def _rope_t(zt, cos_tt, sin_tt):
    sub = lax.broadcasted_iota(I32, (LANES, zt.shape[1]), 0)
    low = (sub % 32) < 16
    outs = []
    for c in range(zt.shape[0] // LANES):
        zc = zt[c * LANES:(c + 1) * LANES]
        up = pltpu.roll(zc, 16, axis=0)
        dn = pltpu.roll(zc, LANES - 16, axis=0)
        outs.append(zc * cos_tt + jnp.where(low, dn, up) * sin_tt)
    return jnp.concatenate(outs, axis=0)


def _pack_rows(z):
    bits = lax.bitcast_convert_type(z.astype(BF16).astype(F32), U32)
    return bits[:, :HALF] | (bits[:, HALF:] >> 16)


def _unpack_rows(p):
    hi = lax.bitcast_convert_type(p & jnp.uint32(0xFFFF0000), F32)
    lo = lax.bitcast_convert_type(p << 16, F32)
    return hi, lo


def _ada_kernel(c_ref, w_ref, b_ref, o_ref):
    c = c_ref[...]
    a = c * (1.0 / (1.0 + jnp.exp(-c)))
    o_ref[0] = jnp.dot(a.astype(BF16), w_ref[0].astype(BF16), preferred_element_type=F32) + b_ref[0]


def _ada(cond, ada_w, ada_b, layer):
    nb = 1536
    depth = ada_w.shape[0]
    out = pl.pallas_call(
        _ada_kernel,
        grid=((N_MOD * D) // nb,),
        in_specs=[
            pl.BlockSpec((N_COND, D), lambda j: (0, 0)),
            pl.BlockSpec((1, D, nb), lambda j: (layer, 0, j)),
            pl.BlockSpec((1, 1, nb), lambda j: (layer, 0, j)),
        ],
        out_specs=pl.BlockSpec((1, N_COND, nb), lambda j: (0, 0, j)),
        out_shape=jax.ShapeDtypeStruct((1, N_COND, N_MOD * D), F32),
        compiler_params=_vmem_limit(40),
        name="ada",
    )(cond, ada_w, ada_b.reshape(depth, 1, N_MOD * D))
    return out.reshape(N_COND, N_MOD, D)


def _cond_index(i, tile):
    n_ctx = T_CTX // tile
    per_batch = DEC_SEQ // tile
    return jnp.where(i < n_ctx, 0, 1 + (i - n_ctx) // per_batch)


def _qkv_kernel(xp_ref, xs_ref, mod_ref, g_ref, w_ref, qg_ref, kg_ref, indq_ref, indqt_ref,
                indk_ref, indkt_ref, cos_ref, sin_ref, cos_t_ref, sin_t_ref,
                qt_ref, k2_ref, vt_ref, nk_ref, nv_ref):
    i = pl.program_id(0)
    is_lat = i >= NT_CTX
    gain_scale = g_ref[...] * (1.0 + mod_ref[0, 1:2, :])
    subs = [slice(b * QB, (b + 1) * QB) for b in range(TILE // QB)]
    hs = [_modulate(jnp.where(is_lat, xs_ref[r, :], xp_ref[r, :]), gain_scale, mod_ref[0, 0:1, :]).astype(BF16)
          for r in subs]
    qkvs = [jnp.dot(h, w_ref[...], preferred_element_type=F32) for h in hs]
    qs = [_head_norm(z[:, :DQ], qg_ref[...], indq_ref[...], indqt_ref[...]) for z in qkvs]
    ks = [_head_norm(z[:, DQ:DQ + DKV], kg_ref[...], indk_ref[...], indkt_ref[...]) for z in qkvs]
    vs = [z[:, DQ + DKV:] for z in qkvs]
    qts = [(q * Q_SCALE).T for q in qs]
    for r, v in zip(subs, vs):
        vt_ref[:, r] = v.T.astype(BF16)

    @pl.when(is_lat)
    def _():
        for b, r in enumerate(subs):
            qt_ref[b] = _rope_t(qts[b], cos_t_ref[:, r], sin_t_ref[:, r]).astype(BF16)
            k2_ref[r, :] = _dup_heads(_rope(ks[b], cos_ref[r, :], sin_ref[r, :])).astype(BF16)

    @pl.when(jnp.logical_not(is_lat))
    def _():
        for b, r in enumerate(subs):
            qt_ref[b] = qts[b].astype(BF16)
            k2_ref[r, :] = _dup_heads(ks[b]).astype(BF16)
            nk_ref[r, :] = ks[b]
            nv_ref[r, :] = vs[b]


def _rope_tables():
    rows = DEC_SEQ // GRID_W
    row = np.broadcast_to(np.arange(rows, dtype=np.float32)[:, None], (rows, GRID_W)).reshape(-1)
    col = np.broadcast_to(np.arange(GRID_W, dtype=np.float32)[None, :], (rows, GRID_W)).reshape(-1)
    axis_dim = HD // 2
    inv_freq = np.power(np.float32(ROPE_THETA),
                        -np.arange(0, axis_dim, 2, dtype=np.float32) / np.float32(axis_dim))
    ang = np.concatenate([row[:, None] * inv_freq, col[:, None] * inv_freq], axis=-1).astype(np.float32)
    cos, sin = np.cos(ang), np.sin(ang)
    quarter = HD // 4
    cos_h = np.concatenate([cos[:, :quarter], cos[:, :quarter], cos[:, quarter:], cos[:, quarter:]], axis=1)
    sin_h = np.concatenate([-sin[:, :quarter], sin[:, :quarter], -sin[:, quarter:], sin[:, quarter:]], axis=1)
    return (np.tile(cos_h, (1, LANES // HD)).astype(np.float32),
            np.tile(sin_h, (1, LANES // HD)).astype(np.float32))


def _head_indicators(width):
    col = np.arange(width)[:, None]
    head = np.arange(LANES)[None, :]
    ind = (col // HD == head).astype(np.float32)
    return jnp.asarray(ind, BF16), jnp.asarray(np.concatenate([ind.T, ind.T], axis=0), BF16)


def _dup_heads(z):
    lane = lax.broadcasted_iota(I32, (z.shape[0], LANES), 1)
    cols = []
    for p in range(N_KV // 2):
        blk = z[:, p * LANES:(p + 1) * LANES]
        swp = pltpu.roll(blk, HD, axis=1)
        cols += [jnp.where(lane < HD, blk, swp), jnp.where(lane < HD, swp, blk)]
    return jnp.concatenate(cols, axis=1)


def _qkv(xp, xs, mod, g, w_qkv, q_gain, k_gain):
    indq, indqt = _head_indicators(DQ)
    indk, indkt = _head_indicators(DKV)
    cos_t, sin_t = _rope_tables()
    per_batch = DEC_SEQ // TILE
    const = lambda shape: pl.BlockSpec(shape, lambda i: (0,) * len(shape))
    return pl.pallas_call(
        _qkv_kernel,
        grid=(NT,),
        in_specs=[
            pl.BlockSpec((TILE, D), lambda i: (jnp.minimum(i, NT_CTX - 1), 0)),
            pl.BlockSpec((TILE, D), lambda i: (jnp.maximum(i - NT_CTX, 0), 0)),
            pl.BlockSpec((1, N_MOD, D), lambda i: (_cond_index(i, TILE), 0, 0)),
            const((1, D)),
            const((D, DQ + 2 * DKV)),
            const((1, DQ)), const((1, DKV)),
            const((DQ, LANES)), const((2 * LANES, DQ)),
            const((DKV, LANES)), const((2 * LANES, DKV)),
            pl.BlockSpec((TILE, LANES), lambda i: (jnp.maximum(i - NT_CTX, 0) % per_batch, 0)),
            pl.BlockSpec((TILE, LANES), lambda i: (jnp.maximum(i - NT_CTX, 0) % per_batch, 0)),
            pl.BlockSpec((LANES, TILE), lambda i: (0, jnp.maximum(i - NT_CTX, 0) % per_batch)),
            pl.BlockSpec((LANES, TILE), lambda i: (0, jnp.maximum(i - NT_CTX, 0) % per_batch)),
        ],
        out_specs=[
            pl.BlockSpec((TILE // QB, DQ, QB), lambda i: (i, 0, 0)),
            pl.BlockSpec((TILE, 2 * DKV), lambda i: (i, 0)),
            pl.BlockSpec((DKV, TILE), lambda i: (0, i)),
            pl.BlockSpec((TILE, DKV), lambda i: (jnp.minimum(i, NT_CTX - 1), 0)),
            pl.BlockSpec((TILE, DKV), lambda i: (jnp.minimum(i, NT_CTX - 1), 0)),
        ],
        out_shape=[
            jax.ShapeDtypeStruct((T // QB, DQ, QB), BF16),
            jax.ShapeDtypeStruct((T, 2 * DKV), BF16),
            jax.ShapeDtypeStruct((DKV, T), BF16),
            jax.ShapeDtypeStruct((T_CTX, DKV), F32),
            jax.ShapeDtypeStruct((T_CTX, DKV), F32),
        ],
        compiler_params=_vmem_limit(56),
        name="qkv",
    )(xp, xs, mod, g, w_qkv, jnp.tile(q_gain, (1, N_HEADS)), jnp.tile(k_gain, (1, N_KV)),
      indq, indqt, indk, indkt, cos_t, sin_t, cos_t.T, sin_t.T)


def _unit_scores(u, pairs, k2_parts, qt_ref):
    cols = []
    for j in range(pairs):
        pair = u * pairs + j
        qt_pair = qt_ref[pair * LANES:(pair + 1) * LANES, :]
        sub = lax.broadcasted_iota(I32, qt_pair.shape, 0)
        zero = jnp.zeros_like(qt_pair)
        cols += [jnp.where(sub < HD, qt_pair, zero), jnp.where(sub >= HD, qt_pair, zero)]
    qtu = jnp.concatenate(cols, axis=1)
    return [jnp.dot(k2, qtu, preferred_element_type=F32) for k2 in k2_parts]


def _softmax_values(ss, vt_parts):
    m = ss[0].max(axis=0, keepdims=True)
    for s in ss[1:]:
        m = jnp.maximum(m, s.max(axis=0, keepdims=True))
    acc = None
    l = None
    for s, vt in zip(ss, vt_parts):
        e = jnp.exp2(s - m)
        ls = e.sum(axis=0, keepdims=True)
        l = ls if l is None else l + ls
        t = jnp.dot(vt, e.astype(BF16), preferred_element_type=F32)
        acc = t if acc is None else acc + t
    return acc * (1.0 / l)


def _attend(jobs, pairs, ahead):
    n_units = N_HEADS // (2 * pairs)
    group = lambda u: (u * 2 * pairs) // (N_HEADS // N_KV)
    units = [(job, u) for job in jobs for u in range(n_units)]
    scores = lambda job, u: _unit_scores(u, pairs, job[0](group(u)), job[2])
    pending = [scores(*unit) for unit in units[:ahead]]
    outs = []
    for i, (job, u) in enumerate(units):
        if i + ahead < len(units):
            pending.append(scores(*units[i + ahead]))
        outs.append(_softmax_values(pending.pop(0), job[1](group(u))))
        if u == n_units - 1:
            o_ref = job[3]
            lq = o_ref.shape[0]
            heads = [o[:, h * lq:(h + 1) * lq] for o in outs for h in range(2 * pairs)]
            o_ref[...] = jnp.concatenate(heads, axis=0).T.astype(BF16)
            outs = []


def _attn_ctx_kernel(qt_ref, k2_ref, vt_ref, o_ref):
    def job(b):
        rows = slice(b * SEQ, (b + 1) * SEQ)
        return (lambda g: [k2_ref[rows, g * LANES:(g + 1) * LANES]],
                lambda g: [vt_ref[g * HD:(g + 1) * HD, rows]], qt_ref.at[b], o_ref.at[rows])
    _attend([job(b) for b in range(CTX_PER_STEP)], pairs=2, ahead=1)


def _attn_lat_kernel(qt_ref, k2c_ref, vtc_ref, k2n_ref, vtn_ref, wg_ref, wu_ref, wd_ref, o_in_ref,
                     o_ref, wgb_ref, wub_ref, wdb_ref):
    del o_in_ref
    wgb_ref[...] = wg_ref[...].astype(BF16)
    wub_ref[...] = wu_ref[...].astype(BF16)
    wdb_ref[...] = wd_ref[...].astype(BF16)
    pc = [slice(h * KEY_PART, (h + 1) * KEY_PART) for h in range(PAST // KEY_PART)]
    pn = [slice(h * KEY_PART, (h + 1) * KEY_PART) for h in range(DEC_SEQ // KEY_PART)]
    _attend([(lambda g: [k2c_ref[0, r, g * LANES:(g + 1) * LANES] for r in pc]
              + [k2n_ref[r, g * LANES:(g + 1) * LANES] for r in pn],
              lambda g: [vtc_ref[0, g * HD:(g + 1) * HD, r] for r in pc]
              + [vtn_ref[g * HD:(g + 1) * HD, r] for r in pn],
              qt_ref.at[0], o_ref)], pairs=1, ahead=2)


def _cache_kernel(ck_ref, cv_ref, k2_ref, vt_ref):
    k2_ref[0] = _dup_heads(ck_ref[0]).astype(BF16)
    vt_ref[0] = cv_ref[0].T.astype(BF16)


def _attention(qt, k2, vt, cache_k, cache_v, w_gate, w_up, w_down):
    k2c, vtc = pl.pallas_call(
        _cache_kernel,
        grid=(DEC_BATCH,),
        in_specs=[
            pl.BlockSpec((1, PAST, DKV), lambda b: (b, 0, 0)),
            pl.BlockSpec((1, PAST, DKV), lambda b: (b, 0, 0)),
        ],
        out_specs=[
            pl.BlockSpec((1, PAST, 2 * DKV), lambda b: (b, 0, 0)),
            pl.BlockSpec((1, DKV, PAST), lambda b: (b, 0, 0)),
        ],
        out_shape=[
            jax.ShapeDtypeStruct((DEC_BATCH, PAST, 2 * DKV), BF16),
            jax.ShapeDtypeStruct((DEC_BATCH, DKV, PAST), BF16),
        ],
        name="cache_prep",
    )(cache_k, cache_v)

    o_ctx = pl.pallas_call(
        _attn_ctx_kernel,
        grid=(BATCH // CTX_PER_STEP,),
        in_specs=[
            pl.BlockSpec((CTX_PER_STEP, DQ, QB), lambda b: (b, 0, 0)),
            pl.BlockSpec((CTX_PER_STEP * SEQ, 2 * DKV), lambda b: (b, 0)),
            pl.BlockSpec((DKV, CTX_PER_STEP * SEQ), lambda b: (0, b)),
        ],
        out_specs=pl.BlockSpec((CTX_PER_STEP * SEQ, DQ), lambda b: (b, 0)),
        out_shape=jax.ShapeDtypeStruct((T, DQ), BF16),
        name="attn_ctx",
    )(qt, k2, vt)

    qb = QB
    nqb = DEC_SEQ // qb
    lat0 = T_CTX // qb
    depth = w_gate.shape[0]
    eps = (depth * NE) // (DEC_BATCH * nqb)
    per_layer = NE // eps
    wblock = lambda b, j: ((b * nqb + j) // per_layer, (b * nqb + j) % per_layer, 0, 0)
    return pl.pallas_call(
        _attn_lat_kernel,
        grid=(DEC_BATCH, nqb),
        in_specs=[
            pl.BlockSpec((1, DQ, QB), lambda b, j: (lat0 + b * nqb + j, 0, 0)),
            pl.BlockSpec((1, PAST, 2 * DKV), lambda b, j: (b, 0, 0)),
            pl.BlockSpec((1, DKV, PAST), lambda b, j: (b, 0, 0)),
            pl.BlockSpec((DEC_SEQ, 2 * DKV), lambda b, j: (T_CTX // DEC_SEQ + b, 0)),
            pl.BlockSpec((DKV, DEC_SEQ), lambda b, j: (0, T_CTX // DEC_SEQ + b)),
            pl.BlockSpec((1, eps, D, DE), wblock),
            pl.BlockSpec((1, eps, D, DE), wblock),
            pl.BlockSpec((1, eps, DE, D), wblock),
            pl.BlockSpec(memory_space=pl.ANY),
        ],
        out_specs=[
            pl.BlockSpec((qb, DQ), lambda b, j: (lat0 + b * nqb + j, 0)),
            pl.BlockSpec((1, eps, D, DE), wblock),
            pl.BlockSpec((1, eps, D, DE), wblock),
            pl.BlockSpec((1, eps, DE, D), wblock),
        ],
        out_shape=[
            jax.ShapeDtypeStruct((T, DQ), BF16),
            jax.ShapeDtypeStruct(w_gate.shape, BF16),
            jax.ShapeDtypeStruct(w_up.shape, BF16),
            jax.ShapeDtypeStruct(w_down.shape, BF16),
        ],
        input_output_aliases={8: 0},
        compiler_params=_vmem_limit(56),
        name="attn_lat",
    )(qt, k2c, vtc, k2, vt, w_gate, w_up, w_down, o_ctx)


def _route(h, wr_hl_ref, wr_hi_ref, tri_ref, cnt_ref, e_ref, rank_ref, wcol_ref, cnt_out_ref):
    tl = h.shape[0]
    h_hi = h.astype(BF16)
    h_lo = (h - h_hi.astype(F32)).astype(BF16)
    lg2 = jnp.dot(h_hi, wr_hl_ref[...], preferred_element_type=F32)
    lg = lg2[:, :LANES] + lg2[:, LANES:] + jnp.dot(h_lo, wr_hi_ref[...], preferred_element_type=F32)
    lt = lg.T

    sub8 = lax.broadcasted_iota(I32, (8, tl), 0).astype(F32)
    gl = jnp.where(sub8 < N_GROUPS, lt[0:8], NEG_INF)
    gmax = gl.max(axis=0, keepdims=True)
    gidx = jnp.min(jnp.where(gl == gmax, sub8, 8.0), axis=0, keepdims=True)
    gsum = jnp.sum(jnp.exp(gl - gmax), axis=0, keepdims=True)
    sel = lt[8:16]
    for g in range(1, N_GROUPS):
        sel = jnp.where(gidx == g, lt[8 + 8 * g:16 + 8 * g], sel)
    t1 = sel.max(axis=0, keepdims=True)
    j1 = jnp.min(jnp.where(sel == t1, sub8, 8.0), axis=0, keepdims=True)
    sel2 = jnp.where(sub8 == j1, NEG_INF, sel)
    t2 = sel2.max(axis=0, keepdims=True)
    j2 = jnp.min(jnp.where(sel2 == t2, sub8, 8.0), axis=0, keepdims=True)
    ex = jnp.exp(t2 - t1)
    den = 1.0 + ex
    gw = 1.0 / gsum
    w0 = gw * (1.0 / den)
    w1 = gw * (ex / den)
    e0 = gidx * E_PER_G + j1
    e1 = gidx * E_PER_G + j2

    sub_e = lax.broadcasted_iota(I32, (NE, tl), 0).astype(F32)
    oh0 = (sub_e == e0).astype(F32)
    oh1 = (sub_e == e1).astype(F32)
    c = oh0 + oh1
    carry = cnt_ref[:, 0:1]
    pieces = []
    for ch in range(tl // 256):
        cc = c[:, ch * 256:(ch + 1) * 256]
        pieces.append(jnp.dot(cc.astype(BF16), tri_ref[...], preferred_element_type=F32) + carry)
        carry = carry + jnp.sum(cc, axis=1, keepdims=True)
    csum = jnp.concatenate(pieces, axis=1)
    cnt_ref[...] = jnp.broadcast_to(carry, (NE, LANES))
    cnt_out_ref[...] = jnp.broadcast_to(carry, (NE, LANES))
    r0 = jnp.sum(oh0 * csum, axis=0, keepdims=True) - 1.0
    r1 = jnp.sum(oh1 * csum, axis=0, keepdims=True) - 1.0
    e_ref[0:1, :] = e0.astype(I32)
    e_ref[1:2, :] = e1.astype(I32)
    rank_ref[0:1, :] = r0.astype(I32)
    rank_ref[1:2, :] = r1.astype(I32)
    wcol_ref[0:1, :] = w0
    wcol_ref[1:2, :] = w1


def _post_kernel(o_ref, xp_hbm, xs_hbm, mod_ref, wo_ref, g_ref, wr_hl_ref, wr_hi_ref, tri_ref,
                 x1_ref, h_ref, e_ref, rank_ref, wcol_ref, cnt_out_ref, cnt_ref, xbuf, xsem):
    i = pl.program_id(0)

    def x_copy(t, start):
        slot = lax.rem(t, X_SLOTS)
        for src, here, tile in ((xp_hbm, t < NT_CTX, t), (xs_hbm, t >= NT_CTX, t - NT_CTX)):
            @pl.when(here)
            def _():
                copy = pltpu.make_async_copy(src.at[pl.ds(tile * TILE, TILE)], xbuf.at[slot], xsem.at[slot])
                copy.start() if start else copy.wait()

    @pl.when(i == 0)
    def _():
        cnt_ref[...] = jnp.zeros_like(cnt_ref)
        for t in range(X_SLOTS - 1):
            x_copy(jnp.int32(t), True)

    @pl.when(i + (X_SLOTS - 1) < NT)
    def _():
        x_copy(i + (X_SLOTS - 1), True)

    x_copy(i, False)
    x = xbuf[lax.rem(i, X_SLOTS)]
    m = jnp.dot(o_ref[...], wo_ref[...], preferred_element_type=F32)
    x1 = x + mod_ref[0, 2:3, :] * m
    x1_ref[...] = x1
    h = _modulate(x1, g_ref[...] * (1.0 + mod_ref[0, 4:5, :]), mod_ref[0, 3:4, :])
    h_ref[...] = _pack_rows(h)
    _route(h, wr_hl_ref, wr_hi_ref, tri_ref, cnt_ref, e_ref, rank_ref, wcol_ref, cnt_out_ref)


def _router_weights(w_group, w_expert):
    w = jnp.concatenate([w_group, jnp.zeros((D, 8 - N_GROUPS), F32),
                         jnp.transpose(w_expert, (1, 0, 2)).reshape(D, NE),
                         jnp.zeros((D, LANES - 8 - NE), F32)], axis=1)
    w_hi = w.astype(BF16)
    w_lo = (w - w_hi.astype(F32)).astype(BF16)
    return jnp.concatenate([w_hi, w_lo], axis=1), w_hi


def _route_outs(tile, first=0):
    specs = [
        pl.BlockSpec((tile, D), lambda i: (first + i, 0)),
        pl.BlockSpec((tile, HALF), lambda i: (first + i, 0)),
        pl.BlockSpec((TOP_K, tile), lambda i: (0, first + i)),
        pl.BlockSpec((TOP_K, tile), lambda i: (0, first + i)),
        pl.BlockSpec((TOP_K, tile), lambda i: (0, first + i)),
        pl.BlockSpec((NE, LANES), lambda i: (0, 0)),
    ]
    shapes = [
        jax.ShapeDtypeStruct((T, D), F32),
        jax.ShapeDtypeStruct((T, HALF), U32),
        jax.ShapeDtypeStruct((TOP_K, T), I32),
        jax.ShapeDtypeStruct((TOP_K, T), I32),
        jax.ShapeDtypeStruct((TOP_K, T), F32),
        jax.ShapeDtypeStruct((NE, LANES), F32),
    ]
    return specs, shapes


def _tri():
    a = np.arange(256)
    return jnp.asarray((a[:, None] <= a[None, :]).astype(np.float32), BF16)


def _post(o, xp, xs, mod, w_o, g_ffn, wr_hl, wr_hi):
    const = lambda shape: pl.BlockSpec(shape, lambda i: (0,) * len(shape))
    out_specs, out_shape = _route_outs(TILE)
    return pl.pallas_call(
        _post_kernel,
        grid=(NT,),
        in_specs=[
            pl.BlockSpec((TILE, DQ), lambda i: (i, 0)),
            pl.BlockSpec(memory_space=pl.ANY),
            pl.BlockSpec(memory_space=pl.ANY),
            pl.BlockSpec((1, N_MOD, D), lambda i: (_cond_index(i, TILE), 0, 0)),
            const((DQ, D)), const((1, D)), const((D, 2 * LANES)), const((D, LANES)), const((256, 256)),
        ],
        out_specs=out_specs,
        out_shape=out_shape,
        scratch_shapes=[pltpu.VMEM((NE, LANES), F32), pltpu.VMEM((X_SLOTS, TILE, D), F32),
                        pltpu.SemaphoreType.DMA((X_SLOTS,))],
        compiler_params=pltpu.CompilerParams(dimension_semantics=("arbitrary",),
                                             vmem_limit_bytes=56 * 1024 * 1024),
        name="post_attn",
    )(o, xp, xs, mod, w_o, g_ffn, wr_hl, wr_hi, _tri())


def _pool_kernel(block_kinds, n_alias, first, tiles, y0_ref, y1_ref, x_hbm, wcin_ref, modp_ref, mod_ref, g_ref,
                 pw_ref, ps_ref, gf_ref, wr_hl_ref, wr_hi_ref, tri_ref, band_ref, inv_cnt_ref, cnt_in_ref, *refs):
    (x1_ref, h_ref, e_ref, rank_ref, wcol_ref, cnt_out_ref, cnt_ref, hhi_ref, hlo_ref, xbuf, xsem) = refs[n_alias:]
    i = pl.program_id(0)

    def x_copy(t):
        slot = lax.rem(t, X_SLOTS)
        return pltpu.make_async_copy(x_hbm.at[pl.ds((first + t) * PTILE, PTILE)], xbuf.at[slot], xsem.at[slot])

    @pl.when(i == 0)
    def _():
        cnt_ref[...] = cnt_in_ref[...]
        for t in range(min(X_SLOTS - 1, tiles)):
            x_copy(t).start()

    @pl.when(i + (X_SLOTS - 1) < tiles)
    def _():
        x_copy(i + (X_SLOTS - 1)).start()

    x_copy(i).wait()
    x = _combined(y0_ref, y1_ref, xbuf.at[lax.rem(i, X_SLOTS)], wcin_ref, modp_ref)
    h = _modulate(x, g_ref[...] * (1.0 + mod_ref[0, 1:2, :]), mod_ref[0, 0:1, :])
    @pl.when(i == 0)
    def _():
        zeros = jnp.zeros((PB_HALO, D), BF16)
        for ref in (hhi_ref, hlo_ref):
            ref[0:PB_HALO, :] = zeros
            ref[PB_HALO + PTILE:PB_HALO + PTILE + PB_HALO, :] = zeros

    h_hi = h.astype(BF16)
    hhi_ref[PB_HALO:PB_HALO + PTILE, :] = h_hi
    hlo_ref[PB_HALO:PB_HALO + PTILE, :] = (h - h_hi.astype(F32)).astype(BF16)

    outs = []
    for gi in range(len(POOL_WINDOWS)):
        cols = slice(gi * PGD, (gi + 1) * PGD)
        sums = []
        for b in range(PTILE // PB):
            bm = band_ref[block_kinds[b], gi]
            sums.append(jnp.dot(bm, hhi_ref[b * PB:b * PB + PBW, cols], preferred_element_type=F32)
                        + jnp.dot(bm, hlo_ref[b * PB:b * PB + PBW, cols], preferred_element_type=F32))
        diff = jnp.concatenate(sums, axis=0) * inv_cnt_ref[:, gi:gi + 1] - h[:, cols]
        outs.append(jnp.dot(diff.astype(BF16), pw_ref[gi], preferred_element_type=F32))
    m = jnp.concatenate(outs, axis=1) * ps_ref[...]
    x1 = x + mod_ref[0, 2:3, :] * m
    x1_ref[...] = x1
    hf = _modulate(x1, gf_ref[...] * (1.0 + mod_ref[0, 4:5, :]), mod_ref[0, 3:4, :])
    h_ref[...] = _pack_rows(hf)
    _route(hf, wr_hl_ref, wr_hi_ref, tri_ref, cnt_ref, e_ref, rank_ref, wcol_ref, cnt_out_ref)


def _pool_constants(seq):
    r = np.arange(PB)[:, None]
    d = np.arange(PBW)[None, :] - PB_HALO - r
    found, kinds = {}, []
    for b in range(PTILE // PB):
        tm = (b * PB + r) % seq
        inseq = (tm + d >= 0) & (tm + d < seq)
        mats = np.stack([(inseq & (d >= -(w // 2)) & (d <= w - w // 2 - 1)).astype(np.float32)
                         for w in POOL_WINDOWS])
        kinds.append(found.setdefault(mats.tobytes(), (len(found), mats))[0])
    bands = np.stack([m for _, m in sorted(found.values(), key=lambda kv: kv[0])])
    tm = np.arange(PTILE) % seq
    inv_cnt = np.zeros((PTILE, LANES), np.float32)
    for g, w in enumerate(POOL_WINDOWS):
        left, right = w // 2, w - w // 2 - 1
        inv_cnt[:, g] = 1.0 / (np.minimum(tm + right + 1, seq) - np.maximum(tm - left, 0)).astype(np.float32)
    return jnp.asarray(bands, BF16), tuple(kinds), inv_cnt


def _pool(group, y2g, x, wcol, mod_prev, mod, g_mix, pool_w, pool_scale, g_ffn, wr_hl, wr_hi, cnt_in, prev):
    const = lambda shape: pl.BlockSpec(shape, lambda i: (0,) * len(shape))
    tiles = (T_CTX if group == 0 else T_LAT) // PTILE
    first = 0 if group == 0 else T_CTX // PTILE
    out_specs, out_shape = _route_outs(PTILE, first)
    aliases = () if prev is None else tuple(prev)
    bands, block_kinds, inv_cnt = _pool_constants(SEQ if group == 0 else DEC_SEQ)
    n_fixed = 16
    return pl.pallas_call(
        functools.partial(_pool_kernel, block_kinds, len(aliases), first, tiles),
        grid=(tiles,),
        in_specs=[
            pl.BlockSpec((PTILE, HALF), lambda i: (i, 0)),
            pl.BlockSpec((PTILE, HALF), lambda i: (tiles + i, 0)),
            pl.BlockSpec(memory_space=pl.ANY),
            pl.BlockSpec((TOP_K, PTILE), lambda i: (0, first + i)),
            pl.BlockSpec((1, N_MOD, D), lambda i: (_cond_index(first + i, PTILE), 0, 0)),
            pl.BlockSpec((1, N_MOD, D), lambda i: (_cond_index(first + i, PTILE), 0, 0)),
            const((1, D)), const((len(POOL_WINDOWS), PGD, PGD)), const((1, D)), const((1, D)),
            const((D, 2 * LANES)), const((D, LANES)), const((256, 256)), const(bands.shape),
            const((PTILE, LANES)), const((NE, LANES)),
        ] + [pl.BlockSpec(memory_space=pl.ANY)] * len(aliases),
        out_specs=out_specs,
        out_shape=out_shape,
        input_output_aliases={n_fixed + k: k for k in range(len(aliases))},
        scratch_shapes=[pltpu.VMEM((NE, LANES), F32), pltpu.VMEM((PTILE + 2 * PB_HALO, D), BF16),
                        pltpu.VMEM((PTILE + 2 * PB_HALO, D), BF16), pltpu.VMEM((X_SLOTS, PTILE, D), F32),
                        pltpu.SemaphoreType.DMA((X_SLOTS,))],
        compiler_params=pltpu.CompilerParams(dimension_semantics=("arbitrary",),
                                             vmem_limit_bytes=56 * 1024 * 1024),
        name="pool_mixer",
    )(y2g, y2g, x, wcol, mod_prev, mod, g_mix, pool_w, pool_scale, g_ffn, wr_hl, wr_hi, _tri(), bands, inv_cnt,
      cnt_in, *aliases)


def _plan_kernel(e_ref, rank_ref, cnt_ref, pos_ref, te_ref, used_ref):
    counts = cnt_ref[:, 0:1]
    tiles_col = jnp.floor((counts + (TM - 1)) * (1.0 / TM))
    sub = lax.broadcasted_iota(I32, (NE, NE), 0)
    lane = lax.broadcasted_iota(I32, (NE, NE), 1)
    tiles_row = jnp.sum(jnp.where(sub == lane, tiles_col, 0.0), axis=0, keepdims=True)
    first_col = jnp.sum(jnp.where(lane < sub, tiles_row, 0.0), axis=1, keepdims=True)
    end_col = first_col + tiles_col
    used = jnp.sum(tiles_col, axis=0, keepdims=True)
    base_col = first_col * TM
    sub_e = lax.broadcasted_iota(I32, (NE, T), 0)
    for k in range(TOP_K):
        hit = sub_e == e_ref[k:k + 1, :]
        base = jnp.sum(jnp.where(hit, base_col, 0.0), axis=0, keepdims=True)
        pos_ref[k:k + 1, :] = base.astype(I32) + rank_ref[k:k + 1, :]
    tile = jnp.minimum(lax.broadcasted_iota(I32, (NE, LANES), 1).astype(F32), used - 1.0)
    te_ref[...] = jnp.sum(jnp.where(tile >= end_col, 1.0, 0.0), axis=0, keepdims=True).astype(I32)
    used_ref[...] = jnp.broadcast_to(used, (1, LANES)).astype(I32)


def _plan(e, rank, cnt):
    assert N_ROW_TILES <= LANES
    pos, te, used = pl.pallas_call(
        _plan_kernel,
        out_shape=[jax.ShapeDtypeStruct((TOP_K, T), I32), jax.ShapeDtypeStruct((1, LANES), I32),
                   jax.ShapeDtypeStruct((1, LANES), I32)],
        name="moe_plan",
    )(e, rank, cnt)
    return pos, te[0, :N_ROW_TILES], used[0, :1]


def _sc_mesh():
    return plsc.VectorSubcoreMesh(core_axis_name="core", subcore_axis_name="subcore")


def _sc_worker():
    return lax.axis_index("core") * SC_SUBCORES + lax.axis_index("subcore")


def _sc_dispatch(h, pos):
    per = T // SC_WORKERS
    nwin = per // SC_WIN

    @functools.partial(
        pl.kernel, out_type=jax.ShapeDtypeStruct((NPAD, HALF), U32), mesh=_sc_mesh(),
        scratch_types=[pltpu.VMEM((2, SC_WIN, HALF), U32), pltpu.VMEM((2, SC_WIN), I32),
                       pltpu.VMEM((2, SC_WIN), I32), pltpu.SemaphoreType.DMA((2,)),
                       pltpu.SemaphoreType.DMA((2,))])
    def run(h_hbm, pos_hbm, xs_hbm, buf, idx0, idx1, sem_in, sem_out):
        first = _sc_worker() * per

        def loads(j, s):
            base = first + j * SC_WIN
            return (pltpu.make_async_copy(h_hbm.at[pl.ds(base, SC_WIN)], buf.at[s], sem_in.at[s]),
                    pltpu.make_async_copy(pos_hbm.at[pl.ds(base, SC_WIN)], idx0.at[s], sem_in.at[s]),
                    pltpu.make_async_copy(pos_hbm.at[pl.ds(T + base, SC_WIN)], idx1.at[s], sem_in.at[s]))

        def stores(s):
            return (pltpu.make_async_copy(buf.at[s], xs_hbm.at[idx0.at[s]], sem_out.at[s]),
                    pltpu.make_async_copy(buf.at[s], xs_hbm.at[idx1.at[s]], sem_out.at[s]))

        for c in loads(0, 0):
            c.start()

        @pl.loop(0, nwin // 2)
        def _(jj):
            for s in range(2):
                j = jj * 2 + s
                for c in loads(j, s):
                    c.wait()
                for c in stores(s):
                    c.start()

                @pl.when(j >= 1)
                def _():
                    for c in stores(1 - s):
                        c.wait()

                @pl.when(j + 1 < nwin)
                def _():
                    for c in loads(j + 1, 1 - s):
                        c.start()

        for c in stores((nwin - 1) % 2):
            c.wait()

    return run(h, pos.reshape(TOP_K * T))


def _sc_gather(ys, pos, group):
    t0, n_g = (0, T_CTX) if group == 0 else (T_CTX, T_LAT)
    idx_all = jnp.concatenate([pos[k, t0:t0 + n_g] for k in range(TOP_K)])
    n = TOP_K * n_g
    per = n // SC_WORKERS
    nwin = per // SC_WIN

    @functools.partial(
        pl.kernel, out_type=jax.ShapeDtypeStruct((n, HALF), U32), mesh=_sc_mesh(),
        scratch_types=[pltpu.VMEM((2, SC_WIN, HALF), U32), pltpu.VMEM((per,), I32),
                       pltpu.SemaphoreType.DMA((2,)), pltpu.SemaphoreType.DMA((2,))])
    def run(ys_hbm, pos_hbm, out_hbm, buf, idx, sem_in, sem_out):
        first = _sc_worker() * per
        pltpu.sync_copy(pos_hbm.at[pl.ds(first, per)], idx)

        def load(j, s):
            return pltpu.make_async_copy(ys_hbm.at[idx.at[pl.ds(j * SC_WIN, SC_WIN)]], buf.at[s], sem_in.at[s])

        def store(j, s):
            return pltpu.make_async_copy(buf.at[s], out_hbm.at[pl.ds(first + j * SC_WIN, SC_WIN)], sem_out.at[s])

        load(0, 0).start()

        @pl.loop(0, nwin // 2)
        def _(jj):
            for s in range(2):
                j = jj * 2 + s
                load(j, s).wait()
                store(j, s).start()

                @pl.when(j >= 1)
                def _():
                    store(j - 1, 1 - s).wait()

                @pl.when(j + 1 < nwin)
                def _():
                    load(j + 1, 1 - s).start()

        store(nwin - 1, (nwin - 1) % 2).wait()

    return run(ys, idx_all)


def _moe_kernel(te_ref, used_ref, x_hbm, wg_ref, wu_ref, wd_ref, y_ref, xbuf, sem):
    del te_ref
    i = pl.program_id(0)
    used = used_ref[0]

    def tile_copy(t):
        slot = lax.rem(t, X_SLOTS)
        return pltpu.make_async_copy(x_hbm.at[pl.ds(t * TM, TM)], xbuf.at[slot], sem.at[slot])

    @pl.when(i == 0)
    def _():
        for t in range(X_SLOTS - 1):
            @pl.when(t < used)
            def _():
                tile_copy(t).start()

    @pl.when(i + (X_SLOTS - 1) < used)
    def _():
        tile_copy(i + (X_SLOTS - 1)).start()

    @pl.when(i < used)
    def _():
        tile_copy(i).wait()
        hi, lo = _unpack_rows(xbuf[lax.rem(i, X_SLOTS)])
        x = jnp.concatenate([hi, lo], axis=1).astype(BF16)
        a = jnp.dot(x, wg_ref[0, 0], preferred_element_type=F32)
        b = jnp.dot(x, wu_ref[0, 0], preferred_element_type=F32)
        hid = a * (1.0 / (1.0 + jnp.exp(-a))) * b
        y_ref[...] = _pack_rows(jnp.dot(hid.astype(BF16), wd_ref[0, 0], preferred_element_type=F32))


def _experts(xs, tile_expert, used, layer, w_gate, w_up, w_down):
    row = lambda i, te, used: (jnp.minimum(i, used[0] - 1), 0)
    weights = lambda i, te, used: (layer, te[i], 0, 0)
    grid_spec = pltpu.PrefetchScalarGridSpec(
        num_scalar_prefetch=2,
        grid=(N_ROW_TILES,),
        in_specs=[
            pl.BlockSpec(memory_space=pl.ANY),
            pl.BlockSpec((1, 1, D, DE), weights),
            pl.BlockSpec((1, 1, D, DE), weights),
            pl.BlockSpec((1, 1, DE, D), weights),
        ],
        out_specs=pl.BlockSpec((TM, HALF), row),
        scratch_shapes=[pltpu.VMEM((X_SLOTS, TM, HALF), U32), pltpu.SemaphoreType.DMA((X_SLOTS,))],
    )
    return pl.pallas_call(
        _moe_kernel,
        grid_spec=grid_spec,
        out_shape=jax.ShapeDtypeStruct((NPAD, HALF), U32),
        compiler_params=pltpu.CompilerParams(dimension_semantics=("arbitrary",),
                                             vmem_limit_bytes=40 * 1024 * 1024),
        name="moe_experts",
    )(tile_expert, used, xs, w_gate, w_up, w_down)


def _combined(y0_ref, y1_ref, x_ref, wcol_ref, mod_ref):
    sub = lax.broadcasted_iota(I32, (LANES, wcol_ref.shape[1]), 0)
    wt = jnp.where(sub == 0, wcol_ref[0:1, :], jnp.where(sub == 1, wcol_ref[1:2, :], 0.0)).T
    w0 = wt[:, 0:1]
    w1 = wt[:, 1:2]
    hi0, lo0 = _unpack_rows(y0_ref[...])
    hi1, lo1 = _unpack_rows(y1_ref[...])
    y = jnp.concatenate([w0 * hi0 + w1 * hi1, w0 * lo0 + w1 * lo1], axis=1)
    return x_ref[...] + mod_ref[0, 5:6, :] * y


def _combine_kernel(y0_ref, y1_ref, x_ref, wcol_ref, mod_ref, o_ref):
    o_ref[...] = _combined(y0_ref, y1_ref, x_ref, wcol_ref, mod_ref)


def _combine(group, y2g, x, wcol, mod):
    n_g = T_CTX if group == 0 else T_LAT
    tiles = n_g // CT
    first = 0 if group == 0 else T_CTX // CT
    return pl.pallas_call(
        _combine_kernel,
        grid=(tiles,),
        in_specs=[
            pl.BlockSpec((CT, HALF), lambda i: (i, 0)),
            pl.BlockSpec((CT, HALF), lambda i: (tiles + i, 0)),
            pl.BlockSpec((CT, D), lambda i: (first + i, 0)),
            pl.BlockSpec((TOP_K, CT), lambda i: (0, first + i)),
            pl.BlockSpec((1, N_MOD, D), lambda i: (_cond_index(first + i, CT), 0, 0)),
        ],
        out_specs=pl.BlockSpec((CT, D), lambda i: (i, 0)),
        out_shape=jax.ShapeDtypeStruct((n_g, D), F32),
        name="moe_combine",
    )(y2g, y2g, x, wcol, mod)


def _moe(h, e, rank, cnt, layer, w_gate, w_up, w_down):
    pos, tile_expert, used = _plan(e, rank, cnt)
    xs = _sc_dispatch(h, pos)
    ys = _experts(xs, tile_expert, used, layer, w_gate, w_up, w_down)
    return _sc_gather(ys, pos, 0), _sc_gather(ys, pos, 1)


def kernel(x_prompt, x_sample, c, cache_k, cache_v, c_ctx, norm_mix_g, norm_ffn_g, ada_w, ada_b, attn_w_qkv, attn_q_norm, attn_k_norm, attn_w_o, pool_w, pool_scale, moe_w_group, moe_w_expert, moe_w_gate, moe_w_up, moe_w_down):
    xp = x_prompt.reshape(T_CTX, D)
    xs = x_sample.reshape(T_LAT, D)
    cond = jnp.concatenate([c_ctx[None, :], c, jnp.zeros((N_COND - 1 - DEC_BATCH, D), F32)], axis=0)
    mod = [_ada(cond, ada_w, ada_b, layer) for layer in range(ada_w.shape[0])]

    qt, k2, vt, new_k, new_v = _qkv(xp, xs, mod[0], norm_mix_g[0:1], attn_w_qkv[0].astype(BF16),
                                    attn_q_norm[0:1], attn_k_norm[0:1])
    o, w_gate, w_up, w_down = _attention(qt, k2, vt, cache_k[:, 0].reshape(DEC_BATCH, PAST, DKV),
                                         cache_v[:, 0].reshape(DEC_BATCH, PAST, DKV),
                                         moe_w_gate, moe_w_up, moe_w_down)
    wr_hl, wr_hi = _router_weights(moe_w_group[0], moe_w_expert[0])
    x1, h, e, rank, wcol, cnt = _post(o, xp, xs, mod[0], attn_w_o[0].astype(BF16), norm_ffn_g[0:1],
                                      wr_hl, wr_hi)
    y2 = _moe(h, e, rank, cnt, 0, w_gate, w_up, w_down)

    wr_hl, wr_hi = _router_weights(moe_w_group[1], moe_w_expert[1])
    pool_args = (x1, wcol, mod[0], mod[1], norm_mix_g[1:2], pool_w[0].astype(BF16), pool_scale[0:1],
                 norm_ffn_g[1:2], wr_hl, wr_hi)
    outs = _pool(0, y2[0], *pool_args, jnp.zeros((NE, LANES), F32), None)
    x3, h, e, rank, wcol, cnt = _pool(1, y2[1], *pool_args, outs[5], outs[:5])
    y2 = _moe(h, e, rank, cnt, 1, w_gate, w_up, w_down)
    y_prompt = _combine(0, y2[0], x3, wcol, mod[1])
    y_sample = _combine(1, y2[1], x3, wcol, mod[1])

    return (y_prompt.reshape(BATCH, SEQ, D), y_sample.reshape(DEC_BATCH, DEC_SEQ, D),
            new_k.reshape(BATCH, 1, SEQ, N_KV, HD), new_v.reshape(BATCH, 1, SEQ, N_KV, HD))
```

```python
import functools

import jax
import jax.numpy as jnp
import numpy as np
from jax import lax
from jax.experimental import pallas as pl
from jax.experimental.pallas import tpu as pltpu
from jax.experimental.pallas import tpu_sc as plsc

F32 = jnp.float32
BF16 = jnp.bfloat16
I32 = jnp.int32
U32 = jnp.uint32

D = 1024
BATCH, SEQ = 32, 256
DEC_BATCH, DEC_SEQ, PAST = 8, 1024, 512
T_CTX = BATCH * SEQ
T_LAT = DEC_BATCH * DEC_SEQ
T = T_CTX + T_LAT
GRID_W = 64
N_HEADS, N_KV, HD = 16, 4, 64
DQ = N_HEADS * HD
DKV = N_KV * HD
ROPE_THETA = 10000.0
POOL_WINDOWS = (2, 4, 8, 16)
PGD = D // len(POOL_WINDOWS)
N_GROUPS, E_PER_G, TOP_K = 4, 8, 2
NE = N_GROUPS * E_PER_G
DE = D // 4
N_MOD = 6
EPS = 1e-6
N_COND = 16

TILE = 1024
NT = T // TILE
NT_CTX = T_CTX // TILE
QB = 256
CTX_PER_STEP = 4
KEY_PART = 256
PTILE = 1024
PB = 128
PBW = 256
PB_HALO = (PBW - PB) // 2
TM = 1024
X_SLOTS = 3
N_ROW_TILES = (TOP_K * T) // TM + NE
NPAD = N_ROW_TILES * TM
CT = 1024
LANES = 128
NEG_INF = float("-inf")
LOG2_E = 1.4426950408889634
Q_SCALE = HD ** -0.5 * LOG2_E
HALF = D // 2
SC_CORES, SC_SUBCORES = 2, 16
SC_WORKERS = SC_CORES * SC_SUBCORES
SC_WIN = 64


def _vmem_limit(mib):
    return pltpu.CompilerParams(vmem_limit_bytes=mib * 1024 * 1024)


def _modulate(x, gain_scale, shift):
    ms = jnp.mean(x * x, axis=-1, keepdims=True)
    return (x * lax.rsqrt(ms + EPS)) * gain_scale + shift


def _head_norm(z, gain, ind, ind_t2):
    ss = jnp.dot((z * z).astype(BF16), ind, preferred_element_type=F32)
    inv = lax.rsqrt(ss * (1.0 / HD) + EPS)
    inv_hi = inv.astype(BF16)
    inv_lo = (inv - inv_hi.astype(F32)).astype(BF16)
    scale = jnp.dot(jnp.concatenate([inv_hi, inv_lo], axis=1), ind_t2, preferred_element_type=F32)
    return z * scale * gain


def _rope(z, cos_t, sin_t):
    lane = lax.broadcasted_iota(I32, (z.shape[0], LANES), 1)
    low = (lane % 32) < 16
    outs = []
    for c in range(z.shape[1] // LANES):
        zc = z[:, c * LANES:(c + 1) * LANES]
        up = pltpu.roll(zc, 16, axis=1)
        dn = pltpu.roll(zc, LANES - 16, axis=1)
        outs.append(zc * cos_t + jnp.where(low, dn, up) * sin_t)
    return jnp.concatenate(outs, axis=1)


def _rope_t(zt, cos_tt, sin_tt):
    sub = lax.broadcasted_iota(I32, (LANES, zt.shape[1]), 0)
    low = (sub % 32) < 16
    outs = []
    for c in range(zt.shape[0] // LANES):
        zc = zt[c * LANES:(c + 1) * LANES]
        up = pltpu.roll(zc, 16, axis=0)
        dn = pltpu.roll(zc, LANES - 16, axis=0)
        outs.append(zc * cos_tt + jnp.where(low, dn, up) * sin_tt)
    return jnp.concatenate(outs, axis=0)


def _pack_rows(z):
    bits = lax.bitcast_convert_type(z.astype(BF16).astype(F32), U32)
    return bits[:, :HALF] | (bits[:, HALF:] >> 16)


def _unpack_rows(p):
    hi = lax.bitcast_convert_type(p & jnp.uint32(0xFFFF0000), F32)
    lo = lax.bitcast_convert_type(p << 16, F32)
    return hi, lo


def _ada_kernel(c_ref, w_ref, b_ref, o_ref):
    c = c_ref[...]
    a = c * (1.0 / (1.0 + jnp.exp(-c)))
    o_ref[0] = jnp.dot(a.astype(BF16), w_ref[0].astype(BF16), preferred_element_type=F32) + b_ref[0]


def _ada(cond, ada_w, ada_b, layer):
    nb = 1536
    depth = ada_w.shape[0]
    out = pl.pallas_call(
        _ada_kernel,
        grid=((N_MOD * D) // nb,),
        in_specs=[
            pl.BlockSpec((N_COND, D), lambda j: (0, 0)),
            pl.BlockSpec((1, D, nb), lambda j: (layer, 0, j)),
            pl.BlockSpec((1, 1, nb), lambda j: (layer, 0, j)),
        ],
        out_specs=pl.BlockSpec((1, N_COND, nb), lambda j: (0, 0, j)),
        out_shape=jax.ShapeDtypeStruct((1, N_COND, N_MOD * D), F32),
        compiler_params=_vmem_limit(40),
        name="ada",
    )(cond, ada_w, ada_b.reshape(depth, 1, N_MOD * D))
    return out.reshape(N_COND, N_MOD, D)


def _cond_index(i, tile):
    n_ctx = T_CTX // tile
    per_batch = DEC_SEQ // tile
    return jnp.where(i < n_ctx, 0, 1 + (i - n_ctx) // per_batch)


def _qkv_kernel(xp_ref, xs_ref, mod_ref, g_ref, w_ref, qg_ref, kg_ref, indq_ref, indqt_ref,
                indk_ref, indkt_ref, cos_ref, sin_ref, cos_t_ref, sin_t_ref,
                qt_ref, k2_ref, vt_ref, nk_ref, nv_ref):
    i = pl.program_id(0)
    is_lat = i >= NT_CTX
    gain_scale = g_ref[...] * (1.0 + mod_ref[0, 1:2, :])
    subs = [slice(b * QB, (b + 1) * QB) for b in range(TILE // QB)]
    hs = [_modulate(jnp.where(is_lat, xs_ref[r, :], xp_ref[r, :]), gain_scale, mod_ref[0, 0:1, :]).astype(BF16)
          for r in subs]
    qkvs = [jnp.dot(h, w_ref[...], preferred_element_type=F32) for h in hs]
    qs = [_head_norm(z[:, :DQ], qg_ref[...], indq_ref[...], indqt_ref[...]) for z in qkvs]
    ks = [_head_norm(z[:, DQ:DQ + DKV], kg_ref[...], indk_ref[...], indkt_ref[...]) for z in qkvs]
    vs = [z[:, DQ + DKV:] for z in qkvs]
    qts = [(q * Q_SCALE).T for q in qs]
    for r, v in zip(subs, vs):
        vt_ref[:, r] = v.T.astype(BF16)

    @pl.when(is_lat)
    def _():
        for b, r in enumerate(subs):
            qt_ref[b] = _rope_t(qts[b], cos_t_ref[:, r], sin_t_ref[:, r]).astype(BF16)
            k2_ref[r, :] = _dup_heads(_rope(ks[b], cos_ref[r, :], sin_ref[r, :])).astype(BF16)

    @pl.when(jnp.logical_not(is_lat))
    def _():
        for b, r in enumerate(subs):
            qt_ref[b] = qts[b].astype(BF16)
            k2_ref[r, :] = _dup_heads(ks[b]).astype(BF16)
            nk_ref[r, :] = ks[b]
            nv_ref[r, :] = vs[b]


def _rope_tables():
    rows = DEC_SEQ // GRID_W
    row = np.broadcast_to(np.arange(rows, dtype=np.float32)[:, None], (rows, GRID_W)).reshape(-1)
    col = np.broadcast_to(np.arange(GRID_W, dtype=np.float32)[None, :], (rows, GRID_W)).reshape(-1)
    axis_dim = HD // 2
    inv_freq = np.power(np.float32(ROPE_THETA),
                        -np.arange(0, axis_dim, 2, dtype=np.float32) / np.float32(axis_dim))
    ang = np.concatenate([row[:, None] * inv_freq, col[:, None] * inv_freq], axis=-1).astype(np.float32)
    cos, sin = np.cos(ang), np.sin(ang)
    quarter = HD // 4
    cos_h = np.concatenate([cos[:, :quarter], cos[:, :quarter], cos[:, quarter:], cos[:, quarter:]], axis=1)
    sin_h = np.concatenate([-sin[:, :quarter], sin[:, :quarter], -sin[:, quarter:], sin[:, quarter:]], axis=1)
    return (np.tile(cos_h, (1, LANES // HD)).astype(np.float32),
            np.tile(sin_h, (1, LANES // HD)).astype(np.float32))


def _head_indicators(width):
    col = np.arange(width)[:, None]
    head = np.arange(LANES)[None, :]
    ind = (col // HD == head).astype(np.float32)
    return jnp.asarray(ind, BF16), jnp.asarray(np.concatenate([ind.T, ind.T], axis=0), BF16)


def _dup_heads(z):
    lane = lax.broadcasted_iota(I32, (z.shape[0], LANES), 1)
    cols = []
    for p in range(N_KV // 2):
        blk = z[:, p * LANES:(p + 1) * LANES]
        swp = pltpu.roll(blk, HD, axis=1)
        cols += [jnp.where(lane < HD, blk, swp), jnp.where(lane < HD, swp, blk)]
    return jnp.concatenate(cols, axis=1)


def _qkv(xp, xs, mod, g, w_qkv, q_gain, k_gain):
    indq, indqt = _head_indicators(DQ)
    indk, indkt = _head_indicators(DKV)
    cos_t, sin_t = _rope_tables()
    per_batch = DEC_SEQ // TILE
    const = lambda shape: pl.BlockSpec(shape, lambda i: (0,) * len(shape))
    return pl.pallas_call(
        _qkv_kernel,
        grid=(NT,),
        in_specs=[
            pl.BlockSpec((TILE, D), lambda i: (jnp.minimum(i, NT_CTX - 1), 0)),
            pl.BlockSpec((TILE, D), lambda i: (jnp.maximum(i - NT_CTX, 0), 0)),
            pl.BlockSpec((1, N_MOD, D), lambda i: (_cond_index(i, TILE), 0, 0)),
            const((1, D)),
            const((D, DQ + 2 * DKV)),
            const((1, DQ)), const((1, DKV)),
            const((DQ, LANES)), const((2 * LANES, DQ)),
            const((DKV, LANES)), const((2 * LANES, DKV)),
            pl.BlockSpec((TILE, LANES), lambda i: (jnp.maximum(i - NT_CTX, 0) % per_batch, 0)),
            pl.BlockSpec((TILE, LANES), lambda i: (jnp.maximum(i - NT_CTX, 0) % per_batch, 0)),
            pl.BlockSpec((LANES, TILE), lambda i: (0, jnp.maximum(i - NT_CTX, 0) % per_batch)),
            pl.BlockSpec((LANES, TILE), lambda i: (0, jnp.maximum(i - NT_CTX, 0) % per_batch)),
        ],
        out_specs=[
            pl.BlockSpec((TILE // QB, DQ, QB), lambda i: (i, 0, 0)),
            pl.BlockSpec((TILE, 2 * DKV), lambda i: (i, 0)),
            pl.BlockSpec((DKV, TILE), lambda i: (0, i)),
            pl.BlockSpec((TILE, DKV), lambda i: (jnp.minimum(i, NT_CTX - 1), 0)),
            pl.BlockSpec((TILE, DKV), lambda i: (jnp.minimum(i, NT_CTX - 1), 0)),
        ],
        out_shape=[
            jax.ShapeDtypeStruct((T // QB, DQ, QB), BF16),
            jax.ShapeDtypeStruct((T, 2 * DKV), BF16),
            jax.ShapeDtypeStruct((DKV, T), BF16),
            jax.ShapeDtypeStruct((T_CTX, DKV), F32),
            jax.ShapeDtypeStruct((T_CTX, DKV), F32),
        ],
        compiler_params=_vmem_limit(56),
        name="qkv",
    )(xp, xs, mod, g, w_qkv, jnp.tile(q_gain, (1, N_HEADS)), jnp.tile(k_gain, (1, N_KV)),
      indq, indqt, indk, indkt, cos_t, sin_t, cos_t.T, sin_t.T)


def _unit_scores(u, pairs, k2_parts, qt_ref):
    cols = []
    for j in range(pairs):
        pair = u * pairs + j
        qt_pair = qt_ref[pair * LANES:(pair + 1) * LANES, :]
        sub = lax.broadcasted_iota(I32, qt_pair.shape, 0)
        zero = jnp.zeros_like(qt_pair)
        cols += [jnp.where(sub < HD, qt_pair, zero), jnp.where(sub >= HD, qt_pair, zero)]
    qtu = jnp.concatenate(cols, axis=1)
    return [jnp.dot(k2, qtu, preferred_element_type=F32) for k2 in k2_parts]


def _softmax_values(ss, vt_parts):
    m = ss[0].max(axis=0, keepdims=True)
    for s in ss[1:]:
        m = jnp.maximum(m, s.max(axis=0, keepdims=True))
    acc = None
    l = None
    for s, vt in zip(ss, vt_parts):
        e = jnp.exp2(s - m)
        ls = e.sum(axis=0, keepdims=True)
        l = ls if l is None else l + ls
        t = jnp.dot(vt, e.astype(BF16), preferred_element_type=F32)
        acc = t if acc is None else acc + t
    return acc * (1.0 / l)


def _attend(jobs, pairs, ahead):
    n_units = N_HEADS // (2 * pairs)
    group = lambda u: (u * 2 * pairs) // (N_HEADS // N_KV)
    units = [(job, u) for job in jobs for u in range(n_units)]
    scores = lambda job, u: _unit_scores(u, pairs, job[0](group(u)), job[2])
    pending = [scores(*unit) for unit in units[:ahead]]
    outs = []
    for i, (job, u) in enumerate(units):
        if i + ahead < len(units):
            pending.append(scores(*units[i + ahead]))
        outs.append(_softmax_values(pending.pop(0), job[1](group(u))))
        if u == n_units - 1:
            o_ref = job[3]
            lq = o_ref.shape[0]
            heads = [o[:, h * lq:(h + 1) * lq] for o in outs for h in range(2 * pairs)]
            o_ref[...] = jnp.concatenate(heads, axis=0).T.astype(BF16)
            outs = []


def _attn_ctx_kernel(qt_ref, k2_ref, vt_ref, o_ref):
    def job(b):
        rows = slice(b * SEQ, (b + 1) * SEQ)
        return (lambda g: [k2_ref[rows, g * LANES:(g + 1) * LANES]],
                lambda g: [vt_ref[g * HD:(g + 1) * HD, rows]], qt_ref.at[b], o_ref.at[rows])
    _attend([job(b) for b in range(CTX_PER_STEP)], pairs=2, ahead=1)


def _attn_lat_kernel(qt_ref, k2c_ref, vtc_ref, k2n_ref, vtn_ref, wg_ref, wu_ref, wd_ref, o_in_ref,
                     o_ref, wgb_ref, wub_ref, wdb_ref):
    del o_in_ref
    wgb_ref[...] = wg_ref[...].astype(BF16)
    wub_ref[...] = wu_ref[...].astype(BF16)
    wdb_ref[...] = wd_ref[...].astype(BF16)
    pc = [slice(h * KEY_PART, (h + 1) * KEY_PART) for h in range(PAST // KEY_PART)]
    pn = [slice(h * KEY_PART, (h + 1) * KEY_PART) for h in range(DEC_SEQ // KEY_PART)]
    _attend([(lambda g: [k2c_ref[0, r, g * LANES:(g + 1) * LANES] for r in pc]
              + [k2n_ref[r, g * LANES:(g + 1) * LANES] for r in pn],
              lambda g: [vtc_ref[0, g * HD:(g + 1) * HD, r] for r in pc]
              + [vtn_ref[g * HD:(g + 1) * HD, r] for r in pn],
              qt_ref.at[0], o_ref)], pairs=1, ahead=2)


def _cache_kernel(ck_ref, cv_ref, k2_ref, vt_ref):
    k2_ref[0] = _dup_heads(ck_ref[0]).astype(BF16)
    vt_ref[0] = cv_ref[0].T.astype(BF16)


def _attention(qt, k2, vt, cache_k, cache_v, w_gate, w_up, w_down):
    k2c, vtc = pl.pallas_call(
        _cache_kernel,
        grid=(DEC_BATCH,),
        in_specs=[
            pl.BlockSpec((1, PAST, DKV), lambda b: (b, 0, 0)),
            pl.BlockSpec((1, PAST, DKV), lambda b: (b, 0, 0)),
        ],
        out_specs=[
            pl.BlockSpec((1, PAST, 2 * DKV), lambda b: (b, 0, 0)),
            pl.BlockSpec((1, DKV, PAST), lambda b: (b, 0, 0)),
        ],
        out_shape=[
            jax.ShapeDtypeStruct((DEC_BATCH, PAST, 2 * DKV), BF16),
            jax.ShapeDtypeStruct((DEC_BATCH, DKV, PAST), BF16),
        ],
        name="cache_prep",
    )(cache_k, cache_v)

    o_ctx = pl.pallas_call(
        _attn_ctx_kernel,
        grid=(BATCH // CTX_PER_STEP,),
        in_specs=[
            pl.BlockSpec((CTX_PER_STEP, DQ, QB), lambda b: (b, 0, 0)),
            pl.BlockSpec((CTX_PER_STEP * SEQ, 2 * DKV), lambda b: (b, 0)),
            pl.BlockSpec((DKV, CTX_PER_STEP * SEQ), lambda b: (0, b)),
        ],
        out_specs=pl.BlockSpec((CTX_PER_STEP * SEQ, DQ), lambda b: (b, 0)),
        out_shape=jax.ShapeDtypeStruct((T, DQ), BF16),
        name="attn_ctx",
    )(qt, k2, vt)

    qb = QB
    nqb = DEC_SEQ // qb
    lat0 = T_CTX // qb
    depth = w_gate.shape[0]
    eps = (depth * NE) // (DEC_BATCH * nqb)
    per_layer = NE // eps
    wblock = lambda b, j: ((b * nqb + j) // per_layer, (b * nqb + j) % per_layer, 0, 0)
    return pl.pallas_call(
        _attn_lat_kernel,
        grid=(DEC_BATCH, nqb),
        in_specs=[
            pl.BlockSpec((1, DQ, QB), lambda b, j: (lat0 + b * nqb + j, 0, 0)),
            pl.BlockSpec((1, PAST, 2 * DKV), lambda b, j: (b, 0, 0)),
            pl.BlockSpec((1, DKV, PAST), lambda b, j: (b, 0, 0)),
            pl.BlockSpec((DEC_SEQ, 2 * DKV), lambda b, j: (T_CTX // DEC_SEQ + b, 0)),
            pl.BlockSpec((DKV, DEC_SEQ), lambda b, j: (0, T_CTX // DEC_SEQ + b)),
            pl.BlockSpec((1, eps, D, DE), wblock),
            pl.BlockSpec((1, eps, D, DE), wblock),
            pl.BlockSpec((1, eps, DE, D), wblock),
            pl.BlockSpec(memory_space=pl.ANY),
        ],
        out_specs=[
            pl.BlockSpec((qb, DQ), lambda b, j: (lat0 + b * nqb + j, 0)),
            pl.BlockSpec((1, eps, D, DE), wblock),
            pl.BlockSpec((1, eps, D, DE), wblock),
            pl.BlockSpec((1, eps, DE, D), wblock),
        ],
        out_shape=[
            jax.ShapeDtypeStruct((T, DQ), BF16),
            jax.ShapeDtypeStruct(w_gate.shape, BF16),
            jax.ShapeDtypeStruct(w_up.shape, BF16),
            jax.ShapeDtypeStruct(w_down.shape, BF16),
        ],
        input_output_aliases={8: 0},
        compiler_params=_vmem_limit(56),
        name="attn_lat",
    )(qt, k2c, vtc, k2, vt, w_gate, w_up, w_down, o_ctx)


def _route(h, wr_hl_ref, wr_hi_ref, tri_ref, cnt_ref, e_ref, rank_ref, wcol_ref, cnt_out_ref):
    tl = h.shape[0]
    h_hi = h.astype(BF16)
    h_lo = (h - h_hi.astype(F32)).astype(BF16)
    lg2 = jnp.dot(h_hi, wr_hl_ref[...], preferred_element_type=F32)
    lg = lg2[:, :LANES] + lg2[:, LANES:] + jnp.dot(h_lo, wr_hi_ref[...], preferred_element_type=F32)
    lt = lg.T

    sub8 = lax.broadcasted_iota(I32, (8, tl), 0).astype(F32)
    gl = jnp.where(sub8 < N_GROUPS, lt[0:8], NEG_INF)
    gmax = gl.max(axis=0, keepdims=True)
    gidx = jnp.min(jnp.where(gl == gmax, sub8, 8.0), axis=0, keepdims=True)
    gsum = jnp.sum(jnp.exp(gl - gmax), axis=0, keepdims=True)
    sel = lt[8:16]
    for g in range(1, N_GROUPS):
        sel = jnp.where(gidx == g, lt[8 + 8 * g:16 + 8 * g], sel)
    t1 = sel.max(axis=0, keepdims=True)
    j1 = jnp.min(jnp.where(sel == t1, sub8, 8.0), axis=0, keepdims=True)
    sel2 = jnp.where(sub8 == j1, NEG_INF, sel)
    t2 = sel2.max(axis=0, keepdims=True)
    j2 = jnp.min(jnp.where(sel2 == t2, sub8, 8.0), axis=0, keepdims=True)
    ex = jnp.exp(t2 - t1)
    den = 1.0 + ex
    gw = 1.0 / gsum
    w0 = gw * (1.0 / den)
    w1 = gw * (ex / den)
    e0 = gidx * E_PER_G + j1
    e1 = gidx * E_PER_G + j2

    sub_e = lax.broadcasted_iota(I32, (NE, tl), 0).astype(F32)
    oh0 = (sub_e == e0).astype(F32)
    oh1 = (sub_e == e1).astype(F32)
    c = oh0 + oh1
    carry = cnt_ref[:, 0:1]
    pieces = []
    for ch in range(tl // 256):
        cc = c[:, ch * 256:(ch + 1) * 256]
        pieces.append(jnp.dot(cc.astype(BF16), tri_ref[...], preferred_element_type=F32) + carry)
        carry = carry + jnp.sum(cc, axis=1, keepdims=True)
    csum = jnp.concatenate(pieces, axis=1)
    cnt_ref[...] = jnp.broadcast_to(carry, (NE, LANES))
    cnt_out_ref[...] = jnp.broadcast_to(carry, (NE, LANES))
    r0 = jnp.sum(oh0 * csum, axis=0, keepdims=True) - 1.0
    r1 = jnp.sum(oh1 * csum, axis=0, keepdims=True) - 1.0
    e_ref[0:1, :] = e0.astype(I32)
    e_ref[1:2, :] = e1.astype(I32)
    rank_ref[0:1, :] = r0.astype(I32)
    rank_ref[1:2, :] = r1.astype(I32)
    wcol_ref[0:1, :] = w0
    wcol_ref[1:2, :] = w1


def _post_kernel(o_ref, xp_ref, xs_ref, mod_ref, wo_ref, g_ref, wr_hl_ref, wr_hi_ref, tri_ref,
                 x1_ref, h_ref, e_ref, rank_ref, wcol_ref, cnt_out_ref, cnt_ref):
    i = pl.program_id(0)

    @pl.when(i == 0)
    def _():
        cnt_ref[...] = jnp.zeros_like(cnt_ref)

    x = jnp.where(i >= NT_CTX, xs_ref[...], xp_ref[...])
    m = jnp.dot(o_ref[...], wo_ref[...], preferred_element_type=F32)
    x1 = x + mod_ref[0, 2:3, :] * m
    x1_ref[...] = x1
    h = _modulate(x1, g_ref[...] * (1.0 + mod_ref[0, 4:5, :]), mod_ref[0, 3:4, :])
    h_ref[...] = _pack_rows(h)
    _route(h, wr_hl_ref, wr_hi_ref, tri_ref, cnt_ref, e_ref, rank_ref, wcol_ref, cnt_out_ref)


def _router_weights(w_group, w_expert):
    w = jnp.concatenate([w_group, jnp.zeros((D, 8 - N_GROUPS), F32),
                         jnp.transpose(w_expert, (1, 0, 2)).reshape(D, NE),
                         jnp.zeros((D, LANES - 8 - NE), F32)], axis=1)
    w_hi = w.astype(BF16)
    w_lo = (w - w_hi.astype(F32)).astype(BF16)
    return jnp.concatenate([w_hi, w_lo], axis=1), w_hi


def _route_outs(tile, first=0):
    specs = [
        pl.BlockSpec((tile, D), lambda i: (first + i, 0)),
        pl.BlockSpec((tile, HALF), lambda i: (first + i, 0)),
        pl.BlockSpec((TOP_K, tile), lambda i: (0, first + i)),
        pl.BlockSpec((TOP_K, tile), lambda i: (0, first + i)),
        pl.BlockSpec((TOP_K, tile), lambda i: (0, first + i)),
        pl.BlockSpec((NE, LANES), lambda i: (0, 0)),
    ]
    shapes = [
        jax.ShapeDtypeStruct((T, D), F32),
        jax.ShapeDtypeStruct((T, HALF), U32),
        jax.ShapeDtypeStruct((TOP_K, T), I32),
        jax.ShapeDtypeStruct((TOP_K, T), I32),
        jax.ShapeDtypeStruct((TOP_K, T), F32),
        jax.ShapeDtypeStruct((NE, LANES), F32),
    ]
    return specs, shapes


def _tri():
    a = np.arange(256)
    return jnp.asarray((a[:, None] <= a[None, :]).astype(np.float32), BF16)


def _post(o, xp, xs, mod, w_o, g_ffn, wr_hl, wr_hi):
    const = lambda shape: pl.BlockSpec(shape, lambda i: (0,) * len(shape))
    out_specs, out_shape = _route_outs(TILE)
    return pl.pallas_call(
        _post_kernel,
        grid=(NT,),
        in_specs=[
            pl.BlockSpec((TILE, DQ), lambda i: (i, 0)),
            pl.BlockSpec((TILE, D), lambda i: (jnp.minimum(i, NT_CTX - 1), 0)),
            pl.BlockSpec((TILE, D), lambda i: (jnp.maximum(i - NT_CTX, 0), 0)),
            pl.BlockSpec((1, N_MOD, D), lambda i: (_cond_index(i, TILE), 0, 0)),
            const((DQ, D)), const((1, D)), const((D, 2 * LANES)), const((D, LANES)), const((256, 256)),
        ],
        out_specs=out_specs,
        out_shape=out_shape,
        scratch_shapes=[pltpu.VMEM((NE, LANES), F32)],
        compiler_params=_vmem_limit(56),
        name="post_attn",
    )(o, xp, xs, mod, w_o, g_ffn, wr_hl, wr_hi, _tri())


def _pool_kernel(block_kinds, n_alias, y0_ref, y1_ref, xin_ref, wcin_ref, modp_ref, mod_ref, g_ref, pw_ref,
                 ps_ref, gf_ref, wr_hl_ref, wr_hi_ref, tri_ref, band_ref, inv_cnt_ref, cnt_in_ref, *refs):
    (x1_ref, h_ref, e_ref, rank_ref, wcol_ref, cnt_out_ref, cnt_ref, hhi_ref, hlo_ref) = refs[n_alias:]
    i = pl.program_id(0)

    @pl.when(i == 0)
    def _():
        cnt_ref[...] = cnt_in_ref[...]

    x = _combined(y0_ref, y1_ref, xin_ref, wcin_ref, modp_ref)
    h = _modulate(x, g_ref[...] * (1.0 + mod_ref[0, 1:2, :]), mod_ref[0, 0:1, :])
    @pl.when(i == 0)
    def _():
        zeros = jnp.zeros((PB_HALO, D), BF16)
        for ref in (hhi_ref, hlo_ref):
            ref[0:PB_HALO, :] = zeros
            ref[PB_HALO + PTILE:PB_HALO + PTILE + PB_HALO, :] = zeros

    h_hi = h.astype(BF16)
    hhi_ref[PB_HALO:PB_HALO + PTILE, :] = h_hi
    hlo_ref[PB_HALO:PB_HALO + PTILE, :] = (h - h_hi.astype(F32)).astype(BF16)

    outs = []
    for gi in range(len(POOL_WINDOWS)):
        cols = slice(gi * PGD, (gi + 1) * PGD)
        sums = []
        for b in range(PTILE // PB):
            bm = band_ref[block_kinds[b], gi]
            sums.append(jnp.dot(bm, hhi_ref[b * PB:b * PB + PBW, cols], preferred_element_type=F32)
                        + jnp.dot(bm, hlo_ref[b * PB:b * PB + PBW, cols], preferred_element_type=F32))
        diff = jnp.concatenate(sums, axis=0) * inv_cnt_ref[:, gi:gi + 1] - h[:, cols]
        outs.append(jnp.dot(diff.astype(BF16), pw_ref[gi], preferred_element_type=F32))
    m = jnp.concatenate(outs, axis=1) * ps_ref[...]
    x1 = x + mod_ref[0, 2:3, :] * m
    x1_ref[...] = x1
    hf = _modulate(x1, gf_ref[...] * (1.0 + mod_ref[0, 4:5, :]), mod_ref[0, 3:4, :])
    h_ref[...] = _pack_rows(hf)
    _route(hf, wr_hl_ref, wr_hi_ref, tri_ref, cnt_ref, e_ref, rank_ref, wcol_ref, cnt_out_ref)


def _pool_constants(seq):
    r = np.arange(PB)[:, None]
    d = np.arange(PBW)[None, :] - PB_HALO - r
    found, kinds = {}, []
    for b in range(PTILE // PB):
        tm = (b * PB + r) % seq
        inseq = (tm + d >= 0) & (tm + d < seq)
        mats = np.stack([(inseq & (d >= -(w // 2)) & (d <= w - w // 2 - 1)).astype(np.float32)
                         for w in POOL_WINDOWS])
        kinds.append(found.setdefault(mats.tobytes(), (len(found), mats))[0])
    bands = np.stack([m for _, m in sorted(found.values(), key=lambda kv: kv[0])])
    tm = np.arange(PTILE) % seq
    inv_cnt = np.zeros((PTILE, LANES), np.float32)
    for g, w in enumerate(POOL_WINDOWS):
        left, right = w // 2, w - w // 2 - 1
        inv_cnt[:, g] = 1.0 / (np.minimum(tm + right + 1, seq) - np.maximum(tm - left, 0)).astype(np.float32)
    return jnp.asarray(bands, BF16), tuple(kinds), inv_cnt


def _pool(group, y2g, x, wcol, mod_prev, mod, g_mix, pool_w, pool_scale, g_ffn, wr_hl, wr_hi, cnt_in, prev):
    const = lambda shape: pl.BlockSpec(shape, lambda i: (0,) * len(shape))
    tiles = (T_CTX if group == 0 else T_LAT) // PTILE
    first = 0 if group == 0 else T_CTX // PTILE
    out_specs, out_shape = _route_outs(PTILE, first)
    aliases = () if prev is None else tuple(prev)
    bands, block_kinds, inv_cnt = _pool_constants(SEQ if group == 0 else DEC_SEQ)
    n_fixed = 16
    return pl.pallas_call(
        functools.partial(_pool_kernel, block_kinds, len(aliases)),
        grid=(tiles,),
        in_specs=[
            pl.BlockSpec((PTILE, HALF), lambda i: (i, 0)),
            pl.BlockSpec((PTILE, HALF), lambda i: (tiles + i, 0)),
            pl.BlockSpec((PTILE, D), lambda i: (first + i, 0)),
            pl.BlockSpec((TOP_K, PTILE), lambda i: (0, first + i)),
            pl.BlockSpec((1, N_MOD, D), lambda i: (_cond_index(first + i, PTILE), 0, 0)),
            pl.BlockSpec((1, N_MOD, D), lambda i: (_cond_index(first + i, PTILE), 0, 0)),
            const((1, D)), const((len(POOL_WINDOWS), PGD, PGD)), const((1, D)), const((1, D)),
            const((D, 2 * LANES)), const((D, LANES)), const((256, 256)), const(bands.shape),
            const((PTILE, LANES)), const((NE, LANES)),
        ] + [pl.BlockSpec(memory_space=pl.ANY)] * len(aliases),
        out_specs=out_specs,
        out_shape=out_shape,
        input_output_aliases={n_fixed + k: k for k in range(len(aliases))},
        scratch_shapes=[pltpu.VMEM((NE, LANES), F32), pltpu.VMEM((PTILE + 2 * PB_HALO, D), BF16),
                        pltpu.VMEM((PTILE + 2 * PB_HALO, D), BF16)],
        compiler_params=_vmem_limit(56),
        name="pool_mixer",
    )(y2g, y2g, x, wcol, mod_prev, mod, g_mix, pool_w, pool_scale, g_ffn, wr_hl, wr_hi, _tri(), bands, inv_cnt,
      cnt_in, *aliases)


def _plan_kernel(e_ref, rank_ref, cnt_ref, pos_ref, te_ref, used_ref):
    counts = cnt_ref[:, 0:1]
    tiles_col = jnp.floor((counts + (TM - 1)) * (1.0 / TM))
    sub = lax.broadcasted_iota(I32, (NE, NE), 0)
    lane = lax.broadcasted_iota(I32, (NE, NE), 1)
    tiles_row = jnp.sum(jnp.where(sub == lane, tiles_col, 0.0), axis=0, keepdims=True)
    first_col = jnp.sum(jnp.where(lane < sub, tiles_row, 0.0), axis=1, keepdims=True)
    end_col = first_col + tiles_col
    used = jnp.sum(tiles_col, axis=0, keepdims=True)
    base_col = first_col * TM
    sub_e = lax.broadcasted_iota(I32, (NE, T), 0)
    for k in range(TOP_K):
        hit = sub_e == e_ref[k:k + 1, :]
        base = jnp.sum(jnp.where(hit, base_col, 0.0), axis=0, keepdims=True)
        pos_ref[k:k + 1, :] = base.astype(I32) + rank_ref[k:k + 1, :]
    tile = jnp.minimum(lax.broadcasted_iota(I32, (NE, LANES), 1).astype(F32), used - 1.0)
    te_ref[...] = jnp.sum(jnp.where(tile >= end_col, 1.0, 0.0), axis=0, keepdims=True).astype(I32)
    used_ref[...] = jnp.broadcast_to(used, (1, LANES)).astype(I32)


def _plan(e, rank, cnt):
    assert N_ROW_TILES <= LANES
    pos, te, used = pl.pallas_call(
        _plan_kernel,
        out_shape=[jax.ShapeDtypeStruct((TOP_K, T), I32), jax.ShapeDtypeStruct((1, LANES), I32),
                   jax.ShapeDtypeStruct((1, LANES), I32)],
        name="moe_plan",
    )(e, rank, cnt)
    return pos, te[0, :N_ROW_TILES], used[0, :1]


def _sc_mesh():
    return plsc.VectorSubcoreMesh(core_axis_name="core", subcore_axis_name="subcore")


def _sc_worker():
    return lax.axis_index("core") * SC_SUBCORES + lax.axis_index("subcore")


def _sc_dispatch(h, pos):
    per = T // SC_WORKERS
    nwin = per // SC_WIN

    @functools.partial(
        pl.kernel, out_type=jax.ShapeDtypeStruct((NPAD, HALF), U32), mesh=_sc_mesh(),
        scratch_types=[pltpu.VMEM((2, SC_WIN, HALF), U32), pltpu.VMEM((2, SC_WIN), I32),
                       pltpu.VMEM((2, SC_WIN), I32), pltpu.SemaphoreType.DMA((2,)),
                       pltpu.SemaphoreType.DMA((2,))])
    def run(h_hbm, pos_hbm, xs_hbm, buf, idx0, idx1, sem_in, sem_out):
        first = _sc_worker() * per

        def loads(j, s):
            base = first + j * SC_WIN
            return (pltpu.make_async_copy(h_hbm.at[pl.ds(base, SC_WIN)], buf.at[s], sem_in.at[s]),
                    pltpu.make_async_copy(pos_hbm.at[pl.ds(base, SC_WIN)], idx0.at[s], sem_in.at[s]),
                    pltpu.make_async_copy(pos_hbm.at[pl.ds(T + base, SC_WIN)], idx1.at[s], sem_in.at[s]))

        def stores(s):
            return (pltpu.make_async_copy(buf.at[s], xs_hbm.at[idx0.at[s]], sem_out.at[s]),
                    pltpu.make_async_copy(buf.at[s], xs_hbm.at[idx1.at[s]], sem_out.at[s]))

        for c in loads(0, 0):
            c.start()

        @pl.loop(0, nwin // 2)
        def _(jj):
            for s in range(2):
                j = jj * 2 + s
                for c in loads(j, s):
                    c.wait()
                for c in stores(s):
                    c.start()

                @pl.when(j >= 1)
                def _():
                    for c in stores(1 - s):
                        c.wait()

                @pl.when(j + 1 < nwin)
                def _():
                    for c in loads(j + 1, 1 - s):
                        c.start()

        for c in stores((nwin - 1) % 2):
            c.wait()

    return run(h, pos.reshape(TOP_K * T))


def _sc_gather(ys, pos, group):
    t0, n_g = (0, T_CTX) if group == 0 else (T_CTX, T_LAT)
    idx_all = jnp.concatenate([pos[k, t0:t0 + n_g] for k in range(TOP_K)])
    n = TOP_K * n_g
    per = n // SC_WORKERS
    nwin = per // SC_WIN

    @functools.partial(
        pl.kernel, out_type=jax.ShapeDtypeStruct((n, HALF), U32), mesh=_sc_mesh(),
        scratch_types=[pltpu.VMEM((2, SC_WIN, HALF), U32), pltpu.VMEM((per,), I32),
                       pltpu.SemaphoreType.DMA((2,)), pltpu.SemaphoreType.DMA((2,))])
    def run(ys_hbm, pos_hbm, out_hbm, buf, idx, sem_in, sem_out):
        first = _sc_worker() * per
        pltpu.sync_copy(pos_hbm.at[pl.ds(first, per)], idx)

        def load(j, s):
            return pltpu.make_async_copy(ys_hbm.at[idx.at[pl.ds(j * SC_WIN, SC_WIN)]], buf.at[s], sem_in.at[s])

        def store(j, s):
            return pltpu.make_async_copy(buf.at[s], out_hbm.at[pl.ds(first + j * SC_WIN, SC_WIN)], sem_out.at[s])

        load(0, 0).start()

        @pl.loop(0, nwin // 2)
        def _(jj):
            for s in range(2):
                j = jj * 2 + s
                load(j, s).wait()
                store(j, s).start()

                @pl.when(j >= 1)
                def _():
                    store(j - 1, 1 - s).wait()

                @pl.when(j + 1 < nwin)
                def _():
                    load(j + 1, 1 - s).start()

        store(nwin - 1, (nwin - 1) % 2).wait()

    return run(ys, idx_all)


def _moe_kernel(te_ref, used_ref, x_hbm, wg_ref, wu_ref, wd_ref, y_ref, xbuf, sem):
    del te_ref
    i = pl.program_id(0)
    used = used_ref[0]
    ring_priority = 1

    def tile_copy(t):
        slot = lax.rem(t, X_SLOTS)
        return pltpu.make_async_copy(x_hbm.at[pl.ds(t * TM, TM)], xbuf.at[slot], sem.at[slot])

    @pl.when(i == 0)
    def _():
        for t in range(X_SLOTS - 1):
            @pl.when(t < used)
            def _():
                tile_copy(t).start(priority=ring_priority)

    @pl.when(i + (X_SLOTS - 1) < used)
    def _():
        tile_copy(i + (X_SLOTS - 1)).start(priority=ring_priority)

    @pl.when(i < used)
    def _():
        tile_copy(i).wait()
        hi, lo = _unpack_rows(xbuf[lax.rem(i, X_SLOTS)])
        x = jnp.concatenate([hi, lo], axis=1).astype(BF16)
        a = jnp.dot(x, wg_ref[0, 0], preferred_element_type=F32)
        b = jnp.dot(x, wu_ref[0, 0], preferred_element_type=F32)
        hid = a * (1.0 / (1.0 + jnp.exp(-a))) * b
        y_ref[...] = _pack_rows(jnp.dot(hid.astype(BF16), wd_ref[0, 0], preferred_element_type=F32))


def _experts(xs, tile_expert, used, layer, w_gate, w_up, w_down):
    row = lambda i, te, used: (jnp.minimum(i, used[0] - 1), 0)
    weights = lambda i, te, used: (layer, te[i], 0, 0)
    grid_spec = pltpu.PrefetchScalarGridSpec(
        num_scalar_prefetch=2,
        grid=(N_ROW_TILES,),
        in_specs=[
            pl.BlockSpec(memory_space=pl.ANY),
            pl.BlockSpec((1, 1, D, DE), weights),
            pl.BlockSpec((1, 1, D, DE), weights),
            pl.BlockSpec((1, 1, DE, D), weights),
        ],
        out_specs=pl.BlockSpec((TM, HALF), row),
        scratch_shapes=[pltpu.VMEM((X_SLOTS, TM, HALF), U32), pltpu.SemaphoreType.DMA((X_SLOTS,))],
    )
    return pl.pallas_call(
        _moe_kernel,
        grid_spec=grid_spec,
        out_shape=jax.ShapeDtypeStruct((NPAD, HALF), U32),
        compiler_params=pltpu.CompilerParams(dimension_semantics=("arbitrary",),
                                             vmem_limit_bytes=40 * 1024 * 1024),
        name="moe_experts",
    )(tile_expert, used, xs, w_gate, w_up, w_down)


def _combined(y0_ref, y1_ref, x_ref, wcol_ref, mod_ref):
    sub = lax.broadcasted_iota(I32, (LANES, wcol_ref.shape[1]), 0)
    wt = jnp.where(sub == 0, wcol_ref[0:1, :], jnp.where(sub == 1, wcol_ref[1:2, :], 0.0)).T
    w0 = wt[:, 0:1]
    w1 = wt[:, 1:2]
    hi0, lo0 = _unpack_rows(y0_ref[...])
    hi1, lo1 = _unpack_rows(y1_ref[...])
    y = jnp.concatenate([w0 * hi0 + w1 * hi1, w0 * lo0 + w1 * lo1], axis=1)
    return x_ref[...] + mod_ref[0, 5:6, :] * y


def _combine_kernel(y0_ref, y1_ref, x_ref, wcol_ref, mod_ref, o_ref):
    o_ref[...] = _combined(y0_ref, y1_ref, x_ref, wcol_ref, mod_ref)


def _combine(group, y2g, x, wcol, mod):
    n_g = T_CTX if group == 0 else T_LAT
    tiles = n_g // CT
    first = 0 if group == 0 else T_CTX // CT
    return pl.pallas_call(
        _combine_kernel,
        grid=(tiles,),
        in_specs=[
            pl.BlockSpec((CT, HALF), lambda i: (i, 0)),
            pl.BlockSpec((CT, HALF), lambda i: (tiles + i, 0)),
            pl.BlockSpec((CT, D), lambda i: (first + i, 0)),
            pl.BlockSpec((TOP_K, CT), lambda i: (0, first + i)),
            pl.BlockSpec((1, N_MOD, D), lambda i: (_cond_index(first + i, CT), 0, 0)),
        ],
        out_specs=pl.BlockSpec((CT, D), lambda i: (i, 0)),
        out_shape=jax.ShapeDtypeStruct((n_g, D), F32),
        name="moe_combine",
    )(y2g, y2g, x, wcol, mod)


def _moe(h, e, rank, cnt, layer, w_gate, w_up, w_down):
    pos, tile_expert, used = _plan(e, rank, cnt)
    xs = _sc_dispatch(h, pos)
    ys = _experts(xs, tile_expert, used, layer, w_gate, w_up, w_down)
    return _sc_gather(ys, pos, 0), _sc_gather(ys, pos, 1)


def kernel(x_prompt, x_sample, c, cache_k, cache_v, c_ctx, norm_mix_g, norm_ffn_g, ada_w, ada_b, attn_w_qkv, attn_q_norm, attn_k_norm, attn_w_o, pool_w, pool_scale, moe_w_group, moe_w_expert, moe_w_gate, moe_w_up, moe_w_down):
    xp = x_prompt.reshape(T_CTX, D)
    xs = x_sample.reshape(T_LAT, D)
    cond = jnp.concatenate([c_ctx[None, :], c, jnp.zeros((N_COND - 1 - DEC_BATCH, D), F32)], axis=0)
    mod = [_ada(cond, ada_w, ada_b, layer) for layer in range(ada_w.shape[0])]

    qt, k2, vt, new_k, new_v = _qkv(xp, xs, mod[0], norm_mix_g[0:1], attn_w_qkv[0].astype(BF16),
                                    attn_q_norm[0:1], attn_k_norm[0:1])
    o, w_gate, w_up, w_down = _attention(qt, k2, vt, cache_k[:, 0].reshape(DEC_BATCH, PAST, DKV),
                                         cache_v[:, 0].reshape(DEC_BATCH, PAST, DKV),
                                         moe_w_gate, moe_w_up, moe_w_down)
    wr_hl, wr_hi = _router_weights(moe_w_group[0], moe_w_expert[0])
    x1, h, e, rank, wcol, cnt = _post(o, xp, xs, mod[0], attn_w_o[0].astype(BF16), norm_ffn_g[0:1],
                                      wr_hl, wr_hi)
    y2 = _moe(h, e, rank, cnt, 0, w_gate, w_up, w_down)

    wr_hl, wr_hi = _router_weights(moe_w_group[1], moe_w_expert[1])
    pool_args = (x1, wcol, mod[0], mod[1], norm_mix_g[1:2], pool_w[0].astype(BF16), pool_scale[0:1],
                 norm_ffn_g[1:2], wr_hl, wr_hi)
    outs = _pool(0, y2[0], *pool_args, jnp.zeros((NE, LANES), F32), None)
    x3, h, e, rank, wcol, cnt = _pool(1, y2[1], *pool_args, outs[5], outs[:5])
    y2 = _moe(h, e, rank, cnt, 1, w_gate, w_up, w_down)
    y_prompt = _combine(0, y2[0], x3, wcol, mod[1])
    y_sample = _combine(1, y2[1], x3, wcol, mod[1])

    return (y_prompt.reshape(BATCH, SEQ, D), y_sample.reshape(DEC_BATCH, DEC_SEQ, D),
            new_k.reshape(BATCH, 1, SEQ, N_KV, HD), new_v.reshape(BATCH, 1, SEQ, N_KV, HD))
```

```python
import functools

import jax
import jax.numpy as jnp
import numpy as np
from jax import lax
from jax.experimental import pallas as pl
from jax.experimental.pallas import tpu as pltpu
from jax.experimental.pallas import tpu_sc as plsc

F32 = jnp.float32
BF16 = jnp.bfloat16
I32 = jnp.int32
U32 = jnp.uint32

D = 1024
BATCH, SEQ = 32, 256
DEC_BATCH, DEC_SEQ, PAST = 8, 1024, 512
T_CTX = BATCH * SEQ
T_LAT = DEC_BATCH * DEC_SEQ
T = T_CTX + T_LAT
GRID_W = 64
N_HEADS, N_KV, HD = 16, 4, 64
DQ = N_HEADS * HD
DKV = N_KV * HD
ROPE_THETA = 10000.0
POOL_WINDOWS = (2, 4, 8, 16)
PGD = D // len(POOL_WINDOWS)
N_GROUPS, E_PER_G, TOP_K = 4, 8, 2
NE = N_GROUPS * E_PER_G
DE = D // 4
N_MOD = 6
EPS = 1e-6
N_COND = 16

TILE = 1024
NT = T // TILE
NT_CTX = T_CTX // TILE
QB = 256
CTX_PER_STEP = 4
KEY_PART = 256
PTILE = 1024
PB = 128
PBW = 256
PB_HALO = (PBW - PB) // 2
TM = 1024
X_SLOTS = 3
N_ROW_TILES = (TOP_K * T) // TM + NE
NPAD = N_ROW_TILES * TM
CT = 1024
LANES = 128
NEG_INF = float("-inf")
LOG2_E = 1.4426950408889634
Q_SCALE = HD ** -0.5 * LOG2_E
HALF = D // 2
SC_CORES, SC_SUBCORES = 2, 16
SC_WORKERS = SC_CORES * SC_SUBCORES
SC_WIN = 64


def _vmem_limit(mib):
    return pltpu.CompilerParams(vmem_limit_bytes=mib * 1024 * 1024)


def _modulate(x, gain_scale, shift):
    ms = jnp.mean(x * x, axis=-1, keepdims=True)
    return (x * lax.rsqrt(ms + EPS)) * gain_scale + shift


def _head_norm(z, gain, ind, ind_t2):
    ss = jnp.dot((z * z).astype(BF16), ind, preferred_element_type=F32)
    inv = lax.rsqrt(ss * (1.0 / HD) + EPS)
    inv_hi = inv.astype(BF16)
    inv_lo = (inv - inv_hi.astype(F32)).astype(BF16)
    scale = jnp.dot(jnp.concatenate([inv_hi, inv_lo], axis=1), ind_t2, preferred_element_type=F32)
    return z * scale * gain


def _rope(z, cos_t, sin_t):
    lane = lax.broadcasted_iota(I32, (z.shape[0], LANES), 1)
    low = (lane % 32) < 16
    outs = []
    for c in range(z.shape[1] // LANES):
        zc = z[:, c * LANES:(c + 1) * LANES]
        up = pltpu.roll(zc, 16, axis=1)
        dn = pltpu.roll(zc, LANES - 16, axis=1)
        outs.append(zc * cos_t + jnp.where(low, dn, up) * sin_t)
    return jnp.concatenate(outs, axis=1)


def _rope_t(zt, cos_tt, sin_tt):
    sub = lax.broadcasted_iota(I32, (LANES, zt.shape[1]), 0)
    low = (sub % 32) < 16
    outs = []
    for c in range(zt.shape[0] // LANES):
        zc = zt[c * LANES:(c + 1) * LANES]
        up = pltpu.roll(zc, 16, axis=0)
        dn = pltpu.roll(zc, LANES - 16, axis=0)
        outs.append(zc * cos_tt + jnp.where(low, dn, up) * sin_tt)
    return jnp.concatenate(outs, axis=0)


def _pack_rows(z):
    bits = lax.bitcast_convert_type(z.astype(BF16).astype(F32), U32)
    return bits[:, :HALF] | (bits[:, HALF:] >> 16)


def _unpack_rows(p):
    hi = lax.bitcast_convert_type(p & jnp.uint32(0xFFFF0000), F32)
    lo = lax.bitcast_convert_type(p << 16, F32)
    return hi, lo


def _ada_kernel(c_ref, w_ref, b_ref, o_ref):
    c = c_ref[...]
    a = c * (1.0 / (1.0 + jnp.exp(-c)))
    o_ref[0] = jnp.dot(a.astype(BF16), w_ref[0].astype(BF16), preferred_element_type=F32) + b_ref[0]


def _ada(cond, ada_w, ada_b, layer):
    nb = 1536
    depth = ada_w.shape[0]
    out = pl.pallas_call(
        _ada_kernel,
        grid=((N_MOD * D) // nb,),
        in_specs=[
            pl.BlockSpec((N_COND, D), lambda j: (0, 0)),
            pl.BlockSpec((1, D, nb), lambda j: (layer, 0, j)),
            pl.BlockSpec((1, 1, nb), lambda j: (layer, 0, j)),
        ],
        out_specs=pl.BlockSpec((1, N_COND, nb), lambda j: (0, 0, j)),
        out_shape=jax.ShapeDtypeStruct((1, N_COND, N_MOD * D), F32),
        compiler_params=_vmem_limit(40),
        name="ada",
    )(cond, ada_w, ada_b.reshape(depth, 1, N_MOD * D))
    return out.reshape(N_COND, N_MOD, D)


def _cond_index(i, tile):
    n_ctx = T_CTX // tile
    per_batch = DEC_SEQ // tile
    return jnp.where(i < n_ctx, 0, 1 + (i - n_ctx) // per_batch)


def _qkv_kernel(xp_ref, xs_ref, mod_ref, g_ref, w_ref, qg_ref, kg_ref, indq_ref, indqt_ref,
                indk_ref, indkt_ref, cos_ref, sin_ref, cos_t_ref, sin_t_ref,
                qt_ref, k2_ref, vt_ref, nk_ref, nv_ref, wb_ref):
    i = pl.program_id(0)
    is_lat = i >= NT_CTX

    @pl.when(i == 0)
    def _():
        wb_ref[...] = w_ref[...].astype(BF16)
    gain_scale = g_ref[...] * (1.0 + mod_ref[0, 1:2, :])
    subs = [slice(b * QB, (b + 1) * QB) for b in range(TILE // QB)]
    hs = [_modulate(jnp.where(is_lat, xs_ref[r, :], xp_ref[r, :]), gain_scale, mod_ref[0, 0:1, :]).astype(BF16)
          for r in subs]
    qkvs = [jnp.dot(h, wb_ref[...], preferred_element_type=F32) for h in hs]
    qs = [_head_norm(z[:, :DQ], qg_ref[...], indq_ref[...], indqt_ref[...]) for z in qkvs]
    ks = [_head_norm(z[:, DQ:DQ + DKV], kg_ref[...], indk_ref[...], indkt_ref[...]) for z in qkvs]
    vs = [z[:, DQ + DKV:] for z in qkvs]
    qts = [(q * Q_SCALE).T for q in qs]
    for r, v in zip(subs, vs):
        vt_ref[:, r] = v.T.astype(BF16)

    @pl.when(is_lat)
    def _():
        for b, r in enumerate(subs):
            qt_ref[b] = _rope_t(qts[b], cos_t_ref[:, r], sin_t_ref[:, r]).astype(BF16)
            k2_ref[r, :] = _dup_heads(_rope(ks[b], cos_ref[r, :], sin_ref[r, :])).astype(BF16)

    @pl.when(jnp.logical_not(is_lat))
    def _():
        for b, r in enumerate(subs):
            qt_ref[b] = qts[b].astype(BF16)
            k2_ref[r, :] = _dup_heads(ks[b]).astype(BF16)
            nk_ref[r, :] = ks[b]
            nv_ref[r, :] = vs[b]


def _rope_tables():
    rows = DEC_SEQ // GRID_W
    row = np.broadcast_to(np.arange(rows, dtype=np.float32)[:, None], (rows, GRID_W)).reshape(-1)
    col = np.broadcast_to(np.arange(GRID_W, dtype=np.float32)[None, :], (rows, GRID_W)).reshape(-1)
    axis_dim = HD // 2
    inv_freq = np.power(np.float32(ROPE_THETA),
                        -np.arange(0, axis_dim, 2, dtype=np.float32) / np.float32(axis_dim))
    ang = np.concatenate([row[:, None] * inv_freq, col[:, None] * inv_freq], axis=-1).astype(np.float32)
    cos, sin = np.cos(ang), np.sin(ang)
    quarter = HD // 4
    cos_h = np.concatenate([cos[:, :quarter], cos[:, :quarter], cos[:, quarter:], cos[:, quarter:]], axis=1)
    sin_h = np.concatenate([-sin[:, :quarter], sin[:, :quarter], -sin[:, quarter:], sin[:, quarter:]], axis=1)
    return (np.tile(cos_h, (1, LANES // HD)).astype(np.float32),
            np.tile(sin_h, (1, LANES // HD)).astype(np.float32))


def _head_indicators(width):
    col = np.arange(width)[:, None]
    head = np.arange(LANES)[None, :]
    ind = (col // HD == head).astype(np.float32)
    return jnp.asarray(ind, BF16), jnp.asarray(np.concatenate([ind.T, ind.T], axis=0), BF16)


def _dup_heads(z):
    lane = lax.broadcasted_iota(I32, (z.shape[0], LANES), 1)
    cols = []
    for p in range(N_KV // 2):
        blk = z[:, p * LANES:(p + 1) * LANES]
        swp = pltpu.roll(blk, HD, axis=1)
        cols += [jnp.where(lane < HD, blk, swp), jnp.where(lane < HD, swp, blk)]
    return jnp.concatenate(cols, axis=1)


def _qkv(xp, xs, mod, g, w_qkv, q_gain, k_gain):
    indq, indqt = _head_indicators(DQ)
    indk, indkt = _head_indicators(DKV)
    cos_t, sin_t = _rope_tables()
    per_batch = DEC_SEQ // TILE
    const = lambda shape: pl.BlockSpec(shape, lambda i: (0,) * len(shape))
    return pl.pallas_call(
        _qkv_kernel,
        grid=(NT,),
        in_specs=[
            pl.BlockSpec((TILE, D), lambda i: (jnp.minimum(i, NT_CTX - 1), 0)),
            pl.BlockSpec((TILE, D), lambda i: (jnp.maximum(i - NT_CTX, 0), 0)),
            pl.BlockSpec((1, N_MOD, D), lambda i: (_cond_index(i, TILE), 0, 0)),
            const((1, D)),
            const((D, DQ + 2 * DKV)),
            const((1, DQ)), const((1, DKV)),
            const((DQ, LANES)), const((2 * LANES, DQ)),
            const((DKV, LANES)), const((2 * LANES, DKV)),
            pl.BlockSpec((TILE, LANES), lambda i: (jnp.maximum(i - NT_CTX, 0) % per_batch, 0)),
            pl.BlockSpec((TILE, LANES), lambda i: (jnp.maximum(i - NT_CTX, 0) % per_batch, 0)),
            pl.BlockSpec((LANES, TILE), lambda i: (0, jnp.maximum(i - NT_CTX, 0) % per_batch)),
            pl.BlockSpec((LANES, TILE), lambda i: (0, jnp.maximum(i - NT_CTX, 0) % per_batch)),
        ],
        out_specs=[
            pl.BlockSpec((TILE // QB, DQ, QB), lambda i: (i, 0, 0)),
            pl.BlockSpec((TILE, 2 * DKV), lambda i: (i, 0)),
            pl.BlockSpec((DKV, TILE), lambda i: (0, i)),
            pl.BlockSpec((TILE, DKV), lambda i: (jnp.minimum(i, NT_CTX - 1), 0)),
            pl.BlockSpec((TILE, DKV), lambda i: (jnp.minimum(i, NT_CTX - 1), 0)),
        ],
        out_shape=[
            jax.ShapeDtypeStruct((T // QB, DQ, QB), BF16),
            jax.ShapeDtypeStruct((T, 2 * DKV), BF16),
            jax.ShapeDtypeStruct((DKV, T), BF16),
            jax.ShapeDtypeStruct((T_CTX, DKV), F32),
            jax.ShapeDtypeStruct((T_CTX, DKV), F32),
        ],
        scratch_shapes=[pltpu.VMEM((D, DQ + 2 * DKV), BF16)],
        compiler_params=pltpu.CompilerParams(dimension_semantics=("arbitrary",),
                                             vmem_limit_bytes=56 * 1024 * 1024),
        name="qkv",
    )(xp, xs, mod, g, w_qkv, jnp.tile(q_gain, (1, N_HEADS)), jnp.tile(k_gain, (1, N_KV)),
      indq, indqt, indk, indkt, cos_t, sin_t, cos_t.T, sin_t.T)


def _unit_scores(u, pairs, k2_parts, qt_ref):
    cols = []
    for j in range(pairs):
        pair = u * pairs + j
        qt_pair = qt_ref[pair * LANES:(pair + 1) * LANES, :]
        sub = lax.broadcasted_iota(I32, qt_pair.shape, 0)
        zero = jnp.zeros_like(qt_pair)
        cols += [jnp.where(sub < HD, qt_pair, zero), jnp.where(sub >= HD, qt_pair, zero)]
    qtu = jnp.concatenate(cols, axis=1)
    return [jnp.dot(k2, qtu, preferred_element_type=F32) for k2 in k2_parts]


def _softmax_values(ss, vt_parts):
    m = ss[0].max(axis=0, keepdims=True)
    for s in ss[1:]:
        m = jnp.maximum(m, s.max(axis=0, keepdims=True))
    acc = None
    l = None
    for s, vt in zip(ss, vt_parts):
        e = jnp.exp2(s - m)
        ls = e.sum(axis=0, keepdims=True)
        l = ls if l is None else l + ls
        t = jnp.dot(vt, e.astype(BF16), preferred_element_type=F32)
        acc = t if acc is None else acc + t
    return acc * (1.0 / l)


def _attend(jobs, pairs, ahead):
    n_units = N_HEADS // (2 * pairs)
    group = lambda u: (u * 2 * pairs) // (N_HEADS // N_KV)
    units = [(job, u) for job in jobs for u in range(n_units)]
    scores = lambda job, u: _unit_scores(u, pairs, job[0](group(u)), job[2])
    pending = [scores(*unit) for unit in units[:ahead]]
    outs = []
    for i, (job, u) in enumerate(units):
        if i + ahead < len(units):
            pending.append(scores(*units[i + ahead]))
        outs.append(_softmax_values(pending.pop(0), job[1](group(u))))
        if u == n_units - 1:
            o_ref = job[3]
            lq = o_ref.shape[0]
            heads = [o[:, h * lq:(h + 1) * lq] for o in outs for h in range(2 * pairs)]
            o_ref[...] = jnp.concatenate(heads, axis=0).T.astype(BF16)
            outs = []


def _attn_ctx_kernel(qt_ref, k2_ref, vt_ref, o_ref):
    def job(b):
        rows = slice(b * SEQ, (b + 1) * SEQ)
        return (lambda g: [k2_ref[rows, g * LANES:(g + 1) * LANES]],
                lambda g: [vt_ref[g * HD:(g + 1) * HD, rows]], qt_ref.at[b], o_ref.at[rows])
    _attend([job(b) for b in range(CTX_PER_STEP)], pairs=2, ahead=1)


def _attn_lat_kernel(qt_ref, k2c_ref, vtc_ref, k2n_ref, vtn_ref, wg_ref, wu_ref, wd_ref, o_in_ref,
                     o_ref, wgb_ref, wub_ref, wdb_ref):
    del o_in_ref
    wgb_ref[...] = wg_ref[...].astype(BF16)
    wub_ref[...] = wu_ref[...].astype(BF16)
    wdb_ref[...] = wd_ref[...].astype(BF16)
    pc = [slice(h * KEY_PART, (h + 1) * KEY_PART) for h in range(PAST // KEY_PART)]
    pn = [slice(h * KEY_PART, (h + 1) * KEY_PART) for h in range(DEC_SEQ // KEY_PART)]
    _attend([(lambda g: [k2c_ref[0, r, g * LANES:(g + 1) * LANES] for r in pc]
              + [k2n_ref[r, g * LANES:(g + 1) * LANES] for r in pn],
              lambda g: [vtc_ref[0, g * HD:(g + 1) * HD, r] for r in pc]
              + [vtn_ref[g * HD:(g + 1) * HD, r] for r in pn],
              qt_ref.at[0], o_ref)], pairs=1, ahead=2)


def _cache_kernel(ck_ref, cv_ref, k2_ref, vt_ref):
    k2_ref[0] = _dup_heads(ck_ref[0]).astype(BF16)
    vt_ref[0] = cv_ref[0].T.astype(BF16)


def _attention(qt, k2, vt, cache_k, cache_v, w_gate, w_up, w_down):
    k2c, vtc = pl.pallas_call(
        _cache_kernel,
        grid=(DEC_BATCH,),
        in_specs=[
            pl.BlockSpec((1, PAST, DKV), lambda b: (b, 0, 0)),
            pl.BlockSpec((1, PAST, DKV), lambda b: (b, 0, 0)),
        ],
        out_specs=[
            pl.BlockSpec((1, PAST, 2 * DKV), lambda b: (b, 0, 0)),
            pl.BlockSpec((1, DKV, PAST), lambda b: (b, 0, 0)),
        ],
        out_shape=[
            jax.ShapeDtypeStruct((DEC_BATCH, PAST, 2 * DKV), BF16),
            jax.ShapeDtypeStruct((DEC_BATCH, DKV, PAST), BF16),
        ],
        name="cache_prep",
    )(cache_k, cache_v)

    o_ctx = pl.pallas_call(
        _attn_ctx_kernel,
        grid=(BATCH // CTX_PER_STEP,),
        in_specs=[
            pl.BlockSpec((CTX_PER_STEP, DQ, QB), lambda b: (b, 0, 0)),
            pl.BlockSpec((CTX_PER_STEP * SEQ, 2 * DKV), lambda b: (b, 0)),
            pl.BlockSpec((DKV, CTX_PER_STEP * SEQ), lambda b: (0, b)),
        ],
        out_specs=pl.BlockSpec((CTX_PER_STEP * SEQ, DQ), lambda b: (b, 0)),
        out_shape=jax.ShapeDtypeStruct((T, DQ), BF16),
        name="attn_ctx",
    )(qt, k2, vt)

    qb = QB
    nqb = DEC_SEQ // qb
    lat0 = T_CTX // qb
    depth = w_gate.shape[0]
    eps = (depth * NE) // (DEC_BATCH * nqb)
    per_layer = NE // eps
    wblock = lambda b, j: ((b * nqb + j) // per_layer, (b * nqb + j) % per_layer, 0, 0)
    return pl.pallas_call(
        _attn_lat_kernel,
        grid=(DEC_BATCH, nqb),
        in_specs=[
            pl.BlockSpec((1, DQ, QB), lambda b, j: (lat0 + b * nqb + j, 0, 0)),
            pl.BlockSpec((1, PAST, 2 * DKV), lambda b, j: (b, 0, 0)),
            pl.BlockSpec((1, DKV, PAST), lambda b, j: (b, 0, 0)),
            pl.BlockSpec((DEC_SEQ, 2 * DKV), lambda b, j: (T_CTX // DEC_SEQ + b, 0)),
            pl.BlockSpec((DKV, DEC_SEQ), lambda b, j: (0, T_CTX // DEC_SEQ + b)),
            pl.BlockSpec((1, eps, D, DE), wblock),
            pl.BlockSpec((1, eps, D, DE), wblock),
            pl.BlockSpec((1, eps, DE, D), wblock),
            pl.BlockSpec(memory_space=pl.ANY),
        ],
        out_specs=[
            pl.BlockSpec((qb, DQ), lambda b, j: (lat0 + b * nqb + j, 0)),
            pl.BlockSpec((1, eps, D, DE), wblock),
            pl.BlockSpec((1, eps, D, DE), wblock),
            pl.BlockSpec((1, eps, DE, D), wblock),
        ],
        out_shape=[
            jax.ShapeDtypeStruct((T, DQ), BF16),
            jax.ShapeDtypeStruct(w_gate.shape, BF16),
            jax.ShapeDtypeStruct(w_up.shape, BF16),
            jax.ShapeDtypeStruct(w_down.shape, BF16),
        ],
        input_output_aliases={8: 0},
        compiler_params=_vmem_limit(56),
        name="attn_lat",
    )(qt, k2c, vtc, k2, vt, w_gate, w_up, w_down, o_ctx)


def _route(h, wr_hl_ref, wr_hi_ref, tri_ref, cnt_ref, e_ref, rank_ref, wcol_ref, cnt_out_ref):
    tl = h.shape[0]
    h_hi = h.astype(BF16)
    h_lo = (h - h_hi.astype(F32)).astype(BF16)
    lg2 = jnp.dot(h_hi, wr_hl_ref[...], preferred_element_type=F32)
    lg = lg2[:, :LANES] + lg2[:, LANES:] + jnp.dot(h_lo, wr_hi_ref[...], preferred_element_type=F32)
    lt = lg.T

    sub8 = lax.broadcasted_iota(I32, (8, tl), 0).astype(F32)
    gl = jnp.where(sub8 < N_GROUPS, lt[0:8], NEG_INF)
    gmax = gl.max(axis=0, keepdims=True)
    gidx = jnp.min(jnp.where(gl == gmax, sub8, 8.0), axis=0, keepdims=True)
    gsum = jnp.sum(jnp.exp(gl - gmax), axis=0, keepdims=True)
    sel = lt[8:16]
    for g in range(1, N_GROUPS):
        sel = jnp.where(gidx == g, lt[8 + 8 * g:16 + 8 * g], sel)
    t1 = sel.max(axis=0, keepdims=True)
    j1 = jnp.min(jnp.where(sel == t1, sub8, 8.0), axis=0, keepdims=True)
    sel2 = jnp.where(sub8 == j1, NEG_INF, sel)
    t2 = sel2.max(axis=0, keepdims=True)
    j2 = jnp.min(jnp.where(sel2 == t2, sub8, 8.0), axis=0, keepdims=True)
    ex = jnp.exp(t2 - t1)
    den = 1.0 + ex
    gw = 1.0 / gsum
    w0 = gw * (1.0 / den)
    w1 = gw * (ex / den)
    e0 = gidx * E_PER_G + j1
    e1 = gidx * E_PER_G + j2

    sub_e = lax.broadcasted_iota(I32, (NE, tl), 0).astype(F32)
    oh0 = (sub_e == e0).astype(F32)
    oh1 = (sub_e == e1).astype(F32)
    c = oh0 + oh1
    carry = cnt_ref[:, 0:1]
    pieces = []
    for ch in range(tl // 256):
        cc = c[:, ch * 256:(ch + 1) * 256]
        pieces.append(jnp.dot(cc.astype(BF16), tri_ref[...], preferred_element_type=F32) + carry)
        carry = carry + jnp.sum(cc, axis=1, keepdims=True)
    csum = jnp.concatenate(pieces, axis=1)
    cnt_ref[...] = jnp.broadcast_to(carry, (NE, LANES))
    cnt_out_ref[...] = jnp.broadcast_to(carry, (NE, LANES))
    r0 = jnp.sum(oh0 * csum, axis=0, keepdims=True) - 1.0
    r1 = jnp.sum(oh1 * csum, axis=0, keepdims=True) - 1.0
    e_ref[0:1, :] = e0.astype(I32)
    e_ref[1:2, :] = e1.astype(I32)
    rank_ref[0:1, :] = r0.astype(I32)
    rank_ref[1:2, :] = r1.astype(I32)
    wcol_ref[0:1, :] = w0
    wcol_ref[1:2, :] = w1


def _post_kernel(o_ref, xp_ref, xs_ref, mod_ref, wo_ref, g_ref, wr_hl_ref, wr_hi_ref, tri_ref,
                 x1_ref, h_ref, e_ref, rank_ref, wcol_ref, cnt_out_ref, cnt_ref):
    i = pl.program_id(0)

    @pl.when(i == 0)
    def _():
        cnt_ref[...] = jnp.zeros_like(cnt_ref)

    x = jnp.where(i >= NT_CTX, xs_ref[...], xp_ref[...])
    m = jnp.dot(o_ref[...], wo_ref[...], preferred_element_type=F32)
    x1 = x + mod_ref[0, 2:3, :] * m
    x1_ref[...] = x1
    h = _modulate(x1, g_ref[...] * (1.0 + mod_ref[0, 4:5, :]), mod_ref[0, 3:4, :])
    h_ref[...] = _pack_rows(h)
    _route(h, wr_hl_ref, wr_hi_ref, tri_ref, cnt_ref, e_ref, rank_ref, wcol_ref, cnt_out_ref)


def _router_weights(w_group, w_expert):
    w = jnp.concatenate([w_group, jnp.zeros((D, 8 - N_GROUPS), F32),
                         jnp.transpose(w_expert, (1, 0, 2)).reshape(D, NE),
                         jnp.zeros((D, LANES - 8 - NE), F32)], axis=1)
    w_hi = w.astype(BF16)
    w_lo = (w - w_hi.astype(F32)).astype(BF16)
    return jnp.concatenate([w_hi, w_lo], axis=1), w_hi


def _route_outs(tile, first=0):
    specs = [
        pl.BlockSpec((tile, D), lambda i: (first + i, 0)),
        pl.BlockSpec((tile, HALF), lambda i: (first + i, 0)),
        pl.BlockSpec((TOP_K, tile), lambda i: (0, first + i)),
        pl.BlockSpec((TOP_K, tile), lambda i: (0, first + i)),
        pl.BlockSpec((TOP_K, tile), lambda i: (0, first + i)),
        pl.BlockSpec((NE, LANES), lambda i: (0, 0)),
    ]
    shapes = [
        jax.ShapeDtypeStruct((T, D), F32),
        jax.ShapeDtypeStruct((T, HALF), U32),
        jax.ShapeDtypeStruct((TOP_K, T), I32),
        jax.ShapeDtypeStruct((TOP_K, T), I32),
        jax.ShapeDtypeStruct((TOP_K, T), F32),
        jax.ShapeDtypeStruct((NE, LANES), F32),
    ]
    return specs, shapes


def _tri():
    a = np.arange(256)
    return jnp.asarray((a[:, None] <= a[None, :]).astype(np.float32), BF16)


def _post(o, xp, xs, mod, w_o, g_ffn, wr_hl, wr_hi):
    const = lambda shape: pl.BlockSpec(shape, lambda i: (0,) * len(shape))
    out_specs, out_shape = _route_outs(TILE)
    return pl.pallas_call(
        _post_kernel,
        grid=(NT,),
        in_specs=[
            pl.BlockSpec((TILE, DQ), lambda i: (i, 0)),
            pl.BlockSpec((TILE, D), lambda i: (jnp.minimum(i, NT_CTX - 1), 0)),
            pl.BlockSpec((TILE, D), lambda i: (jnp.maximum(i - NT_CTX, 0), 0)),
            pl.BlockSpec((1, N_MOD, D), lambda i: (_cond_index(i, TILE), 0, 0)),
            const((DQ, D)), const((1, D)), const((D, 2 * LANES)), const((D, LANES)), const((256, 256)),
        ],
        out_specs=out_specs,
        out_shape=out_shape,
        scratch_shapes=[pltpu.VMEM((NE, LANES), F32)],
        compiler_params=_vmem_limit(56),
        name="post_attn",
    )(o, xp, xs, mod, w_o, g_ffn, wr_hl, wr_hi, _tri())


def _pool_kernel(block_kinds, n_alias, y0_ref, y1_ref, xin_ref, wcin_ref, modp_ref, mod_ref, g_ref, pw_ref,
                 ps_ref, gf_ref, wr_hl_ref, wr_hi_ref, tri_ref, band_ref, inv_cnt_ref, cnt_in_ref, *refs):
    (x1_ref, h_ref, e_ref, rank_ref, wcol_ref, cnt_out_ref, cnt_ref, hhi_ref, hlo_ref) = refs[n_alias:]
    i = pl.program_id(0)

    @pl.when(i == 0)
    def _():
        cnt_ref[...] = cnt_in_ref[...]

    x = _combined(y0_ref, y1_ref, xin_ref, wcin_ref, modp_ref)
    h = _modulate(x, g_ref[...] * (1.0 + mod_ref[0, 1:2, :]), mod_ref[0, 0:1, :])
    @pl.when(i == 0)
    def _():
        zeros = jnp.zeros((PB_HALO, D), BF16)
        for ref in (hhi_ref, hlo_ref):
            ref[0:PB_HALO, :] = zeros
            ref[PB_HALO + PTILE:PB_HALO + PTILE + PB_HALO, :] = zeros

    h_hi = h.astype(BF16)
    hhi_ref[PB_HALO:PB_HALO + PTILE, :] = h_hi
    hlo_ref[PB_HALO:PB_HALO + PTILE, :] = (h - h_hi.astype(F32)).astype(BF16)

    outs = []
    for gi in range(len(POOL_WINDOWS)):
        cols = slice(gi * PGD, (gi + 1) * PGD)
        sums = []
        for b in range(PTILE // PB):
            bm = band_ref[block_kinds[b], gi]
            sums.append(jnp.dot(bm, hhi_ref[b * PB:b * PB + PBW, cols], preferred_element_type=F32)
                        + jnp.dot(bm, hlo_ref[b * PB:b * PB + PBW, cols], preferred_element_type=F32))
        diff = jnp.concatenate(sums, axis=0) * inv_cnt_ref[:, gi:gi + 1] - h[:, cols]
        outs.append(jnp.dot(diff.astype(BF16), pw_ref[gi], preferred_element_type=F32))
    m = jnp.concatenate(outs, axis=1) * ps_ref[...]
    x1 = x + mod_ref[0, 2:3, :] * m
    x1_ref[...] = x1
    hf = _modulate(x1, gf_ref[...] * (1.0 + mod_ref[0, 4:5, :]), mod_ref[0, 3:4, :])
    h_ref[...] = _pack_rows(hf)
    _route(hf, wr_hl_ref, wr_hi_ref, tri_ref, cnt_ref, e_ref, rank_ref, wcol_ref, cnt_out_ref)


def _pool_constants(seq):
    r = np.arange(PB)[:, None]
    d = np.arange(PBW)[None, :] - PB_HALO - r
    found, kinds = {}, []
    for b in range(PTILE // PB):
        tm = (b * PB + r) % seq
        inseq = (tm + d >= 0) & (tm + d < seq)
        mats = np.stack([(inseq & (d >= -(w // 2)) & (d <= w - w // 2 - 1)).astype(np.float32)
                         for w in POOL_WINDOWS])
        kinds.append(found.setdefault(mats.tobytes(), (len(found), mats))[0])
    bands = np.stack([m for _, m in sorted(found.values(), key=lambda kv: kv[0])])
    tm = np.arange(PTILE) % seq
    inv_cnt = np.zeros((PTILE, LANES), np.float32)
    for g, w in enumerate(POOL_WINDOWS):
        left, right = w // 2, w - w // 2 - 1
        inv_cnt[:, g] = 1.0 / (np.minimum(tm + right + 1, seq) - np.maximum(tm - left, 0)).astype(np.float32)
    return jnp.asarray(bands, BF16), tuple(kinds), inv_cnt


def _pool(group, y2g, x, wcol, mod_prev, mod, g_mix, pool_w, pool_scale, g_ffn, wr_hl, wr_hi, cnt_in, prev):
    const = lambda shape: pl.BlockSpec(shape, lambda i: (0,) * len(shape))
    tiles = (T_CTX if group == 0 else T_LAT) // PTILE
    first = 0 if group == 0 else T_CTX // PTILE
    out_specs, out_shape = _route_outs(PTILE, first)
    aliases = () if prev is None else tuple(prev)
    bands, block_kinds, inv_cnt = _pool_constants(SEQ if group == 0 else DEC_SEQ)
    n_fixed = 16
    return pl.pallas_call(
        functools.partial(_pool_kernel, block_kinds, len(aliases)),
        grid=(tiles,),
        in_specs=[
            pl.BlockSpec((PTILE, HALF), lambda i: (i, 0)),
            pl.BlockSpec((PTILE, HALF), lambda i: (tiles + i, 0)),
            pl.BlockSpec((PTILE, D), lambda i: (first + i, 0)),
            pl.BlockSpec((TOP_K, PTILE), lambda i: (0, first + i)),
            pl.BlockSpec((1, N_MOD, D), lambda i: (_cond_index(first + i, PTILE), 0, 0)),
            pl.BlockSpec((1, N_MOD, D), lambda i: (_cond_index(first + i, PTILE), 0, 0)),
            const((1, D)), const((len(POOL_WINDOWS), PGD, PGD)), const((1, D)), const((1, D)),
            const((D, 2 * LANES)), const((D, LANES)), const((256, 256)), const(bands.shape),
            const((PTILE, LANES)), const((NE, LANES)),
        ] + [pl.BlockSpec(memory_space=pl.ANY)] * len(aliases),
        out_specs=out_specs,
        out_shape=out_shape,
        input_output_aliases={n_fixed + k: k for k in range(len(aliases))},
        scratch_shapes=[pltpu.VMEM((NE, LANES), F32), pltpu.VMEM((PTILE + 2 * PB_HALO, D), BF16),
                        pltpu.VMEM((PTILE + 2 * PB_HALO, D), BF16)],
        compiler_params=_vmem_limit(56),
        name="pool_mixer",
    )(y2g, y2g, x, wcol, mod_prev, mod, g_mix, pool_w, pool_scale, g_ffn, wr_hl, wr_hi, _tri(), bands, inv_cnt,
      cnt_in, *aliases)


def _plan_kernel(e_ref, rank_ref, cnt_ref, pos_ref, te_ref, used_ref):
    counts = cnt_ref[:, 0:1]
    tiles_col = jnp.floor((counts + (TM - 1)) * (1.0 / TM))
    sub = lax.broadcasted_iota(I32, (NE, NE), 0)
    lane = lax.broadcasted_iota(I32, (NE, NE), 1)
    tiles_row = jnp.sum(jnp.where(sub == lane, tiles_col, 0.0), axis=0, keepdims=True)
    first_col = jnp.sum(jnp.where(lane < sub, tiles_row, 0.0), axis=1, keepdims=True)
    end_col = first_col + tiles_col
    used = jnp.sum(tiles_col, axis=0, keepdims=True)
    base_col = first_col * TM
    sub_e = lax.broadcasted_iota(I32, (NE, T), 0)
    for k in range(TOP_K):
        hit = sub_e == e_ref[k:k + 1, :]
        base = jnp.sum(jnp.where(hit, base_col, 0.0), axis=0, keepdims=True)
        pos_ref[k:k + 1, :] = base.astype(I32) + rank_ref[k:k + 1, :]
    tile = jnp.minimum(lax.broadcasted_iota(I32, (NE, LANES), 1).astype(F32), used - 1.0)
    te_ref[...] = jnp.sum(jnp.where(tile >= end_col, 1.0, 0.0), axis=0, keepdims=True).astype(I32)
    used_ref[...] = jnp.broadcast_to(used, (1, LANES)).astype(I32)


def _plan(e, rank, cnt):
    assert N_ROW_TILES <= LANES
    pos, te, used = pl.pallas_call(
        _plan_kernel,
        out_shape=[jax.ShapeDtypeStruct((TOP_K, T), I32), jax.ShapeDtypeStruct((1, LANES), I32),
                   jax.ShapeDtypeStruct((1, LANES), I32)],
        name="moe_plan",
    )(e, rank, cnt)
    return pos, te[0, :N_ROW_TILES], used[0, :1]


def _sc_mesh():
    return plsc.VectorSubcoreMesh(core_axis_name="core", subcore_axis_name="subcore")


def _sc_worker():
    return lax.axis_index("core") * SC_SUBCORES + lax.axis_index("subcore")


def _sc_dispatch(h, pos):
    per = T // SC_WORKERS
    nwin = per // SC_WIN

    @functools.partial(
        pl.kernel, out_type=jax.ShapeDtypeStruct((NPAD, HALF), U32), mesh=_sc_mesh(),
        scratch_types=[pltpu.VMEM((2, SC_WIN, HALF), U32), pltpu.VMEM((2, SC_WIN), I32),
                       pltpu.VMEM((2, SC_WIN), I32), pltpu.SemaphoreType.DMA((2,)),
                       pltpu.SemaphoreType.DMA((2,))])
    def run(h_hbm, pos_hbm, xs_hbm, buf, idx0, idx1, sem_in, sem_out):
        first = _sc_worker() * per

        def loads(j, s):
            base = first + j * SC_WIN
            return (pltpu.make_async_copy(h_hbm.at[pl.ds(base, SC_WIN)], buf.at[s], sem_in.at[s]),
                    pltpu.make_async_copy(pos_hbm.at[pl.ds(base, SC_WIN)], idx0.at[s], sem_in.at[s]),
                    pltpu.make_async_copy(pos_hbm.at[pl.ds(T + base, SC_WIN)], idx1.at[s], sem_in.at[s]))

        def stores(s):
            return (pltpu.make_async_copy(buf.at[s], xs_hbm.at[idx0.at[s]], sem_out.at[s]),
                    pltpu.make_async_copy(buf.at[s], xs_hbm.at[idx1.at[s]], sem_out.at[s]))

        for c in loads(0, 0):
            c.start()

        @pl.loop(0, nwin // 2)
        def _(jj):
            for s in range(2):
                j = jj * 2 + s
                for c in loads(j, s):
                    c.wait()
                for c in stores(s):
                    c.start()

                @pl.when(j >= 1)
                def _():
                    for c in stores(1 - s):
                        c.wait()

                @pl.when(j + 1 < nwin)
                def _():
                    for c in loads(j + 1, 1 - s):
                        c.start()

        for c in stores((nwin - 1) % 2):
            c.wait()

    return run(h, pos.reshape(TOP_K * T))


def _sc_gather(ys, pos, group):
    t0, n_g = (0, T_CTX) if group == 0 else (T_CTX, T_LAT)
    idx_all = jnp.concatenate([pos[k, t0:t0 + n_g] for k in range(TOP_K)])
    n = TOP_K * n_g
    per = n // SC_WORKERS
    nwin = per // SC_WIN

    @functools.partial(
        pl.kernel, out_type=jax.ShapeDtypeStruct((n, HALF), U32), mesh=_sc_mesh(),
        scratch_types=[pltpu.VMEM((2, SC_WIN, HALF), U32), pltpu.VMEM((per,), I32),
                       pltpu.SemaphoreType.DMA((2,)), pltpu.SemaphoreType.DMA((2,))])
    def run(ys_hbm, pos_hbm, out_hbm, buf, idx, sem_in, sem_out):
        first = _sc_worker() * per
        pltpu.sync_copy(pos_hbm.at[pl.ds(first, per)], idx)

        def load(j, s):
            return pltpu.make_async_copy(ys_hbm.at[idx.at[pl.ds(j * SC_WIN, SC_WIN)]], buf.at[s], sem_in.at[s])

        def store(j, s):
            return pltpu.make_async_copy(buf.at[s], out_hbm.at[pl.ds(first + j * SC_WIN, SC_WIN)], sem_out.at[s])

        load(0, 0).start()

        @pl.loop(0, nwin // 2)
        def _(jj):
            for s in range(2):
                j = jj * 2 + s
                load(j, s).wait()
                store(j, s).start()

                @pl.when(j >= 1)
                def _():
                    store(j - 1, 1 - s).wait()

                @pl.when(j + 1 < nwin)
                def _():
                    load(j + 1, 1 - s).start()

        store(nwin - 1, (nwin - 1) % 2).wait()

    return run(ys, idx_all)


def _moe_kernel(te_ref, used_ref, x_hbm, wg_ref, wu_ref, wd_ref, y_ref, xbuf, sem):
    del te_ref
    i = pl.program_id(0)
    used = used_ref[0]

    def tile_copy(t):
        slot = lax.rem(t, X_SLOTS)
        return pltpu.make_async_copy(x_hbm.at[pl.ds(t * TM, TM)], xbuf.at[slot], sem.at[slot])

    @pl.when(i == 0)
    def _():
        for t in range(X_SLOTS - 1):
            @pl.when(t < used)
            def _():
                tile_copy(t).start()

    @pl.when(i + (X_SLOTS - 1) < used)
    def _():
        tile_copy(i + (X_SLOTS - 1)).start()

    @pl.when(i < used)
    def _():
        tile_copy(i).wait()
        hi, lo = _unpack_rows(xbuf[lax.rem(i, X_SLOTS)])
        x = jnp.concatenate([hi, lo], axis=1).astype(BF16)
        a = jnp.dot(x, wg_ref[0, 0], preferred_element_type=F32)
        b = jnp.dot(x, wu_ref[0, 0], preferred_element_type=F32)
        hid = a * (1.0 / (1.0 + jnp.exp(-a))) * b
        y_ref[...] = _pack_rows(jnp.dot(hid.astype(BF16), wd_ref[0, 0], preferred_element_type=F32))


def _experts(xs, tile_expert, used, layer, w_gate, w_up, w_down):
    row = lambda i, te, used: (jnp.minimum(i, used[0] - 1), 0)
    weights = lambda i, te, used: (layer, te[i], 0, 0)
    grid_spec = pltpu.PrefetchScalarGridSpec(
        num_scalar_prefetch=2,
        grid=(N_ROW_TILES,),
        in_specs=[
            pl.BlockSpec(memory_space=pl.ANY),
            pl.BlockSpec((1, 1, D, DE), weights),
            pl.BlockSpec((1, 1, D, DE), weights),
            pl.BlockSpec((1, 1, DE, D), weights),
        ],
        out_specs=pl.BlockSpec((TM, HALF), row),
        scratch_shapes=[pltpu.VMEM((X_SLOTS, TM, HALF), U32), pltpu.SemaphoreType.DMA((X_SLOTS,))],
    )
    return pl.pallas_call(
        _moe_kernel,
        grid_spec=grid_spec,
        out_shape=jax.ShapeDtypeStruct((NPAD, HALF), U32),
        compiler_params=pltpu.CompilerParams(dimension_semantics=("arbitrary",),
                                             vmem_limit_bytes=40 * 1024 * 1024),
        name="moe_experts",
    )(tile_expert, used, xs, w_gate, w_up, w_down)


def _combined(y0_ref, y1_ref, x_ref, wcol_ref, mod_ref):
    sub = lax.broadcasted_iota(I32, (LANES, wcol_ref.shape[1]), 0)
    wt = jnp.where(sub == 0, wcol_ref[0:1, :], jnp.where(sub == 1, wcol_ref[1:2, :], 0.0)).T
    w0 = wt[:, 0:1]
    w1 = wt[:, 1:2]
    hi0, lo0 = _unpack_rows(y0_ref[...])
    hi1, lo1 = _unpack_rows(y1_ref[...])
    y = jnp.concatenate([w0 * hi0 + w1 * hi1, w0 * lo0 + w1 * lo1], axis=1)
    return x_ref[...] + mod_ref[0, 5:6, :] * y


def _combine_kernel(y0_ref, y1_ref, x_ref, wcol_ref, mod_ref, o_ref):
    o_ref[...] = _combined(y0_ref, y1_ref, x_ref, wcol_ref, mod_ref)


def _combine(group, y2g, x, wcol, mod):
    n_g = T_CTX if group == 0 else T_LAT
    tiles = n_g // CT
    first = 0 if group == 0 else T_CTX // CT
    return pl.pallas_call(
        _combine_kernel,
        grid=(tiles,),
        in_specs=[
            pl.BlockSpec((CT, HALF), lambda i: (i, 0)),
            pl.BlockSpec((CT, HALF), lambda i: (tiles + i, 0)),
            pl.BlockSpec((CT, D), lambda i: (first + i, 0)),
            pl.BlockSpec((TOP_K, CT), lambda i: (0, first + i)),
            pl.BlockSpec((1, N_MOD, D), lambda i: (_cond_index(first + i, CT), 0, 0)),
        ],
        out_specs=pl.BlockSpec((CT, D), lambda i: (i, 0)),
        out_shape=jax.ShapeDtypeStruct((n_g, D), F32),
        name="moe_combine",
    )(y2g, y2g, x, wcol, mod)


def _moe(h, e, rank, cnt, layer, w_gate, w_up, w_down):
    pos, tile_expert, used = _plan(e, rank, cnt)
    xs = _sc_dispatch(h, pos)
    ys = _experts(xs, tile_expert, used, layer, w_gate, w_up, w_down)
    return _sc_gather(ys, pos, 0), _sc_gather(ys, pos, 1)


def kernel(x_prompt, x_sample, c, cache_k, cache_v, c_ctx, norm_mix_g, norm_ffn_g, ada_w, ada_b, attn_w_qkv, attn_q_norm, attn_k_norm, attn_w_o, pool_w, pool_scale, moe_w_group, moe_w_expert, moe_w_gate, moe_w_up, moe_w_down):
    xp = x_prompt.reshape(T_CTX, D)
    xs = x_sample.reshape(T_LAT, D)
    cond = jnp.concatenate([c_ctx[None, :], c, jnp.zeros((N_COND - 1 - DEC_BATCH, D), F32)], axis=0)
    mod = [_ada(cond, ada_w, ada_b, layer) for layer in range(ada_w.shape[0])]

    qt, k2, vt, new_k, new_v = _qkv(xp, xs, mod[0], norm_mix_g[0:1], attn_w_qkv[0],
                                    attn_q_norm[0:1], attn_k_norm[0:1])
    o, w_gate, w_up, w_down = _attention(qt, k2, vt, cache_k[:, 0].reshape(DEC_BATCH, PAST, DKV),
                                         cache_v[:, 0].reshape(DEC_BATCH, PAST, DKV),
                                         moe_w_gate, moe_w_up, moe_w_down)
    wr_hl, wr_hi = _router_weights(moe_w_group[0], moe_w_expert[0])
    x1, h, e, rank, wcol, cnt = _post(o, xp, xs, mod[0], attn_w_o[0].astype(BF16), norm_ffn_g[0:1],
                                      wr_hl, wr_hi)
    y2 = _moe(h, e, rank, cnt, 0, w_gate, w_up, w_down)

    wr_hl, wr_hi = _router_weights(moe_w_group[1], moe_w_expert[1])
    pool_args = (x1, wcol, mod[0], mod[1], norm_mix_g[1:2], pool_w[0].astype(BF16), pool_scale[0:1],
                 norm_ffn_g[1:2], wr_hl, wr_hi)
    outs = _pool(0, y2[0], *pool_args, jnp.zeros((NE, LANES), F32), None)
    x3, h, e, rank, wcol, cnt = _pool(1, y2[1], *pool_args, outs[5], outs[:5])
    y2 = _moe(h, e, rank, cnt, 1, w_gate, w_up, w_down)
    y_prompt = _combine(0, y2[0], x3, wcol, mod[1])
    y_sample = _combine(1, y2[1], x3, wcol, mod[1])

    return (y_prompt.reshape(BATCH, SEQ, D), y_sample.reshape(DEC_BATCH, DEC_SEQ, D),
            new_k.reshape(BATCH, 1, SEQ, N_KV, HD), new_v.reshape(BATCH, 1, SEQ, N_KV, HD))
```
